```python
import jax
import jax.numpy as jnp
from jax import lax
import numpy as np

D_MODEL = 2048
BATCH = 4
SEQ = 8192
DEPTH = 1

EPS = 1e-6
NEG = -1e30
FORCE_BONUS = 1e4
NSA_HEAD_DIM = 128
NSA_WIDTH = D_MODEL // 2
NSA_HEADS = NSA_WIDTH // NSA_HEAD_DIM
NSA_REP = 4
NSA_KV_GROUPS = NSA_HEADS // NSA_REP
NSA_KV_WIDTH = NSA_KV_GROUPS * NSA_HEAD_DIM
CMP_BLOCK = 32
CMP_STRIDE = 16
SEL_BLOCK = 64
SEL_TOPK = 16
WINDOW = 512
Q_BLOCK = 128
N_BRANCH = 3
MLSTM_HEAD_DIM = 256
MLSTM_WIDTH = D_MODEL - NSA_WIDTH
MLSTM_HEADS = MLSTM_WIDTH // MLSTM_HEAD_DIM
MLSTM_CHUNK = 64
CONV_WIDTH = 4
D_FF = 4 * D_MODEL
IN_SIZES = (NSA_WIDTH,) + (NSA_KV_WIDTH,) * 6 + (NSA_HEADS * N_BRANCH, 2 * MLSTM_WIDTH, MLSTM_WIDTH, MLSTM_WIDTH, MLSTM_HEADS, MLSTM_HEADS)
D_IN = sum(IN_SIZES)
SPLIT_IDX = tuple(int(v) for v in np.cumsum(IN_SIZES)[:-1])

kernel_name = 'hymba_nsa_mlstm_hybrid'


def _rms_norm(x, g):
    xf = x.astype(jnp.float32)
    y = xf * lax.rsqrt(jnp.mean(xf * xf, axis=-1, keepdims=True) + EPS)
    return (y * g.astype(jnp.float32)).astype(x.dtype)


def _alibi_slopes(n):
    return jnp.exp2(-8.0 * jnp.arange(1, n + 1, dtype=jnp.float32) / n)


def _causal_conv_silu(u, w, b):
    S = u.shape[1]
    up = jnp.pad(u, ((0, 0), (CONV_WIDTH - 1, 0), (0, 0)))
    y = b + up[:, 0:S] * w[0]
    for i in range(1, CONV_WIDTH):
        y = y + up[:, i:i + S] * w[i]
    return jax.nn.silu(y)


def _nsa_mixer(q, k_cmp, v_cmp, k_slc, v_slc, k_win, v_win, gate_pre,
               w_cmp_k1, w_cmp_k2, pos_cmp_k, w_cmp_v1, w_cmp_v2, pos_cmp_v):
    B, S = q.shape[0], q.shape[1]
    G, R, Dh = NSA_KV_GROUPS, NSA_REP, NSA_HEAD_DIM
    f32 = jnp.float32
    n_sub = S // CMP_STRIDE
    ratio = CMP_BLOCK // CMP_STRIDE
    n_cmp = n_sub - ratio + 1
    n_sel = S // SEL_BLOCK
    n_qb = S // Q_BLOCK
    topk = min(SEL_TOPK, n_sel)
    scale = Dh ** -0.5
    slopes = _alibi_slopes(NSA_HEADS).reshape(G, R)[None, :, :, None, None]

    def heads(t):
        return t.reshape(B, S, G, Dh).transpose(0, 2, 1, 3).astype(f32)

    def compress(t, w1, w2, pos):
        sub = heads(t).reshape(B, G, n_sub, CMP_STRIDE, Dh)
        blocks = jnp.concatenate([sub[:, :, r:r + n_cmp] for r in range(ratio)], axis=3)
        blocks = (blocks + pos.astype(f32)).reshape(B, G, n_cmp, CMP_BLOCK * Dh)
        return jax.nn.silu(blocks @ w1.astype(f32)) @ w2.astype(f32)

    kc = compress(k_cmp, w_cmp_k1, w_cmp_k2, pos_cmp_k)
    vc = compress(v_cmp, w_cmp_v1, w_cmp_v2, pos_cmp_v)
    cmp_start = jnp.arange(n_cmp) * CMP_STRIDE
    cmp_end = cmp_start + CMP_BLOCK - 1
    sel_start = jnp.arange(n_sel) * SEL_BLOCK
    overlap = ((cmp_start[:, None] < sel_start[None, :] + SEL_BLOCK) & (cmp_end[:, None] >= sel_start[None, :])).astype(f32)

    ks_blk = heads(k_slc).reshape(B, G, n_sel, SEL_BLOCK, Dh)
    vs_blk = heads(v_slc).reshape(B, G, n_sel, SEL_BLOCK, Dh)
    kw_pad = jnp.pad(heads(k_win), ((0, 0), (0, 0), (WINDOW, 0), (0, 0)))
    vw_pad = jnp.pad(heads(v_win), ((0, 0), (0, 0), (WINDOW, 0), (0, 0)))
    bi = jnp.arange(B)[:, None, None, None]
    gi = jnp.arange(G)[None, :, None, None]
    sel_ids = jnp.arange(n_sel)
    sel_off = jnp.arange(SEL_BLOCK)
    win_off = jnp.arange(WINDOW + Q_BLOCK)

    q_blocks = q.reshape(B, n_qb, Q_BLOCK, G, R, Dh).transpose(1, 0, 3, 4, 2, 5)
    g_blocks = gate_pre.reshape(B, n_qb, Q_BLOCK, G, R, N_BRANCH).transpose(1, 0, 3, 4, 2, 5)

    def block_fn(args):
        qb, gb, c = args
        qb = qb.astype(f32) * scale
        t0 = c * Q_BLOCK
        t = t0 + jnp.arange(Q_BLOCK)
        s_c = jnp.einsum('bgrqd,bgnd->bgrqn', qb, kc)
        dist_c = t[:, None] - cmp_end[None, :]
        valid_c = dist_c >= 0
        s_c = jnp.where(valid_c, s_c - slopes * dist_c.astype(f32), NEG)
        p_c = jax.nn.softmax(s_c, axis=-1) * valid_c
        o_c = jnp.einsum('bgrqn,bgnd->bgrqd', p_c, vc)
        imp = jnp.einsum('bgrqn,nj->bgqj', p_c, overlap)
        cur = t // SEL_BLOCK
        forced = (sel_ids[None, :] == 0) | (sel_ids[None, :] == cur[:, None]) | (sel_ids[None, :] == cur[:, None] - 1)
        causal_blk = sel_start[None, :] <= t[:, None]
        imp = jnp.where(causal_blk, jnp.where(forced, imp + FORCE_BONUS, imp), NEG)
        _, idx = lax.top_k(imp, topk)
        ks_g = ks_blk[bi, gi, idx].reshape(B, G, Q_BLOCK, topk * SEL_BLOCK, Dh)
        vs_g = vs_blk[bi, gi, idx].reshape(B, G, Q_BLOCK, topk * SEL_BLOCK, Dh)
        pos_s = (idx[..., None] * SEL_BLOCK + sel_off).reshape(B, G, Q_BLOCK, topk * SEL_BLOCK)
        dist_s = (t[None, None, :, None] - pos_s)[:, :, None]
        s_s = jnp.einsum('bgrqd,bgqkd->bgrqk', qb, ks_g)
        s_s = jnp.where(dist_s >= 0, s_s - slopes * dist_s.astype(f32), NEG)
        o_s = jnp.einsum('bgrqk,bgqkd->bgrqd', jax.nn.softmax(s_s, axis=-1), vs_g)
        kw = lax.dynamic_slice_in_dim(kw_pad, t0, WINDOW + Q_BLOCK, axis=2)
        vw = lax.dynamic_slice_in_dim(vw_pad, t0, WINDOW + Q_BLOCK, axis=2)
        pos_w = t0 - WINDOW + win_off
        dist_w = t[:, None] - pos_w[None, :]
        valid_w = (dist_w >= 0) & (dist_w < WINDOW) & (pos_w[None, :] >= 0)
        s_w = jnp.einsum('bgrqd,bgkd->bgrqk', qb, kw)
        s_w = jnp.where(valid_w, s_w - slopes * dist_w.astype(f32), NEG)
        o_w = jnp.einsum('bgrqk,bgkd->bgrqd', jax.nn.softmax(s_w, axis=-1), vw)
        g = jax.nn.sigmoid(gb.astype(f32))
        return g[..., 0:1] * o_c + g[..., 1:2] * o_s + g[..., 2:3] * o_w

    out = lax.map(block_fn, (q_blocks, g_blocks, jnp.arange(n_qb)))
    return out.transpose(1, 0, 4, 2, 3, 5).reshape(B, S, NSA_WIDTH)


def _mlstm_mixer(q, k, v, o_pre, i_pre, f_pre, norm_g):
    B, S = q.shape[0], q.shape[1]
    NH, DH, L = MLSTM_HEADS, MLSTM_HEAD_DIM, MLSTM_CHUNK
    nc = S // L
    f32 = jnp.float32

    def chunks(t):
        return t.astype(f32).reshape(B, nc, L, NH, DH).transpose(1, 0, 3, 2, 4)

    def gchunks(t):
        return t.astype(f32).reshape(B, nc, L, NH).transpose(1, 0, 3, 2)

    qc, kc, vc = chunks(q), chunks(k) * DH ** -0.5, chunks(v)
    li_c = gchunks(i_pre)
    lf_c = jax.nn.log_sigmoid(gchunks(f_pre))
    causal = jnp.tril(jnp.ones((L, L), dtype=bool))

    def step(carry, xs):
        C, n, m = carry
        qj, kj, vj, li, lf = xs
        b = jnp.cumsum(lf, axis=-1)
        D = jnp.where(causal, b[..., :, None] - b[..., None, :] + li[..., None, :], NEG)
        a = b + m[..., None]
        m_j = jnp.maximum(a, jnp.max(D, axis=-1))
        w_intra = jnp.exp(D - m_j[..., None])
        w_inter = jnp.exp(a - m_j)
        sc = jnp.einsum('bhjd,bhsd->bhjs', qj, kj) * w_intra
        num = w_inter[..., None] * jnp.einsum('bhjd,bhde->bhje', qj, C) + jnp.einsum('bhjs,bhse->bhje', sc, vj)
        den = w_inter * jnp.einsum('bhjd,bhd->bhj', qj, n) + jnp.sum(sc, axis=-1)
        h = num / jnp.maximum(jnp.abs(den), jnp.exp(-m_j))[..., None]
        g = b[..., -1]
        lw = g[..., None] - b + li
        m_new = jnp.maximum(g + m, jnp.max(lw, axis=-1))
        w_s = jnp.exp(lw - m_new[..., None])
        decay = jnp.exp(g + m - m_new)
        kw = w_s[..., None] * kj
        C_new = decay[..., None, None] * C + jnp.einsum('bhsd,bhse->bhde', kw, vj)
        n_new = decay[..., None] * n + jnp.sum(kw, axis=2)
        return (C_new, n_new, m_new), h

    init = (jnp.zeros((B, NH, DH, DH), f32), jnp.zeros((B, NH, DH), f32), jnp.zeros((B, NH), f32))
    _, hs = lax.scan(step, init, (qc, kc, vc, li_c, lf_c))
    h = hs.transpose(1, 0, 3, 2, 4).reshape(B, S, NH, DH)
    h = h * lax.rsqrt(jnp.mean(h * h, axis=-1, keepdims=True) + EPS) * norm_g.astype(f32).reshape(NH, DH)
    return jax.nn.sigmoid(o_pre.astype(f32)) * h.reshape(B, S, MLSTM_WIDTH)


def setup_inputs(seed: int = 0) -> dict:
    key = jax.random.key(seed)
    ks = jax.random.split(key, 20)
    f32 = jnp.float32
    L = DEPTH

    def nrm(k, shape, s):
        return jax.random.normal(k, shape, f32) * s

    cmp_in = CMP_BLOCK * NSA_HEAD_DIM
    return {
        'x': nrm(ks[0], (BATCH, SEQ, D_MODEL), 1.0),
        'norm_mix_g': 1.0 + nrm(ks[1], (L, D_MODEL), 0.02),
        'w_in': nrm(ks[2], (L, D_MODEL, D_IN), D_MODEL ** -0.5),
        'w_cmp_k1': nrm(ks[3], (L, cmp_in, NSA_HEAD_DIM), cmp_in ** -0.5),
        'w_cmp_k2': nrm(ks[4], (L, NSA_HEAD_DIM, NSA_HEAD_DIM), NSA_HEAD_DIM ** -0.5),
        'pos_cmp_k': nrm(ks[5], (L, CMP_BLOCK, NSA_HEAD_DIM), 0.02),
        'w_cmp_v1': nrm(ks[6], (L, cmp_in, NSA_HEAD_DIM), cmp_in ** -0.5),
        'w_cmp_v2': nrm(ks[7], (L, NSA_HEAD_DIM, NSA_HEAD_DIM), NSA_HEAD_DIM ** -0.5),
        'pos_cmp_v': nrm(ks[8], (L, CMP_BLOCK, NSA_HEAD_DIM), 0.02),
        'conv_w': nrm(ks[9], (L, CONV_WIDTH, 2 * MLSTM_WIDTH), CONV_WIDTH ** -0.5),
        'conv_b': nrm(ks[10], (L, 2 * MLSTM_WIDTH), 0.01),
        'b_igate': nrm(ks[11], (L, MLSTM_HEADS), 0.1),
        'b_fgate': jnp.linspace(3.0, 6.0, MLSTM_HEADS, dtype=f32)[None, :] + nrm(ks[12], (L, MLSTM_HEADS), 0.1),
        'mlstm_norm_g': 1.0 + nrm(ks[13], (L, MLSTM_WIDTH), 0.02),
        'w_out': nrm(ks[14], (L, D_MODEL, D_MODEL), D_MODEL ** -0.5),
        'norm_mlp_g': 1.0 + nrm(ks[15], (L, D_MODEL), 0.02),
        'w_mlp_in': nrm(ks[16], (L, D_MODEL, D_FF), D_MODEL ** -0.5),
        'w_mlp_out': nrm(ks[17], (L, D_FF, D_MODEL), D_FF ** -0.5),
        'norm_f_g': 1.0 + nrm(ks[18], (D_MODEL,), 0.02),
    }


def reference(x, norm_mix_g, w_in, w_cmp_k1, w_cmp_k2, pos_cmp_k, w_cmp_v1, w_cmp_v2, pos_cmp_v,
              conv_w, conv_b, b_igate, b_fgate, mlstm_norm_g, w_out, norm_mlp_g, w_mlp_in, w_mlp_out, norm_f_g):
    for l in range(DEPTH):
        h = _rms_norm(x, norm_mix_g[l])
        (q_a, k_c, v_c, k_s, v_s, k_w, v_w, g_a,
         qk_m, v_m, o_m, i_m, f_m) = jnp.split(h @ w_in[l], SPLIT_IDX, axis=-1)
        y_a = _nsa_mixer(q_a, k_c, v_c, k_s, v_s, k_w, v_w, g_a,
                         w_cmp_k1[l], w_cmp_k2[l], pos_cmp_k[l], w_cmp_v1[l], w_cmp_v2[l], pos_cmp_v[l])
        q_m, k_m = jnp.split(_causal_conv_silu(qk_m, conv_w[l], conv_b[l]), 2, axis=-1)
        y_m = _mlstm_mixer(q_m, k_m, v_m, o_m, i_m + b_igate[l], f_m + b_fgate[l], mlstm_norm_g[l])
        mixed = jnp.concatenate([y_a.astype(x.dtype), y_m.astype(x.dtype)], axis=-1)
        x = x + mixed @ w_out[l]
        h = _rms_norm(x, norm_mlp_g[l])
        x = x + jnp.square(jax.nn.relu(h @ w_mlp_in[l])) @ w_mlp_out[l]
    return _rms_norm(x, norm_f_g)
```

```python
import functools

import numpy as np
import jax
import jax.numpy as jnp
from jax import lax
from jax.experimental import pallas as pl
from jax.experimental.pallas import tpu as pltpu

F32 = jnp.float32
BF16 = jnp.bfloat16

EPS = 1e-6
NEG = -1e30
FORCE_BONUS = 1e4
PICKED = -3e38
MASK_BIG = 1e30

D_MODEL = 2048
NSA_HEAD_DIM = 128
NSA_WIDTH = D_MODEL // 2
NSA_HEADS = NSA_WIDTH // NSA_HEAD_DIM
NSA_REP = 4
NSA_KV_GROUPS = NSA_HEADS // NSA_REP
NSA_KV_WIDTH = NSA_KV_GROUPS * NSA_HEAD_DIM
CMP_BLOCK = 32
CMP_STRIDE = 16
SEL_BLOCK = 64
SEL_TOPK = 16
WINDOW = 512
Q_BLOCK = 128
N_BRANCH = 3
MLSTM_HEAD_DIM = 256
MLSTM_WIDTH = D_MODEL - NSA_WIDTH
MLSTM_HEADS = MLSTM_WIDTH // MLSTM_HEAD_DIM
CONV_WIDTH = 4
D_FF = 4 * D_MODEL

LANE = 128
SEL_LANES = 128
VMEM_LIMIT = 56 * 1024 * 1024

SEG_A = NSA_WIDTH + 6 * NSA_KV_WIDTH
B_QK, B_V, B_O = 0, 2 * MLSTM_WIDTH, 3 * MLSTM_WIDTH
B_GATE = 4 * MLSTM_WIDTH
B_IF = B_GATE + NSA_KV_GROUPS * LANE
SEG_B = 4608


def _dot(a, b):
    return jnp.dot(a, b, preferred_element_type=F32)


def _dot_nt(a, b):
    return lax.dot_general(a, b, (((1,), (1,)), ((), ())), preferred_element_type=F32)


def _sigmoid(x):
    return 1.0 / (1.0 + jnp.exp(-x))


def _inproj_kernel(x_ref, g_ref, w_ref, cs_ref, o_ref, h_ref):
    @pl.when(pl.program_id(1) == 0)
    def _():
        x = x_ref[...]
        r = lax.rsqrt(jnp.mean(x * x, axis=-1, keepdims=True) + EPS)
        h_ref[...] = (x * r * g_ref[...]).astype(BF16)

    o_ref[...] = (_dot(h_ref[...], w_ref[...]) * cs_ref[...]).astype(o_ref.dtype)


def _inproj(x2, g, w, cs, out_dtype, tm, tn):
    n, d = x2.shape
    nc = w.shape[1]
    return pl.pallas_call(
        _inproj_kernel,
        grid=(n // tm, nc // tn),
        in_specs=[
            pl.BlockSpec((tm, d), lambda i, j: (i, 0)),
            pl.BlockSpec((1, d), lambda i, j: (0, 0)),
            pl.BlockSpec((d, tn), lambda i, j: (0, j)),
            pl.BlockSpec((1, tn), lambda i, j: (0, j)),
        ],
        out_specs=pl.BlockSpec((tm, tn), lambda i, j: (i, j)),
        out_shape=jax.ShapeDtypeStruct((n, nc), out_dtype),
        scratch_shapes=[pltpu.VMEM((tm, d), BF16)],
        compiler_params=pltpu.CompilerParams(
            dimension_semantics=("parallel", "arbitrary"), vmem_limit_bytes=VMEM_LIMIT),
        name="inproj",
    )(x2, g, w, cs)


def _compress_kernel(x_ref, w1_ref, w2_ref, pos_ref, o_ref):
    x = x_ref[0, 0]
    n_sub = x.shape[0]
    half = CMP_STRIDE * NSA_HEAD_DIM
    w1 = w1_ref[0]
    top = _dot(x, w1[:half])
    bot = _dot(x, w1[half:])
    bot = pltpu.roll(bot, n_sub - 1, axis=0)
    posw = _dot(pos_ref[0], w1)[0:1]
    pre = top + bot + posw
    hid = pre * _sigmoid(pre)
    o_ref[0, 0] = _dot(hid.astype(BF16), w2_ref[0]).astype(o_ref.dtype)


def _compress(xs, w1s, w2s, poss):
    b, c, n_sub, feat = xs.shape
    g = NSA_KV_GROUPS
    return pl.pallas_call(
        _compress_kernel,
        grid=(b, c),
        in_specs=[
            pl.BlockSpec((1, 1, n_sub, feat), lambda i, j: (i, j, 0, 0)),
            pl.BlockSpec((1, 2 * feat, NSA_HEAD_DIM), lambda i, j: (j // g, 0, 0)),
            pl.BlockSpec((1, NSA_HEAD_DIM, NSA_HEAD_DIM), lambda i, j: (j // g, 0, 0)),
            pl.BlockSpec((1, 8, 2 * feat), lambda i, j: (j // g, 0, 0)),
        ],
        out_specs=pl.BlockSpec((1, 1, n_sub, NSA_HEAD_DIM), lambda i, j: (i, j, 0, 0)),
        out_shape=jax.ShapeDtypeStruct((b, c, n_sub, NSA_HEAD_DIM), BF16),
        compiler_params=pltpu.CompilerParams(
            dimension_semantics=("parallel", "parallel"), vmem_limit_bytes=VMEM_LIMIT),
        name="compress",
    )(xs, w1s, w2s, poss)


def _nsa_kernel(slopes_ref, q_ref, gate_ref, kc_ref, vc_ref, ks_ref, vs_ref, kw_ref, vw_ref,
                onehot_ref, ovt_ref, o_ref, kaug_ref, qaug_ref, m_ref, l_ref, acc_ref,
                *, seq, tk):
    g = pl.program_id(1)
    qb = pl.program_id(2)
    dh = NSA_HEAD_DIM
    nq = Q_BLOCK
    n_cpad = kc_ref.shape[2]
    n_cmp = seq // CMP_STRIDE - CMP_BLOCK // CMP_STRIDE + 1
    t0 = qb * nq

    @pl.when(qb == 0)
    def _():
        kaug_ref[:, :dh] = ks_ref[...]
        kaug_ref[:, dh:] = onehot_ref[...]

    slopes = [slopes_ref[g * NSA_REP + r] for r in range(NSA_REP)]
    row_i = lax.broadcasted_iota(jnp.int32, (nq, 1), 0)
    q_all = q_ref[...]

    n_i = lax.broadcasted_iota(jnp.int32, (1, n_cpad), 1)
    dist_c = (t0 - (CMP_BLOCK - 1)) + row_i - n_i * CMP_STRIDE
    real_c = n_i < n_cmp
    valid_c = (dist_c >= 0) & real_c
    dist_cf = dist_c.astype(F32)
    kc = kc_ref[0, 0]
    vc = vc_ref[0, 0]
    o_cmp = []
    p_sum = jnp.zeros((nq, n_cpad), F32)
    for r in range(NSA_REP):
        s = _dot_nt(q_all[:, r * dh:(r + 1) * dh], kc)
        s = jnp.where(valid_c, s - slopes[r] * dist_cf, NEG)
        e = jnp.exp(s - jnp.max(s, axis=1, keepdims=True))
        denom = jnp.sum(jnp.where(real_c, e, 0.0), axis=1, keepdims=True)
        p = jnp.where(valid_c, e, 0.0) / denom
        o_cmp.append(_dot(p.astype(BF16), vc))
        p_sum = p_sum + p

    p_hi = p_sum.astype(BF16)
    p_lo = (p_sum - p_hi.astype(F32)).astype(BF16)
    ovt = ovt_ref[...]
    imp = _dot_nt(ovt, p_hi) + _dot_nt(ovt, p_lo)
    j_i = lax.broadcasted_iota(jnp.int32, (SEL_LANES, 1), 0)
    t_l = t0 + lax.broadcasted_iota(jnp.int32, (1, nq), 1)
    cur = t_l // SEL_BLOCK
    forced = (j_i == 0) | (j_i == cur) | (j_i == cur - 1)
    causal_blk = j_i * SEL_BLOCK <= t_l
    val = jnp.where(causal_blk, jnp.where(forced, imp + FORCE_BONUS, imp), NEG)
    j_f = j_i.astype(F32)
    sel_t = jnp.zeros((SEL_LANES, nq), F32)
    for _ in range(min(SEL_TOPK, seq // SEL_BLOCK)):
        mx = jnp.max(val, axis=0, keepdims=True)
        first = jnp.min(jnp.where(val == mx, j_f, float(SEL_LANES)), axis=0, keepdims=True)
        pick = j_f == first
        sel_t = jnp.where(pick, 1.0, sel_t)
        val = jnp.where(pick, PICKED, val)
    penalty = ((sel_t.T - 1.0) * MASK_BIG).astype(BF16)

    for r in range(NSA_REP):
        qaug_ref[r * nq:(r + 1) * nq, :dh] = q_all[:, r * dh:(r + 1) * dh]
        qaug_ref[r * nq:(r + 1) * nq, dh:] = penalty
    m_ref[...] = jnp.full(m_ref.shape, NEG, F32)
    l_ref[...] = jnp.zeros(l_ref.shape, F32)
    acc_ref[...] = jnp.zeros(acc_ref.shape, F32)
    lane_k = lax.broadcasted_iota(jnp.int32, (1, tk), 1)

    def sel_body(kt, carry):
        k0 = pl.multiple_of(kt * tk, tk)
        ka = kaug_ref[pl.ds(k0, tk), :]
        vv = vs_ref[pl.ds(k0, tk), :]
        s_all = _dot_nt(qaug_ref[...], ka)
        rel = (k0 - t0) + lane_k - row_i
        causal = rel <= 0
        rel_f = rel.astype(F32)
        for r in range(NSA_REP):
            rows = slice(r * nq, (r + 1) * nq)
            s = jnp.where(causal, s_all[rows] + slopes[r] * rel_f, NEG)
            m_old = m_ref[rows]
            m_new = jnp.maximum(m_old, jnp.max(s, axis=1, keepdims=True))
            alpha = jnp.exp(m_old - m_new)
            p = jnp.exp(s - m_new)
            l_ref[rows] = alpha * l_ref[rows] + jnp.sum(p, axis=1, keepdims=True)
            acc_ref[rows] = alpha * acc_ref[rows] + _dot(p.astype(BF16), vv)
            m_ref[rows] = m_new
        return carry

    n_tiles = (t0 + nq + tk - 1) // tk
    lax.fori_loop(0, n_tiles, sel_body, 0)

    wlen = WINDOW + nq
    start = pl.multiple_of(jnp.maximum(t0 - WINDOW, 0), nq)
    kwin = kw_ref[pl.ds(start, wlen), :]
    vwin = vw_ref[pl.ds(start, wlen), :]
    rel_w = (start - t0) + lax.broadcasted_iota(jnp.int32, (1, wlen), 1) - row_i
    valid_w = (rel_w <= 0) & (rel_w > -WINDOW)
    rel_wf = rel_w.astype(F32)

    gate = _sigmoid(gate_ref[...])
    for r in range(NSA_REP):
        rows = slice(r * nq, (r + 1) * nq)
        s = _dot_nt(q_all[:, r * dh:(r + 1) * dh], kwin)
        s = jnp.where(valid_w, s + slopes[r] * rel_wf, NEG)
        e = jnp.exp(s - jnp.max(s, axis=1, keepdims=True))
        o_win = _dot(e.astype(BF16), vwin) / jnp.sum(e, axis=1, keepdims=True)
        o_sel = acc_ref[rows] / l_ref[rows]
        c0 = N_BRANCH * r
        out = (gate[:, c0:c0 + 1] * o_cmp[r] + gate[:, c0 + 1:c0 + 2] * o_sel
               + gate[:, c0 + 2:c0 + 3] * o_win)
        o_ref[:, r * dh:(r + 1) * dh] = out.astype(o_ref.dtype)


def _nsa(seg_a, seg_b, kvc, onehot, ovt, slopes, batch, seq):
    n = batch * seq
    nqb = seq // Q_BLOCK
    gq = NSA_REP * NSA_HEAD_DIM
    kv0 = NSA_WIDTH // NSA_HEAD_DIM
    g_ = NSA_KV_GROUPS
    n_cpad = kvc.shape[2]
    tk = 512 if seq % 512 == 0 else Q_BLOCK

    def seq_spec(kind):
        return pl.BlockSpec((seq, NSA_HEAD_DIM), lambda b, g, q, k=kind: (b, kv0 + k * g_ + g))

    return pl.pallas_call(
        functools.partial(_nsa_kernel, seq=seq, tk=tk),
        grid=(batch, g_, nqb),
        in_specs=[
            pl.BlockSpec(memory_space=pltpu.SMEM),
            pl.BlockSpec((Q_BLOCK, gq), lambda b, g, q: (b * nqb + q, g)),
            pl.BlockSpec((Q_BLOCK, LANE), lambda b, g, q: (b * nqb + q, B_GATE // LANE + g)),
            pl.BlockSpec((1, 1, n_cpad, NSA_HEAD_DIM), lambda b, g, q: (b, g, 0, 0)),
            pl.BlockSpec((1, 1, n_cpad, NSA_HEAD_DIM), lambda b, g, q: (b, g_ + g, 0, 0)),
            seq_spec(2), seq_spec(3), seq_spec(4), seq_spec(5),
            pl.BlockSpec((seq, SEL_LANES), lambda b, g, q: (0, 0)),
            pl.BlockSpec((SEL_LANES, n_cpad), lambda b, g, q: (0, 0)),
        ],
        out_specs=pl.BlockSpec((Q_BLOCK, gq), lambda b, g, q: (b * nqb + q, g)),
        out_shape=jax.ShapeDtypeStruct((n, NSA_WIDTH), BF16),
        scratch_shapes=[
            pltpu.VMEM((seq, NSA_HEAD_DIM + SEL_LANES), BF16),
            pltpu.VMEM((NSA_REP * Q_BLOCK, NSA_HEAD_DIM + SEL_LANES), BF16),
            pltpu.VMEM((NSA_REP * Q_BLOCK, 1), F32),
            pltpu.VMEM((NSA_REP * Q_BLOCK, 1), F32),
            pltpu.VMEM((NSA_REP * Q_BLOCK, NSA_HEAD_DIM), F32),
        ],
        compiler_params=pltpu.CompilerParams(
            dimension_semantics=("parallel", "parallel", "arbitrary"), vmem_limit_bytes=VMEM_LIMIT),
        name="nsa",
    )(slopes, seg_a, seg_b, kvc, kvc, seg_a, seg_a, seg_a, seg_a, onehot, ovt)


def _shifted(x, tail, s):
    xs = pltpu.roll(x, s, axis=0)
    ts = pltpu.roll(tail, s, axis=0)
    row8 = lax.broadcasted_iota(jnp.int32, (8, 1), 0)
    head = jnp.where(row8 < s, ts, xs[:8])
    return jnp.concatenate([head, xs[8:]], axis=0)


def _conv_silu(x, tail, w, b):
    y = b + _shifted(x, tail, CONV_WIDTH - 1) * w[0:1]
    for i in range(1, CONV_WIDTH - 1):
        y = y + _shifted(x, tail, CONV_WIDTH - 1 - i) * w[i:i + 1]
    y = y + x * w[CONV_WIDTH - 1:CONV_WIDTH]
    return y * _sigmoid(y)


def _log_sigmoid(x):
    return jnp.minimum(x, 0.0) - jnp.log(1.0 + jnp.exp(-jnp.abs(x)))


def _mlstm_kernel(bias_ref, q_ref, k_ref, v_ref, o_ref, ifc_ref, ifr_ref, cw_ref, cb_ref, ng_ref,
                  tri_ref, y_ref, c_ref, n_ref, m_ref, qtail_ref, ktail_ref):
    ch = pl.program_id(1)
    nh, dh = MLSTM_HEADS, MLSTM_HEAD_DIM
    L = q_ref.shape[0]

    @pl.when(ch == 0)
    def _():
        c_ref[...] = jnp.zeros(c_ref.shape, F32)
        n_ref[...] = jnp.zeros(n_ref.shape, F32)
        m_ref[...] = jnp.zeros(m_ref.shape, F32)
        qtail_ref[...] = jnp.zeros(qtail_ref.shape, F32)
        ktail_ref[...] = jnp.zeros(ktail_ref.shape, F32)

    hi = lax.Precision.HIGHEST
    tri = tri_ref[...]
    lane8 = lax.broadcasted_iota(jnp.int32, (1, LANE), 1)
    bias_c = jnp.zeros((1, LANE), F32)
    for h in range(nh):
        bias_c = jnp.where(lane8 == h, bias_ref[h], bias_c)
        bias_c = jnp.where(lane8 == nh + h, bias_ref[nh + h], bias_c)
    pre_c = ifc_ref[...] + bias_c
    cum_c = jnp.dot(tri, _log_sigmoid(pre_c), precision=hi, preferred_element_type=F32)
    row8 = lax.broadcasted_iota(jnp.int32, (8, 1), 0)
    bias_r = jnp.zeros((8, 1), F32)
    for h in range(2 * nh):
        bias_r = jnp.where(row8 == h, bias_ref[h], bias_r)
    pre_r = ifr_ref[...] + bias_r
    cum_r = lax.dot_general(_log_sigmoid(pre_r), tri, (((1,), (1,)), ((), ())),
                            precision=hi, preferred_element_type=F32)

    rr = lax.broadcasted_iota(jnp.int32, (L, 1), 0)
    cc = lax.broadcasted_iota(jnp.int32, (1, L), 1)
    causal = cc <= rr

    q_raw = q_ref[...]
    k_raw = k_ref[...]
    cw = cw_ref[...]
    cb = cb_ref[...]
    qc = _conv_silu(q_raw, qtail_ref[...], cw[:, :nh * dh], cb[:, :nh * dh])
    kc = _conv_silu(k_raw, ktail_ref[...], cw[:, nh * dh:], cb[:, nh * dh:]) * (dh ** -0.5)
    qtail_ref[...] = q_raw[L - 8:]
    ktail_ref[...] = k_raw[L - 8:]

    for h in range(nh):
        cols = slice(h * dh, (h + 1) * dh)
        qh = qc[:, cols]
        kh = kc[:, cols]
        vh = v_ref[:, cols].astype(BF16)
        qb = qh.astype(BF16)
        b_c = cum_c[:, nh + h:nh + h + 1]
        li_c = pre_c[:, h:h + 1]
        b_r = cum_r[nh + h:nh + h + 1, :]
        li_r = pre_r[h:h + 1, :]
        m_prev = m_ref[h:h + 1, 0:1]

        dmat = jnp.where(causal, b_c - b_r + li_r, NEG)
        a = b_c + m_prev
        m_j = jnp.maximum(a, jnp.max(dmat, axis=1, keepdims=True))
        w_intra = jnp.exp(dmat - m_j)
        w_inter = jnp.exp(a - m_j)
        sc = _dot_nt(qb, kh.astype(BF16)) * w_intra
        c_old = c_ref[h]
        n_old = n_ref[h:h + 1, :]
        num = w_inter * _dot(qb, c_old.astype(BF16)) + _dot(sc.astype(BF16), vh)
        den = (w_inter * jnp.sum(qh * n_old, axis=1, keepdims=True)
               + jnp.sum(sc, axis=1, keepdims=True))
        hid = num / jnp.maximum(jnp.abs(den), jnp.exp(-m_j))

        g_tot = b_r[:, L - 1:L]
        lw_c = g_tot - b_c + li_c
        lw_r = g_tot - b_r + li_r
        m_new = jnp.maximum(g_tot + m_prev, jnp.max(lw_r, axis=1, keepdims=True))
        decay = jnp.exp(g_tot + m_prev - m_new)
        kw = jnp.exp(lw_c - m_new) * kh
        c_ref[h] = decay * c_old + _dot(kw.T.astype(BF16), vh)
        n_ref[h:h + 1, :] = decay * n_old + jnp.sum(kw, axis=0, keepdims=True)
        m_ref[h:h + 1, :] = jnp.broadcast_to(m_new, (1, LANE))

        hn = hid * lax.rsqrt(jnp.mean(hid * hid, axis=-1, keepdims=True) + EPS) * ng_ref[:, cols]
        y_ref[:, cols] = (_sigmoid(o_ref[:, cols]) * hn).astype(y_ref.dtype)


def _mlstm(seg_b, if_rows, bias, conv_w, conv_b, norm_g, tri, batch, seq, chunk):
    n = batch * seq
    nc = seq // chunk
    w = MLSTM_WIDTH
    nh, dh = MLSTM_HEADS, MLSTM_HEAD_DIM

    def col_spec(off):
        return pl.BlockSpec((chunk, w), lambda b, c, o=off // w: (b * nc + c, o))

    return pl.pallas_call(
        _mlstm_kernel,
        grid=(batch, nc),
        in_specs=[
            pl.BlockSpec(memory_space=pltpu.SMEM),
            col_spec(B_QK), col_spec(B_QK + w), col_spec(B_V), col_spec(B_O),
            pl.BlockSpec((chunk, LANE), lambda b, c: (b * nc + c, B_IF // LANE)),
            pl.BlockSpec((8, chunk), lambda b, c: (0, b * nc + c)),
            pl.BlockSpec((CONV_WIDTH, 2 * w), lambda b, c: (0, 0)),
            pl.BlockSpec((1, 2 * w), lambda b, c: (0, 0)),
            pl.BlockSpec((1, w), lambda b, c: (0, 0)),
            pl.BlockSpec((chunk, chunk), lambda b, c: (0, 0)),
        ],
        out_specs=pl.BlockSpec((chunk, w), lambda b, c: (b * nc + c, 0)),
        out_shape=jax.ShapeDtypeStruct((n, w), BF16),
        scratch_shapes=[
            pltpu.VMEM((nh, dh, dh), F32),
            pltpu.VMEM((8, dh), F32),
            pltpu.VMEM((8, LANE), F32),
            pltpu.VMEM((8, w), F32),
            pltpu.VMEM((8, w), F32),
        ],
        compiler_params=pltpu.CompilerParams(
            dimension_semantics=("parallel", "arbitrary"), vmem_limit_bytes=VMEM_LIMIT),
        name="mlstm",
    )(bias, seg_b, seg_b, seg_b, seg_b, seg_b, if_rows, conv_w, conv_b, norm_g, tri)


def _outproj_kernel(x_ref, ya_ref, ym_ref, wa_ref, wm_ref, o_ref):
    o_ref[...] = x_ref[...] + _dot(ya_ref[...], wa_ref[...]) + _dot(ym_ref[...], wm_ref[...])


def _outproj(x2, ya, ym, wa, wm, tm):
    n, d = x2.shape
    return pl.pallas_call(
        _outproj_kernel,
        grid=(n // tm,),
        in_specs=[
            pl.BlockSpec((tm, d), lambda i: (i, 0)),
            pl.BlockSpec((tm, ya.shape[1]), lambda i: (i, 0)),
            pl.BlockSpec((tm, ym.shape[1]), lambda i: (i, 0)),
            pl.BlockSpec(wa.shape, lambda i: (0, 0)),
            pl.BlockSpec(wm.shape, lambda i: (0, 0)),
        ],
        out_specs=pl.BlockSpec((tm, d), lambda i: (i, 0)),
        out_shape=jax.ShapeDtypeStruct((n, d), F32),
        compiler_params=pltpu.CompilerParams(
            dimension_semantics=("parallel",), vmem_limit_bytes=VMEM_LIMIT),
        name="outproj",
    )(x2, ya, ym, wa, wm)


def _mlp_kernel(x_ref, g_ref, w1_ref, w2_ref, gf_ref, o_ref, h_ref, acc_ref):
    f = pl.program_id(1)

    @pl.when(f == 0)
    def _():
        x = x_ref[...]
        r = lax.rsqrt(jnp.mean(x * x, axis=-1, keepdims=True) + EPS)
        h_ref[...] = (x * r * g_ref[...]).astype(BF16)
        acc_ref[...] = jnp.zeros(acc_ref.shape, F32)

    u = jnp.maximum(_dot(h_ref[...], w1_ref[...]), 0.0)
    acc_ref[...] += _dot((u * u).astype(BF16), w2_ref[...])

    @pl.when(f == pl.num_programs(1) - 1)
    def _():
        x2 = x_ref[...] + acc_ref[...]
        r = lax.rsqrt(jnp.mean(x2 * x2, axis=-1, keepdims=True) + EPS)
        o_ref[...] = x2 * r * gf_ref[...]


def _mlp(x1, g, w1, w2, gf, tm, tf):
    n, d = x1.shape
    dff = w1.shape[1]
    return pl.pallas_call(
        _mlp_kernel,
        grid=(n // tm, dff // tf),
        in_specs=[
            pl.BlockSpec((tm, d), lambda i, f: (i, 0)),
            pl.BlockSpec((1, d), lambda i, f: (0, 0)),
            pl.BlockSpec((d, tf), lambda i, f: (0, f)),
            pl.BlockSpec((tf, d), lambda i, f: (f, 0)),
            pl.BlockSpec((1, d), lambda i, f: (0, 0)),
        ],
        out_specs=pl.BlockSpec((tm, d), lambda i, f: (i, 0)),
        out_shape=jax.ShapeDtypeStruct((n, d), F32),
        scratch_shapes=[pltpu.VMEM((tm, d), BF16), pltpu.VMEM((tm, d), F32)],
        compiler_params=pltpu.CompilerParams(
            dimension_semantics=("parallel", "arbitrary"), vmem_limit_bytes=VMEM_LIMIT),
        name="mlp",
    )(x1, g, w1, w2, gf)


def _row_tile(n, want):
    t = want
    while n % t:
        t //= 2
    return t


def _layer(x2, batch, seq, norm_mix_g, w_in, w_cmp_k1, w_cmp_k2, pos_cmp_k, w_cmp_v1, w_cmp_v2,
           pos_cmp_v, conv_w, conv_b, b_igate, b_fgate, mlstm_norm_g, w_out, norm_mlp_g,
           w_mlp_in, w_mlp_out):
    n, d = x2.shape
    assert seq % Q_BLOCK == 0 and seq >= WINDOW + Q_BLOCK and seq // SEL_BLOCK <= SEL_LANES
    g_ = NSA_KV_GROUPS
    nh = MLSTM_HEADS

    c_gate = SEG_A
    c_qk = c_gate + NSA_HEADS * N_BRANCH
    c_i = c_qk + 4 * MLSTM_WIDTH
    c_f = c_i + nh
    w_a = w_in[:, :SEG_A].astype(BF16)
    gate_cols = []
    per_g = NSA_REP * N_BRANCH
    for g in range(g_):
        gate_cols += [w_in[:, c_gate + g * per_g:c_gate + (g + 1) * per_g],
                      jnp.zeros((d, LANE - per_g), w_in.dtype)]
    w_b = jnp.concatenate(
        [w_in[:, c_qk:c_i]] + gate_cols
        + [w_in[:, c_i:c_f + nh], jnp.zeros((d, SEG_B - B_IF - 2 * nh), w_in.dtype)],
        axis=1).astype(BF16)
    scale_a = jnp.concatenate([jnp.full((1, NSA_WIDTH), NSA_HEAD_DIM ** -0.5, F32),
                               jnp.ones((1, SEG_A - NSA_WIDTH), F32)], axis=1)
    scale_b = jnp.ones((1, SEG_B), F32)
    g_mix = norm_mix_g.reshape(1, d)

    tm = _row_tile(n, 512)
    seg_a = _inproj(x2, g_mix, w_a, scale_a, BF16, tm, 512)
    seg_b = _inproj(x2, g_mix, w_b, scale_b, F32, tm, 1536)

    n_sub = seq // CMP_STRIDE
    cmp_in = seg_a[:, NSA_WIDTH:NSA_WIDTH + 2 * NSA_KV_WIDTH]
    cmp_in = cmp_in.reshape(batch, n_sub, CMP_STRIDE, 2 * g_, NSA_HEAD_DIM)
    cmp_in = cmp_in.transpose(0, 3, 1, 2, 4).reshape(batch, 2 * g_, n_sub, CMP_STRIDE * NSA_HEAD_DIM)
    w1s = jnp.stack([w_cmp_k1, w_cmp_v1]).astype(BF16)
    w2s = jnp.stack([w_cmp_k2, w_cmp_v2]).astype(BF16)
    poss = jnp.stack([pos_cmp_k, pos_cmp_v]).reshape(2, 1, CMP_BLOCK * NSA_HEAD_DIM)
    poss = jnp.broadcast_to(poss, (2, 8, CMP_BLOCK * NSA_HEAD_DIM)).astype(BF16)
    kvc = _compress(cmp_in, w1s, w2s, poss)

    key_blk = np.arange(seq) // SEL_BLOCK
    onehot = jnp.asarray(key_blk[:, None] == np.arange(SEL_LANES)[None, :], BF16)
    cmp_start = np.arange(n_sub) * CMP_STRIDE
    sel_start = np.arange(SEL_LANES) * SEL_BLOCK
    ovt = ((cmp_start[None, :] < sel_start[:, None] + SEL_BLOCK)
           & (cmp_start[None, :] + CMP_BLOCK - 1 >= sel_start[:, None])
           & (np.arange(n_sub)[None, :] < n_sub - CMP_BLOCK // CMP_STRIDE + 1))
    ovt = jnp.asarray(ovt, BF16)
    slopes = jnp.exp2(-8.0 * jnp.arange(1, NSA_HEADS + 1, dtype=F32) / NSA_HEADS)
    y_a = _nsa(seg_a, seg_b, kvc, onehot, ovt, slopes, batch, seq)

    chunk = 256 if seq % 256 == 0 else 128
    if_rows = seg_b[:, B_IF:B_IF + 2 * nh].T
    bias = jnp.concatenate([b_igate, b_fgate]).astype(F32)
    tri = jnp.asarray(np.tril(np.ones((chunk, chunk), np.float32)))
    y_m = _mlstm(seg_b, if_rows, bias, conv_w, conv_b.reshape(1, -1), mlstm_norm_g.reshape(1, -1),
                 tri, batch, seq, chunk)

    w_o = w_out.astype(BF16)
    x1 = _outproj(x2, y_a, y_m, w_o[:NSA_WIDTH], w_o[NSA_WIDTH:], tm)
    return x1, (norm_mlp_g.reshape(1, d), w_mlp_in.astype(BF16), w_mlp_out.astype(BF16))


def kernel(x, norm_mix_g, w_in, w_cmp_k1, w_cmp_k2, pos_cmp_k, w_cmp_v1, w_cmp_v2, pos_cmp_v, conv_w, conv_b, b_igate, b_fgate, mlstm_norm_g, w_out, norm_mlp_g, w_mlp_in, w_mlp_out, norm_f_g):
    batch, seq, d = x.shape
    depth = w_in.shape[0]
    assert depth == 1, "the final RMSNorm is fused into the last layer's channel mixer"
    x2 = x.reshape(batch * seq, d)
    tm = _row_tile(batch * seq, 512)
    for l in range(depth):
        x1, (g_mlp, w1, w2) = _layer(
            x2, batch, seq, norm_mix_g[l], w_in[l], w_cmp_k1[l], w_cmp_k2[l], pos_cmp_k[l],
            w_cmp_v1[l], w_cmp_v2[l], pos_cmp_v[l], conv_w[l], conv_b[l], b_igate[l], b_fgate[l],
            mlstm_norm_g[l], w_out[l], norm_mlp_g[l], w_mlp_in[l], w_mlp_out[l])
        x2 = _mlp(x1, g_mlp, w1, w2, norm_f_g.reshape(1, d), tm, 512)
    return x2.reshape(batch, seq, d)
```

```python
import functools

import numpy as np
import jax
import jax.numpy as jnp
from jax import lax
from jax.experimental import pallas as pl
from jax.experimental.pallas import tpu as pltpu

F32 = jnp.float32
BF16 = jnp.bfloat16

EPS = 1e-6
NEG = -1e30
FORCE_BONUS = 1e4
PICKED = -3e38
MASK_BIG = 1e30

D_MODEL = 2048
NSA_HEAD_DIM = 128
NSA_WIDTH = D_MODEL // 2
NSA_HEADS = NSA_WIDTH // NSA_HEAD_DIM
NSA_REP = 4
NSA_KV_GROUPS = NSA_HEADS // NSA_REP
NSA_KV_WIDTH = NSA_KV_GROUPS * NSA_HEAD_DIM
CMP_BLOCK = 32
CMP_STRIDE = 16
SEL_BLOCK = 64
SEL_TOPK = 16
WINDOW = 512
Q_BLOCK = 128
N_BRANCH = 3
MLSTM_HEAD_DIM = 256
MLSTM_WIDTH = D_MODEL - NSA_WIDTH
MLSTM_HEADS = MLSTM_WIDTH // MLSTM_HEAD_DIM
CONV_WIDTH = 4
D_FF = 4 * D_MODEL

LANE = 128
SEL_LANES = 128
KEY_TILE = 128
SEL_TILES_PER_STEP = 2
WIN_TILES = WINDOW // KEY_TILE
VMEM_LIMIT = 56 * 1024 * 1024

SEG_A = NSA_WIDTH + 6 * NSA_KV_WIDTH
B_QK, B_V, B_O = 0, 2 * MLSTM_WIDTH, 3 * MLSTM_WIDTH
B_GATE = 4 * MLSTM_WIDTH
B_IF = B_GATE + NSA_KV_GROUPS * LANE
SEG_B = 4608

WCOL_PAD, WCOL_HI, WCOL_LO, WCOL_QHI, WCOL_QLO = 0, 1, 2, 3, 4


def _dot(a, b):
    return jnp.dot(a, b, preferred_element_type=F32)


def _dot_nt(a, b):
    return lax.dot_general(a, b, (((1,), (1,)), ((), ())), preferred_element_type=F32)


def _sigmoid(x):
    return 1.0 / (1.0 + jnp.exp(-x))


def _inproj_kernel(x_ref, g_ref, w_ref, cs_ref, o_ref, h_ref):
    @pl.when(pl.program_id(1) == 0)
    def _():
        x = x_ref[...]
        r = lax.rsqrt(jnp.mean(x * x, axis=-1, keepdims=True) + EPS)
        h_ref[...] = (x * r * g_ref[...]).astype(BF16)

    o_ref[...] = (_dot(h_ref[...], w_ref[...]) * cs_ref[...]).astype(o_ref.dtype)


def _inproj(x2, g, w, cs, out_dtype, tm, tn):
    n, d = x2.shape
    nc = w.shape[1]
    return pl.pallas_call(
        _inproj_kernel,
        grid=(n // tm, nc // tn),
        in_specs=[
            pl.BlockSpec((tm, d), lambda i, j: (i, 0)),
            pl.BlockSpec((1, d), lambda i, j: (0, 0)),
            pl.BlockSpec((d, tn), lambda i, j: (0, j)),
            pl.BlockSpec((1, tn), lambda i, j: (0, j)),
        ],
        out_specs=pl.BlockSpec((tm, tn), lambda i, j: (i, j)),
        out_shape=jax.ShapeDtypeStruct((n, nc), out_dtype),
        scratch_shapes=[pltpu.VMEM((tm, d), BF16)],
        compiler_params=pltpu.CompilerParams(
            dimension_semantics=("parallel", "arbitrary"), vmem_limit_bytes=VMEM_LIMIT),
        name="inproj",
    )(x2, g, w, cs)


def _compress_kernel(x_ref, w1_ref, w2_ref, pos_ref, o_ref, ot_ref):
    x = x_ref[0, 0]
    n_sub = x.shape[0]
    half = CMP_STRIDE * NSA_HEAD_DIM
    w1 = w1_ref[0]
    top = _dot(x, w1[:half])
    bot = _dot(x, w1[half:])
    bot = pltpu.roll(bot, n_sub - 1, axis=0)
    posw = _dot(pos_ref[0], w1)[0:1]
    pre = top + bot + posw
    hid = pre * _sigmoid(pre)
    out = _dot(hid.astype(BF16), w2_ref[0])
    o_ref[0, 0] = out.astype(o_ref.dtype)
    ot_ref[0, 0] = out.T.astype(ot_ref.dtype)


def _compress(xs, w1s, w2s, poss):
    b, c, n_sub, feat = xs.shape
    g = NSA_KV_GROUPS
    return pl.pallas_call(
        _compress_kernel,
        grid=(b, c),
        in_specs=[
            pl.BlockSpec((1, 1, n_sub, feat), lambda i, j: (i, j, 0, 0)),
            pl.BlockSpec((1, 2 * feat, NSA_HEAD_DIM), lambda i, j: (j // g, 0, 0)),
            pl.BlockSpec((1, NSA_HEAD_DIM, NSA_HEAD_DIM), lambda i, j: (j // g, 0, 0)),
            pl.BlockSpec((1, 8, 2 * feat), lambda i, j: (j // g, 0, 0)),
        ],
        out_specs=[pl.BlockSpec((1, 1, n_sub, NSA_HEAD_DIM), lambda i, j: (i, j, 0, 0)),
                   pl.BlockSpec((1, 1, NSA_HEAD_DIM, n_sub), lambda i, j: (i, j, 0, 0))],
        out_shape=[jax.ShapeDtypeStruct((b, c, n_sub, NSA_HEAD_DIM), BF16),
                   jax.ShapeDtypeStruct((b, c, NSA_HEAD_DIM, n_sub), BF16)],
        compiler_params=pltpu.CompilerParams(
            dimension_semantics=("parallel", "parallel"), vmem_limit_bytes=VMEM_LIMIT),
        name="compress",
    )(xs, w1s, w2s, poss)


def _stack_heads(q_all):
    dh = NSA_HEAD_DIM
    return jnp.concatenate([q_all[:, r * dh:(r + 1) * dh] for r in range(NSA_REP)], axis=0)


def _nsa_cmp_kernel(slopes_ref, q_ref, kc_ref, vct_ref, ovt_ref, ocmp_ref, pen_ref, flag_ref, *, seq):
    g = pl.program_id(1)
    qb = pl.program_id(2)
    nq = Q_BLOCK
    n_cpad = kc_ref.shape[2]
    n_cmp = seq // CMP_STRIDE - CMP_BLOCK // CMP_STRIDE + 1
    t0 = qb * nq

    s_t = _dot_nt(kc_ref[0, 0], _stack_heads(q_ref[...]))
    n_s = lax.broadcasted_iota(jnp.int32, (n_cpad, 1), 0)
    q_l = lax.broadcasted_iota(jnp.int32, (1, nq), 1)
    dist = (t0 - (CMP_BLOCK - 1)) + q_l - n_s * CMP_STRIDE
    real = n_s < n_cmp
    valid = (dist >= 0) & real
    dist_f = dist.astype(F32)
    probs = []
    p_sum = jnp.zeros((n_cpad, nq), F32)
    for r in range(NSA_REP):
        slope = slopes_ref[g * NSA_REP + r]
        s = jnp.where(valid, s_t[:, r * nq:(r + 1) * nq] - slope * dist_f, NEG)
        e = jnp.exp(s - jnp.max(s, axis=0, keepdims=True))
        denom = jnp.sum(jnp.where(real, e, 0.0), axis=0, keepdims=True)
        p = jnp.where(valid, e, 0.0) / denom
        probs.append(p.astype(BF16))
        p_sum = p_sum + p
    ocmp_ref[0, 0, 0] = _dot(vct_ref[0, 0], jnp.concatenate(probs, axis=1))

    p_hi = p_sum.astype(BF16)
    p_lo = (p_sum - p_hi.astype(F32)).astype(BF16)
    ovt = ovt_ref[...]
    imp = _dot(ovt, p_hi) + _dot(ovt, p_lo)
    j_i = lax.broadcasted_iota(jnp.int32, (SEL_LANES, 1), 0)
    t_l = t0 + q_l
    cur = t_l // SEL_BLOCK
    forced = (j_i == 0) | (j_i == cur) | (j_i == cur - 1)
    causal_blk = j_i * SEL_BLOCK <= t_l
    val = jnp.where(causal_blk, jnp.where(forced, imp + FORCE_BONUS, imp), NEG)
    j_f = j_i.astype(F32)
    sel_t = jnp.zeros((SEL_LANES, nq), F32)
    for _ in range(min(SEL_TOPK, seq // SEL_BLOCK)):
        mx = jnp.max(val, axis=0, keepdims=True)
        first = jnp.min(jnp.where(val == mx, j_f, float(SEL_LANES)), axis=0, keepdims=True)
        pick = j_f == first
        sel_t = jnp.where(pick, 1.0, sel_t)
        val = jnp.where(pick, PICKED, val)
    sel = sel_t.T
    pen_ref[0, 0] = ((sel - 1.0) * MASK_BIG).astype(pen_ref.dtype)
    flag_ref[0, 0, 0] = (jnp.max(sel, axis=0, keepdims=True) > 0.0).astype(jnp.int32)


def _nsa_cmp(seg_a, kvc, kvct, ovt, slopes, batch, seq):
    nqb = seq // Q_BLOCK
    gq = NSA_REP * NSA_HEAD_DIM
    g_ = NSA_KV_GROUPS
    n_cpad = kvc.shape[2]
    return pl.pallas_call(
        functools.partial(_nsa_cmp_kernel, seq=seq),
        grid=(batch, g_, nqb),
        in_specs=[
            pl.BlockSpec(memory_space=pltpu.SMEM),
            pl.BlockSpec((Q_BLOCK, gq), lambda b, g, q: (b * nqb + q, g)),
            pl.BlockSpec((1, 1, n_cpad, NSA_HEAD_DIM), lambda b, g, q: (b, g, 0, 0)),
            pl.BlockSpec((1, 1, NSA_HEAD_DIM, n_cpad), lambda b, g, q: (b, g_ + g, 0, 0)),
            pl.BlockSpec((SEL_LANES, n_cpad), lambda b, g, q: (0, 0)),
        ],
        out_specs=[
            pl.BlockSpec((1, 1, 1, NSA_HEAD_DIM, gq), lambda b, g, q: (b, g, q, 0, 0)),
            pl.BlockSpec((1, 1, Q_BLOCK, SEL_LANES), lambda b, g, q: (b, g, q, 0)),
            pl.BlockSpec((1, 1, 1, 1, SEL_LANES), lambda b, g, q: (b, g, q, 0, 0)),
        ],
        out_shape=[
            jax.ShapeDtypeStruct((batch, g_, nqb, NSA_HEAD_DIM, gq), F32),
            jax.ShapeDtypeStruct((batch, g_, seq, SEL_LANES), BF16),
            jax.ShapeDtypeStruct((batch, g_, nqb, 1, SEL_LANES), jnp.int32),
        ],
        compiler_params=pltpu.CompilerParams(
            dimension_semantics=("parallel", "parallel", "parallel"), vmem_limit_bytes=VMEM_LIMIT),
        name="nsa_cmp",
    )(slopes, seg_a, kvc, kvct, ovt)


def _nsa_attn_kernel(slopes_ref, flag_ref, q_ref, gate_ref, pen_ref, ocmp_ref, ksel_ref, vselt_ref,
                     kwin_ref, vwint_ref, dbias_ref, ubias_ref, wbias_ref, o_ref,
                     qa_ref, list_ref, m_ref, l_ref, acc_ref, *, seq):
    g = pl.program_id(1)
    qb = pl.program_id(2)
    dh = NSA_HEAD_DIM
    nq = Q_BLOCK
    kt = KEY_TILE
    per = SEL_TILES_PER_STEP
    pad_tile = seq // kt
    t0 = qb * nq
    slopes = [slopes_ref[g * NSA_REP + r] for r in range(NSA_REP)]
    row_q = lax.broadcasted_iota(jnp.int32, (nq, 1), 0)
    lane = lax.broadcasted_iota(jnp.int32, (1, LANE), 1)
    q_all = q_ref[...]

    pen = pen_ref[0, 0].astype(F32)
    blk_rel = ((lane - (t0 + row_q) // SEL_BLOCK) * SEL_BLOCK).astype(F32)
    for r in range(NSA_REP):
        qa_ref[r * nq:(r + 1) * nq, :dh] = q_all[:, r * dh:(r + 1) * dh]
        qa_ref[r * nq:(r + 1) * nq, dh:] = (pen + slopes[r] * blk_rel).astype(BF16)
    qa = qa_ref[...]

    s = _dot_nt(ksel_ref[0, 0, pl.ds(pl.multiple_of(t0, kt), kt), :], qa) + dbias_ref[0]
    m0 = jnp.max(s, axis=0, keepdims=True)
    p = jnp.exp(s - m0)
    m_ref[...] = m0
    l_ref[...] = jnp.sum(p, axis=0, keepdims=True)
    acc_ref[...] = _dot(vselt_ref[0, 0, qb], p.astype(BF16))

    def scan(i, cnt):
        act = (flag_ref[0, 0, 0, 0, 2 * i] + flag_ref[0, 0, 0, 0, 2 * i + 1]) > 0

        @pl.when(act)
        def _():
            list_ref[cnt] = i

        return cnt + act.astype(jnp.int32)

    cnt = lax.fori_loop(0, qb, scan, 0)
    for i in range(per):
        list_ref[cnt + i] = pad_tile
    ubias = ubias_ref[0]

    def sel_body(it, carry):
        idx = [list_ref[it * per + i] for i in range(per)]
        keys = jnp.concatenate(
            [ksel_ref[0, 0, pl.ds(pl.multiple_of(ix * kt, kt), kt), :] for ix in idx], axis=0)
        vals = jnp.concatenate([vselt_ref[0, 0, ix] for ix in idx], axis=1)
        s = _dot_nt(keys, qa) + ubias
        m_old = m_ref[...]
        m_new = jnp.maximum(m_old, jnp.max(s, axis=0, keepdims=True))
        alpha = jnp.exp(m_old - m_new)
        p = jnp.exp(s - m_new)
        l_ref[...] = alpha * l_ref[...] + jnp.sum(p, axis=0, keepdims=True)
        acc_ref[...] = alpha * acc_ref[...] + _dot(vals, p.astype(BF16))
        m_ref[...] = m_new
        return carry

    lax.fori_loop(0, (cnt + per - 1) // per, sel_body, 0)
    o_sel = acc_ref[...] / l_ref[...]

    tp = t0 + WINDOW + row_q
    t_hi = (tp // SEL_BLOCK).astype(F32)
    t_lo = (tp % SEL_BLOCK).astype(F32)
    for r in range(NSA_REP):
        sl = slopes[r]
        ext = jnp.where(lane == WCOL_PAD, -MASK_BIG, 0.0)
        ext = jnp.where(lane == WCOL_HI, sl * SEL_BLOCK, ext)
        ext = jnp.where(lane == WCOL_LO, sl, ext)
        ext = jnp.where(lane == WCOL_QHI, -sl * SEL_BLOCK * t_hi, ext)
        ext = jnp.where(lane == WCOL_QLO, -sl * t_lo, ext)
        qa_ref[r * nq:(r + 1) * nq, dh:] = ext.astype(BF16)
    wlen = WINDOW + nq
    s = _dot_nt(kwin_ref[0, 0, pl.ds(pl.multiple_of(t0, kt), wlen), :], qa_ref[...]) + wbias_ref[...]
    e = jnp.exp(s - jnp.max(s, axis=0, keepdims=True))
    vwin = jnp.concatenate([vwint_ref[0, 0, qb + i] for i in range(WIN_TILES + 1)], axis=1)
    o_win = _dot(vwin, e.astype(BF16)) / jnp.sum(e, axis=0, keepdims=True)

    gate_t = _sigmoid(gate_ref[...]).T
    o_cmp = ocmp_ref[0, 0, 0]
    for r in range(NSA_REP):
        cols = slice(r * nq, (r + 1) * nq)
        c0 = N_BRANCH * r
        out_t = (gate_t[c0:c0 + 1] * o_cmp[:, cols] + gate_t[c0 + 1:c0 + 2] * o_sel[:, cols]
                 + gate_t[c0 + 2:c0 + 3] * o_win[:, cols])
        o_ref[:, r * dh:(r + 1) * dh] = out_t.T.astype(o_ref.dtype)


def _nsa_attn(seg_a, seg_b, pen, ocmp, flags, ksel, vselt, kwin, vwint, dbias, ubias, wbias,
              slopes, batch, seq):
    n = batch * seq
    nqb = seq // Q_BLOCK
    gq = NSA_REP * NSA_HEAD_DIM
    g_ = NSA_KV_GROUPS
    dk = ksel.shape[-1]
    rq = NSA_REP * Q_BLOCK

    def whole(arr):
        shp = (1, 1) + arr.shape[2:]
        return pl.BlockSpec(shp, lambda b, g, q, nd=len(shp): (b, g) + (0,) * (nd - 2))

    return pl.pallas_call(
        functools.partial(_nsa_attn_kernel, seq=seq),
        grid=(batch, g_, nqb),
        in_specs=[
            pl.BlockSpec(memory_space=pltpu.SMEM),
            pl.BlockSpec((1, 1, 1, 1, SEL_LANES), lambda b, g, q: (b, g, q, 0, 0),
                         memory_space=pltpu.SMEM),
            pl.BlockSpec((Q_BLOCK, gq), lambda b, g, q: (b * nqb + q, g)),
            pl.BlockSpec((Q_BLOCK, LANE), lambda b, g, q: (b * nqb + q, B_GATE // LANE + g)),
            pl.BlockSpec((1, 1, Q_BLOCK, SEL_LANES), lambda b, g, q: (b, g, q, 0)),
            pl.BlockSpec((1, 1, 1, NSA_HEAD_DIM, gq), lambda b, g, q: (b, g, q, 0, 0)),
            whole(ksel), whole(vselt), whole(kwin), whole(vwint),
            pl.BlockSpec((1,) + dbias.shape[1:], lambda b, g, q: (g, 0, 0)),
            pl.BlockSpec((1,) + ubias.shape[1:], lambda b, g, q: (g, 0, 0)),
            pl.BlockSpec(wbias.shape, lambda b, g, q: (0, 0)),
        ],
        out_specs=pl.BlockSpec((Q_BLOCK, gq), lambda b, g, q: (b * nqb + q, g)),
        out_shape=jax.ShapeDtypeStruct((n, NSA_WIDTH), BF16),
        scratch_shapes=[
            pltpu.VMEM((rq, dk), BF16),
            pltpu.SMEM((seq // KEY_TILE + SEL_TILES_PER_STEP,), jnp.int32),
            pltpu.VMEM((1, rq), F32),
            pltpu.VMEM((1, rq), F32),
            pltpu.VMEM((NSA_HEAD_DIM, rq), F32),
        ],
        compiler_params=pltpu.CompilerParams(
            dimension_semantics=("parallel", "parallel", "arbitrary"), vmem_limit_bytes=VMEM_LIMIT),
        name="nsa_attn",
    )(slopes, flags, seg_a, seg_b, pen, ocmp, ksel, vselt, kwin, vwint, dbias, ubias, wbias)


def _nsa_operands(seg_a, slopes, batch, seq):
    g_, dh, kt = NSA_KV_GROUPS, NSA_HEAD_DIM, KEY_TILE
    n_t = seq // kt
    a3 = seg_a.reshape(batch, seq, SEG_A)

    def grouped(kind):
        c0 = NSA_WIDTH + kind * NSA_KV_WIDTH
        return a3[:, :, c0:c0 + NSA_KV_WIDTH].reshape(batch, seq, g_, dh).transpose(0, 2, 1, 3)

    def tiles_t(v, front):
        vt = v.reshape(batch, g_, n_t, kt, dh).transpose(0, 1, 2, 4, 3)
        return jnp.pad(vt, ((0, 0), (0, 0), (front, 1 - min(front, 1)), (0, 0), (0, 0)))

    pos = np.arange(seq)
    onehot = jnp.asarray(pos[:, None] // SEL_BLOCK == np.arange(SEL_LANES)[None, :], BF16)
    ksel = jnp.concatenate([grouped(2), jnp.broadcast_to(onehot, (batch, g_, seq, SEL_LANES))], axis=-1)
    pad_rows = jnp.concatenate([jnp.zeros((kt, dh), BF16), jnp.ones((kt, SEL_LANES), BF16)], axis=-1)
    ksel = jnp.concatenate([ksel, jnp.broadcast_to(pad_rows, (batch, g_, kt, dh + SEL_LANES))], axis=2)
    vselt = tiles_t(grouped(3), 0)

    ext = np.zeros((seq, LANE), np.float32)
    ext[:, WCOL_HI] = (pos + WINDOW) // SEL_BLOCK
    ext[:, WCOL_LO] = (pos + WINDOW) % SEL_BLOCK
    ext[:, WCOL_QHI] = 1.0
    ext[:, WCOL_QLO] = 1.0
    kwin = jnp.concatenate([grouped(4), jnp.broadcast_to(jnp.asarray(ext, BF16), (batch, g_, seq, LANE))],
                           axis=-1)
    front = np.zeros((WINDOW, dh + LANE), np.float32)
    front[:, dh + WCOL_PAD] = 1.0
    kwin = jnp.concatenate([jnp.broadcast_to(jnp.asarray(front, BF16), (batch, g_, WINDOW, dh + LANE)),
                            kwin], axis=2)
    vwint = tiles_t(grouped(5), WIN_TILES)

    rows = SEL_TILES_PER_STEP * kt
    u = jnp.asarray((np.arange(rows) % SEL_BLOCK).astype(np.float32))[None, :, None, None]
    ubias = jnp.broadcast_to(slopes.reshape(g_, 1, NSA_REP, 1) * u, (g_, rows, NSA_REP, Q_BLOCK))
    ubias = ubias.reshape(g_, rows, NSA_REP * Q_BLOCK)
    kq = np.arange(kt)[:, None] <= np.arange(Q_BLOCK)[None, :]
    causal = np.tile(np.where(kq, 0.0, NEG).astype(np.float32), (1, NSA_REP))
    dbias = ubias[:, :kt] + jnp.asarray(causal)[None]
    ki = np.arange(WINDOW + Q_BLOCK)[:, None]
    qi = np.arange(Q_BLOCK)[None, :]
    band = np.where((ki > qi) & (ki <= qi + WINDOW), 0.0, NEG).astype(np.float32)
    wbias = jnp.asarray(np.tile(band, (1, NSA_REP)))
    return ksel, vselt, kwin, vwint, dbias, ubias, wbias


def _shifted(x, tail, s):
    xs = pltpu.roll(x, s, axis=0)
    ts = pltpu.roll(tail, s, axis=0)
    row8 = lax.broadcasted_iota(jnp.int32, (8, 1), 0)
    head = jnp.where(row8 < s, ts, xs[:8])
    return jnp.concatenate([head, xs[8:]], axis=0)


def _conv_silu(x, tail, w, b):
    y = b + _shifted(x, tail, CONV_WIDTH - 1) * w[0:1]
    for i in range(1, CONV_WIDTH - 1):
        y = y + _shifted(x, tail, CONV_WIDTH - 1 - i) * w[i:i + 1]
    y = y + x * w[CONV_WIDTH - 1:CONV_WIDTH]
    return y * _sigmoid(y)


def _log_sigmoid(x):
    return jnp.minimum(x, 0.0) - jnp.log(1.0 + jnp.exp(-jnp.abs(x)))


def _mlstm_kernel(bias_ref, q_ref, k_ref, v_ref, o_ref, ifc_ref, ifr_ref, cw_ref, cb_ref, ng_ref,
                  tri_ref, y_ref, c_ref, n_ref, m_ref, qtail_ref, ktail_ref):
    ch = pl.program_id(1)
    nh, dh = MLSTM_HEADS, MLSTM_HEAD_DIM
    L = q_ref.shape[0]

    @pl.when(ch == 0)
    def _():
        c_ref[...] = jnp.zeros(c_ref.shape, F32)
        n_ref[...] = jnp.zeros(n_ref.shape, F32)
        m_ref[...] = jnp.zeros(m_ref.shape, F32)
        qtail_ref[...] = jnp.zeros(qtail_ref.shape, F32)
        ktail_ref[...] = jnp.zeros(ktail_ref.shape, F32)

    hi = lax.Precision.HIGHEST
    tri = tri_ref[...]
    lane8 = lax.broadcasted_iota(jnp.int32, (1, LANE), 1)
    bias_c = jnp.zeros((1, LANE), F32)
    for h in range(nh):
        bias_c = jnp.where(lane8 == h, bias_ref[h], bias_c)
        bias_c = jnp.where(lane8 == nh + h, bias_ref[nh + h], bias_c)
    pre_c = ifc_ref[...] + bias_c
    cum_c = jnp.dot(tri, _log_sigmoid(pre_c), precision=hi, preferred_element_type=F32)
    row8 = lax.broadcasted_iota(jnp.int32, (8, 1), 0)
    bias_r = jnp.zeros((8, 1), F32)
    for h in range(2 * nh):
        bias_r = jnp.where(row8 == h, bias_ref[h], bias_r)
    pre_r = ifr_ref[...] + bias_r
    cum_r = lax.dot_general(_log_sigmoid(pre_r), tri, (((1,), (1,)), ((), ())),
                            precision=hi, preferred_element_type=F32)

    rr = lax.broadcasted_iota(jnp.int32, (L, 1), 0)
    cc = lax.broadcasted_iota(jnp.int32, (1, L), 1)
    causal = cc <= rr

    q_raw = q_ref[...]
    k_raw = k_ref[...]
    cw = cw_ref[...]
    cb = cb_ref[...]
    qc = _conv_silu(q_raw, qtail_ref[...], cw[:, :nh * dh], cb[:, :nh * dh])
    kc = _conv_silu(k_raw, ktail_ref[...], cw[:, nh * dh:], cb[:, nh * dh:]) * (dh ** -0.5)
    qtail_ref[...] = q_raw[L - 8:]
    ktail_ref[...] = k_raw[L - 8:]

    for h in range(nh):
        cols = slice(h * dh, (h + 1) * dh)
        qh = qc[:, cols]
        kh = kc[:, cols]
        vh = v_ref[:, cols].astype(BF16)
        qb = qh.astype(BF16)
        b_c = cum_c[:, nh + h:nh + h + 1]
        li_c = pre_c[:, h:h + 1]
        b_r = cum_r[nh + h:nh + h + 1, :]
        li_r = pre_r[h:h + 1, :]
        m_prev = m_ref[h:h + 1, 0:1]

        dmat = jnp.where(causal, b_c - b_r + li_r, NEG)
        a = b_c + m_prev
        m_j = jnp.maximum(a, jnp.max(dmat, axis=1, keepdims=True))
        w_intra = jnp.exp(dmat - m_j)
        w_inter = jnp.exp(a - m_j)
        sc = _dot_nt(qb, kh.astype(BF16)) * w_intra
        c_old = c_ref[h]
        n_old = n_ref[h:h + 1, :]
        num = w_inter * _dot(qb, c_old.astype(BF16)) + _dot(sc.astype(BF16), vh)
        den = (w_inter * jnp.sum(qh * n_old, axis=1, keepdims=True)
               + jnp.sum(sc, axis=1, keepdims=True))
        hid = num / jnp.maximum(jnp.abs(den), jnp.exp(-m_j))

        g_tot = b_r[:, L - 1:L]
        lw_c = g_tot - b_c + li_c
        lw_r = g_tot - b_r + li_r
        m_new = jnp.maximum(g_tot + m_prev, jnp.max(lw_r, axis=1, keepdims=True))
        decay = jnp.exp(g_tot + m_prev - m_new)
        kw = jnp.exp(lw_c - m_new) * kh
        c_ref[h] = decay * c_old + _dot(kw.T.astype(BF16), vh)
        n_ref[h:h + 1, :] = decay * n_old + jnp.sum(kw, axis=0, keepdims=True)
        m_ref[h:h + 1, :] = jnp.broadcast_to(m_new, (1, LANE))

        hn = hid * lax.rsqrt(jnp.mean(hid * hid, axis=-1, keepdims=True) + EPS) * ng_ref[:, cols]
        y_ref[:, cols] = (_sigmoid(o_ref[:, cols]) * hn).astype(y_ref.dtype)


def _mlstm(seg_b, if_rows, bias, conv_w, conv_b, norm_g, tri, batch, seq, chunk):
    n = batch * seq
    nc = seq // chunk
    w = MLSTM_WIDTH
    nh, dh = MLSTM_HEADS, MLSTM_HEAD_DIM

    def col_spec(off):
        return pl.BlockSpec((chunk, w), lambda b, c, o=off // w: (b * nc + c, o))

    return pl.pallas_call(
        _mlstm_kernel,
        grid=(batch, nc),
        in_specs=[
            pl.BlockSpec(memory_space=pltpu.SMEM),
            col_spec(B_QK), col_spec(B_QK + w), col_spec(B_V), col_spec(B_O),
            pl.BlockSpec((chunk, LANE), lambda b, c: (b * nc + c, B_IF // LANE)),
            pl.BlockSpec((8, chunk), lambda b, c: (0, b * nc + c)),
            pl.BlockSpec((CONV_WIDTH, 2 * w), lambda b, c: (0, 0)),
            pl.BlockSpec((1, 2 * w), lambda b, c: (0, 0)),
            pl.BlockSpec((1, w), lambda b, c: (0, 0)),
            pl.BlockSpec((chunk, chunk), lambda b, c: (0, 0)),
        ],
        out_specs=pl.BlockSpec((chunk, w), lambda b, c: (b * nc + c, 0)),
        out_shape=jax.ShapeDtypeStruct((n, w), BF16),
        scratch_shapes=[
            pltpu.VMEM((nh, dh, dh), F32),
            pltpu.VMEM((8, dh), F32),
            pltpu.VMEM((8, LANE), F32),
            pltpu.VMEM((8, w), F32),
            pltpu.VMEM((8, w), F32),
        ],
        compiler_params=pltpu.CompilerParams(
            dimension_semantics=("parallel", "arbitrary"), vmem_limit_bytes=VMEM_LIMIT),
        name="mlstm",
    )(bias, seg_b, seg_b, seg_b, seg_b, seg_b, if_rows, conv_w, conv_b, norm_g, tri)


def _outproj_kernel(x_ref, ya_ref, ym_ref, wa_ref, wm_ref, o_ref):
    o_ref[...] = x_ref[...] + _dot(ya_ref[...], wa_ref[...]) + _dot(ym_ref[...], wm_ref[...])


def _outproj(x2, ya, ym, wa, wm, tm):
    n, d = x2.shape
    return pl.pallas_call(
        _outproj_kernel,
        grid=(n // tm,),
        in_specs=[
            pl.BlockSpec((tm, d), lambda i: (i, 0)),
            pl.BlockSpec((tm, ya.shape[1]), lambda i: (i, 0)),
            pl.BlockSpec((tm, ym.shape[1]), lambda i: (i, 0)),
            pl.BlockSpec(wa.shape, lambda i: (0, 0)),
            pl.BlockSpec(wm.shape, lambda i: (0, 0)),
        ],
        out_specs=pl.BlockSpec((tm, d), lambda i: (i, 0)),
        out_shape=jax.ShapeDtypeStruct((n, d), F32),
        compiler_params=pltpu.CompilerParams(
            dimension_semantics=("parallel",), vmem_limit_bytes=VMEM_LIMIT),
        name="outproj",
    )(x2, ya, ym, wa, wm)


def _mlp_kernel(x_ref, g_ref, w1_ref, w2_ref, gf_ref, o_ref, h_ref, acc_ref):
    f = pl.program_id(1)

    @pl.when(f == 0)
    def _():
        x = x_ref[...]
        r = lax.rsqrt(jnp.mean(x * x, axis=-1, keepdims=True) + EPS)
        h_ref[...] = (x * r * g_ref[...]).astype(BF16)
        acc_ref[...] = jnp.zeros(acc_ref.shape, F32)

    u = jnp.maximum(_dot(h_ref[...], w1_ref[...]), 0.0)
    acc_ref[...] += _dot((u * u).astype(BF16), w2_ref[...])

    @pl.when(f == pl.num_programs(1) - 1)
    def _():
        x2 = x_ref[...] + acc_ref[...]
        r = lax.rsqrt(jnp.mean(x2 * x2, axis=-1, keepdims=True) + EPS)
        o_ref[...] = x2 * r * gf_ref[...]


def _mlp(x1, g, w1, w2, gf, tm, tf):
    n, d = x1.shape
    dff = w1.shape[1]
    return pl.pallas_call(
        _mlp_kernel,
        grid=(n // tm, dff // tf),
        in_specs=[
            pl.BlockSpec((tm, d), lambda i, f: (i, 0)),
            pl.BlockSpec((1, d), lambda i, f: (0, 0)),
            pl.BlockSpec((d, tf), lambda i, f: (0, f)),
            pl.BlockSpec((tf, d), lambda i, f: (f, 0)),
            pl.BlockSpec((1, d), lambda i, f: (0, 0)),
        ],
        out_specs=pl.BlockSpec((tm, d), lambda i, f: (i, 0)),
        out_shape=jax.ShapeDtypeStruct((n, d), F32),
        scratch_shapes=[pltpu.VMEM((tm, d), BF16), pltpu.VMEM((tm, d), F32)],
        compiler_params=pltpu.CompilerParams(
            dimension_semantics=("parallel", "arbitrary"), vmem_limit_bytes=VMEM_LIMIT),
        name="mlp",
    )(x1, g, w1, w2, gf)


def _row_tile(n, want):
    t = want
    while n % t:
        t //= 2
    return t


def _layer(x2, batch, seq, norm_mix_g, w_in, w_cmp_k1, w_cmp_k2, pos_cmp_k, w_cmp_v1, w_cmp_v2,
           pos_cmp_v, conv_w, conv_b, b_igate, b_fgate, mlstm_norm_g, w_out, norm_mlp_g,
           w_mlp_in, w_mlp_out):
    n, d = x2.shape
    assert seq % Q_BLOCK == 0 and seq >= WINDOW + Q_BLOCK and seq // SEL_BLOCK <= SEL_LANES
    g_ = NSA_KV_GROUPS
    nh = MLSTM_HEADS

    c_gate = SEG_A
    c_qk = c_gate + NSA_HEADS * N_BRANCH
    c_i = c_qk + 4 * MLSTM_WIDTH
    c_f = c_i + nh
    w_a = w_in[:, :SEG_A].astype(BF16)
    gate_cols = []
    per_g = NSA_REP * N_BRANCH
    for g in range(g_):
        gate_cols += [w_in[:, c_gate + g * per_g:c_gate + (g + 1) * per_g],
                      jnp.zeros((d, LANE - per_g), w_in.dtype)]
    w_b = jnp.concatenate(
        [w_in[:, c_qk:c_i]] + gate_cols
        + [w_in[:, c_i:c_f + nh], jnp.zeros((d, SEG_B - B_IF - 2 * nh), w_in.dtype)],
        axis=1).astype(BF16)
    scale_a = jnp.concatenate([jnp.full((1, NSA_WIDTH), NSA_HEAD_DIM ** -0.5, F32),
                               jnp.ones((1, SEG_A - NSA_WIDTH), F32)], axis=1)
    scale_b = jnp.ones((1, SEG_B), F32)
    g_mix = norm_mix_g.reshape(1, d)

    tm = _row_tile(n, 512)
    seg_a = _inproj(x2, g_mix, w_a, scale_a, BF16, tm, 512)
    seg_b = _inproj(x2, g_mix, w_b, scale_b, F32, tm, 1536)

    n_sub = seq // CMP_STRIDE
    cmp_in = seg_a[:, NSA_WIDTH:NSA_WIDTH + 2 * NSA_KV_WIDTH]
    cmp_in = cmp_in.reshape(batch, n_sub, CMP_STRIDE, 2 * g_, NSA_HEAD_DIM)
    cmp_in = cmp_in.transpose(0, 3, 1, 2, 4).reshape(batch, 2 * g_, n_sub, CMP_STRIDE * NSA_HEAD_DIM)
    w1s = jnp.stack([w_cmp_k1, w_cmp_v1]).astype(BF16)
    w2s = jnp.stack([w_cmp_k2, w_cmp_v2]).astype(BF16)
    poss = jnp.stack([pos_cmp_k, pos_cmp_v]).reshape(2, 1, CMP_BLOCK * NSA_HEAD_DIM)
    poss = jnp.broadcast_to(poss, (2, 8, CMP_BLOCK * NSA_HEAD_DIM)).astype(BF16)
    kvc, kvct = _compress(cmp_in, w1s, w2s, poss)

    cmp_start = np.arange(n_sub) * CMP_STRIDE
    sel_start = np.arange(SEL_LANES) * SEL_BLOCK
    ovt = ((cmp_start[None, :] < sel_start[:, None] + SEL_BLOCK)
           & (cmp_start[None, :] + CMP_BLOCK - 1 >= sel_start[:, None])
           & (np.arange(n_sub)[None, :] < n_sub - CMP_BLOCK // CMP_STRIDE + 1))
    ovt = jnp.asarray(ovt, BF16)
    slopes = jnp.exp2(-8.0 * jnp.arange(1, NSA_HEADS + 1, dtype=F32) / NSA_HEADS)
    ocmp, pen, flags = _nsa_cmp(seg_a, kvc, kvct, ovt, slopes, batch, seq)
    ksel, vselt, kwin, vwint, dbias, ubias, wbias = _nsa_operands(seg_a, slopes, batch, seq)
    y_a = _nsa_attn(seg_a, seg_b, pen, ocmp, flags, ksel, vselt, kwin, vwint, dbias, ubias, wbias,
                    slopes, batch, seq)

    chunk = 256 if seq % 256 == 0 else 128
    if_rows = seg_b[:, B_IF:B_IF + 2 * nh].T
    bias = jnp.concatenate([b_igate, b_fgate]).astype(F32)
    tri = jnp.asarray(np.tril(np.ones((chunk, chunk), np.float32)))
    y_m = _mlstm(seg_b, if_rows, bias, conv_w, conv_b.reshape(1, -1), mlstm_norm_g.reshape(1, -1),
                 tri, batch, seq, chunk)

    w_o = w_out.astype(BF16)
    x1 = _outproj(x2, y_a, y_m, w_o[:NSA_WIDTH], w_o[NSA_WIDTH:], tm)
    return x1, (norm_mlp_g.reshape(1, d), w_mlp_in.astype(BF16), w_mlp_out.astype(BF16))


def kernel(x, norm_mix_g, w_in, w_cmp_k1, w_cmp_k2, pos_cmp_k, w_cmp_v1, w_cmp_v2, pos_cmp_v, conv_w, conv_b, b_igate, b_fgate, mlstm_norm_g, w_out, norm_mlp_g, w_mlp_in, w_mlp_out, norm_f_g):
    batch, seq, d = x.shape
    depth = w_in.shape[0]
    assert depth == 1, "the final RMSNorm is fused into the last layer's channel mixer"
    x2 = x.reshape(batch * seq, d)
    tm = _row_tile(batch * seq, 512)
    for l in range(depth):
        x1, (g_mlp, w1, w2) = _layer(
            x2, batch, seq, norm_mix_g[l], w_in[l], w_cmp_k1[l], w_cmp_k2[l], pos_cmp_k[l],
            w_cmp_v1[l], w_cmp_v2[l], pos_cmp_v[l], conv_w[l], conv_b[l], b_igate[l], b_fgate[l],
            mlstm_norm_g[l], w_out[l], norm_mlp_g[l], w_mlp_in[l], w_mlp_out[l])
        x2 = _mlp(x1, g_mlp, w1, w2, norm_f_g.reshape(1, d), tm, 512)
    return x2.reshape(batch, seq, d)
```

```python
import functools

import numpy as np
import jax
import jax.numpy as jnp
from jax import lax
from jax.experimental import pallas as pl
from jax.experimental.pallas import tpu as pltpu

F32 = jnp.float32
BF16 = jnp.bfloat16

EPS = 1e-6
NEG = -1e30
FORCE_BONUS = 1e4
PICKED = -3e38
MASK_BIG = 1e30

D_MODEL = 2048
NSA_HEAD_DIM = 128
NSA_WIDTH = D_MODEL // 2
NSA_HEADS = NSA_WIDTH // NSA_HEAD_DIM
NSA_REP = 4
NSA_KV_GROUPS = NSA_HEADS // NSA_REP
NSA_KV_WIDTH = NSA_KV_GROUPS * NSA_HEAD_DIM
CMP_BLOCK = 32
CMP_STRIDE = 16
SEL_BLOCK = 64
SEL_TOPK = 16
WINDOW = 512
Q_BLOCK = 128
N_BRANCH = 3
MLSTM_HEAD_DIM = 256
MLSTM_WIDTH = D_MODEL - NSA_WIDTH
MLSTM_HEADS = MLSTM_WIDTH // MLSTM_HEAD_DIM
CONV_WIDTH = 4
D_FF = 4 * D_MODEL

LANE = 128
SEL_LANES = 128
KEY_TILE = 128
SEL_TILES_PER_STEP = 4
WIN_TILES = WINDOW // KEY_TILE
VMEM_LIMIT = 56 * 1024 * 1024

SEG_A = NSA_WIDTH + 6 * NSA_KV_WIDTH
B_QK, B_V, B_O = 0, 2 * MLSTM_WIDTH, 3 * MLSTM_WIDTH
B_GATE = 4 * MLSTM_WIDTH
B_IF = B_GATE + NSA_KV_GROUPS * LANE
SEG_B = 4608

WCOL_PAD, WCOL_HI, WCOL_LO, WCOL_QHI, WCOL_QLO = 0, 1, 2, 3, 4


def _dot(a, b):
    return jnp.dot(a, b, preferred_element_type=F32)


def _dot_nt(a, b):
    return lax.dot_general(a, b, (((1,), (1,)), ((), ())), preferred_element_type=F32)


def _sigmoid(x):
    return 1.0 / (1.0 + jnp.exp(-x))


def _inproj_kernel(x_ref, g_ref, w_ref, cs_ref, o_ref, h_ref):
    @pl.when(pl.program_id(1) == 0)
    def _():
        x = x_ref[...]
        r = lax.rsqrt(jnp.mean(x * x, axis=-1, keepdims=True) + EPS)
        h_ref[...] = (x * r * g_ref[...]).astype(BF16)

    o_ref[...] = (_dot(h_ref[...], w_ref[...]) * cs_ref[...]).astype(o_ref.dtype)


def _inproj(x2, g, w, cs, out_dtype, tm, tn):
    n, d = x2.shape
    nc = w.shape[1]
    return pl.pallas_call(
        _inproj_kernel,
        grid=(n // tm, nc // tn),
        in_specs=[
            pl.BlockSpec((tm, d), lambda i, j: (i, 0)),
            pl.BlockSpec((1, d), lambda i, j: (0, 0)),
            pl.BlockSpec((d, tn), lambda i, j: (0, j)),
            pl.BlockSpec((1, tn), lambda i, j: (0, j)),
        ],
        out_specs=pl.BlockSpec((tm, tn), lambda i, j: (i, j)),
        out_shape=jax.ShapeDtypeStruct((n, nc), out_dtype),
        scratch_shapes=[pltpu.VMEM((tm, d), BF16)],
        compiler_params=pltpu.CompilerParams(
            dimension_semantics=("parallel", "arbitrary"), vmem_limit_bytes=VMEM_LIMIT),
        name="inproj",
    )(x2, g, w, cs)


def _compress_kernel(x_ref, w1_ref, w2_ref, pos_ref, o_ref, ot_ref):
    x = x_ref[0, 0]
    n_sub = x.shape[0]
    half = CMP_STRIDE * NSA_HEAD_DIM
    w1 = w1_ref[0]
    top = _dot(x, w1[:half])
    bot = _dot(x, w1[half:])
    bot = pltpu.roll(bot, n_sub - 1, axis=0)
    posw = _dot(pos_ref[0], w1)[0:1]
    pre = top + bot + posw
    hid = pre * _sigmoid(pre)
    out = _dot(hid.astype(BF16), w2_ref[0])
    o_ref[0, 0] = out.astype(o_ref.dtype)
    ot_ref[0, 0] = out.T.astype(ot_ref.dtype)


def _compress(xs, w1s, w2s, poss):
    b, c, n_sub, feat = xs.shape
    g = NSA_KV_GROUPS
    return pl.pallas_call(
        _compress_kernel,
        grid=(b, c),
        in_specs=[
            pl.BlockSpec((1, 1, n_sub, feat), lambda i, j: (i, j, 0, 0)),
            pl.BlockSpec((1, 2 * feat, NSA_HEAD_DIM), lambda i, j: (j // g, 0, 0)),
            pl.BlockSpec((1, NSA_HEAD_DIM, NSA_HEAD_DIM), lambda i, j: (j // g, 0, 0)),
            pl.BlockSpec((1, 8, 2 * feat), lambda i, j: (j // g, 0, 0)),
        ],
        out_specs=[pl.BlockSpec((1, 1, n_sub, NSA_HEAD_DIM), lambda i, j: (i, j, 0, 0)),
                   pl.BlockSpec((1, 1, NSA_HEAD_DIM, n_sub), lambda i, j: (i, j, 0, 0))],
        out_shape=[jax.ShapeDtypeStruct((b, c, n_sub, NSA_HEAD_DIM), BF16),
                   jax.ShapeDtypeStruct((b, c, NSA_HEAD_DIM, n_sub), BF16)],
        compiler_params=pltpu.CompilerParams(
            dimension_semantics=("parallel", "parallel"), vmem_limit_bytes=VMEM_LIMIT),
        name="compress",
    )(xs, w1s, w2s, poss)


def _stack_heads(q_all):
    dh = NSA_HEAD_DIM
    return jnp.concatenate([q_all[:, r * dh:(r + 1) * dh] for r in range(NSA_REP)], axis=0)


def _nsa_cmp_kernel(slopes_ref, q_ref, kc_ref, vct_ref, ovt_ref, ocmp_ref, pen_ref, flag_ref, *, seq):
    g = pl.program_id(1)
    qb = pl.program_id(2)
    nq = Q_BLOCK
    n_cpad = kc_ref.shape[2]
    n_cmp = seq // CMP_STRIDE - CMP_BLOCK // CMP_STRIDE + 1
    t0 = qb * nq

    s_t = _dot_nt(kc_ref[0, 0], _stack_heads(q_ref[...]))
    n_s = lax.broadcasted_iota(jnp.int32, (n_cpad, 1), 0)
    q_l = lax.broadcasted_iota(jnp.int32, (1, nq), 1)
    dist = (t0 - (CMP_BLOCK - 1)) + q_l - n_s * CMP_STRIDE
    real = n_s < n_cmp
    valid = (dist >= 0) & real
    dist_f = dist.astype(F32)
    probs = []
    p_sum = jnp.zeros((n_cpad, nq), F32)
    for r in range(NSA_REP):
        slope = slopes_ref[g * NSA_REP + r]
        s = jnp.where(valid, s_t[:, r * nq:(r + 1) * nq] - slope * dist_f, NEG)
        e = jnp.exp(s - jnp.max(s, axis=0, keepdims=True))
        denom = jnp.sum(jnp.where(real, e, 0.0), axis=0, keepdims=True)
        p = jnp.where(valid, e, 0.0) / denom
        probs.append(p.astype(BF16))
        p_sum = p_sum + p
    ocmp_ref[0, 0, 0] = _dot(vct_ref[0, 0], jnp.concatenate(probs, axis=1))

    p_hi = p_sum.astype(BF16)
    p_lo = (p_sum - p_hi.astype(F32)).astype(BF16)
    ovt = ovt_ref[...]
    imp = _dot(ovt, p_hi) + _dot(ovt, p_lo)
    j_i = lax.broadcasted_iota(jnp.int32, (SEL_LANES, 1), 0)
    t_l = t0 + q_l
    cur = t_l // SEL_BLOCK
    forced = (j_i == 0) | (j_i == cur) | (j_i == cur - 1)
    causal_blk = j_i * SEL_BLOCK <= t_l
    val = jnp.where(causal_blk, jnp.where(forced, imp + FORCE_BONUS, imp), NEG)
    j_f = j_i.astype(F32)
    sel_t = jnp.zeros((SEL_LANES, nq), F32)
    for _ in range(min(SEL_TOPK, seq // SEL_BLOCK)):
        mx = jnp.max(val, axis=0, keepdims=True)
        first = jnp.min(jnp.where(val == mx, j_f, float(SEL_LANES)), axis=0, keepdims=True)
        pick = j_f == first
        sel_t = jnp.where(pick, 1.0, sel_t)
        val = jnp.where(pick, PICKED, val)
    sel = sel_t.T
    pen_ref[0, 0] = ((sel - 1.0) * MASK_BIG).astype(pen_ref.dtype)
    flag_ref[0, 0, 0] = (jnp.max(sel, axis=0, keepdims=True) > 0.0).astype(jnp.int32)


def _nsa_cmp(seg_a, kvc, kvct, ovt, slopes, batch, seq):
    nqb = seq // Q_BLOCK
    gq = NSA_REP * NSA_HEAD_DIM
    g_ = NSA_KV_GROUPS
    n_cpad = kvc.shape[2]
    return pl.pallas_call(
        functools.partial(_nsa_cmp_kernel, seq=seq),
        grid=(batch, g_, nqb),
        in_specs=[
            pl.BlockSpec(memory_space=pltpu.SMEM),
            pl.BlockSpec((Q_BLOCK, gq), lambda b, g, q: (b * nqb + q, g)),
            pl.BlockSpec((1, 1, n_cpad, NSA_HEAD_DIM), lambda b, g, q: (b, g, 0, 0)),
            pl.BlockSpec((1, 1, NSA_HEAD_DIM, n_cpad), lambda b, g, q: (b, g_ + g, 0, 0)),
            pl.BlockSpec((SEL_LANES, n_cpad), lambda b, g, q: (0, 0)),
        ],
        out_specs=[
            pl.BlockSpec((1, 1, 1, NSA_HEAD_DIM, gq), lambda b, g, q: (b, g, q, 0, 0)),
            pl.BlockSpec((1, 1, Q_BLOCK, SEL_LANES), lambda b, g, q: (b, g, q, 0)),
            pl.BlockSpec((1, 1, 1, 1, SEL_LANES), lambda b, g, q: (b, g, q, 0, 0)),
        ],
        out_shape=[
            jax.ShapeDtypeStruct((batch, g_, nqb, NSA_HEAD_DIM, gq), F32),
            jax.ShapeDtypeStruct((batch, g_, seq, SEL_LANES), BF16),
            jax.ShapeDtypeStruct((batch, g_, nqb, 1, SEL_LANES), jnp.int32),
        ],
        compiler_params=pltpu.CompilerParams(
            dimension_semantics=("parallel", "parallel", "parallel"), vmem_limit_bytes=VMEM_LIMIT),
        name="nsa_cmp",
    )(slopes, seg_a, kvc, kvct, ovt)


def _nsa_attn_kernel(slopes_ref, flag_ref, q_ref, gate_ref, pen_ref, ocmp_ref, ksel_ref, vselt_ref,
                     kwin_ref, vwint_ref, dbias_ref, ubias_ref, wbias_ref, o_ref,
                     qa_ref, qw_ref, list_ref, m_ref, l_ref, acc_ref, owin_ref, sbuf_ref, *, seq):
    g = pl.program_id(1)
    qb = pl.program_id(2)
    dh = NSA_HEAD_DIM
    nq = Q_BLOCK
    kt = KEY_TILE
    per = SEL_TILES_PER_STEP
    pad_tile = seq // kt
    t0 = qb * nq
    slopes = [slopes_ref[g * NSA_REP + r] for r in range(NSA_REP)]
    row_q = lax.broadcasted_iota(jnp.int32, (nq, 1), 0)
    lane = lax.broadcasted_iota(jnp.int32, (1, LANE), 1)
    q_all = q_ref[...]

    pen = pen_ref[0, 0].astype(F32)
    blk_rel = ((lane - (t0 + row_q) // SEL_BLOCK) * SEL_BLOCK).astype(F32)
    for r in range(NSA_REP):
        qa_ref[r * nq:(r + 1) * nq, :dh] = q_all[:, r * dh:(r + 1) * dh]
        qa_ref[r * nq:(r + 1) * nq, dh:] = (pen + slopes[r] * blk_rel).astype(BF16)

    def scan(i, cnt):
        act = (flag_ref[0, 0, 0, 0, 2 * i] + flag_ref[0, 0, 0, 0, 2 * i + 1]) > 0

        @pl.when(act)
        def _():
            list_ref[cnt] = i

        return cnt + act.astype(jnp.int32)

    cnt = lax.fori_loop(0, qb, scan, 0)
    for i in range(2 * per):
        list_ref[cnt + i] = pad_tile

    def group_scores(it):
        keys = jnp.concatenate(
            [ksel_ref[0, 0, pl.ds(pl.multiple_of(list_ref[it * per + i] * kt, kt), kt), :]
             for i in range(per)], axis=0)
        return _dot_nt(keys, qa_ref[...]) + ubias_ref[0]

    s = _dot_nt(ksel_ref[0, 0, pl.ds(pl.multiple_of(t0, kt), kt), :], qa_ref[...]) + dbias_ref[0]
    m0 = jnp.max(s, axis=0, keepdims=True)
    p = jnp.exp(s - m0)
    m_ref[...] = m0
    l_ref[...] = jnp.sum(p, axis=0, keepdims=True)
    acc_ref[...] = _dot(vselt_ref[0, 0, qb], p.astype(BF16))
    sbuf_ref[0] = group_scores(0)

    tp = t0 + WINDOW + row_q
    t_hi = (tp // SEL_BLOCK).astype(F32)
    t_lo = (tp % SEL_BLOCK).astype(F32)
    for r in range(NSA_REP):
        sl = slopes[r]
        ext = jnp.where(lane == WCOL_PAD, -MASK_BIG, 0.0)
        ext = jnp.where(lane == WCOL_HI, sl * SEL_BLOCK, ext)
        ext = jnp.where(lane == WCOL_LO, sl, ext)
        ext = jnp.where(lane == WCOL_QHI, -sl * SEL_BLOCK * t_hi, ext)
        ext = jnp.where(lane == WCOL_QLO, -sl * t_lo, ext)
        qw_ref[r * nq:(r + 1) * nq, :dh] = q_all[:, r * dh:(r + 1) * dh]
        qw_ref[r * nq:(r + 1) * nq, dh:] = ext.astype(BF16)
    wlen = WINDOW + nq
    s = _dot_nt(kwin_ref[0, 0, pl.ds(pl.multiple_of(t0, kt), wlen), :], qw_ref[...]) + wbias_ref[...]
    e = jnp.exp(s - jnp.max(s, axis=0, keepdims=True))
    vwin = jnp.concatenate([vwint_ref[0, 0, qb + i] for i in range(WIN_TILES + 1)], axis=1)
    owin_ref[...] = _dot(vwin, e.astype(BF16)) / jnp.sum(e, axis=0, keepdims=True)

    def sel_body(it, carry):
        slot = it % 2
        s = sbuf_ref[slot]
        sbuf_ref[1 - slot] = group_scores(it + 1)
        vals = jnp.concatenate([vselt_ref[0, 0, list_ref[it * per + i]] for i in range(per)], axis=1)
        m_old = m_ref[...]
        m_new = jnp.maximum(m_old, jnp.max(s, axis=0, keepdims=True))
        alpha = jnp.exp(m_old - m_new)
        p = jnp.exp(s - m_new)
        l_ref[...] = alpha * l_ref[...] + jnp.sum(p, axis=0, keepdims=True)
        acc_ref[...] = alpha * acc_ref[...] + _dot(vals, p.astype(BF16))
        m_ref[...] = m_new
        return carry

    lax.fori_loop(0, (cnt + per - 1) // per, sel_body, 0)
    o_sel = acc_ref[...] / l_ref[...]
    o_win = owin_ref[...]

    gate_t = _sigmoid(gate_ref[...]).T
    o_cmp = ocmp_ref[0, 0, 0]
    for r in range(NSA_REP):
        cols = slice(r * nq, (r + 1) * nq)
        c0 = N_BRANCH * r
        out_t = (gate_t[c0:c0 + 1] * o_cmp[:, cols] + gate_t[c0 + 1:c0 + 2] * o_sel[:, cols]
                 + gate_t[c0 + 2:c0 + 3] * o_win[:, cols])
        o_ref[:, r * dh:(r + 1) * dh] = out_t.T.astype(o_ref.dtype)


def _nsa_attn(seg_a, seg_b, pen, ocmp, flags, ksel, vselt, kwin, vwint, dbias, ubias, wbias,
              slopes, batch, seq):
    n = batch * seq
    nqb = seq // Q_BLOCK
    gq = NSA_REP * NSA_HEAD_DIM
    g_ = NSA_KV_GROUPS
    dk = ksel.shape[-1]
    rq = NSA_REP * Q_BLOCK

    def whole(arr):
        shp = (1, 1) + arr.shape[2:]
        return pl.BlockSpec(shp, lambda b, g, q, nd=len(shp): (b, g) + (0,) * (nd - 2))

    return pl.pallas_call(
        functools.partial(_nsa_attn_kernel, seq=seq),
        grid=(batch, g_, nqb),
        in_specs=[
            pl.BlockSpec(memory_space=pltpu.SMEM),
            pl.BlockSpec((1, 1, 1, 1, SEL_LANES), lambda b, g, q: (b, g, q, 0, 0),
                         memory_space=pltpu.SMEM),
            pl.BlockSpec((Q_BLOCK, gq), lambda b, g, q: (b * nqb + q, g)),
            pl.BlockSpec((Q_BLOCK, LANE), lambda b, g, q: (b * nqb + q, B_GATE // LANE + g)),
            pl.BlockSpec((1, 1, Q_BLOCK, SEL_LANES), lambda b, g, q: (b, g, q, 0)),
            pl.BlockSpec((1, 1, 1, NSA_HEAD_DIM, gq), lambda b, g, q: (b, g, q, 0, 0)),
            whole(ksel), whole(vselt), whole(kwin), whole(vwint),
            pl.BlockSpec((1,) + dbias.shape[1:], lambda b, g, q: (g, 0, 0)),
            pl.BlockSpec((1,) + ubias.shape[1:], lambda b, g, q: (g, 0, 0)),
            pl.BlockSpec(wbias.shape, lambda b, g, q: (0, 0)),
        ],
        out_specs=pl.BlockSpec((Q_BLOCK, gq), lambda b, g, q: (b * nqb + q, g)),
        out_shape=jax.ShapeDtypeStruct((n, NSA_WIDTH), BF16),
        scratch_shapes=[
            pltpu.VMEM((rq, dk), BF16),
            pltpu.VMEM((rq, dk), BF16),
            pltpu.SMEM((seq // KEY_TILE + 2 * SEL_TILES_PER_STEP,), jnp.int32),
            pltpu.VMEM((1, rq), F32),
            pltpu.VMEM((1, rq), F32),
            pltpu.VMEM((NSA_HEAD_DIM, rq), F32),
            pltpu.VMEM((NSA_HEAD_DIM, rq), F32),
            pltpu.VMEM((2, SEL_TILES_PER_STEP * KEY_TILE, rq), F32),
        ],
        compiler_params=pltpu.CompilerParams(
            dimension_semantics=("parallel", "parallel", "arbitrary"), vmem_limit_bytes=VMEM_LIMIT),
        name="nsa_attn",
    )(slopes, flags, seg_a, seg_b, pen, ocmp, ksel, vselt, kwin, vwint, dbias, ubias, wbias)


def _nsa_operands(seg_a, slopes, batch, seq):
    g_, dh, kt = NSA_KV_GROUPS, NSA_HEAD_DIM, KEY_TILE
    n_t = seq // kt
    a3 = seg_a.reshape(batch, seq, SEG_A)

    def grouped(kind):
        c0 = NSA_WIDTH + kind * NSA_KV_WIDTH
        return a3[:, :, c0:c0 + NSA_KV_WIDTH].reshape(batch, seq, g_, dh).transpose(0, 2, 1, 3)

    def tiles_t(v, front):
        vt = v.reshape(batch, g_, n_t, kt, dh).transpose(0, 1, 2, 4, 3)
        return jnp.pad(vt, ((0, 0), (0, 0), (front, 1 - min(front, 1)), (0, 0), (0, 0)))

    pos = np.arange(seq)
    onehot = jnp.asarray(pos[:, None] // SEL_BLOCK == np.arange(SEL_LANES)[None, :], BF16)
    ksel = jnp.concatenate([grouped(2), jnp.broadcast_to(onehot, (batch, g_, seq, SEL_LANES))], axis=-1)
    pad_rows = jnp.concatenate([jnp.zeros((kt, dh), BF16), jnp.ones((kt, SEL_LANES), BF16)], axis=-1)
    ksel = jnp.concatenate([ksel, jnp.broadcast_to(pad_rows, (batch, g_, kt, dh + SEL_LANES))], axis=2)
    vselt = tiles_t(grouped(3), 0)

    ext = np.zeros((seq, LANE), np.float32)
    ext[:, WCOL_HI] = (pos + WINDOW) // SEL_BLOCK
    ext[:, WCOL_LO] = (pos + WINDOW) % SEL_BLOCK
    ext[:, WCOL_QHI] = 1.0
    ext[:, WCOL_QLO] = 1.0
    kwin = jnp.concatenate([grouped(4), jnp.broadcast_to(jnp.asarray(ext, BF16), (batch, g_, seq, LANE))],
                           axis=-1)
    front = np.zeros((WINDOW, dh + LANE), np.float32)
    front[:, dh + WCOL_PAD] = 1.0
    kwin = jnp.concatenate([jnp.broadcast_to(jnp.asarray(front, BF16), (batch, g_, WINDOW, dh + LANE)),
                            kwin], axis=2)
    vwint = tiles_t(grouped(5), WIN_TILES)

    rows = SEL_TILES_PER_STEP * kt
    u = jnp.asarray((np.arange(rows) % SEL_BLOCK).astype(np.float32))[None, :, None, None]
    ubias = jnp.broadcast_to(slopes.reshape(g_, 1, NSA_REP, 1) * u, (g_, rows, NSA_REP, Q_BLOCK))
    ubias = ubias.reshape(g_, rows, NSA_REP * Q_BLOCK)
    kq = np.arange(kt)[:, None] <= np.arange(Q_BLOCK)[None, :]
    causal = np.tile(np.where(kq, 0.0, NEG).astype(np.float32), (1, NSA_REP))
    dbias = ubias[:, :kt] + jnp.asarray(causal)[None]
    ki = np.arange(WINDOW + Q_BLOCK)[:, None]
    qi = np.arange(Q_BLOCK)[None, :]
    band = np.where((ki > qi) & (ki <= qi + WINDOW), 0.0, NEG).astype(np.float32)
    wbias = jnp.asarray(np.tile(band, (1, NSA_REP)))
    return ksel, vselt, kwin, vwint, dbias, ubias, wbias


def _shifted(x, tail, s):
    xs = pltpu.roll(x, s, axis=0)
    ts = pltpu.roll(tail, s, axis=0)
    row8 = lax.broadcasted_iota(jnp.int32, (8, 1), 0)
    head = jnp.where(row8 < s, ts, xs[:8])
    return jnp.concatenate([head, xs[8:]], axis=0)


def _conv_silu(x, tail, w, b):
    y = b + _shifted(x, tail, CONV_WIDTH - 1) * w[0:1]
    for i in range(1, CONV_WIDTH - 1):
        y = y + _shifted(x, tail, CONV_WIDTH - 1 - i) * w[i:i + 1]
    y = y + x * w[CONV_WIDTH - 1:CONV_WIDTH]
    return y * _sigmoid(y)


def _log_sigmoid(x):
    return jnp.minimum(x, 0.0) - jnp.log(1.0 + jnp.exp(-jnp.abs(x)))


def _mlstm_kernel(bias_ref, q_ref, k_ref, v_ref, o_ref, ifc_ref, ifr_ref, cw_ref, cb_ref, ng_ref,
                  tri_ref, y_ref, c_ref, n_ref, m_ref, qtail_ref, ktail_ref):
    ch = pl.program_id(1)
    nh, dh = MLSTM_HEADS, MLSTM_HEAD_DIM
    L = q_ref.shape[0]

    @pl.when(ch == 0)
    def _():
        c_ref[...] = jnp.zeros(c_ref.shape, F32)
        n_ref[...] = jnp.zeros(n_ref.shape, F32)
        m_ref[...] = jnp.zeros(m_ref.shape, F32)
        qtail_ref[...] = jnp.zeros(qtail_ref.shape, F32)
        ktail_ref[...] = jnp.zeros(ktail_ref.shape, F32)

    hi = lax.Precision.HIGHEST
    tri = tri_ref[...]
    lane8 = lax.broadcasted_iota(jnp.int32, (1, LANE), 1)
    bias_c = jnp.zeros((1, LANE), F32)
    for h in range(nh):
        bias_c = jnp.where(lane8 == h, bias_ref[h], bias_c)
        bias_c = jnp.where(lane8 == nh + h, bias_ref[nh + h], bias_c)
    pre_c = ifc_ref[...] + bias_c
    cum_c = jnp.dot(tri, _log_sigmoid(pre_c), precision=hi, preferred_element_type=F32)
    row8 = lax.broadcasted_iota(jnp.int32, (8, 1), 0)
    bias_r = jnp.zeros((8, 1), F32)
    for h in range(2 * nh):
        bias_r = jnp.where(row8 == h, bias_ref[h], bias_r)
    pre_r = ifr_ref[...] + bias_r
    cum_r = lax.dot_general(_log_sigmoid(pre_r), tri, (((1,), (1,)), ((), ())),
                            precision=hi, preferred_element_type=F32)

    rr = lax.broadcasted_iota(jnp.int32, (L, 1), 0)
    cc = lax.broadcasted_iota(jnp.int32, (1, L), 1)
    causal = cc <= rr

    q_raw = q_ref[...]
    k_raw = k_ref[...]
    cw = cw_ref[...]
    cb = cb_ref[...]
    qc = _conv_silu(q_raw, qtail_ref[...], cw[:, :nh * dh], cb[:, :nh * dh])
    kc = _conv_silu(k_raw, ktail_ref[...], cw[:, nh * dh:], cb[:, nh * dh:]) * (dh ** -0.5)
    qtail_ref[...] = q_raw[L - 8:]
    ktail_ref[...] = k_raw[L - 8:]

    for h in range(nh):
        cols = slice(h * dh, (h + 1) * dh)
        qh = qc[:, cols]
        kh = kc[:, cols]
        vh = v_ref[:, cols].astype(BF16)
        qb = qh.astype(BF16)
        b_c = cum_c[:, nh + h:nh + h + 1]
        li_c = pre_c[:, h:h + 1]
        b_r = cum_r[nh + h:nh + h + 1, :]
        li_r = pre_r[h:h + 1, :]
        m_prev = m_ref[h:h + 1, 0:1]

        dmat = jnp.where(causal, b_c - b_r + li_r, NEG)
        a = b_c + m_prev
        m_j = jnp.maximum(a, jnp.max(dmat, axis=1, keepdims=True))
        w_intra = jnp.exp(dmat - m_j)
        w_inter = jnp.exp(a - m_j)
        sc = _dot_nt(qb, kh.astype(BF16)) * w_intra
        c_old = c_ref[h]
        n_old = n_ref[h:h + 1, :]
        num = w_inter * _dot(qb, c_old.astype(BF16)) + _dot(sc.astype(BF16), vh)
        den = (w_inter * jnp.sum(qh * n_old, axis=1, keepdims=True)
               + jnp.sum(sc, axis=1, keepdims=True))
        hid = num / jnp.maximum(jnp.abs(den), jnp.exp(-m_j))

        g_tot = b_r[:, L - 1:L]
        lw_c = g_tot - b_c + li_c
        lw_r = g_tot - b_r + li_r
        m_new = jnp.maximum(g_tot + m_prev, jnp.max(lw_r, axis=1, keepdims=True))
        decay = jnp.exp(g_tot + m_prev - m_new)
        kw = jnp.exp(lw_c - m_new) * kh
        c_ref[h] = decay * c_old + _dot(kw.T.astype(BF16), vh)
        n_ref[h:h + 1, :] = decay * n_old + jnp.sum(kw, axis=0, keepdims=True)
        m_ref[h:h + 1, :] = jnp.broadcast_to(m_new, (1, LANE))

        hn = hid * lax.rsqrt(jnp.mean(hid * hid, axis=-1, keepdims=True) + EPS) * ng_ref[:, cols]
        y_ref[:, cols] = (_sigmoid(o_ref[:, cols]) * hn).astype(y_ref.dtype)


def _mlstm(seg_b, if_rows, bias, conv_w, conv_b, norm_g, tri, batch, seq, chunk):
    n = batch * seq
    nc = seq // chunk
    w = MLSTM_WIDTH
    nh, dh = MLSTM_HEADS, MLSTM_HEAD_DIM

    def col_spec(off):
        return pl.BlockSpec((chunk, w), lambda b, c, o=off // w: (b * nc + c, o))

    return pl.pallas_call(
        _mlstm_kernel,
        grid=(batch, nc),
        in_specs=[
            pl.BlockSpec(memory_space=pltpu.SMEM),
            col_spec(B_QK), col_spec(B_QK + w), col_spec(B_V), col_spec(B_O),
            pl.BlockSpec((chunk, LANE), lambda b, c: (b * nc + c, B_IF // LANE)),
            pl.BlockSpec((8, chunk), lambda b, c: (0, b * nc + c)),
            pl.BlockSpec((CONV_WIDTH, 2 * w), lambda b, c: (0, 0)),
            pl.BlockSpec((1, 2 * w), lambda b, c: (0, 0)),
            pl.BlockSpec((1, w), lambda b, c: (0, 0)),
            pl.BlockSpec((chunk, chunk), lambda b, c: (0, 0)),
        ],
        out_specs=pl.BlockSpec((chunk, w), lambda b, c: (b * nc + c, 0)),
        out_shape=jax.ShapeDtypeStruct((n, w), BF16),
        scratch_shapes=[
            pltpu.VMEM((nh, dh, dh), F32),
            pltpu.VMEM((8, dh), F32),
            pltpu.VMEM((8, LANE), F32),
            pltpu.VMEM((8, w), F32),
            pltpu.VMEM((8, w), F32),
        ],
        compiler_params=pltpu.CompilerParams(
            dimension_semantics=("parallel", "arbitrary"), vmem_limit_bytes=VMEM_LIMIT),
        name="mlstm",
    )(bias, seg_b, seg_b, seg_b, seg_b, seg_b, if_rows, conv_w, conv_b, norm_g, tri)


def _outproj_kernel(x_ref, ya_ref, ym_ref, wa_ref, wm_ref, o_ref):
    o_ref[...] = x_ref[...] + _dot(ya_ref[...], wa_ref[...]) + _dot(ym_ref[...], wm_ref[...])


def _outproj(x2, ya, ym, wa, wm, tm):
    n, d = x2.shape
    return pl.pallas_call(
        _outproj_kernel,
        grid=(n // tm,),
        in_specs=[
            pl.BlockSpec((tm, d), lambda i: (i, 0)),
            pl.BlockSpec((tm, ya.shape[1]), lambda i: (i, 0)),
            pl.BlockSpec((tm, ym.shape[1]), lambda i: (i, 0)),
            pl.BlockSpec(wa.shape, lambda i: (0, 0)),
            pl.BlockSpec(wm.shape, lambda i: (0, 0)),
        ],
        out_specs=pl.BlockSpec((tm, d), lambda i: (i, 0)),
        out_shape=jax.ShapeDtypeStruct((n, d), F32),
        compiler_params=pltpu.CompilerParams(
            dimension_semantics=("parallel",), vmem_limit_bytes=VMEM_LIMIT),
        name="outproj",
    )(x2, ya, ym, wa, wm)


def _mlp_kernel(x_ref, g_ref, w1_ref, w2_ref, gf_ref, o_ref, h_ref, acc_ref):
    f = pl.program_id(1)

    @pl.when(f == 0)
    def _():
        x = x_ref[...]
        r = lax.rsqrt(jnp.mean(x * x, axis=-1, keepdims=True) + EPS)
        h_ref[...] = (x * r * g_ref[...]).astype(BF16)
        acc_ref[...] = jnp.zeros(acc_ref.shape, F32)

    u = jnp.maximum(_dot(h_ref[...], w1_ref[...]), 0.0)
    acc_ref[...] += _dot((u * u).astype(BF16), w2_ref[...])

    @pl.when(f == pl.num_programs(1) - 1)
    def _():
        x2 = x_ref[...] + acc_ref[...]
        r = lax.rsqrt(jnp.mean(x2 * x2, axis=-1, keepdims=True) + EPS)
        o_ref[...] = x2 * r * gf_ref[...]


def _mlp(x1, g, w1, w2, gf, tm, tf):
    n, d = x1.shape
    dff = w1.shape[1]
    return pl.pallas_call(
        _mlp_kernel,
        grid=(n // tm, dff // tf),
        in_specs=[
            pl.BlockSpec((tm, d), lambda i, f: (i, 0)),
            pl.BlockSpec((1, d), lambda i, f: (0, 0)),
            pl.BlockSpec((d, tf), lambda i, f: (0, f)),
            pl.BlockSpec((tf, d), lambda i, f: (f, 0)),
            pl.BlockSpec((1, d), lambda i, f: (0, 0)),
        ],
        out_specs=pl.BlockSpec((tm, d), lambda i, f: (i, 0)),
        out_shape=jax.ShapeDtypeStruct((n, d), F32),
        scratch_shapes=[pltpu.VMEM((tm, d), BF16), pltpu.VMEM((tm, d), F32)],
        compiler_params=pltpu.CompilerParams(
            dimension_semantics=("parallel", "arbitrary"), vmem_limit_bytes=VMEM_LIMIT),
        name="mlp",
    )(x1, g, w1, w2, gf)


def _row_tile(n, want):
    t = want
    while n % t:
        t //= 2
    return t


def _layer(x2, batch, seq, norm_mix_g, w_in, w_cmp_k1, w_cmp_k2, pos_cmp_k, w_cmp_v1, w_cmp_v2,
           pos_cmp_v, conv_w, conv_b, b_igate, b_fgate, mlstm_norm_g, w_out, norm_mlp_g,
           w_mlp_in, w_mlp_out):
    n, d = x2.shape
    assert seq % Q_BLOCK == 0 and seq >= WINDOW + Q_BLOCK and seq // SEL_BLOCK <= SEL_LANES
    g_ = NSA_KV_GROUPS
    nh = MLSTM_HEADS

    c_gate = SEG_A
    c_qk = c_gate + NSA_HEADS * N_BRANCH
    c_i = c_qk + 4 * MLSTM_WIDTH
    c_f = c_i + nh
    w_a = w_in[:, :SEG_A].astype(BF16)
    gate_cols = []
    per_g = NSA_REP * N_BRANCH
    for g in range(g_):
        gate_cols += [w_in[:, c_gate + g * per_g:c_gate + (g + 1) * per_g],
                      jnp.zeros((d, LANE - per_g), w_in.dtype)]
    w_b = jnp.concatenate(
        [w_in[:, c_qk:c_i]] + gate_cols
        + [w_in[:, c_i:c_f + nh], jnp.zeros((d, SEG_B - B_IF - 2 * nh), w_in.dtype)],
        axis=1).astype(BF16)
    scale_a = jnp.concatenate([jnp.full((1, NSA_WIDTH), NSA_HEAD_DIM ** -0.5, F32),
                               jnp.ones((1, SEG_A - NSA_WIDTH), F32)], axis=1)
    scale_b = jnp.ones((1, SEG_B), F32)
    g_mix = norm_mix_g.reshape(1, d)

    tm = _row_tile(n, 512)
    seg_a = _inproj(x2, g_mix, w_a, scale_a, BF16, tm, 512)
    seg_b = _inproj(x2, g_mix, w_b, scale_b, F32, tm, 1536)

    n_sub = seq // CMP_STRIDE
    cmp_in = seg_a[:, NSA_WIDTH:NSA_WIDTH + 2 * NSA_KV_WIDTH]
    cmp_in = cmp_in.reshape(batch, n_sub, CMP_STRIDE, 2 * g_, NSA_HEAD_DIM)
    cmp_in = cmp_in.transpose(0, 3, 1, 2, 4).reshape(batch, 2 * g_, n_sub, CMP_STRIDE * NSA_HEAD_DIM)
    w1s = jnp.stack([w_cmp_k1, w_cmp_v1]).astype(BF16)
    w2s = jnp.stack([w_cmp_k2, w_cmp_v2]).astype(BF16)
    poss = jnp.stack([pos_cmp_k, pos_cmp_v]).reshape(2, 1, CMP_BLOCK * NSA_HEAD_DIM)
    poss = jnp.broadcast_to(poss, (2, 8, CMP_BLOCK * NSA_HEAD_DIM)).astype(BF16)
    kvc, kvct = _compress(cmp_in, w1s, w2s, poss)

    cmp_start = np.arange(n_sub) * CMP_STRIDE
    sel_start = np.arange(SEL_LANES) * SEL_BLOCK
    ovt = ((cmp_start[None, :] < sel_start[:, None] + SEL_BLOCK)
           & (cmp_start[None, :] + CMP_BLOCK - 1 >= sel_start[:, None])
           & (np.arange(n_sub)[None, :] < n_sub - CMP_BLOCK // CMP_STRIDE + 1))
    ovt = jnp.asarray(ovt, BF16)
    slopes = jnp.exp2(-8.0 * jnp.arange(1, NSA_HEADS + 1, dtype=F32) / NSA_HEADS)
    ocmp, pen, flags = _nsa_cmp(seg_a, kvc, kvct, ovt, slopes, batch, seq)
    ksel, vselt, kwin, vwint, dbias, ubias, wbias = _nsa_operands(seg_a, slopes, batch, seq)
    y_a = _nsa_attn(seg_a, seg_b, pen, ocmp, flags, ksel, vselt, kwin, vwint, dbias, ubias, wbias,
                    slopes, batch, seq)

    chunk = 256 if seq % 256 == 0 else 128
    if_rows = seg_b[:, B_IF:B_IF + 2 * nh].T
    bias = jnp.concatenate([b_igate, b_fgate]).astype(F32)
    tri = jnp.asarray(np.tril(np.ones((chunk, chunk), np.float32)))
    y_m = _mlstm(seg_b, if_rows, bias, conv_w, conv_b.reshape(1, -1), mlstm_norm_g.reshape(1, -1),
                 tri, batch, seq, chunk)

    w_o = w_out.astype(BF16)
    x1 = _outproj(x2, y_a, y_m, w_o[:NSA_WIDTH], w_o[NSA_WIDTH:], tm)
    return x1, (norm_mlp_g.reshape(1, d), w_mlp_in.astype(BF16), w_mlp_out.astype(BF16))


def kernel(x, norm_mix_g, w_in, w_cmp_k1, w_cmp_k2, pos_cmp_k, w_cmp_v1, w_cmp_v2, pos_cmp_v, conv_w, conv_b, b_igate, b_fgate, mlstm_norm_g, w_out, norm_mlp_g, w_mlp_in, w_mlp_out, norm_f_g):
    batch, seq, d = x.shape
    depth = w_in.shape[0]
    assert depth == 1, "the final RMSNorm is fused into the last layer's channel mixer"
    x2 = x.reshape(batch * seq, d)
    tm = _row_tile(batch * seq, 512)
    for l in range(depth):
        x1, (g_mlp, w1, w2) = _layer(
            x2, batch, seq, norm_mix_g[l], w_in[l], w_cmp_k1[l], w_cmp_k2[l], pos_cmp_k[l],
            w_cmp_v1[l], w_cmp_v2[l], pos_cmp_v[l], conv_w[l], conv_b[l], b_igate[l], b_fgate[l],
            mlstm_norm_g[l], w_out[l], norm_mlp_g[l], w_mlp_in[l], w_mlp_out[l])
        x2 = _mlp(x1, g_mlp, w1, w2, norm_f_g.reshape(1, d), tm, 512)
    return x2.reshape(batch, seq, d)
```

```python
import functools

import numpy as np
import jax
import jax.numpy as jnp
from jax import lax
from jax.experimental import pallas as pl
from jax.experimental.pallas import tpu as pltpu

F32 = jnp.float32
BF16 = jnp.bfloat16

EPS = 1e-6
NEG = -1e30
FORCE_BONUS = 1e4
PICKED = -3e38
MASK_BIG = 1e30

D_MODEL = 2048
NSA_HEAD_DIM = 128
NSA_WIDTH = D_MODEL // 2
NSA_HEADS = NSA_WIDTH // NSA_HEAD_DIM
NSA_REP = 4
NSA_KV_GROUPS = NSA_HEADS // NSA_REP
NSA_KV_WIDTH = NSA_KV_GROUPS * NSA_HEAD_DIM
CMP_BLOCK = 32
CMP_STRIDE = 16
SEL_BLOCK = 64
SEL_TOPK = 16
WINDOW = 512
Q_BLOCK = 128
N_BRANCH = 3
MLSTM_HEAD_DIM = 256
MLSTM_WIDTH = D_MODEL - NSA_WIDTH
MLSTM_HEADS = MLSTM_WIDTH // MLSTM_HEAD_DIM
CONV_WIDTH = 4
D_FF = 4 * D_MODEL

LANE = 128
SEL_LANES = 128
KEY_TILE = 128
SEL_TILES_PER_STEP = 4
WIN_TILES = WINDOW // KEY_TILE
VMEM_LIMIT = 56 * 1024 * 1024
ROW_TILE = 1024

SEG_A = NSA_WIDTH + 6 * NSA_KV_WIDTH
B_QK, B_V, B_O = 0, 2 * MLSTM_WIDTH, 3 * MLSTM_WIDTH
B_GATE = 4 * MLSTM_WIDTH
B_IF = B_GATE + NSA_KV_GROUPS * LANE
SEG_B = 4608

WCOL_PAD, WCOL_HI, WCOL_LO, WCOL_QHI, WCOL_QLO = 0, 1, 2, 3, 4


def _dot(a, b):
    return jnp.dot(a, b, preferred_element_type=F32)


def _dot_nt(a, b):
    return lax.dot_general(a, b, (((1,), (1,)), ((), ())), preferred_element_type=F32)


def _sigmoid(x):
    return 1.0 / (1.0 + jnp.exp(-x))


def _inproj_kernel(x_ref, g_ref, w_ref, cs_ref, o_ref, h_ref):
    @pl.when(pl.program_id(1) == 0)
    def _():
        x = x_ref[...]
        r = lax.rsqrt(jnp.mean(x * x, axis=-1, keepdims=True) + EPS)
        h_ref[...] = (x * r * g_ref[...]).astype(BF16)

    o_ref[...] = (_dot(h_ref[...], w_ref[...]) * cs_ref[...]).astype(o_ref.dtype)


def _inproj(x2, g, w, cs, out_dtype, tm, tn):
    n, d = x2.shape
    nc = w.shape[1]
    return pl.pallas_call(
        _inproj_kernel,
        grid=(n // tm, nc // tn),
        in_specs=[
            pl.BlockSpec((tm, d), lambda i, j: (i, 0)),
            pl.BlockSpec((1, d), lambda i, j: (0, 0)),
            pl.BlockSpec((d, tn), lambda i, j: (0, j)),
            pl.BlockSpec((1, tn), lambda i, j: (0, j)),
        ],
        out_specs=pl.BlockSpec((tm, tn), lambda i, j: (i, j)),
        out_shape=jax.ShapeDtypeStruct((n, nc), out_dtype),
        scratch_shapes=[pltpu.VMEM((tm, d), BF16)],
        compiler_params=pltpu.CompilerParams(
            dimension_semantics=("parallel", "arbitrary"), vmem_limit_bytes=VMEM_LIMIT),
        name="inproj",
    )(x2, g, w, cs)


def _compress_kernel(x_ref, w1_ref, w2_ref, pos_ref, o_ref, ot_ref):
    x = x_ref[0, 0]
    n_sub = x.shape[0]
    half = CMP_STRIDE * NSA_HEAD_DIM
    w1 = w1_ref[0]
    top = _dot(x, w1[:half])
    bot = _dot(x, w1[half:])
    bot = pltpu.roll(bot, n_sub - 1, axis=0)
    posw = _dot(pos_ref[0], w1)[0:1]
    pre = top + bot + posw
    hid = pre * _sigmoid(pre)
    out = _dot(hid.astype(BF16), w2_ref[0])
    o_ref[0, 0] = out.astype(o_ref.dtype)
    ot_ref[0, 0] = out.T.astype(ot_ref.dtype)


def _compress(xs, w1s, w2s, poss):
    b, c, n_sub, feat = xs.shape
    g = NSA_KV_GROUPS
    return pl.pallas_call(
        _compress_kernel,
        grid=(b, c),
        in_specs=[
            pl.BlockSpec((1, 1, n_sub, feat), lambda i, j: (i, j, 0, 0)),
            pl.BlockSpec((1, 2 * feat, NSA_HEAD_DIM), lambda i, j: (j // g, 0, 0)),
            pl.BlockSpec((1, NSA_HEAD_DIM, NSA_HEAD_DIM), lambda i, j: (j // g, 0, 0)),
            pl.BlockSpec((1, 8, 2 * feat), lambda i, j: (j // g, 0, 0)),
        ],
        out_specs=[pl.BlockSpec((1, 1, n_sub, NSA_HEAD_DIM), lambda i, j: (i, j, 0, 0)),
                   pl.BlockSpec((1, 1, NSA_HEAD_DIM, n_sub), lambda i, j: (i, j, 0, 0))],
        out_shape=[jax.ShapeDtypeStruct((b, c, n_sub, NSA_HEAD_DIM), BF16),
                   jax.ShapeDtypeStruct((b, c, NSA_HEAD_DIM, n_sub), BF16)],
        compiler_params=pltpu.CompilerParams(
            dimension_semantics=("parallel", "parallel"), vmem_limit_bytes=VMEM_LIMIT),
        name="compress",
    )(xs, w1s, w2s, poss)


def _stack_heads(q_all):
    dh = NSA_HEAD_DIM
    return jnp.concatenate([q_all[:, r * dh:(r + 1) * dh] for r in range(NSA_REP)], axis=0)


def _nsa_cmp_kernel(slopes_ref, q_ref, kc_ref, vct_ref, ovt_ref, ocmp_ref, pen_ref, flag_ref, *, seq):
    g = pl.program_id(1)
    qb = pl.program_id(2)
    nq = Q_BLOCK
    n_cpad = kc_ref.shape[2]
    n_cmp = seq // CMP_STRIDE - CMP_BLOCK // CMP_STRIDE + 1
    t0 = qb * nq

    s_t = _dot_nt(kc_ref[0, 0], _stack_heads(q_ref[...]))
    n_s = lax.broadcasted_iota(jnp.int32, (n_cpad, 1), 0)
    q_l = lax.broadcasted_iota(jnp.int32, (1, nq), 1)
    dist = (t0 - (CMP_BLOCK - 1)) + q_l - n_s * CMP_STRIDE
    real = n_s < n_cmp
    valid = (dist >= 0) & real
    dist_f = dist.astype(F32)
    probs = []
    p_sum = jnp.zeros((n_cpad, nq), F32)
    for r in range(NSA_REP):
        slope = slopes_ref[g * NSA_REP + r]
        s = jnp.where(valid, s_t[:, r * nq:(r + 1) * nq] - slope * dist_f, NEG)
        e = jnp.exp(s - jnp.max(s, axis=0, keepdims=True))
        denom = jnp.sum(jnp.where(real, e, 0.0), axis=0, keepdims=True)
        p = jnp.where(valid, e, 0.0) / denom
        probs.append(p.astype(BF16))
        p_sum = p_sum + p
    ocmp_ref[0, 0, 0] = _dot(vct_ref[0, 0], jnp.concatenate(probs, axis=1))

    p_hi = p_sum.astype(BF16)
    p_lo = (p_sum - p_hi.astype(F32)).astype(BF16)
    ovt = ovt_ref[...]
    imp = _dot(ovt, p_hi) + _dot(ovt, p_lo)
    j_i = lax.broadcasted_iota(jnp.int32, (SEL_LANES, 1), 0)
    t_l = t0 + q_l
    cur = t_l // SEL_BLOCK
    forced = (j_i == 0) | (j_i == cur) | (j_i == cur - 1)
    causal_blk = j_i * SEL_BLOCK <= t_l
    val = jnp.where(causal_blk, jnp.where(forced, imp + FORCE_BONUS, imp), NEG)
    j_f = j_i.astype(F32)
    sel_t = jnp.zeros((SEL_LANES, nq), F32)
    for _ in range(min(SEL_TOPK, seq // SEL_BLOCK)):
        mx = jnp.max(val, axis=0, keepdims=True)
        first = jnp.min(jnp.where(val == mx, j_f, float(SEL_LANES)), axis=0, keepdims=True)
        pick = j_f == first
        sel_t = jnp.where(pick, 1.0, sel_t)
        val = jnp.where(pick, PICKED, val)
    sel = sel_t.T
    pen_ref[0, 0] = ((sel - 1.0) * MASK_BIG).astype(pen_ref.dtype)
    flag_ref[0, 0, 0] = (jnp.max(sel, axis=0, keepdims=True) > 0.0).astype(jnp.int32)


def _nsa_cmp(seg_a, kvc, kvct, ovt, slopes, batch, seq):
    nqb = seq // Q_BLOCK
    gq = NSA_REP * NSA_HEAD_DIM
    g_ = NSA_KV_GROUPS
    n_cpad = kvc.shape[2]
    return pl.pallas_call(
        functools.partial(_nsa_cmp_kernel, seq=seq),
        grid=(batch, g_, nqb),
        in_specs=[
            pl.BlockSpec(memory_space=pltpu.SMEM),
            pl.BlockSpec((Q_BLOCK, gq), lambda b, g, q: (b * nqb + q, g)),
            pl.BlockSpec((1, 1, n_cpad, NSA_HEAD_DIM), lambda b, g, q: (b, g, 0, 0)),
            pl.BlockSpec((1, 1, NSA_HEAD_DIM, n_cpad), lambda b, g, q: (b, g_ + g, 0, 0)),
            pl.BlockSpec((SEL_LANES, n_cpad), lambda b, g, q: (0, 0)),
        ],
        out_specs=[
            pl.BlockSpec((1, 1, 1, NSA_HEAD_DIM, gq), lambda b, g, q: (b, g, q, 0, 0)),
            pl.BlockSpec((1, 1, Q_BLOCK, SEL_LANES), lambda b, g, q: (b, g, q, 0)),
            pl.BlockSpec((1, 1, 1, 1, SEL_LANES), lambda b, g, q: (b, g, q, 0, 0)),
        ],
        out_shape=[
            jax.ShapeDtypeStruct((batch, g_, nqb, NSA_HEAD_DIM, gq), F32),
            jax.ShapeDtypeStruct((batch, g_, seq, SEL_LANES), BF16),
            jax.ShapeDtypeStruct((batch, g_, nqb, 1, SEL_LANES), jnp.int32),
        ],
        compiler_params=pltpu.CompilerParams(
            dimension_semantics=("parallel", "parallel", "parallel"), vmem_limit_bytes=VMEM_LIMIT),
        name="nsa_cmp",
    )(slopes, seg_a, kvc, kvct, ovt)


def _nsa_attn_kernel(slopes_ref, flag_ref, q_ref, gate_ref, pen_ref, ocmp_ref, ksel_ref, vselt_ref,
                     kwin_ref, vwint_ref, dbias_ref, ubias_ref, wbias_ref, o_ref,
                     qa_ref, qw_ref, list_ref, m_ref, l_ref, acc_ref, owin_ref, sbuf_ref, *, seq):
    g = pl.program_id(1)
    qb = pl.program_id(2)
    dh = NSA_HEAD_DIM
    nq = Q_BLOCK
    kt = KEY_TILE
    per = SEL_TILES_PER_STEP
    pad_tile = seq // kt
    t0 = qb * nq
    slopes = [slopes_ref[g * NSA_REP + r] for r in range(NSA_REP)]
    row_q = lax.broadcasted_iota(jnp.int32, (nq, 1), 0)
    lane = lax.broadcasted_iota(jnp.int32, (1, LANE), 1)
    q_all = q_ref[...]

    pen = pen_ref[0, 0].astype(F32)
    blk_rel = ((lane - (t0 + row_q) // SEL_BLOCK) * SEL_BLOCK).astype(F32)
    for r in range(NSA_REP):
        qa_ref[r * nq:(r + 1) * nq, :dh] = q_all[:, r * dh:(r + 1) * dh]
        qa_ref[r * nq:(r + 1) * nq, dh:] = (pen + slopes[r] * blk_rel).astype(BF16)

    def scan(i, cnt):
        act = (flag_ref[0, 0, 0, 0, 2 * i] + flag_ref[0, 0, 0, 0, 2 * i + 1]) > 0

        @pl.when(act)
        def _():
            list_ref[cnt] = i

        return cnt + act.astype(jnp.int32)

    cnt = lax.fori_loop(0, qb, scan, 0)
    for i in range(2 * per):
        list_ref[cnt + i] = pad_tile

    def group_scores(it):
        keys = jnp.concatenate(
            [ksel_ref[0, 0, pl.ds(pl.multiple_of(list_ref[it * per + i] * kt, kt), kt), :]
             for i in range(per)], axis=0)
        return _dot_nt(keys, qa_ref[...]) + ubias_ref[0]

    s = _dot_nt(ksel_ref[0, 0, pl.ds(pl.multiple_of(t0, kt), kt), :], qa_ref[...]) + dbias_ref[0]
    m0 = jnp.max(s, axis=0, keepdims=True)
    p = jnp.exp(s - m0)
    m_ref[...] = m0
    l_ref[...] = jnp.sum(p, axis=0, keepdims=True)
    acc_ref[...] = _dot(vselt_ref[0, 0, qb], p.astype(BF16))
    sbuf_ref[0] = group_scores(0)

    tp = t0 + WINDOW + row_q
    t_hi = (tp // SEL_BLOCK).astype(F32)
    t_lo = (tp % SEL_BLOCK).astype(F32)
    for r in range(NSA_REP):
        sl = slopes[r]
        ext = jnp.where(lane == WCOL_PAD, -MASK_BIG, 0.0)
        ext = jnp.where(lane == WCOL_HI, sl * SEL_BLOCK, ext)
        ext = jnp.where(lane == WCOL_LO, sl, ext)
        ext = jnp.where(lane == WCOL_QHI, -sl * SEL_BLOCK * t_hi, ext)
        ext = jnp.where(lane == WCOL_QLO, -sl * t_lo, ext)
        qw_ref[r * nq:(r + 1) * nq, :dh] = q_all[:, r * dh:(r + 1) * dh]
        qw_ref[r * nq:(r + 1) * nq, dh:] = ext.astype(BF16)
    wlen = WINDOW + nq
    s = _dot_nt(kwin_ref[0, 0, pl.ds(pl.multiple_of(t0, kt), wlen), :], qw_ref[...]) + wbias_ref[...]
    e = jnp.exp(s - jnp.max(s, axis=0, keepdims=True))
    vwin = jnp.concatenate([vwint_ref[0, 0, qb + i] for i in range(WIN_TILES + 1)], axis=1)
    owin_ref[...] = _dot(vwin, e.astype(BF16)) / jnp.sum(e, axis=0, keepdims=True)

    def sel_body(it, carry):
        slot = it % 2
        s = sbuf_ref[slot]
        sbuf_ref[1 - slot] = group_scores(it + 1)
        vals = jnp.concatenate([vselt_ref[0, 0, list_ref[it * per + i]] for i in range(per)], axis=1)
        m_old = m_ref[...]
        m_new = jnp.maximum(m_old, jnp.max(s, axis=0, keepdims=True))
        alpha = jnp.exp(m_old - m_new)
        p = jnp.exp(s - m_new)
        l_ref[...] = alpha * l_ref[...] + jnp.sum(p, axis=0, keepdims=True)
        acc_ref[...] = alpha * acc_ref[...] + _dot(vals, p.astype(BF16))
        m_ref[...] = m_new
        return carry

    lax.fori_loop(0, (cnt + per - 1) // per, sel_body, 0)
    o_sel = acc_ref[...] / l_ref[...]
    o_win = owin_ref[...]

    gate_t = _sigmoid(gate_ref[...]).T
    o_cmp = ocmp_ref[0, 0, 0]
    for r in range(NSA_REP):
        cols = slice(r * nq, (r + 1) * nq)
        c0 = N_BRANCH * r
        out_t = (gate_t[c0:c0 + 1] * o_cmp[:, cols] + gate_t[c0 + 1:c0 + 2] * o_sel[:, cols]
                 + gate_t[c0 + 2:c0 + 3] * o_win[:, cols])
        o_ref[:, r * dh:(r + 1) * dh] = out_t.T.astype(o_ref.dtype)


def _nsa_attn(seg_a, seg_b, pen, ocmp, flags, ksel, vselt, kwin, vwint, dbias, ubias, wbias,
              slopes, batch, seq):
    n = batch * seq
    nqb = seq // Q_BLOCK
    gq = NSA_REP * NSA_HEAD_DIM
    g_ = NSA_KV_GROUPS
    dk = ksel.shape[-1]
    rq = NSA_REP * Q_BLOCK

    def whole(arr):
        shp = (1, 1) + arr.shape[2:]
        return pl.BlockSpec(shp, lambda b, g, q, nd=len(shp): (b, g) + (0,) * (nd - 2))

    return pl.pallas_call(
        functools.partial(_nsa_attn_kernel, seq=seq),
        grid=(batch, g_, nqb),
        in_specs=[
            pl.BlockSpec(memory_space=pltpu.SMEM),
            pl.BlockSpec((1, 1, 1, 1, SEL_LANES), lambda b, g, q: (b, g, q, 0, 0),
                         memory_space=pltpu.SMEM),
            pl.BlockSpec((Q_BLOCK, gq), lambda b, g, q: (b * nqb + q, g)),
            pl.BlockSpec((Q_BLOCK, LANE), lambda b, g, q: (b * nqb + q, B_GATE // LANE + g)),
            pl.BlockSpec((1, 1, Q_BLOCK, SEL_LANES), lambda b, g, q: (b, g, q, 0)),
            pl.BlockSpec((1, 1, 1, NSA_HEAD_DIM, gq), lambda b, g, q: (b, g, q, 0, 0)),
            whole(ksel), whole(vselt), whole(kwin), whole(vwint),
            pl.BlockSpec((1,) + dbias.shape[1:], lambda b, g, q: (g, 0, 0)),
            pl.BlockSpec((1,) + ubias.shape[1:], lambda b, g, q: (g, 0, 0)),
            pl.BlockSpec(wbias.shape, lambda b, g, q: (0, 0)),
        ],
        out_specs=pl.BlockSpec((Q_BLOCK, gq), lambda b, g, q: (b * nqb + q, g)),
        out_shape=jax.ShapeDtypeStruct((n, NSA_WIDTH), BF16),
        scratch_shapes=[
            pltpu.VMEM((rq, dk), BF16),
            pltpu.VMEM((rq, dk), BF16),
            pltpu.SMEM((seq // KEY_TILE + 2 * SEL_TILES_PER_STEP,), jnp.int32),
            pltpu.VMEM((1, rq), F32),
            pltpu.VMEM((1, rq), F32),
            pltpu.VMEM((NSA_HEAD_DIM, rq), F32),
            pltpu.VMEM((NSA_HEAD_DIM, rq), F32),
            pltpu.VMEM((2, SEL_TILES_PER_STEP * KEY_TILE, rq), F32),
        ],
        compiler_params=pltpu.CompilerParams(
            dimension_semantics=("parallel", "parallel", "arbitrary"), vmem_limit_bytes=VMEM_LIMIT),
        name="nsa_attn",
    )(slopes, flags, seg_a, seg_b, pen, ocmp, ksel, vselt, kwin, vwint, dbias, ubias, wbias)


def _nsa_operands(seg_a, slopes, batch, seq):
    g_, dh, kt = NSA_KV_GROUPS, NSA_HEAD_DIM, KEY_TILE
    n_t = seq // kt
    a3 = seg_a.reshape(batch, seq, SEG_A)

    def grouped(kind):
        c0 = NSA_WIDTH + kind * NSA_KV_WIDTH
        return a3[:, :, c0:c0 + NSA_KV_WIDTH].reshape(batch, seq, g_, dh).transpose(0, 2, 1, 3)

    def tiles_t(v, front):
        vt = v.reshape(batch, g_, n_t, kt, dh).transpose(0, 1, 2, 4, 3)
        return jnp.pad(vt, ((0, 0), (0, 0), (front, 1 - min(front, 1)), (0, 0), (0, 0)))

    pos = np.arange(seq)
    onehot = jnp.asarray(pos[:, None] // SEL_BLOCK == np.arange(SEL_LANES)[None, :], BF16)
    ksel = jnp.concatenate([grouped(2), jnp.broadcast_to(onehot, (batch, g_, seq, SEL_LANES))], axis=-1)
    pad_rows = jnp.concatenate([jnp.zeros((kt, dh), BF16), jnp.ones((kt, SEL_LANES), BF16)], axis=-1)
    ksel = jnp.concatenate([ksel, jnp.broadcast_to(pad_rows, (batch, g_, kt, dh + SEL_LANES))], axis=2)
    vselt = tiles_t(grouped(3), 0)

    ext = np.zeros((seq, LANE), np.float32)
    ext[:, WCOL_HI] = (pos + WINDOW) // SEL_BLOCK
    ext[:, WCOL_LO] = (pos + WINDOW) % SEL_BLOCK
    ext[:, WCOL_QHI] = 1.0
    ext[:, WCOL_QLO] = 1.0
    kwin = jnp.concatenate([grouped(4), jnp.broadcast_to(jnp.asarray(ext, BF16), (batch, g_, seq, LANE))],
                           axis=-1)
    front = np.zeros((WINDOW, dh + LANE), np.float32)
    front[:, dh + WCOL_PAD] = 1.0
    kwin = jnp.concatenate([jnp.broadcast_to(jnp.asarray(front, BF16), (batch, g_, WINDOW, dh + LANE)),
                            kwin], axis=2)
    vwint = tiles_t(grouped(5), WIN_TILES)

    rows = SEL_TILES_PER_STEP * kt
    u = jnp.asarray((np.arange(rows) % SEL_BLOCK).astype(np.float32))[None, :, None, None]
    ubias = jnp.broadcast_to(slopes.reshape(g_, 1, NSA_REP, 1) * u, (g_, rows, NSA_REP, Q_BLOCK))
    ubias = ubias.reshape(g_, rows, NSA_REP * Q_BLOCK)
    kq = np.arange(kt)[:, None] <= np.arange(Q_BLOCK)[None, :]
    causal = np.tile(np.where(kq, 0.0, NEG).astype(np.float32), (1, NSA_REP))
    dbias = ubias[:, :kt] + jnp.asarray(causal)[None]
    ki = np.arange(WINDOW + Q_BLOCK)[:, None]
    qi = np.arange(Q_BLOCK)[None, :]
    band = np.where((ki > qi) & (ki <= qi + WINDOW), 0.0, NEG).astype(np.float32)
    wbias = jnp.asarray(np.tile(band, (1, NSA_REP)))
    return ksel, vselt, kwin, vwint, dbias, ubias, wbias


def _shifted(x, tail, s):
    xs = pltpu.roll(x, s, axis=0)
    ts = pltpu.roll(tail, s, axis=0)
    row8 = lax.broadcasted_iota(jnp.int32, (8, 1), 0)
    head = jnp.where(row8 < s, ts, xs[:8])
    return jnp.concatenate([head, xs[8:]], axis=0)


def _conv_silu(x, tail, w, b):
    y = b + _shifted(x, tail, CONV_WIDTH - 1) * w[0:1]
    for i in range(1, CONV_WIDTH - 1):
        y = y + _shifted(x, tail, CONV_WIDTH - 1 - i) * w[i:i + 1]
    y = y + x * w[CONV_WIDTH - 1:CONV_WIDTH]
    return y * _sigmoid(y)


def _log_sigmoid(x):
    return jnp.minimum(x, 0.0) - jnp.log(1.0 + jnp.exp(-jnp.abs(x)))


def _mlstm_kernel(bias_ref, q_ref, k_ref, v_ref, o_ref, ifc_ref, ifr_ref, cw_ref, cb_ref, ng_ref,
                  tri_ref, y_ref, c_ref, n_ref, m_ref, qtail_ref, ktail_ref):
    ch = pl.program_id(1)
    nh, dh = MLSTM_HEADS, MLSTM_HEAD_DIM
    L = q_ref.shape[0]

    @pl.when(ch == 0)
    def _():
        c_ref[...] = jnp.zeros(c_ref.shape, F32)
        n_ref[...] = jnp.zeros(n_ref.shape, F32)
        m_ref[...] = jnp.zeros(m_ref.shape, F32)
        qtail_ref[...] = jnp.zeros(qtail_ref.shape, F32)
        ktail_ref[...] = jnp.zeros(ktail_ref.shape, F32)

    hi = lax.Precision.HIGHEST
    tri = tri_ref[...]
    lane8 = lax.broadcasted_iota(jnp.int32, (1, LANE), 1)
    bias_c = jnp.zeros((1, LANE), F32)
    for h in range(nh):
        bias_c = jnp.where(lane8 == h, bias_ref[h], bias_c)
        bias_c = jnp.where(lane8 == nh + h, bias_ref[nh + h], bias_c)
    pre_c = ifc_ref[...] + bias_c
    cum_c = jnp.dot(tri, _log_sigmoid(pre_c), precision=hi, preferred_element_type=F32)
    row8 = lax.broadcasted_iota(jnp.int32, (8, 1), 0)
    bias_r = jnp.zeros((8, 1), F32)
    for h in range(2 * nh):
        bias_r = jnp.where(row8 == h, bias_ref[h], bias_r)
    pre_r = ifr_ref[...] + bias_r
    cum_r = lax.dot_general(_log_sigmoid(pre_r), tri, (((1,), (1,)), ((), ())),
                            precision=hi, preferred_element_type=F32)

    rr = lax.broadcasted_iota(jnp.int32, (L, 1), 0)
    cc = lax.broadcasted_iota(jnp.int32, (1, L), 1)
    causal = cc <= rr

    q_raw = q_ref[...]
    k_raw = k_ref[...]
    cw = cw_ref[...]
    cb = cb_ref[...]
    qc = _conv_silu(q_raw, qtail_ref[...], cw[:, :nh * dh], cb[:, :nh * dh])
    kc = _conv_silu(k_raw, ktail_ref[...], cw[:, nh * dh:], cb[:, nh * dh:]) * (dh ** -0.5)
    qtail_ref[...] = q_raw[L - 8:]
    ktail_ref[...] = k_raw[L - 8:]

    for h in range(nh):
        cols = slice(h * dh, (h + 1) * dh)
        qh = qc[:, cols]
        kh = kc[:, cols]
        vh = v_ref[:, cols].astype(BF16)
        qb = qh.astype(BF16)
        b_c = cum_c[:, nh + h:nh + h + 1]
        li_c = pre_c[:, h:h + 1]
        b_r = cum_r[nh + h:nh + h + 1, :]
        li_r = pre_r[h:h + 1, :]
        m_prev = m_ref[h:h + 1, 0:1]

        dmat = jnp.where(causal, b_c - b_r + li_r, NEG)
        a = b_c + m_prev
        m_j = jnp.maximum(a, jnp.max(dmat, axis=1, keepdims=True))
        w_intra = jnp.exp(dmat - m_j)
        w_inter = jnp.exp(a - m_j)
        sc = _dot_nt(qb, kh.astype(BF16)) * w_intra
        c_old = c_ref[h]
        n_old = n_ref[h:h + 1, :]
        num = w_inter * _dot(qb, c_old.astype(BF16)) + _dot(sc.astype(BF16), vh)
        den = (w_inter * jnp.sum(qh * n_old, axis=1, keepdims=True)
               + jnp.sum(sc, axis=1, keepdims=True))
        hid = num / jnp.maximum(jnp.abs(den), jnp.exp(-m_j))

        g_tot = b_r[:, L - 1:L]
        lw_c = g_tot - b_c + li_c
        lw_r = g_tot - b_r + li_r
        m_new = jnp.maximum(g_tot + m_prev, jnp.max(lw_r, axis=1, keepdims=True))
        decay = jnp.exp(g_tot + m_prev - m_new)
        kw = jnp.exp(lw_c - m_new) * kh
        c_ref[h] = decay * c_old + _dot(kw.T.astype(BF16), vh)
        n_ref[h:h + 1, :] = decay * n_old + jnp.sum(kw, axis=0, keepdims=True)
        m_ref[h:h + 1, :] = jnp.broadcast_to(m_new, (1, LANE))

        hn = hid * lax.rsqrt(jnp.mean(hid * hid, axis=-1, keepdims=True) + EPS) * ng_ref[:, cols]
        y_ref[:, cols] = (_sigmoid(o_ref[:, cols]) * hn).astype(y_ref.dtype)


def _mlstm(seg_b, if_rows, bias, conv_w, conv_b, norm_g, tri, batch, seq, chunk):
    n = batch * seq
    nc = seq // chunk
    w = MLSTM_WIDTH
    nh, dh = MLSTM_HEADS, MLSTM_HEAD_DIM

    def col_spec(off):
        return pl.BlockSpec((chunk, w), lambda b, c, o=off // w: (b * nc + c, o))

    return pl.pallas_call(
        _mlstm_kernel,
        grid=(batch, nc),
        in_specs=[
            pl.BlockSpec(memory_space=pltpu.SMEM),
            col_spec(B_QK), col_spec(B_QK + w), col_spec(B_V), col_spec(B_O),
            pl.BlockSpec((chunk, LANE), lambda b, c: (b * nc + c, B_IF // LANE)),
            pl.BlockSpec((8, chunk), lambda b, c: (0, b * nc + c)),
            pl.BlockSpec((CONV_WIDTH, 2 * w), lambda b, c: (0, 0)),
            pl.BlockSpec((1, 2 * w), lambda b, c: (0, 0)),
            pl.BlockSpec((1, w), lambda b, c: (0, 0)),
            pl.BlockSpec((chunk, chunk), lambda b, c: (0, 0)),
        ],
        out_specs=pl.BlockSpec((chunk, w), lambda b, c: (b * nc + c, 0)),
        out_shape=jax.ShapeDtypeStruct((n, w), BF16),
        scratch_shapes=[
            pltpu.VMEM((nh, dh, dh), F32),
            pltpu.VMEM((8, dh), F32),
            pltpu.VMEM((8, LANE), F32),
            pltpu.VMEM((8, w), F32),
            pltpu.VMEM((8, w), F32),
        ],
        compiler_params=pltpu.CompilerParams(
            dimension_semantics=("parallel", "arbitrary"), vmem_limit_bytes=VMEM_LIMIT),
        name="mlstm",
    )(bias, seg_b, seg_b, seg_b, seg_b, seg_b, if_rows, conv_w, conv_b, norm_g, tri)


def _outproj_kernel(x_ref, ya_ref, ym_ref, wa_ref, wm_ref, o_ref):
    o_ref[...] = x_ref[...] + _dot(ya_ref[...], wa_ref[...]) + _dot(ym_ref[...], wm_ref[...])


def _outproj(x2, ya, ym, wa, wm, tm):
    n, d = x2.shape
    return pl.pallas_call(
        _outproj_kernel,
        grid=(n // tm,),
        in_specs=[
            pl.BlockSpec((tm, d), lambda i: (i, 0)),
            pl.BlockSpec((tm, ya.shape[1]), lambda i: (i, 0)),
            pl.BlockSpec((tm, ym.shape[1]), lambda i: (i, 0)),
            pl.BlockSpec(wa.shape, lambda i: (0, 0)),
            pl.BlockSpec(wm.shape, lambda i: (0, 0)),
        ],
        out_specs=pl.BlockSpec((tm, d), lambda i: (i, 0)),
        out_shape=jax.ShapeDtypeStruct((n, d), F32),
        compiler_params=pltpu.CompilerParams(
            dimension_semantics=("parallel",), vmem_limit_bytes=VMEM_LIMIT),
        name="outproj",
    )(x2, ya, ym, wa, wm)


def _mlp_kernel(x_ref, g_ref, w1_ref, w2_ref, gf_ref, o_ref, h_ref):
    f = pl.program_id(1)

    @pl.when(f == 0)
    def _():
        x = x_ref[...]
        r = lax.rsqrt(jnp.mean(x * x, axis=-1, keepdims=True) + EPS)
        h_ref[...] = (x * r * g_ref[...]).astype(BF16)
        o_ref[...] = x

    u = jnp.maximum(_dot(h_ref[...], w1_ref[...]), 0.0)
    o_ref[...] += _dot((u * u).astype(BF16), w2_ref[...])

    @pl.when(f == pl.num_programs(1) - 1)
    def _():
        x2 = o_ref[...]
        r = lax.rsqrt(jnp.mean(x2 * x2, axis=-1, keepdims=True) + EPS)
        o_ref[...] = x2 * r * gf_ref[...]


def _mlp(x1, g, w1, w2, gf, tm, tf):
    n, d = x1.shape
    dff = w1.shape[1]
    return pl.pallas_call(
        _mlp_kernel,
        grid=(n // tm, dff // tf),
        in_specs=[
            pl.BlockSpec((tm, d), lambda i, f: (i, 0)),
            pl.BlockSpec((1, d), lambda i, f: (0, 0)),
            pl.BlockSpec((d, tf), lambda i, f: (0, f)),
            pl.BlockSpec((tf, d), lambda i, f: (f, 0)),
            pl.BlockSpec((1, d), lambda i, f: (0, 0)),
        ],
        out_specs=pl.BlockSpec((tm, d), lambda i, f: (i, 0)),
        out_shape=jax.ShapeDtypeStruct((n, d), F32),
        scratch_shapes=[pltpu.VMEM((tm, d), BF16)],
        compiler_params=pltpu.CompilerParams(
            dimension_semantics=("parallel", "arbitrary"), vmem_limit_bytes=VMEM_LIMIT),
        name="mlp",
    )(x1, g, w1, w2, gf)


def _row_tile(n, want):
    t = want
    while n % t:
        t //= 2
    return t


def _layer(x2, batch, seq, norm_mix_g, w_in, w_cmp_k1, w_cmp_k2, pos_cmp_k, w_cmp_v1, w_cmp_v2,
           pos_cmp_v, conv_w, conv_b, b_igate, b_fgate, mlstm_norm_g, w_out, norm_mlp_g,
           w_mlp_in, w_mlp_out):
    n, d = x2.shape
    assert seq % Q_BLOCK == 0 and seq >= WINDOW + Q_BLOCK and seq // SEL_BLOCK <= SEL_LANES
    g_ = NSA_KV_GROUPS
    nh = MLSTM_HEADS

    c_gate = SEG_A
    c_qk = c_gate + NSA_HEADS * N_BRANCH
    c_i = c_qk + 4 * MLSTM_WIDTH
    c_f = c_i + nh
    w_a = w_in[:, :SEG_A].astype(BF16)
    gate_cols = []
    per_g = NSA_REP * N_BRANCH
    for g in range(g_):
        gate_cols += [w_in[:, c_gate + g * per_g:c_gate + (g + 1) * per_g],
                      jnp.zeros((d, LANE - per_g), w_in.dtype)]
    w_b = jnp.concatenate(
        [w_in[:, c_qk:c_i]] + gate_cols
        + [w_in[:, c_i:c_f + nh], jnp.zeros((d, SEG_B - B_IF - 2 * nh), w_in.dtype)],
        axis=1).astype(BF16)
    scale_a = jnp.concatenate([jnp.full((1, NSA_WIDTH), NSA_HEAD_DIM ** -0.5, F32),
                               jnp.ones((1, SEG_A - NSA_WIDTH), F32)], axis=1)
    scale_b = jnp.ones((1, SEG_B), F32)
    g_mix = norm_mix_g.reshape(1, d)

    tm = _row_tile(n, ROW_TILE)
    seg_a = _inproj(x2, g_mix, w_a, scale_a, BF16, tm, 512)
    seg_b = _inproj(x2, g_mix, w_b, scale_b, F32, tm, 768)

    n_sub = seq // CMP_STRIDE
    cmp_in = seg_a[:, NSA_WIDTH:NSA_WIDTH + 2 * NSA_KV_WIDTH]
    cmp_in = cmp_in.reshape(batch, n_sub, CMP_STRIDE, 2 * g_, NSA_HEAD_DIM)
    cmp_in = cmp_in.transpose(0, 3, 1, 2, 4).reshape(batch, 2 * g_, n_sub, CMP_STRIDE * NSA_HEAD_DIM)
    w1s = jnp.stack([w_cmp_k1, w_cmp_v1]).astype(BF16)
    w2s = jnp.stack([w_cmp_k2, w_cmp_v2]).astype(BF16)
    poss = jnp.stack([pos_cmp_k, pos_cmp_v]).reshape(2, 1, CMP_BLOCK * NSA_HEAD_DIM)
    poss = jnp.broadcast_to(poss, (2, 8, CMP_BLOCK * NSA_HEAD_DIM)).astype(BF16)
    kvc, kvct = _compress(cmp_in, w1s, w2s, poss)

    cmp_start = np.arange(n_sub) * CMP_STRIDE
    sel_start = np.arange(SEL_LANES) * SEL_BLOCK
    ovt = ((cmp_start[None, :] < sel_start[:, None] + SEL_BLOCK)
           & (cmp_start[None, :] + CMP_BLOCK - 1 >= sel_start[:, None])
           & (np.arange(n_sub)[None, :] < n_sub - CMP_BLOCK // CMP_STRIDE + 1))
    ovt = jnp.asarray(ovt, BF16)
    slopes = jnp.exp2(-8.0 * jnp.arange(1, NSA_HEADS + 1, dtype=F32) / NSA_HEADS)
    ocmp, pen, flags = _nsa_cmp(seg_a, kvc, kvct, ovt, slopes, batch, seq)
    ksel, vselt, kwin, vwint, dbias, ubias, wbias = _nsa_operands(seg_a, slopes, batch, seq)
    y_a = _nsa_attn(seg_a, seg_b, pen, ocmp, flags, ksel, vselt, kwin, vwint, dbias, ubias, wbias,
                    slopes, batch, seq)

    chunk = 256 if seq % 256 == 0 else 128
    if_rows = seg_b[:, B_IF:B_IF + 2 * nh].T
    bias = jnp.concatenate([b_igate, b_fgate]).astype(F32)
    tri = jnp.asarray(np.tril(np.ones((chunk, chunk), np.float32)))
    y_m = _mlstm(seg_b, if_rows, bias, conv_w, conv_b.reshape(1, -1), mlstm_norm_g.reshape(1, -1),
                 tri, batch, seq, chunk)

    w_o = w_out.astype(BF16)
    x1 = _outproj(x2, y_a, y_m, w_o[:NSA_WIDTH], w_o[NSA_WIDTH:], _row_tile(n, 512))
    return x1, (norm_mlp_g.reshape(1, d), w_mlp_in.astype(BF16), w_mlp_out.astype(BF16))


def kernel(x, norm_mix_g, w_in, w_cmp_k1, w_cmp_k2, pos_cmp_k, w_cmp_v1, w_cmp_v2, pos_cmp_v, conv_w, conv_b, b_igate, b_fgate, mlstm_norm_g, w_out, norm_mlp_g, w_mlp_in, w_mlp_out, norm_f_g):
    batch, seq, d = x.shape
    depth = w_in.shape[0]
    assert depth == 1, "the final RMSNorm is fused into the last layer's channel mixer"
    x2 = x.reshape(batch * seq, d)
    tm = _row_tile(batch * seq, ROW_TILE)
    for l in range(depth):
        x1, (g_mlp, w1, w2) = _layer(
            x2, batch, seq, norm_mix_g[l], w_in[l], w_cmp_k1[l], w_cmp_k2[l], pos_cmp_k[l],
            w_cmp_v1[l], w_cmp_v2[l], pos_cmp_v[l], conv_w[l], conv_b[l], b_igate[l], b_fgate[l],
            mlstm_norm_g[l], w_out[l], norm_mlp_g[l], w_mlp_in[l], w_mlp_out[l])
        x2 = _mlp(x1, g_mlp, w1, w2, norm_f_g.reshape(1, d), tm, 512)
    return x2.reshape(batch, seq, d)
```

```python
import functools

import numpy as np
import jax
import jax.numpy as jnp
from jax import lax
from jax.experimental import pallas as pl
from jax.experimental.pallas import tpu as pltpu

F32 = jnp.float32
BF16 = jnp.bfloat16

EPS = 1e-6
NEG = -1e30
FORCE_BONUS = 1e4
PICKED = -3e38
MASK_BIG = 1e30

D_MODEL = 2048
NSA_HEAD_DIM = 128
NSA_WIDTH = D_MODEL // 2
NSA_HEADS = NSA_WIDTH // NSA_HEAD_DIM
NSA_REP = 4
NSA_KV_GROUPS = NSA_HEADS // NSA_REP
NSA_KV_WIDTH = NSA_KV_GROUPS * NSA_HEAD_DIM
CMP_BLOCK = 32
CMP_STRIDE = 16
SEL_BLOCK = 64
SEL_TOPK = 16
WINDOW = 512
Q_BLOCK = 128
N_BRANCH = 3
MLSTM_HEAD_DIM = 256
MLSTM_WIDTH = D_MODEL - NSA_WIDTH
MLSTM_HEADS = MLSTM_WIDTH // MLSTM_HEAD_DIM
CONV_WIDTH = 4
D_FF = 4 * D_MODEL

LANE = 128
SEL_LANES = 128
KEY_TILE = 128
SEL_TILES_PER_STEP = 2
WIN_TILES = WINDOW // KEY_TILE
VMEM_LIMIT = 56 * 1024 * 1024
ROW_TILE = 1024

SEG_A = NSA_WIDTH + 6 * NSA_KV_WIDTH
B_QK, B_V, B_O = 0, 2 * MLSTM_WIDTH, 3 * MLSTM_WIDTH
B_GATE = 4 * MLSTM_WIDTH
B_IF = B_GATE + NSA_KV_GROUPS * LANE
SEG_B = 4608

WCOL_PAD, WCOL_HI, WCOL_LO, WCOL_QHI, WCOL_QLO = 0, 1, 2, 3, 4


def _dot(a, b):
    return jnp.dot(a, b, preferred_element_type=F32)


def _dot_nt(a, b):
    return lax.dot_general(a, b, (((1,), (1,)), ((), ())), preferred_element_type=F32)


def _sigmoid(x):
    return 1.0 / (1.0 + jnp.exp(-x))


def _inproj_kernel(x_ref, g_ref, w_ref, cs_ref, o_ref, h_ref):
    @pl.when(pl.program_id(1) == 0)
    def _():
        x = x_ref[...]
        r = lax.rsqrt(jnp.mean(x * x, axis=-1, keepdims=True) + EPS)
        h_ref[...] = (x * r * g_ref[...]).astype(BF16)

    o_ref[...] = (_dot(h_ref[...], w_ref[...]) * cs_ref[...]).astype(o_ref.dtype)


def _inproj(x2, g, w, cs, out_dtype, tm, tn):
    n, d = x2.shape
    nc = w.shape[1]
    return pl.pallas_call(
        _inproj_kernel,
        grid=(n // tm, nc // tn),
        in_specs=[
            pl.BlockSpec((tm, d), lambda i, j: (i, 0)),
            pl.BlockSpec((1, d), lambda i, j: (0, 0)),
            pl.BlockSpec((d, tn), lambda i, j: (0, j)),
            pl.BlockSpec((1, tn), lambda i, j: (0, j)),
        ],
        out_specs=pl.BlockSpec((tm, tn), lambda i, j: (i, j)),
        out_shape=jax.ShapeDtypeStruct((n, nc), out_dtype),
        scratch_shapes=[pltpu.VMEM((tm, d), BF16)],
        compiler_params=pltpu.CompilerParams(
            dimension_semantics=("parallel", "arbitrary"), vmem_limit_bytes=VMEM_LIMIT),
        name="inproj",
    )(x2, g, w, cs)


def _compress_kernel(x_ref, w1_ref, w2_ref, pos_ref, o_ref, ot_ref):
    x = x_ref[0, 0]
    n_sub = x.shape[0]
    half = CMP_STRIDE * NSA_HEAD_DIM
    w1 = w1_ref[0]
    top = _dot(x, w1[:half])
    bot = _dot(x, w1[half:])
    bot = pltpu.roll(bot, n_sub - 1, axis=0)
    posw = _dot(pos_ref[0], w1)[0:1]
    pre = top + bot + posw
    hid = pre * _sigmoid(pre)
    out = _dot(hid.astype(BF16), w2_ref[0])
    o_ref[0, 0] = out.astype(o_ref.dtype)
    ot_ref[0, 0] = out.T.astype(ot_ref.dtype)


def _compress(xs, w1s, w2s, poss):
    b, c, n_sub, feat = xs.shape
    g = NSA_KV_GROUPS
    return pl.pallas_call(
        _compress_kernel,
        grid=(b, c),
        in_specs=[
            pl.BlockSpec((1, 1, n_sub, feat), lambda i, j: (i, j, 0, 0)),
            pl.BlockSpec((1, 2 * feat, NSA_HEAD_DIM), lambda i, j: (j // g, 0, 0)),
            pl.BlockSpec((1, NSA_HEAD_DIM, NSA_HEAD_DIM), lambda i, j: (j // g, 0, 0)),
            pl.BlockSpec((1, 8, 2 * feat), lambda i, j: (j // g, 0, 0)),
        ],
        out_specs=[pl.BlockSpec((1, 1, n_sub, NSA_HEAD_DIM), lambda i, j: (i, j, 0, 0)),
                   pl.BlockSpec((1, 1, NSA_HEAD_DIM, n_sub), lambda i, j: (i, j, 0, 0))],
        out_shape=[jax.ShapeDtypeStruct((b, c, n_sub, NSA_HEAD_DIM), BF16),
                   jax.ShapeDtypeStruct((b, c, NSA_HEAD_DIM, n_sub), BF16)],
        compiler_params=pltpu.CompilerParams(
            dimension_semantics=("parallel", "parallel"), vmem_limit_bytes=VMEM_LIMIT),
        name="compress",
    )(xs, w1s, w2s, poss)


def _stack_heads(q_all):
    dh = NSA_HEAD_DIM
    return jnp.concatenate([q_all[:, r * dh:(r + 1) * dh] for r in range(NSA_REP)], axis=0)


def _nsa_cmp_kernel(slopes_ref, q_ref, kc_ref, vct_ref, ovt_ref, ocmp_ref, pen_ref, flag_ref, *,
                    seq, sub_blocks):
    for sub in range(sub_blocks):
        rows = slice(sub * Q_BLOCK, (sub + 1) * Q_BLOCK)
        _nsa_cmp_block(slopes_ref, q_ref[rows, :], kc_ref, vct_ref, ovt_ref,
                       ocmp_ref.at[0, 0, sub], pen_ref.at[0, 0, rows], flag_ref.at[0, 0, sub],
                       pl.program_id(2) * sub_blocks + sub, seq)


def _nsa_cmp_block(slopes_ref, q_all, kc_ref, vct_ref, ovt_ref, ocmp_ref, pen_ref, flag_ref, qb, seq):
    g = pl.program_id(1)
    nq = Q_BLOCK
    n_cpad = kc_ref.shape[2]
    n_cmp = seq // CMP_STRIDE - CMP_BLOCK // CMP_STRIDE + 1
    t0 = qb * nq

    s_t = _dot_nt(kc_ref[0, 0], _stack_heads(q_all))
    n_s = lax.broadcasted_iota(jnp.int32, (n_cpad, 1), 0)
    q_l = lax.broadcasted_iota(jnp.int32, (1, nq), 1)
    dist = (t0 - (CMP_BLOCK - 1)) + q_l - n_s * CMP_STRIDE
    valid = (dist >= 0) & (n_s < n_cmp)
    dist_f = dist.astype(F32)
    probs = []
    p_sum = jnp.zeros((n_cpad, nq), F32)
    for r in range(NSA_REP):
        slope = slopes_ref[g * NSA_REP + r]
        s = jnp.where(valid, s_t[:, r * nq:(r + 1) * nq] - slope * dist_f, NEG)
        m = jnp.max(s, axis=0, keepdims=True)
        e = jnp.exp(s - m)
        inv = jnp.where(m > 0.5 * NEG, 1.0 / jnp.sum(e, axis=0, keepdims=True), 0.0)
        p = e * inv
        probs.append(p.astype(BF16))
        p_sum = p_sum + p
    ocmp_ref[...] = _dot(vct_ref[0, 0], jnp.concatenate(probs, axis=1))

    p_hi = p_sum.astype(BF16)
    p_lo = (p_sum - p_hi.astype(F32)).astype(BF16)
    ovt = ovt_ref[...]
    imp = _dot(ovt, p_hi) + _dot(ovt, p_lo)
    j_i = lax.broadcasted_iota(jnp.int32, (SEL_LANES, 1), 0)
    t_l = t0 + q_l
    cur = t_l // SEL_BLOCK
    forced = (j_i == 0) | (j_i == cur) | (j_i == cur - 1)
    causal_blk = j_i * SEL_BLOCK <= t_l
    val = jnp.where(causal_blk, jnp.where(forced, imp + FORCE_BONUS, imp), NEG)
    j_f = j_i.astype(F32)
    sel_t = jnp.zeros((SEL_LANES, nq), F32)
    for _ in range(min(SEL_TOPK, seq // SEL_BLOCK)):
        mx = jnp.max(val, axis=0, keepdims=True)
        first = jnp.min(jnp.where(val == mx, j_f, float(SEL_LANES)), axis=0, keepdims=True)
        pick = j_f == first
        sel_t = jnp.where(pick, 1.0, sel_t)
        val = jnp.where(pick, PICKED, val)
    sel = sel_t.T
    pen_ref[...] = ((sel - 1.0) * MASK_BIG).astype(pen_ref.dtype)
    flag_ref[...] = (jnp.max(sel, axis=0, keepdims=True) > 0.0).astype(jnp.int32)


def _nsa_cmp(seg_a, kvc, kvct, ovt, slopes, batch, seq):
    nqb = seq // Q_BLOCK
    gq = NSA_REP * NSA_HEAD_DIM
    g_ = NSA_KV_GROUPS
    n_cpad = kvc.shape[2]
    sub = 2 if nqb % 2 == 0 else 1
    nstep = nqb // sub
    return pl.pallas_call(
        functools.partial(_nsa_cmp_kernel, seq=seq, sub_blocks=sub),
        grid=(batch, g_, nstep),
        in_specs=[
            pl.BlockSpec(memory_space=pltpu.SMEM),
            pl.BlockSpec((sub * Q_BLOCK, gq), lambda b, g, q: (b * nstep + q, g)),
            pl.BlockSpec((1, 1, n_cpad, NSA_HEAD_DIM), lambda b, g, q: (b, g, 0, 0)),
            pl.BlockSpec((1, 1, NSA_HEAD_DIM, n_cpad), lambda b, g, q: (b, g_ + g, 0, 0)),
            pl.BlockSpec((SEL_LANES, n_cpad), lambda b, g, q: (0, 0)),
        ],
        out_specs=[
            pl.BlockSpec((1, 1, sub, NSA_HEAD_DIM, gq), lambda b, g, q: (b, g, q, 0, 0)),
            pl.BlockSpec((1, 1, sub * Q_BLOCK, SEL_LANES), lambda b, g, q: (b, g, q, 0)),
            pl.BlockSpec((1, 1, sub, 1, SEL_LANES), lambda b, g, q: (b, g, q, 0, 0)),
        ],
        out_shape=[
            jax.ShapeDtypeStruct((batch, g_, nqb, NSA_HEAD_DIM, gq), F32),
            jax.ShapeDtypeStruct((batch, g_, seq, SEL_LANES), BF16),
            jax.ShapeDtypeStruct((batch, g_, nqb, 1, SEL_LANES), jnp.int32),
        ],
        compiler_params=pltpu.CompilerParams(
            dimension_semantics=("parallel", "parallel", "parallel"), vmem_limit_bytes=VMEM_LIMIT),
        name="nsa_cmp",
    )(slopes, seg_a, kvc, kvct, ovt)


def _nsa_attn_kernel(slopes_ref, flag_ref, q_ref, gate_ref, pen_ref, ocmp_ref, ksel_ref, vselt_ref,
                     kwin_ref, vwint_ref, dbias_ref, ubias_ref, wbias_ref, o_ref,
                     qa_ref, qw_ref, list_ref, m_ref, l_ref, acc_ref, owin_ref, sa_ref, sb_ref,
                     *, seq):
    g = pl.program_id(1)
    qb = pl.program_id(2)
    dh = NSA_HEAD_DIM
    nq = Q_BLOCK
    kt = KEY_TILE
    per = SEL_TILES_PER_STEP
    pad_tile = seq // kt
    t0 = qb * nq
    slopes = [slopes_ref[g * NSA_REP + r] for r in range(NSA_REP)]
    row_q = lax.broadcasted_iota(jnp.int32, (nq, 1), 0)
    lane = lax.broadcasted_iota(jnp.int32, (1, LANE), 1)
    q_all = q_ref[...]

    pen = pen_ref[0, 0].astype(F32)
    blk_rel = ((lane - (t0 + row_q) // SEL_BLOCK) * SEL_BLOCK).astype(F32)
    for r in range(NSA_REP):
        qa_ref[r * nq:(r + 1) * nq, :dh] = q_all[:, r * dh:(r + 1) * dh]
        qa_ref[r * nq:(r + 1) * nq, dh:] = (pen + slopes[r] * blk_rel).astype(BF16)

    def scan(k, cnt):
        for j in range(2):
            i = 2 * k + j
            act = ((flag_ref[0, 0, 0, 0, 2 * i] + flag_ref[0, 0, 0, 0, 2 * i + 1]) > 0) & (i < qb)
            list_ref[cnt] = i
            cnt = cnt + act.astype(jnp.int32)
        return cnt

    cnt = lax.fori_loop(0, (qb + 1) // 2, scan, 0)
    for i in range(4 * per):
        list_ref[cnt + i] = pad_tile

    def group_scores(it):
        keys = jnp.concatenate(
            [ksel_ref[0, 0, pl.ds(pl.multiple_of(list_ref[it * per + i] * kt, kt), kt), :]
             for i in range(per)], axis=0)
        return _dot_nt(keys, qa_ref[...]) + ubias_ref[0]

    s = _dot_nt(ksel_ref[0, 0, pl.ds(pl.multiple_of(t0, kt), kt), :], qa_ref[...]) + dbias_ref[0]
    m0 = jnp.max(s, axis=0, keepdims=True)
    p = jnp.exp(s - m0)
    m_ref[...] = m0
    l_ref[...] = jnp.sum(p, axis=0, keepdims=True)
    acc_ref[...] = _dot(vselt_ref[0, 0, qb], p.astype(BF16))
    sa_ref[...] = group_scores(0)

    tp = t0 + WINDOW + row_q
    t_hi = (tp // SEL_BLOCK).astype(F32)
    t_lo = (tp % SEL_BLOCK).astype(F32)
    for r in range(NSA_REP):
        sl = slopes[r]
        ext = jnp.where(lane == WCOL_PAD, -MASK_BIG, 0.0)
        ext = jnp.where(lane == WCOL_HI, sl * SEL_BLOCK, ext)
        ext = jnp.where(lane == WCOL_LO, sl, ext)
        ext = jnp.where(lane == WCOL_QHI, -sl * SEL_BLOCK * t_hi, ext)
        ext = jnp.where(lane == WCOL_QLO, -sl * t_lo, ext)
        qw_ref[r * nq:(r + 1) * nq, :dh] = q_all[:, r * dh:(r + 1) * dh]
        qw_ref[r * nq:(r + 1) * nq, dh:] = ext.astype(BF16)
    wlen = WINDOW + nq
    s = _dot_nt(kwin_ref[0, 0, pl.ds(pl.multiple_of(t0, kt), wlen), :], qw_ref[...]) + wbias_ref[...]
    e = jnp.exp(s - jnp.max(s, axis=0, keepdims=True))
    vwin = jnp.concatenate([vwint_ref[0, 0, qb + i] for i in range(WIN_TILES + 1)], axis=1)
    owin_ref[...] = _dot(vwin, e.astype(BF16)) / jnp.sum(e, axis=0, keepdims=True)

    def absorb(s, grp):
        vals = jnp.concatenate([vselt_ref[0, 0, list_ref[grp * per + i]] for i in range(per)], axis=1)
        m_old = m_ref[...]
        m_new = jnp.maximum(m_old, jnp.max(s, axis=0, keepdims=True))
        alpha = jnp.exp(m_old - m_new)
        p = jnp.exp(s - m_new)
        l_ref[...] = alpha * l_ref[...] + jnp.sum(p, axis=0, keepdims=True)
        acc_ref[...] = alpha * acc_ref[...] + _dot(vals, p.astype(BF16))
        m_ref[...] = m_new

    def sel_body(it, carry):
        s = sa_ref[...]
        sb_ref[...] = group_scores(2 * it + 1)
        absorb(s, 2 * it)
        s = sb_ref[...]
        sa_ref[...] = group_scores(2 * it + 2)
        absorb(s, 2 * it + 1)
        return carry

    lax.fori_loop(0, (cnt + 2 * per - 1) // (2 * per), sel_body, 0)
    o_sel = acc_ref[...] / l_ref[...]
    o_win = owin_ref[...]

    gate_t = _sigmoid(gate_ref[...]).T
    o_cmp = ocmp_ref[0, 0, 0]
    for r in range(NSA_REP):
        cols = slice(r * nq, (r + 1) * nq)
        c0 = N_BRANCH * r
        out_t = (gate_t[c0:c0 + 1] * o_cmp[:, cols] + gate_t[c0 + 1:c0 + 2] * o_sel[:, cols]
                 + gate_t[c0 + 2:c0 + 3] * o_win[:, cols])
        o_ref[:, r * dh:(r + 1) * dh] = out_t.T.astype(o_ref.dtype)


def _nsa_attn(seg_a, seg_b, pen, ocmp, flags, ksel, vselt, kwin, vwint, dbias, ubias, wbias,
              slopes, batch, seq):
    n = batch * seq
    nqb = seq // Q_BLOCK
    gq = NSA_REP * NSA_HEAD_DIM
    g_ = NSA_KV_GROUPS
    dk = ksel.shape[-1]
    rq = NSA_REP * Q_BLOCK

    def whole(arr):
        shp = (1, 1) + arr.shape[2:]
        return pl.BlockSpec(shp, lambda b, g, q, nd=len(shp): (b, g) + (0,) * (nd - 2))

    return pl.pallas_call(
        functools.partial(_nsa_attn_kernel, seq=seq),
        grid=(batch, g_, nqb),
        in_specs=[
            pl.BlockSpec(memory_space=pltpu.SMEM),
            pl.BlockSpec((1, 1, 1, 1, SEL_LANES), lambda b, g, q: (b, g, q, 0, 0),
                         memory_space=pltpu.SMEM),
            pl.BlockSpec((Q_BLOCK, gq), lambda b, g, q: (b * nqb + q, g)),
            pl.BlockSpec((Q_BLOCK, LANE), lambda b, g, q: (b * nqb + q, B_GATE // LANE + g)),
            pl.BlockSpec((1, 1, Q_BLOCK, SEL_LANES), lambda b, g, q: (b, g, q, 0)),
            pl.BlockSpec((1, 1, 1, NSA_HEAD_DIM, gq), lambda b, g, q: (b, g, q, 0, 0)),
            whole(ksel), whole(vselt), whole(kwin), whole(vwint),
            pl.BlockSpec((1,) + dbias.shape[1:], lambda b, g, q: (g, 0, 0)),
            pl.BlockSpec((1,) + ubias.shape[1:], lambda b, g, q: (g, 0, 0)),
            pl.BlockSpec(wbias.shape, lambda b, g, q: (0, 0)),
        ],
        out_specs=pl.BlockSpec((Q_BLOCK, gq), lambda b, g, q: (b * nqb + q, g)),
        out_shape=jax.ShapeDtypeStruct((n, NSA_WIDTH), BF16),
        scratch_shapes=[
            pltpu.VMEM((rq, dk), BF16),
            pltpu.VMEM((rq, dk), BF16),
            pltpu.SMEM((seq // KEY_TILE + 4 * SEL_TILES_PER_STEP,), jnp.int32),
            pltpu.VMEM((1, rq), F32),
            pltpu.VMEM((1, rq), F32),
            pltpu.VMEM((NSA_HEAD_DIM, rq), F32),
            pltpu.VMEM((NSA_HEAD_DIM, rq), F32),
            pltpu.VMEM((SEL_TILES_PER_STEP * KEY_TILE, rq), F32),
            pltpu.VMEM((SEL_TILES_PER_STEP * KEY_TILE, rq), F32),
        ],
        compiler_params=pltpu.CompilerParams(
            dimension_semantics=("parallel", "parallel", "arbitrary"), vmem_limit_bytes=VMEM_LIMIT),
        name="nsa_attn",
    )(slopes, flags, seg_a, seg_b, pen, ocmp, ksel, vselt, kwin, vwint, dbias, ubias, wbias)


def _nsa_operands(seg_a, slopes, batch, seq):
    g_, dh, kt = NSA_KV_GROUPS, NSA_HEAD_DIM, KEY_TILE
    n_t = seq // kt
    a3 = seg_a.reshape(batch, seq, SEG_A)

    def grouped(kind):
        c0 = NSA_WIDTH + kind * NSA_KV_WIDTH
        return a3[:, :, c0:c0 + NSA_KV_WIDTH].reshape(batch, seq, g_, dh).transpose(0, 2, 1, 3)

    def tiles_t(v, front):
        vt = v.reshape(batch, g_, n_t, kt, dh).transpose(0, 1, 2, 4, 3)
        return jnp.pad(vt, ((0, 0), (0, 0), (front, 1 - min(front, 1)), (0, 0), (0, 0)))

    pos = np.arange(seq)
    onehot = jnp.asarray(pos[:, None] // SEL_BLOCK == np.arange(SEL_LANES)[None, :], BF16)
    ksel = jnp.concatenate([grouped(2), jnp.broadcast_to(onehot, (batch, g_, seq, SEL_LANES))], axis=-1)
    pad_rows = jnp.concatenate([jnp.zeros((kt, dh), BF16), jnp.ones((kt, SEL_LANES), BF16)], axis=-1)
    ksel = jnp.concatenate([ksel, jnp.broadcast_to(pad_rows, (batch, g_, kt, dh + SEL_LANES))], axis=2)
    vselt = tiles_t(grouped(3), 0)

    ext = np.zeros((seq, LANE), np.float32)
    ext[:, WCOL_HI] = (pos + WINDOW) // SEL_BLOCK
    ext[:, WCOL_LO] = (pos + WINDOW) % SEL_BLOCK
    ext[:, WCOL_QHI] = 1.0
    ext[:, WCOL_QLO] = 1.0
    kwin = jnp.concatenate([grouped(4), jnp.broadcast_to(jnp.asarray(ext, BF16), (batch, g_, seq, LANE))],
                           axis=-1)
    front = np.zeros((WINDOW, dh + LANE), np.float32)
    front[:, dh + WCOL_PAD] = 1.0
    kwin = jnp.concatenate([jnp.broadcast_to(jnp.asarray(front, BF16), (batch, g_, WINDOW, dh + LANE)),
                            kwin], axis=2)
    vwint = tiles_t(grouped(5), WIN_TILES)

    rows = SEL_TILES_PER_STEP * kt
    u = jnp.asarray((np.arange(rows) % SEL_BLOCK).astype(np.float32))[None, :, None, None]
    ubias = jnp.broadcast_to(slopes.reshape(g_, 1, NSA_REP, 1) * u, (g_, rows, NSA_REP, Q_BLOCK))
    ubias = ubias.reshape(g_, rows, NSA_REP * Q_BLOCK)
    kq = np.arange(kt)[:, None] <= np.arange(Q_BLOCK)[None, :]
    causal = np.tile(np.where(kq, 0.0, NEG).astype(np.float32), (1, NSA_REP))
    dbias = ubias[:, :kt] + jnp.asarray(causal)[None]
    ki = np.arange(WINDOW + Q_BLOCK)[:, None]
    qi = np.arange(Q_BLOCK)[None, :]
    band = np.where((ki > qi) & (ki <= qi + WINDOW), 0.0, NEG).astype(np.float32)
    wbias = jnp.asarray(np.tile(band, (1, NSA_REP)))
    return ksel, vselt, kwin, vwint, dbias, ubias, wbias


def _shifted(x, tail, s):
    xs = pltpu.roll(x, s, axis=0)
    ts = pltpu.roll(tail, s, axis=0)
    row8 = lax.broadcasted_iota(jnp.int32, (8, 1), 0)
    head = jnp.where(row8 < s, ts, xs[:8])
    return jnp.concatenate([head, xs[8:]], axis=0)


def _conv_silu(x, tail, w, b):
    y = b + _shifted(x, tail, CONV_WIDTH - 1) * w[0:1]
    for i in range(1, CONV_WIDTH - 1):
        y = y + _shifted(x, tail, CONV_WIDTH - 1 - i) * w[i:i + 1]
    y = y + x * w[CONV_WIDTH - 1:CONV_WIDTH]
    return y * _sigmoid(y)


def _log_sigmoid(x):
    return jnp.minimum(x, 0.0) - jnp.log(1.0 + jnp.exp(-jnp.abs(x)))


def _mlstm_kernel(bias_ref, q_ref, k_ref, v_ref, o_ref, ifc_ref, ifr_ref, cw_ref, cb_ref, ng_ref,
                  tri_ref, y_ref, c_ref, n_ref, m_ref, qtail_ref, ktail_ref):
    ch = pl.program_id(1)
    nh, dh = MLSTM_HEADS, MLSTM_HEAD_DIM
    L = q_ref.shape[0]

    @pl.when(ch == 0)
    def _():
        c_ref[...] = jnp.zeros(c_ref.shape, F32)
        n_ref[...] = jnp.zeros(n_ref.shape, F32)
        m_ref[...] = jnp.zeros(m_ref.shape, F32)
        qtail_ref[...] = jnp.zeros(qtail_ref.shape, F32)
        ktail_ref[...] = jnp.zeros(ktail_ref.shape, F32)

    hi = lax.Precision.HIGHEST
    tri = tri_ref[...]
    lane8 = lax.broadcasted_iota(jnp.int32, (1, LANE), 1)
    bias_c = jnp.zeros((1, LANE), F32)
    for h in range(nh):
        bias_c = jnp.where(lane8 == h, bias_ref[h], bias_c)
        bias_c = jnp.where(lane8 == nh + h, bias_ref[nh + h], bias_c)
    pre_c = ifc_ref[...] + bias_c
    cum_c = jnp.dot(tri, _log_sigmoid(pre_c), precision=hi, preferred_element_type=F32)
    row8 = lax.broadcasted_iota(jnp.int32, (8, 1), 0)
    bias_r = jnp.zeros((8, 1), F32)
    for h in range(2 * nh):
        bias_r = jnp.where(row8 == h, bias_ref[h], bias_r)
    pre_r = ifr_ref[...] + bias_r
    cum_r = lax.dot_general(_log_sigmoid(pre_r), tri, (((1,), (1,)), ((), ())),
                            precision=hi, preferred_element_type=F32)

    rr = lax.broadcasted_iota(jnp.int32, (L, 1), 0)
    cc = lax.broadcasted_iota(jnp.int32, (1, L), 1)
    causal = cc <= rr

    q_raw = q_ref[...]
    k_raw = k_ref[...]
    cw = cw_ref[...]
    cb = cb_ref[...]
    qc = _conv_silu(q_raw, qtail_ref[...], cw[:, :nh * dh], cb[:, :nh * dh])
    kc = _conv_silu(k_raw, ktail_ref[...], cw[:, nh * dh:], cb[:, nh * dh:]) * (dh ** -0.5)
    qtail_ref[...] = q_raw[L - 8:]
    ktail_ref[...] = k_raw[L - 8:]

    for h in range(nh):
        cols = slice(h * dh, (h + 1) * dh)
        qh = qc[:, cols]
        kh = kc[:, cols]
        vh = v_ref[:, cols].astype(BF16)
        qb = qh.astype(BF16)
        b_c = cum_c[:, nh + h:nh + h + 1]
        li_c = pre_c[:, h:h + 1]
        b_r = cum_r[nh + h:nh + h + 1, :]
        li_r = pre_r[h:h + 1, :]
        m_prev = m_ref[h:h + 1, 0:1]

        dmat = jnp.where(causal, b_c - b_r + li_r, NEG)
        a = b_c + m_prev
        m_j = jnp.maximum(a, jnp.max(dmat, axis=1, keepdims=True))
        w_intra = jnp.exp(dmat - m_j)
        w_inter = jnp.exp(a - m_j)
        sc = _dot_nt(qb, kh.astype(BF16)) * w_intra
        c_old = c_ref[h]
        n_old = n_ref[h:h + 1, :]
        num = w_inter * _dot(qb, c_old.astype(BF16)) + _dot(sc.astype(BF16), vh)
        den = (w_inter * jnp.sum(qh * n_old, axis=1, keepdims=True)
               + jnp.sum(sc, axis=1, keepdims=True))
        hid = num / jnp.maximum(jnp.abs(den), jnp.exp(-m_j))

        g_tot = b_r[:, L - 1:L]
        lw_c = g_tot - b_c + li_c
        lw_r = g_tot - b_r + li_r
        m_new = jnp.maximum(g_tot + m_prev, jnp.max(lw_r, axis=1, keepdims=True))
        decay = jnp.exp(g_tot + m_prev - m_new)
        kw = jnp.exp(lw_c - m_new) * kh
        c_ref[h] = decay * c_old + _dot(kw.T.astype(BF16), vh)
        n_ref[h:h + 1, :] = decay * n_old + jnp.sum(kw, axis=0, keepdims=True)
        m_ref[h:h + 1, :] = jnp.broadcast_to(m_new, (1, LANE))

        hn = hid * lax.rsqrt(jnp.mean(hid * hid, axis=-1, keepdims=True) + EPS) * ng_ref[:, cols]
        y_ref[:, cols] = (_sigmoid(o_ref[:, cols]) * hn).astype(y_ref.dtype)


def _mlstm(seg_b, if_rows, bias, conv_w, conv_b, norm_g, tri, batch, seq, chunk):
    n = batch * seq
    nc = seq // chunk
    w = MLSTM_WIDTH
    nh, dh = MLSTM_HEADS, MLSTM_HEAD_DIM

    def col_spec(off):
        return pl.BlockSpec((chunk, w), lambda b, c, o=off // w: (b * nc + c, o))

    return pl.pallas_call(
        _mlstm_kernel,
        grid=(batch, nc),
        in_specs=[
            pl.BlockSpec(memory_space=pltpu.SMEM),
            col_spec(B_QK), col_spec(B_QK + w), col_spec(B_V), col_spec(B_O),
            pl.BlockSpec((chunk, LANE), lambda b, c: (b * nc + c, B_IF // LANE)),
            pl.BlockSpec((8, chunk), lambda b, c: (0, b * nc + c)),
            pl.BlockSpec((CONV_WIDTH, 2 * w), lambda b, c: (0, 0)),
            pl.BlockSpec((1, 2 * w), lambda b, c: (0, 0)),
            pl.BlockSpec((1, w), lambda b, c: (0, 0)),
            pl.BlockSpec((chunk, chunk), lambda b, c: (0, 0)),
        ],
        out_specs=pl.BlockSpec((chunk, w), lambda b, c: (b * nc + c, 0)),
        out_shape=jax.ShapeDtypeStruct((n, w), BF16),
        scratch_shapes=[
            pltpu.VMEM((nh, dh, dh), F32),
            pltpu.VMEM((8, dh), F32),
            pltpu.VMEM((8, LANE), F32),
            pltpu.VMEM((8, w), F32),
            pltpu.VMEM((8, w), F32),
        ],
        compiler_params=pltpu.CompilerParams(
            dimension_semantics=("parallel", "arbitrary"), vmem_limit_bytes=VMEM_LIMIT),
        name="mlstm",
    )(bias, seg_b, seg_b, seg_b, seg_b, seg_b, if_rows, conv_w, conv_b, norm_g, tri)


def _outproj_kernel(x_ref, ya_ref, ym_ref, wa_ref, wm_ref, o_ref):
    o_ref[...] = x_ref[...] + _dot(ya_ref[...], wa_ref[...]) + _dot(ym_ref[...], wm_ref[...])


def _outproj(x2, ya, ym, wa, wm, tm):
    n, d = x2.shape
    return pl.pallas_call(
        _outproj_kernel,
        grid=(n // tm,),
        in_specs=[
            pl.BlockSpec((tm, d), lambda i: (i, 0)),
            pl.BlockSpec((tm, ya.shape[1]), lambda i: (i, 0)),
            pl.BlockSpec((tm, ym.shape[1]), lambda i: (i, 0)),
            pl.BlockSpec(wa.shape, lambda i: (0, 0)),
            pl.BlockSpec(wm.shape, lambda i: (0, 0)),
        ],
        out_specs=pl.BlockSpec((tm, d), lambda i: (i, 0)),
        out_shape=jax.ShapeDtypeStruct((n, d), F32),
        compiler_params=pltpu.CompilerParams(
            dimension_semantics=("parallel",), vmem_limit_bytes=VMEM_LIMIT),
        name="outproj",
    )(x2, ya, ym, wa, wm)


def _mlp_kernel(x_ref, g_ref, w1_ref, w2_ref, gf_ref, o_ref, h_ref):
    f = pl.program_id(1)

    @pl.when(f == 0)
    def _():
        x = x_ref[...]
        r = lax.rsqrt(jnp.mean(x * x, axis=-1, keepdims=True) + EPS)
        h_ref[...] = (x * r * g_ref[...]).astype(BF16)
        o_ref[...] = x

    u = jnp.maximum(_dot(h_ref[...], w1_ref[...]), 0.0)
    o_ref[...] += _dot((u * u).astype(BF16), w2_ref[...])

    @pl.when(f == pl.num_programs(1) - 1)
    def _():
        x2 = o_ref[...]
        r = lax.rsqrt(jnp.mean(x2 * x2, axis=-1, keepdims=True) + EPS)
        o_ref[...] = x2 * r * gf_ref[...]


def _mlp(x1, g, w1, w2, gf, tm, tf):
    n, d = x1.shape
    dff = w1.shape[1]
    return pl.pallas_call(
        _mlp_kernel,
        grid=(n // tm, dff // tf),
        in_specs=[
            pl.BlockSpec((tm, d), lambda i, f: (i, 0)),
            pl.BlockSpec((1, d), lambda i, f: (0, 0)),
            pl.BlockSpec((d, tf), lambda i, f: (0, f)),
            pl.BlockSpec((tf, d), lambda i, f: (f, 0)),
            pl.BlockSpec((1, d), lambda i, f: (0, 0)),
        ],
        out_specs=pl.BlockSpec((tm, d), lambda i, f: (i, 0)),
        out_shape=jax.ShapeDtypeStruct((n, d), F32),
        scratch_shapes=[pltpu.VMEM((tm, d), BF16)],
        compiler_params=pltpu.CompilerParams(
            dimension_semantics=("parallel", "arbitrary"), vmem_limit_bytes=VMEM_LIMIT),
        name="mlp",
    )(x1, g, w1, w2, gf)


def _row_tile(n, want):
    t = want
    while n % t:
        t //= 2
    return t


def _layer(x2, batch, seq, norm_mix_g, w_in, w_cmp_k1, w_cmp_k2, pos_cmp_k, w_cmp_v1, w_cmp_v2,
           pos_cmp_v, conv_w, conv_b, b_igate, b_fgate, mlstm_norm_g, w_out, norm_mlp_g,
           w_mlp_in, w_mlp_out):
    n, d = x2.shape
    assert seq % Q_BLOCK == 0 and seq >= WINDOW + Q_BLOCK and seq // SEL_BLOCK <= SEL_LANES
    g_ = NSA_KV_GROUPS
    nh = MLSTM_HEADS

    c_gate = SEG_A
    c_qk = c_gate + NSA_HEADS * N_BRANCH
    c_i = c_qk + 4 * MLSTM_WIDTH
    c_f = c_i + nh
    w_a = w_in[:, :SEG_A].astype(BF16)
    gate_cols = []
    per_g = NSA_REP * N_BRANCH
    for g in range(g_):
        gate_cols += [w_in[:, c_gate + g * per_g:c_gate + (g + 1) * per_g],
                      jnp.zeros((d, LANE - per_g), w_in.dtype)]
    w_b = jnp.concatenate(
        [w_in[:, c_qk:c_i]] + gate_cols
        + [w_in[:, c_i:c_f + nh], jnp.zeros((d, SEG_B - B_IF - 2 * nh), w_in.dtype)],
        axis=1).astype(BF16)
    scale_a = jnp.concatenate([jnp.full((1, NSA_WIDTH), NSA_HEAD_DIM ** -0.5, F32),
                               jnp.ones((1, SEG_A - NSA_WIDTH), F32)], axis=1)
    scale_b = jnp.ones((1, SEG_B), F32)
    g_mix = norm_mix_g.reshape(1, d)

    tm = _row_tile(n, ROW_TILE)
    seg_a = _inproj(x2, g_mix, w_a, scale_a, BF16, tm, 512)
    seg_b = _inproj(x2, g_mix, w_b, scale_b, F32, tm, 768)

    n_sub = seq // CMP_STRIDE
    cmp_in = seg_a[:, NSA_WIDTH:NSA_WIDTH + 2 * NSA_KV_WIDTH]
    cmp_in = cmp_in.reshape(batch, n_sub, CMP_STRIDE, 2 * g_, NSA_HEAD_DIM)
    cmp_in = cmp_in.transpose(0, 3, 1, 2, 4).reshape(batch, 2 * g_, n_sub, CMP_STRIDE * NSA_HEAD_DIM)
    w1s = jnp.stack([w_cmp_k1, w_cmp_v1]).astype(BF16)
    w2s = jnp.stack([w_cmp_k2, w_cmp_v2]).astype(BF16)
    poss = jnp.stack([pos_cmp_k, pos_cmp_v]).reshape(2, 1, CMP_BLOCK * NSA_HEAD_DIM)
    poss = jnp.broadcast_to(poss, (2, 8, CMP_BLOCK * NSA_HEAD_DIM)).astype(BF16)
    kvc, kvct = _compress(cmp_in, w1s, w2s, poss)

    cmp_start = np.arange(n_sub) * CMP_STRIDE
    sel_start = np.arange(SEL_LANES) * SEL_BLOCK
    ovt = ((cmp_start[None, :] < sel_start[:, None] + SEL_BLOCK)
           & (cmp_start[None, :] + CMP_BLOCK - 1 >= sel_start[:, None])
           & (np.arange(n_sub)[None, :] < n_sub - CMP_BLOCK // CMP_STRIDE + 1))
    ovt = jnp.asarray(ovt, BF16)
    slopes = jnp.exp2(-8.0 * jnp.arange(1, NSA_HEADS + 1, dtype=F32) / NSA_HEADS)
    ocmp, pen, flags = _nsa_cmp(seg_a, kvc, kvct, ovt, slopes, batch, seq)
    ksel, vselt, kwin, vwint, dbias, ubias, wbias = _nsa_operands(seg_a, slopes, batch, seq)
    y_a = _nsa_attn(seg_a, seg_b, pen, ocmp, flags, ksel, vselt, kwin, vwint, dbias, ubias, wbias,
                    slopes, batch, seq)

    chunk = 256 if seq % 256 == 0 else 128
    if_rows = seg_b[:, B_IF:B_IF + 2 * nh].T
    bias = jnp.concatenate([b_igate, b_fgate]).astype(F32)
    tri = jnp.asarray(np.tril(np.ones((chunk, chunk), np.float32)))
    y_m = _mlstm(seg_b, if_rows, bias, conv_w, conv_b.reshape(1, -1), mlstm_norm_g.reshape(1, -1),
                 tri, batch, seq, chunk)

    w_o = w_out.astype(BF16)
    x1 = _outproj(x2, y_a, y_m, w_o[:NSA_WIDTH], w_o[NSA_WIDTH:], _row_tile(n, 512))
    return x1, (norm_mlp_g.reshape(1, d), w_mlp_in.astype(BF16), w_mlp_out.astype(BF16))


def kernel(x, norm_mix_g, w_in, w_cmp_k1, w_cmp_k2, pos_cmp_k, w_cmp_v1, w_cmp_v2, pos_cmp_v, conv_w, conv_b, b_igate, b_fgate, mlstm_norm_g, w_out, norm_mlp_g, w_mlp_in, w_mlp_out, norm_f_g):
    batch, seq, d = x.shape
    depth = w_in.shape[0]
    assert depth == 1, "the final RMSNorm is fused into the last layer's channel mixer"
    x2 = x.reshape(batch * seq, d)
    tm = _row_tile(batch * seq, ROW_TILE)
    for l in range(depth):
        x1, (g_mlp, w1, w2) = _layer(
            x2, batch, seq, norm_mix_g[l], w_in[l], w_cmp_k1[l], w_cmp_k2[l], pos_cmp_k[l],
            w_cmp_v1[l], w_cmp_v2[l], pos_cmp_v[l], conv_w[l], conv_b[l], b_igate[l], b_fgate[l],
            mlstm_norm_g[l], w_out[l], norm_mlp_g[l], w_mlp_in[l], w_mlp_out[l])
        x2 = _mlp(x1, g_mlp, w1, w2, norm_f_g.reshape(1, d), tm, 512)
    return x2.reshape(batch, seq, d)
```

```python
import functools

import numpy as np
import jax
import jax.numpy as jnp
from jax import lax
from jax.experimental import pallas as pl
from jax.experimental.pallas import tpu as pltpu

F32 = jnp.float32
BF16 = jnp.bfloat16

EPS = 1e-6
NEG = -1e30
FORCE_BONUS = 1e4
PICKED = -3e38
MASK_BIG = 1e30

D_MODEL = 2048
NSA_HEAD_DIM = 128
NSA_WIDTH = D_MODEL // 2
NSA_HEADS = NSA_WIDTH // NSA_HEAD_DIM
NSA_REP = 4
NSA_KV_GROUPS = NSA_HEADS // NSA_REP
NSA_KV_WIDTH = NSA_KV_GROUPS * NSA_HEAD_DIM
CMP_BLOCK = 32
CMP_STRIDE = 16
SEL_BLOCK = 64
SEL_TOPK = 16
WINDOW = 512
Q_BLOCK = 128
N_BRANCH = 3
MLSTM_HEAD_DIM = 256
MLSTM_WIDTH = D_MODEL - NSA_WIDTH
MLSTM_HEADS = MLSTM_WIDTH // MLSTM_HEAD_DIM
CONV_WIDTH = 4
D_FF = 4 * D_MODEL

LANE = 128
SEL_LANES = 128
KEY_TILE = 128
SEL_TILES_PER_STEP = 2
WIN_TILES = WINDOW // KEY_TILE
VMEM_LIMIT = 56 * 1024 * 1024
ROW_TILE = 1024

A_V, A_QK = 0, MLSTM_WIDTH
A_Q = A_QK + 2 * MLSTM_WIDTH
A_KV = A_Q + NSA_WIDTH
SEG_A = A_KV + 6 * NSA_KV_WIDTH
B_O = 0
B_GATE = MLSTM_WIDTH
B_IF = B_GATE + NSA_KV_GROUPS * LANE
SEG_B = B_IF + 2 * LANE
COL_TILE = 512

WCOL_PAD, WCOL_HI, WCOL_LO, WCOL_QHI, WCOL_QLO = 0, 1, 2, 3, 4


def _dot(a, b):
    return jnp.dot(a, b, preferred_element_type=F32)


def _dot_nt(a, b):
    return lax.dot_general(a, b, (((1,), (1,)), ((), ())), preferred_element_type=F32)


def _sigmoid(x):
    return 1.0 / (1.0 + jnp.exp(-x))


def _shifted(x, tail, s):
    xs = pltpu.roll(x, s, axis=0)
    ts = pltpu.roll(tail, s, axis=0)
    row8 = lax.broadcasted_iota(jnp.int32, (8, 1), 0)
    head = jnp.where(row8 < s, ts, xs[:8])
    return jnp.concatenate([head, xs[8:]], axis=0)


def _conv_silu(x, tail, w, b):
    y = b + _shifted(x, tail, CONV_WIDTH - 1) * w[0:1]
    for i in range(1, CONV_WIDTH - 1):
        y = y + _shifted(x, tail, CONV_WIDTH - 1 - i) * w[i:i + 1]
    y = y + x * w[CONV_WIDTH - 1:CONV_WIDTH]
    return y * _sigmoid(y)


def _inproj_kernel(x_ref, g_ref, w_ref, cs_ref, cw_ref, cb_ref, o_ref, h_ref, halo_ref, *,
                   conv_lo, conv_hi, tiles_per_seq):
    i = pl.program_id(0)
    j = pl.program_id(1)

    @pl.when(j == 0)
    def _():
        x = x_ref[...]
        r = lax.rsqrt(jnp.mean(x * x, axis=-1, keepdims=True) + EPS)
        h_ref[...] = (x * r * g_ref[...]).astype(BF16)

    is_conv = (j >= conv_lo) & (j < conv_hi)

    @pl.when(jnp.logical_not(is_conv))
    def _():
        o_ref[...] = (_dot(h_ref[...], w_ref[...]) * cs_ref[...]).astype(o_ref.dtype)

    if conv_hi > conv_lo:
        @pl.when(is_conv)
        def _():
            slot = j - conv_lo

            @pl.when(i % tiles_per_seq == 0)
            def _():
                halo_ref[slot] = jnp.zeros(halo_ref.shape[1:], F32)

            acc = _dot(h_ref[...], w_ref[...])
            tail = halo_ref[slot]
            halo_ref[slot] = acc[acc.shape[0] - 8:]
            y = _conv_silu(acc, tail, cw_ref[...], cb_ref[...])
            o_ref[...] = (y * cs_ref[...]).astype(o_ref.dtype)


def _inproj(x2, g, w, cs, cw, cb, out_dtype, tm, tn, conv_cols, seq):
    n, d = x2.shape
    nc = w.shape[1]
    conv_lo, conv_hi = conv_cols[0] // tn, conv_cols[1] // tn
    assert conv_cols[0] % tn == 0 and conv_cols[1] % tn == 0 and seq % tm == 0
    return pl.pallas_call(
        functools.partial(_inproj_kernel, conv_lo=conv_lo, conv_hi=conv_hi, tiles_per_seq=seq // tm),
        grid=(n // tm, nc // tn),
        in_specs=[
            pl.BlockSpec((tm, d), lambda i, j: (i, 0)),
            pl.BlockSpec((1, d), lambda i, j: (0, 0)),
            pl.BlockSpec((d, tn), lambda i, j: (0, j)),
            pl.BlockSpec((1, tn), lambda i, j: (0, j)),
            pl.BlockSpec((CONV_WIDTH, tn), lambda i, j: (0, j)),
            pl.BlockSpec((1, tn), lambda i, j: (0, j)),
        ],
        out_specs=pl.BlockSpec((tm, tn), lambda i, j: (i, j)),
        out_shape=jax.ShapeDtypeStruct((n, nc), out_dtype),
        scratch_shapes=[pltpu.VMEM((tm, d), BF16),
                        pltpu.VMEM((max(conv_hi - conv_lo, 1), 8, tn), F32)],
        compiler_params=pltpu.CompilerParams(
            dimension_semantics=("arbitrary", "arbitrary"), vmem_limit_bytes=VMEM_LIMIT),
        name="inproj",
    )(x2, g, w, cs, cw, cb)


def _compress_kernel(x_ref, w1_ref, w2_ref, pos_ref, o_ref, ot_ref):
    x = x_ref[0, 0]
    n_sub = x.shape[0]
    half = CMP_STRIDE * NSA_HEAD_DIM
    w1 = w1_ref[0]
    top = _dot(x, w1[:half])
    bot = _dot(x, w1[half:])
    bot = pltpu.roll(bot, n_sub - 1, axis=0)
    posw = _dot(pos_ref[0], w1)[0:1]
    pre = top + bot + posw
    hid = pre * _sigmoid(pre)
    out = _dot(hid.astype(BF16), w2_ref[0])
    o_ref[0, 0] = out.astype(o_ref.dtype)
    ot_ref[0, 0] = out.T.astype(ot_ref.dtype)


def _compress(xs, w1s, w2s, poss):
    b, c, n_sub, feat = xs.shape
    g = NSA_KV_GROUPS
    return pl.pallas_call(
        _compress_kernel,
        grid=(b, c),
        in_specs=[
            pl.BlockSpec((1, 1, n_sub, feat), lambda i, j: (i, j, 0, 0)),
            pl.BlockSpec((1, 2 * feat, NSA_HEAD_DIM), lambda i, j: (j // g, 0, 0)),
            pl.BlockSpec((1, NSA_HEAD_DIM, NSA_HEAD_DIM), lambda i, j: (j // g, 0, 0)),
            pl.BlockSpec((1, 8, 2 * feat), lambda i, j: (j // g, 0, 0)),
        ],
        out_specs=[pl.BlockSpec((1, 1, n_sub, NSA_HEAD_DIM), lambda i, j: (i, j, 0, 0)),
                   pl.BlockSpec((1, 1, NSA_HEAD_DIM, n_sub), lambda i, j: (i, j, 0, 0))],
        out_shape=[jax.ShapeDtypeStruct((b, c, n_sub, NSA_HEAD_DIM), BF16),
                   jax.ShapeDtypeStruct((b, c, NSA_HEAD_DIM, n_sub), BF16)],
        compiler_params=pltpu.CompilerParams(
            dimension_semantics=("parallel", "parallel"), vmem_limit_bytes=VMEM_LIMIT),
        name="compress",
    )(xs, w1s, w2s, poss)


def _stack_heads(q_all):
    dh = NSA_HEAD_DIM
    return jnp.concatenate([q_all[:, r * dh:(r + 1) * dh] for r in range(NSA_REP)], axis=0)


def _nsa_cmp_kernel(slopes_ref, q_ref, kc_ref, vct_ref, ovt_ref, ocmp_ref, pen_ref, flag_ref, *,
                    seq, sub_blocks):
    for sub in range(sub_blocks):
        rows = slice(sub * Q_BLOCK, (sub + 1) * Q_BLOCK)
        _nsa_cmp_block(slopes_ref, q_ref[rows, :], kc_ref, vct_ref, ovt_ref,
                       ocmp_ref.at[0, 0, sub], pen_ref.at[0, 0, rows], flag_ref.at[0, 0, sub],
                       pl.program_id(2) * sub_blocks + sub, seq)


def _nsa_cmp_block(slopes_ref, q_all, kc_ref, vct_ref, ovt_ref, ocmp_ref, pen_ref, flag_ref, qb, seq):
    g = pl.program_id(1)
    nq = Q_BLOCK
    n_cpad = kc_ref.shape[2]
    n_cmp = seq // CMP_STRIDE - CMP_BLOCK // CMP_STRIDE + 1
    t0 = qb * nq

    s_t = _dot_nt(kc_ref[0, 0], _stack_heads(q_all))
    n_s = lax.broadcasted_iota(jnp.int32, (n_cpad, 1), 0)
    q_l = lax.broadcasted_iota(jnp.int32, (1, nq), 1)
    dist = (t0 - (CMP_BLOCK - 1)) + q_l - n_s * CMP_STRIDE
    valid = (dist >= 0) & (n_s < n_cmp)
    dist_f = dist.astype(F32)
    probs = []
    p_sum = jnp.zeros((n_cpad, nq), F32)
    for r in range(NSA_REP):
        slope = slopes_ref[g * NSA_REP + r]
        s = jnp.where(valid, s_t[:, r * nq:(r + 1) * nq] - slope * dist_f, NEG)
        m = jnp.max(s, axis=0, keepdims=True)
        e = jnp.exp(s - m)
        inv = jnp.where(m > 0.5 * NEG, 1.0 / jnp.sum(e, axis=0, keepdims=True), 0.0)
        p = e * inv
        probs.append(p.astype(BF16))
        p_sum = p_sum + p
    ocmp_ref[...] = _dot(vct_ref[0, 0], jnp.concatenate(probs, axis=1))

    p_hi = p_sum.astype(BF16)
    p_lo = (p_sum - p_hi.astype(F32)).astype(BF16)
    ovt = ovt_ref[...]
    imp = _dot(ovt, p_hi) + _dot(ovt, p_lo)
    j_i = lax.broadcasted_iota(jnp.int32, (SEL_LANES, 1), 0)
    t_l = t0 + q_l
    cur = t_l // SEL_BLOCK
    forced = (j_i == 0) | (j_i == cur) | (j_i == cur - 1)
    causal_blk = j_i * SEL_BLOCK <= t_l
    val = jnp.where(causal_blk, jnp.where(forced, imp + FORCE_BONUS, imp), NEG)
    j_f = j_i.astype(F32)
    sel_t = jnp.zeros((SEL_LANES, nq), F32)
    for _ in range(min(SEL_TOPK, seq // SEL_BLOCK)):
        mx = jnp.max(val, axis=0, keepdims=True)
        first = jnp.min(jnp.where(val == mx, j_f, float(SEL_LANES)), axis=0, keepdims=True)
        pick = j_f == first
        sel_t = jnp.where(pick, 1.0, sel_t)
        val = jnp.where(pick, PICKED, val)
    sel = sel_t.T
    pen_ref[...] = ((sel - 1.0) * MASK_BIG).astype(pen_ref.dtype)
    flag_ref[...] = (jnp.max(sel, axis=0, keepdims=True) > 0.0).astype(jnp.int32)


def _nsa_cmp(seg_a, kvc, kvct, ovt, slopes, batch, seq):
    nqb = seq // Q_BLOCK
    gq = NSA_REP * NSA_HEAD_DIM
    g_ = NSA_KV_GROUPS
    n_cpad = kvc.shape[2]
    sub = 2 if nqb % 2 == 0 else 1
    nstep = nqb // sub
    return pl.pallas_call(
        functools.partial(_nsa_cmp_kernel, seq=seq, sub_blocks=sub),
        grid=(batch, g_, nstep),
        in_specs=[
            pl.BlockSpec(memory_space=pltpu.SMEM),
            pl.BlockSpec((sub * Q_BLOCK, gq), lambda b, g, q: (b * nstep + q, A_Q // gq + g)),
            pl.BlockSpec((1, 1, n_cpad, NSA_HEAD_DIM), lambda b, g, q: (b, g, 0, 0)),
            pl.BlockSpec((1, 1, NSA_HEAD_DIM, n_cpad), lambda b, g, q: (b, g_ + g, 0, 0)),
            pl.BlockSpec((SEL_LANES, n_cpad), lambda b, g, q: (0, 0)),
        ],
        out_specs=[
            pl.BlockSpec((1, 1, sub, NSA_HEAD_DIM, gq), lambda b, g, q: (b, g, q, 0, 0)),
            pl.BlockSpec((1, 1, sub * Q_BLOCK, SEL_LANES), lambda b, g, q: (b, g, q, 0)),
            pl.BlockSpec((1, 1, sub, 1, SEL_LANES), lambda b, g, q: (b, g, q, 0, 0)),
        ],
        out_shape=[
            jax.ShapeDtypeStruct((batch, g_, nqb, NSA_HEAD_DIM, gq), F32),
            jax.ShapeDtypeStruct((batch, g_, seq, SEL_LANES), BF16),
            jax.ShapeDtypeStruct((batch, g_, nqb, 1, SEL_LANES), jnp.int32),
        ],
        compiler_params=pltpu.CompilerParams(
            dimension_semantics=("parallel", "parallel", "parallel"), vmem_limit_bytes=VMEM_LIMIT),
        name="nsa_cmp",
    )(slopes, seg_a, kvc, kvct, ovt)


def _nsa_attn_kernel(slopes_ref, flag_ref, q_ref, gate_ref, pen_ref, ocmp_ref, ks_ref, vs_ref,
                     kw_ref, vw_ref, onehot_ref, wext_ref, dbias_ref, ubias_ref, wbias_ref, o_ref,
                     ksel_ref, vselt_ref, kwin_ref, vwint_ref,
                     qa_ref, qw_ref, list_ref, m_ref, l_ref, acc_ref, owin_ref, sa_ref, sb_ref,
                     *, seq):
    g = pl.program_id(1)
    qb = pl.program_id(2)
    dh = NSA_HEAD_DIM
    nq = Q_BLOCK
    kt = KEY_TILE
    per = SEL_TILES_PER_STEP
    pad_tile = seq // kt
    t0 = qb * nq

    @pl.when(qb == 0)
    def _():
        ksel_ref[0:seq, :dh] = ks_ref[...]
        ksel_ref[0:seq, dh:] = onehot_ref[...]
        ksel_ref[seq:, :dh] = jnp.zeros((kt, dh), BF16)
        ksel_ref[seq:, dh:] = jnp.ones((kt, SEL_LANES), BF16)
        lane2 = lax.broadcasted_iota(jnp.int32, (WINDOW, dh + LANE), 1)
        kwin_ref[0:WINDOW, :] = jnp.where(lane2 == dh + WCOL_PAD, 1.0, 0.0).astype(BF16)
        kwin_ref[WINDOW:, :dh] = kw_ref[...]
        kwin_ref[WINDOW:, dh:] = wext_ref[...]
        zero_tile = jnp.zeros((dh, kt), BF16)
        vselt_ref[pad_tile] = zero_tile
        for i in range(WIN_TILES):
            vwint_ref[i] = zero_tile

        def transpose_tile(t, carry):
            r0 = pl.multiple_of(t * kt, kt)
            vselt_ref[t] = vs_ref[pl.ds(r0, kt), :].astype(F32).T.astype(BF16)
            vwint_ref[t + WIN_TILES] = vw_ref[pl.ds(r0, kt), :].astype(F32).T.astype(BF16)
            return carry

        lax.fori_loop(0, seq // kt, transpose_tile, 0)

    slopes = [slopes_ref[g * NSA_REP + r] for r in range(NSA_REP)]
    row_q = lax.broadcasted_iota(jnp.int32, (nq, 1), 0)
    lane = lax.broadcasted_iota(jnp.int32, (1, LANE), 1)
    q_all = q_ref[...]

    pen = pen_ref[0, 0].astype(F32)
    blk_rel = ((lane - (t0 + row_q) // SEL_BLOCK) * SEL_BLOCK).astype(F32)
    for r in range(NSA_REP):
        qa_ref[r * nq:(r + 1) * nq, :dh] = q_all[:, r * dh:(r + 1) * dh]
        qa_ref[r * nq:(r + 1) * nq, dh:] = (pen + slopes[r] * blk_rel).astype(BF16)

    def scan(k, cnt):
        for j in range(2):
            i = 2 * k + j
            act = ((flag_ref[0, 0, 0, 0, 2 * i] + flag_ref[0, 0, 0, 0, 2 * i + 1]) > 0) & (i < qb)
            list_ref[cnt] = i
            cnt = cnt + act.astype(jnp.int32)
        return cnt

    cnt = lax.fori_loop(0, (qb + 1) // 2, scan, 0)
    for i in range(4 * per):
        list_ref[cnt + i] = pad_tile

    def group_scores(it):
        keys = jnp.concatenate(
            [ksel_ref[pl.ds(pl.multiple_of(list_ref[it * per + i] * kt, kt), kt), :]
             for i in range(per)], axis=0)
        return _dot_nt(keys, qa_ref[...]) + ubias_ref[0]

    s = _dot_nt(ksel_ref[pl.ds(pl.multiple_of(t0, kt), kt), :], qa_ref[...]) + dbias_ref[0]
    m0 = jnp.max(s, axis=0, keepdims=True)
    p = jnp.exp(s - m0)
    m_ref[...] = m0
    l_ref[...] = jnp.sum(p, axis=0, keepdims=True)
    acc_ref[...] = _dot(vselt_ref[qb], p.astype(BF16))
    sa_ref[...] = group_scores(0)

    tp = t0 + WINDOW + row_q
    t_hi = (tp // SEL_BLOCK).astype(F32)
    t_lo = (tp % SEL_BLOCK).astype(F32)
    for r in range(NSA_REP):
        sl = slopes[r]
        ext = jnp.where(lane == WCOL_PAD, -MASK_BIG, 0.0)
        ext = jnp.where(lane == WCOL_HI, sl * SEL_BLOCK, ext)
        ext = jnp.where(lane == WCOL_LO, sl, ext)
        ext = jnp.where(lane == WCOL_QHI, -sl * SEL_BLOCK * t_hi, ext)
        ext = jnp.where(lane == WCOL_QLO, -sl * t_lo, ext)
        qw_ref[r * nq:(r + 1) * nq, :dh] = q_all[:, r * dh:(r + 1) * dh]
        qw_ref[r * nq:(r + 1) * nq, dh:] = ext.astype(BF16)
    wlen = WINDOW + nq
    s = _dot_nt(kwin_ref[pl.ds(pl.multiple_of(t0, kt), wlen), :], qw_ref[...]) + wbias_ref[...]
    e = jnp.exp(s - jnp.max(s, axis=0, keepdims=True))
    vwin = jnp.concatenate([vwint_ref[qb + i] for i in range(WIN_TILES + 1)], axis=1)
    owin_ref[...] = _dot(vwin, e.astype(BF16)) / jnp.sum(e, axis=0, keepdims=True)

    def absorb(s, grp):
        vals = jnp.concatenate([vselt_ref[list_ref[grp * per + i]] for i in range(per)], axis=1)
        m_old = m_ref[...]
        m_new = jnp.maximum(m_old, jnp.max(s, axis=0, keepdims=True))
        alpha = jnp.exp(m_old - m_new)
        p = jnp.exp(s - m_new)
        l_ref[...] = alpha * l_ref[...] + jnp.sum(p, axis=0, keepdims=True)
        acc_ref[...] = alpha * acc_ref[...] + _dot(vals, p.astype(BF16))
        m_ref[...] = m_new

    def sel_body(it, carry):
        s = sa_ref[...]
        sb_ref[...] = group_scores(2 * it + 1)
        absorb(s, 2 * it)
        s = sb_ref[...]
        sa_ref[...] = group_scores(2 * it + 2)
        absorb(s, 2 * it + 1)
        return carry

    lax.fori_loop(0, (cnt + 2 * per - 1) // (2 * per), sel_body, 0)
    o_sel = acc_ref[...] / l_ref[...]
    o_win = owin_ref[...]

    gate_t = _sigmoid(gate_ref[...]).T
    o_cmp = ocmp_ref[0, 0, 0]
    for r in range(NSA_REP):
        cols = slice(r * nq, (r + 1) * nq)
        c0 = N_BRANCH * r
        out_t = (gate_t[c0:c0 + 1] * o_cmp[:, cols] + gate_t[c0 + 1:c0 + 2] * o_sel[:, cols]
                 + gate_t[c0 + 2:c0 + 3] * o_win[:, cols])
        o_ref[:, r * dh:(r + 1) * dh] = out_t.T.astype(o_ref.dtype)


def _nsa_attn(seg_a, seg_b, pen, ocmp, flags, onehot, wext, dbias, ubias, wbias, slopes, batch, seq):
    n = batch * seq
    nqb = seq // Q_BLOCK
    gq = NSA_REP * NSA_HEAD_DIM
    g_ = NSA_KV_GROUPS
    dh, kt = NSA_HEAD_DIM, KEY_TILE
    dk = dh + SEL_LANES
    rq = NSA_REP * Q_BLOCK
    n_t = seq // kt

    def kv_spec(kind):
        return pl.BlockSpec((seq, dh), lambda b, g, q, k=kind: (b, A_KV // dh + k * g_ + g))

    def const_spec(arr):
        return pl.BlockSpec(arr.shape, lambda b, g, q, nd=arr.ndim: (0,) * nd)

    return pl.pallas_call(
        functools.partial(_nsa_attn_kernel, seq=seq),
        grid=(batch, g_, nqb),
        in_specs=[
            pl.BlockSpec(memory_space=pltpu.SMEM),
            pl.BlockSpec((1, 1, 1, 1, SEL_LANES), lambda b, g, q: (b, g, q, 0, 0),
                         memory_space=pltpu.SMEM),
            pl.BlockSpec((Q_BLOCK, gq), lambda b, g, q: (b * nqb + q, A_Q // gq + g)),
            pl.BlockSpec((Q_BLOCK, LANE), lambda b, g, q: (b * nqb + q, B_GATE // LANE + g)),
            pl.BlockSpec((1, 1, Q_BLOCK, SEL_LANES), lambda b, g, q: (b, g, q, 0)),
            pl.BlockSpec((1, 1, 1, NSA_HEAD_DIM, gq), lambda b, g, q: (b, g, q, 0, 0)),
            kv_spec(2), kv_spec(3), kv_spec(4), kv_spec(5),
            const_spec(onehot), const_spec(wext),
            pl.BlockSpec((1,) + dbias.shape[1:], lambda b, g, q: (g, 0, 0)),
            pl.BlockSpec((1,) + ubias.shape[1:], lambda b, g, q: (g, 0, 0)),
            const_spec(wbias),
        ],
        out_specs=pl.BlockSpec((Q_BLOCK, gq), lambda b, g, q: (b * nqb + q, g)),
        out_shape=jax.ShapeDtypeStruct((n, NSA_WIDTH), BF16),
        scratch_shapes=[
            pltpu.VMEM((seq + kt, dk), BF16),
            pltpu.VMEM((n_t + 1, dh, kt), BF16),
            pltpu.VMEM((seq + WINDOW, dh + LANE), BF16),
            pltpu.VMEM((n_t + WIN_TILES, dh, kt), BF16),
            pltpu.VMEM((rq, dk), BF16),
            pltpu.VMEM((rq, dk), BF16),
            pltpu.SMEM((n_t + 4 * SEL_TILES_PER_STEP,), jnp.int32),
            pltpu.VMEM((1, rq), F32),
            pltpu.VMEM((1, rq), F32),
            pltpu.VMEM((NSA_HEAD_DIM, rq), F32),
            pltpu.VMEM((NSA_HEAD_DIM, rq), F32),
            pltpu.VMEM((SEL_TILES_PER_STEP * KEY_TILE, rq), F32),
            pltpu.VMEM((SEL_TILES_PER_STEP * KEY_TILE, rq), F32),
        ],
        compiler_params=pltpu.CompilerParams(
            dimension_semantics=("parallel", "parallel", "arbitrary"), vmem_limit_bytes=VMEM_LIMIT),
        name="nsa_attn",
    )(slopes, flags, seg_a, seg_b, pen, ocmp, seg_a, seg_a, seg_a, seg_a, onehot, wext,
      dbias, ubias, wbias)


def _nsa_tables(slopes, seq):
    g_, kt = NSA_KV_GROUPS, KEY_TILE
    pos = np.arange(seq)
    onehot = jnp.asarray(pos[:, None] // SEL_BLOCK == np.arange(SEL_LANES)[None, :], BF16)
    ext = np.zeros((seq, LANE), np.float32)
    ext[:, WCOL_HI] = (pos + WINDOW) // SEL_BLOCK
    ext[:, WCOL_LO] = (pos + WINDOW) % SEL_BLOCK
    ext[:, WCOL_QHI] = 1.0
    ext[:, WCOL_QLO] = 1.0
    wext = jnp.asarray(ext, BF16)

    rows = SEL_TILES_PER_STEP * kt
    u = jnp.asarray((np.arange(rows) % SEL_BLOCK).astype(np.float32))[None, :, None, None]
    ubias = jnp.broadcast_to(slopes.reshape(g_, 1, NSA_REP, 1) * u, (g_, rows, NSA_REP, Q_BLOCK))
    ubias = ubias.reshape(g_, rows, NSA_REP * Q_BLOCK)
    kq = np.arange(kt)[:, None] <= np.arange(Q_BLOCK)[None, :]
    causal = np.tile(np.where(kq, 0.0, NEG).astype(np.float32), (1, NSA_REP))
    dbias = ubias[:, :kt] + jnp.asarray(causal)[None]
    ki = np.arange(WINDOW + Q_BLOCK)[:, None]
    qi = np.arange(Q_BLOCK)[None, :]
    band = np.where((ki > qi) & (ki <= qi + WINDOW), 0.0, NEG).astype(np.float32)
    wbias = jnp.asarray(np.tile(band, (1, NSA_REP)))
    return onehot, wext, dbias, ubias, wbias


def _log_sigmoid(x):
    return jnp.minimum(x, 0.0) - jnp.log(1.0 + jnp.exp(-jnp.abs(x)))


def _split3(x):
    hi = x.astype(BF16)
    r1 = x - hi.astype(F32)
    mid = r1.astype(BF16)
    lo = (r1 - mid.astype(F32)).astype(BF16)
    return hi, mid, lo


def _mlstm_kernel(bias_ref, q_ref, k_ref, v_ref, o_ref, ifc_ref, ifr_ref, ng_ref,
                  tri_ref, y_ref, c_ref, n_ref, m_ref):
    ch = pl.program_id(1)
    nh, dh = MLSTM_HEADS, MLSTM_HEAD_DIM
    L = q_ref.shape[0]

    @pl.when(ch == 0)
    def _():
        c_ref[...] = jnp.zeros(c_ref.shape, F32)
        n_ref[...] = jnp.zeros(n_ref.shape, F32)
        m_ref[...] = jnp.zeros(m_ref.shape, F32)

    tri = tri_ref[...]
    lane8 = lax.broadcasted_iota(jnp.int32, (1, LANE), 1)
    bias_c = jnp.zeros((1, LANE), F32)
    for h in range(nh):
        bias_c = jnp.where(lane8 == h, bias_ref[h], bias_c)
        bias_c = jnp.where(lane8 == nh + h, bias_ref[nh + h], bias_c)
    pre_c = ifc_ref[...] + bias_c
    cum_c = sum(_dot(tri, part) for part in _split3(_log_sigmoid(pre_c)))
    row8 = lax.broadcasted_iota(jnp.int32, (8, 1), 0)
    bias_r = jnp.zeros((8, 1), F32)
    for h in range(2 * nh):
        bias_r = jnp.where(row8 == h, bias_ref[h], bias_r)
    pre_r = ifr_ref[...] + bias_r
    cum_r = sum(_dot_nt(part, tri) for part in _split3(_log_sigmoid(pre_r)))

    rr = lax.broadcasted_iota(jnp.int32, (L, 1), 0)
    cc = lax.broadcasted_iota(jnp.int32, (1, L), 1)
    causal = cc <= rr

    for h in range(nh):
        cols = slice(h * dh, (h + 1) * dh)
        qb = q_ref[:, cols]
        kb = k_ref[:, cols]
        vh = v_ref[:, cols]
        qh = qb.astype(F32)
        kh = kb.astype(F32)
        b_c = cum_c[:, nh + h:nh + h + 1]
        li_c = pre_c[:, h:h + 1]
        b_r = cum_r[nh + h:nh + h + 1, :]
        li_r = pre_r[h:h + 1, :]
        m_prev = m_ref[h:h + 1, 0:1]

        dmat = jnp.where(causal, b_c - b_r + li_r, NEG)
        a = b_c + m_prev
        m_j = jnp.maximum(a, jnp.max(dmat, axis=1, keepdims=True))
        w_intra = jnp.exp(dmat - m_j)
        w_inter = jnp.exp(a - m_j)
        sc = _dot_nt(qb, kb) * w_intra
        c_old = c_ref[h]
        n_old = n_ref[h:h + 1, :]
        num = w_inter * _dot(qb, c_old.astype(BF16)) + _dot(sc.astype(BF16), vh)
        den = (w_inter * jnp.sum(qh * n_old, axis=1, keepdims=True)
               + jnp.sum(sc, axis=1, keepdims=True))
        hid = num / jnp.maximum(jnp.abs(den), jnp.exp(-m_j))

        g_tot = b_r[:, L - 1:L]
        lw_c = g_tot - b_c + li_c
        lw_r = g_tot - b_r + li_r
        m_new = jnp.maximum(g_tot + m_prev, jnp.max(lw_r, axis=1, keepdims=True))
        decay = jnp.exp(g_tot + m_prev - m_new)
        kw = jnp.exp(lw_c - m_new) * kh
        c_ref[h] = decay * c_old + _dot(kw.T.astype(BF16), vh)
        n_ref[h:h + 1, :] = decay * n_old + jnp.sum(kw, axis=0, keepdims=True)
        m_ref[h:h + 1, :] = jnp.broadcast_to(m_new, (1, LANE))

        hn = hid * lax.rsqrt(jnp.mean(hid * hid, axis=-1, keepdims=True) + EPS) * ng_ref[:, cols]
        y_ref[:, cols] = (_sigmoid(o_ref[:, cols]) * hn).astype(y_ref.dtype)


def _mlstm(seg_a, seg_b, if_rows, bias, norm_g, tri, batch, seq, chunk):
    n = batch * seq
    nc = seq // chunk
    w = MLSTM_WIDTH
    nh, dh = MLSTM_HEADS, MLSTM_HEAD_DIM

    def col_spec(off):
        return pl.BlockSpec((chunk, w), lambda b, c, o=off // w: (b * nc + c, o))

    return pl.pallas_call(
        _mlstm_kernel,
        grid=(batch, nc),
        in_specs=[
            pl.BlockSpec(memory_space=pltpu.SMEM),
            col_spec(A_QK), col_spec(A_QK + w), col_spec(A_V), col_spec(B_O),
            pl.BlockSpec((chunk, LANE), lambda b, c: (b * nc + c, B_IF // LANE)),
            pl.BlockSpec((8, chunk), lambda b, c: (0, b * nc + c)),
            pl.BlockSpec((1, w), lambda b, c: (0, 0)),
            pl.BlockSpec((chunk, chunk), lambda b, c: (0, 0)),
        ],
        out_specs=pl.BlockSpec((chunk, w), lambda b, c: (b * nc + c, 0)),
        out_shape=jax.ShapeDtypeStruct((n, w), BF16),
        scratch_shapes=[
            pltpu.VMEM((nh, dh, dh), F32),
            pltpu.VMEM((8, dh), F32),
            pltpu.VMEM((8, LANE), F32),
        ],
        compiler_params=pltpu.CompilerParams(
            dimension_semantics=("parallel", "arbitrary"), vmem_limit_bytes=VMEM_LIMIT),
        name="mlstm",
    )(bias, seg_a, seg_a, seg_a, seg_b, seg_b, if_rows, norm_g, tri)


def _outproj_kernel(x_ref, ya_ref, ym_ref, wa_ref, wm_ref, o_ref):
    o_ref[...] = x_ref[...] + _dot(ya_ref[...], wa_ref[...]) + _dot(ym_ref[...], wm_ref[...])


def _outproj(x2, ya, ym, wa, wm, tm):
    n, d = x2.shape
    return pl.pallas_call(
        _outproj_kernel,
        grid=(n // tm,),
        in_specs=[
            pl.BlockSpec((tm, d), lambda i: (i, 0)),
            pl.BlockSpec((tm, ya.shape[1]), lambda i: (i, 0)),
            pl.BlockSpec((tm, ym.shape[1]), lambda i: (i, 0)),
            pl.BlockSpec(wa.shape, lambda i: (0, 0)),
            pl.BlockSpec(wm.shape, lambda i: (0, 0)),
        ],
        out_specs=pl.BlockSpec((tm, d), lambda i: (i, 0)),
        out_shape=jax.ShapeDtypeStruct((n, d), F32),
        compiler_params=pltpu.CompilerParams(
            dimension_semantics=("parallel",), vmem_limit_bytes=VMEM_LIMIT),
        name="outproj",
    )(x2, ya, ym, wa, wm)


def _mlp_kernel(x_ref, g_ref, w1_ref, w2_ref, gf_ref, o_ref, h_ref):
    f = pl.program_id(1)

    @pl.when(f == 0)
    def _():
        x = x_ref[...]
        r = lax.rsqrt(jnp.mean(x * x, axis=-1, keepdims=True) + EPS)
        h_ref[...] = (x * r * g_ref[...]).astype(BF16)
        o_ref[...] = x

    u = jnp.maximum(_dot(h_ref[...], w1_ref[...]), 0.0)
    o_ref[...] += _dot((u * u).astype(BF16), w2_ref[...])

    @pl.when(f == pl.num_programs(1) - 1)
    def _():
        x2 = o_ref[...]
        r = lax.rsqrt(jnp.mean(x2 * x2, axis=-1, keepdims=True) + EPS)
        o_ref[...] = x2 * r * gf_ref[...]


def _mlp(x1, g, w1, w2, gf, tm, tf):
    n, d = x1.shape
    dff = w1.shape[1]
    return pl.pallas_call(
        _mlp_kernel,
        grid=(n // tm, dff // tf),
        in_specs=[
            pl.BlockSpec((tm, d), lambda i, f: (i, 0)),
            pl.BlockSpec((1, d), lambda i, f: (0, 0)),
            pl.BlockSpec((d, tf), lambda i, f: (0, f)),
            pl.BlockSpec((tf, d), lambda i, f: (f, 0)),
            pl.BlockSpec((1, d), lambda i, f: (0, 0)),
        ],
        out_specs=pl.BlockSpec((tm, d), lambda i, f: (i, 0)),
        out_shape=jax.ShapeDtypeStruct((n, d), F32),
        scratch_shapes=[pltpu.VMEM((tm, d), BF16)],
        compiler_params=pltpu.CompilerParams(
            dimension_semantics=("parallel", "arbitrary"), vmem_limit_bytes=VMEM_LIMIT),
        name="mlp",
    )(x1, g, w1, w2, gf)


def _row_tile(n, want):
    t = want
    while n % t:
        t //= 2
    return t


def _layer(x2, batch, seq, norm_mix_g, w_in, w_cmp_k1, w_cmp_k2, pos_cmp_k, w_cmp_v1, w_cmp_v2,
           pos_cmp_v, conv_w, conv_b, b_igate, b_fgate, mlstm_norm_g, w_out, norm_mlp_g,
           w_mlp_in, w_mlp_out):
    n, d = x2.shape
    assert seq % Q_BLOCK == 0 and seq >= WINDOW + Q_BLOCK and seq // SEL_BLOCK <= SEL_LANES
    g_ = NSA_KV_GROUPS
    nh = MLSTM_HEADS

    c_gate = NSA_WIDTH + 6 * NSA_KV_WIDTH
    c_qk = c_gate + NSA_HEADS * N_BRANCH
    c_v = c_qk + 2 * MLSTM_WIDTH
    c_o = c_v + MLSTM_WIDTH
    c_i = c_o + MLSTM_WIDTH
    c_f = c_i + nh
    w_a = jnp.concatenate([w_in[:, c_v:c_o], w_in[:, c_qk:c_v], w_in[:, :c_gate]], axis=1).astype(BF16)
    gate_cols = []
    per_g = NSA_REP * N_BRANCH
    for g in range(g_):
        gate_cols += [w_in[:, c_gate + g * per_g:c_gate + (g + 1) * per_g],
                      jnp.zeros((d, LANE - per_g), w_in.dtype)]
    w_b = jnp.concatenate(
        [w_in[:, c_o:c_i]] + gate_cols
        + [w_in[:, c_i:c_f + nh], jnp.zeros((d, SEG_B - B_IF - 2 * nh), w_in.dtype)],
        axis=1).astype(BF16)
    scale_a = jnp.concatenate([jnp.ones((1, A_QK + MLSTM_WIDTH), F32),
                               jnp.full((1, MLSTM_WIDTH), MLSTM_HEAD_DIM ** -0.5, F32),
                               jnp.full((1, NSA_WIDTH), NSA_HEAD_DIM ** -0.5, F32),
                               jnp.ones((1, SEG_A - A_KV), F32)], axis=1)
    scale_b = jnp.ones((1, SEG_B), F32)
    conv_pad = ((0, 0), (A_QK, SEG_A - A_Q))
    cw_a = jnp.pad(conv_w, conv_pad)
    cb_a = jnp.pad(conv_b.reshape(1, -1), conv_pad)
    g_mix = norm_mix_g.reshape(1, d)

    tm = _row_tile(seq, ROW_TILE)
    seg_a = _inproj(x2, g_mix, w_a, scale_a, cw_a, cb_a, BF16, tm, COL_TILE, (A_QK, A_Q), seq)
    seg_b = _inproj(x2, g_mix, w_b, scale_b, jnp.zeros((CONV_WIDTH, SEG_B), F32),
                    jnp.zeros((1, SEG_B), F32), F32, tm, COL_TILE, (0, 0), seq)

    n_sub = seq // CMP_STRIDE
    cmp_in = seg_a[:, A_KV:A_KV + 2 * NSA_KV_WIDTH]
    cmp_in = cmp_in.reshape(batch, n_sub, CMP_STRIDE, 2 * g_, NSA_HEAD_DIM)
    cmp_in = cmp_in.transpose(0, 3, 1, 2, 4).reshape(batch, 2 * g_, n_sub, CMP_STRIDE * NSA_HEAD_DIM)
    w1s = jnp.stack([w_cmp_k1, w_cmp_v1]).astype(BF16)
    w2s = jnp.stack([w_cmp_k2, w_cmp_v2]).astype(BF16)
    poss = jnp.stack([pos_cmp_k, pos_cmp_v]).reshape(2, 1, CMP_BLOCK * NSA_HEAD_DIM)
    poss = jnp.broadcast_to(poss, (2, 8, CMP_BLOCK * NSA_HEAD_DIM)).astype(BF16)
    kvc, kvct = _compress(cmp_in, w1s, w2s, poss)

    cmp_start = np.arange(n_sub) * CMP_STRIDE
    sel_start = np.arange(SEL_LANES) * SEL_BLOCK
    ovt = ((cmp_start[None, :] < sel_start[:, None] + SEL_BLOCK)
           & (cmp_start[None, :] + CMP_BLOCK - 1 >= sel_start[:, None])
           & (np.arange(n_sub)[None, :] < n_sub - CMP_BLOCK // CMP_STRIDE + 1))
    ovt = jnp.asarray(ovt, BF16)
    slopes = jnp.exp2(-8.0 * jnp.arange(1, NSA_HEADS + 1, dtype=F32) / NSA_HEADS)
    ocmp, pen, flags = _nsa_cmp(seg_a, kvc, kvct, ovt, slopes, batch, seq)
    onehot, wext, dbias, ubias, wbias = _nsa_tables(slopes, seq)
    y_a = _nsa_attn(seg_a, seg_b, pen, ocmp, flags, onehot, wext, dbias, ubias, wbias,
                    slopes, batch, seq)

    chunk = 256 if seq % 256 == 0 else 128
    if_rows = seg_b[:, B_IF:B_IF + 2 * nh].T
    bias = jnp.concatenate([b_igate, b_fgate]).astype(F32)
    tri = jnp.asarray(np.tril(np.ones((chunk, chunk), np.float32)), BF16)
    y_m = _mlstm(seg_a, seg_b, if_rows, bias, mlstm_norm_g.reshape(1, -1), tri, batch, seq, chunk)

    w_o = w_out.astype(BF16)
    x1 = _outproj(x2, y_a, y_m, w_o[:NSA_WIDTH], w_o[NSA_WIDTH:], _row_tile(n, 512))
    return x1, (norm_mlp_g.reshape(1, d), w_mlp_in.astype(BF16), w_mlp_out.astype(BF16))


def kernel(x, norm_mix_g, w_in, w_cmp_k1, w_cmp_k2, pos_cmp_k, w_cmp_v1, w_cmp_v2, pos_cmp_v, conv_w, conv_b, b_igate, b_fgate, mlstm_norm_g, w_out, norm_mlp_g, w_mlp_in, w_mlp_out, norm_f_g):
    batch, seq, d = x.shape
    depth = w_in.shape[0]
    assert depth == 1, "the final RMSNorm is fused into the last layer's channel mixer"
    x2 = x.reshape(batch * seq, d)
    tm = _row_tile(batch * seq, ROW_TILE)
    for l in range(depth):
        x1, (g_mlp, w1, w2) = _layer(
            x2, batch, seq, norm_mix_g[l], w_in[l], w_cmp_k1[l], w_cmp_k2[l], pos_cmp_k[l],
            w_cmp_v1[l], w_cmp_v2[l], pos_cmp_v[l], conv_w[l], conv_b[l], b_igate[l], b_fgate[l],
            mlstm_norm_g[l], w_out[l], norm_mlp_g[l], w_mlp_in[l], w_mlp_out[l])
        x2 = _mlp(x1, g_mlp, w1, w2, norm_f_g.reshape(1, d), tm, 512)
    return x2.reshape(batch, seq, d)
```

```python
import functools

import numpy as np
import jax
import jax.numpy as jnp
from jax import lax
from jax.experimental import pallas as pl
from jax.experimental.pallas import tpu as pltpu

F32 = jnp.float32
BF16 = jnp.bfloat16

EPS = 1e-6
NEG = -1e30
FORCE_BONUS = 1e4
PICKED = -3e38
MASK_BIG = 1e30

D_MODEL = 2048
NSA_HEAD_DIM = 128
NSA_WIDTH = D_MODEL // 2
NSA_HEADS = NSA_WIDTH // NSA_HEAD_DIM
NSA_REP = 4
NSA_KV_GROUPS = NSA_HEADS // NSA_REP
NSA_KV_WIDTH = NSA_KV_GROUPS * NSA_HEAD_DIM
CMP_BLOCK = 32
CMP_STRIDE = 16
SEL_BLOCK = 64
SEL_TOPK = 16
WINDOW = 512
Q_BLOCK = 128
N_BRANCH = 3
MLSTM_HEAD_DIM = 256
MLSTM_WIDTH = D_MODEL - NSA_WIDTH
MLSTM_HEADS = MLSTM_WIDTH // MLSTM_HEAD_DIM
CONV_WIDTH = 4
D_FF = 4 * D_MODEL

LANE = 128
SEL_LANES = 128
KEY_TILE = 128
SEL_TILES_PER_STEP = 2
WIN_TILES = WINDOW // KEY_TILE
VMEM_LIMIT = 56 * 1024 * 1024
ROW_TILE = 1024

A_V, A_QK = 0, MLSTM_WIDTH
A_Q = A_QK + 2 * MLSTM_WIDTH
A_KV = A_Q + NSA_WIDTH
SEG_A = A_KV + 6 * NSA_KV_WIDTH
B_O = 0
B_GATE = MLSTM_WIDTH
B_IF = B_GATE + NSA_KV_GROUPS * LANE
SEG_B = B_IF + 2 * LANE
COL_TILE = 512

WCOL_PAD, WCOL_HI, WCOL_LO, WCOL_QHI, WCOL_QLO = 0, 1, 2, 3, 4


def _dot(a, b):
    return jnp.dot(a, b, preferred_element_type=F32)


def _dot_nt(a, b):
    return lax.dot_general(a, b, (((1,), (1,)), ((), ())), preferred_element_type=F32)


def _sigmoid(x):
    return 1.0 / (1.0 + jnp.exp(-x))


def _shifted(x, tail, s):
    xs = pltpu.roll(x, s, axis=0)
    ts = pltpu.roll(tail, s, axis=0)
    row8 = lax.broadcasted_iota(jnp.int32, (8, 1), 0)
    head = jnp.where(row8 < s, ts, xs[:8])
    return jnp.concatenate([head, xs[8:]], axis=0)


def _conv_silu(x, tail, w, b):
    y = b + _shifted(x, tail, CONV_WIDTH - 1) * w[0:1]
    for i in range(1, CONV_WIDTH - 1):
        y = y + _shifted(x, tail, CONV_WIDTH - 1 - i) * w[i:i + 1]
    y = y + x * w[CONV_WIDTH - 1:CONV_WIDTH]
    return y * _sigmoid(y)


def _inproj_kernel(x_ref, g_ref, w_ref, cs_ref, cw_ref, cb_ref, o_ref, h_ref, halo_ref, *,
                   conv_lo, conv_hi, tiles_per_seq):
    i = pl.program_id(0)
    j = pl.program_id(1)

    @pl.when(j == 0)
    def _():
        x = x_ref[...]
        r = lax.rsqrt(jnp.mean(x * x, axis=-1, keepdims=True) + EPS)
        h_ref[...] = (x * r * g_ref[...]).astype(BF16)

    is_conv = (j >= conv_lo) & (j < conv_hi)

    @pl.when(jnp.logical_not(is_conv))
    def _():
        o_ref[...] = (_dot(h_ref[...], w_ref[...]) * cs_ref[...]).astype(o_ref.dtype)

    if conv_hi > conv_lo:
        @pl.when(is_conv)
        def _():
            slot = j - conv_lo

            @pl.when(i % tiles_per_seq == 0)
            def _():
                halo_ref[slot] = jnp.zeros(halo_ref.shape[1:], F32)

            acc = _dot(h_ref[...], w_ref[...])
            tail = halo_ref[slot]
            halo_ref[slot] = acc[acc.shape[0] - 8:]
            y = _conv_silu(acc, tail, cw_ref[...], cb_ref[...])
            o_ref[...] = (y * cs_ref[...]).astype(o_ref.dtype)


def _inproj(x2, g, w, cs, cw, cb, out_dtype, tm, tn, conv_cols, seq):
    n, d = x2.shape
    nc = w.shape[1]
    conv_lo, conv_hi = conv_cols[0] // tn, conv_cols[1] // tn
    assert conv_cols[0] % tn == 0 and conv_cols[1] % tn == 0 and seq % tm == 0
    return pl.pallas_call(
        functools.partial(_inproj_kernel, conv_lo=conv_lo, conv_hi=conv_hi, tiles_per_seq=seq // tm),
        grid=(n // tm, nc // tn),
        in_specs=[
            pl.BlockSpec((tm, d), lambda i, j: (i, 0)),
            pl.BlockSpec((1, d), lambda i, j: (0, 0)),
            pl.BlockSpec((d, tn), lambda i, j: (0, j)),
            pl.BlockSpec((1, tn), lambda i, j: (0, j)),
            pl.BlockSpec((CONV_WIDTH, tn), lambda i, j: (0, j)),
            pl.BlockSpec((1, tn), lambda i, j: (0, j)),
        ],
        out_specs=pl.BlockSpec((tm, tn), lambda i, j: (i, j)),
        out_shape=jax.ShapeDtypeStruct((n, nc), out_dtype),
        scratch_shapes=[pltpu.VMEM((tm, d), BF16),
                        pltpu.VMEM((max(conv_hi - conv_lo, 1), 8, tn), F32)],
        compiler_params=pltpu.CompilerParams(
            dimension_semantics=("arbitrary", "arbitrary"), vmem_limit_bytes=VMEM_LIMIT),
        name="inproj",
    )(x2, g, w, cs, cw, cb)


def _compress_kernel(x_ref, w1_ref, w2_ref, pos_ref, o_ref, ot_ref):
    x = x_ref[0, 0]
    n_sub = x.shape[0]
    half = CMP_STRIDE * NSA_HEAD_DIM
    w1 = w1_ref[0]
    top = _dot(x, w1[:half])
    bot = _dot(x, w1[half:])
    bot = pltpu.roll(bot, n_sub - 1, axis=0)
    posw = _dot(pos_ref[0], w1)[0:1]
    pre = top + bot + posw
    hid = pre * _sigmoid(pre)
    out = _dot(hid.astype(BF16), w2_ref[0])
    o_ref[0, 0] = out.astype(o_ref.dtype)
    ot_ref[0, 0] = out.T.astype(ot_ref.dtype)


def _compress(xs, w1s, w2s, poss):
    b, c, n_sub, feat = xs.shape
    g = NSA_KV_GROUPS
    return pl.pallas_call(
        _compress_kernel,
        grid=(b, c),
        in_specs=[
            pl.BlockSpec((1, 1, n_sub, feat), lambda i, j: (i, j, 0, 0)),
            pl.BlockSpec((1, 2 * feat, NSA_HEAD_DIM), lambda i, j: (j // g, 0, 0)),
            pl.BlockSpec((1, NSA_HEAD_DIM, NSA_HEAD_DIM), lambda i, j: (j // g, 0, 0)),
            pl.BlockSpec((1, 8, 2 * feat), lambda i, j: (j // g, 0, 0)),
        ],
        out_specs=[pl.BlockSpec((1, 1, n_sub, NSA_HEAD_DIM), lambda i, j: (i, j, 0, 0)),
                   pl.BlockSpec((1, 1, NSA_HEAD_DIM, n_sub), lambda i, j: (i, j, 0, 0))],
        out_shape=[jax.ShapeDtypeStruct((b, c, n_sub, NSA_HEAD_DIM), BF16),
                   jax.ShapeDtypeStruct((b, c, NSA_HEAD_DIM, n_sub), BF16)],
        compiler_params=pltpu.CompilerParams(
            dimension_semantics=("parallel", "parallel"), vmem_limit_bytes=VMEM_LIMIT),
        name="compress",
    )(xs, w1s, w2s, poss)


def _stack_heads(q_all):
    dh = NSA_HEAD_DIM
    return jnp.concatenate([q_all[:, r * dh:(r + 1) * dh] for r in range(NSA_REP)], axis=0)


def _nsa_cmp_kernel(slopes_ref, q_ref, kc_ref, vct_ref, ovt_ref, ocmp_ref, pen_ref, flag_ref, *,
                    seq, sub_blocks):
    for sub in range(sub_blocks):
        rows = slice(sub * Q_BLOCK, (sub + 1) * Q_BLOCK)
        _nsa_cmp_block(slopes_ref, q_ref[rows, :], kc_ref, vct_ref, ovt_ref,
                       ocmp_ref.at[0, 0, sub], pen_ref.at[0, 0, rows], flag_ref.at[0, 0, sub],
                       pl.program_id(2) * sub_blocks + sub, seq)


def _nsa_cmp_block(slopes_ref, q_all, kc_ref, vct_ref, ovt_ref, ocmp_ref, pen_ref, flag_ref, qb, seq):
    g = pl.program_id(1)
    nq = Q_BLOCK
    n_cpad = kc_ref.shape[2]
    n_cmp = seq // CMP_STRIDE - CMP_BLOCK // CMP_STRIDE + 1
    t0 = qb * nq

    s_t = _dot_nt(kc_ref[0, 0], _stack_heads(q_all))
    n_s = lax.broadcasted_iota(jnp.int32, (n_cpad, 1), 0)
    q_l = lax.broadcasted_iota(jnp.int32, (1, nq), 1)
    dist = (t0 - (CMP_BLOCK - 1)) + q_l - n_s * CMP_STRIDE
    valid = (dist >= 0) & (n_s < n_cmp)
    dist_f = dist.astype(F32)
    probs = []
    p_sum = jnp.zeros((n_cpad, nq), F32)
    for r in range(NSA_REP):
        slope = slopes_ref[g * NSA_REP + r]
        s = jnp.where(valid, s_t[:, r * nq:(r + 1) * nq] - slope * dist_f, NEG)
        m = jnp.max(s, axis=0, keepdims=True)
        e = jnp.exp(s - m)
        inv = jnp.where(m > 0.5 * NEG, 1.0 / jnp.sum(e, axis=0, keepdims=True), 0.0)
        p = e * inv
        probs.append(p.astype(BF16))
        p_sum = p_sum + p
    ocmp_ref[...] = _dot(vct_ref[0, 0], jnp.concatenate(probs, axis=1))

    p_hi = p_sum.astype(BF16)
    p_lo = (p_sum - p_hi.astype(F32)).astype(BF16)
    ovt = ovt_ref[...]
    imp = _dot(ovt, p_hi) + _dot(ovt, p_lo)
    j_i = lax.broadcasted_iota(jnp.int32, (SEL_LANES, 1), 0)
    t_l = t0 + q_l
    cur = t_l // SEL_BLOCK
    forced = (j_i == 0) | (j_i == cur) | (j_i == cur - 1)
    causal_blk = j_i * SEL_BLOCK <= t_l
    val = jnp.where(causal_blk, jnp.where(forced, imp + FORCE_BONUS, imp), NEG)
    j_f = j_i.astype(F32)
    sel_t = jnp.zeros((SEL_LANES, nq), F32)
    for _ in range(min(SEL_TOPK, seq // SEL_BLOCK)):
        mx = jnp.max(val, axis=0, keepdims=True)
        first = jnp.min(jnp.where(val == mx, j_f, float(SEL_LANES)), axis=0, keepdims=True)
        pick = j_f == first
        sel_t = jnp.where(pick, 1.0, sel_t)
        val = jnp.where(pick, PICKED, val)
    sel = sel_t.T
    pen_ref[...] = ((sel - 1.0) * MASK_BIG).astype(pen_ref.dtype)
    flag_ref[...] = (jnp.max(sel, axis=0, keepdims=True) > 0.0).astype(jnp.int32)


def _nsa_cmp(seg_a, kvc, kvct, ovt, slopes, batch, seq):
    nqb = seq // Q_BLOCK
    gq = NSA_REP * NSA_HEAD_DIM
    g_ = NSA_KV_GROUPS
    n_cpad = kvc.shape[2]
    sub = _nsa_sub_blocks(seq)
    nstep = nqb // sub
    return pl.pallas_call(
        functools.partial(_nsa_cmp_kernel, seq=seq, sub_blocks=sub),
        grid=(batch, g_, nstep),
        in_specs=[
            pl.BlockSpec(memory_space=pltpu.SMEM),
            pl.BlockSpec((sub * Q_BLOCK, gq), lambda b, g, q: (b * nstep + q, A_Q // gq + g)),
            pl.BlockSpec((1, 1, n_cpad, NSA_HEAD_DIM), lambda b, g, q: (b, g, 0, 0)),
            pl.BlockSpec((1, 1, NSA_HEAD_DIM, n_cpad), lambda b, g, q: (b, g_ + g, 0, 0)),
            pl.BlockSpec((SEL_LANES, n_cpad), lambda b, g, q: (0, 0)),
        ],
        out_specs=[
            pl.BlockSpec((1, 1, sub, NSA_HEAD_DIM, gq), lambda b, g, q: (b, g, q, 0, 0)),
            pl.BlockSpec((1, 1, sub * Q_BLOCK, SEL_LANES), lambda b, g, q: (b, g, q, 0)),
            pl.BlockSpec((1, 1, sub, 1, SEL_LANES), lambda b, g, q: (b, g, q, 0, 0)),
        ],
        out_shape=[
            jax.ShapeDtypeStruct((batch, g_, nqb, NSA_HEAD_DIM, gq), F32),
            jax.ShapeDtypeStruct((batch, g_, seq, SEL_LANES), BF16),
            jax.ShapeDtypeStruct((batch, g_, nqb, 1, SEL_LANES), jnp.int32),
        ],
        compiler_params=pltpu.CompilerParams(
            dimension_semantics=("parallel", "parallel", "parallel"), vmem_limit_bytes=VMEM_LIMIT),
        name="nsa_cmp",
    )(slopes, seg_a, kvc, kvct, ovt)


def _nsa_attn_kernel(slopes_ref, flag_ref, q_ref, gate_ref, pen_ref, ocmp_ref, ks_ref, vs_ref,
                     kw_ref, vw_ref, onehot_ref, wext_ref, dbias_ref, ubias_ref, wbias_ref, o_ref,
                     ksel_ref, vselt_ref, kwin_ref, vwint_ref,
                     qa_ref, qw_ref, list_ref, m_ref, l_ref, acc_ref, owin_ref, sa_ref, sb_ref,
                     *, seq, sub_blocks):
    g = pl.program_id(1)
    step = pl.program_id(2)
    dh = NSA_HEAD_DIM
    nq = sub_blocks * Q_BLOCK
    kt = KEY_TILE
    per = SEL_TILES_PER_STEP
    pad_tile = seq // kt
    t0 = step * nq
    tile0 = step * sub_blocks

    @pl.when(step == 0)
    def _():
        ksel_ref[0:seq, :dh] = ks_ref[...]
        ksel_ref[0:seq, dh:] = onehot_ref[...]
        ksel_ref[seq:, :dh] = jnp.zeros((kt, dh), BF16)
        ksel_ref[seq:, dh:] = jnp.ones((kt, SEL_LANES), BF16)
        lane2 = lax.broadcasted_iota(jnp.int32, (WINDOW, dh + LANE), 1)
        kwin_ref[0:WINDOW, :] = jnp.where(lane2 == dh + WCOL_PAD, 1.0, 0.0).astype(BF16)
        kwin_ref[WINDOW:, :dh] = kw_ref[...]
        kwin_ref[WINDOW:, dh:] = wext_ref[...]
        zero_tile = jnp.zeros((dh, kt), BF16)
        vselt_ref[pad_tile] = zero_tile
        for i in range(WIN_TILES):
            vwint_ref[i] = zero_tile

        def transpose_tile(t, carry):
            r0 = pl.multiple_of(t * kt, kt)
            vselt_ref[t] = vs_ref[pl.ds(r0, kt), :].astype(F32).T.astype(BF16)
            vwint_ref[t + WIN_TILES] = vw_ref[pl.ds(r0, kt), :].astype(F32).T.astype(BF16)
            return carry

        lax.fori_loop(0, seq // kt, transpose_tile, 0)

    slopes = [slopes_ref[g * NSA_REP + r] for r in range(NSA_REP)]
    row_q = lax.broadcasted_iota(jnp.int32, (nq, 1), 0)
    lane = lax.broadcasted_iota(jnp.int32, (1, LANE), 1)
    q_all = q_ref[...]

    pen = pen_ref[0, 0].astype(F32)
    blk_rel = ((lane - (t0 + row_q) // SEL_BLOCK) * SEL_BLOCK).astype(F32)
    for r in range(NSA_REP):
        qa_ref[r * nq:(r + 1) * nq, :dh] = q_all[:, r * dh:(r + 1) * dh]
        qa_ref[r * nq:(r + 1) * nq, dh:] = (pen + slopes[r] * blk_rel).astype(BF16)

    def scan(k, cnt):
        for j in range(2):
            i = 2 * k + j
            hits = sum(flag_ref[0, 0, a, 0, 2 * i] + flag_ref[0, 0, a, 0, 2 * i + 1]
                       for a in range(sub_blocks))
            act = (hits > 0) & (i < tile0)
            list_ref[cnt] = i
            cnt = cnt + act.astype(jnp.int32)
        return cnt

    cnt = lax.fori_loop(0, (tile0 + 1) // 2, scan, 0)
    for i in range(4 * per):
        list_ref[cnt + i] = pad_tile

    def group_scores(it):
        keys = jnp.concatenate(
            [ksel_ref[pl.ds(pl.multiple_of(list_ref[it * per + i] * kt, kt), kt), :]
             for i in range(per)], axis=0)
        return _dot_nt(keys, qa_ref[...]) + ubias_ref[0]

    s = _dot_nt(ksel_ref[pl.ds(pl.multiple_of(t0, kt), nq), :], qa_ref[...]) + dbias_ref[0]
    m0 = jnp.max(s, axis=0, keepdims=True)
    p = jnp.exp(s - m0)
    m_ref[...] = m0
    l_ref[...] = jnp.sum(p, axis=0, keepdims=True)
    vals = jnp.concatenate([vselt_ref[tile0 + a] for a in range(sub_blocks)], axis=1)
    acc_ref[...] = _dot(vals, p.astype(BF16))
    sa_ref[...] = group_scores(0)

    tp = t0 + WINDOW + row_q
    t_hi = (tp // SEL_BLOCK).astype(F32)
    t_lo = (tp % SEL_BLOCK).astype(F32)
    for r in range(NSA_REP):
        sl = slopes[r]
        ext = jnp.where(lane == WCOL_PAD, -MASK_BIG, 0.0)
        ext = jnp.where(lane == WCOL_HI, sl * SEL_BLOCK, ext)
        ext = jnp.where(lane == WCOL_LO, sl, ext)
        ext = jnp.where(lane == WCOL_QHI, -sl * SEL_BLOCK * t_hi, ext)
        ext = jnp.where(lane == WCOL_QLO, -sl * t_lo, ext)
        qw_ref[r * nq:(r + 1) * nq, :dh] = q_all[:, r * dh:(r + 1) * dh]
        qw_ref[r * nq:(r + 1) * nq, dh:] = ext.astype(BF16)
    wlen = WINDOW + nq
    s = _dot_nt(kwin_ref[pl.ds(pl.multiple_of(t0, kt), wlen), :], qw_ref[...]) + wbias_ref[...]
    e = jnp.exp(s - jnp.max(s, axis=0, keepdims=True))
    vwin = jnp.concatenate([vwint_ref[tile0 + i] for i in range(WIN_TILES + sub_blocks)], axis=1)
    owin_ref[...] = _dot(vwin, e.astype(BF16)) / jnp.sum(e, axis=0, keepdims=True)

    def absorb(s, grp):
        vals = jnp.concatenate([vselt_ref[list_ref[grp * per + i]] for i in range(per)], axis=1)
        m_old = m_ref[...]
        m_new = jnp.maximum(m_old, jnp.max(s, axis=0, keepdims=True))
        alpha = jnp.exp(m_old - m_new)
        p = jnp.exp(s - m_new)
        l_ref[...] = alpha * l_ref[...] + jnp.sum(p, axis=0, keepdims=True)
        acc_ref[...] = alpha * acc_ref[...] + _dot(vals, p.astype(BF16))
        m_ref[...] = m_new

    def sel_body(it, carry):
        s = sa_ref[...]
        sb_ref[...] = group_scores(2 * it + 1)
        absorb(s, 2 * it)
        s = sb_ref[...]
        sa_ref[...] = group_scores(2 * it + 2)
        absorb(s, 2 * it + 1)
        return carry

    lax.fori_loop(0, (cnt + 2 * per - 1) // (2 * per), sel_body, 0)
    o_sel = acc_ref[...] / l_ref[...]
    o_win = owin_ref[...]

    gate_t = _sigmoid(gate_ref[...]).T
    for a in range(sub_blocks):
        o_cmp = ocmp_ref[0, 0, a]
        qs = slice(a * Q_BLOCK, (a + 1) * Q_BLOCK)
        for r in range(NSA_REP):
            cols = slice(r * nq + a * Q_BLOCK, r * nq + (a + 1) * Q_BLOCK)
            c0 = N_BRANCH * r
            out_t = (gate_t[c0:c0 + 1, qs] * o_cmp[:, r * Q_BLOCK:(r + 1) * Q_BLOCK]
                     + gate_t[c0 + 1:c0 + 2, qs] * o_sel[:, cols]
                     + gate_t[c0 + 2:c0 + 3, qs] * o_win[:, cols])
            o_ref[qs, r * dh:(r + 1) * dh] = out_t.T.astype(o_ref.dtype)


def _nsa_sub_blocks(seq):
    return 2 if (seq // Q_BLOCK) % 2 == 0 else 1


def _nsa_attn(seg_a, seg_b, pen, ocmp, flags, onehot, wext, dbias, ubias, wbias, slopes, batch, seq):
    n = batch * seq
    sub = _nsa_sub_blocks(seq)
    nq = sub * Q_BLOCK
    nqb = seq // nq
    gq = NSA_REP * NSA_HEAD_DIM
    g_ = NSA_KV_GROUPS
    dh, kt = NSA_HEAD_DIM, KEY_TILE
    dk = dh + SEL_LANES
    rq = NSA_REP * nq
    n_t = seq // kt

    def kv_spec(kind):
        return pl.BlockSpec((seq, dh), lambda b, g, q, k=kind: (b, A_KV // dh + k * g_ + g))

    def const_spec(arr):
        return pl.BlockSpec(arr.shape, lambda b, g, q, nd=arr.ndim: (0,) * nd)

    return pl.pallas_call(
        functools.partial(_nsa_attn_kernel, seq=seq, sub_blocks=sub),
        grid=(batch, g_, nqb),
        in_specs=[
            pl.BlockSpec(memory_space=pltpu.SMEM),
            pl.BlockSpec((1, 1, sub, 1, SEL_LANES), lambda b, g, q: (b, g, q, 0, 0),
                         memory_space=pltpu.SMEM),
            pl.BlockSpec((nq, gq), lambda b, g, q: (b * nqb + q, A_Q // gq + g)),
            pl.BlockSpec((nq, LANE), lambda b, g, q: (b * nqb + q, B_GATE // LANE + g)),
            pl.BlockSpec((1, 1, nq, SEL_LANES), lambda b, g, q: (b, g, q, 0)),
            pl.BlockSpec((1, 1, sub, NSA_HEAD_DIM, gq), lambda b, g, q: (b, g, q, 0, 0)),
            kv_spec(2), kv_spec(3), kv_spec(4), kv_spec(5),
            const_spec(onehot), const_spec(wext),
            pl.BlockSpec((1,) + dbias.shape[1:], lambda b, g, q: (g, 0, 0)),
            pl.BlockSpec((1,) + ubias.shape[1:], lambda b, g, q: (g, 0, 0)),
            const_spec(wbias),
        ],
        out_specs=pl.BlockSpec((nq, gq), lambda b, g, q: (b * nqb + q, g)),
        out_shape=jax.ShapeDtypeStruct((n, NSA_WIDTH), BF16),
        scratch_shapes=[
            pltpu.VMEM((seq + kt, dk), BF16),
            pltpu.VMEM((n_t + 1, dh, kt), BF16),
            pltpu.VMEM((seq + WINDOW, dh + LANE), BF16),
            pltpu.VMEM((n_t + WIN_TILES, dh, kt), BF16),
            pltpu.VMEM((rq, dk), BF16),
            pltpu.VMEM((rq, dk), BF16),
            pltpu.SMEM((n_t + 4 * SEL_TILES_PER_STEP,), jnp.int32),
            pltpu.VMEM((1, rq), F32),
            pltpu.VMEM((1, rq), F32),
            pltpu.VMEM((NSA_HEAD_DIM, rq), F32),
            pltpu.VMEM((NSA_HEAD_DIM, rq), F32),
            pltpu.VMEM((SEL_TILES_PER_STEP * KEY_TILE, rq), F32),
            pltpu.VMEM((SEL_TILES_PER_STEP * KEY_TILE, rq), F32),
        ],
        compiler_params=pltpu.CompilerParams(
            dimension_semantics=("parallel", "parallel", "arbitrary"), vmem_limit_bytes=VMEM_LIMIT),
        name="nsa_attn",
    )(slopes, flags, seg_a, seg_b, pen, ocmp, seg_a, seg_a, seg_a, seg_a, onehot, wext,
      dbias, ubias, wbias)


def _nsa_tables(slopes, seq):
    g_, kt = NSA_KV_GROUPS, KEY_TILE
    pos = np.arange(seq)
    onehot = jnp.asarray(pos[:, None] // SEL_BLOCK == np.arange(SEL_LANES)[None, :], BF16)
    ext = np.zeros((seq, LANE), np.float32)
    ext[:, WCOL_HI] = (pos + WINDOW) // SEL_BLOCK
    ext[:, WCOL_LO] = (pos + WINDOW) % SEL_BLOCK
    ext[:, WCOL_QHI] = 1.0
    ext[:, WCOL_QLO] = 1.0
    wext = jnp.asarray(ext, BF16)

    nq = _nsa_sub_blocks(seq) * Q_BLOCK

    def alibi_in_block(rows):
        u = jnp.asarray((np.arange(rows) % SEL_BLOCK).astype(np.float32))[None, :, None, None]
        t = jnp.broadcast_to(slopes.reshape(g_, 1, NSA_REP, 1) * u, (g_, rows, NSA_REP, nq))
        return t.reshape(g_, rows, NSA_REP * nq)

    ubias = alibi_in_block(SEL_TILES_PER_STEP * kt)
    kq = np.arange(nq)[:, None] <= np.arange(nq)[None, :]
    causal = np.tile(np.where(kq, 0.0, NEG).astype(np.float32), (1, NSA_REP))
    dbias = alibi_in_block(nq) + jnp.asarray(causal)[None]
    ki = np.arange(WINDOW + nq)[:, None]
    qi = np.arange(nq)[None, :]
    band = np.where((ki > qi) & (ki <= qi + WINDOW), 0.0, NEG).astype(np.float32)
    wbias = jnp.asarray(np.tile(band, (1, NSA_REP)))
    return onehot, wext, dbias, ubias, wbias


def _log_sigmoid(x):
    return jnp.minimum(x, 0.0) - jnp.log(1.0 + jnp.exp(-jnp.abs(x)))


def _split3(x):
    hi = x.astype(BF16)
    r1 = x - hi.astype(F32)
    mid = r1.astype(BF16)
    lo = (r1 - mid.astype(F32)).astype(BF16)
    return hi, mid, lo


def _mlstm_kernel(bias_ref, q_ref, k_ref, v_ref, o_ref, ifc_ref, ifr_ref, ng_ref,
                  tri_ref, y_ref, c_ref, n_ref, m_ref):
    ch = pl.program_id(1)
    nh, dh = MLSTM_HEADS, MLSTM_HEAD_DIM
    L = q_ref.shape[0]

    @pl.when(ch == 0)
    def _():
        c_ref[...] = jnp.zeros(c_ref.shape, F32)
        n_ref[...] = jnp.zeros(n_ref.shape, F32)
        m_ref[...] = jnp.zeros(m_ref.shape, F32)

    tri = tri_ref[...]
    lane8 = lax.broadcasted_iota(jnp.int32, (1, LANE), 1)
    bias_c = jnp.zeros((1, LANE), F32)
    for h in range(nh):
        bias_c = jnp.where(lane8 == h, bias_ref[h], bias_c)
        bias_c = jnp.where(lane8 == nh + h, bias_ref[nh + h], bias_c)
    pre_c = ifc_ref[...] + bias_c
    cum_c = sum(_dot(tri, part) for part in _split3(_log_sigmoid(pre_c)))
    row8 = lax.broadcasted_iota(jnp.int32, (8, 1), 0)
    bias_r = jnp.zeros((8, 1), F32)
    for h in range(2 * nh):
        bias_r = jnp.where(row8 == h, bias_ref[h], bias_r)
    pre_r = ifr_ref[...] + bias_r
    cum_r = sum(_dot_nt(part, tri) for part in _split3(_log_sigmoid(pre_r)))

    rr = lax.broadcasted_iota(jnp.int32, (L, 1), 0)
    cc = lax.broadcasted_iota(jnp.int32, (1, L), 1)
    causal = cc <= rr

    for h in range(nh):
        cols = slice(h * dh, (h + 1) * dh)
        qb = q_ref[:, cols]
        kb = k_ref[:, cols]
        vh = v_ref[:, cols]
        qh = qb.astype(F32)
        kh = kb.astype(F32)
        b_c = cum_c[:, nh + h:nh + h + 1]
        li_c = pre_c[:, h:h + 1]
        b_r = cum_r[nh + h:nh + h + 1, :]
        li_r = pre_r[h:h + 1, :]
        m_prev = m_ref[h:h + 1, 0:1]

        dmat = jnp.where(causal, b_c - b_r + li_r, NEG)
        a = b_c + m_prev
        m_j = jnp.maximum(a, jnp.max(dmat, axis=1, keepdims=True))
        w_intra = jnp.exp(dmat - m_j)
        w_inter = jnp.exp(a - m_j)
        sc = _dot_nt(qb, kb) * w_intra
        c_old = c_ref[h]
        n_old = n_ref[h:h + 1, :]
        num = w_inter * _dot(qb, c_old.astype(BF16)) + _dot(sc.astype(BF16), vh)
        den = (w_inter * jnp.sum(qh * n_old, axis=1, keepdims=True)
               + jnp.sum(sc, axis=1, keepdims=True))
        hid = num / jnp.maximum(jnp.abs(den), jnp.exp(-m_j))

        g_tot = b_r[:, L - 1:L]
        lw_c = g_tot - b_c + li_c
        lw_r = g_tot - b_r + li_r
        m_new = jnp.maximum(g_tot + m_prev, jnp.max(lw_r, axis=1, keepdims=True))
        decay = jnp.exp(g_tot + m_prev - m_new)
        kw = jnp.exp(lw_c - m_new) * kh
        c_ref[h] = decay * c_old + _dot(kw.T.astype(BF16), vh)
        n_ref[h:h + 1, :] = decay * n_old + jnp.sum(kw, axis=0, keepdims=True)
        m_ref[h:h + 1, :] = jnp.broadcast_to(m_new, (1, LANE))

        hn = hid * lax.rsqrt(jnp.mean(hid * hid, axis=-1, keepdims=True) + EPS) * ng_ref[:, cols]
        y_ref[:, cols] = (_sigmoid(o_ref[:, cols]) * hn).astype(y_ref.dtype)


def _mlstm(seg_a, seg_b, if_rows, bias, norm_g, tri, batch, seq, chunk):
    n = batch * seq
    nc = seq // chunk
    w = MLSTM_WIDTH
    nh, dh = MLSTM_HEADS, MLSTM_HEAD_DIM

    def col_spec(off):
        return pl.BlockSpec((chunk, w), lambda b, c, o=off // w: (b * nc + c, o))

    return pl.pallas_call(
        _mlstm_kernel,
        grid=(batch, nc),
        in_specs=[
            pl.BlockSpec(memory_space=pltpu.SMEM),
            col_spec(A_QK), col_spec(A_QK + w), col_spec(A_V), col_spec(B_O),
            pl.BlockSpec((chunk, LANE), lambda b, c: (b * nc + c, B_IF // LANE)),
            pl.BlockSpec((8, chunk), lambda b, c: (0, b * nc + c)),
            pl.BlockSpec((1, w), lambda b, c: (0, 0)),
            pl.BlockSpec((chunk, chunk), lambda b, c: (0, 0)),
        ],
        out_specs=pl.BlockSpec((chunk, w), lambda b, c: (b * nc + c, 0)),
        out_shape=jax.ShapeDtypeStruct((n, w), BF16),
        scratch_shapes=[
            pltpu.VMEM((nh, dh, dh), F32),
            pltpu.VMEM((8, dh), F32),
            pltpu.VMEM((8, LANE), F32),
        ],
        compiler_params=pltpu.CompilerParams(
            dimension_semantics=("parallel", "arbitrary"), vmem_limit_bytes=VMEM_LIMIT),
        name="mlstm",
    )(bias, seg_a, seg_a, seg_a, seg_b, seg_b, if_rows, norm_g, tri)


def _outproj_kernel(x_ref, ya_ref, ym_ref, wa_ref, wm_ref, o_ref):
    o_ref[...] = x_ref[...] + _dot(ya_ref[...], wa_ref[...]) + _dot(ym_ref[...], wm_ref[...])


def _outproj(x2, ya, ym, wa, wm, tm):
    n, d = x2.shape
    return pl.pallas_call(
        _outproj_kernel,
        grid=(n // tm,),
        in_specs=[
            pl.BlockSpec((tm, d), lambda i: (i, 0)),
            pl.BlockSpec((tm, ya.shape[1]), lambda i: (i, 0)),
            pl.BlockSpec((tm, ym.shape[1]), lambda i: (i, 0)),
            pl.BlockSpec(wa.shape, lambda i: (0, 0)),
            pl.BlockSpec(wm.shape, lambda i: (0, 0)),
        ],
        out_specs=pl.BlockSpec((tm, d), lambda i: (i, 0)),
        out_shape=jax.ShapeDtypeStruct((n, d), F32),
        compiler_params=pltpu.CompilerParams(
            dimension_semantics=("parallel",), vmem_limit_bytes=VMEM_LIMIT),
        name="outproj",
    )(x2, ya, ym, wa, wm)


def _mlp_kernel(x_ref, g_ref, w1_ref, w2_ref, gf_ref, o_ref, h_ref):
    f = pl.program_id(1)

    @pl.when(f == 0)
    def _():
        x = x_ref[...]
        r = lax.rsqrt(jnp.mean(x * x, axis=-1, keepdims=True) + EPS)
        h_ref[...] = (x * r * g_ref[...]).astype(BF16)
        o_ref[...] = x

    u = jnp.maximum(_dot(h_ref[...], w1_ref[...]), 0.0)
    o_ref[...] += _dot((u * u).astype(BF16), w2_ref[...])

    @pl.when(f == pl.num_programs(1) - 1)
    def _():
        x2 = o_ref[...]
        r = lax.rsqrt(jnp.mean(x2 * x2, axis=-1, keepdims=True) + EPS)
        o_ref[...] = x2 * r * gf_ref[...]


def _mlp(x1, g, w1, w2, gf, tm, tf):
    n, d = x1.shape
    dff = w1.shape[1]
    return pl.pallas_call(
        _mlp_kernel,
        grid=(n // tm, dff // tf),
        in_specs=[
            pl.BlockSpec((tm, d), lambda i, f: (i, 0)),
            pl.BlockSpec((1, d), lambda i, f: (0, 0)),
            pl.BlockSpec((d, tf), lambda i, f: (0, f)),
            pl.BlockSpec((tf, d), lambda i, f: (f, 0)),
            pl.BlockSpec((1, d), lambda i, f: (0, 0)),
        ],
        out_specs=pl.BlockSpec((tm, d), lambda i, f: (i, 0)),
        out_shape=jax.ShapeDtypeStruct((n, d), F32),
        scratch_shapes=[pltpu.VMEM((tm, d), BF16)],
        compiler_params=pltpu.CompilerParams(
            dimension_semantics=("parallel", "arbitrary"), vmem_limit_bytes=VMEM_LIMIT),
        name="mlp",
    )(x1, g, w1, w2, gf)


def _row_tile(n, want):
    t = want
    while n % t:
        t //= 2
    return t


def _layer(x2, batch, seq, norm_mix_g, w_in, w_cmp_k1, w_cmp_k2, pos_cmp_k, w_cmp_v1, w_cmp_v2,
           pos_cmp_v, conv_w, conv_b, b_igate, b_fgate, mlstm_norm_g, w_out, norm_mlp_g,
           w_mlp_in, w_mlp_out):
    n, d = x2.shape
    assert seq % Q_BLOCK == 0 and seq >= WINDOW + Q_BLOCK and seq // SEL_BLOCK <= SEL_LANES
    g_ = NSA_KV_GROUPS
    nh = MLSTM_HEADS

    c_gate = NSA_WIDTH + 6 * NSA_KV_WIDTH
    c_qk = c_gate + NSA_HEADS * N_BRANCH
    c_v = c_qk + 2 * MLSTM_WIDTH
    c_o = c_v + MLSTM_WIDTH
    c_i = c_o + MLSTM_WIDTH
    c_f = c_i + nh
    w_a = jnp.concatenate([w_in[:, c_v:c_o], w_in[:, c_qk:c_v], w_in[:, :c_gate]], axis=1).astype(BF16)
    gate_cols = []
    per_g = NSA_REP * N_BRANCH
    for g in range(g_):
        gate_cols += [w_in[:, c_gate + g * per_g:c_gate + (g + 1) * per_g],
                      jnp.zeros((d, LANE - per_g), w_in.dtype)]
    w_b = jnp.concatenate(
        [w_in[:, c_o:c_i]] + gate_cols
        + [w_in[:, c_i:c_f + nh], jnp.zeros((d, SEG_B - B_IF - 2 * nh), w_in.dtype)],
        axis=1).astype(BF16)
    scale_a = jnp.concatenate([jnp.ones((1, A_QK + MLSTM_WIDTH), F32),
                               jnp.full((1, MLSTM_WIDTH), MLSTM_HEAD_DIM ** -0.5, F32),
                               jnp.full((1, NSA_WIDTH), NSA_HEAD_DIM ** -0.5, F32),
                               jnp.ones((1, SEG_A - A_KV), F32)], axis=1)
    scale_b = jnp.ones((1, SEG_B), F32)
    conv_pad = ((0, 0), (A_QK, SEG_A - A_Q))
    cw_a = jnp.pad(conv_w, conv_pad)
    cb_a = jnp.pad(conv_b.reshape(1, -1), conv_pad)
    g_mix = norm_mix_g.reshape(1, d)

    tm = _row_tile(seq, ROW_TILE)
    seg_a = _inproj(x2, g_mix, w_a, scale_a, cw_a, cb_a, BF16, tm, COL_TILE, (A_QK, A_Q), seq)
    seg_b = _inproj(x2, g_mix, w_b, scale_b, jnp.zeros((CONV_WIDTH, SEG_B), F32),
                    jnp.zeros((1, SEG_B), F32), F32, tm, COL_TILE, (0, 0), seq)

    n_sub = seq // CMP_STRIDE
    cmp_in = seg_a[:, A_KV:A_KV + 2 * NSA_KV_WIDTH]
    cmp_in = cmp_in.reshape(batch, n_sub, CMP_STRIDE, 2 * g_, NSA_HEAD_DIM)
    cmp_in = cmp_in.transpose(0, 3, 1, 2, 4).reshape(batch, 2 * g_, n_sub, CMP_STRIDE * NSA_HEAD_DIM)
    w1s = jnp.stack([w_cmp_k1, w_cmp_v1]).astype(BF16)
    w2s = jnp.stack([w_cmp_k2, w_cmp_v2]).astype(BF16)
    poss = jnp.stack([pos_cmp_k, pos_cmp_v]).reshape(2, 1, CMP_BLOCK * NSA_HEAD_DIM)
    poss = jnp.broadcast_to(poss, (2, 8, CMP_BLOCK * NSA_HEAD_DIM)).astype(BF16)
    kvc, kvct = _compress(cmp_in, w1s, w2s, poss)

    cmp_start = np.arange(n_sub) * CMP_STRIDE
    sel_start = np.arange(SEL_LANES) * SEL_BLOCK
    ovt = ((cmp_start[None, :] < sel_start[:, None] + SEL_BLOCK)
           & (cmp_start[None, :] + CMP_BLOCK - 1 >= sel_start[:, None])
           & (np.arange(n_sub)[None, :] < n_sub - CMP_BLOCK // CMP_STRIDE + 1))
    ovt = jnp.asarray(ovt, BF16)
    slopes = jnp.exp2(-8.0 * jnp.arange(1, NSA_HEADS + 1, dtype=F32) / NSA_HEADS)
    ocmp, pen, flags = _nsa_cmp(seg_a, kvc, kvct, ovt, slopes, batch, seq)
    onehot, wext, dbias, ubias, wbias = _nsa_tables(slopes, seq)
    y_a = _nsa_attn(seg_a, seg_b, pen, ocmp, flags, onehot, wext, dbias, ubias, wbias,
                    slopes, batch, seq)

    chunk = 256 if seq % 256 == 0 else 128
    if_rows = seg_b[:, B_IF:B_IF + 2 * nh].T
    bias = jnp.concatenate([b_igate, b_fgate]).astype(F32)
    tri = jnp.asarray(np.tril(np.ones((chunk, chunk), np.float32)), BF16)
    y_m = _mlstm(seg_a, seg_b, if_rows, bias, mlstm_norm_g.reshape(1, -1), tri, batch, seq, chunk)

    w_o = w_out.astype(BF16)
    x1 = _outproj(x2, y_a, y_m, w_o[:NSA_WIDTH], w_o[NSA_WIDTH:], _row_tile(n, 512))
    return x1, (norm_mlp_g.reshape(1, d), w_mlp_in.astype(BF16), w_mlp_out.astype(BF16))


def kernel(x, norm_mix_g, w_in, w_cmp_k1, w_cmp_k2, pos_cmp_k, w_cmp_v1, w_cmp_v2, pos_cmp_v, conv_w, conv_b, b_igate, b_fgate, mlstm_norm_g, w_out, norm_mlp_g, w_mlp_in, w_mlp_out, norm_f_g):
    batch, seq, d = x.shape
    depth = w_in.shape[0]
    assert depth == 1, "the final RMSNorm is fused into the last layer's channel mixer"
    x2 = x.reshape(batch * seq, d)
    tm = _row_tile(batch * seq, ROW_TILE)
    for l in range(depth):
        x1, (g_mlp, w1, w2) = _layer(
            x2, batch, seq, norm_mix_g[l], w_in[l], w_cmp_k1[l], w_cmp_k2[l], pos_cmp_k[l],
            w_cmp_v1[l], w_cmp_v2[l], pos_cmp_v[l], conv_w[l], conv_b[l], b_igate[l], b_fgate[l],
            mlstm_norm_g[l], w_out[l], norm_mlp_g[l], w_mlp_in[l], w_mlp_out[l])
        x2 = _mlp(x1, g_mlp, w1, w2, norm_f_g.reshape(1, d), tm, 512)
    return x2.reshape(batch, seq, d)
```

```python
import functools

import numpy as np
import jax
import jax.numpy as jnp
from jax import lax
from jax.experimental import pallas as pl
from jax.experimental.pallas import tpu as pltpu

F32 = jnp.float32
BF16 = jnp.bfloat16

EPS = 1e-6
NEG = -1e30
FORCE_BONUS = 1e4
PICKED = -3e38
MASK_BIG = 1e30

D_MODEL = 2048
NSA_HEAD_DIM = 128
NSA_WIDTH = D_MODEL // 2
NSA_HEADS = NSA_WIDTH // NSA_HEAD_DIM
NSA_REP = 4
NSA_KV_GROUPS = NSA_HEADS // NSA_REP
NSA_KV_WIDTH = NSA_KV_GROUPS * NSA_HEAD_DIM
CMP_BLOCK = 32
CMP_STRIDE = 16
SEL_BLOCK = 64
SEL_TOPK = 16
WINDOW = 512
Q_BLOCK = 128
N_BRANCH = 3
MLSTM_HEAD_DIM = 256
MLSTM_WIDTH = D_MODEL - NSA_WIDTH
MLSTM_HEADS = MLSTM_WIDTH // MLSTM_HEAD_DIM
CONV_WIDTH = 4
D_FF = 4 * D_MODEL

LANE = 128
SEL_LANES = 128
KEY_TILE = 128
SEL_TILES_PER_STEP = 2
WIN_TILES = WINDOW // KEY_TILE
VMEM_LIMIT = 56 * 1024 * 1024
ROW_TILE = 1024

A_V, A_QK = 0, MLSTM_WIDTH
A_Q = A_QK + 2 * MLSTM_WIDTH
A_KV = A_Q + NSA_WIDTH
SEG_A = A_KV + 6 * NSA_KV_WIDTH
B_O = 0
B_GATE = MLSTM_WIDTH
B_IF = B_GATE + NSA_KV_GROUPS * LANE
SEG_B = B_IF + 2 * LANE
COL_TILE = 512

WCOL_PAD, WCOL_HI, WCOL_LO, WCOL_QHI, WCOL_QLO = 0, 1, 2, 3, 4


def _dot(a, b):
    return jnp.dot(a, b, preferred_element_type=F32)


def _dot_nt(a, b):
    return lax.dot_general(a, b, (((1,), (1,)), ((), ())), preferred_element_type=F32)


def _sigmoid(x):
    return 1.0 / (1.0 + jnp.exp(-x))


def _shifted(x, tail, s):
    xs = pltpu.roll(x, s, axis=0)
    ts = pltpu.roll(tail, s, axis=0)
    row8 = lax.broadcasted_iota(jnp.int32, (8, 1), 0)
    head = jnp.where(row8 < s, ts, xs[:8])
    return jnp.concatenate([head, xs[8:]], axis=0)


def _conv_silu(x, tail, w, b):
    y = b + _shifted(x, tail, CONV_WIDTH - 1) * w[0:1]
    for i in range(1, CONV_WIDTH - 1):
        y = y + _shifted(x, tail, CONV_WIDTH - 1 - i) * w[i:i + 1]
    y = y + x * w[CONV_WIDTH - 1:CONV_WIDTH]
    return y * _sigmoid(y)


def _inproj_kernel(x_ref, g_ref, w_ref, cs_ref, cw_ref, cb_ref, o_ref, h_ref, halo_ref, *,
                   conv_lo, conv_hi, tiles_per_seq):
    i = pl.program_id(0)
    j = pl.program_id(1)

    @pl.when(j == 0)
    def _():
        x = x_ref[...]
        r = lax.rsqrt(jnp.mean(x * x, axis=-1, keepdims=True) + EPS)
        h_ref[...] = (x * r * g_ref[...]).astype(BF16)

    is_conv = (j >= conv_lo) & (j < conv_hi)

    @pl.when(jnp.logical_not(is_conv))
    def _():
        o_ref[...] = (_dot(h_ref[...], w_ref[...]) * cs_ref[...]).astype(o_ref.dtype)

    if conv_hi > conv_lo:
        @pl.when(is_conv)
        def _():
            slot = j - conv_lo

            @pl.when(i % tiles_per_seq == 0)
            def _():
                halo_ref[slot] = jnp.zeros(halo_ref.shape[1:], F32)

            acc = _dot(h_ref[...], w_ref[...])
            tail = halo_ref[slot]
            halo_ref[slot] = acc[acc.shape[0] - 8:]
            y = _conv_silu(acc, tail, cw_ref[...], cb_ref[...])
            o_ref[...] = (y * cs_ref[...]).astype(o_ref.dtype)


def _inproj(x2, g, w, cs, cw, cb, out_dtype, tm, tn, conv_cols, seq):
    n, d = x2.shape
    nc = w.shape[1]
    conv_lo, conv_hi = conv_cols[0] // tn, conv_cols[1] // tn
    assert conv_cols[0] % tn == 0 and conv_cols[1] % tn == 0 and seq % tm == 0
    return pl.pallas_call(
        functools.partial(_inproj_kernel, conv_lo=conv_lo, conv_hi=conv_hi, tiles_per_seq=seq // tm),
        grid=(n // tm, nc // tn),
        in_specs=[
            pl.BlockSpec((tm, d), lambda i, j: (i, 0)),
            pl.BlockSpec((1, d), lambda i, j: (0, 0)),
            pl.BlockSpec((d, tn), lambda i, j: (0, j)),
            pl.BlockSpec((1, tn), lambda i, j: (0, j)),
            pl.BlockSpec((CONV_WIDTH, tn), lambda i, j: (0, j)),
            pl.BlockSpec((1, tn), lambda i, j: (0, j)),
        ],
        out_specs=pl.BlockSpec((tm, tn), lambda i, j: (i, j)),
        out_shape=jax.ShapeDtypeStruct((n, nc), out_dtype),
        scratch_shapes=[pltpu.VMEM((tm, d), BF16),
                        pltpu.VMEM((max(conv_hi - conv_lo, 1), 8, tn), F32)],
        compiler_params=pltpu.CompilerParams(
            dimension_semantics=("arbitrary", "arbitrary"), vmem_limit_bytes=VMEM_LIMIT),
        name="inproj",
    )(x2, g, w, cs, cw, cb)


def _compress_kernel(x_ref, w1_ref, w2_ref, pos_ref, o_ref, ot_ref, xf_ref):
    n_sub = o_ref.shape[2]
    dh = NSA_HEAD_DIM
    xf_ref[...] = x_ref[...].astype(F32)
    acc = jnp.zeros((n_sub, 2 * dh), F32)
    posw = jnp.zeros((1, dh), F32)
    for p in range(CMP_STRIDE):
        wp = w1_ref[0, p]
        acc = acc + _dot(xf_ref[pl.ds(p, n_sub, stride=CMP_STRIDE), :].astype(BF16), wp)
        pw = _dot(pos_ref[0, p], wp)
        posw = posw + pw[0:1, :dh] + pw[1:2, dh:]
    bot = pltpu.roll(acc[:, dh:], n_sub - 1, axis=0)
    pre = acc[:, :dh] + bot + posw
    hid = pre * _sigmoid(pre)
    out = _dot(hid.astype(BF16), w2_ref[0])
    o_ref[0, 0] = out.astype(o_ref.dtype)
    ot_ref[0, 0] = out.T.astype(ot_ref.dtype)


def _compress(seg_a, w1r, w2s, posr, batch, seq):
    g = NSA_KV_GROUPS
    c = 2 * g
    dh = NSA_HEAD_DIM
    n_sub = seq // CMP_STRIDE
    return pl.pallas_call(
        _compress_kernel,
        grid=(batch, c),
        in_specs=[
            pl.BlockSpec((seq, dh), lambda i, j: (i, A_KV // dh + j)),
            pl.BlockSpec((1, CMP_STRIDE, dh, 2 * dh), lambda i, j: (j // g, 0, 0, 0)),
            pl.BlockSpec((1, dh, dh), lambda i, j: (j // g, 0, 0)),
            pl.BlockSpec((1, CMP_STRIDE, 8, dh), lambda i, j: (j // g, 0, 0, 0)),
        ],
        out_specs=[pl.BlockSpec((1, 1, n_sub, dh), lambda i, j: (i, j, 0, 0)),
                   pl.BlockSpec((1, 1, dh, n_sub), lambda i, j: (i, j, 0, 0))],
        out_shape=[jax.ShapeDtypeStruct((batch, c, n_sub, dh), BF16),
                   jax.ShapeDtypeStruct((batch, c, dh, n_sub), BF16)],
        scratch_shapes=[pltpu.VMEM((seq, dh), F32)],
        compiler_params=pltpu.CompilerParams(
            dimension_semantics=("parallel", "parallel"), vmem_limit_bytes=VMEM_LIMIT),
        name="compress",
    )(seg_a, w1r, w2s, posr)


def _stack_heads(q_all):
    dh = NSA_HEAD_DIM
    return jnp.concatenate([q_all[:, r * dh:(r + 1) * dh] for r in range(NSA_REP)], axis=0)


def _nsa_cmp_kernel(slopes_ref, q_ref, kc_ref, vct_ref, ovt_ref, ocmp_ref, pen_ref, flag_ref, *,
                    seq, sub_blocks):
    step = pl.program_id(2)
    nq_step = sub_blocks * Q_BLOCK
    chunk = SEL_LANES
    n_chunks = kc_ref.shape[2] // chunk
    need = ((step + 1) * nq_step - CMP_BLOCK) // CMP_STRIDE + 1
    n_need = (need + chunk - 1) // chunk
    for v in range(1, n_chunks + 1):
        cond = (n_need == v) if v < n_chunks else (n_need >= v)
        if v == 1:
            cond = n_need <= 1

        @pl.when(cond)
        def _(v=v):
            for sub in range(sub_blocks):
                rows = slice(sub * Q_BLOCK, (sub + 1) * Q_BLOCK)
                _nsa_cmp_block(slopes_ref, q_ref[rows, :], kc_ref, vct_ref, ovt_ref,
                               ocmp_ref.at[0, 0, sub], pen_ref.at[0, 0, rows], flag_ref.at[0, 0, sub],
                               step * sub_blocks + sub, seq, v * chunk)


def _nsa_cmp_block(slopes_ref, q_all, kc_ref, vct_ref, ovt_ref, ocmp_ref, pen_ref, flag_ref, qb, seq,
                   n_use):
    g = pl.program_id(1)
    nq = Q_BLOCK
    n_cpad = n_use
    n_cmp = seq // CMP_STRIDE - CMP_BLOCK // CMP_STRIDE + 1
    t0 = qb * nq

    s_t = _dot_nt(kc_ref[0, 0, :n_use, :], _stack_heads(q_all))
    n_s = lax.broadcasted_iota(jnp.int32, (n_cpad, 1), 0)
    q_l = lax.broadcasted_iota(jnp.int32, (1, nq), 1)
    dist = (t0 - (CMP_BLOCK - 1)) + q_l - n_s * CMP_STRIDE
    valid = (dist >= 0) & (n_s < n_cmp)
    dist_f = dist.astype(F32)
    probs = []
    p_sum = jnp.zeros((n_cpad, nq), F32)
    for r in range(NSA_REP):
        slope = slopes_ref[g * NSA_REP + r]
        s = jnp.where(valid, s_t[:, r * nq:(r + 1) * nq] - slope * dist_f, NEG)
        m = jnp.max(s, axis=0, keepdims=True)
        e = jnp.exp(s - m)
        inv = jnp.where(m > 0.5 * NEG, 1.0 / jnp.sum(e, axis=0, keepdims=True), 0.0)
        p = e * inv
        probs.append(p.astype(BF16))
        p_sum = p_sum + p
    ocmp_ref[...] = _dot(vct_ref[0, 0, :, :n_use], jnp.concatenate(probs, axis=1))

    p_hi = p_sum.astype(BF16)
    p_lo = (p_sum - p_hi.astype(F32)).astype(BF16)
    ovt = ovt_ref[:, :n_use]
    imp = _dot(ovt, p_hi) + _dot(ovt, p_lo)
    j_i = lax.broadcasted_iota(jnp.int32, (SEL_LANES, 1), 0)
    t_l = t0 + q_l
    cur = t_l // SEL_BLOCK
    forced = (j_i == 0) | (j_i == cur) | (j_i == cur - 1)
    causal_blk = j_i * SEL_BLOCK <= t_l
    val = jnp.where(causal_blk, jnp.where(forced, imp + FORCE_BONUS, imp), NEG)
    j_f = j_i.astype(F32)
    sel_t = jnp.zeros((SEL_LANES, nq), F32)
    for _ in range(min(SEL_TOPK, seq // SEL_BLOCK)):
        mx = jnp.max(val, axis=0, keepdims=True)
        first = jnp.min(jnp.where(val == mx, j_f, float(SEL_LANES)), axis=0, keepdims=True)
        pick = j_f == first
        sel_t = jnp.where(pick, 1.0, sel_t)
        val = jnp.where(pick, PICKED, val)
    sel = sel_t.T
    pen_ref[...] = ((sel - 1.0) * MASK_BIG).astype(pen_ref.dtype)
    flag_ref[...] = (jnp.max(sel, axis=0, keepdims=True) > 0.0).astype(jnp.int32)


def _nsa_cmp(seg_a, kvc, kvct, ovt, slopes, batch, seq):
    nqb = seq // Q_BLOCK
    gq = NSA_REP * NSA_HEAD_DIM
    g_ = NSA_KV_GROUPS
    n_cpad = kvc.shape[2]
    sub = _nsa_sub_blocks(seq)
    nstep = nqb // sub
    return pl.pallas_call(
        functools.partial(_nsa_cmp_kernel, seq=seq, sub_blocks=sub),
        grid=(batch, g_, nstep),
        in_specs=[
            pl.BlockSpec(memory_space=pltpu.SMEM),
            pl.BlockSpec((sub * Q_BLOCK, gq), lambda b, g, q: (b * nstep + q, A_Q // gq + g)),
            pl.BlockSpec((1, 1, n_cpad, NSA_HEAD_DIM), lambda b, g, q: (b, g, 0, 0)),
            pl.BlockSpec((1, 1, NSA_HEAD_DIM, n_cpad), lambda b, g, q: (b, g_ + g, 0, 0)),
            pl.BlockSpec((SEL_LANES, n_cpad), lambda b, g, q: (0, 0)),
        ],
        out_specs=[
            pl.BlockSpec((1, 1, sub, NSA_HEAD_DIM, gq), lambda b, g, q: (b, g, q, 0, 0)),
            pl.BlockSpec((1, 1, sub * Q_BLOCK, SEL_LANES), lambda b, g, q: (b, g, q, 0)),
            pl.BlockSpec((1, 1, sub, 1, SEL_LANES), lambda b, g, q: (b, g, q, 0, 0)),
        ],
        out_shape=[
            jax.ShapeDtypeStruct((batch, g_, nqb, NSA_HEAD_DIM, gq), F32),
            jax.ShapeDtypeStruct((batch, g_, seq, SEL_LANES), BF16),
            jax.ShapeDtypeStruct((batch, g_, nqb, 1, SEL_LANES), jnp.int32),
        ],
        compiler_params=pltpu.CompilerParams(
            dimension_semantics=("parallel", "parallel", "parallel"), vmem_limit_bytes=VMEM_LIMIT),
        name="nsa_cmp",
    )(slopes, seg_a, kvc, kvct, ovt)


def _nsa_attn_kernel(slopes_ref, flag_ref, q_ref, gate_ref, pen_ref, ocmp_ref, ks_ref, vs_ref,
                     kw_ref, vw_ref, onehot_ref, wext_ref, dbias_ref, ubias_ref, wbias_ref, o_ref,
                     ksel_ref, vselt_ref, kwin_ref, vwint_ref,
                     qa_ref, qw_ref, list_ref, m_ref, l_ref, acc_ref, owin_ref, sa_ref, sb_ref,
                     *, seq, sub_blocks):
    g = pl.program_id(1)
    step = pl.program_id(2)
    dh = NSA_HEAD_DIM
    nq = sub_blocks * Q_BLOCK
    kt = KEY_TILE
    per = SEL_TILES_PER_STEP
    pad_tile = seq // kt
    t0 = step * nq
    tile0 = step * sub_blocks

    @pl.when(step == 0)
    def _():
        ksel_ref[0:seq, :dh] = ks_ref[...]
        ksel_ref[0:seq, dh:] = onehot_ref[...]
        ksel_ref[seq:, :dh] = jnp.zeros((kt, dh), BF16)
        ksel_ref[seq:, dh:] = jnp.ones((kt, SEL_LANES), BF16)
        lane2 = lax.broadcasted_iota(jnp.int32, (WINDOW, dh + LANE), 1)
        kwin_ref[0:WINDOW, :] = jnp.where(lane2 == dh + WCOL_PAD, 1.0, 0.0).astype(BF16)
        kwin_ref[WINDOW:, :dh] = kw_ref[...]
        kwin_ref[WINDOW:, dh:] = wext_ref[...]
        zero_tile = jnp.zeros((dh, kt), BF16)
        vselt_ref[pad_tile] = zero_tile
        for i in range(WIN_TILES):
            vwint_ref[i] = zero_tile

        def transpose_tile(t, carry):
            r0 = pl.multiple_of(t * kt, kt)
            vselt_ref[t] = vs_ref[pl.ds(r0, kt), :].astype(F32).T.astype(BF16)
            vwint_ref[t + WIN_TILES] = vw_ref[pl.ds(r0, kt), :].astype(F32).T.astype(BF16)
            return carry

        lax.fori_loop(0, seq // kt, transpose_tile, 0)

    slopes = [slopes_ref[g * NSA_REP + r] for r in range(NSA_REP)]
    row_q = lax.broadcasted_iota(jnp.int32, (nq, 1), 0)
    lane = lax.broadcasted_iota(jnp.int32, (1, LANE), 1)
    q_all = q_ref[...]

    pen = pen_ref[0, 0].astype(F32)
    blk_rel = ((lane - (t0 + row_q) // SEL_BLOCK) * SEL_BLOCK).astype(F32)
    for r in range(NSA_REP):
        qa_ref[r * nq:(r + 1) * nq, :dh] = q_all[:, r * dh:(r + 1) * dh]
        qa_ref[r * nq:(r + 1) * nq, dh:] = (pen + slopes[r] * blk_rel).astype(BF16)

    def scan(k, cnt):
        for j in range(2):
            i = 2 * k + j
            hits = sum(flag_ref[0, 0, a, 0, 2 * i] + flag_ref[0, 0, a, 0, 2 * i + 1]
                       for a in range(sub_blocks))
            act = (hits > 0) & (i < tile0)
            list_ref[cnt] = i
            cnt = cnt + act.astype(jnp.int32)
        return cnt

    cnt = lax.fori_loop(0, (tile0 + 1) // 2, scan, 0)
    for i in range(4 * per):
        list_ref[cnt + i] = pad_tile

    def group_scores(it):
        keys = jnp.concatenate(
            [ksel_ref[pl.ds(pl.multiple_of(list_ref[it * per + i] * kt, kt), kt), :]
             for i in range(per)], axis=0)
        return _dot_nt(keys, qa_ref[...]) + ubias_ref[0]

    s = _dot_nt(ksel_ref[pl.ds(pl.multiple_of(t0, kt), nq), :], qa_ref[...]) + dbias_ref[0]
    m0 = jnp.max(s, axis=0, keepdims=True)
    p = jnp.exp(s - m0)
    m_ref[...] = m0
    l_ref[...] = jnp.sum(p, axis=0, keepdims=True)
    vals = jnp.concatenate([vselt_ref[tile0 + a] for a in range(sub_blocks)], axis=1)
    acc_ref[...] = _dot(vals, p.astype(BF16))
    sa_ref[...] = group_scores(0)

    tp = t0 + WINDOW + row_q
    t_hi = (tp // SEL_BLOCK).astype(F32)
    t_lo = (tp % SEL_BLOCK).astype(F32)
    for r in range(NSA_REP):
        sl = slopes[r]
        ext = jnp.where(lane == WCOL_PAD, -MASK_BIG, 0.0)
        ext = jnp.where(lane == WCOL_HI, sl * SEL_BLOCK, ext)
        ext = jnp.where(lane == WCOL_LO, sl, ext)
        ext = jnp.where(lane == WCOL_QHI, -sl * SEL_BLOCK * t_hi, ext)
        ext = jnp.where(lane == WCOL_QLO, -sl * t_lo, ext)
        qw_ref[r * nq:(r + 1) * nq, :dh] = q_all[:, r * dh:(r + 1) * dh]
        qw_ref[r * nq:(r + 1) * nq, dh:] = ext.astype(BF16)
    wlen = WINDOW + nq
    s = _dot_nt(kwin_ref[pl.ds(pl.multiple_of(t0, kt), wlen), :], qw_ref[...]) + wbias_ref[...]
    e = jnp.exp(s - jnp.max(s, axis=0, keepdims=True))
    vwin = jnp.concatenate([vwint_ref[tile0 + i] for i in range(WIN_TILES + sub_blocks)], axis=1)
    owin_ref[...] = _dot(vwin, e.astype(BF16)) / jnp.sum(e, axis=0, keepdims=True)

    def absorb(s, grp):
        vals = jnp.concatenate([vselt_ref[list_ref[grp * per + i]] for i in range(per)], axis=1)
        m_old = m_ref[...]
        m_new = jnp.maximum(m_old, jnp.max(s, axis=0, keepdims=True))
        alpha = jnp.exp(m_old - m_new)
        p = jnp.exp(s - m_new)
        l_ref[...] = alpha * l_ref[...] + jnp.sum(p, axis=0, keepdims=True)
        acc_ref[...] = alpha * acc_ref[...] + _dot(vals, p.astype(BF16))
        m_ref[...] = m_new

    def sel_body(it, carry):
        s = sa_ref[...]
        sb_ref[...] = group_scores(2 * it + 1)
        absorb(s, 2 * it)
        s = sb_ref[...]
        sa_ref[...] = group_scores(2 * it + 2)
        absorb(s, 2 * it + 1)
        return carry

    lax.fori_loop(0, (cnt + 2 * per - 1) // (2 * per), sel_body, 0)
    o_sel = acc_ref[...] / l_ref[...]
    o_win = owin_ref[...]

    gate_t = _sigmoid(gate_ref[...]).T
    for a in range(sub_blocks):
        o_cmp = ocmp_ref[0, 0, a]
        qs = slice(a * Q_BLOCK, (a + 1) * Q_BLOCK)
        for r in range(NSA_REP):
            cols = slice(r * nq + a * Q_BLOCK, r * nq + (a + 1) * Q_BLOCK)
            c0 = N_BRANCH * r
            out_t = (gate_t[c0:c0 + 1, qs] * o_cmp[:, r * Q_BLOCK:(r + 1) * Q_BLOCK]
                     + gate_t[c0 + 1:c0 + 2, qs] * o_sel[:, cols]
                     + gate_t[c0 + 2:c0 + 3, qs] * o_win[:, cols])
            o_ref[qs, r * dh:(r + 1) * dh] = out_t.T.astype(o_ref.dtype)


def _nsa_sub_blocks(seq):
    return 2 if (seq // Q_BLOCK) % 2 == 0 else 1


def _nsa_attn(seg_a, seg_b, pen, ocmp, flags, onehot, wext, dbias, ubias, wbias, slopes, batch, seq):
    n = batch * seq
    sub = _nsa_sub_blocks(seq)
    nq = sub * Q_BLOCK
    nqb = seq // nq
    gq = NSA_REP * NSA_HEAD_DIM
    g_ = NSA_KV_GROUPS
    dh, kt = NSA_HEAD_DIM, KEY_TILE
    dk = dh + SEL_LANES
    rq = NSA_REP * nq
    n_t = seq // kt

    def kv_spec(kind):
        return pl.BlockSpec((seq, dh), lambda b, g, q, k=kind: (b, A_KV // dh + k * g_ + g))

    def const_spec(arr):
        return pl.BlockSpec(arr.shape, lambda b, g, q, nd=arr.ndim: (0,) * nd)

    return pl.pallas_call(
        functools.partial(_nsa_attn_kernel, seq=seq, sub_blocks=sub),
        grid=(batch, g_, nqb),
        in_specs=[
            pl.BlockSpec(memory_space=pltpu.SMEM),
            pl.BlockSpec((1, 1, sub, 1, SEL_LANES), lambda b, g, q: (b, g, q, 0, 0),
                         memory_space=pltpu.SMEM),
            pl.BlockSpec((nq, gq), lambda b, g, q: (b * nqb + q, A_Q // gq + g)),
            pl.BlockSpec((nq, LANE), lambda b, g, q: (b * nqb + q, B_GATE // LANE + g)),
            pl.BlockSpec((1, 1, nq, SEL_LANES), lambda b, g, q: (b, g, q, 0)),
            pl.BlockSpec((1, 1, sub, NSA_HEAD_DIM, gq), lambda b, g, q: (b, g, q, 0, 0)),
            kv_spec(2), kv_spec(3), kv_spec(4), kv_spec(5),
            const_spec(onehot), const_spec(wext),
            pl.BlockSpec((1,) + dbias.shape[1:], lambda b, g, q: (g, 0, 0)),
            pl.BlockSpec((1,) + ubias.shape[1:], lambda b, g, q: (g, 0, 0)),
            const_spec(wbias),
        ],
        out_specs=pl.BlockSpec((nq, gq), lambda b, g, q: (b * nqb + q, g)),
        out_shape=jax.ShapeDtypeStruct((n, NSA_WIDTH), BF16),
        scratch_shapes=[
            pltpu.VMEM((seq + kt, dk), BF16),
            pltpu.VMEM((n_t + 1, dh, kt), BF16),
            pltpu.VMEM((seq + WINDOW, dh + LANE), BF16),
            pltpu.VMEM((n_t + WIN_TILES, dh, kt), BF16),
            pltpu.VMEM((rq, dk), BF16),
            pltpu.VMEM((rq, dk), BF16),
            pltpu.SMEM((n_t + 4 * SEL_TILES_PER_STEP,), jnp.int32),
            pltpu.VMEM((1, rq), F32),
            pltpu.VMEM((1, rq), F32),
            pltpu.VMEM((NSA_HEAD_DIM, rq), F32),
            pltpu.VMEM((NSA_HEAD_DIM, rq), F32),
            pltpu.VMEM((SEL_TILES_PER_STEP * KEY_TILE, rq), F32),
            pltpu.VMEM((SEL_TILES_PER_STEP * KEY_TILE, rq), F32),
        ],
        compiler_params=pltpu.CompilerParams(
            dimension_semantics=("parallel", "parallel", "arbitrary"), vmem_limit_bytes=VMEM_LIMIT),
        name="nsa_attn",
    )(slopes, flags, seg_a, seg_b, pen, ocmp, seg_a, seg_a, seg_a, seg_a, onehot, wext,
      dbias, ubias, wbias)


def _nsa_tables(slopes, seq):
    g_, kt = NSA_KV_GROUPS, KEY_TILE
    pos = np.arange(seq)
    onehot = jnp.asarray(pos[:, None] // SEL_BLOCK == np.arange(SEL_LANES)[None, :], BF16)
    ext = np.zeros((seq, LANE), np.float32)
    ext[:, WCOL_HI] = (pos + WINDOW) // SEL_BLOCK
    ext[:, WCOL_LO] = (pos + WINDOW) % SEL_BLOCK
    ext[:, WCOL_QHI] = 1.0
    ext[:, WCOL_QLO] = 1.0
    wext = jnp.asarray(ext, BF16)

    nq = _nsa_sub_blocks(seq) * Q_BLOCK

    def alibi_in_block(rows):
        u = jnp.asarray((np.arange(rows) % SEL_BLOCK).astype(np.float32))[None, :, None, None]
        t = jnp.broadcast_to(slopes.reshape(g_, 1, NSA_REP, 1) * u, (g_, rows, NSA_REP, nq))
        return t.reshape(g_, rows, NSA_REP * nq)

    ubias = alibi_in_block(SEL_TILES_PER_STEP * kt)
    kq = np.arange(nq)[:, None] <= np.arange(nq)[None, :]
    causal = np.tile(np.where(kq, 0.0, NEG).astype(np.float32), (1, NSA_REP))
    dbias = alibi_in_block(nq) + jnp.asarray(causal)[None]
    ki = np.arange(WINDOW + nq)[:, None]
    qi = np.arange(nq)[None, :]
    band = np.where((ki > qi) & (ki <= qi + WINDOW), 0.0, NEG).astype(np.float32)
    wbias = jnp.asarray(np.tile(band, (1, NSA_REP)))
    return onehot, wext, dbias, ubias, wbias


def _log_sigmoid(x):
    return jnp.minimum(x, 0.0) - jnp.log(1.0 + jnp.exp(-jnp.abs(x)))


def _split3(x):
    hi = x.astype(BF16)
    r1 = x - hi.astype(F32)
    mid = r1.astype(BF16)
    lo = (r1 - mid.astype(F32)).astype(BF16)
    return hi, mid, lo


def _mlstm_kernel(bias_ref, q_ref, k_ref, v_ref, o_ref, ifc_ref, ifr_ref, ng_ref,
                  tri_ref, y_ref, c_ref, n_ref, m_ref):
    ch = pl.program_id(1)
    nh, dh = MLSTM_HEADS, MLSTM_HEAD_DIM
    L = q_ref.shape[0]

    @pl.when(ch == 0)
    def _():
        c_ref[...] = jnp.zeros(c_ref.shape, F32)
        n_ref[...] = jnp.zeros(n_ref.shape, F32)
        m_ref[...] = jnp.zeros(m_ref.shape, F32)

    tri = tri_ref[...]
    lane8 = lax.broadcasted_iota(jnp.int32, (1, LANE), 1)
    bias_c = jnp.zeros((1, LANE), F32)
    for h in range(nh):
        bias_c = jnp.where(lane8 == h, bias_ref[h], bias_c)
        bias_c = jnp.where(lane8 == nh + h, bias_ref[nh + h], bias_c)
    pre_c = ifc_ref[...] + bias_c
    cum_c = sum(_dot(tri, part) for part in _split3(_log_sigmoid(pre_c)))
    row8 = lax.broadcasted_iota(jnp.int32, (8, 1), 0)
    bias_r = jnp.zeros((8, 1), F32)
    for h in range(2 * nh):
        bias_r = jnp.where(row8 == h, bias_ref[h], bias_r)
    pre_r = ifr_ref[...] + bias_r
    cum_r = sum(_dot_nt(part, tri) for part in _split3(_log_sigmoid(pre_r)))

    rr = lax.broadcasted_iota(jnp.int32, (L, 1), 0)
    cc = lax.broadcasted_iota(jnp.int32, (1, L), 1)
    causal = cc <= rr

    for h in range(nh):
        cols = slice(h * dh, (h + 1) * dh)
        qb = q_ref[:, cols]
        kb = k_ref[:, cols]
        vh = v_ref[:, cols]
        qh = qb.astype(F32)
        kh = kb.astype(F32)
        b_c = cum_c[:, nh + h:nh + h + 1]
        li_c = pre_c[:, h:h + 1]
        b_r = cum_r[nh + h:nh + h + 1, :]
        li_r = pre_r[h:h + 1, :]
        m_prev = m_ref[h:h + 1, 0:1]

        dmat = jnp.where(causal, b_c - b_r + li_r, NEG)
        a = b_c + m_prev
        m_j = jnp.maximum(a, jnp.max(dmat, axis=1, keepdims=True))
        w_intra = jnp.exp(dmat - m_j)
        w_inter = jnp.exp(a - m_j)
        sc = _dot_nt(qb, kb) * w_intra
        c_old = c_ref[h]
        n_old = n_ref[h:h + 1, :]
        num = w_inter * _dot(qb, c_old.astype(BF16)) + _dot(sc.astype(BF16), vh)
        den = (w_inter * jnp.sum(qh * n_old, axis=1, keepdims=True)
               + jnp.sum(sc, axis=1, keepdims=True))
        hid = num / jnp.maximum(jnp.abs(den), jnp.exp(-m_j))

        g_tot = b_r[:, L - 1:L]
        lw_c = g_tot - b_c + li_c
        lw_r = g_tot - b_r + li_r
        m_new = jnp.maximum(g_tot + m_prev, jnp.max(lw_r, axis=1, keepdims=True))
        decay = jnp.exp(g_tot + m_prev - m_new)
        kw = jnp.exp(lw_c - m_new) * kh
        c_ref[h] = decay * c_old + _dot(kw.T.astype(BF16), vh)
        n_ref[h:h + 1, :] = decay * n_old + jnp.sum(kw, axis=0, keepdims=True)
        m_ref[h:h + 1, :] = jnp.broadcast_to(m_new, (1, LANE))

        hn = hid * lax.rsqrt(jnp.mean(hid * hid, axis=-1, keepdims=True) + EPS) * ng_ref[:, cols]
        y_ref[:, cols] = (_sigmoid(o_ref[:, cols]) * hn).astype(y_ref.dtype)


def _mlstm(seg_a, seg_b, if_rows, bias, norm_g, tri, batch, seq, chunk):
    n = batch * seq
    nc = seq // chunk
    w = MLSTM_WIDTH
    nh, dh = MLSTM_HEADS, MLSTM_HEAD_DIM

    def col_spec(off):
        return pl.BlockSpec((chunk, w), lambda b, c, o=off // w: (b * nc + c, o))

    return pl.pallas_call(
        _mlstm_kernel,
        grid=(batch, nc),
        in_specs=[
            pl.BlockSpec(memory_space=pltpu.SMEM),
            col_spec(A_QK), col_spec(A_QK + w), col_spec(A_V), col_spec(B_O),
            pl.BlockSpec((chunk, LANE), lambda b, c: (b * nc + c, B_IF // LANE)),
            pl.BlockSpec((8, chunk), lambda b, c: (0, b * nc + c)),
            pl.BlockSpec((1, w), lambda b, c: (0, 0)),
            pl.BlockSpec((chunk, chunk), lambda b, c: (0, 0)),
        ],
        out_specs=pl.BlockSpec((chunk, w), lambda b, c: (b * nc + c, 0)),
        out_shape=jax.ShapeDtypeStruct((n, w), BF16),
        scratch_shapes=[
            pltpu.VMEM((nh, dh, dh), F32),
            pltpu.VMEM((8, dh), F32),
            pltpu.VMEM((8, LANE), F32),
        ],
        compiler_params=pltpu.CompilerParams(
            dimension_semantics=("parallel", "arbitrary"), vmem_limit_bytes=VMEM_LIMIT),
        name="mlstm",
    )(bias, seg_a, seg_a, seg_a, seg_b, seg_b, if_rows, norm_g, tri)


def _outproj_kernel(x_ref, ya_ref, ym_ref, wa_ref, wm_ref, o_ref):
    o_ref[...] = x_ref[...] + _dot(ya_ref[...], wa_ref[...]) + _dot(ym_ref[...], wm_ref[...])


def _outproj(x2, ya, ym, wa, wm, tm):
    n, d = x2.shape
    return pl.pallas_call(
        _outproj_kernel,
        grid=(n // tm,),
        in_specs=[
            pl.BlockSpec((tm, d), lambda i: (i, 0)),
            pl.BlockSpec((tm, ya.shape[1]), lambda i: (i, 0)),
            pl.BlockSpec((tm, ym.shape[1]), lambda i: (i, 0)),
            pl.BlockSpec(wa.shape, lambda i: (0, 0)),
            pl.BlockSpec(wm.shape, lambda i: (0, 0)),
        ],
        out_specs=pl.BlockSpec((tm, d), lambda i: (i, 0)),
        out_shape=jax.ShapeDtypeStruct((n, d), F32),
        compiler_params=pltpu.CompilerParams(
            dimension_semantics=("parallel",), vmem_limit_bytes=VMEM_LIMIT),
        name="outproj",
    )(x2, ya, ym, wa, wm)


def _mlp_kernel(x_ref, g_ref, w1_ref, w2_ref, gf_ref, o_ref, h_ref):
    f = pl.program_id(1)

    @pl.when(f == 0)
    def _():
        x = x_ref[...]
        r = lax.rsqrt(jnp.mean(x * x, axis=-1, keepdims=True) + EPS)
        h_ref[...] = (x * r * g_ref[...]).astype(BF16)
        o_ref[...] = x

    u = jnp.maximum(_dot(h_ref[...], w1_ref[...]), 0.0)
    o_ref[...] += _dot((u * u).astype(BF16), w2_ref[...])

    @pl.when(f == pl.num_programs(1) - 1)
    def _():
        x2 = o_ref[...]
        r = lax.rsqrt(jnp.mean(x2 * x2, axis=-1, keepdims=True) + EPS)
        o_ref[...] = x2 * r * gf_ref[...]


def _mlp(x1, g, w1, w2, gf, tm, tf):
    n, d = x1.shape
    dff = w1.shape[1]
    return pl.pallas_call(
        _mlp_kernel,
        grid=(n // tm, dff // tf),
        in_specs=[
            pl.BlockSpec((tm, d), lambda i, f: (i, 0)),
            pl.BlockSpec((1, d), lambda i, f: (0, 0)),
            pl.BlockSpec((d, tf), lambda i, f: (0, f)),
            pl.BlockSpec((tf, d), lambda i, f: (f, 0)),
            pl.BlockSpec((1, d), lambda i, f: (0, 0)),
        ],
        out_specs=pl.BlockSpec((tm, d), lambda i, f: (i, 0)),
        out_shape=jax.ShapeDtypeStruct((n, d), F32),
        scratch_shapes=[pltpu.VMEM((tm, d), BF16)],
        compiler_params=pltpu.CompilerParams(
            dimension_semantics=("parallel", "arbitrary"), vmem_limit_bytes=VMEM_LIMIT),
        name="mlp",
    )(x1, g, w1, w2, gf)


def _row_tile(n, want):
    t = want
    while n % t:
        t //= 2
    return t


def _layer(x2, batch, seq, norm_mix_g, w_in, w_cmp_k1, w_cmp_k2, pos_cmp_k, w_cmp_v1, w_cmp_v2,
           pos_cmp_v, conv_w, conv_b, b_igate, b_fgate, mlstm_norm_g, w_out, norm_mlp_g,
           w_mlp_in, w_mlp_out):
    n, d = x2.shape
    assert seq % Q_BLOCK == 0 and seq >= WINDOW + Q_BLOCK and seq // SEL_BLOCK <= SEL_LANES
    g_ = NSA_KV_GROUPS
    nh = MLSTM_HEADS

    c_gate = NSA_WIDTH + 6 * NSA_KV_WIDTH
    c_qk = c_gate + NSA_HEADS * N_BRANCH
    c_v = c_qk + 2 * MLSTM_WIDTH
    c_o = c_v + MLSTM_WIDTH
    c_i = c_o + MLSTM_WIDTH
    c_f = c_i + nh
    w_a = jnp.concatenate([w_in[:, c_v:c_o], w_in[:, c_qk:c_v], w_in[:, :c_gate]], axis=1).astype(BF16)
    gate_cols = []
    per_g = NSA_REP * N_BRANCH
    for g in range(g_):
        gate_cols += [w_in[:, c_gate + g * per_g:c_gate + (g + 1) * per_g],
                      jnp.zeros((d, LANE - per_g), w_in.dtype)]
    w_b = jnp.concatenate(
        [w_in[:, c_o:c_i]] + gate_cols
        + [w_in[:, c_i:c_f + nh], jnp.zeros((d, SEG_B - B_IF - 2 * nh), w_in.dtype)],
        axis=1).astype(BF16)
    scale_a = jnp.concatenate([jnp.ones((1, A_QK + MLSTM_WIDTH), F32),
                               jnp.full((1, MLSTM_WIDTH), MLSTM_HEAD_DIM ** -0.5, F32),
                               jnp.full((1, NSA_WIDTH), NSA_HEAD_DIM ** -0.5, F32),
                               jnp.ones((1, SEG_A - A_KV), F32)], axis=1)
    scale_b = jnp.ones((1, SEG_B), F32)
    conv_pad = ((0, 0), (A_QK, SEG_A - A_Q))
    cw_a = jnp.pad(conv_w, conv_pad)
    cb_a = jnp.pad(conv_b.reshape(1, -1), conv_pad)
    g_mix = norm_mix_g.reshape(1, d)

    tm = _row_tile(seq, ROW_TILE)
    seg_a = _inproj(x2, g_mix, w_a, scale_a, cw_a, cb_a, BF16, tm, COL_TILE, (A_QK, A_Q), seq)
    seg_b = _inproj(x2, g_mix, w_b, scale_b, jnp.zeros((CONV_WIDTH, SEG_B), F32),
                    jnp.zeros((1, SEG_B), F32), F32, tm, COL_TILE, (0, 0), seq)

    n_sub = seq // CMP_STRIDE
    dh = NSA_HEAD_DIM
    w1s = jnp.stack([w_cmp_k1, w_cmp_v1]).reshape(2, 2, CMP_STRIDE, dh, dh)
    w1r = jnp.concatenate([w1s[:, 0], w1s[:, 1]], axis=-1).astype(BF16)
    w2s = jnp.stack([w_cmp_k2, w_cmp_v2]).astype(BF16)
    poss = jnp.stack([pos_cmp_k, pos_cmp_v]).reshape(2, 2, CMP_STRIDE, dh).transpose(0, 2, 1, 3)
    posr = jnp.pad(poss, ((0, 0), (0, 0), (0, 6), (0, 0))).astype(BF16)
    kvc, kvct = _compress(seg_a, w1r, w2s, posr, batch, seq)

    cmp_start = np.arange(n_sub) * CMP_STRIDE
    sel_start = np.arange(SEL_LANES) * SEL_BLOCK
    ovt = ((cmp_start[None, :] < sel_start[:, None] + SEL_BLOCK)
           & (cmp_start[None, :] + CMP_BLOCK - 1 >= sel_start[:, None])
           & (np.arange(n_sub)[None, :] < n_sub - CMP_BLOCK // CMP_STRIDE + 1))
    ovt = jnp.asarray(ovt, BF16)
    slopes = jnp.exp2(-8.0 * jnp.arange(1, NSA_HEADS + 1, dtype=F32) / NSA_HEADS)
    ocmp, pen, flags = _nsa_cmp(seg_a, kvc, kvct, ovt, slopes, batch, seq)
    onehot, wext, dbias, ubias, wbias = _nsa_tables(slopes, seq)
    y_a = _nsa_attn(seg_a, seg_b, pen, ocmp, flags, onehot, wext, dbias, ubias, wbias,
                    slopes, batch, seq)

    chunk = 256 if seq % 256 == 0 else 128
    if_rows = seg_b[:, B_IF:B_IF + 2 * nh].T
    bias = jnp.concatenate([b_igate, b_fgate]).astype(F32)
    tri = jnp.asarray(np.tril(np.ones((chunk, chunk), np.float32)), BF16)
    y_m = _mlstm(seg_a, seg_b, if_rows, bias, mlstm_norm_g.reshape(1, -1), tri, batch, seq, chunk)

    w_o = w_out.astype(BF16)
    x1 = _outproj(x2, y_a, y_m, w_o[:NSA_WIDTH], w_o[NSA_WIDTH:], _row_tile(n, 512))
    return x1, (norm_mlp_g.reshape(1, d), w_mlp_in.astype(BF16), w_mlp_out.astype(BF16))


def kernel(x, norm_mix_g, w_in, w_cmp_k1, w_cmp_k2, pos_cmp_k, w_cmp_v1, w_cmp_v2, pos_cmp_v, conv_w, conv_b, b_igate, b_fgate, mlstm_norm_g, w_out, norm_mlp_g, w_mlp_in, w_mlp_out, norm_f_g):
    batch, seq, d = x.shape
    depth = w_in.shape[0]
    assert depth == 1, "the final RMSNorm is fused into the last layer's channel mixer"
    x2 = x.reshape(batch * seq, d)
    tm = _row_tile(batch * seq, ROW_TILE)
    for l in range(depth):
        x1, (g_mlp, w1, w2) = _layer(
            x2, batch, seq, norm_mix_g[l], w_in[l], w_cmp_k1[l], w_cmp_k2[l], pos_cmp_k[l],
            w_cmp_v1[l], w_cmp_v2[l], pos_cmp_v[l], conv_w[l], conv_b[l], b_igate[l], b_fgate[l],
            mlstm_norm_g[l], w_out[l], norm_mlp_g[l], w_mlp_in[l], w_mlp_out[l])
        x2 = _mlp(x1, g_mlp, w1, w2, norm_f_g.reshape(1, d), tm, 512)
    return x2.reshape(batch, seq, d)
```

```python
import functools

import numpy as np
import jax
import jax.numpy as jnp
from jax import lax
from jax.experimental import pallas as pl
from jax.experimental.pallas import tpu as pltpu

F32 = jnp.float32
BF16 = jnp.bfloat16

EPS = 1e-6
NEG = -1e30
FORCE_BONUS = 1e4
PICKED = -3e38
MASK_BIG = 1e30

D_MODEL = 2048
NSA_HEAD_DIM = 128
NSA_WIDTH = D_MODEL // 2
NSA_HEADS = NSA_WIDTH // NSA_HEAD_DIM
NSA_REP = 4
NSA_KV_GROUPS = NSA_HEADS // NSA_REP
NSA_KV_WIDTH = NSA_KV_GROUPS * NSA_HEAD_DIM
CMP_BLOCK = 32
CMP_STRIDE = 16
SEL_BLOCK = 64
SEL_TOPK = 16
WINDOW = 512
Q_BLOCK = 128
N_BRANCH = 3
MLSTM_HEAD_DIM = 256
MLSTM_WIDTH = D_MODEL - NSA_WIDTH
MLSTM_HEADS = MLSTM_WIDTH // MLSTM_HEAD_DIM
CONV_WIDTH = 4
D_FF = 4 * D_MODEL

LANE = 128
SEL_LANES = 128
KEY_TILE = 128
SEL_TILES_PER_STEP = 2
WIN_TILES = WINDOW // KEY_TILE
VMEM_LIMIT = 56 * 1024 * 1024
ROW_TILE = 1024

A_V, A_QK = 0, MLSTM_WIDTH
A_Q = A_QK + 2 * MLSTM_WIDTH
A_KV = A_Q + NSA_WIDTH
SEG_A = A_KV + 6 * NSA_KV_WIDTH
B_O = 0
B_GATE = MLSTM_WIDTH
B_IF = B_GATE + NSA_KV_GROUPS * LANE
SEG_B = B_IF + 2 * LANE
COL_TILE = 512

WCOL_PAD, WCOL_HI, WCOL_LO, WCOL_QHI, WCOL_QLO = 0, 1, 2, 3, 4


def _dot(a, b):
    return jnp.dot(a, b, preferred_element_type=F32)


def _dot_nt(a, b):
    return lax.dot_general(a, b, (((1,), (1,)), ((), ())), preferred_element_type=F32)


def _sigmoid(x):
    return 1.0 / (1.0 + jnp.exp(-x))


def _shifted(x, tail, s):
    xs = pltpu.roll(x, s, axis=0)
    ts = pltpu.roll(tail, s, axis=0)
    row8 = lax.broadcasted_iota(jnp.int32, (8, 1), 0)
    head = jnp.where(row8 < s, ts, xs[:8])
    return jnp.concatenate([head, xs[8:]], axis=0)


def _conv_silu(x, tail, w, b):
    y = b + _shifted(x, tail, CONV_WIDTH - 1) * w[0:1]
    for i in range(1, CONV_WIDTH - 1):
        y = y + _shifted(x, tail, CONV_WIDTH - 1 - i) * w[i:i + 1]
    y = y + x * w[CONV_WIDTH - 1:CONV_WIDTH]
    return y * _sigmoid(y)


def _inproj_kernel(x_ref, g_ref, w_ref, cs_ref, cw_ref, cb_ref, oa_ref, ob_ref, h_ref, halo_ref, *,
                   na, conv_lo, conv_hi, tiles_per_seq):
    i = pl.program_id(0)
    j = pl.program_id(1)

    @pl.when(j == 0)
    def _():
        x = x_ref[...]
        r = lax.rsqrt(jnp.mean(x * x, axis=-1, keepdims=True) + EPS)
        h_ref[...] = (x * r * g_ref[...]).astype(BF16)

    is_conv = (j >= conv_lo) & (j < conv_hi)

    @pl.when((j < na) & jnp.logical_not(is_conv))
    def _():
        oa_ref[...] = (_dot(h_ref[...], w_ref[...]) * cs_ref[...]).astype(oa_ref.dtype)

    @pl.when(j >= na)
    def _():
        ob_ref[...] = (_dot(h_ref[...], w_ref[...]) * cs_ref[...]).astype(ob_ref.dtype)

    @pl.when(is_conv)
    def _():
        slot = j - conv_lo

        @pl.when(i % tiles_per_seq == 0)
        def _():
            halo_ref[slot] = jnp.zeros(halo_ref.shape[1:], F32)

        acc = _dot(h_ref[...], w_ref[...])
        tail = halo_ref[slot]
        halo_ref[slot] = acc[acc.shape[0] - 8:]
        y = _conv_silu(acc, tail, cw_ref[...], cb_ref[...])
        oa_ref[...] = (y * cs_ref[...]).astype(oa_ref.dtype)


def _inproj(x2, g, w, cs, cw, cb, tm, tn, conv_cols, seq):
    n, d = x2.shape
    na, nb = SEG_A // tn, SEG_B // tn
    conv_lo, conv_hi = conv_cols[0] // tn, conv_cols[1] // tn
    assert conv_cols[0] % tn == 0 and conv_cols[1] % tn == 0 and seq % tm == 0 and conv_hi <= na
    assert SEG_A % tn == 0 and SEG_B % tn == 0 and w.shape[1] == SEG_A + SEG_B
    return pl.pallas_call(
        functools.partial(_inproj_kernel, na=na, conv_lo=conv_lo, conv_hi=conv_hi,
                          tiles_per_seq=seq // tm),
        grid=(n // tm, na + nb),
        in_specs=[
            pl.BlockSpec((tm, d), lambda i, j: (i, 0)),
            pl.BlockSpec((1, d), lambda i, j: (0, 0)),
            pl.BlockSpec((d, tn), lambda i, j: (0, j)),
            pl.BlockSpec((1, tn), lambda i, j: (0, j)),
            pl.BlockSpec((CONV_WIDTH, tn), lambda i, j: (0, j)),
            pl.BlockSpec((1, tn), lambda i, j: (0, j)),
        ],
        out_specs=[pl.BlockSpec((tm, tn), lambda i, j: (i, jnp.minimum(j, na - 1))),
                   pl.BlockSpec((tm, tn), lambda i, j: (i, jnp.maximum(j - na, 0)))],
        out_shape=[jax.ShapeDtypeStruct((n, SEG_A), BF16), jax.ShapeDtypeStruct((n, SEG_B), F32)],
        scratch_shapes=[pltpu.VMEM((tm, d), BF16),
                        pltpu.VMEM((conv_hi - conv_lo, 8, tn), F32)],
        compiler_params=pltpu.CompilerParams(
            dimension_semantics=("arbitrary", "arbitrary"), vmem_limit_bytes=VMEM_LIMIT),
        name="inproj",
    )(x2, g, w, cs, cw, cb)


def _compress_kernel(x_ref, w1_ref, w2_ref, pos_ref, o_ref, ot_ref, xf_ref):
    n_sub = o_ref.shape[2]
    dh = NSA_HEAD_DIM
    xf_ref[...] = x_ref[...].astype(F32)
    acc = jnp.zeros((n_sub, 2 * dh), F32)
    posw = jnp.zeros((1, dh), F32)
    for p in range(CMP_STRIDE):
        wp = w1_ref[0, p]
        acc = acc + _dot(xf_ref[pl.ds(p, n_sub, stride=CMP_STRIDE), :].astype(BF16), wp)
        pw = _dot(pos_ref[0, p], wp)
        posw = posw + pw[0:1, :dh] + pw[1:2, dh:]
    bot = pltpu.roll(acc[:, dh:], n_sub - 1, axis=0)
    pre = acc[:, :dh] + bot + posw
    hid = pre * _sigmoid(pre)
    out = _dot(hid.astype(BF16), w2_ref[0])
    o_ref[0, 0] = out.astype(o_ref.dtype)
    ot_ref[0, 0] = out.T.astype(ot_ref.dtype)


def _compress(seg_a, w1r, w2s, posr, batch, seq):
    g = NSA_KV_GROUPS
    c = 2 * g
    dh = NSA_HEAD_DIM
    n_sub = seq // CMP_STRIDE
    return pl.pallas_call(
        _compress_kernel,
        grid=(batch, c),
        in_specs=[
            pl.BlockSpec((seq, dh), lambda i, j: (i, A_KV // dh + j)),
            pl.BlockSpec((1, CMP_STRIDE, dh, 2 * dh), lambda i, j: (j // g, 0, 0, 0)),
            pl.BlockSpec((1, dh, dh), lambda i, j: (j // g, 0, 0)),
            pl.BlockSpec((1, CMP_STRIDE, 8, dh), lambda i, j: (j // g, 0, 0, 0)),
        ],
        out_specs=[pl.BlockSpec((1, 1, n_sub, dh), lambda i, j: (i, j, 0, 0)),
                   pl.BlockSpec((1, 1, dh, n_sub), lambda i, j: (i, j, 0, 0))],
        out_shape=[jax.ShapeDtypeStruct((batch, c, n_sub, dh), BF16),
                   jax.ShapeDtypeStruct((batch, c, dh, n_sub), BF16)],
        scratch_shapes=[pltpu.VMEM((seq, dh), F32)],
        compiler_params=pltpu.CompilerParams(
            dimension_semantics=("parallel", "parallel"), vmem_limit_bytes=VMEM_LIMIT),
        name="compress",
    )(seg_a, w1r, w2s, posr)


def _stack_heads(q_all):
    dh = NSA_HEAD_DIM
    return jnp.concatenate([q_all[:, r * dh:(r + 1) * dh] for r in range(NSA_REP)], axis=0)


def _nsa_cmp_kernel(slopes_ref, q_ref, kc_ref, vct_ref, ovt_ref, ocmp_ref, pen_ref, flag_ref, *,
                    seq, sub_blocks):
    step = pl.program_id(2)
    nq_step = sub_blocks * Q_BLOCK
    chunk = SEL_LANES
    n_chunks = kc_ref.shape[2] // chunk
    need = ((step + 1) * nq_step - CMP_BLOCK) // CMP_STRIDE + 1
    n_need = (need + chunk - 1) // chunk
    for v in range(1, n_chunks + 1):
        cond = (n_need == v) if v < n_chunks else (n_need >= v)
        if v == 1:
            cond = n_need <= 1

        @pl.when(cond)
        def _(v=v):
            for sub in range(sub_blocks):
                rows = slice(sub * Q_BLOCK, (sub + 1) * Q_BLOCK)
                _nsa_cmp_block(slopes_ref, q_ref[rows, :], kc_ref, vct_ref, ovt_ref,
                               ocmp_ref.at[0, 0, sub], pen_ref.at[0, 0, rows], flag_ref.at[0, 0, sub],
                               step * sub_blocks + sub, seq, v * chunk)


def _nsa_cmp_block(slopes_ref, q_all, kc_ref, vct_ref, ovt_ref, ocmp_ref, pen_ref, flag_ref, qb, seq,
                   n_use):
    g = pl.program_id(1)
    nq = Q_BLOCK
    n_cpad = n_use
    n_cmp = seq // CMP_STRIDE - CMP_BLOCK // CMP_STRIDE + 1
    t0 = qb * nq

    s_t = _dot_nt(kc_ref[0, 0, :n_use, :], _stack_heads(q_all))
    n_s = lax.broadcasted_iota(jnp.int32, (n_cpad, 1), 0)
    q_l = lax.broadcasted_iota(jnp.int32, (1, nq), 1)
    dist = (t0 - (CMP_BLOCK - 1)) + q_l - n_s * CMP_STRIDE
    valid = (dist >= 0) & (n_s < n_cmp)
    dist_f = dist.astype(F32)
    probs = []
    p_sum = jnp.zeros((n_cpad, nq), F32)
    for r in range(NSA_REP):
        slope = slopes_ref[g * NSA_REP + r]
        s = jnp.where(valid, s_t[:, r * nq:(r + 1) * nq] - slope * dist_f, NEG)
        m = jnp.max(s, axis=0, keepdims=True)
        e = jnp.exp(s - m)
        inv = jnp.where(m > 0.5 * NEG, 1.0 / jnp.sum(e, axis=0, keepdims=True), 0.0)
        p = e * inv
        probs.append(p.astype(BF16))
        p_sum = p_sum + p
    ocmp_ref[...] = _dot(vct_ref[0, 0, :, :n_use], jnp.concatenate(probs, axis=1))

    p_hi = p_sum.astype(BF16)
    p_lo = (p_sum - p_hi.astype(F32)).astype(BF16)
    ovt = ovt_ref[:, :n_use]
    imp = _dot(ovt, p_hi) + _dot(ovt, p_lo)
    j_i = lax.broadcasted_iota(jnp.int32, (SEL_LANES, 1), 0)
    t_l = t0 + q_l
    cur = t_l // SEL_BLOCK
    forced = (j_i == 0) | (j_i == cur) | (j_i == cur - 1)
    causal_blk = j_i * SEL_BLOCK <= t_l
    val = jnp.where(causal_blk, jnp.where(forced, imp + FORCE_BONUS, imp), NEG)
    j_f = j_i.astype(F32)
    sel_t = jnp.zeros((SEL_LANES, nq), F32)
    for _ in range(min(SEL_TOPK, seq // SEL_BLOCK)):
        mx = jnp.max(val, axis=0, keepdims=True)
        first = jnp.min(jnp.where(val == mx, j_f, float(SEL_LANES)), axis=0, keepdims=True)
        pick = j_f == first
        sel_t = jnp.where(pick, 1.0, sel_t)
        val = jnp.where(pick, PICKED, val)
    sel = sel_t.T
    pen_ref[...] = ((sel - 1.0) * MASK_BIG).astype(pen_ref.dtype)
    flag_ref[...] = (jnp.max(sel, axis=0, keepdims=True) > 0.0).astype(jnp.int32)


def _nsa_cmp(seg_a, kvc, kvct, ovt, slopes, batch, seq):
    nqb = seq // Q_BLOCK
    gq = NSA_REP * NSA_HEAD_DIM
    g_ = NSA_KV_GROUPS
    n_cpad = kvc.shape[2]
    sub = _nsa_sub_blocks(seq)
    nstep = nqb // sub
    return pl.pallas_call(
        functools.partial(_nsa_cmp_kernel, seq=seq, sub_blocks=sub),
        grid=(batch, g_, nstep),
        in_specs=[
            pl.BlockSpec(memory_space=pltpu.SMEM),
            pl.BlockSpec((sub * Q_BLOCK, gq), lambda b, g, q: (b * nstep + q, A_Q // gq + g)),
            pl.BlockSpec((1, 1, n_cpad, NSA_HEAD_DIM), lambda b, g, q: (b, g, 0, 0)),
            pl.BlockSpec((1, 1, NSA_HEAD_DIM, n_cpad), lambda b, g, q: (b, g_ + g, 0, 0)),
            pl.BlockSpec((SEL_LANES, n_cpad), lambda b, g, q: (0, 0)),
        ],
        out_specs=[
            pl.BlockSpec((1, 1, sub, NSA_HEAD_DIM, gq), lambda b, g, q: (b, g, q, 0, 0)),
            pl.BlockSpec((1, 1, sub * Q_BLOCK, SEL_LANES), lambda b, g, q: (b, g, q, 0)),
            pl.BlockSpec((1, 1, sub, 1, SEL_LANES), lambda b, g, q: (b, g, q, 0, 0)),
        ],
        out_shape=[
            jax.ShapeDtypeStruct((batch, g_, nqb, NSA_HEAD_DIM, gq), F32),
            jax.ShapeDtypeStruct((batch, g_, seq, SEL_LANES), BF16),
            jax.ShapeDtypeStruct((batch, g_, nqb, 1, SEL_LANES), jnp.int32),
        ],
        compiler_params=pltpu.CompilerParams(
            dimension_semantics=("parallel", "parallel", "parallel"), vmem_limit_bytes=VMEM_LIMIT),
        name="nsa_cmp",
    )(slopes, seg_a, kvc, kvct, ovt)


def _nsa_attn_kernel(slopes_ref, flag_ref, q_ref, gate_ref, pen_ref, ocmp_ref, ks_ref, vs_ref,
                     kw_ref, vw_ref, onehot_ref, wext_ref, dbias_ref, ubias_ref, wbias_ref, o_ref,
                     ksel_ref, vselt_ref, kwin_ref, vwint_ref,
                     qa_ref, qw_ref, list_ref, m_ref, l_ref, acc_ref, owin_ref, sa_ref, sb_ref,
                     *, seq, sub_blocks):
    g = pl.program_id(1)
    step = pl.program_id(2)
    dh = NSA_HEAD_DIM
    nq = sub_blocks * Q_BLOCK
    kt = KEY_TILE
    per = SEL_TILES_PER_STEP
    pad_tile = seq // kt
    t0 = step * nq
    tile0 = step * sub_blocks

    @pl.when(step == 0)
    def _():
        ksel_ref[0:seq, :dh] = ks_ref[...]
        ksel_ref[0:seq, dh:] = onehot_ref[...]
        ksel_ref[seq:, :dh] = jnp.zeros((kt, dh), BF16)
        ksel_ref[seq:, dh:] = jnp.ones((kt, SEL_LANES), BF16)
        lane2 = lax.broadcasted_iota(jnp.int32, (WINDOW, dh + LANE), 1)
        kwin_ref[0:WINDOW, :] = jnp.where(lane2 == dh + WCOL_PAD, 1.0, 0.0).astype(BF16)
        kwin_ref[WINDOW:, :dh] = kw_ref[...]
        kwin_ref[WINDOW:, dh:] = wext_ref[...]
        zero_tile = jnp.zeros((dh, kt), BF16)
        vselt_ref[pad_tile] = zero_tile
        for i in range(WIN_TILES):
            vwint_ref[i] = zero_tile

        def transpose_tile(t, carry):
            r0 = pl.multiple_of(t * kt, kt)
            vselt_ref[t] = vs_ref[pl.ds(r0, kt), :].astype(F32).T.astype(BF16)
            vwint_ref[t + WIN_TILES] = vw_ref[pl.ds(r0, kt), :].astype(F32).T.astype(BF16)
            return carry

        lax.fori_loop(0, seq // kt, transpose_tile, 0)

    slopes = [slopes_ref[g * NSA_REP + r] for r in range(NSA_REP)]
    row_q = lax.broadcasted_iota(jnp.int32, (nq, 1), 0)
    lane = lax.broadcasted_iota(jnp.int32, (1, LANE), 1)
    q_all = q_ref[...]

    pen = pen_ref[0, 0].astype(F32)
    blk_rel = ((lane - (t0 + row_q) // SEL_BLOCK) * SEL_BLOCK).astype(F32)
    for r in range(NSA_REP):
        qa_ref[r * nq:(r + 1) * nq, :dh] = q_all[:, r * dh:(r + 1) * dh]
        qa_ref[r * nq:(r + 1) * nq, dh:] = (pen + slopes[r] * blk_rel).astype(BF16)

    def scan(k, cnt):
        for j in range(2):
            i = 2 * k + j
            hits = sum(flag_ref[0, 0, a, 0, 2 * i] + flag_ref[0, 0, a, 0, 2 * i + 1]
                       for a in range(sub_blocks))
            act = (hits > 0) & (i < tile0)
            list_ref[cnt] = i
            cnt = cnt + act.astype(jnp.int32)
        return cnt

    cnt = lax.fori_loop(0, (tile0 + 1) // 2, scan, 0)
    for i in range(4 * per):
        list_ref[cnt + i] = pad_tile

    def group_scores(it):
        keys = jnp.concatenate(
            [ksel_ref[pl.ds(pl.multiple_of(list_ref[it * per + i] * kt, kt), kt), :]
             for i in range(per)], axis=0)
        return _dot_nt(keys, qa_ref[...]) + ubias_ref[0]

    s = _dot_nt(ksel_ref[pl.ds(pl.multiple_of(t0, kt), nq), :], qa_ref[...]) + dbias_ref[0]
    m0 = jnp.max(s, axis=0, keepdims=True)
    p = jnp.exp(s - m0)
    m_ref[...] = m0
    l_ref[...] = jnp.sum(p, axis=0, keepdims=True)
    vals = jnp.concatenate([vselt_ref[tile0 + a] for a in range(sub_blocks)], axis=1)
    acc_ref[...] = _dot(vals, p.astype(BF16))
    sa_ref[...] = group_scores(0)

    tp = t0 + WINDOW + row_q
    t_hi = (tp // SEL_BLOCK).astype(F32)
    t_lo = (tp % SEL_BLOCK).astype(F32)
    for r in range(NSA_REP):
        sl = slopes[r]
        ext = jnp.where(lane == WCOL_PAD, -MASK_BIG, 0.0)
        ext = jnp.where(lane == WCOL_HI, sl * SEL_BLOCK, ext)
        ext = jnp.where(lane == WCOL_LO, sl, ext)
        ext = jnp.where(lane == WCOL_QHI, -sl * SEL_BLOCK * t_hi, ext)
        ext = jnp.where(lane == WCOL_QLO, -sl * t_lo, ext)
        qw_ref[r * nq:(r + 1) * nq, :dh] = q_all[:, r * dh:(r + 1) * dh]
        qw_ref[r * nq:(r + 1) * nq, dh:] = ext.astype(BF16)
    wlen = WINDOW + nq
    s = _dot_nt(kwin_ref[pl.ds(pl.multiple_of(t0, kt), wlen), :], qw_ref[...]) + wbias_ref[...]
    e = jnp.exp(s - jnp.max(s, axis=0, keepdims=True))
    vwin = jnp.concatenate([vwint_ref[tile0 + i] for i in range(WIN_TILES + sub_blocks)], axis=1)
    owin_ref[...] = _dot(vwin, e.astype(BF16)) / jnp.sum(e, axis=0, keepdims=True)

    def absorb(s, grp):
        vals = jnp.concatenate([vselt_ref[list_ref[grp * per + i]] for i in range(per)], axis=1)
        m_old = m_ref[...]
        m_new = jnp.maximum(m_old, jnp.max(s, axis=0, keepdims=True))
        alpha = jnp.exp(m_old - m_new)
        p = jnp.exp(s - m_new)
        l_ref[...] = alpha * l_ref[...] + jnp.sum(p, axis=0, keepdims=True)
        acc_ref[...] = alpha * acc_ref[...] + _dot(vals, p.astype(BF16))
        m_ref[...] = m_new

    def sel_body(it, carry):
        s = sa_ref[...]
        sb_ref[...] = group_scores(2 * it + 1)
        absorb(s, 2 * it)
        s = sb_ref[...]
        sa_ref[...] = group_scores(2 * it + 2)
        absorb(s, 2 * it + 1)
        return carry

    lax.fori_loop(0, (cnt + 2 * per - 1) // (2 * per), sel_body, 0)
    o_sel = acc_ref[...] / l_ref[...]
    o_win = owin_ref[...]

    gate_t = _sigmoid(gate_ref[...]).T
    for a in range(sub_blocks):
        o_cmp = ocmp_ref[0, 0, a]
        qs = slice(a * Q_BLOCK, (a + 1) * Q_BLOCK)
        for r in range(NSA_REP):
            cols = slice(r * nq + a * Q_BLOCK, r * nq + (a + 1) * Q_BLOCK)
            c0 = N_BRANCH * r
            out_t = (gate_t[c0:c0 + 1, qs] * o_cmp[:, r * Q_BLOCK:(r + 1) * Q_BLOCK]
                     + gate_t[c0 + 1:c0 + 2, qs] * o_sel[:, cols]
                     + gate_t[c0 + 2:c0 + 3, qs] * o_win[:, cols])
            o_ref[qs, r * dh:(r + 1) * dh] = out_t.T.astype(o_ref.dtype)


def _nsa_sub_blocks(seq):
    return 2 if (seq // Q_BLOCK) % 2 == 0 else 1


def _nsa_attn(seg_a, seg_b, pen, ocmp, flags, onehot, wext, dbias, ubias, wbias, slopes, batch, seq):
    n = batch * seq
    sub = _nsa_sub_blocks(seq)
    nq = sub * Q_BLOCK
    nqb = seq // nq
    gq = NSA_REP * NSA_HEAD_DIM
    g_ = NSA_KV_GROUPS
    dh, kt = NSA_HEAD_DIM, KEY_TILE
    dk = dh + SEL_LANES
    rq = NSA_REP * nq
    n_t = seq // kt

    def kv_spec(kind):
        return pl.BlockSpec((seq, dh), lambda b, g, q, k=kind: (b, A_KV // dh + k * g_ + g))

    def const_spec(arr):
        return pl.BlockSpec(arr.shape, lambda b, g, q, nd=arr.ndim: (0,) * nd)

    return pl.pallas_call(
        functools.partial(_nsa_attn_kernel, seq=seq, sub_blocks=sub),
        grid=(batch, g_, nqb),
        in_specs=[
            pl.BlockSpec(memory_space=pltpu.SMEM),
            pl.BlockSpec((1, 1, sub, 1, SEL_LANES), lambda b, g, q: (b, g, q, 0, 0),
                         memory_space=pltpu.SMEM),
            pl.BlockSpec((nq, gq), lambda b, g, q: (b * nqb + q, A_Q // gq + g)),
            pl.BlockSpec((nq, LANE), lambda b, g, q: (b * nqb + q, B_GATE // LANE + g)),
            pl.BlockSpec((1, 1, nq, SEL_LANES), lambda b, g, q: (b, g, q, 0)),
            pl.BlockSpec((1, 1, sub, NSA_HEAD_DIM, gq), lambda b, g, q: (b, g, q, 0, 0)),
            kv_spec(2), kv_spec(3), kv_spec(4), kv_spec(5),
            const_spec(onehot), const_spec(wext),
            pl.BlockSpec((1,) + dbias.shape[1:], lambda b, g, q: (g, 0, 0)),
            pl.BlockSpec((1,) + ubias.shape[1:], lambda b, g, q: (g, 0, 0)),
            const_spec(wbias),
        ],
        out_specs=pl.BlockSpec((nq, gq), lambda b, g, q: (b * nqb + q, g)),
        out_shape=jax.ShapeDtypeStruct((n, NSA_WIDTH), BF16),
        scratch_shapes=[
            pltpu.VMEM((seq + kt, dk), BF16),
            pltpu.VMEM((n_t + 1, dh, kt), BF16),
            pltpu.VMEM((seq + WINDOW, dh + LANE), BF16),
            pltpu.VMEM((n_t + WIN_TILES, dh, kt), BF16),
            pltpu.VMEM((rq, dk), BF16),
            pltpu.VMEM((rq, dk), BF16),
            pltpu.SMEM((n_t + 4 * SEL_TILES_PER_STEP,), jnp.int32),
            pltpu.VMEM((1, rq), F32),
            pltpu.VMEM((1, rq), F32),
            pltpu.VMEM((NSA_HEAD_DIM, rq), F32),
            pltpu.VMEM((NSA_HEAD_DIM, rq), F32),
            pltpu.VMEM((SEL_TILES_PER_STEP * KEY_TILE, rq), F32),
            pltpu.VMEM((SEL_TILES_PER_STEP * KEY_TILE, rq), F32),
        ],
        compiler_params=pltpu.CompilerParams(
            dimension_semantics=("parallel", "parallel", "arbitrary"), vmem_limit_bytes=VMEM_LIMIT),
        name="nsa_attn",
    )(slopes, flags, seg_a, seg_b, pen, ocmp, seg_a, seg_a, seg_a, seg_a, onehot, wext,
      dbias, ubias, wbias)


def _nsa_tables(slopes, seq):
    g_, kt = NSA_KV_GROUPS, KEY_TILE
    pos = np.arange(seq)
    onehot = jnp.asarray(pos[:, None] // SEL_BLOCK == np.arange(SEL_LANES)[None, :], BF16)
    ext = np.zeros((seq, LANE), np.float32)
    ext[:, WCOL_HI] = (pos + WINDOW) // SEL_BLOCK
    ext[:, WCOL_LO] = (pos + WINDOW) % SEL_BLOCK
    ext[:, WCOL_QHI] = 1.0
    ext[:, WCOL_QLO] = 1.0
    wext = jnp.asarray(ext, BF16)

    nq = _nsa_sub_blocks(seq) * Q_BLOCK

    def alibi_in_block(rows):
        u = jnp.asarray((np.arange(rows) % SEL_BLOCK).astype(np.float32))[None, :, None, None]
        t = jnp.broadcast_to(slopes.reshape(g_, 1, NSA_REP, 1) * u, (g_, rows, NSA_REP, nq))
        return t.reshape(g_, rows, NSA_REP * nq)

    ubias = alibi_in_block(SEL_TILES_PER_STEP * kt)
    kq = np.arange(nq)[:, None] <= np.arange(nq)[None, :]
    causal = np.tile(np.where(kq, 0.0, NEG).astype(np.float32), (1, NSA_REP))
    dbias = alibi_in_block(nq) + jnp.asarray(causal)[None]
    ki = np.arange(WINDOW + nq)[:, None]
    qi = np.arange(nq)[None, :]
    band = np.where((ki > qi) & (ki <= qi + WINDOW), 0.0, NEG).astype(np.float32)
    wbias = jnp.asarray(np.tile(band, (1, NSA_REP)))
    return onehot, wext, dbias, ubias, wbias


def _log_sigmoid(x):
    return jnp.minimum(x, 0.0) - jnp.log(1.0 + jnp.exp(-jnp.abs(x)))


def _split3(x):
    hi = x.astype(BF16)
    r1 = x - hi.astype(F32)
    mid = r1.astype(BF16)
    lo = (r1 - mid.astype(F32)).astype(BF16)
    return hi, mid, lo


def _mlstm_kernel(bias_ref, q_ref, k_ref, v_ref, o_ref, ifc_ref, ifr_ref, ng_ref,
                  tri_ref, y_ref, c_ref, n_ref, m_ref):
    ch = pl.program_id(1)
    nh, dh = MLSTM_HEADS, MLSTM_HEAD_DIM
    L = q_ref.shape[0]

    @pl.when(ch == 0)
    def _():
        c_ref[...] = jnp.zeros(c_ref.shape, F32)
        n_ref[...] = jnp.zeros(n_ref.shape, F32)
        m_ref[...] = jnp.zeros(m_ref.shape, F32)

    tri = tri_ref[...]
    lane8 = lax.broadcasted_iota(jnp.int32, (1, LANE), 1)
    bias_c = jnp.zeros((1, LANE), F32)
    for h in range(nh):
        bias_c = jnp.where(lane8 == h, bias_ref[h], bias_c)
        bias_c = jnp.where(lane8 == nh + h, bias_ref[nh + h], bias_c)
    pre_c = ifc_ref[...] + bias_c
    cum_c = sum(_dot(tri, part) for part in _split3(_log_sigmoid(pre_c)))
    row8 = lax.broadcasted_iota(jnp.int32, (8, 1), 0)
    bias_r = jnp.zeros((8, 1), F32)
    for h in range(2 * nh):
        bias_r = jnp.where(row8 == h, bias_ref[h], bias_r)
    pre_r = ifr_ref[...] + bias_r
    cum_r = sum(_dot_nt(part, tri) for part in _split3(_log_sigmoid(pre_r)))

    rr = lax.broadcasted_iota(jnp.int32, (L, 1), 0)
    cc = lax.broadcasted_iota(jnp.int32, (1, L), 1)
    causal = cc <= rr

    for h in range(nh):
        cols = slice(h * dh, (h + 1) * dh)
        qb = q_ref[:, cols]
        kb = k_ref[:, cols]
        vh = v_ref[:, cols]
        qh = qb.astype(F32)
        kh = kb.astype(F32)
        b_c = cum_c[:, nh + h:nh + h + 1]
        li_c = pre_c[:, h:h + 1]
        b_r = cum_r[nh + h:nh + h + 1, :]
        li_r = pre_r[h:h + 1, :]
        m_prev = m_ref[h:h + 1, 0:1]

        dmat = jnp.where(causal, b_c - b_r + li_r, NEG)
        a = b_c + m_prev
        m_j = jnp.maximum(a, jnp.max(dmat, axis=1, keepdims=True))
        w_intra = jnp.exp(dmat - m_j)
        w_inter = jnp.exp(a - m_j)
        sc = _dot_nt(qb, kb) * w_intra
        c_old = c_ref[h]
        n_old = n_ref[h:h + 1, :]
        num = w_inter * _dot(qb, c_old.astype(BF16)) + _dot(sc.astype(BF16), vh)
        den = (w_inter * jnp.sum(qh * n_old, axis=1, keepdims=True)
               + jnp.sum(sc, axis=1, keepdims=True))
        hid = num / jnp.maximum(jnp.abs(den), jnp.exp(-m_j))

        g_tot = b_r[:, L - 1:L]
        lw_c = g_tot - b_c + li_c
        lw_r = g_tot - b_r + li_r
        m_new = jnp.maximum(g_tot + m_prev, jnp.max(lw_r, axis=1, keepdims=True))
        decay = jnp.exp(g_tot + m_prev - m_new)
        kw = jnp.exp(lw_c - m_new) * kh
        c_ref[h] = decay * c_old + _dot(kw.T.astype(BF16), vh)
        n_ref[h:h + 1, :] = decay * n_old + jnp.sum(kw, axis=0, keepdims=True)
        m_ref[h:h + 1, :] = jnp.broadcast_to(m_new, (1, LANE))

        hn = hid * lax.rsqrt(jnp.mean(hid * hid, axis=-1, keepdims=True) + EPS) * ng_ref[:, cols]
        y_ref[:, cols] = (_sigmoid(o_ref[:, cols]) * hn).astype(y_ref.dtype)


def _mlstm(seg_a, seg_b, if_rows, bias, norm_g, tri, batch, seq, chunk):
    n = batch * seq
    nc = seq // chunk
    w = MLSTM_WIDTH
    nh, dh = MLSTM_HEADS, MLSTM_HEAD_DIM

    def col_spec(off):
        return pl.BlockSpec((chunk, w), lambda b, c, o=off // w: (b * nc + c, o))

    return pl.pallas_call(
        _mlstm_kernel,
        grid=(batch, nc),
        in_specs=[
            pl.BlockSpec(memory_space=pltpu.SMEM),
            col_spec(A_QK), col_spec(A_QK + w), col_spec(A_V), col_spec(B_O),
            pl.BlockSpec((chunk, LANE), lambda b, c: (b * nc + c, B_IF // LANE)),
            pl.BlockSpec((8, chunk), lambda b, c: (0, b * nc + c)),
            pl.BlockSpec((1, w), lambda b, c: (0, 0)),
            pl.BlockSpec((chunk, chunk), lambda b, c: (0, 0)),
        ],
        out_specs=pl.BlockSpec((chunk, w), lambda b, c: (b * nc + c, 0)),
        out_shape=jax.ShapeDtypeStruct((n, w), BF16),
        scratch_shapes=[
            pltpu.VMEM((nh, dh, dh), F32),
            pltpu.VMEM((8, dh), F32),
            pltpu.VMEM((8, LANE), F32),
        ],
        compiler_params=pltpu.CompilerParams(
            dimension_semantics=("parallel", "arbitrary"), vmem_limit_bytes=VMEM_LIMIT),
        name="mlstm",
    )(bias, seg_a, seg_a, seg_a, seg_b, seg_b, if_rows, norm_g, tri)


def _outproj_kernel(x_ref, ya_ref, ym_ref, wa_ref, wm_ref, o_ref):
    o_ref[...] = x_ref[...] + _dot(ya_ref[...], wa_ref[...]) + _dot(ym_ref[...], wm_ref[...])


def _outproj(x2, ya, ym, wa, wm, tm):
    n, d = x2.shape
    return pl.pallas_call(
        _outproj_kernel,
        grid=(n // tm,),
        in_specs=[
            pl.BlockSpec((tm, d), lambda i: (i, 0)),
            pl.BlockSpec((tm, ya.shape[1]), lambda i: (i, 0)),
            pl.BlockSpec((tm, ym.shape[1]), lambda i: (i, 0)),
            pl.BlockSpec(wa.shape, lambda i: (0, 0)),
            pl.BlockSpec(wm.shape, lambda i: (0, 0)),
        ],
        out_specs=pl.BlockSpec((tm, d), lambda i: (i, 0)),
        out_shape=jax.ShapeDtypeStruct((n, d), F32),
        compiler_params=pltpu.CompilerParams(
            dimension_semantics=("parallel",), vmem_limit_bytes=VMEM_LIMIT),
        name="outproj",
    )(x2, ya, ym, wa, wm)


def _mlp_kernel(x_ref, g_ref, w1_ref, w2_ref, gf_ref, o_ref, h_ref):
    f = pl.program_id(1)

    @pl.when(f == 0)
    def _():
        x = x_ref[...]
        r = lax.rsqrt(jnp.mean(x * x, axis=-1, keepdims=True) + EPS)
        h_ref[...] = (x * r * g_ref[...]).astype(BF16)
        o_ref[...] = x

    u = jnp.maximum(_dot(h_ref[...], w1_ref[...]), 0.0)
    o_ref[...] += _dot((u * u).astype(BF16), w2_ref[...])

    @pl.when(f == pl.num_programs(1) - 1)
    def _():
        x2 = o_ref[...]
        r = lax.rsqrt(jnp.mean(x2 * x2, axis=-1, keepdims=True) + EPS)
        o_ref[...] = x2 * r * gf_ref[...]


def _mlp(x1, g, w1, w2, gf, tm, tf):
    n, d = x1.shape
    dff = w1.shape[1]
    return pl.pallas_call(
        _mlp_kernel,
        grid=(n // tm, dff // tf),
        in_specs=[
            pl.BlockSpec((tm, d), lambda i, f: (i, 0)),
            pl.BlockSpec((1, d), lambda i, f: (0, 0)),
            pl.BlockSpec((d, tf), lambda i, f: (0, f)),
            pl.BlockSpec((tf, d), lambda i, f: (f, 0)),
            pl.BlockSpec((1, d), lambda i, f: (0, 0)),
        ],
        out_specs=pl.BlockSpec((tm, d), lambda i, f: (i, 0)),
        out_shape=jax.ShapeDtypeStruct((n, d), F32),
        scratch_shapes=[pltpu.VMEM((tm, d), BF16)],
        compiler_params=pltpu.CompilerParams(
            dimension_semantics=("parallel", "arbitrary"), vmem_limit_bytes=VMEM_LIMIT),
        name="mlp",
    )(x1, g, w1, w2, gf)


def _row_tile(n, want):
    t = want
    while n % t:
        t //= 2
    return t


def _layer(x2, batch, seq, norm_mix_g, w_in, w_cmp_k1, w_cmp_k2, pos_cmp_k, w_cmp_v1, w_cmp_v2,
           pos_cmp_v, conv_w, conv_b, b_igate, b_fgate, mlstm_norm_g, w_out, norm_mlp_g,
           w_mlp_in, w_mlp_out):
    n, d = x2.shape
    assert seq % Q_BLOCK == 0 and seq >= WINDOW + Q_BLOCK and seq // SEL_BLOCK <= SEL_LANES
    g_ = NSA_KV_GROUPS
    nh = MLSTM_HEADS

    c_gate = NSA_WIDTH + 6 * NSA_KV_WIDTH
    c_qk = c_gate + NSA_HEADS * N_BRANCH
    c_v = c_qk + 2 * MLSTM_WIDTH
    c_o = c_v + MLSTM_WIDTH
    c_i = c_o + MLSTM_WIDTH
    c_f = c_i + nh
    gate_cols = []
    per_g = NSA_REP * N_BRANCH
    for g in range(g_):
        gate_cols += [w_in[:, c_gate + g * per_g:c_gate + (g + 1) * per_g],
                      jnp.zeros((d, LANE - per_g), w_in.dtype)]
    w_ab = jnp.concatenate(
        [w_in[:, c_v:c_o], w_in[:, c_qk:c_v], w_in[:, :c_gate],
         w_in[:, c_o:c_i]] + gate_cols
        + [w_in[:, c_i:c_f + nh], jnp.zeros((d, SEG_B - B_IF - 2 * nh), w_in.dtype)],
        axis=1).astype(BF16)
    scale = jnp.concatenate([jnp.ones((1, A_QK + MLSTM_WIDTH), F32),
                             jnp.full((1, MLSTM_WIDTH), MLSTM_HEAD_DIM ** -0.5, F32),
                             jnp.full((1, NSA_WIDTH), NSA_HEAD_DIM ** -0.5, F32),
                             jnp.ones((1, SEG_A - A_KV + SEG_B), F32)], axis=1)
    conv_pad = ((0, 0), (A_QK, SEG_A - A_Q + SEG_B))
    cw_ab = jnp.pad(conv_w, conv_pad)
    cb_ab = jnp.pad(conv_b.reshape(1, -1), conv_pad)
    g_mix = norm_mix_g.reshape(1, d)

    tm = _row_tile(seq, ROW_TILE)
    seg_a, seg_b = _inproj(x2, g_mix, w_ab, scale, cw_ab, cb_ab, tm, COL_TILE, (A_QK, A_Q), seq)

    n_sub = seq // CMP_STRIDE
    dh = NSA_HEAD_DIM
    w1s = jnp.stack([w_cmp_k1, w_cmp_v1]).reshape(2, 2, CMP_STRIDE, dh, dh)
    w1r = jnp.concatenate([w1s[:, 0], w1s[:, 1]], axis=-1).astype(BF16)
    w2s = jnp.stack([w_cmp_k2, w_cmp_v2]).astype(BF16)
    poss = jnp.stack([pos_cmp_k, pos_cmp_v]).reshape(2, 2, CMP_STRIDE, dh).transpose(0, 2, 1, 3)
    posr = jnp.pad(poss, ((0, 0), (0, 0), (0, 6), (0, 0))).astype(BF16)
    kvc, kvct = _compress(seg_a, w1r, w2s, posr, batch, seq)

    cmp_start = np.arange(n_sub) * CMP_STRIDE
    sel_start = np.arange(SEL_LANES) * SEL_BLOCK
    ovt = ((cmp_start[None, :] < sel_start[:, None] + SEL_BLOCK)
           & (cmp_start[None, :] + CMP_BLOCK - 1 >= sel_start[:, None])
           & (np.arange(n_sub)[None, :] < n_sub - CMP_BLOCK // CMP_STRIDE + 1))
    ovt = jnp.asarray(ovt, BF16)
    slopes = jnp.exp2(-8.0 * jnp.arange(1, NSA_HEADS + 1, dtype=F32) / NSA_HEADS)
    ocmp, pen, flags = _nsa_cmp(seg_a, kvc, kvct, ovt, slopes, batch, seq)
    onehot, wext, dbias, ubias, wbias = _nsa_tables(slopes, seq)
    y_a = _nsa_attn(seg_a, seg_b, pen, ocmp, flags, onehot, wext, dbias, ubias, wbias,
                    slopes, batch, seq)

    chunk = 256 if seq % 256 == 0 else 128
    if_rows = seg_b[:, B_IF:B_IF + 2 * nh].T
    bias = jnp.concatenate([b_igate, b_fgate]).astype(F32)
    tri = jnp.asarray(np.tril(np.ones((chunk, chunk), np.float32)), BF16)
    y_m = _mlstm(seg_a, seg_b, if_rows, bias, mlstm_norm_g.reshape(1, -1), tri, batch, seq, chunk)

    w_o = w_out.astype(BF16)
    x1 = _outproj(x2, y_a, y_m, w_o[:NSA_WIDTH], w_o[NSA_WIDTH:], _row_tile(n, 512))
    return x1, (norm_mlp_g.reshape(1, d), w_mlp_in.astype(BF16), w_mlp_out.astype(BF16))


def kernel(x, norm_mix_g, w_in, w_cmp_k1, w_cmp_k2, pos_cmp_k, w_cmp_v1, w_cmp_v2, pos_cmp_v, conv_w, conv_b, b_igate, b_fgate, mlstm_norm_g, w_out, norm_mlp_g, w_mlp_in, w_mlp_out, norm_f_g):
    batch, seq, d = x.shape
    depth = w_in.shape[0]
    assert depth == 1, "the final RMSNorm is fused into the last layer's channel mixer"
    x2 = x.reshape(batch * seq, d)
    tm = _row_tile(batch * seq, ROW_TILE)
    for l in range(depth):
        x1, (g_mlp, w1, w2) = _layer(
            x2, batch, seq, norm_mix_g[l], w_in[l], w_cmp_k1[l], w_cmp_k2[l], pos_cmp_k[l],
            w_cmp_v1[l], w_cmp_v2[l], pos_cmp_v[l], conv_w[l], conv_b[l], b_igate[l], b_fgate[l],
            mlstm_norm_g[l], w_out[l], norm_mlp_g[l], w_mlp_in[l], w_mlp_out[l])
        x2 = _mlp(x1, g_mlp, w1, w2, norm_f_g.reshape(1, d), tm, 512)
    return x2.reshape(batch, seq, d)
```

```python
import functools

import numpy as np
import jax
import jax.numpy as jnp
from jax import lax
from jax.experimental import pallas as pl
from jax.experimental.pallas import tpu as pltpu

F32 = jnp.float32
BF16 = jnp.bfloat16

EPS = 1e-6
NEG = -1e30
FORCE_BONUS = 1e4
PICKED = -3e38
MASK_BIG = 1e30

D_MODEL = 2048
NSA_HEAD_DIM = 128
NSA_WIDTH = D_MODEL // 2
NSA_HEADS = NSA_WIDTH // NSA_HEAD_DIM
NSA_REP = 4
NSA_KV_GROUPS = NSA_HEADS // NSA_REP
NSA_KV_WIDTH = NSA_KV_GROUPS * NSA_HEAD_DIM
CMP_BLOCK = 32
CMP_STRIDE = 16
SEL_BLOCK = 64
SEL_TOPK = 16
WINDOW = 512
Q_BLOCK = 128
N_BRANCH = 3
MLSTM_HEAD_DIM = 256
MLSTM_WIDTH = D_MODEL - NSA_WIDTH
MLSTM_HEADS = MLSTM_WIDTH // MLSTM_HEAD_DIM
CONV_WIDTH = 4
D_FF = 4 * D_MODEL

LANE = 128
SEL_LANES = 128
KEY_TILE = 128
SEL_TILES_PER_STEP = 2
WIN_TILES = WINDOW // KEY_TILE
SUM_ROWS = 16
VMEM_LIMIT = 56 * 1024 * 1024
ROW_TILE = 1024

A_V, A_QK = 0, MLSTM_WIDTH
A_Q = A_QK + 2 * MLSTM_WIDTH
A_KV = A_Q + NSA_WIDTH
SEG_A = A_KV + 6 * NSA_KV_WIDTH
B_O = 0
B_GATE = MLSTM_WIDTH
B_IF = B_GATE + NSA_KV_GROUPS * LANE
SEG_B = B_IF + 2 * LANE
COL_TILE = 512

WCOL_PAD, WCOL_HI, WCOL_LO, WCOL_QHI, WCOL_QLO = 0, 1, 2, 3, 4


def _dot(a, b):
    return jnp.dot(a, b, preferred_element_type=F32)


def _dot_nt(a, b):
    return lax.dot_general(a, b, (((1,), (1,)), ((), ())), preferred_element_type=F32)


def _sigmoid(x):
    return 1.0 / (1.0 + jnp.exp(-x))


def _shifted(x, tail, s):
    xs = pltpu.roll(x, s, axis=0)
    ts = pltpu.roll(tail, s, axis=0)
    row8 = lax.broadcasted_iota(jnp.int32, (8, 1), 0)
    head = jnp.where(row8 < s, ts, xs[:8])
    return jnp.concatenate([head, xs[8:]], axis=0)


def _conv_silu(x, tail, w, b):
    y = b + _shifted(x, tail, CONV_WIDTH - 1) * w[0:1]
    for i in range(1, CONV_WIDTH - 1):
        y = y + _shifted(x, tail, CONV_WIDTH - 1 - i) * w[i:i + 1]
    y = y + x * w[CONV_WIDTH - 1:CONV_WIDTH]
    return y * _sigmoid(y)


def _inproj_kernel(x_ref, g_ref, w_ref, cs_ref, cw_ref, cb_ref, oa_ref, ob_ref, h_ref, halo_ref, *,
                   na, conv_lo, conv_hi, tiles_per_seq):
    i = pl.program_id(0)
    j = pl.program_id(1)

    @pl.when(j == 0)
    def _():
        x = x_ref[...]
        r = lax.rsqrt(jnp.mean(x * x, axis=-1, keepdims=True) + EPS)
        h_ref[...] = (x * r * g_ref[...]).astype(BF16)

    is_conv = (j >= conv_lo) & (j < conv_hi)

    @pl.when((j < na) & jnp.logical_not(is_conv))
    def _():
        oa_ref[...] = (_dot(h_ref[...], w_ref[...]) * cs_ref[...]).astype(oa_ref.dtype)

    @pl.when(j >= na)
    def _():
        ob_ref[...] = (_dot(h_ref[...], w_ref[...]) * cs_ref[...]).astype(ob_ref.dtype)

    @pl.when(is_conv)
    def _():
        slot = j - conv_lo

        @pl.when(i % tiles_per_seq == 0)
        def _():
            halo_ref[slot] = jnp.zeros(halo_ref.shape[1:], F32)

        acc = _dot(h_ref[...], w_ref[...])
        tail = halo_ref[slot]
        halo_ref[slot] = acc[acc.shape[0] - 8:]
        y = _conv_silu(acc, tail, cw_ref[...], cb_ref[...])
        oa_ref[...] = (y * cs_ref[...]).astype(oa_ref.dtype)


def _inproj(x2, g, w, cs, cw, cb, tm, tn, conv_cols, seq):
    n, d = x2.shape
    na, nb = SEG_A // tn, SEG_B // tn
    conv_lo, conv_hi = conv_cols[0] // tn, conv_cols[1] // tn
    assert conv_cols[0] % tn == 0 and conv_cols[1] % tn == 0 and seq % tm == 0 and conv_hi <= na
    assert SEG_A % tn == 0 and SEG_B % tn == 0 and w.shape[1] == SEG_A + SEG_B
    return pl.pallas_call(
        functools.partial(_inproj_kernel, na=na, conv_lo=conv_lo, conv_hi=conv_hi,
                          tiles_per_seq=seq // tm),
        grid=(n // tm, na + nb),
        in_specs=[
            pl.BlockSpec((tm, d), lambda i, j: (i, 0)),
            pl.BlockSpec((1, d), lambda i, j: (0, 0)),
            pl.BlockSpec((d, tn), lambda i, j: (0, j)),
            pl.BlockSpec((1, tn), lambda i, j: (0, j)),
            pl.BlockSpec((CONV_WIDTH, tn), lambda i, j: (0, j)),
            pl.BlockSpec((1, tn), lambda i, j: (0, j)),
        ],
        out_specs=[pl.BlockSpec((tm, tn), lambda i, j: (i, jnp.minimum(j, na - 1))),
                   pl.BlockSpec((tm, tn), lambda i, j: (i, jnp.maximum(j - na, 0)))],
        out_shape=[jax.ShapeDtypeStruct((n, SEG_A), BF16), jax.ShapeDtypeStruct((n, SEG_B), F32)],
        scratch_shapes=[pltpu.VMEM((tm, d), BF16),
                        pltpu.VMEM((conv_hi - conv_lo, 8, tn), F32)],
        compiler_params=pltpu.CompilerParams(
            dimension_semantics=("arbitrary", "arbitrary"), vmem_limit_bytes=VMEM_LIMIT),
        name="inproj",
    )(x2, g, w, cs, cw, cb)


def _compress_kernel(x_ref, w1_ref, w2_ref, pos_ref, o_ref, ot_ref, xf_ref):
    n_sub = o_ref.shape[2]
    dh = NSA_HEAD_DIM
    xf_ref[...] = x_ref[...].astype(F32)
    acc = jnp.zeros((n_sub, 2 * dh), F32)
    posw = jnp.zeros((1, dh), F32)
    for p in range(CMP_STRIDE):
        wp = w1_ref[0, p]
        acc = acc + _dot(xf_ref[pl.ds(p, n_sub, stride=CMP_STRIDE), :].astype(BF16), wp)
        pw = _dot(pos_ref[0, p], wp)
        posw = posw + pw[0:1, :dh] + pw[1:2, dh:]
    bot = pltpu.roll(acc[:, dh:], n_sub - 1, axis=0)
    pre = acc[:, :dh] + bot + posw
    hid = pre * _sigmoid(pre)
    out = _dot(hid.astype(BF16), w2_ref[0])
    o_ref[0, 0] = out.astype(o_ref.dtype)
    ot_ref[0, 0] = out.T.astype(ot_ref.dtype)


def _compress(seg_a, w1r, w2s, posr, batch, seq):
    g = NSA_KV_GROUPS
    c = 2 * g
    dh = NSA_HEAD_DIM
    n_sub = seq // CMP_STRIDE
    return pl.pallas_call(
        _compress_kernel,
        grid=(batch, c),
        in_specs=[
            pl.BlockSpec((seq, dh), lambda i, j: (i, A_KV // dh + j)),
            pl.BlockSpec((1, CMP_STRIDE, dh, 2 * dh), lambda i, j: (j // g, 0, 0, 0)),
            pl.BlockSpec((1, dh, dh), lambda i, j: (j // g, 0, 0)),
            pl.BlockSpec((1, CMP_STRIDE, 8, dh), lambda i, j: (j // g, 0, 0, 0)),
        ],
        out_specs=[pl.BlockSpec((1, 1, n_sub, dh), lambda i, j: (i, j, 0, 0)),
                   pl.BlockSpec((1, 1, dh, n_sub), lambda i, j: (i, j, 0, 0))],
        out_shape=[jax.ShapeDtypeStruct((batch, c, n_sub, dh), BF16),
                   jax.ShapeDtypeStruct((batch, c, dh, n_sub), BF16)],
        scratch_shapes=[pltpu.VMEM((seq, dh), F32)],
        compiler_params=pltpu.CompilerParams(
            dimension_semantics=("parallel", "parallel"), vmem_limit_bytes=VMEM_LIMIT),
        name="compress",
    )(seg_a, w1r, w2s, posr)


def _stack_heads(q_all):
    dh = NSA_HEAD_DIM
    return jnp.concatenate([q_all[:, r * dh:(r + 1) * dh] for r in range(NSA_REP)], axis=0)


def _nsa_cmp_kernel(slopes_ref, q_ref, kc_ref, vct_ref, ovt_ref, ocmp_ref, pen_ref, flag_ref, *,
                    seq, sub_blocks):
    step = pl.program_id(2)
    nq_step = sub_blocks * Q_BLOCK
    chunk = min(SEL_LANES, kc_ref.shape[2])
    n_chunks = kc_ref.shape[2] // chunk
    need = ((step + 1) * nq_step - CMP_BLOCK) // CMP_STRIDE + 1
    n_need = (need + chunk - 1) // chunk
    for v in range(1, n_chunks + 1):
        cond = (n_need == v) if v < n_chunks else (n_need >= v)
        if v == 1:
            cond = n_need <= 1

        @pl.when(cond)
        def _(v=v):
            for sub in range(sub_blocks):
                rows = slice(sub * Q_BLOCK, (sub + 1) * Q_BLOCK)
                _nsa_cmp_block(slopes_ref, q_ref[rows, :], kc_ref, vct_ref, ovt_ref,
                               ocmp_ref.at[0, 0, sub], pen_ref.at[0, 0, rows], flag_ref.at[0, 0, sub],
                               step * sub_blocks + sub, seq, v * chunk)


def _nsa_cmp_block(slopes_ref, q_all, kc_ref, vct_ref, ovt_ref, ocmp_ref, pen_ref, flag_ref, qb, seq,
                   n_use):
    g = pl.program_id(1)
    nq = Q_BLOCK
    n_cpad = n_use
    n_cmp = seq // CMP_STRIDE - CMP_BLOCK // CMP_STRIDE + 1
    t0 = qb * nq

    s_t = _dot_nt(kc_ref[0, 0, :n_use, :], _stack_heads(q_all))
    n_s = lax.broadcasted_iota(jnp.int32, (n_cpad, 1), 0)
    q_l = lax.broadcasted_iota(jnp.int32, (1, nq), 1)
    dist = (t0 - (CMP_BLOCK - 1)) + q_l - n_s * CMP_STRIDE
    valid = (dist >= 0) & (n_s < n_cmp)
    dist_f = dist.astype(F32)
    probs = []
    p_sum = jnp.zeros((n_cpad, nq), F32)
    for r in range(NSA_REP):
        slope = slopes_ref[g * NSA_REP + r]
        s = jnp.where(valid, s_t[:, r * nq:(r + 1) * nq] - slope * dist_f, NEG)
        m = jnp.max(s, axis=0, keepdims=True)
        e = jnp.exp(s - m)
        inv = jnp.where(m > 0.5 * NEG, 1.0 / jnp.sum(e, axis=0, keepdims=True), 0.0)
        p = e * inv
        probs.append(p.astype(BF16))
        p_sum = p_sum + p
    ocmp_ref[...] = _dot(vct_ref[0, 0, :, :n_use], jnp.concatenate(probs, axis=1))

    p_hi = p_sum.astype(BF16)
    p_lo = (p_sum - p_hi.astype(F32)).astype(BF16)
    ovt = ovt_ref[:, :n_use]
    imp = _dot(ovt, p_hi) + _dot(ovt, p_lo)
    j_i = lax.broadcasted_iota(jnp.int32, (SEL_LANES, 1), 0)
    t_l = t0 + q_l
    cur = t_l // SEL_BLOCK
    forced = (j_i == 0) | (j_i == cur) | (j_i == cur - 1)
    causal_blk = j_i * SEL_BLOCK <= t_l
    val = jnp.where(causal_blk, jnp.where(forced, imp + FORCE_BONUS, imp), NEG)
    j_f = j_i.astype(F32)
    sel_t = jnp.zeros((SEL_LANES, nq), F32)
    for _ in range(min(SEL_TOPK, seq // SEL_BLOCK)):
        mx = jnp.max(val, axis=0, keepdims=True)
        first = jnp.min(jnp.where(val == mx, j_f, float(SEL_LANES)), axis=0, keepdims=True)
        pick = j_f == first
        sel_t = jnp.where(pick, 1.0, sel_t)
        val = jnp.where(pick, PICKED, val)
    sel = sel_t.T
    pen_ref[...] = ((sel - 1.0) * MASK_BIG).astype(pen_ref.dtype)
    flag_ref[...] = (jnp.max(sel, axis=0, keepdims=True) > 0.0).astype(jnp.int32)


def _nsa_cmp(seg_a, kvc, kvct, ovt, slopes, batch, seq):
    nqb = seq // Q_BLOCK
    gq = NSA_REP * NSA_HEAD_DIM
    g_ = NSA_KV_GROUPS
    n_cpad = kvc.shape[2]
    sub = _nsa_sub_blocks(seq)
    nstep = nqb // sub
    return pl.pallas_call(
        functools.partial(_nsa_cmp_kernel, seq=seq, sub_blocks=sub),
        grid=(batch, g_, nstep),
        in_specs=[
            pl.BlockSpec(memory_space=pltpu.SMEM),
            pl.BlockSpec((sub * Q_BLOCK, gq), lambda b, g, q: (b * nstep + q, A_Q // gq + g)),
            pl.BlockSpec((1, 1, n_cpad, NSA_HEAD_DIM), lambda b, g, q: (b, g, 0, 0)),
            pl.BlockSpec((1, 1, NSA_HEAD_DIM, n_cpad), lambda b, g, q: (b, g_ + g, 0, 0)),
            pl.BlockSpec((SEL_LANES, n_cpad), lambda b, g, q: (0, 0)),
        ],
        out_specs=[
            pl.BlockSpec((1, 1, sub, NSA_HEAD_DIM, gq), lambda b, g, q: (b, g, q, 0, 0)),
            pl.BlockSpec((1, 1, sub * Q_BLOCK, SEL_LANES), lambda b, g, q: (b, g, q, 0)),
            pl.BlockSpec((1, 1, sub, 1, SEL_LANES), lambda b, g, q: (b, g, q, 0, 0)),
        ],
        out_shape=[
            jax.ShapeDtypeStruct((batch, g_, nqb, NSA_HEAD_DIM, gq), F32),
            jax.ShapeDtypeStruct((batch, g_, seq, SEL_LANES), BF16),
            jax.ShapeDtypeStruct((batch, g_, nqb, 1, SEL_LANES), jnp.int32),
        ],
        compiler_params=pltpu.CompilerParams(
            dimension_semantics=("parallel", "parallel", "parallel"), vmem_limit_bytes=VMEM_LIMIT),
        name="nsa_cmp",
    )(slopes, seg_a, kvc, kvct, ovt)


def _nsa_attn_kernel(slopes_ref, flag_ref, q_ref, gate_ref, pen_ref, ocmp_ref, ks_ref, vs_ref,
                     kw_ref, vw_ref, onehot_ref, wext_ref, dbias_ref, ubias_ref, wbias_ref, o_ref,
                     ksel_ref, vselt_ref, kwin_ref, vwint_ref,
                     qa_ref, qw_ref, list_ref, m_ref, acc_ref, owin_ref, sa_ref, sb_ref,
                     *, seq, sub_blocks):
    g = pl.program_id(1)
    step = pl.program_id(2)
    dh = NSA_HEAD_DIM
    nq = sub_blocks * Q_BLOCK
    kt = KEY_TILE
    per = SEL_TILES_PER_STEP
    pad_tile = seq // kt
    t0 = step * nq
    tile0 = step * sub_blocks

    @pl.when(step == 0)
    def _():
        ksel_ref[0:seq, :dh] = ks_ref[...]
        ksel_ref[0:seq, dh:] = onehot_ref[...]
        ksel_ref[seq:, :dh] = jnp.zeros((kt, dh), BF16)
        ksel_ref[seq:, dh:] = jnp.ones((kt, SEL_LANES), BF16)
        lane2 = lax.broadcasted_iota(jnp.int32, (WINDOW, dh + LANE), 1)
        kwin_ref[0:WINDOW, :] = jnp.where(lane2 == dh + WCOL_PAD, 1.0, 0.0).astype(BF16)
        kwin_ref[WINDOW:, :dh] = kw_ref[...]
        kwin_ref[WINDOW:, dh:] = wext_ref[...]
        zero_tile = jnp.zeros((dh + SUM_ROWS, kt), BF16)
        ones_rows = jnp.ones((SUM_ROWS, kt), BF16)
        vselt_ref[pad_tile] = zero_tile
        for i in range(WIN_TILES):
            vwint_ref[i] = zero_tile

        def transpose_tile(t, carry):
            r0 = pl.multiple_of(t * kt, kt)
            vselt_ref[t, :dh] = vs_ref[pl.ds(r0, kt), :].astype(F32).T.astype(BF16)
            vselt_ref[t, dh:] = ones_rows
            vwint_ref[t + WIN_TILES, :dh] = vw_ref[pl.ds(r0, kt), :].astype(F32).T.astype(BF16)
            vwint_ref[t + WIN_TILES, dh:] = ones_rows
            return carry

        lax.fori_loop(0, seq // kt, transpose_tile, 0)

    slopes = [slopes_ref[g * NSA_REP + r] for r in range(NSA_REP)]
    row_q = lax.broadcasted_iota(jnp.int32, (nq, 1), 0)
    lane = lax.broadcasted_iota(jnp.int32, (1, LANE), 1)
    q_all = q_ref[...]

    pen = pen_ref[0, 0].astype(F32)
    blk_rel = ((lane - (t0 + row_q) // SEL_BLOCK) * SEL_BLOCK).astype(F32)
    for r in range(NSA_REP):
        qa_ref[r * nq:(r + 1) * nq, :dh] = q_all[:, r * dh:(r + 1) * dh]
        qa_ref[r * nq:(r + 1) * nq, dh:] = (pen + slopes[r] * blk_rel).astype(BF16)

    def scan(k, cnt):
        for j in range(2):
            i = 2 * k + j
            hits = sum(flag_ref[0, 0, a, 0, 2 * i] + flag_ref[0, 0, a, 0, 2 * i + 1]
                       for a in range(sub_blocks))
            act = (hits > 0) & (i < tile0)
            list_ref[cnt] = i
            cnt = cnt + act.astype(jnp.int32)
        return cnt

    cnt = lax.fori_loop(0, (tile0 + 1) // 2, scan, 0)
    for i in range(4 * per):
        list_ref[cnt + i] = pad_tile

    def group_scores(it):
        keys = jnp.concatenate(
            [ksel_ref[pl.ds(pl.multiple_of(list_ref[it * per + i] * kt, kt), kt), :]
             for i in range(per)], axis=0)
        return _dot_nt(keys, qa_ref[...]) + ubias_ref[0]

    s = _dot_nt(ksel_ref[pl.ds(pl.multiple_of(t0, kt), nq), :], qa_ref[...]) + dbias_ref[0]
    m0 = jnp.max(s, axis=0, keepdims=True)
    p = jnp.exp(s - m0)
    m_ref[...] = m0
    vals = jnp.concatenate([vselt_ref[tile0 + a] for a in range(sub_blocks)], axis=1)
    acc_ref[...] = _dot(vals, p.astype(BF16))
    sa_ref[...] = group_scores(0)

    tp = t0 + WINDOW + row_q
    t_hi = (tp // SEL_BLOCK).astype(F32)
    t_lo = (tp % SEL_BLOCK).astype(F32)
    for r in range(NSA_REP):
        sl = slopes[r]
        ext = jnp.where(lane == WCOL_PAD, -MASK_BIG, 0.0)
        ext = jnp.where(lane == WCOL_HI, sl * SEL_BLOCK, ext)
        ext = jnp.where(lane == WCOL_LO, sl, ext)
        ext = jnp.where(lane == WCOL_QHI, -sl * SEL_BLOCK * t_hi, ext)
        ext = jnp.where(lane == WCOL_QLO, -sl * t_lo, ext)
        qw_ref[r * nq:(r + 1) * nq, :dh] = q_all[:, r * dh:(r + 1) * dh]
        qw_ref[r * nq:(r + 1) * nq, dh:] = ext.astype(BF16)
    wlen = WINDOW + nq
    s = _dot_nt(kwin_ref[pl.ds(pl.multiple_of(t0, kt), wlen), :], qw_ref[...]) + wbias_ref[...]
    e = jnp.exp(s - jnp.max(s, axis=0, keepdims=True))
    vwin = jnp.concatenate([vwint_ref[tile0 + i] for i in range(WIN_TILES + sub_blocks)], axis=1)
    ow = _dot(vwin, e.astype(BF16))
    owin_ref[...] = ow[:dh] / ow[dh:dh + 1]

    def absorb(s, grp):
        vals = jnp.concatenate([vselt_ref[list_ref[grp * per + i]] for i in range(per)], axis=1)
        m_old = m_ref[...]
        m_new = jnp.maximum(m_old, jnp.max(s, axis=0, keepdims=True))
        alpha = jnp.exp(m_old - m_new)
        p = jnp.exp(s - m_new)
        acc_ref[...] = alpha * acc_ref[...] + _dot(vals, p.astype(BF16))
        m_ref[...] = m_new

    def sel_body(it, carry):
        s = sa_ref[...]
        sb_ref[...] = group_scores(2 * it + 1)
        absorb(s, 2 * it)
        s = sb_ref[...]
        sa_ref[...] = group_scores(2 * it + 2)
        absorb(s, 2 * it + 1)
        return carry

    lax.fori_loop(0, (cnt + 2 * per - 1) // (2 * per), sel_body, 0)
    o_sel = acc_ref[:dh, :] / acc_ref[dh:dh + 1, :]
    o_win = owin_ref[...]

    gate_t = _sigmoid(gate_ref[...]).T
    for a in range(sub_blocks):
        o_cmp = ocmp_ref[0, 0, a]
        qs = slice(a * Q_BLOCK, (a + 1) * Q_BLOCK)
        for r in range(NSA_REP):
            cols = slice(r * nq + a * Q_BLOCK, r * nq + (a + 1) * Q_BLOCK)
            c0 = N_BRANCH * r
            out_t = (gate_t[c0:c0 + 1, qs] * o_cmp[:, r * Q_BLOCK:(r + 1) * Q_BLOCK]
                     + gate_t[c0 + 1:c0 + 2, qs] * o_sel[:, cols]
                     + gate_t[c0 + 2:c0 + 3, qs] * o_win[:, cols])
            o_ref[qs, r * dh:(r + 1) * dh] = out_t.T.astype(o_ref.dtype)


def _nsa_sub_blocks(seq):
    return 2 if (seq // Q_BLOCK) % 2 == 0 else 1


def _nsa_attn(seg_a, seg_b, pen, ocmp, flags, onehot, wext, dbias, ubias, wbias, slopes, batch, seq):
    n = batch * seq
    sub = _nsa_sub_blocks(seq)
    nq = sub * Q_BLOCK
    nqb = seq // nq
    gq = NSA_REP * NSA_HEAD_DIM
    g_ = NSA_KV_GROUPS
    dh, kt = NSA_HEAD_DIM, KEY_TILE
    dk = dh + SEL_LANES
    rq = NSA_REP * nq
    n_t = seq // kt

    def kv_spec(kind):
        return pl.BlockSpec((seq, dh), lambda b, g, q, k=kind: (b, A_KV // dh + k * g_ + g))

    def const_spec(arr):
        return pl.BlockSpec(arr.shape, lambda b, g, q, nd=arr.ndim: (0,) * nd)

    return pl.pallas_call(
        functools.partial(_nsa_attn_kernel, seq=seq, sub_blocks=sub),
        grid=(batch, g_, nqb),
        in_specs=[
            pl.BlockSpec(memory_space=pltpu.SMEM),
            pl.BlockSpec((1, 1, sub, 1, SEL_LANES), lambda b, g, q: (b, g, q, 0, 0),
                         memory_space=pltpu.SMEM),
            pl.BlockSpec((nq, gq), lambda b, g, q: (b * nqb + q, A_Q // gq + g)),
            pl.BlockSpec((nq, LANE), lambda b, g, q: (b * nqb + q, B_GATE // LANE + g)),
            pl.BlockSpec((1, 1, nq, SEL_LANES), lambda b, g, q: (b, g, q, 0)),
            pl.BlockSpec((1, 1, sub, NSA_HEAD_DIM, gq), lambda b, g, q: (b, g, q, 0, 0)),
            kv_spec(2), kv_spec(3), kv_spec(4), kv_spec(5),
            const_spec(onehot), const_spec(wext),
            pl.BlockSpec((1,) + dbias.shape[1:], lambda b, g, q: (g, 0, 0)),
            pl.BlockSpec((1,) + ubias.shape[1:], lambda b, g, q: (g, 0, 0)),
            const_spec(wbias),
        ],
        out_specs=pl.BlockSpec((nq, gq), lambda b, g, q: (b * nqb + q, g)),
        out_shape=jax.ShapeDtypeStruct((n, NSA_WIDTH), BF16),
        scratch_shapes=[
            pltpu.VMEM((seq + kt, dk), BF16),
            pltpu.VMEM((n_t + 1, dh + SUM_ROWS, kt), BF16),
            pltpu.VMEM((seq + WINDOW, dh + LANE), BF16),
            pltpu.VMEM((n_t + WIN_TILES, dh + SUM_ROWS, kt), BF16),
            pltpu.VMEM((rq, dk), BF16),
            pltpu.VMEM((rq, dk), BF16),
            pltpu.SMEM((n_t + 4 * SEL_TILES_PER_STEP,), jnp.int32),
            pltpu.VMEM((1, rq), F32),
            pltpu.VMEM((dh + SUM_ROWS, rq), F32),
            pltpu.VMEM((dh, rq), F32),
            pltpu.VMEM((SEL_TILES_PER_STEP * KEY_TILE, rq), F32),
            pltpu.VMEM((SEL_TILES_PER_STEP * KEY_TILE, rq), F32),
        ],
        compiler_params=pltpu.CompilerParams(
            dimension_semantics=("parallel", "parallel", "arbitrary"), vmem_limit_bytes=VMEM_LIMIT),
        name="nsa_attn",
    )(slopes, flags, seg_a, seg_b, pen, ocmp, seg_a, seg_a, seg_a, seg_a, onehot, wext,
      dbias, ubias, wbias)


def _nsa_tables(slopes, seq):
    g_, kt = NSA_KV_GROUPS, KEY_TILE
    pos = np.arange(seq)
    onehot = jnp.asarray(pos[:, None] // SEL_BLOCK == np.arange(SEL_LANES)[None, :], BF16)
    ext = np.zeros((seq, LANE), np.float32)
    ext[:, WCOL_HI] = (pos + WINDOW) // SEL_BLOCK
    ext[:, WCOL_LO] = (pos + WINDOW) % SEL_BLOCK
    ext[:, WCOL_QHI] = 1.0
    ext[:, WCOL_QLO] = 1.0
    wext = jnp.asarray(ext, BF16)

    nq = _nsa_sub_blocks(seq) * Q_BLOCK

    def alibi_in_block(rows):
        u = jnp.asarray((np.arange(rows) % SEL_BLOCK).astype(np.float32))[None, :, None, None]
        t = jnp.broadcast_to(slopes.reshape(g_, 1, NSA_REP, 1) * u, (g_, rows, NSA_REP, nq))
        return t.reshape(g_, rows, NSA_REP * nq)

    ubias = alibi_in_block(SEL_TILES_PER_STEP * kt)
    kq = np.arange(nq)[:, None] <= np.arange(nq)[None, :]
    causal = np.tile(np.where(kq, 0.0, NEG).astype(np.float32), (1, NSA_REP))
    dbias = alibi_in_block(nq) + jnp.asarray(causal)[None]
    ki = np.arange(WINDOW + nq)[:, None]
    qi = np.arange(nq)[None, :]
    band = np.where((ki > qi) & (ki <= qi + WINDOW), 0.0, NEG).astype(np.float32)
    wbias = jnp.asarray(np.tile(band, (1, NSA_REP)))
    return onehot, wext, dbias, ubias, wbias


def _log_sigmoid(x):
    return jnp.minimum(x, 0.0) - jnp.log(1.0 + jnp.exp(-jnp.abs(x)))


def _split3(x):
    hi = x.astype(BF16)
    r1 = x - hi.astype(F32)
    mid = r1.astype(BF16)
    lo = (r1 - mid.astype(F32)).astype(BF16)
    return hi, mid, lo


def _mlstm_kernel(bias_ref, q_ref, k_ref, v_ref, o_ref, ifc_ref, ng_ref,
                  tri_ref, y_ref, c_ref, n_ref, m_ref):
    ch = pl.program_id(1)
    nh, dh = MLSTM_HEADS, MLSTM_HEAD_DIM
    L = q_ref.shape[0]

    @pl.when(ch == 0)
    def _():
        c_ref[...] = jnp.zeros(c_ref.shape, F32)
        n_ref[...] = jnp.zeros(n_ref.shape, F32)
        m_ref[...] = jnp.zeros(m_ref.shape, F32)

    tri = tri_ref[...]
    lane8 = lax.broadcasted_iota(jnp.int32, (1, LANE), 1)
    bias_c = jnp.zeros((1, LANE), F32)
    for h in range(nh):
        bias_c = jnp.where(lane8 == h, bias_ref[h], bias_c)
        bias_c = jnp.where(lane8 == nh + h, bias_ref[nh + h], bias_c)
    pre_c = ifc_ref[...] + bias_c
    cum_c = sum(_dot(tri, part) for part in _split3(_log_sigmoid(pre_c)))
    pre_r = pre_c.T[:2 * nh]
    cum_r = sum(_dot_nt(part, tri) for part in _split3(_log_sigmoid(pre_r)))

    rr = lax.broadcasted_iota(jnp.int32, (L, 1), 0)
    cc = lax.broadcasted_iota(jnp.int32, (1, L), 1)
    causal = cc <= rr

    for h in range(nh):
        cols = slice(h * dh, (h + 1) * dh)
        qb = q_ref[:, cols]
        kb = k_ref[:, cols]
        vh = v_ref[:, cols]
        qh = qb.astype(F32)
        kh = kb.astype(F32)
        b_c = cum_c[:, nh + h:nh + h + 1]
        li_c = pre_c[:, h:h + 1]
        b_r = cum_r[nh + h:nh + h + 1, :]
        li_r = pre_r[h:h + 1, :]
        m_prev = m_ref[h:h + 1, 0:1]

        dmat = jnp.where(causal, b_c - b_r + li_r, NEG)
        a = b_c + m_prev
        m_j = jnp.maximum(a, jnp.max(dmat, axis=1, keepdims=True))
        w_intra = jnp.exp(dmat - m_j)
        w_inter = jnp.exp(a - m_j)
        sc = _dot_nt(qb, kb) * w_intra
        c_old = c_ref[h]
        n_old = n_ref[h:h + 1, :]
        num = w_inter * _dot(qb, c_old.astype(BF16)) + _dot(sc.astype(BF16), vh)
        den = (w_inter * jnp.sum(qh * n_old, axis=1, keepdims=True)
               + jnp.sum(sc, axis=1, keepdims=True))
        hid = num / jnp.maximum(jnp.abs(den), jnp.exp(-m_j))

        g_tot = b_r[:, L - 1:L]
        lw_c = g_tot - b_c + li_c
        lw_r = g_tot - b_r + li_r
        m_new = jnp.maximum(g_tot + m_prev, jnp.max(lw_r, axis=1, keepdims=True))
        decay = jnp.exp(g_tot + m_prev - m_new)
        kw = jnp.exp(lw_c - m_new) * kh
        c_ref[h] = decay * c_old + _dot(kw.T.astype(BF16), vh)
        n_ref[h:h + 1, :] = decay * n_old + jnp.sum(kw, axis=0, keepdims=True)
        m_ref[h:h + 1, :] = jnp.broadcast_to(m_new, (1, LANE))

        hn = hid * lax.rsqrt(jnp.mean(hid * hid, axis=-1, keepdims=True) + EPS) * ng_ref[:, cols]
        y_ref[:, cols] = (_sigmoid(o_ref[:, cols]) * hn).astype(y_ref.dtype)


def _mlstm(seg_a, seg_b, bias, norm_g, tri, batch, seq, chunk):
    n = batch * seq
    nc = seq // chunk
    w = MLSTM_WIDTH
    nh, dh = MLSTM_HEADS, MLSTM_HEAD_DIM

    def col_spec(off):
        return pl.BlockSpec((chunk, w), lambda b, c, o=off // w: (b * nc + c, o))

    return pl.pallas_call(
        _mlstm_kernel,
        grid=(batch, nc),
        in_specs=[
            pl.BlockSpec(memory_space=pltpu.SMEM),
            col_spec(A_QK), col_spec(A_QK + w), col_spec(A_V), col_spec(B_O),
            pl.BlockSpec((chunk, LANE), lambda b, c: (b * nc + c, B_IF // LANE)),
            pl.BlockSpec((1, w), lambda b, c: (0, 0)),
            pl.BlockSpec((chunk, chunk), lambda b, c: (0, 0)),
        ],
        out_specs=pl.BlockSpec((chunk, w), lambda b, c: (b * nc + c, 0)),
        out_shape=jax.ShapeDtypeStruct((n, w), BF16),
        scratch_shapes=[
            pltpu.VMEM((nh, dh, dh), F32),
            pltpu.VMEM((8, dh), F32),
            pltpu.VMEM((8, LANE), F32),
        ],
        compiler_params=pltpu.CompilerParams(
            dimension_semantics=("parallel", "arbitrary"), vmem_limit_bytes=VMEM_LIMIT),
        name="mlstm",
    )(bias, seg_a, seg_a, seg_a, seg_b, seg_b, norm_g, tri)


def _outproj_kernel(x_ref, ya_ref, ym_ref, wa_ref, wm_ref, o_ref):
    o_ref[...] = x_ref[...] + _dot(ya_ref[...], wa_ref[...]) + _dot(ym_ref[...], wm_ref[...])


def _outproj(x2, ya, ym, wa, wm, tm):
    n, d = x2.shape
    return pl.pallas_call(
        _outproj_kernel,
        grid=(n // tm,),
        in_specs=[
            pl.BlockSpec((tm, d), lambda i: (i, 0)),
            pl.BlockSpec((tm, ya.shape[1]), lambda i: (i, 0)),
            pl.BlockSpec((tm, ym.shape[1]), lambda i: (i, 0)),
            pl.BlockSpec(wa.shape, lambda i: (0, 0)),
            pl.BlockSpec(wm.shape, lambda i: (0, 0)),
        ],
        out_specs=pl.BlockSpec((tm, d), lambda i: (i, 0)),
        out_shape=jax.ShapeDtypeStruct((n, d), F32),
        compiler_params=pltpu.CompilerParams(
            dimension_semantics=("parallel",), vmem_limit_bytes=VMEM_LIMIT),
        name="outproj",
    )(x2, ya, ym, wa, wm)


def _mlp_kernel(x_ref, g_ref, w1_ref, w2_ref, gf_ref, o_ref, h_ref):
    f = pl.program_id(1)

    @pl.when(f == 0)
    def _():
        x = x_ref[...]
        r = lax.rsqrt(jnp.mean(x * x, axis=-1, keepdims=True) + EPS)
        h_ref[...] = (x * r * g_ref[...]).astype(BF16)
        o_ref[...] = x

    u = jnp.maximum(_dot(h_ref[...], w1_ref[...]), 0.0)
    o_ref[...] += _dot((u * u).astype(BF16), w2_ref[...])

    @pl.when(f == pl.num_programs(1) - 1)
    def _():
        x2 = o_ref[...]
        r = lax.rsqrt(jnp.mean(x2 * x2, axis=-1, keepdims=True) + EPS)
        o_ref[...] = x2 * r * gf_ref[...]


def _mlp(x1, g, w1, w2, gf, tm, tf):
    n, d = x1.shape
    dff = w1.shape[1]
    return pl.pallas_call(
        _mlp_kernel,
        grid=(n // tm, dff // tf),
        in_specs=[
            pl.BlockSpec((tm, d), lambda i, f: (i, 0)),
            pl.BlockSpec((1, d), lambda i, f: (0, 0)),
            pl.BlockSpec((d, tf), lambda i, f: (0, f)),
            pl.BlockSpec((tf, d), lambda i, f: (f, 0)),
            pl.BlockSpec((1, d), lambda i, f: (0, 0)),
        ],
        out_specs=pl.BlockSpec((tm, d), lambda i, f: (i, 0)),
        out_shape=jax.ShapeDtypeStruct((n, d), F32),
        scratch_shapes=[pltpu.VMEM((tm, d), BF16)],
        compiler_params=pltpu.CompilerParams(
            dimension_semantics=("parallel", "arbitrary"), vmem_limit_bytes=VMEM_LIMIT),
        name="mlp",
    )(x1, g, w1, w2, gf)


def _row_tile(n, want):
    t = want
    while n % t:
        t //= 2
    return t


def _layer(x2, batch, seq, norm_mix_g, w_in, w_cmp_k1, w_cmp_k2, pos_cmp_k, w_cmp_v1, w_cmp_v2,
           pos_cmp_v, conv_w, conv_b, b_igate, b_fgate, mlstm_norm_g, w_out, norm_mlp_g,
           w_mlp_in, w_mlp_out):
    n, d = x2.shape
    assert seq % Q_BLOCK == 0 and seq >= WINDOW + Q_BLOCK and seq // SEL_BLOCK <= SEL_LANES
    g_ = NSA_KV_GROUPS
    nh = MLSTM_HEADS

    c_gate = NSA_WIDTH + 6 * NSA_KV_WIDTH
    c_qk = c_gate + NSA_HEADS * N_BRANCH
    c_v = c_qk + 2 * MLSTM_WIDTH
    c_o = c_v + MLSTM_WIDTH
    c_i = c_o + MLSTM_WIDTH
    c_f = c_i + nh
    gate_cols = []
    per_g = NSA_REP * N_BRANCH
    for g in range(g_):
        gate_cols += [w_in[:, c_gate + g * per_g:c_gate + (g + 1) * per_g],
                      jnp.zeros((d, LANE - per_g), w_in.dtype)]
    w_ab = jnp.concatenate(
        [w_in[:, c_v:c_o], w_in[:, c_qk:c_v], w_in[:, :c_gate],
         w_in[:, c_o:c_i]] + gate_cols
        + [w_in[:, c_i:c_f + nh], jnp.zeros((d, SEG_B - B_IF - 2 * nh), w_in.dtype)],
        axis=1).astype(BF16)
    scale = jnp.concatenate([jnp.ones((1, A_QK + MLSTM_WIDTH), F32),
                             jnp.full((1, MLSTM_WIDTH), MLSTM_HEAD_DIM ** -0.5, F32),
                             jnp.full((1, NSA_WIDTH), NSA_HEAD_DIM ** -0.5, F32),
                             jnp.ones((1, SEG_A - A_KV + SEG_B), F32)], axis=1)
    conv_pad = ((0, 0), (A_QK, SEG_A - A_Q + SEG_B))
    cw_ab = jnp.pad(conv_w, conv_pad)
    cb_ab = jnp.pad(conv_b.reshape(1, -1), conv_pad)
    g_mix = norm_mix_g.reshape(1, d)

    tm = _row_tile(seq, ROW_TILE)
    seg_a, seg_b = _inproj(x2, g_mix, w_ab, scale, cw_ab, cb_ab, tm, COL_TILE, (A_QK, A_Q), seq)

    n_sub = seq // CMP_STRIDE
    dh = NSA_HEAD_DIM
    w1s = jnp.stack([w_cmp_k1, w_cmp_v1]).reshape(2, 2, CMP_STRIDE, dh, dh)
    w1r = jnp.concatenate([w1s[:, 0], w1s[:, 1]], axis=-1).astype(BF16)
    w2s = jnp.stack([w_cmp_k2, w_cmp_v2]).astype(BF16)
    poss = jnp.stack([pos_cmp_k, pos_cmp_v]).reshape(2, 2, CMP_STRIDE, dh).transpose(0, 2, 1, 3)
    posr = jnp.pad(poss, ((0, 0), (0, 0), (0, 6), (0, 0))).astype(BF16)
    kvc, kvct = _compress(seg_a, w1r, w2s, posr, batch, seq)

    cmp_start = np.arange(n_sub) * CMP_STRIDE
    sel_start = np.arange(SEL_LANES) * SEL_BLOCK
    ovt = ((cmp_start[None, :] < sel_start[:, None] + SEL_BLOCK)
           & (cmp_start[None, :] + CMP_BLOCK - 1 >= sel_start[:, None])
           & (np.arange(n_sub)[None, :] < n_sub - CMP_BLOCK // CMP_STRIDE + 1))
    ovt = jnp.asarray(ovt, BF16)
    slopes = jnp.exp2(-8.0 * jnp.arange(1, NSA_HEADS + 1, dtype=F32) / NSA_HEADS)
    ocmp, pen, flags = _nsa_cmp(seg_a, kvc, kvct, ovt, slopes, batch, seq)
    onehot, wext, dbias, ubias, wbias = _nsa_tables(slopes, seq)
    y_a = _nsa_attn(seg_a, seg_b, pen, ocmp, flags, onehot, wext, dbias, ubias, wbias,
                    slopes, batch, seq)

    chunk = 256 if seq % 256 == 0 else 128
    bias = jnp.concatenate([b_igate, b_fgate]).astype(F32)
    tri = jnp.asarray(np.tril(np.ones((chunk, chunk), np.float32)), BF16)
    y_m = _mlstm(seg_a, seg_b, bias, mlstm_norm_g.reshape(1, -1), tri, batch, seq, chunk)

    w_o = w_out.astype(BF16)
    x1 = _outproj(x2, y_a, y_m, w_o[:NSA_WIDTH], w_o[NSA_WIDTH:], _row_tile(n, 512))
    return x1, (norm_mlp_g.reshape(1, d), w_mlp_in.astype(BF16), w_mlp_out.astype(BF16))


def kernel(x, norm_mix_g, w_in, w_cmp_k1, w_cmp_k2, pos_cmp_k, w_cmp_v1, w_cmp_v2, pos_cmp_v, conv_w, conv_b, b_igate, b_fgate, mlstm_norm_g, w_out, norm_mlp_g, w_mlp_in, w_mlp_out, norm_f_g):
    batch, seq, d = x.shape
    depth = w_in.shape[0]
    assert depth == 1, "the final RMSNorm is fused into the last layer's channel mixer"
    x2 = x.reshape(batch * seq, d)
    tm = _row_tile(batch * seq, ROW_TILE)
    for l in range(depth):
        x1, (g_mlp, w1, w2) = _layer(
            x2, batch, seq, norm_mix_g[l], w_in[l], w_cmp_k1[l], w_cmp_k2[l], pos_cmp_k[l],
            w_cmp_v1[l], w_cmp_v2[l], pos_cmp_v[l], conv_w[l], conv_b[l], b_igate[l], b_fgate[l],
            mlstm_norm_g[l], w_out[l], norm_mlp_g[l], w_mlp_in[l], w_mlp_out[l])
        x2 = _mlp(x1, g_mlp, w1, w2, norm_f_g.reshape(1, d), tm, 512)
    return x2.reshape(batch, seq, d)
```

```python
import functools

import numpy as np
import jax
import jax.numpy as jnp
from jax import lax
from jax.experimental import pallas as pl
from jax.experimental.pallas import tpu as pltpu

F32 = jnp.float32
BF16 = jnp.bfloat16

EPS = 1e-6
NEG = -1e30
FORCE_BONUS = 1e4
PICKED = -3e38
MASK_BIG = 1e30

D_MODEL = 2048
NSA_HEAD_DIM = 128
NSA_WIDTH = D_MODEL // 2
NSA_HEADS = NSA_WIDTH // NSA_HEAD_DIM
NSA_REP = 4
NSA_KV_GROUPS = NSA_HEADS // NSA_REP
NSA_KV_WIDTH = NSA_KV_GROUPS * NSA_HEAD_DIM
CMP_BLOCK = 32
CMP_STRIDE = 16
SEL_BLOCK = 64
SEL_TOPK = 16
WINDOW = 512
Q_BLOCK = 128
N_BRANCH = 3
MLSTM_HEAD_DIM = 256
MLSTM_WIDTH = D_MODEL - NSA_WIDTH
MLSTM_HEADS = MLSTM_WIDTH // MLSTM_HEAD_DIM
CONV_WIDTH = 4
D_FF = 4 * D_MODEL

LANE = 128
SEL_LANES = 128
KEY_TILE = 128
SEL_TILES_PER_STEP = 2
WIN_TILES = WINDOW // KEY_TILE
SUM_ROWS = 16
VMEM_LIMIT = 56 * 1024 * 1024
ROW_TILE = 1024

A_V, A_QK = 0, MLSTM_WIDTH
A_Q = A_QK + 2 * MLSTM_WIDTH
A_KV = A_Q + NSA_WIDTH
SEG_A = A_KV + 6 * NSA_KV_WIDTH
B_O = 0
B_GATE = MLSTM_WIDTH
B_IF = B_GATE + NSA_KV_GROUPS * LANE
SEG_B = B_IF + 2 * LANE
COL_TILE = 512

WCOL_PAD, WCOL_HI, WCOL_LO, WCOL_QHI, WCOL_QLO = 0, 1, 2, 3, 4


def _dot(a, b):
    return jnp.dot(a, b, preferred_element_type=F32)


def _dot_nt(a, b):
    return lax.dot_general(a, b, (((1,), (1,)), ((), ())), preferred_element_type=F32)


def _sigmoid(x):
    return 1.0 / (1.0 + jnp.exp(-x))


def _shifted(x, tail, s):
    xs = pltpu.roll(x, s, axis=0)
    ts = pltpu.roll(tail, s, axis=0)
    row8 = lax.broadcasted_iota(jnp.int32, (8, 1), 0)
    head = jnp.where(row8 < s, ts, xs[:8])
    return jnp.concatenate([head, xs[8:]], axis=0)


def _conv_silu(x, tail, w, b):
    y = b + _shifted(x, tail, CONV_WIDTH - 1) * w[0:1]
    for i in range(1, CONV_WIDTH - 1):
        y = y + _shifted(x, tail, CONV_WIDTH - 1 - i) * w[i:i + 1]
    y = y + x * w[CONV_WIDTH - 1:CONV_WIDTH]
    return y * _sigmoid(y)


def _inproj_kernel(x_ref, g_ref, w_ref, cs_ref, cw_ref, cb_ref, oa_ref, ob_ref, h_ref, halo_ref, *,
                   na, conv_lo, conv_hi, tiles_per_seq):
    i = pl.program_id(0)
    j = pl.program_id(1)

    @pl.when(j == 0)
    def _():
        x = x_ref[...]
        r = lax.rsqrt(jnp.mean(x * x, axis=-1, keepdims=True) + EPS)
        h_ref[...] = (x * r * g_ref[...]).astype(BF16)

    is_conv = (j >= conv_lo) & (j < conv_hi)

    @pl.when((j < na) & jnp.logical_not(is_conv))
    def _():
        oa_ref[...] = (_dot(h_ref[...], w_ref[...]) * cs_ref[...]).astype(oa_ref.dtype)

    @pl.when(j >= na)
    def _():
        ob_ref[...] = (_dot(h_ref[...], w_ref[...]) * cs_ref[...]).astype(ob_ref.dtype)

    @pl.when(is_conv)
    def _():
        slot = j - conv_lo

        @pl.when(i % tiles_per_seq == 0)
        def _():
            halo_ref[slot] = jnp.zeros(halo_ref.shape[1:], F32)

        acc = _dot(h_ref[...], w_ref[...])
        tail = halo_ref[slot]
        halo_ref[slot] = acc[acc.shape[0] - 8:]
        y = _conv_silu(acc, tail, cw_ref[...], cb_ref[...])
        oa_ref[...] = (y * cs_ref[...]).astype(oa_ref.dtype)


def _inproj(x2, g, w, cs, cw, cb, tm, tn, conv_cols, seq):
    n, d = x2.shape
    na, nb = SEG_A // tn, SEG_B // tn
    conv_lo, conv_hi = conv_cols[0] // tn, conv_cols[1] // tn
    assert conv_cols[0] % tn == 0 and conv_cols[1] % tn == 0 and seq % tm == 0 and conv_hi <= na
    assert SEG_A % tn == 0 and SEG_B % tn == 0 and w.shape[1] == SEG_A + SEG_B
    return pl.pallas_call(
        functools.partial(_inproj_kernel, na=na, conv_lo=conv_lo, conv_hi=conv_hi,
                          tiles_per_seq=seq // tm),
        grid=(n // tm, na + nb),
        in_specs=[
            pl.BlockSpec((tm, d), lambda i, j: (i, 0)),
            pl.BlockSpec((1, d), lambda i, j: (0, 0)),
            pl.BlockSpec((d, tn), lambda i, j: (0, j)),
            pl.BlockSpec((1, tn), lambda i, j: (0, j)),
            pl.BlockSpec((CONV_WIDTH, tn), lambda i, j: (0, j)),
            pl.BlockSpec((1, tn), lambda i, j: (0, j)),
        ],
        out_specs=[pl.BlockSpec((tm, tn), lambda i, j: (i, jnp.minimum(j, na - 1))),
                   pl.BlockSpec((tm, tn), lambda i, j: (i, jnp.maximum(j - na, 0)))],
        out_shape=[jax.ShapeDtypeStruct((n, SEG_A), BF16), jax.ShapeDtypeStruct((n, SEG_B), F32)],
        scratch_shapes=[pltpu.VMEM((tm, d), BF16),
                        pltpu.VMEM((conv_hi - conv_lo, 8, tn), F32)],
        compiler_params=pltpu.CompilerParams(
            dimension_semantics=("arbitrary", "arbitrary"), vmem_limit_bytes=VMEM_LIMIT),
        name="inproj",
    )(x2, g, w, cs, cw, cb)


def _compress_kernel(x_ref, w1_ref, w2_ref, pos_ref, o_ref, ot_ref, xf_ref):
    n_sub = o_ref.shape[2]
    dh = NSA_HEAD_DIM
    xf_ref[...] = x_ref[...].astype(F32)
    acc = jnp.zeros((n_sub, 2 * dh), F32)
    posw = jnp.zeros((1, dh), F32)
    for p in range(CMP_STRIDE):
        wp = w1_ref[0, p]
        acc = acc + _dot(xf_ref[pl.ds(p, n_sub, stride=CMP_STRIDE), :].astype(BF16), wp)
        pw = _dot(pos_ref[0, p], wp)
        posw = posw + pw[0:1, :dh] + pw[1:2, dh:]
    bot = pltpu.roll(acc[:, dh:], n_sub - 1, axis=0)
    pre = acc[:, :dh] + bot + posw
    hid = pre * _sigmoid(pre)
    out = _dot(hid.astype(BF16), w2_ref[0])
    o_ref[0, 0] = out.astype(o_ref.dtype)
    ot_ref[0, 0] = out.T.astype(ot_ref.dtype)


def _compress(seg_a, w1r, w2s, posr, batch, seq):
    g = NSA_KV_GROUPS
    c = 2 * g
    dh = NSA_HEAD_DIM
    n_sub = seq // CMP_STRIDE
    return pl.pallas_call(
        _compress_kernel,
        grid=(batch, c),
        in_specs=[
            pl.BlockSpec((seq, dh), lambda i, j: (i, A_KV // dh + j)),
            pl.BlockSpec((1, CMP_STRIDE, dh, 2 * dh), lambda i, j: (j // g, 0, 0, 0)),
            pl.BlockSpec((1, dh, dh), lambda i, j: (j // g, 0, 0)),
            pl.BlockSpec((1, CMP_STRIDE, 8, dh), lambda i, j: (j // g, 0, 0, 0)),
        ],
        out_specs=[pl.BlockSpec((1, 1, n_sub, dh), lambda i, j: (i, j, 0, 0)),
                   pl.BlockSpec((1, 1, dh, n_sub), lambda i, j: (i, j, 0, 0))],
        out_shape=[jax.ShapeDtypeStruct((batch, c, n_sub, dh), BF16),
                   jax.ShapeDtypeStruct((batch, c, dh, n_sub), BF16)],
        scratch_shapes=[pltpu.VMEM((seq, dh), F32)],
        compiler_params=pltpu.CompilerParams(
            dimension_semantics=("parallel", "parallel"), vmem_limit_bytes=VMEM_LIMIT),
        name="compress",
    )(seg_a, w1r, w2s, posr)


def _stack_heads(q_all):
    dh = NSA_HEAD_DIM
    return jnp.concatenate([q_all[:, r * dh:(r + 1) * dh] for r in range(NSA_REP)], axis=0)


def _nsa_cmp_kernel(slopes_ref, q_ref, kc_ref, vct_ref, ovt_ref, ocmp_ref, pen_ref, flag_ref, *,
                    seq, sub_blocks):
    step = pl.program_id(2)
    nq_step = sub_blocks * Q_BLOCK
    chunk = min(SEL_LANES, kc_ref.shape[2])
    n_chunks = kc_ref.shape[2] // chunk
    need = ((step + 1) * nq_step - CMP_BLOCK) // CMP_STRIDE + 1
    n_need = (need + chunk - 1) // chunk
    for v in range(1, n_chunks + 1):
        cond = (n_need == v) if v < n_chunks else (n_need >= v)
        if v == 1:
            cond = n_need <= 1

        @pl.when(cond)
        def _(v=v):
            for sub in range(sub_blocks):
                rows = slice(sub * Q_BLOCK, (sub + 1) * Q_BLOCK)
                _nsa_cmp_block(slopes_ref, q_ref[rows, :], kc_ref, vct_ref, ovt_ref,
                               ocmp_ref.at[0, 0, sub], pen_ref.at[0, 0, rows], flag_ref.at[0, 0, sub],
                               step * sub_blocks + sub, seq, v * chunk)


def _nsa_cmp_block(slopes_ref, q_all, kc_ref, vct_ref, ovt_ref, ocmp_ref, pen_ref, flag_ref, qb, seq,
                   n_use):
    g = pl.program_id(1)
    nq = Q_BLOCK
    n_cpad = n_use
    n_cmp = seq // CMP_STRIDE - CMP_BLOCK // CMP_STRIDE + 1
    t0 = qb * nq

    s_t = _dot_nt(kc_ref[0, 0, :n_use, :], _stack_heads(q_all))
    n_s = lax.broadcasted_iota(jnp.int32, (n_cpad, 1), 0)
    q_l = lax.broadcasted_iota(jnp.int32, (1, nq), 1)
    dist = (t0 - (CMP_BLOCK - 1)) + q_l - n_s * CMP_STRIDE
    valid = (dist >= 0) & (n_s < n_cmp)
    dist_f = dist.astype(F32)
    probs = []
    p_sum = jnp.zeros((n_cpad, nq), F32)
    for r in range(NSA_REP):
        slope = slopes_ref[g * NSA_REP + r]
        s = jnp.where(valid, s_t[:, r * nq:(r + 1) * nq] - slope * dist_f, NEG)
        m = jnp.max(s, axis=0, keepdims=True)
        e = jnp.exp(s - m)
        inv = jnp.where(m > 0.5 * NEG, 1.0 / jnp.sum(e, axis=0, keepdims=True), 0.0)
        p = e * inv
        probs.append(p.astype(BF16))
        p_sum = p_sum + p
    ocmp_ref[...] = _dot(vct_ref[0, 0, :, :n_use], jnp.concatenate(probs, axis=1))

    p_hi = p_sum.astype(BF16)
    p_lo = (p_sum - p_hi.astype(F32)).astype(BF16)
    ovt = ovt_ref[:, :n_use]
    imp = _dot(ovt, p_hi) + _dot(ovt, p_lo)
    j_i = lax.broadcasted_iota(jnp.int32, (SEL_LANES, 1), 0)
    t_l = t0 + q_l
    cur = t_l // SEL_BLOCK
    forced = (j_i == 0) | (j_i == cur) | (j_i == cur - 1)
    causal_blk = j_i * SEL_BLOCK <= t_l
    val = jnp.where(causal_blk, jnp.where(forced, imp + FORCE_BONUS, imp), NEG)
    j_f = j_i.astype(F32)
    sel_t = jnp.zeros((SEL_LANES, nq), F32)
    for _ in range(min(SEL_TOPK, seq // SEL_BLOCK)):
        mx = jnp.max(val, axis=0, keepdims=True)
        first = jnp.min(jnp.where(val == mx, j_f, float(SEL_LANES)), axis=0, keepdims=True)
        pick = j_f == first
        sel_t = jnp.where(pick, 1.0, sel_t)
        val = jnp.where(pick, PICKED, val)
    sel = sel_t.T
    pen_ref[...] = ((sel - 1.0) * MASK_BIG).astype(pen_ref.dtype)
    flag_ref[...] = (jnp.max(sel, axis=0, keepdims=True) > 0.0).astype(jnp.int32)


def _nsa_cmp(seg_a, kvc, kvct, ovt, slopes, batch, seq):
    nqb = seq // Q_BLOCK
    gq = NSA_REP * NSA_HEAD_DIM
    g_ = NSA_KV_GROUPS
    n_cpad = kvc.shape[2]
    sub = _nsa_sub_blocks(seq)
    nstep = nqb // sub
    return pl.pallas_call(
        functools.partial(_nsa_cmp_kernel, seq=seq, sub_blocks=sub),
        grid=(batch, g_, nstep),
        in_specs=[
            pl.BlockSpec(memory_space=pltpu.SMEM),
            pl.BlockSpec((sub * Q_BLOCK, gq), lambda b, g, q: (b * nstep + q, A_Q // gq + g)),
            pl.BlockSpec((1, 1, n_cpad, NSA_HEAD_DIM), lambda b, g, q: (b, g, 0, 0)),
            pl.BlockSpec((1, 1, NSA_HEAD_DIM, n_cpad), lambda b, g, q: (b, g_ + g, 0, 0)),
            pl.BlockSpec((SEL_LANES, n_cpad), lambda b, g, q: (0, 0)),
        ],
        out_specs=[
            pl.BlockSpec((1, 1, sub, NSA_HEAD_DIM, gq), lambda b, g, q: (b, g, q, 0, 0)),
            pl.BlockSpec((1, 1, sub * Q_BLOCK, SEL_LANES), lambda b, g, q: (b, g, q, 0)),
            pl.BlockSpec((1, 1, sub, 1, SEL_LANES), lambda b, g, q: (b, g, q, 0, 0)),
        ],
        out_shape=[
            jax.ShapeDtypeStruct((batch, g_, nqb, NSA_HEAD_DIM, gq), F32),
            jax.ShapeDtypeStruct((batch, g_, seq, SEL_LANES), BF16),
            jax.ShapeDtypeStruct((batch, g_, nqb, 1, SEL_LANES), jnp.int32),
        ],
        compiler_params=pltpu.CompilerParams(
            dimension_semantics=("parallel", "parallel", "parallel"), vmem_limit_bytes=VMEM_LIMIT),
        name="nsa_cmp",
    )(slopes, seg_a, kvc, kvct, ovt)


def _nsa_attn_kernel(slopes_ref, flag_ref, q_ref, gate_ref, pen_ref, ocmp_ref, ks_ref, vs_ref,
                     kw_ref, vw_ref, onehot_ref, wext_ref, dbias_ref, ubias_ref, wbias_ref, o_ref,
                     ksel_ref, vselt_ref, kwin_ref, vwint_ref,
                     qa_ref, qw_ref, list_ref, m_ref, acc_ref, owin_ref, sa_ref, sb_ref,
                     *, seq, sub_blocks):
    g = pl.program_id(1)
    step = pl.program_id(2)
    dh = NSA_HEAD_DIM
    nq = sub_blocks * Q_BLOCK
    kt = KEY_TILE
    per = SEL_TILES_PER_STEP
    pad_tile = seq // kt
    t0 = step * nq
    tile0 = step * sub_blocks

    @pl.when(step == 0)
    def _():
        ksel_ref[0:seq, :dh] = ks_ref[...]
        ksel_ref[0:seq, dh:] = onehot_ref[...]
        ksel_ref[seq:, :dh] = jnp.zeros((kt, dh), BF16)
        ksel_ref[seq:, dh:] = jnp.ones((kt, SEL_LANES), BF16)
        lane2 = lax.broadcasted_iota(jnp.int32, (WINDOW, dh + LANE), 1)
        kwin_ref[0:WINDOW, :] = jnp.where(lane2 == dh + WCOL_PAD, 1.0, 0.0).astype(BF16)
        kwin_ref[WINDOW:, :dh] = kw_ref[...]
        kwin_ref[WINDOW:, dh:] = wext_ref[...]
        zero_tile = jnp.zeros((dh + SUM_ROWS, kt), BF16)
        ones_rows = jnp.ones((SUM_ROWS, kt), BF16)
        vselt_ref[pad_tile] = zero_tile
        for i in range(WIN_TILES):
            vwint_ref[i] = zero_tile

        def transpose_tile(t, carry):
            r0 = pl.multiple_of(t * kt, kt)
            vselt_ref[t, :dh] = vs_ref[pl.ds(r0, kt), :].astype(F32).T.astype(BF16)
            vselt_ref[t, dh:] = ones_rows
            vwint_ref[t + WIN_TILES, :dh] = vw_ref[pl.ds(r0, kt), :].astype(F32).T.astype(BF16)
            vwint_ref[t + WIN_TILES, dh:] = ones_rows
            return carry

        lax.fori_loop(0, seq // kt, transpose_tile, 0)

    slopes = [slopes_ref[g * NSA_REP + r] for r in range(NSA_REP)]
    row_q = lax.broadcasted_iota(jnp.int32, (nq, 1), 0)
    lane = lax.broadcasted_iota(jnp.int32, (1, LANE), 1)
    q_all = q_ref[...]

    pen = pen_ref[0, 0].astype(F32)
    blk_rel = ((lane - (t0 + row_q) // SEL_BLOCK) * SEL_BLOCK).astype(F32)
    for r in range(NSA_REP):
        qa_ref[r * nq:(r + 1) * nq, :dh] = q_all[:, r * dh:(r + 1) * dh]
        qa_ref[r * nq:(r + 1) * nq, dh:] = (pen + slopes[r] * blk_rel).astype(BF16)

    def scan(k, cnt):
        for j in range(2):
            i = 2 * k + j
            hits = sum(flag_ref[0, 0, a, 0, 2 * i] + flag_ref[0, 0, a, 0, 2 * i + 1]
                       for a in range(sub_blocks))
            act = (hits > 0) & (i < tile0)
            list_ref[cnt] = i
            cnt = cnt + act.astype(jnp.int32)
        return cnt

    cnt = lax.fori_loop(0, (tile0 + 1) // 2, scan, 0)
    for i in range(4 * per):
        list_ref[cnt + i] = pad_tile

    def group_scores(it):
        keys = jnp.concatenate(
            [ksel_ref[pl.ds(pl.multiple_of(list_ref[it * per + i] * kt, kt), kt), :]
             for i in range(per)], axis=0)
        return _dot_nt(keys, qa_ref[...]) + ubias_ref[0]

    s = _dot_nt(ksel_ref[pl.ds(pl.multiple_of(t0, kt), nq), :], qa_ref[...]) + dbias_ref[0]
    m0 = jnp.max(s, axis=0, keepdims=True)
    p = jnp.exp(s - m0)
    m_ref[...] = m0
    vals = jnp.concatenate([vselt_ref[tile0 + a] for a in range(sub_blocks)], axis=1)
    acc_ref[...] = _dot(vals, p.astype(BF16))
    sa_ref[...] = group_scores(0)

    tp = t0 + WINDOW + row_q
    t_hi = (tp // SEL_BLOCK).astype(F32)
    t_lo = (tp % SEL_BLOCK).astype(F32)
    for r in range(NSA_REP):
        sl = slopes[r]
        ext = jnp.where(lane == WCOL_PAD, -MASK_BIG, 0.0)
        ext = jnp.where(lane == WCOL_HI, sl * SEL_BLOCK, ext)
        ext = jnp.where(lane == WCOL_LO, sl, ext)
        ext = jnp.where(lane == WCOL_QHI, -sl * SEL_BLOCK * t_hi, ext)
        ext = jnp.where(lane == WCOL_QLO, -sl * t_lo, ext)
        qw_ref[r * nq:(r + 1) * nq, :dh] = q_all[:, r * dh:(r + 1) * dh]
        qw_ref[r * nq:(r + 1) * nq, dh:] = ext.astype(BF16)
    wlen = WINDOW + nq
    s = _dot_nt(kwin_ref[pl.ds(pl.multiple_of(t0, kt), wlen), :], qw_ref[...]) + wbias_ref[...]
    e = jnp.exp(s - jnp.max(s, axis=0, keepdims=True))
    vwin = jnp.concatenate([vwint_ref[tile0 + i] for i in range(WIN_TILES + sub_blocks)], axis=1)
    ow = _dot(vwin, e.astype(BF16))
    owin_ref[...] = ow[:dh] / ow[dh:dh + 1]

    def absorb(s, grp):
        vals = jnp.concatenate([vselt_ref[list_ref[grp * per + i]] for i in range(per)], axis=1)
        m_old = m_ref[...]
        m_new = jnp.maximum(m_old, jnp.max(s, axis=0, keepdims=True))
        alpha = jnp.exp(m_old - m_new)
        p = jnp.exp(s - m_new)
        acc_ref[...] = alpha * acc_ref[...] + _dot(vals, p.astype(BF16))
        m_ref[...] = m_new

    def sel_body(it, carry):
        s = sa_ref[...]
        sb_ref[...] = group_scores(2 * it + 1)
        absorb(s, 2 * it)
        s = sb_ref[...]
        sa_ref[...] = group_scores(2 * it + 2)
        absorb(s, 2 * it + 1)
        return carry

    lax.fori_loop(0, (cnt + 2 * per - 1) // (2 * per), sel_body, 0)
    o_sel = acc_ref[:dh, :] / acc_ref[dh:dh + 1, :]
    o_win = owin_ref[...]

    gate_t = _sigmoid(gate_ref[...]).T
    for a in range(sub_blocks):
        o_cmp = ocmp_ref[0, 0, a]
        qs = slice(a * Q_BLOCK, (a + 1) * Q_BLOCK)
        for r in range(NSA_REP):
            cols = slice(r * nq + a * Q_BLOCK, r * nq + (a + 1) * Q_BLOCK)
            c0 = N_BRANCH * r
            out_t = (gate_t[c0:c0 + 1, qs] * o_cmp[:, r * Q_BLOCK:(r + 1) * Q_BLOCK]
                     + gate_t[c0 + 1:c0 + 2, qs] * o_sel[:, cols]
                     + gate_t[c0 + 2:c0 + 3, qs] * o_win[:, cols])
            o_ref[qs, r * dh:(r + 1) * dh] = out_t.T.astype(o_ref.dtype)


def _nsa_sub_blocks(seq):
    return 2 if (seq // Q_BLOCK) % 2 == 0 else 1


def _nsa_attn(seg_a, seg_b, pen, ocmp, flags, onehot, wext, dbias, ubias, wbias, slopes, batch, seq):
    n = batch * seq
    sub = _nsa_sub_blocks(seq)
    nq = sub * Q_BLOCK
    nqb = seq // nq
    gq = NSA_REP * NSA_HEAD_DIM
    g_ = NSA_KV_GROUPS
    dh, kt = NSA_HEAD_DIM, KEY_TILE
    dk = dh + SEL_LANES
    rq = NSA_REP * nq
    n_t = seq // kt

    def kv_spec(kind):
        return pl.BlockSpec((seq, dh), lambda b, g, q, k=kind: (b, A_KV // dh + k * g_ + g))

    def const_spec(arr):
        return pl.BlockSpec(arr.shape, lambda b, g, q, nd=arr.ndim: (0,) * nd)

    return pl.pallas_call(
        functools.partial(_nsa_attn_kernel, seq=seq, sub_blocks=sub),
        grid=(batch, g_, nqb),
        in_specs=[
            pl.BlockSpec(memory_space=pltpu.SMEM),
            pl.BlockSpec((1, 1, sub, 1, SEL_LANES), lambda b, g, q: (b, g, q, 0, 0),
                         memory_space=pltpu.SMEM),
            pl.BlockSpec((nq, gq), lambda b, g, q: (b * nqb + q, A_Q // gq + g)),
            pl.BlockSpec((nq, LANE), lambda b, g, q: (b * nqb + q, B_GATE // LANE + g)),
            pl.BlockSpec((1, 1, nq, SEL_LANES), lambda b, g, q: (b, g, q, 0)),
            pl.BlockSpec((1, 1, sub, NSA_HEAD_DIM, gq), lambda b, g, q: (b, g, q, 0, 0)),
            kv_spec(2), kv_spec(3), kv_spec(4), kv_spec(5),
            const_spec(onehot), const_spec(wext),
            pl.BlockSpec((1,) + dbias.shape[1:], lambda b, g, q: (g, 0, 0)),
            pl.BlockSpec((1,) + ubias.shape[1:], lambda b, g, q: (g, 0, 0)),
            const_spec(wbias),
        ],
        out_specs=pl.BlockSpec((nq, gq), lambda b, g, q: (b * nqb + q, g)),
        out_shape=jax.ShapeDtypeStruct((n, NSA_WIDTH), BF16),
        scratch_shapes=[
            pltpu.VMEM((seq + kt, dk), BF16),
            pltpu.VMEM((n_t + 1, dh + SUM_ROWS, kt), BF16),
            pltpu.VMEM((seq + WINDOW, dh + LANE), BF16),
            pltpu.VMEM((n_t + WIN_TILES, dh + SUM_ROWS, kt), BF16),
            pltpu.VMEM((rq, dk), BF16),
            pltpu.VMEM((rq, dk), BF16),
            pltpu.SMEM((n_t + 4 * SEL_TILES_PER_STEP,), jnp.int32),
            pltpu.VMEM((1, rq), F32),
            pltpu.VMEM((dh + SUM_ROWS, rq), F32),
            pltpu.VMEM((dh, rq), F32),
            pltpu.VMEM((SEL_TILES_PER_STEP * KEY_TILE, rq), F32),
            pltpu.VMEM((SEL_TILES_PER_STEP * KEY_TILE, rq), F32),
        ],
        compiler_params=pltpu.CompilerParams(
            dimension_semantics=("parallel", "parallel", "arbitrary"), vmem_limit_bytes=VMEM_LIMIT),
        name="nsa_attn",
    )(slopes, flags, seg_a, seg_b, pen, ocmp, seg_a, seg_a, seg_a, seg_a, onehot, wext,
      dbias, ubias, wbias)


def _nsa_tables(slopes, seq):
    g_, kt = NSA_KV_GROUPS, KEY_TILE
    pos = np.arange(seq)
    onehot = jnp.asarray(pos[:, None] // SEL_BLOCK == np.arange(SEL_LANES)[None, :], BF16)
    ext = np.zeros((seq, LANE), np.float32)
    ext[:, WCOL_HI] = (pos + WINDOW) // SEL_BLOCK
    ext[:, WCOL_LO] = (pos + WINDOW) % SEL_BLOCK
    ext[:, WCOL_QHI] = 1.0
    ext[:, WCOL_QLO] = 1.0
    wext = jnp.asarray(ext, BF16)

    nq = _nsa_sub_blocks(seq) * Q_BLOCK

    def alibi_in_block(rows):
        u = jnp.asarray((np.arange(rows) % SEL_BLOCK).astype(np.float32))[None, :, None, None]
        t = jnp.broadcast_to(slopes.reshape(g_, 1, NSA_REP, 1) * u, (g_, rows, NSA_REP, nq))
        return t.reshape(g_, rows, NSA_REP * nq)

    ubias = alibi_in_block(SEL_TILES_PER_STEP * kt)
    kq = np.arange(nq)[:, None] <= np.arange(nq)[None, :]
    causal = np.tile(np.where(kq, 0.0, NEG).astype(np.float32), (1, NSA_REP))
    dbias = alibi_in_block(nq) + jnp.asarray(causal)[None]
    ki = np.arange(WINDOW + nq)[:, None]
    qi = np.arange(nq)[None, :]
    band = np.where((ki > qi) & (ki <= qi + WINDOW), 0.0, NEG).astype(np.float32)
    wbias = jnp.asarray(np.tile(band, (1, NSA_REP)))
    return onehot, wext, dbias, ubias, wbias


def _log_sigmoid(x):
    return jnp.minimum(x, 0.0) - jnp.log(1.0 + jnp.exp(-jnp.abs(x)))


def _split3(x):
    hi = x.astype(BF16)
    r1 = x - hi.astype(F32)
    mid = r1.astype(BF16)
    lo = (r1 - mid.astype(F32)).astype(BF16)
    return hi, mid, lo


def _mlstm_kernel(bias_ref, q_ref, k_ref, v_ref, o_ref, ifc_ref, ng_ref,
                  tri_ref, y_ref, c_ref, n_ref, m_ref):
    ch = pl.program_id(1)

    @pl.when(ch == 0)
    def _():
        c_ref[...] = jnp.zeros(c_ref.shape, F32)
        n_ref[...] = jnp.zeros(n_ref.shape, F32)
        m_ref[...] = jnp.zeros(m_ref.shape, F32)

    for sq in range(q_ref.shape[0]):
        _mlstm_chunk(bias_ref, q_ref.at[sq], k_ref.at[sq], v_ref.at[sq], o_ref.at[sq], ifc_ref.at[sq],
                     ng_ref, tri_ref, y_ref.at[sq], c_ref.at[sq], n_ref.at[sq], m_ref.at[sq])


def _mlstm_chunk(bias_ref, q_ref, k_ref, v_ref, o_ref, ifc_ref, ng_ref, tri_ref, y_ref,
                 c_ref, n_ref, m_ref):
    nh, dh = MLSTM_HEADS, MLSTM_HEAD_DIM
    L = q_ref.shape[0]
    tri = tri_ref[...]
    lane8 = lax.broadcasted_iota(jnp.int32, (1, LANE), 1)
    bias_c = jnp.zeros((1, LANE), F32)
    for h in range(nh):
        bias_c = jnp.where(lane8 == h, bias_ref[h], bias_c)
        bias_c = jnp.where(lane8 == nh + h, bias_ref[nh + h], bias_c)
    pre_c = ifc_ref[...] + bias_c
    cum_c = sum(_dot(tri, part) for part in _split3(_log_sigmoid(pre_c)))
    pre_r = pre_c.T[:2 * nh]
    cum_r = sum(_dot_nt(part, tri) for part in _split3(_log_sigmoid(pre_r)))

    rr = lax.broadcasted_iota(jnp.int32, (L, 1), 0)
    cc = lax.broadcasted_iota(jnp.int32, (1, L), 1)
    causal = cc <= rr

    for h in range(nh):
        cols = slice(h * dh, (h + 1) * dh)
        qb = q_ref[:, cols]
        kb = k_ref[:, cols]
        vh = v_ref[:, cols]
        qh = qb.astype(F32)
        kh = kb.astype(F32)
        b_c = cum_c[:, nh + h:nh + h + 1]
        li_c = pre_c[:, h:h + 1]
        b_r = cum_r[nh + h:nh + h + 1, :]
        li_r = pre_r[h:h + 1, :]
        m_prev = m_ref[h:h + 1, 0:1]

        dmat = jnp.where(causal, b_c - b_r + li_r, NEG)
        a = b_c + m_prev
        m_j = jnp.maximum(a, jnp.max(dmat, axis=1, keepdims=True))
        w_intra = jnp.exp(dmat - m_j)
        w_inter = jnp.exp(a - m_j)
        sc = _dot_nt(qb, kb) * w_intra
        c_old = c_ref[h]
        n_old = n_ref[h:h + 1, :]
        num = w_inter * _dot(qb, c_old.astype(BF16)) + _dot(sc.astype(BF16), vh)
        den = (w_inter * jnp.sum(qh * n_old, axis=1, keepdims=True)
               + jnp.sum(sc, axis=1, keepdims=True))
        hid = num / jnp.maximum(jnp.abs(den), jnp.exp(-m_j))

        g_tot = b_r[:, L - 1:L]
        lw_c = g_tot - b_c + li_c
        lw_r = g_tot - b_r + li_r
        m_new = jnp.maximum(g_tot + m_prev, jnp.max(lw_r, axis=1, keepdims=True))
        decay = jnp.exp(g_tot + m_prev - m_new)
        kw = jnp.exp(lw_c - m_new) * kh
        c_ref[h] = decay * c_old + _dot(kw.T.astype(BF16), vh)
        n_ref[h:h + 1, :] = decay * n_old + jnp.sum(kw, axis=0, keepdims=True)
        m_ref[h:h + 1, :] = jnp.broadcast_to(m_new, (1, LANE))

        hn = hid * lax.rsqrt(jnp.mean(hid * hid, axis=-1, keepdims=True) + EPS) * ng_ref[:, cols]
        y_ref[:, cols] = (_sigmoid(o_ref[:, cols]) * hn).astype(y_ref.dtype)


def _mlstm(seg_a, seg_b, bias, norm_g, tri, batch, seq, chunk):
    nc = seq // chunk
    w = MLSTM_WIDTH
    nh, dh = MLSTM_HEADS, MLSTM_HEAD_DIM
    seqs = 1
    a3 = seg_a.reshape(batch, seq, SEG_A)
    b3 = seg_b.reshape(batch, seq, SEG_B)

    def col_spec(off):
        return pl.BlockSpec((seqs, chunk, w), lambda b, c, o=off // w: (b, c, o))

    y = pl.pallas_call(
        _mlstm_kernel,
        grid=(batch // seqs, nc),
        in_specs=[
            pl.BlockSpec(memory_space=pltpu.SMEM),
            col_spec(A_QK), col_spec(A_QK + w), col_spec(A_V), col_spec(B_O),
            pl.BlockSpec((seqs, chunk, LANE), lambda b, c: (b, c, B_IF // LANE)),
            pl.BlockSpec((1, w), lambda b, c: (0, 0)),
            pl.BlockSpec((chunk, chunk), lambda b, c: (0, 0)),
        ],
        out_specs=pl.BlockSpec((seqs, chunk, w), lambda b, c: (b, c, 0)),
        out_shape=jax.ShapeDtypeStruct((batch, seq, w), BF16),
        scratch_shapes=[
            pltpu.VMEM((seqs, nh, dh, dh), F32),
            pltpu.VMEM((seqs, 8, dh), F32),
            pltpu.VMEM((seqs, 8, LANE), F32),
        ],
        compiler_params=pltpu.CompilerParams(
            dimension_semantics=("parallel", "arbitrary"), vmem_limit_bytes=VMEM_LIMIT),
        name="mlstm",
    )(bias, a3, a3, a3, b3, b3, norm_g, tri)
    return y.reshape(batch * seq, w)


def _outproj_kernel(x_ref, ya_ref, ym_ref, wa_ref, wm_ref, o_ref):
    o_ref[...] = x_ref[...] + _dot(ya_ref[...], wa_ref[...]) + _dot(ym_ref[...], wm_ref[...])


def _outproj(x2, ya, ym, wa, wm, tm):
    n, d = x2.shape
    return pl.pallas_call(
        _outproj_kernel,
        grid=(n // tm,),
        in_specs=[
            pl.BlockSpec((tm, d), lambda i: (i, 0)),
            pl.BlockSpec((tm, ya.shape[1]), lambda i: (i, 0)),
            pl.BlockSpec((tm, ym.shape[1]), lambda i: (i, 0)),
            pl.BlockSpec(wa.shape, lambda i: (0, 0)),
            pl.BlockSpec(wm.shape, lambda i: (0, 0)),
        ],
        out_specs=pl.BlockSpec((tm, d), lambda i: (i, 0)),
        out_shape=jax.ShapeDtypeStruct((n, d), F32),
        compiler_params=pltpu.CompilerParams(
            dimension_semantics=("parallel",), vmem_limit_bytes=VMEM_LIMIT),
        name="outproj",
    )(x2, ya, ym, wa, wm)


def _mlp_kernel(x_ref, g_ref, w1_ref, w2_ref, gf_ref, o_ref, h_ref):
    f = pl.program_id(1)

    @pl.when(f == 0)
    def _():
        x = x_ref[...]
        r = lax.rsqrt(jnp.mean(x * x, axis=-1, keepdims=True) + EPS)
        h_ref[...] = (x * r * g_ref[...]).astype(BF16)
        o_ref[...] = x

    u = jnp.maximum(_dot(h_ref[...], w1_ref[...]), 0.0)
    o_ref[...] += _dot((u * u).astype(BF16), w2_ref[...])

    @pl.when(f == pl.num_programs(1) - 1)
    def _():
        x2 = o_ref[...]
        r = lax.rsqrt(jnp.mean(x2 * x2, axis=-1, keepdims=True) + EPS)
        o_ref[...] = x2 * r * gf_ref[...]


def _mlp(x1, g, w1, w2, gf, tm, tf):
    n, d = x1.shape
    dff = w1.shape[1]
    return pl.pallas_call(
        _mlp_kernel,
        grid=(n // tm, dff // tf),
        in_specs=[
            pl.BlockSpec((tm, d), lambda i, f: (i, 0)),
            pl.BlockSpec((1, d), lambda i, f: (0, 0)),
            pl.BlockSpec((d, tf), lambda i, f: (0, f)),
            pl.BlockSpec((tf, d), lambda i, f: (f, 0)),
            pl.BlockSpec((1, d), lambda i, f: (0, 0)),
        ],
        out_specs=pl.BlockSpec((tm, d), lambda i, f: (i, 0)),
        out_shape=jax.ShapeDtypeStruct((n, d), F32),
        scratch_shapes=[pltpu.VMEM((tm, d), BF16)],
        compiler_params=pltpu.CompilerParams(
            dimension_semantics=("parallel", "arbitrary"), vmem_limit_bytes=VMEM_LIMIT),
        name="mlp",
    )(x1, g, w1, w2, gf)


def _row_tile(n, want):
    t = want
    while n % t:
        t //= 2
    return t


def _layer(x2, batch, seq, norm_mix_g, w_in, w_cmp_k1, w_cmp_k2, pos_cmp_k, w_cmp_v1, w_cmp_v2,
           pos_cmp_v, conv_w, conv_b, b_igate, b_fgate, mlstm_norm_g, w_out, norm_mlp_g,
           w_mlp_in, w_mlp_out):
    n, d = x2.shape
    assert seq % Q_BLOCK == 0 and seq >= WINDOW + Q_BLOCK and seq // SEL_BLOCK <= SEL_LANES
    g_ = NSA_KV_GROUPS
    nh = MLSTM_HEADS

    c_gate = NSA_WIDTH + 6 * NSA_KV_WIDTH
    c_qk = c_gate + NSA_HEADS * N_BRANCH
    c_v = c_qk + 2 * MLSTM_WIDTH
    c_o = c_v + MLSTM_WIDTH
    c_i = c_o + MLSTM_WIDTH
    c_f = c_i + nh
    w16 = w_in.astype(BF16)
    gate_cols = []
    per_g = NSA_REP * N_BRANCH
    for g in range(g_):
        gate_cols += [w16[:, c_gate + g * per_g:c_gate + (g + 1) * per_g],
                      jnp.zeros((d, LANE - per_g), BF16)]
    w_ab = jnp.concatenate(
        [w16[:, c_v:c_o], w16[:, c_qk:c_v], w16[:, :c_gate],
         w16[:, c_o:c_i]] + gate_cols
        + [w16[:, c_i:c_f + nh], jnp.zeros((d, SEG_B - B_IF - 2 * nh), BF16)],
        axis=1)
    scale = jnp.concatenate([jnp.ones((1, A_QK + MLSTM_WIDTH), F32),
                             jnp.full((1, MLSTM_WIDTH), MLSTM_HEAD_DIM ** -0.5, F32),
                             jnp.full((1, NSA_WIDTH), NSA_HEAD_DIM ** -0.5, F32),
                             jnp.ones((1, SEG_A - A_KV + SEG_B), F32)], axis=1)
    conv_pad = ((0, 0), (A_QK, SEG_A - A_Q + SEG_B))
    cw_ab = jnp.pad(conv_w, conv_pad)
    cb_ab = jnp.pad(conv_b.reshape(1, -1), conv_pad)
    g_mix = norm_mix_g.reshape(1, d)

    tm = _row_tile(seq, ROW_TILE)
    seg_a, seg_b = _inproj(x2, g_mix, w_ab, scale, cw_ab, cb_ab, tm, COL_TILE, (A_QK, A_Q), seq)

    n_sub = seq // CMP_STRIDE
    dh = NSA_HEAD_DIM
    w1s = jnp.stack([w_cmp_k1, w_cmp_v1]).reshape(2, 2, CMP_STRIDE, dh, dh)
    w1r = jnp.concatenate([w1s[:, 0], w1s[:, 1]], axis=-1).astype(BF16)
    w2s = jnp.stack([w_cmp_k2, w_cmp_v2]).astype(BF16)
    poss = jnp.stack([pos_cmp_k, pos_cmp_v]).reshape(2, 2, CMP_STRIDE, dh).transpose(0, 2, 1, 3)
    posr = jnp.pad(poss, ((0, 0), (0, 0), (0, 6), (0, 0))).astype(BF16)
    kvc, kvct = _compress(seg_a, w1r, w2s, posr, batch, seq)

    cmp_start = np.arange(n_sub) * CMP_STRIDE
    sel_start = np.arange(SEL_LANES) * SEL_BLOCK
    ovt = ((cmp_start[None, :] < sel_start[:, None] + SEL_BLOCK)
           & (cmp_start[None, :] + CMP_BLOCK - 1 >= sel_start[:, None])
           & (np.arange(n_sub)[None, :] < n_sub - CMP_BLOCK // CMP_STRIDE + 1))
    ovt = jnp.asarray(ovt, BF16)
    slopes = jnp.exp2(-8.0 * jnp.arange(1, NSA_HEADS + 1, dtype=F32) / NSA_HEADS)
    ocmp, pen, flags = _nsa_cmp(seg_a, kvc, kvct, ovt, slopes, batch, seq)
    onehot, wext, dbias, ubias, wbias = _nsa_tables(slopes, seq)
    y_a = _nsa_attn(seg_a, seg_b, pen, ocmp, flags, onehot, wext, dbias, ubias, wbias,
                    slopes, batch, seq)

    chunk = 256 if seq % 256 == 0 else 128
    bias = jnp.concatenate([b_igate, b_fgate]).astype(F32)
    tri = jnp.asarray(np.tril(np.ones((chunk, chunk), np.float32)), BF16)
    y_m = _mlstm(seg_a, seg_b, bias, mlstm_norm_g.reshape(1, -1), tri, batch, seq, chunk)

    w_o = w_out.astype(BF16)
    x1 = _outproj(x2, y_a, y_m, w_o[:NSA_WIDTH], w_o[NSA_WIDTH:], _row_tile(n, 512))
    return x1, (norm_mlp_g.reshape(1, d), w_mlp_in.astype(BF16), w_mlp_out.astype(BF16))


def kernel(x, norm_mix_g, w_in, w_cmp_k1, w_cmp_k2, pos_cmp_k, w_cmp_v1, w_cmp_v2, pos_cmp_v, conv_w, conv_b, b_igate, b_fgate, mlstm_norm_g, w_out, norm_mlp_g, w_mlp_in, w_mlp_out, norm_f_g):
    batch, seq, d = x.shape
    depth = w_in.shape[0]
    assert depth == 1, "the final RMSNorm is fused into the last layer's channel mixer"
    x2 = x.reshape(batch * seq, d)
    tm = _row_tile(batch * seq, ROW_TILE)
    for l in range(depth):
        x1, (g_mlp, w1, w2) = _layer(
            x2, batch, seq, norm_mix_g[l], w_in[l], w_cmp_k1[l], w_cmp_k2[l], pos_cmp_k[l],
            w_cmp_v1[l], w_cmp_v2[l], pos_cmp_v[l], conv_w[l], conv_b[l], b_igate[l], b_fgate[l],
            mlstm_norm_g[l], w_out[l], norm_mlp_g[l], w_mlp_in[l], w_mlp_out[l])
        x2 = _mlp(x1, g_mlp, w1, w2, norm_f_g.reshape(1, d), tm, 512)
    return x2.reshape(batch, seq, d)
```

```python
import functools

import numpy as np
import jax
import jax.numpy as jnp
from jax import lax
from jax.experimental import pallas as pl
from jax.experimental.pallas import tpu as pltpu

F32 = jnp.float32
BF16 = jnp.bfloat16

EPS = 1e-6
NEG = -1e30
FORCE_BONUS = 1e4
PICKED = -3e38
MASK_BIG = 1e30

D_MODEL = 2048
NSA_HEAD_DIM = 128
NSA_WIDTH = D_MODEL // 2
NSA_HEADS = NSA_WIDTH // NSA_HEAD_DIM
NSA_REP = 4
NSA_KV_GROUPS = NSA_HEADS // NSA_REP
NSA_KV_WIDTH = NSA_KV_GROUPS * NSA_HEAD_DIM
CMP_BLOCK = 32
CMP_STRIDE = 16
SEL_BLOCK = 64
SEL_TOPK = 16
WINDOW = 512
Q_BLOCK = 128
N_BRANCH = 3
MLSTM_HEAD_DIM = 256
MLSTM_WIDTH = D_MODEL - NSA_WIDTH
MLSTM_HEADS = MLSTM_WIDTH // MLSTM_HEAD_DIM
CONV_WIDTH = 4

LANE = 128
SEL_LANES = 128
KEY_TILE = 128
SEL_TILES_PER_STEP = 2
WIN_TILES = WINDOW // KEY_TILE
SUM_ROWS = 16
VMEM_LIMIT = 56 * 1024 * 1024
ROW_TILE = 1024

A_V, A_QK = 0, MLSTM_WIDTH
A_Q = A_QK + 2 * MLSTM_WIDTH
A_KV = A_Q + NSA_WIDTH
SEG_A = A_KV + 6 * NSA_KV_WIDTH
B_O = 0
B_GATE = MLSTM_WIDTH
B_IF = B_GATE + NSA_KV_GROUPS * LANE
SEG_B = B_IF + 2 * LANE
COL_TILE = 512

WCOL_PAD, WCOL_HI, WCOL_LO, WCOL_QHI, WCOL_QLO = 0, 1, 2, 3, 4


def _dot(a, b):
    return jnp.dot(a, b, preferred_element_type=F32)


def _dot_nt(a, b):
    return lax.dot_general(a, b, (((1,), (1,)), ((), ())), preferred_element_type=F32)


def _sigmoid(x):
    return 1.0 / (1.0 + jnp.exp(-x))


def _shifted(x, tail, s):
    xs = pltpu.roll(x, s, axis=0)
    ts = pltpu.roll(tail, s, axis=0)
    row8 = lax.broadcasted_iota(jnp.int32, (8, 1), 0)
    head = jnp.where(row8 < s, ts, xs[:8])
    return jnp.concatenate([head, xs[8:]], axis=0)


def _conv_silu(x, tail, w, b):
    y = b + _shifted(x, tail, CONV_WIDTH - 1) * w[0:1]
    for i in range(1, CONV_WIDTH - 1):
        y = y + _shifted(x, tail, CONV_WIDTH - 1 - i) * w[i:i + 1]
    y = y + x * w[CONV_WIDTH - 1:CONV_WIDTH]
    return y * _sigmoid(y)


def _inproj_kernel(x_ref, g_ref, w_ref, cs_ref, cw_ref, cb_ref, oa_ref, ob_ref, h_ref, halo_ref, *,
                   na, conv_lo, conv_hi, tiles_per_seq):
    i = pl.program_id(0)
    j = pl.program_id(1)

    @pl.when(j == 0)
    def _():
        x = x_ref[...]
        r = lax.rsqrt(jnp.mean(x * x, axis=-1, keepdims=True) + EPS)
        h_ref[...] = (x * r * g_ref[...]).astype(BF16)

    is_conv = (j >= conv_lo) & (j < conv_hi)

    @pl.when((j < na) & jnp.logical_not(is_conv))
    def _():
        oa_ref[...] = (_dot(h_ref[...], w_ref[...]) * cs_ref[...]).astype(oa_ref.dtype)

    @pl.when(j >= na)
    def _():
        ob_ref[...] = (_dot(h_ref[...], w_ref[...]) * cs_ref[...]).astype(ob_ref.dtype)

    @pl.when(is_conv)
    def _():
        slot = j - conv_lo

        @pl.when(i % tiles_per_seq == 0)
        def _():
            halo_ref[slot] = jnp.zeros(halo_ref.shape[1:], F32)

        acc = _dot(h_ref[...], w_ref[...])
        tail = halo_ref[slot]
        halo_ref[slot] = acc[acc.shape[0] - 8:]
        y = _conv_silu(acc, tail, cw_ref[...], cb_ref[...])
        oa_ref[...] = (y * cs_ref[...]).astype(oa_ref.dtype)


def _inproj(x2, g, w, cs, cw, cb, tm, tn, conv_cols, seq):
    n, d = x2.shape
    na, nb = SEG_A // tn, SEG_B // tn
    conv_lo, conv_hi = conv_cols[0] // tn, conv_cols[1] // tn
    assert conv_cols[0] % tn == 0 and conv_cols[1] % tn == 0 and seq % tm == 0 and conv_hi <= na
    assert SEG_A % tn == 0 and SEG_B % tn == 0 and w.shape[1] == SEG_A + SEG_B
    return pl.pallas_call(
        functools.partial(_inproj_kernel, na=na, conv_lo=conv_lo, conv_hi=conv_hi,
                          tiles_per_seq=seq // tm),
        grid=(n // tm, na + nb),
        in_specs=[
            pl.BlockSpec((tm, d), lambda i, j: (i, 0)),
            pl.BlockSpec((1, d), lambda i, j: (0, 0)),
            pl.BlockSpec((d, tn), lambda i, j: (0, j)),
            pl.BlockSpec((1, tn), lambda i, j: (0, j)),
            pl.BlockSpec((CONV_WIDTH, tn), lambda i, j: (0, j)),
            pl.BlockSpec((1, tn), lambda i, j: (0, j)),
        ],
        out_specs=[pl.BlockSpec((tm, tn), lambda i, j: (i, jnp.minimum(j, na - 1))),
                   pl.BlockSpec((tm, tn), lambda i, j: (i, jnp.maximum(j - na, 0)))],
        out_shape=[jax.ShapeDtypeStruct((n, SEG_A), BF16), jax.ShapeDtypeStruct((n, SEG_B), F32)],
        scratch_shapes=[pltpu.VMEM((tm, d), BF16),
                        pltpu.VMEM((conv_hi - conv_lo, 8, tn), F32)],
        compiler_params=pltpu.CompilerParams(
            dimension_semantics=("arbitrary", "arbitrary"), vmem_limit_bytes=VMEM_LIMIT),
        name="inproj",
    )(x2, g, w, cs, cw, cb)


def _compress_kernel(x_ref, w1_ref, w2_ref, pos_ref, o_ref, ot_ref, xf_ref):
    n_sub = o_ref.shape[2]
    dh = NSA_HEAD_DIM
    xf_ref[...] = x_ref[...].astype(F32)
    acc = jnp.zeros((n_sub, 2 * dh), F32)
    posw = jnp.zeros((1, dh), F32)
    for p in range(CMP_STRIDE):
        wp = w1_ref[0, p]
        acc = acc + _dot(xf_ref[pl.ds(p, n_sub, stride=CMP_STRIDE), :].astype(BF16), wp)
        pw = _dot(pos_ref[0, p], wp)
        posw = posw + pw[0:1, :dh] + pw[1:2, dh:]
    bot = pltpu.roll(acc[:, dh:], n_sub - 1, axis=0)
    pre = acc[:, :dh] + bot + posw
    hid = pre * _sigmoid(pre)
    out = _dot(hid.astype(BF16), w2_ref[0])
    o_ref[0, 0] = out.astype(o_ref.dtype)
    ot_ref[0, 0] = out.T.astype(ot_ref.dtype)


def _compress(seg_a, w1r, w2s, posr, batch, seq):
    g = NSA_KV_GROUPS
    c = 2 * g
    dh = NSA_HEAD_DIM
    n_sub = seq // CMP_STRIDE
    return pl.pallas_call(
        _compress_kernel,
        grid=(batch, c),
        in_specs=[
            pl.BlockSpec((seq, dh), lambda i, j: (i, A_KV // dh + j)),
            pl.BlockSpec((1, CMP_STRIDE, dh, 2 * dh), lambda i, j: (j // g, 0, 0, 0)),
            pl.BlockSpec((1, dh, dh), lambda i, j: (j // g, 0, 0)),
            pl.BlockSpec((1, CMP_STRIDE, 8, dh), lambda i, j: (j // g, 0, 0, 0)),
        ],
        out_specs=[pl.BlockSpec((1, 1, n_sub, dh), lambda i, j: (i, j, 0, 0)),
                   pl.BlockSpec((1, 1, dh, n_sub), lambda i, j: (i, j, 0, 0))],
        out_shape=[jax.ShapeDtypeStruct((batch, c, n_sub, dh), BF16),
                   jax.ShapeDtypeStruct((batch, c, dh, n_sub), BF16)],
        scratch_shapes=[pltpu.VMEM((seq, dh), F32)],
        compiler_params=pltpu.CompilerParams(
            dimension_semantics=("parallel", "parallel"), vmem_limit_bytes=VMEM_LIMIT),
        name="compress",
    )(seg_a, w1r, w2s, posr)


def _stack_heads(q_all):
    dh = NSA_HEAD_DIM
    return jnp.concatenate([q_all[:, r * dh:(r + 1) * dh] for r in range(NSA_REP)], axis=0)


def _nsa_cmp_kernel(slopes_ref, q_ref, kc_ref, vct_ref, ovt_ref, ocmp_ref, pen_ref, flag_ref, *,
                    seq, sub_blocks):
    step = pl.program_id(2)
    nq_step = sub_blocks * Q_BLOCK
    chunk = min(SEL_LANES, kc_ref.shape[2])
    n_chunks = kc_ref.shape[2] // chunk
    need = ((step + 1) * nq_step - CMP_BLOCK) // CMP_STRIDE + 1
    n_need = (need + chunk - 1) // chunk
    for v in range(1, n_chunks + 1):
        cond = (n_need == v) if v < n_chunks else (n_need >= v)
        if v == 1:
            cond = n_need <= 1

        @pl.when(cond)
        def _(v=v):
            for sub in range(sub_blocks):
                rows = slice(sub * Q_BLOCK, (sub + 1) * Q_BLOCK)
                _nsa_cmp_block(slopes_ref, q_ref[rows, :], kc_ref, vct_ref, ovt_ref,
                               ocmp_ref.at[0, 0, sub], pen_ref.at[0, 0, rows], flag_ref.at[0, 0, sub],
                               step * sub_blocks + sub, seq, v * chunk)


def _nsa_cmp_block(slopes_ref, q_all, kc_ref, vct_ref, ovt_ref, ocmp_ref, pen_ref, flag_ref, qb, seq,
                   n_use):
    g = pl.program_id(1)
    nq = Q_BLOCK
    n_cpad = n_use
    n_cmp = seq // CMP_STRIDE - CMP_BLOCK // CMP_STRIDE + 1
    t0 = qb * nq

    s_t = _dot_nt(kc_ref[0, 0, :n_use, :], _stack_heads(q_all))
    n_s = lax.broadcasted_iota(jnp.int32, (n_cpad, 1), 0)
    q_l = lax.broadcasted_iota(jnp.int32, (1, nq), 1)
    dist = (t0 - (CMP_BLOCK - 1)) + q_l - n_s * CMP_STRIDE
    valid = (dist >= 0) & (n_s < n_cmp)
    dist_f = dist.astype(F32)
    probs = []
    p_sum = jnp.zeros((n_cpad, nq), F32)
    for r in range(NSA_REP):
        slope = slopes_ref[g * NSA_REP + r]
        s = jnp.where(valid, s_t[:, r * nq:(r + 1) * nq] - slope * dist_f, NEG)
        m = jnp.max(s, axis=0, keepdims=True)
        e = jnp.exp(s - m)
        inv = jnp.where(m > 0.5 * NEG, 1.0 / jnp.sum(e, axis=0, keepdims=True), 0.0)
        p = e * inv
        probs.append(p.astype(BF16))
        p_sum = p_sum + p
    ocmp_ref[...] = _dot(vct_ref[0, 0, :, :n_use], jnp.concatenate(probs, axis=1))

    p_hi = p_sum.astype(BF16)
    p_lo = (p_sum - p_hi.astype(F32)).astype(BF16)
    ovt = ovt_ref[:, :n_use]
    imp = _dot(ovt, p_hi) + _dot(ovt, p_lo)
    j_i = lax.broadcasted_iota(jnp.int32, (SEL_LANES, 1), 0)
    t_l = t0 + q_l
    cur = t_l // SEL_BLOCK
    forced = (j_i == 0) | (j_i == cur) | (j_i == cur - 1)
    causal_blk = j_i * SEL_BLOCK <= t_l
    val = jnp.where(causal_blk, jnp.where(forced, imp + FORCE_BONUS, imp), NEG)
    j_f = j_i.astype(F32)
    sel_t = jnp.zeros((SEL_LANES, nq), F32)
    for _ in range(min(SEL_TOPK, seq // SEL_BLOCK)):
        mx = jnp.max(val, axis=0, keepdims=True)
        first = jnp.min(jnp.where(val == mx, j_f, float(SEL_LANES)), axis=0, keepdims=True)
        pick = j_f == first
        sel_t = jnp.where(pick, 1.0, sel_t)
        val = jnp.where(pick, PICKED, val)
    sel = sel_t.T
    pen_ref[...] = ((sel - 1.0) * MASK_BIG).astype(pen_ref.dtype)
    flag_ref[...] = (jnp.max(sel, axis=0, keepdims=True) > 0.0).astype(jnp.int32)


def _nsa_cmp(seg_a, kvc, kvct, ovt, slopes, batch, seq):
    nqb = seq // Q_BLOCK
    gq = NSA_REP * NSA_HEAD_DIM
    g_ = NSA_KV_GROUPS
    n_cpad = kvc.shape[2]
    sub = _nsa_sub_blocks(seq)
    nstep = nqb // sub
    return pl.pallas_call(
        functools.partial(_nsa_cmp_kernel, seq=seq, sub_blocks=sub),
        grid=(batch, g_, nstep),
        in_specs=[
            pl.BlockSpec(memory_space=pltpu.SMEM),
            pl.BlockSpec((sub * Q_BLOCK, gq), lambda b, g, q: (b * nstep + q, A_Q // gq + g)),
            pl.BlockSpec((1, 1, n_cpad, NSA_HEAD_DIM), lambda b, g, q: (b, g, 0, 0)),
            pl.BlockSpec((1, 1, NSA_HEAD_DIM, n_cpad), lambda b, g, q: (b, g_ + g, 0, 0)),
            pl.BlockSpec((SEL_LANES, n_cpad), lambda b, g, q: (0, 0)),
        ],
        out_specs=[
            pl.BlockSpec((1, 1, sub, NSA_HEAD_DIM, gq), lambda b, g, q: (b, g, q, 0, 0)),
            pl.BlockSpec((1, 1, sub * Q_BLOCK, SEL_LANES), lambda b, g, q: (b, g, q, 0)),
            pl.BlockSpec((1, 1, sub, 1, SEL_LANES), lambda b, g, q: (b, g, q, 0, 0)),
        ],
        out_shape=[
            jax.ShapeDtypeStruct((batch, g_, nqb, NSA_HEAD_DIM, gq), F32),
            jax.ShapeDtypeStruct((batch, g_, seq, SEL_LANES), BF16),
            jax.ShapeDtypeStruct((batch, g_, nqb, 1, SEL_LANES), jnp.int32),
        ],
        compiler_params=pltpu.CompilerParams(
            dimension_semantics=("parallel", "parallel", "parallel"), vmem_limit_bytes=VMEM_LIMIT),
        name="nsa_cmp",
    )(slopes, seg_a, kvc, kvct, ovt)


def _nsa_attn_kernel(slopes_ref, flag_ref, q_ref, gate_ref, pen_ref, ocmp_ref, ks_ref, vs_ref,
                     kw_ref, vw_ref, onehot_ref, wext_ref, dbias_ref, ubias_ref, wbias_ref, o_ref,
                     ksel_ref, vselt_ref, kwin_ref, vwint_ref,
                     qa_ref, qw_ref, list_ref, m_ref, acc_ref, owin_ref, sa_ref, sb_ref,
                     *, seq, sub_blocks):
    g = pl.program_id(1)
    step = pl.program_id(2)
    dh = NSA_HEAD_DIM
    nq = sub_blocks * Q_BLOCK
    kt = KEY_TILE
    per = SEL_TILES_PER_STEP
    pad_tile = seq // kt
    t0 = step * nq
    tile0 = step * sub_blocks

    @pl.when(step == 0)
    def _():
        ksel_ref[0:seq, :dh] = ks_ref[...]
        ksel_ref[0:seq, dh:] = onehot_ref[...]
        ksel_ref[seq:, :dh] = jnp.zeros((kt, dh), BF16)
        ksel_ref[seq:, dh:] = jnp.ones((kt, SEL_LANES), BF16)
        lane2 = lax.broadcasted_iota(jnp.int32, (WINDOW, dh + LANE), 1)
        kwin_ref[0:WINDOW, :] = jnp.where(lane2 == dh + WCOL_PAD, 1.0, 0.0).astype(BF16)
        kwin_ref[WINDOW:, :dh] = kw_ref[...]
        kwin_ref[WINDOW:, dh:] = wext_ref[...]
        zero_tile = jnp.zeros((dh + SUM_ROWS, kt), BF16)
        ones_rows = jnp.ones((SUM_ROWS, kt), BF16)
        vselt_ref[pad_tile] = zero_tile
        for i in range(WIN_TILES):
            vwint_ref[i] = zero_tile

        def transpose_tile(t, carry):
            r0 = pl.multiple_of(t * kt, kt)
            vselt_ref[t, :dh] = vs_ref[pl.ds(r0, kt), :].astype(F32).T.astype(BF16)
            vselt_ref[t, dh:] = ones_rows
            vwint_ref[t + WIN_TILES, :dh] = vw_ref[pl.ds(r0, kt), :].astype(F32).T.astype(BF16)
            vwint_ref[t + WIN_TILES, dh:] = ones_rows
            return carry

        lax.fori_loop(0, seq // kt, transpose_tile, 0)

    slopes = [slopes_ref[g * NSA_REP + r] for r in range(NSA_REP)]
    row_q = lax.broadcasted_iota(jnp.int32, (nq, 1), 0)
    lane = lax.broadcasted_iota(jnp.int32, (1, LANE), 1)
    q_all = q_ref[...]

    pen = pen_ref[0, 0].astype(F32)
    blk_rel = ((lane - (t0 + row_q) // SEL_BLOCK) * SEL_BLOCK).astype(F32)
    for r in range(NSA_REP):
        qa_ref[r * nq:(r + 1) * nq, :dh] = q_all[:, r * dh:(r + 1) * dh]
        qa_ref[r * nq:(r + 1) * nq, dh:] = (pen + slopes[r] * blk_rel).astype(BF16)

    def scan(k, cnt):
        for j in range(2):
            i = 2 * k + j
            hits = sum(flag_ref[0, 0, a, 0, 2 * i] + flag_ref[0, 0, a, 0, 2 * i + 1]
                       for a in range(sub_blocks))
            act = (hits > 0) & (i < tile0)
            list_ref[cnt] = i
            cnt = cnt + act.astype(jnp.int32)
        return cnt

    cnt = lax.fori_loop(0, (tile0 + 1) // 2, scan, 0)
    for i in range(4 * per):
        list_ref[cnt + i] = pad_tile

    def group_scores(it):
        keys = jnp.concatenate(
            [ksel_ref[pl.ds(pl.multiple_of(list_ref[it * per + i] * kt, kt), kt), :]
             for i in range(per)], axis=0)
        return _dot_nt(keys, qa_ref[...]) + ubias_ref[0]

    s = _dot_nt(ksel_ref[pl.ds(pl.multiple_of(t0, kt), nq), :], qa_ref[...]) + dbias_ref[0]
    m0 = jnp.max(s, axis=0, keepdims=True)
    p = jnp.exp(s - m0)
    m_ref[...] = m0
    vals = jnp.concatenate([vselt_ref[tile0 + a] for a in range(sub_blocks)], axis=1)
    acc_ref[...] = _dot(vals, p.astype(BF16))
    sa_ref[...] = group_scores(0)

    tp = t0 + WINDOW + row_q
    t_hi = (tp // SEL_BLOCK).astype(F32)
    t_lo = (tp % SEL_BLOCK).astype(F32)
    for r in range(NSA_REP):
        sl = slopes[r]
        ext = jnp.where(lane == WCOL_PAD, -MASK_BIG, 0.0)
        ext = jnp.where(lane == WCOL_HI, sl * SEL_BLOCK, ext)
        ext = jnp.where(lane == WCOL_LO, sl, ext)
        ext = jnp.where(lane == WCOL_QHI, -sl * SEL_BLOCK * t_hi, ext)
        ext = jnp.where(lane == WCOL_QLO, -sl * t_lo, ext)
        qw_ref[r * nq:(r + 1) * nq, :dh] = q_all[:, r * dh:(r + 1) * dh]
        qw_ref[r * nq:(r + 1) * nq, dh:] = ext.astype(BF16)
    wlen = WINDOW + nq
    s = _dot_nt(kwin_ref[pl.ds(pl.multiple_of(t0, kt), wlen), :], qw_ref[...]) + wbias_ref[...]
    e = jnp.exp(s - jnp.max(s, axis=0, keepdims=True))
    vwin = jnp.concatenate([vwint_ref[tile0 + i] for i in range(WIN_TILES + sub_blocks)], axis=1)
    ow = _dot(vwin, e.astype(BF16))
    owin_ref[...] = ow[:dh] / ow[dh:dh + 1]

    def absorb(s, grp):
        vals = jnp.concatenate([vselt_ref[list_ref[grp * per + i]] for i in range(per)], axis=1)
        m_old = m_ref[...]
        m_new = jnp.maximum(m_old, jnp.max(s, axis=0, keepdims=True))
        alpha = jnp.exp(m_old - m_new)
        p = jnp.exp(s - m_new)
        acc_ref[...] = alpha * acc_ref[...] + _dot(vals, p.astype(BF16))
        m_ref[...] = m_new

    def sel_body(it, carry):
        s = sa_ref[...]
        sb_ref[...] = group_scores(2 * it + 1)
        absorb(s, 2 * it)
        s = sb_ref[...]
        sa_ref[...] = group_scores(2 * it + 2)
        absorb(s, 2 * it + 1)
        return carry

    lax.fori_loop(0, (cnt + 2 * per - 1) // (2 * per), sel_body, 0)
    o_sel = acc_ref[:dh, :] / acc_ref[dh:dh + 1, :]
    o_win = owin_ref[...]

    gate_t = _sigmoid(gate_ref[...]).T
    for a in range(sub_blocks):
        o_cmp = ocmp_ref[0, 0, a]
        qs = slice(a * Q_BLOCK, (a + 1) * Q_BLOCK)
        for r in range(NSA_REP):
            cols = slice(r * nq + a * Q_BLOCK, r * nq + (a + 1) * Q_BLOCK)
            c0 = N_BRANCH * r
            out_t = (gate_t[c0:c0 + 1, qs] * o_cmp[:, r * Q_BLOCK:(r + 1) * Q_BLOCK]
                     + gate_t[c0 + 1:c0 + 2, qs] * o_sel[:, cols]
                     + gate_t[c0 + 2:c0 + 3, qs] * o_win[:, cols])
            o_ref[qs, r * dh:(r + 1) * dh] = out_t.T.astype(o_ref.dtype)


def _nsa_sub_blocks(seq):
    return 2 if (seq // Q_BLOCK) % 2 == 0 else 1


def _nsa_attn(seg_a, seg_b, pen, ocmp, flags, onehot, wext, dbias, ubias, wbias, slopes, batch, seq):
    n = batch * seq
    sub = _nsa_sub_blocks(seq)
    nq = sub * Q_BLOCK
    nqb = seq // nq
    gq = NSA_REP * NSA_HEAD_DIM
    g_ = NSA_KV_GROUPS
    dh, kt = NSA_HEAD_DIM, KEY_TILE
    dk = dh + SEL_LANES
    rq = NSA_REP * nq
    n_t = seq // kt

    def kv_spec(kind):
        return pl.BlockSpec((seq, dh), lambda b, g, q, k=kind: (b, A_KV // dh + k * g_ + g))

    def const_spec(arr):
        return pl.BlockSpec(arr.shape, lambda b, g, q, nd=arr.ndim: (0,) * nd)

    return pl.pallas_call(
        functools.partial(_nsa_attn_kernel, seq=seq, sub_blocks=sub),
        grid=(batch, g_, nqb),
        in_specs=[
            pl.BlockSpec(memory_space=pltpu.SMEM),
            pl.BlockSpec((1, 1, sub, 1, SEL_LANES), lambda b, g, q: (b, g, q, 0, 0),
                         memory_space=pltpu.SMEM),
            pl.BlockSpec((nq, gq), lambda b, g, q: (b * nqb + q, A_Q // gq + g)),
            pl.BlockSpec((nq, LANE), lambda b, g, q: (b * nqb + q, B_GATE // LANE + g)),
            pl.BlockSpec((1, 1, nq, SEL_LANES), lambda b, g, q: (b, g, q, 0)),
            pl.BlockSpec((1, 1, sub, NSA_HEAD_DIM, gq), lambda b, g, q: (b, g, q, 0, 0)),
            kv_spec(2), kv_spec(3), kv_spec(4), kv_spec(5),
            const_spec(onehot), const_spec(wext),
            pl.BlockSpec((1,) + dbias.shape[1:], lambda b, g, q: (g, 0, 0)),
            pl.BlockSpec((1,) + ubias.shape[1:], lambda b, g, q: (g, 0, 0)),
            const_spec(wbias),
        ],
        out_specs=pl.BlockSpec((nq, gq), lambda b, g, q: (b * nqb + q, g)),
        out_shape=jax.ShapeDtypeStruct((n, NSA_WIDTH), BF16),
        scratch_shapes=[
            pltpu.VMEM((seq + kt, dk), BF16),
            pltpu.VMEM((n_t + 1, dh + SUM_ROWS, kt), BF16),
            pltpu.VMEM((seq + WINDOW, dh + LANE), BF16),
            pltpu.VMEM((n_t + WIN_TILES, dh + SUM_ROWS, kt), BF16),
            pltpu.VMEM((rq, dk), BF16),
            pltpu.VMEM((rq, dk), BF16),
            pltpu.SMEM((n_t + 4 * SEL_TILES_PER_STEP,), jnp.int32),
            pltpu.VMEM((1, rq), F32),
            pltpu.VMEM((dh + SUM_ROWS, rq), F32),
            pltpu.VMEM((dh, rq), F32),
            pltpu.VMEM((SEL_TILES_PER_STEP * KEY_TILE, rq), F32),
            pltpu.VMEM((SEL_TILES_PER_STEP * KEY_TILE, rq), F32),
        ],
        compiler_params=pltpu.CompilerParams(
            dimension_semantics=("parallel", "parallel", "arbitrary"), vmem_limit_bytes=VMEM_LIMIT),
        name="nsa_attn",
    )(slopes, flags, seg_a, seg_b, pen, ocmp, seg_a, seg_a, seg_a, seg_a, onehot, wext,
      dbias, ubias, wbias)


def _nsa_tables(slopes, seq):
    g_, kt = NSA_KV_GROUPS, KEY_TILE
    pos = np.arange(seq)
    onehot = jnp.asarray(pos[:, None] // SEL_BLOCK == np.arange(SEL_LANES)[None, :], BF16)
    ext = np.zeros((seq, LANE), np.float32)
    ext[:, WCOL_HI] = (pos + WINDOW) // SEL_BLOCK
    ext[:, WCOL_LO] = (pos + WINDOW) % SEL_BLOCK
    ext[:, WCOL_QHI] = 1.0
    ext[:, WCOL_QLO] = 1.0
    wext = jnp.asarray(ext, BF16)

    nq = _nsa_sub_blocks(seq) * Q_BLOCK

    def alibi_in_block(rows):
        u = jnp.asarray((np.arange(rows) % SEL_BLOCK).astype(np.float32))[None, :, None, None]
        t = jnp.broadcast_to(slopes.reshape(g_, 1, NSA_REP, 1) * u, (g_, rows, NSA_REP, nq))
        return t.reshape(g_, rows, NSA_REP * nq)

    ubias = alibi_in_block(SEL_TILES_PER_STEP * kt)
    kq = np.arange(nq)[:, None] <= np.arange(nq)[None, :]
    causal = np.tile(np.where(kq, 0.0, NEG).astype(np.float32), (1, NSA_REP))
    dbias = alibi_in_block(nq) + jnp.asarray(causal)[None]
    ki = np.arange(WINDOW + nq)[:, None]
    qi = np.arange(nq)[None, :]
    band = np.where((ki > qi) & (ki <= qi + WINDOW), 0.0, NEG).astype(np.float32)
    wbias = jnp.asarray(np.tile(band, (1, NSA_REP)))
    return onehot, wext, dbias, ubias, wbias


def _log_sigmoid(x):
    return jnp.minimum(x, 0.0) - jnp.log(1.0 + jnp.exp(-jnp.abs(x)))


def _split3(x):
    hi = x.astype(BF16)
    r1 = x - hi.astype(F32)
    mid = r1.astype(BF16)
    lo = (r1 - mid.astype(F32)).astype(BF16)
    return hi, mid, lo


def _mlstm_kernel(bias_ref, q_ref, k_ref, v_ref, o_ref, ifc_ref, ng_ref,
                  tri_ref, y_ref, c_ref, n_ref, m_ref):
    ch = pl.program_id(1)

    @pl.when(ch == 0)
    def _():
        c_ref[...] = jnp.zeros(c_ref.shape, F32)
        n_ref[...] = jnp.zeros(n_ref.shape, F32)
        m_ref[...] = jnp.zeros(m_ref.shape, F32)

    for sq in range(q_ref.shape[0]):
        _mlstm_chunk(bias_ref, q_ref.at[sq], k_ref.at[sq], v_ref.at[sq], o_ref.at[sq], ifc_ref.at[sq],
                     ng_ref, tri_ref, y_ref.at[sq], c_ref.at[sq], n_ref.at[sq], m_ref.at[sq])


def _mlstm_chunk(bias_ref, q_ref, k_ref, v_ref, o_ref, ifc_ref, ng_ref, tri_ref, y_ref,
                 c_ref, n_ref, m_ref):
    nh, dh = MLSTM_HEADS, MLSTM_HEAD_DIM
    L = q_ref.shape[0]
    tri = tri_ref[...]
    lane8 = lax.broadcasted_iota(jnp.int32, (1, LANE), 1)
    bias_c = jnp.zeros((1, LANE), F32)
    for h in range(nh):
        bias_c = jnp.where(lane8 == h, bias_ref[h], bias_c)
        bias_c = jnp.where(lane8 == nh + h, bias_ref[nh + h], bias_c)
    pre_c = ifc_ref[...] + bias_c
    cum_c = sum(_dot(tri, part) for part in _split3(_log_sigmoid(pre_c)))
    pre_r = pre_c.T[:2 * nh]
    cum_r = sum(_dot_nt(part, tri) for part in _split3(_log_sigmoid(pre_r)))

    rr = lax.broadcasted_iota(jnp.int32, (L, 1), 0)
    cc = lax.broadcasted_iota(jnp.int32, (1, L), 1)
    causal = cc <= rr

    for h in range(nh):
        cols = slice(h * dh, (h + 1) * dh)
        qb = q_ref[:, cols]
        kb = k_ref[:, cols]
        vh = v_ref[:, cols]
        qh = qb.astype(F32)
        kh = kb.astype(F32)
        b_c = cum_c[:, nh + h:nh + h + 1]
        li_c = pre_c[:, h:h + 1]
        b_r = cum_r[nh + h:nh + h + 1, :]
        li_r = pre_r[h:h + 1, :]
        m_prev = m_ref[h:h + 1, 0:1]

        dmat = jnp.where(causal, b_c - b_r + li_r, NEG)
        a = b_c + m_prev
        m_j = jnp.maximum(a, jnp.max(dmat, axis=1, keepdims=True))
        w_intra = jnp.exp(dmat - m_j)
        w_inter = jnp.exp(a - m_j)
        sc = _dot_nt(qb, kb) * w_intra
        c_old = c_ref[h]
        n_old = n_ref[h:h + 1, :]
        num = w_inter * _dot(qb, c_old.astype(BF16)) + _dot(sc.astype(BF16), vh)
        den = (w_inter * jnp.sum(qh * n_old, axis=1, keepdims=True)
               + jnp.sum(sc, axis=1, keepdims=True))
        hid = num / jnp.maximum(jnp.abs(den), jnp.exp(-m_j))

        g_tot = b_r[:, L - 1:L]
        lw_c = g_tot - b_c + li_c
        lw_r = g_tot - b_r + li_r
        m_new = jnp.maximum(g_tot + m_prev, jnp.max(lw_r, axis=1, keepdims=True))
        decay = jnp.exp(g_tot + m_prev - m_new)
        kw = jnp.exp(lw_c - m_new) * kh
        c_ref[h] = decay * c_old + _dot(kw.T.astype(BF16), vh)
        n_ref[h:h + 1, :] = decay * n_old + jnp.sum(kw, axis=0, keepdims=True)
        m_ref[h:h + 1, :] = jnp.broadcast_to(m_new, (1, LANE))

        hn = hid * lax.rsqrt(jnp.mean(hid * hid, axis=-1, keepdims=True) + EPS) * ng_ref[:, cols]
        y_ref[:, cols] = (_sigmoid(o_ref[:, cols]) * hn).astype(y_ref.dtype)


def _mlstm(seg_a, seg_b, bias, norm_g, tri, batch, seq, chunk):
    nc = seq // chunk
    w = MLSTM_WIDTH
    nh, dh = MLSTM_HEADS, MLSTM_HEAD_DIM
    seqs = 1
    a3 = seg_a.reshape(batch, seq, SEG_A)
    b3 = seg_b.reshape(batch, seq, SEG_B)

    def col_spec(off):
        return pl.BlockSpec((seqs, chunk, w), lambda b, c, o=off // w: (b, c, o))

    y = pl.pallas_call(
        _mlstm_kernel,
        grid=(batch // seqs, nc),
        in_specs=[
            pl.BlockSpec(memory_space=pltpu.SMEM),
            col_spec(A_QK), col_spec(A_QK + w), col_spec(A_V), col_spec(B_O),
            pl.BlockSpec((seqs, chunk, LANE), lambda b, c: (b, c, B_IF // LANE)),
            pl.BlockSpec((1, w), lambda b, c: (0, 0)),
            pl.BlockSpec((chunk, chunk), lambda b, c: (0, 0)),
        ],
        out_specs=pl.BlockSpec((seqs, chunk, w), lambda b, c: (b, c, 0)),
        out_shape=jax.ShapeDtypeStruct((batch, seq, w), BF16),
        scratch_shapes=[
            pltpu.VMEM((seqs, nh, dh, dh), F32),
            pltpu.VMEM((seqs, 8, dh), F32),
            pltpu.VMEM((seqs, 8, LANE), F32),
        ],
        compiler_params=pltpu.CompilerParams(
            dimension_semantics=("parallel", "arbitrary"), vmem_limit_bytes=VMEM_LIMIT),
        name="mlstm",
    )(bias, a3, a3, a3, b3, b3, norm_g, tri)
    return y.reshape(batch * seq, w)


def _outproj_kernel(x_ref, ya_ref, ym_ref, wa_ref, wm_ref, o_ref):
    o_ref[...] = x_ref[...] + _dot(ya_ref[...], wa_ref[...]) + _dot(ym_ref[...], wm_ref[...])


def _outproj(x2, ya, ym, wa, wm, tm):
    n, d = x2.shape
    return pl.pallas_call(
        _outproj_kernel,
        grid=(n // tm,),
        in_specs=[
            pl.BlockSpec((tm, d), lambda i: (i, 0)),
            pl.BlockSpec((tm, ya.shape[1]), lambda i: (i, 0)),
            pl.BlockSpec((tm, ym.shape[1]), lambda i: (i, 0)),
            pl.BlockSpec(wa.shape, lambda i: (0, 0)),
            pl.BlockSpec(wm.shape, lambda i: (0, 0)),
        ],
        out_specs=pl.BlockSpec((tm, d), lambda i: (i, 0)),
        out_shape=jax.ShapeDtypeStruct((n, d), F32),
        compiler_params=pltpu.CompilerParams(
            dimension_semantics=("parallel",), vmem_limit_bytes=VMEM_LIMIT),
        name="outproj",
    )(x2, ya, ym, wa, wm)


def _mlp_kernel(x_ref, g_ref, w1_ref, w2_ref, gf_ref, o_ref, h_ref):
    f = pl.program_id(1)

    @pl.when(f == 0)
    def _():
        x = x_ref[...]
        r = lax.rsqrt(jnp.mean(x * x, axis=-1, keepdims=True) + EPS)
        h_ref[...] = (x * r * g_ref[...]).astype(BF16)
        o_ref[...] = x

    u = jnp.maximum(_dot(h_ref[...], w1_ref[...]), 0.0)
    o_ref[...] += _dot((u * u).astype(BF16), w2_ref[...])

    @pl.when(f == pl.num_programs(1) - 1)
    def _():
        x2 = o_ref[...]
        r = lax.rsqrt(jnp.mean(x2 * x2, axis=-1, keepdims=True) + EPS)
        o_ref[...] = x2 * r * gf_ref[...]


def _mlp(x1, g, w1, w2, gf, tm, tf):
    n, d = x1.shape
    dff = w1.shape[1]
    return pl.pallas_call(
        _mlp_kernel,
        grid=(n // tm, dff // tf),
        in_specs=[
            pl.BlockSpec((tm, d), lambda i, f: (i, 0)),
            pl.BlockSpec((1, d), lambda i, f: (0, 0)),
            pl.BlockSpec((d, tf), lambda i, f: (0, f)),
            pl.BlockSpec((tf, d), lambda i, f: (f, 0)),
            pl.BlockSpec((1, d), lambda i, f: (0, 0)),
        ],
        out_specs=pl.BlockSpec((tm, d), lambda i, f: (i, 0)),
        out_shape=jax.ShapeDtypeStruct((n, d), F32),
        scratch_shapes=[pltpu.VMEM((tm, d), BF16)],
        compiler_params=pltpu.CompilerParams(
            dimension_semantics=("parallel", "arbitrary"), vmem_limit_bytes=VMEM_LIMIT),
        name="mlp",
    )(x1, g, w1, w2, gf)


def _row_tile(n, want):
    t = want
    while n % t:
        t //= 2
    return t


def _layer(x2, batch, seq, norm_mix_g, w_in, w_cmp_k1, w_cmp_k2, pos_cmp_k, w_cmp_v1, w_cmp_v2,
           pos_cmp_v, conv_w, conv_b, b_igate, b_fgate, mlstm_norm_g, w_out, norm_mlp_g,
           w_mlp_in, w_mlp_out):
    n, d = x2.shape
    assert seq % Q_BLOCK == 0 and seq >= WINDOW + Q_BLOCK and seq // SEL_BLOCK <= SEL_LANES
    g_ = NSA_KV_GROUPS
    nh = MLSTM_HEADS

    c_gate = NSA_WIDTH + 6 * NSA_KV_WIDTH
    c_qk = c_gate + NSA_HEADS * N_BRANCH
    c_v = c_qk + 2 * MLSTM_WIDTH
    c_o = c_v + MLSTM_WIDTH
    c_i = c_o + MLSTM_WIDTH
    c_f = c_i + nh
    w16 = w_in.astype(BF16)
    gate_cols = []
    per_g = NSA_REP * N_BRANCH
    for g in range(g_):
        gate_cols += [w16[:, c_gate + g * per_g:c_gate + (g + 1) * per_g],
                      jnp.zeros((d, LANE - per_g), BF16)]
    w_ab = jnp.concatenate(
        [w16[:, c_v:c_o], w16[:, c_qk:c_v], w16[:, :c_gate],
         w16[:, c_o:c_i]] + gate_cols
        + [w16[:, c_i:c_f + nh], jnp.zeros((d, SEG_B - B_IF - 2 * nh), BF16)],
        axis=1)
    scale = jnp.concatenate([jnp.ones((1, A_QK + MLSTM_WIDTH), F32),
                             jnp.full((1, MLSTM_WIDTH), MLSTM_HEAD_DIM ** -0.5, F32),
                             jnp.full((1, NSA_WIDTH), NSA_HEAD_DIM ** -0.5, F32),
                             jnp.ones((1, SEG_A - A_KV + SEG_B), F32)], axis=1)
    conv_pad = ((0, 0), (A_QK, SEG_A - A_Q + SEG_B))
    cw_ab = jnp.pad(conv_w, conv_pad)
    cb_ab = jnp.pad(conv_b.reshape(1, -1), conv_pad)
    g_mix = norm_mix_g.reshape(1, d)

    tm = _row_tile(seq, ROW_TILE)
    seg_a, seg_b = _inproj(x2, g_mix, w_ab, scale, cw_ab, cb_ab, tm, COL_TILE, (A_QK, A_Q), seq)

    n_sub = seq // CMP_STRIDE
    dh = NSA_HEAD_DIM
    w1s = jnp.stack([w_cmp_k1, w_cmp_v1]).reshape(2, 2, CMP_STRIDE, dh, dh)
    w1r = jnp.concatenate([w1s[:, 0], w1s[:, 1]], axis=-1).astype(BF16)
    w2s = jnp.stack([w_cmp_k2, w_cmp_v2]).astype(BF16)
    poss = jnp.stack([pos_cmp_k, pos_cmp_v]).reshape(2, 2, CMP_STRIDE, dh).transpose(0, 2, 1, 3)
    posr = jnp.pad(poss, ((0, 0), (0, 0), (0, 6), (0, 0))).astype(BF16)
    kvc, kvct = _compress(seg_a, w1r, w2s, posr, batch, seq)

    cmp_start = np.arange(n_sub) * CMP_STRIDE
    sel_start = np.arange(SEL_LANES) * SEL_BLOCK
    ovt = ((cmp_start[None, :] < sel_start[:, None] + SEL_BLOCK)
           & (cmp_start[None, :] + CMP_BLOCK - 1 >= sel_start[:, None])
           & (np.arange(n_sub)[None, :] < n_sub - CMP_BLOCK // CMP_STRIDE + 1))
    ovt = jnp.asarray(ovt, BF16)
    slopes = jnp.exp2(-8.0 * jnp.arange(1, NSA_HEADS + 1, dtype=F32) / NSA_HEADS)
    ocmp, pen, flags = _nsa_cmp(seg_a, kvc, kvct, ovt, slopes, batch, seq)
    onehot, wext, dbias, ubias, wbias = _nsa_tables(slopes, seq)
    y_a = _nsa_attn(seg_a, seg_b, pen, ocmp, flags, onehot, wext, dbias, ubias, wbias,
                    slopes, batch, seq)

    chunk = 256 if seq % 256 == 0 else 128
    bias = jnp.concatenate([b_igate, b_fgate]).astype(F32)
    tri = jnp.asarray(np.tril(np.ones((chunk, chunk), np.float32)), BF16)
    y_m = _mlstm(seg_a, seg_b, bias, mlstm_norm_g.reshape(1, -1), tri, batch, seq, chunk)

    w_o = w_out.astype(BF16)
    x1 = _outproj(x2, y_a, y_m, w_o[:NSA_WIDTH], w_o[NSA_WIDTH:], _row_tile(n, 512))
    return x1, (norm_mlp_g.reshape(1, d), w_mlp_in.astype(BF16), w_mlp_out.astype(BF16))


def kernel(x, norm_mix_g, w_in, w_cmp_k1, w_cmp_k2, pos_cmp_k, w_cmp_v1, w_cmp_v2, pos_cmp_v, conv_w, conv_b, b_igate, b_fgate, mlstm_norm_g, w_out, norm_mlp_g, w_mlp_in, w_mlp_out, norm_f_g):
    batch, seq, d = x.shape
    depth = w_in.shape[0]
    assert depth == 1, "the final RMSNorm is fused into the last layer's channel mixer"
    x2 = x.reshape(batch * seq, d)
    tm = _row_tile(batch * seq, ROW_TILE)
    for l in range(depth):
        x1, (g_mlp, w1, w2) = _layer(
            x2, batch, seq, norm_mix_g[l], w_in[l], w_cmp_k1[l], w_cmp_k2[l], pos_cmp_k[l],
            w_cmp_v1[l], w_cmp_v2[l], pos_cmp_v[l], conv_w[l], conv_b[l], b_igate[l], b_fgate[l],
            mlstm_norm_g[l], w_out[l], norm_mlp_g[l], w_mlp_in[l], w_mlp_out[l])
        x2 = _mlp(x1, g_mlp, w1, w2, norm_f_g.reshape(1, d), tm, 512)
    return x2.reshape(batch, seq, d)
```

```python
import functools

import numpy as np
import jax
import jax.numpy as jnp
from jax import lax
from jax.experimental import pallas as pl
from jax.experimental.pallas import tpu as pltpu

F32 = jnp.float32
BF16 = jnp.bfloat16

EPS = 1e-6
NEG = -1e30
FORCE_BONUS = 1e4
PICKED = -3e38
MASK_BIG = 1e30

D_MODEL = 2048
NSA_HEAD_DIM = 128
NSA_WIDTH = D_MODEL // 2
NSA_HEADS = NSA_WIDTH // NSA_HEAD_DIM
NSA_REP = 4
NSA_KV_GROUPS = NSA_HEADS // NSA_REP
NSA_KV_WIDTH = NSA_KV_GROUPS * NSA_HEAD_DIM
CMP_BLOCK = 32
CMP_STRIDE = 16
SEL_BLOCK = 64
SEL_TOPK = 16
WINDOW = 512
Q_BLOCK = 128
N_BRANCH = 3
MLSTM_HEAD_DIM = 256
MLSTM_WIDTH = D_MODEL - NSA_WIDTH
MLSTM_HEADS = MLSTM_WIDTH // MLSTM_HEAD_DIM
CONV_WIDTH = 4

LANE = 128
SEL_LANES = 128
KEY_TILE = 128
SEL_TILES_PER_STEP = 2
WIN_TILES = WINDOW // KEY_TILE
SUM_ROWS = 16
VMEM_LIMIT = 56 * 1024 * 1024
ROW_TILE = 1024

A_V, A_QK = 0, MLSTM_WIDTH
A_Q = A_QK + 2 * MLSTM_WIDTH
A_KV = A_Q + NSA_WIDTH
SEG_A = A_KV + 6 * NSA_KV_WIDTH
B_O = 0
B_GATE = MLSTM_WIDTH
B_IF = B_GATE + NSA_KV_GROUPS * LANE
SEG_B = B_IF + 2 * LANE
COL_TILE = 512

WCOL_PAD, WCOL_HI, WCOL_LO, WCOL_QHI, WCOL_QLO = 0, 1, 2, 3, 4


def _dot(a, b):
    return jnp.dot(a, b, preferred_element_type=F32)


def _dot_nt(a, b):
    return lax.dot_general(a, b, (((1,), (1,)), ((), ())), preferred_element_type=F32)


def _sigmoid(x):
    return 1.0 / (1.0 + jnp.exp(-x))


def _shifted(x, tail, s):
    xs = pltpu.roll(x, s, axis=0)
    ts = pltpu.roll(tail, s, axis=0)
    row8 = lax.broadcasted_iota(jnp.int32, (8, 1), 0)
    head = jnp.where(row8 < s, ts, xs[:8])
    return jnp.concatenate([head, xs[8:]], axis=0)


def _conv_silu(x, tail, w, b):
    y = b + _shifted(x, tail, CONV_WIDTH - 1) * w[0:1]
    for i in range(1, CONV_WIDTH - 1):
        y = y + _shifted(x, tail, CONV_WIDTH - 1 - i) * w[i:i + 1]
    y = y + x * w[CONV_WIDTH - 1:CONV_WIDTH]
    return y * _sigmoid(y)


def _inproj_kernel(x_ref, g_ref, w_ref, cs_ref, cw_ref, cb_ref, oa_ref, ob_ref, h_ref, halo_ref, *,
                   na, conv_lo, conv_hi, tiles_per_seq):
    i = pl.program_id(0)
    j = pl.program_id(1)

    @pl.when(j == 0)
    def _():
        x = x_ref[...]
        r = lax.rsqrt(jnp.mean(x * x, axis=-1, keepdims=True) + EPS)
        h_ref[...] = (x * r * g_ref[...]).astype(BF16)

    is_conv = (j >= conv_lo) & (j < conv_hi)

    @pl.when((j < na) & jnp.logical_not(is_conv))
    def _():
        oa_ref[...] = (_dot(h_ref[...], w_ref[...]) * cs_ref[...]).astype(oa_ref.dtype)

    @pl.when(j >= na)
    def _():
        ob_ref[...] = (_dot(h_ref[...], w_ref[...]) * cs_ref[...]).astype(ob_ref.dtype)

    @pl.when(is_conv)
    def _():
        slot = j - conv_lo

        @pl.when(i % tiles_per_seq == 0)
        def _():
            halo_ref[slot] = jnp.zeros(halo_ref.shape[1:], F32)

        acc = _dot(h_ref[...], w_ref[...])
        tail = halo_ref[slot]
        halo_ref[slot] = acc[acc.shape[0] - 8:]
        y = _conv_silu(acc, tail, cw_ref[...], cb_ref[...])
        oa_ref[...] = (y * cs_ref[...]).astype(oa_ref.dtype)


def _inproj(x2, g, w, cs, cw, cb, tm, tn, conv_cols, seq):
    n, d = x2.shape
    na, nb = SEG_A // tn, SEG_B // tn
    conv_lo, conv_hi = conv_cols[0] // tn, conv_cols[1] // tn
    assert conv_cols[0] % tn == 0 and conv_cols[1] % tn == 0 and seq % tm == 0 and conv_hi <= na
    assert SEG_A % tn == 0 and SEG_B % tn == 0 and w.shape[1] == SEG_A + SEG_B
    return pl.pallas_call(
        functools.partial(_inproj_kernel, na=na, conv_lo=conv_lo, conv_hi=conv_hi,
                          tiles_per_seq=seq // tm),
        grid=(n // tm, na + nb),
        in_specs=[
            pl.BlockSpec((tm, d), lambda i, j: (i, 0)),
            pl.BlockSpec((1, d), lambda i, j: (0, 0)),
            pl.BlockSpec((d, tn), lambda i, j: (0, j)),
            pl.BlockSpec((1, tn), lambda i, j: (0, j)),
            pl.BlockSpec((CONV_WIDTH, tn), lambda i, j: (0, j)),
            pl.BlockSpec((1, tn), lambda i, j: (0, j)),
        ],
        out_specs=[pl.BlockSpec((tm, tn), lambda i, j: (i, jnp.minimum(j, na - 1))),
                   pl.BlockSpec((tm, tn), lambda i, j: (i, jnp.maximum(j - na, 0)))],
        out_shape=[jax.ShapeDtypeStruct((n, SEG_A), BF16), jax.ShapeDtypeStruct((n, SEG_B), F32)],
        scratch_shapes=[pltpu.VMEM((tm, d), BF16),
                        pltpu.VMEM((conv_hi - conv_lo, 8, tn), F32)],
        compiler_params=pltpu.CompilerParams(
            dimension_semantics=("arbitrary", "arbitrary"), vmem_limit_bytes=VMEM_LIMIT),
        name="inproj",
    )(x2, g, w, cs, cw, cb)


def _compress_kernel(x_ref, w1_ref, w2_ref, pos_ref, o_ref, ot_ref, xf_ref):
    n_sub = o_ref.shape[2]
    dh = NSA_HEAD_DIM
    xf_ref[...] = x_ref[...].astype(F32)
    acc = jnp.zeros((n_sub, 2 * dh), F32)
    posw = jnp.zeros((1, dh), F32)
    for p in range(CMP_STRIDE):
        wp = w1_ref[0, p]
        acc = acc + _dot(xf_ref[pl.ds(p, n_sub, stride=CMP_STRIDE), :].astype(BF16), wp)
        pw = _dot(pos_ref[0, p], wp)
        posw = posw + pw[0:1, :dh] + pw[1:2, dh:]
    bot = pltpu.roll(acc[:, dh:], n_sub - 1, axis=0)
    pre = acc[:, :dh] + bot + posw
    hid = pre * _sigmoid(pre)
    out = _dot(hid.astype(BF16), w2_ref[0])
    o_ref[0, 0] = out.astype(o_ref.dtype)
    ot_ref[0, 0] = out.T.astype(ot_ref.dtype)


def _compress(seg_a, w1r, w2s, posr, batch, seq):
    g = NSA_KV_GROUPS
    c = 2 * g
    dh = NSA_HEAD_DIM
    n_sub = seq // CMP_STRIDE
    return pl.pallas_call(
        _compress_kernel,
        grid=(batch, c),
        in_specs=[
            pl.BlockSpec((seq, dh), lambda i, j: (i, A_KV // dh + j)),
            pl.BlockSpec((1, CMP_STRIDE, dh, 2 * dh), lambda i, j: (j // g, 0, 0, 0)),
            pl.BlockSpec((1, dh, dh), lambda i, j: (j // g, 0, 0)),
            pl.BlockSpec((1, CMP_STRIDE, 8, dh), lambda i, j: (j // g, 0, 0, 0)),
        ],
        out_specs=[pl.BlockSpec((1, 1, n_sub, dh), lambda i, j: (i, j, 0, 0)),
                   pl.BlockSpec((1, 1, dh, n_sub), lambda i, j: (i, j, 0, 0))],
        out_shape=[jax.ShapeDtypeStruct((batch, c, n_sub, dh), BF16),
                   jax.ShapeDtypeStruct((batch, c, dh, n_sub), BF16)],
        scratch_shapes=[pltpu.VMEM((seq, dh), F32)],
        compiler_params=pltpu.CompilerParams(
            dimension_semantics=("parallel", "parallel"), vmem_limit_bytes=VMEM_LIMIT),
        name="compress",
    )(seg_a, w1r, w2s, posr)


def _stack_heads(q_all):
    dh = NSA_HEAD_DIM
    return jnp.concatenate([q_all[:, r * dh:(r + 1) * dh] for r in range(NSA_REP)], axis=0)


def _tile_list(hit, tile0, pad_tile):
    n = SEL_LANES
    row = lax.broadcasted_iota(jnp.int32, (n, n), 0)
    col = lax.broadcasted_iota(jnp.int32, (n, n), 1)
    lane = lax.broadcasted_iota(jnp.int32, (1, n), 1)
    hit8 = jnp.broadcast_to(hit, (8, n)).astype(BF16)
    pair = _dot(hit8, jnp.where(row // 2 == col, 1.0, 0.0).astype(BF16))
    act = jnp.where((pair > 0.0) & (lane < tile0), 1.0, 0.0)
    rank = _dot(act.astype(BF16), jnp.where(row < col, 1.0, 0.0).astype(BF16))
    cnt = rank[0:1, n - 1:n]
    both = jnp.concatenate([act[0:1], rank[0:1], jnp.zeros((n - 2, n), F32)], axis=0).T
    act_c, rank_c = both[:, 0:1], both[:, 1:2]
    place = jnp.where((act_c > 0.0) & (rank_c == col.astype(F32)), 1.0, 0.0)
    tiles = _dot(jnp.broadcast_to(lane.astype(F32), (8, n)).astype(BF16), place.astype(BF16))[0:1]
    out = jnp.where(lane.astype(F32) < cnt, tiles, float(pad_tile))
    return jnp.where(lane == n - 1, cnt, out).astype(jnp.int32)


def _nsa_cmp_kernel(slopes_ref, q_ref, kc_ref, vct_ref, ovt_ref, ocmp_ref, pen_ref, tiles_ref, *,
                    seq, sub_blocks):
    step = pl.program_id(2)
    nq_step = sub_blocks * Q_BLOCK
    chunk = min(SEL_LANES, kc_ref.shape[2])
    n_chunks = kc_ref.shape[2] // chunk
    need = ((step + 1) * nq_step - CMP_BLOCK) // CMP_STRIDE + 1
    n_need = (need + chunk - 1) // chunk
    for v in range(1, n_chunks + 1):
        cond = (n_need == v) if v < n_chunks else (n_need >= v)
        if v == 1:
            cond = n_need <= 1

        @pl.when(cond)
        def _(v=v):
            hit = None
            for sub in range(sub_blocks):
                rows = slice(sub * Q_BLOCK, (sub + 1) * Q_BLOCK)
                h = _nsa_cmp_block(slopes_ref, q_ref[rows, :], kc_ref, vct_ref, ovt_ref,
                                   ocmp_ref.at[0, 0, sub], pen_ref.at[0, 0, rows],
                                   step * sub_blocks + sub, seq, v * chunk)
                hit = h if hit is None else jnp.maximum(hit, h)
            tiles_ref[0, 0, 0] = _tile_list(hit, step * sub_blocks, seq // KEY_TILE)


def _nsa_cmp_block(slopes_ref, q_all, kc_ref, vct_ref, ovt_ref, ocmp_ref, pen_ref, qb, seq, n_use):
    g = pl.program_id(1)
    nq = Q_BLOCK
    n_cpad = n_use
    n_cmp = seq // CMP_STRIDE - CMP_BLOCK // CMP_STRIDE + 1
    t0 = qb * nq

    s_t = _dot_nt(kc_ref[0, 0, :n_use, :], _stack_heads(q_all))
    n_s = lax.broadcasted_iota(jnp.int32, (n_cpad, 1), 0)
    q_l = lax.broadcasted_iota(jnp.int32, (1, nq), 1)
    dist = (t0 - (CMP_BLOCK - 1)) + q_l - n_s * CMP_STRIDE
    valid = (dist >= 0) & (n_s < n_cmp)
    dist_f = dist.astype(F32)
    probs = []
    p_sum = jnp.zeros((n_cpad, nq), F32)
    for r in range(NSA_REP):
        slope = slopes_ref[g * NSA_REP + r]
        s = jnp.where(valid, s_t[:, r * nq:(r + 1) * nq] - slope * dist_f, NEG)
        m = jnp.max(s, axis=0, keepdims=True)
        e = jnp.exp(s - m)
        inv = jnp.where(m > 0.5 * NEG, 1.0 / jnp.sum(e, axis=0, keepdims=True), 0.0)
        p = e * inv
        probs.append(p.astype(BF16))
        p_sum = p_sum + p
    ocmp_ref[...] = _dot(vct_ref[0, 0, :, :n_use], jnp.concatenate(probs, axis=1))

    p_hi = p_sum.astype(BF16)
    p_lo = (p_sum - p_hi.astype(F32)).astype(BF16)
    ovt = ovt_ref[:, :n_use]
    imp = _dot(ovt, p_hi) + _dot(ovt, p_lo)
    j_i = lax.broadcasted_iota(jnp.int32, (SEL_LANES, 1), 0)
    t_l = t0 + q_l
    cur = t_l // SEL_BLOCK
    forced = (j_i == 0) | (j_i == cur) | (j_i == cur - 1)
    causal_blk = j_i * SEL_BLOCK <= t_l
    val = jnp.where(causal_blk, jnp.where(forced, imp + FORCE_BONUS, imp), NEG)
    j_f = j_i.astype(F32)
    sel_t = jnp.zeros((SEL_LANES, nq), F32)
    for _ in range(min(SEL_TOPK, seq // SEL_BLOCK)):
        mx = jnp.max(val, axis=0, keepdims=True)
        first = jnp.min(jnp.where(val == mx, j_f, float(SEL_LANES)), axis=0, keepdims=True)
        pick = j_f == first
        sel_t = jnp.where(pick, 1.0, sel_t)
        val = jnp.where(pick, PICKED, val)
    sel = sel_t.T
    pen_ref[...] = ((sel - 1.0) * MASK_BIG).astype(pen_ref.dtype)
    return jnp.max(sel, axis=0, keepdims=True)


def _nsa_cmp(seg_a, kvc, kvct, ovt, slopes, batch, seq):
    nqb = seq // Q_BLOCK
    gq = NSA_REP * NSA_HEAD_DIM
    g_ = NSA_KV_GROUPS
    n_cpad = kvc.shape[2]
    sub = _nsa_sub_blocks(seq)
    nstep = nqb // sub
    return pl.pallas_call(
        functools.partial(_nsa_cmp_kernel, seq=seq, sub_blocks=sub),
        grid=(batch, g_, nstep),
        in_specs=[
            pl.BlockSpec(memory_space=pltpu.SMEM),
            pl.BlockSpec((sub * Q_BLOCK, gq), lambda b, g, q: (b * nstep + q, A_Q // gq + g)),
            pl.BlockSpec((1, 1, n_cpad, NSA_HEAD_DIM), lambda b, g, q: (b, g, 0, 0)),
            pl.BlockSpec((1, 1, NSA_HEAD_DIM, n_cpad), lambda b, g, q: (b, g_ + g, 0, 0)),
            pl.BlockSpec((SEL_LANES, n_cpad), lambda b, g, q: (0, 0)),
        ],
        out_specs=[
            pl.BlockSpec((1, 1, sub, NSA_HEAD_DIM, gq), lambda b, g, q: (b, g, q, 0, 0)),
            pl.BlockSpec((1, 1, sub * Q_BLOCK, SEL_LANES), lambda b, g, q: (b, g, q, 0)),
            pl.BlockSpec((1, 1, 1, 1, SEL_LANES), lambda b, g, q: (b, g, q, 0, 0)),
        ],
        out_shape=[
            jax.ShapeDtypeStruct((batch, g_, nqb, NSA_HEAD_DIM, gq), F32),
            jax.ShapeDtypeStruct((batch, g_, seq, SEL_LANES), BF16),
            jax.ShapeDtypeStruct((batch, g_, nstep, 1, SEL_LANES), jnp.int32),
        ],
        compiler_params=pltpu.CompilerParams(
            dimension_semantics=("parallel", "parallel", "parallel"), vmem_limit_bytes=VMEM_LIMIT),
        name="nsa_cmp",
    )(slopes, seg_a, kvc, kvct, ovt)


def _nsa_attn_kernel(slopes_ref, tiles_ref, q_ref, gate_ref, pen_ref, ocmp_ref, ks_ref, vs_ref,
                     kw_ref, vw_ref, onehot_ref, wext_ref, dbias_ref, ubias_ref, wbias_ref, o_ref,
                     ksel_ref, vselt_ref, kwin_ref, vwint_ref,
                     qa_ref, qw_ref, m_ref, acc_ref, owin_ref, sa_ref, sb_ref,
                     *, seq, sub_blocks):
    g = pl.program_id(1)
    step = pl.program_id(2)
    dh = NSA_HEAD_DIM
    nq = sub_blocks * Q_BLOCK
    kt = KEY_TILE
    per = SEL_TILES_PER_STEP
    pad_tile = seq // kt
    t0 = step * nq
    tile0 = step * sub_blocks

    @pl.when(step == 0)
    def _():
        ksel_ref[0:seq, :dh] = ks_ref[...]
        ksel_ref[0:seq, dh:] = onehot_ref[...]
        ksel_ref[seq:, :dh] = jnp.zeros((kt, dh), BF16)
        ksel_ref[seq:, dh:] = jnp.ones((kt, SEL_LANES), BF16)
        lane2 = lax.broadcasted_iota(jnp.int32, (WINDOW, dh + LANE), 1)
        kwin_ref[0:WINDOW, :] = jnp.where(lane2 == dh + WCOL_PAD, 1.0, 0.0).astype(BF16)
        kwin_ref[WINDOW:, :dh] = kw_ref[...]
        kwin_ref[WINDOW:, dh:] = wext_ref[...]
        zero_tile = jnp.zeros((dh + SUM_ROWS, kt), BF16)
        ones_rows = jnp.ones((SUM_ROWS, kt), BF16)
        vselt_ref[pad_tile] = zero_tile
        for i in range(WIN_TILES):
            vwint_ref[i] = zero_tile

        def transpose_tile(t, carry):
            r0 = pl.multiple_of(t * kt, kt)
            vselt_ref[t, :dh] = vs_ref[pl.ds(r0, kt), :].astype(F32).T.astype(BF16)
            vselt_ref[t, dh:] = ones_rows
            vwint_ref[t + WIN_TILES, :dh] = vw_ref[pl.ds(r0, kt), :].astype(F32).T.astype(BF16)
            vwint_ref[t + WIN_TILES, dh:] = ones_rows
            return carry

        lax.fori_loop(0, seq // kt, transpose_tile, 0)

    slopes = [slopes_ref[g * NSA_REP + r] for r in range(NSA_REP)]
    row_q = lax.broadcasted_iota(jnp.int32, (nq, 1), 0)
    lane = lax.broadcasted_iota(jnp.int32, (1, LANE), 1)
    q_all = q_ref[...]

    pen = pen_ref[0, 0].astype(F32)
    blk_rel = ((lane - (t0 + row_q) // SEL_BLOCK) * SEL_BLOCK).astype(F32)
    for r in range(NSA_REP):
        qa_ref[r * nq:(r + 1) * nq, :dh] = q_all[:, r * dh:(r + 1) * dh]
        qa_ref[r * nq:(r + 1) * nq, dh:] = (pen + slopes[r] * blk_rel).astype(BF16)

    def listed(idx):
        return tiles_ref[0, 0, 0, 0, idx]

    cnt = listed(SEL_LANES - 1)

    def group_scores(it):
        keys = jnp.concatenate(
            [ksel_ref[pl.ds(pl.multiple_of(listed(it * per + i) * kt, kt), kt), :]
             for i in range(per)], axis=0)
        return _dot_nt(keys, qa_ref[...]) + ubias_ref[0]

    s = _dot_nt(ksel_ref[pl.ds(pl.multiple_of(t0, kt), nq), :], qa_ref[...]) + dbias_ref[0]
    m0 = jnp.max(s, axis=0, keepdims=True)
    p = jnp.exp(s - m0)
    m_ref[...] = m0
    vals = jnp.concatenate([vselt_ref[tile0 + a] for a in range(sub_blocks)], axis=1)
    acc_ref[...] = _dot(vals, p.astype(BF16))
    sa_ref[...] = group_scores(0)

    tp = t0 + WINDOW + row_q
    t_hi = (tp // SEL_BLOCK).astype(F32)
    t_lo = (tp % SEL_BLOCK).astype(F32)
    for r in range(NSA_REP):
        sl = slopes[r]
        ext = jnp.where(lane == WCOL_PAD, -MASK_BIG, 0.0)
        ext = jnp.where(lane == WCOL_HI, sl * SEL_BLOCK, ext)
        ext = jnp.where(lane == WCOL_LO, sl, ext)
        ext = jnp.where(lane == WCOL_QHI, -sl * SEL_BLOCK * t_hi, ext)
        ext = jnp.where(lane == WCOL_QLO, -sl * t_lo, ext)
        qw_ref[r * nq:(r + 1) * nq, :dh] = q_all[:, r * dh:(r + 1) * dh]
        qw_ref[r * nq:(r + 1) * nq, dh:] = ext.astype(BF16)
    wlen = WINDOW + nq
    s = _dot_nt(kwin_ref[pl.ds(pl.multiple_of(t0, kt), wlen), :], qw_ref[...]) + wbias_ref[...]
    e = jnp.exp(s - jnp.max(s, axis=0, keepdims=True))
    vwin = jnp.concatenate([vwint_ref[tile0 + i] for i in range(WIN_TILES + sub_blocks)], axis=1)
    ow = _dot(vwin, e.astype(BF16))
    owin_ref[...] = ow[:dh] / ow[dh:dh + 1]

    def absorb(s, grp):
        vals = jnp.concatenate([vselt_ref[listed(grp * per + i)] for i in range(per)], axis=1)
        m_old = m_ref[...]
        m_new = jnp.maximum(m_old, jnp.max(s, axis=0, keepdims=True))
        alpha = jnp.exp(m_old - m_new)
        p = jnp.exp(s - m_new)
        acc_ref[...] = alpha * acc_ref[...] + _dot(vals, p.astype(BF16))
        m_ref[...] = m_new

    def sel_body(it, carry):
        s = sa_ref[...]
        sb_ref[...] = group_scores(2 * it + 1)
        absorb(s, 2 * it)
        s = sb_ref[...]
        sa_ref[...] = group_scores(2 * it + 2)
        absorb(s, 2 * it + 1)
        return carry

    lax.fori_loop(0, (cnt + 2 * per - 1) // (2 * per), sel_body, 0)
    o_sel = acc_ref[:dh, :] / acc_ref[dh:dh + 1, :]
    o_win = owin_ref[...]

    gate_t = _sigmoid(gate_ref[...]).T
    for a in range(sub_blocks):
        o_cmp = ocmp_ref[0, 0, a]
        qs = slice(a * Q_BLOCK, (a + 1) * Q_BLOCK)
        for r in range(NSA_REP):
            cols = slice(r * nq + a * Q_BLOCK, r * nq + (a + 1) * Q_BLOCK)
            c0 = N_BRANCH * r
            out_t = (gate_t[c0:c0 + 1, qs] * o_cmp[:, r * Q_BLOCK:(r + 1) * Q_BLOCK]
                     + gate_t[c0 + 1:c0 + 2, qs] * o_sel[:, cols]
                     + gate_t[c0 + 2:c0 + 3, qs] * o_win[:, cols])
            o_ref[qs, r * dh:(r + 1) * dh] = out_t.T.astype(o_ref.dtype)


def _nsa_sub_blocks(seq):
    return 2 if (seq // Q_BLOCK) % 2 == 0 else 1


def _nsa_attn(seg_a, seg_b, pen, ocmp, tiles, onehot, wext, dbias, ubias, wbias, slopes, batch, seq):
    n = batch * seq
    sub = _nsa_sub_blocks(seq)
    nq = sub * Q_BLOCK
    nqb = seq // nq
    gq = NSA_REP * NSA_HEAD_DIM
    g_ = NSA_KV_GROUPS
    dh, kt = NSA_HEAD_DIM, KEY_TILE
    dk = dh + SEL_LANES
    rq = NSA_REP * nq
    n_t = seq // kt

    def kv_spec(kind):
        return pl.BlockSpec((seq, dh), lambda b, g, q, k=kind: (b, A_KV // dh + k * g_ + g))

    def const_spec(arr):
        return pl.BlockSpec(arr.shape, lambda b, g, q, nd=arr.ndim: (0,) * nd)

    return pl.pallas_call(
        functools.partial(_nsa_attn_kernel, seq=seq, sub_blocks=sub),
        grid=(batch, g_, nqb),
        in_specs=[
            pl.BlockSpec(memory_space=pltpu.SMEM),
            pl.BlockSpec((1, 1, 1, 1, SEL_LANES), lambda b, g, q: (b, g, q, 0, 0),
                         memory_space=pltpu.SMEM),
            pl.BlockSpec((nq, gq), lambda b, g, q: (b * nqb + q, A_Q // gq + g)),
            pl.BlockSpec((nq, LANE), lambda b, g, q: (b * nqb + q, B_GATE // LANE + g)),
            pl.BlockSpec((1, 1, nq, SEL_LANES), lambda b, g, q: (b, g, q, 0)),
            pl.BlockSpec((1, 1, sub, NSA_HEAD_DIM, gq), lambda b, g, q: (b, g, q, 0, 0)),
            kv_spec(2), kv_spec(3), kv_spec(4), kv_spec(5),
            const_spec(onehot), const_spec(wext),
            pl.BlockSpec((1,) + dbias.shape[1:], lambda b, g, q: (g, 0, 0)),
            pl.BlockSpec((1,) + ubias.shape[1:], lambda b, g, q: (g, 0, 0)),
            const_spec(wbias),
        ],
        out_specs=pl.BlockSpec((nq, gq), lambda b, g, q: (b * nqb + q, g)),
        out_shape=jax.ShapeDtypeStruct((n, NSA_WIDTH), BF16),
        scratch_shapes=[
            pltpu.VMEM((seq + kt, dk), BF16),
            pltpu.VMEM((n_t + 1, dh + SUM_ROWS, kt), BF16),
            pltpu.VMEM((seq + WINDOW, dh + LANE), BF16),
            pltpu.VMEM((n_t + WIN_TILES, dh + SUM_ROWS, kt), BF16),
            pltpu.VMEM((rq, dk), BF16),
            pltpu.VMEM((rq, dk), BF16),
            pltpu.VMEM((1, rq), F32),
            pltpu.VMEM((dh + SUM_ROWS, rq), F32),
            pltpu.VMEM((dh, rq), F32),
            pltpu.VMEM((SEL_TILES_PER_STEP * KEY_TILE, rq), F32),
            pltpu.VMEM((SEL_TILES_PER_STEP * KEY_TILE, rq), F32),
        ],
        compiler_params=pltpu.CompilerParams(
            dimension_semantics=("parallel", "parallel", "arbitrary"), vmem_limit_bytes=VMEM_LIMIT),
        name="nsa_attn",
    )(slopes, tiles, seg_a, seg_b, pen, ocmp, seg_a, seg_a, seg_a, seg_a, onehot, wext,
      dbias, ubias, wbias)


def _nsa_tables(slopes, seq):
    g_, kt = NSA_KV_GROUPS, KEY_TILE
    pos = np.arange(seq)
    onehot = jnp.asarray(pos[:, None] // SEL_BLOCK == np.arange(SEL_LANES)[None, :], BF16)
    ext = np.zeros((seq, LANE), np.float32)
    ext[:, WCOL_HI] = (pos + WINDOW) // SEL_BLOCK
    ext[:, WCOL_LO] = (pos + WINDOW) % SEL_BLOCK
    ext[:, WCOL_QHI] = 1.0
    ext[:, WCOL_QLO] = 1.0
    wext = jnp.asarray(ext, BF16)

    nq = _nsa_sub_blocks(seq) * Q_BLOCK

    def alibi_in_block(rows):
        u = jnp.asarray((np.arange(rows) % SEL_BLOCK).astype(np.float32))[None, :, None, None]
        t = jnp.broadcast_to(slopes.reshape(g_, 1, NSA_REP, 1) * u, (g_, rows, NSA_REP, nq))
        return t.reshape(g_, rows, NSA_REP * nq)

    ubias = alibi_in_block(SEL_TILES_PER_STEP * kt)
    kq = np.arange(nq)[:, None] <= np.arange(nq)[None, :]
    causal = np.tile(np.where(kq, 0.0, NEG).astype(np.float32), (1, NSA_REP))
    dbias = alibi_in_block(nq) + jnp.asarray(causal)[None]
    ki = np.arange(WINDOW + nq)[:, None]
    qi = np.arange(nq)[None, :]
    band = np.where((ki > qi) & (ki <= qi + WINDOW), 0.0, NEG).astype(np.float32)
    wbias = jnp.asarray(np.tile(band, (1, NSA_REP)))
    return onehot, wext, dbias, ubias, wbias


def _log_sigmoid(x):
    return jnp.minimum(x, 0.0) - jnp.log(1.0 + jnp.exp(-jnp.abs(x)))


def _split3(x):
    hi = x.astype(BF16)
    r1 = x - hi.astype(F32)
    mid = r1.astype(BF16)
    lo = (r1 - mid.astype(F32)).astype(BF16)
    return hi, mid, lo


def _mlstm_kernel(bias_ref, q_ref, k_ref, v_ref, o_ref, ifc_ref, ng_ref,
                  tri_ref, y_ref, c_ref, n_ref, m_ref):
    ch = pl.program_id(1)

    @pl.when(ch == 0)
    def _():
        c_ref[...] = jnp.zeros(c_ref.shape, F32)
        n_ref[...] = jnp.zeros(n_ref.shape, F32)
        m_ref[...] = jnp.zeros(m_ref.shape, F32)

    for sq in range(q_ref.shape[0]):
        _mlstm_chunk(bias_ref, q_ref.at[sq], k_ref.at[sq], v_ref.at[sq], o_ref.at[sq], ifc_ref.at[sq],
                     ng_ref, tri_ref, y_ref.at[sq], c_ref.at[sq], n_ref.at[sq], m_ref.at[sq])


def _mlstm_chunk(bias_ref, q_ref, k_ref, v_ref, o_ref, ifc_ref, ng_ref, tri_ref, y_ref,
                 c_ref, n_ref, m_ref):
    nh, dh = MLSTM_HEADS, MLSTM_HEAD_DIM
    L = q_ref.shape[0]
    tri = tri_ref[...]
    lane8 = lax.broadcasted_iota(jnp.int32, (1, LANE), 1)
    bias_c = jnp.zeros((1, LANE), F32)
    for h in range(nh):
        bias_c = jnp.where(lane8 == h, bias_ref[h], bias_c)
        bias_c = jnp.where(lane8 == nh + h, bias_ref[nh + h], bias_c)
    pre_c = ifc_ref[...] + bias_c
    cum_c = sum(_dot(tri, part) for part in _split3(_log_sigmoid(pre_c)))
    pre_r = pre_c.T[:2 * nh]
    cum_r = sum(_dot_nt(part, tri) for part in _split3(_log_sigmoid(pre_r)))

    rr = lax.broadcasted_iota(jnp.int32, (L, 1), 0)
    cc = lax.broadcasted_iota(jnp.int32, (1, L), 1)
    causal = cc <= rr

    for h in range(nh):
        cols = slice(h * dh, (h + 1) * dh)
        qb = q_ref[:, cols]
        kb = k_ref[:, cols]
        vh = v_ref[:, cols]
        qh = qb.astype(F32)
        kh = kb.astype(F32)
        b_c = cum_c[:, nh + h:nh + h + 1]
        li_c = pre_c[:, h:h + 1]
        b_r = cum_r[nh + h:nh + h + 1, :]
        li_r = pre_r[h:h + 1, :]
        m_prev = m_ref[h:h + 1, 0:1]

        dmat = jnp.where(causal, b_c - b_r + li_r, NEG)
        a = b_c + m_prev
        m_j = jnp.maximum(a, jnp.max(dmat, axis=1, keepdims=True))
        w_intra = jnp.exp(dmat - m_j)
        w_inter = jnp.exp(a - m_j)
        sc = _dot_nt(qb, kb) * w_intra
        c_old = c_ref[h]
        n_old = n_ref[h:h + 1, :]
        num = w_inter * _dot(qb, c_old.astype(BF16)) + _dot(sc.astype(BF16), vh)
        den = (w_inter * jnp.sum(qh * n_old, axis=1, keepdims=True)
               + jnp.sum(sc, axis=1, keepdims=True))
        hid = num / jnp.maximum(jnp.abs(den), jnp.exp(-m_j))

        g_tot = b_r[:, L - 1:L]
        lw_c = g_tot - b_c + li_c
        lw_r = g_tot - b_r + li_r
        m_new = jnp.maximum(g_tot + m_prev, jnp.max(lw_r, axis=1, keepdims=True))
        decay = jnp.exp(g_tot + m_prev - m_new)
        kw = jnp.exp(lw_c - m_new) * kh
        c_ref[h] = decay * c_old + _dot(kw.T.astype(BF16), vh)
        n_ref[h:h + 1, :] = decay * n_old + jnp.sum(kw, axis=0, keepdims=True)
        m_ref[h:h + 1, :] = jnp.broadcast_to(m_new, (1, LANE))

        hn = hid * lax.rsqrt(jnp.mean(hid * hid, axis=-1, keepdims=True) + EPS) * ng_ref[:, cols]
        y_ref[:, cols] = (_sigmoid(o_ref[:, cols]) * hn).astype(y_ref.dtype)


def _mlstm(seg_a, seg_b, bias, norm_g, tri, batch, seq, chunk):
    nc = seq // chunk
    w = MLSTM_WIDTH
    nh, dh = MLSTM_HEADS, MLSTM_HEAD_DIM
    seqs = 1
    a3 = seg_a.reshape(batch, seq, SEG_A)
    b3 = seg_b.reshape(batch, seq, SEG_B)

    def col_spec(off):
        return pl.BlockSpec((seqs, chunk, w), lambda b, c, o=off // w: (b, c, o))

    y = pl.pallas_call(
        _mlstm_kernel,
        grid=(batch // seqs, nc),
        in_specs=[
            pl.BlockSpec(memory_space=pltpu.SMEM),
            col_spec(A_QK), col_spec(A_QK + w), col_spec(A_V), col_spec(B_O),
            pl.BlockSpec((seqs, chunk, LANE), lambda b, c: (b, c, B_IF // LANE)),
            pl.BlockSpec((1, w), lambda b, c: (0, 0)),
            pl.BlockSpec((chunk, chunk), lambda b, c: (0, 0)),
        ],
        out_specs=pl.BlockSpec((seqs, chunk, w), lambda b, c: (b, c, 0)),
        out_shape=jax.ShapeDtypeStruct((batch, seq, w), BF16),
        scratch_shapes=[
            pltpu.VMEM((seqs, nh, dh, dh), F32),
            pltpu.VMEM((seqs, 8, dh), F32),
            pltpu.VMEM((seqs, 8, LANE), F32),
        ],
        compiler_params=pltpu.CompilerParams(
            dimension_semantics=("parallel", "arbitrary"), vmem_limit_bytes=VMEM_LIMIT),
        name="mlstm",
    )(bias, a3, a3, a3, b3, b3, norm_g, tri)
    return y.reshape(batch * seq, w)


def _outproj_kernel(x_ref, ya_ref, ym_ref, wa_ref, wm_ref, o_ref):
    o_ref[...] = x_ref[...] + _dot(ya_ref[...], wa_ref[...]) + _dot(ym_ref[...], wm_ref[...])


def _outproj(x2, ya, ym, wa, wm, tm):
    n, d = x2.shape
    return pl.pallas_call(
        _outproj_kernel,
        grid=(n // tm,),
        in_specs=[
            pl.BlockSpec((tm, d), lambda i: (i, 0)),
            pl.BlockSpec((tm, ya.shape[1]), lambda i: (i, 0)),
            pl.BlockSpec((tm, ym.shape[1]), lambda i: (i, 0)),
            pl.BlockSpec(wa.shape, lambda i: (0, 0)),
            pl.BlockSpec(wm.shape, lambda i: (0, 0)),
        ],
        out_specs=pl.BlockSpec((tm, d), lambda i: (i, 0)),
        out_shape=jax.ShapeDtypeStruct((n, d), F32),
        compiler_params=pltpu.CompilerParams(
            dimension_semantics=("parallel",), vmem_limit_bytes=VMEM_LIMIT),
        name="outproj",
    )(x2, ya, ym, wa, wm)


def _mlp_kernel(x_ref, g_ref, w1_ref, w2_ref, gf_ref, o_ref, h_ref):
    f = pl.program_id(1)

    @pl.when(f == 0)
    def _():
        x = x_ref[...]
        r = lax.rsqrt(jnp.mean(x * x, axis=-1, keepdims=True) + EPS)
        h_ref[...] = (x * r * g_ref[...]).astype(BF16)
        o_ref[...] = x

    u = jnp.maximum(_dot(h_ref[...], w1_ref[...]), 0.0)
    o_ref[...] += _dot((u * u).astype(BF16), w2_ref[...])

    @pl.when(f == pl.num_programs(1) - 1)
    def _():
        x2 = o_ref[...]
        r = lax.rsqrt(jnp.mean(x2 * x2, axis=-1, keepdims=True) + EPS)
        o_ref[...] = x2 * r * gf_ref[...]


def _mlp(x1, g, w1, w2, gf, tm, tf):
    n, d = x1.shape
    dff = w1.shape[1]
    return pl.pallas_call(
        _mlp_kernel,
        grid=(n // tm, dff // tf),
        in_specs=[
            pl.BlockSpec((tm, d), lambda i, f: (i, 0)),
            pl.BlockSpec((1, d), lambda i, f: (0, 0)),
            pl.BlockSpec((d, tf), lambda i, f: (0, f)),
            pl.BlockSpec((tf, d), lambda i, f: (f, 0)),
            pl.BlockSpec((1, d), lambda i, f: (0, 0)),
        ],
        out_specs=pl.BlockSpec((tm, d), lambda i, f: (i, 0)),
        out_shape=jax.ShapeDtypeStruct((n, d), F32),
        scratch_shapes=[pltpu.VMEM((tm, d), BF16)],
        compiler_params=pltpu.CompilerParams(
            dimension_semantics=("parallel", "arbitrary"), vmem_limit_bytes=VMEM_LIMIT),
        name="mlp",
    )(x1, g, w1, w2, gf)


def _row_tile(n, want):
    t = want
    while n % t:
        t //= 2
    return t


def _layer(x2, batch, seq, norm_mix_g, w_in, w_cmp_k1, w_cmp_k2, pos_cmp_k, w_cmp_v1, w_cmp_v2,
           pos_cmp_v, conv_w, conv_b, b_igate, b_fgate, mlstm_norm_g, w_out, norm_mlp_g,
           w_mlp_in, w_mlp_out):
    n, d = x2.shape
    assert seq % Q_BLOCK == 0 and seq >= WINDOW + Q_BLOCK and seq // SEL_BLOCK <= SEL_LANES
    g_ = NSA_KV_GROUPS
    nh = MLSTM_HEADS

    c_gate = NSA_WIDTH + 6 * NSA_KV_WIDTH
    c_qk = c_gate + NSA_HEADS * N_BRANCH
    c_v = c_qk + 2 * MLSTM_WIDTH
    c_o = c_v + MLSTM_WIDTH
    c_i = c_o + MLSTM_WIDTH
    c_f = c_i + nh
    w16 = w_in.astype(BF16)
    gate_cols = []
    per_g = NSA_REP * N_BRANCH
    for g in range(g_):
        gate_cols += [w16[:, c_gate + g * per_g:c_gate + (g + 1) * per_g],
                      jnp.zeros((d, LANE - per_g), BF16)]
    w_ab = jnp.concatenate(
        [w16[:, c_v:c_o], w16[:, c_qk:c_v], w16[:, :c_gate],
         w16[:, c_o:c_i]] + gate_cols
        + [w16[:, c_i:c_f + nh], jnp.zeros((d, SEG_B - B_IF - 2 * nh), BF16)],
        axis=1)
    scale = jnp.concatenate([jnp.ones((1, A_QK + MLSTM_WIDTH), F32),
                             jnp.full((1, MLSTM_WIDTH), MLSTM_HEAD_DIM ** -0.5, F32),
                             jnp.full((1, NSA_WIDTH), NSA_HEAD_DIM ** -0.5, F32),
                             jnp.ones((1, SEG_A - A_KV + SEG_B), F32)], axis=1)
    conv_pad = ((0, 0), (A_QK, SEG_A - A_Q + SEG_B))
    cw_ab = jnp.pad(conv_w, conv_pad)
    cb_ab = jnp.pad(conv_b.reshape(1, -1), conv_pad)
    g_mix = norm_mix_g.reshape(1, d)

    tm = _row_tile(seq, ROW_TILE)
    seg_a, seg_b = _inproj(x2, g_mix, w_ab, scale, cw_ab, cb_ab, tm, COL_TILE, (A_QK, A_Q), seq)

    n_sub = seq // CMP_STRIDE
    dh = NSA_HEAD_DIM
    w1s = jnp.stack([w_cmp_k1, w_cmp_v1]).reshape(2, 2, CMP_STRIDE, dh, dh)
    w1r = jnp.concatenate([w1s[:, 0], w1s[:, 1]], axis=-1).astype(BF16)
    w2s = jnp.stack([w_cmp_k2, w_cmp_v2]).astype(BF16)
    poss = jnp.stack([pos_cmp_k, pos_cmp_v]).reshape(2, 2, CMP_STRIDE, dh).transpose(0, 2, 1, 3)
    posr = jnp.pad(poss, ((0, 0), (0, 0), (0, 6), (0, 0))).astype(BF16)
    kvc, kvct = _compress(seg_a, w1r, w2s, posr, batch, seq)

    cmp_start = np.arange(n_sub) * CMP_STRIDE
    sel_start = np.arange(SEL_LANES) * SEL_BLOCK
    ovt = ((cmp_start[None, :] < sel_start[:, None] + SEL_BLOCK)
           & (cmp_start[None, :] + CMP_BLOCK - 1 >= sel_start[:, None])
           & (np.arange(n_sub)[None, :] < n_sub - CMP_BLOCK // CMP_STRIDE + 1))
    ovt = jnp.asarray(ovt, BF16)
    slopes = jnp.exp2(-8.0 * jnp.arange(1, NSA_HEADS + 1, dtype=F32) / NSA_HEADS)
    ocmp, pen, tiles = _nsa_cmp(seg_a, kvc, kvct, ovt, slopes, batch, seq)
    onehot, wext, dbias, ubias, wbias = _nsa_tables(slopes, seq)
    y_a = _nsa_attn(seg_a, seg_b, pen, ocmp, tiles, onehot, wext, dbias, ubias, wbias,
                    slopes, batch, seq)

    chunk = 256 if seq % 256 == 0 else 128
    bias = jnp.concatenate([b_igate, b_fgate]).astype(F32)
    tri = jnp.asarray(np.tril(np.ones((chunk, chunk), np.float32)), BF16)
    y_m = _mlstm(seg_a, seg_b, bias, mlstm_norm_g.reshape(1, -1), tri, batch, seq, chunk)

    w_o = w_out.astype(BF16)
    x1 = _outproj(x2, y_a, y_m, w_o[:NSA_WIDTH], w_o[NSA_WIDTH:], _row_tile(n, 512))
    return x1, (norm_mlp_g.reshape(1, d), w_mlp_in.astype(BF16), w_mlp_out.astype(BF16))


def kernel(x, norm_mix_g, w_in, w_cmp_k1, w_cmp_k2, pos_cmp_k, w_cmp_v1, w_cmp_v2, pos_cmp_v, conv_w, conv_b, b_igate, b_fgate, mlstm_norm_g, w_out, norm_mlp_g, w_mlp_in, w_mlp_out, norm_f_g):
    batch, seq, d = x.shape
    depth = w_in.shape[0]
    assert depth == 1, "the final RMSNorm is fused into the last layer's channel mixer"
    x2 = x.reshape(batch * seq, d)
    tm = _row_tile(batch * seq, ROW_TILE)
    for l in range(depth):
        x1, (g_mlp, w1, w2) = _layer(
            x2, batch, seq, norm_mix_g[l], w_in[l], w_cmp_k1[l], w_cmp_k2[l], pos_cmp_k[l],
            w_cmp_v1[l], w_cmp_v2[l], pos_cmp_v[l], conv_w[l], conv_b[l], b_igate[l], b_fgate[l],
            mlstm_norm_g[l], w_out[l], norm_mlp_g[l], w_mlp_in[l], w_mlp_out[l])
        x2 = _mlp(x1, g_mlp, w1, w2, norm_f_g.reshape(1, d), tm, 512)
    return x2.reshape(batch, seq, d)
```

```python
import functools

import numpy as np
import jax
import jax.numpy as jnp
from jax import lax
from jax.experimental import pallas as pl
from jax.experimental.pallas import tpu as pltpu

F32 = jnp.float32
BF16 = jnp.bfloat16

EPS = 1e-6
NEG = -1e30
FORCE_BONUS = 1e4
PICKED = -3e38
MASK_BIG = 1e30

D_MODEL = 2048
NSA_HEAD_DIM = 128
NSA_WIDTH = D_MODEL // 2
NSA_HEADS = NSA_WIDTH // NSA_HEAD_DIM
NSA_REP = 4
NSA_KV_GROUPS = NSA_HEADS // NSA_REP
NSA_KV_WIDTH = NSA_KV_GROUPS * NSA_HEAD_DIM
CMP_BLOCK = 32
CMP_STRIDE = 16
SEL_BLOCK = 64
SEL_TOPK = 16
WINDOW = 512
Q_BLOCK = 128
N_BRANCH = 3
MLSTM_HEAD_DIM = 256
MLSTM_WIDTH = D_MODEL - NSA_WIDTH
MLSTM_HEADS = MLSTM_WIDTH // MLSTM_HEAD_DIM
CONV_WIDTH = 4

LANE = 128
SEL_LANES = 128
KEY_TILE = 128
SEL_TILES_PER_STEP = 2
WIN_TILES = WINDOW // KEY_TILE
SUM_ROWS = 16
VMEM_LIMIT = 56 * 1024 * 1024
ROW_TILE = 1024

A_V, A_QK = 0, MLSTM_WIDTH
A_Q = A_QK + 2 * MLSTM_WIDTH
A_KV = A_Q + NSA_WIDTH
SEG_A = A_KV + 6 * NSA_KV_WIDTH
B_O = 0
B_GATE = MLSTM_WIDTH
B_IF = B_GATE + NSA_KV_GROUPS * LANE
SEG_B = B_IF + 2 * LANE
COL_TILE = 512

WCOL_PAD, WCOL_HI, WCOL_LO, WCOL_QHI, WCOL_QLO = 0, 1, 2, 3, 4


def _dot(a, b):
    return jnp.dot(a, b, preferred_element_type=F32)


def _dot_nt(a, b):
    return lax.dot_general(a, b, (((1,), (1,)), ((), ())), preferred_element_type=F32)


def _sigmoid(x):
    return 1.0 / (1.0 + jnp.exp(-x))


def _shifted(x, tail, s):
    xs = pltpu.roll(x, s, axis=0)
    ts = pltpu.roll(tail, s, axis=0)
    row8 = lax.broadcasted_iota(jnp.int32, (8, 1), 0)
    head = jnp.where(row8 < s, ts, xs[:8])
    return jnp.concatenate([head, xs[8:]], axis=0)


def _conv_silu(x, tail, w, b):
    y = b + _shifted(x, tail, CONV_WIDTH - 1) * w[0:1]
    for i in range(1, CONV_WIDTH - 1):
        y = y + _shifted(x, tail, CONV_WIDTH - 1 - i) * w[i:i + 1]
    y = y + x * w[CONV_WIDTH - 1:CONV_WIDTH]
    return y * _sigmoid(y)


def _inproj_kernel(x_ref, g_ref, w_ref, cs_ref, cw_ref, cb_ref, oa_ref, ob_ref, h_ref, halo_ref, *,
                   na, conv_lo, conv_hi, tiles_per_seq):
    i = pl.program_id(0)
    j = pl.program_id(1)

    @pl.when(j == 0)
    def _():
        x = x_ref[...]
        r = lax.rsqrt(jnp.mean(x * x, axis=-1, keepdims=True) + EPS)
        h_ref[...] = (x * r * g_ref[...]).astype(BF16)

    is_conv = (j >= conv_lo) & (j < conv_hi)

    @pl.when((j < na) & jnp.logical_not(is_conv))
    def _():
        oa_ref[...] = (_dot(h_ref[...], w_ref[...]) * cs_ref[...]).astype(oa_ref.dtype)

    @pl.when(j >= na)
    def _():
        ob_ref[...] = (_dot(h_ref[...], w_ref[...]) * cs_ref[...]).astype(ob_ref.dtype)

    @pl.when(is_conv)
    def _():
        slot = j - conv_lo

        @pl.when(i % tiles_per_seq == 0)
        def _():
            halo_ref[slot] = jnp.zeros(halo_ref.shape[1:], F32)

        acc = _dot(h_ref[...], w_ref[...])
        tail = halo_ref[slot]
        halo_ref[slot] = acc[acc.shape[0] - 8:]
        y = _conv_silu(acc, tail, cw_ref[...], cb_ref[...])
        oa_ref[...] = (y * cs_ref[...]).astype(oa_ref.dtype)


def _inproj(x2, g, w, cs, cw, cb, tm, tn, conv_cols, seq):
    n, d = x2.shape
    na, nb = SEG_A // tn, SEG_B // tn
    conv_lo, conv_hi = conv_cols[0] // tn, conv_cols[1] // tn
    assert conv_cols[0] % tn == 0 and conv_cols[1] % tn == 0 and seq % tm == 0 and conv_hi <= na
    assert SEG_A % tn == 0 and SEG_B % tn == 0 and w.shape[1] == SEG_A + SEG_B
    return pl.pallas_call(
        functools.partial(_inproj_kernel, na=na, conv_lo=conv_lo, conv_hi=conv_hi,
                          tiles_per_seq=seq // tm),
        grid=(n // tm, na + nb),
        in_specs=[
            pl.BlockSpec((tm, d), lambda i, j: (i, 0)),
            pl.BlockSpec((1, d), lambda i, j: (0, 0)),
            pl.BlockSpec((d, tn), lambda i, j: (0, j)),
            pl.BlockSpec((1, tn), lambda i, j: (0, j)),
            pl.BlockSpec((CONV_WIDTH, tn), lambda i, j: (0, j)),
            pl.BlockSpec((1, tn), lambda i, j: (0, j)),
        ],
        out_specs=[pl.BlockSpec((tm, tn), lambda i, j: (i, jnp.minimum(j, na - 1))),
                   pl.BlockSpec((tm, tn), lambda i, j: (i, jnp.maximum(j - na, 0)))],
        out_shape=[jax.ShapeDtypeStruct((n, SEG_A), BF16), jax.ShapeDtypeStruct((n, SEG_B), F32)],
        scratch_shapes=[pltpu.VMEM((tm, d), BF16),
                        pltpu.VMEM((conv_hi - conv_lo, 8, tn), F32)],
        compiler_params=pltpu.CompilerParams(
            dimension_semantics=("arbitrary", "arbitrary"), vmem_limit_bytes=VMEM_LIMIT),
        name="inproj",
    )(x2, g, w, cs, cw, cb)


def _compress_kernel(x_ref, w1_ref, w2_ref, pos_ref, o_ref, ot_ref, xf_ref):
    n_sub = o_ref.shape[2]
    dh = NSA_HEAD_DIM
    xf_ref[...] = x_ref[...].astype(F32)
    acc = jnp.zeros((n_sub, 2 * dh), F32)
    posw = jnp.zeros((1, dh), F32)
    for p in range(CMP_STRIDE):
        wp = w1_ref[0, p]
        acc = acc + _dot(xf_ref[pl.ds(p, n_sub, stride=CMP_STRIDE), :].astype(BF16), wp)
        pw = _dot(pos_ref[0, p], wp)
        posw = posw + pw[0:1, :dh] + pw[1:2, dh:]
    bot = pltpu.roll(acc[:, dh:], n_sub - 1, axis=0)
    pre = acc[:, :dh] + bot + posw
    hid = pre * _sigmoid(pre)
    out = _dot(hid.astype(BF16), w2_ref[0])
    o_ref[0, 0] = out.astype(o_ref.dtype)
    ot_ref[0, 0] = out.T.astype(ot_ref.dtype)


def _compress(seg_a, w1r, w2s, posr, batch, seq):
    g = NSA_KV_GROUPS
    c = 2 * g
    dh = NSA_HEAD_DIM
    n_sub = seq // CMP_STRIDE
    return pl.pallas_call(
        _compress_kernel,
        grid=(batch, c),
        in_specs=[
            pl.BlockSpec((seq, dh), lambda i, j: (i, A_KV // dh + j)),
            pl.BlockSpec((1, CMP_STRIDE, dh, 2 * dh), lambda i, j: (j // g, 0, 0, 0)),
            pl.BlockSpec((1, dh, dh), lambda i, j: (j // g, 0, 0)),
            pl.BlockSpec((1, CMP_STRIDE, 8, dh), lambda i, j: (j // g, 0, 0, 0)),
        ],
        out_specs=[pl.BlockSpec((1, 1, n_sub, dh), lambda i, j: (i, j, 0, 0)),
                   pl.BlockSpec((1, 1, dh, n_sub), lambda i, j: (i, j, 0, 0))],
        out_shape=[jax.ShapeDtypeStruct((batch, c, n_sub, dh), BF16),
                   jax.ShapeDtypeStruct((batch, c, dh, n_sub), BF16)],
        scratch_shapes=[pltpu.VMEM((seq, dh), F32)],
        compiler_params=pltpu.CompilerParams(
            dimension_semantics=("parallel", "parallel"), vmem_limit_bytes=VMEM_LIMIT),
        name="compress",
    )(seg_a, w1r, w2s, posr)


def _stack_heads(q_all):
    dh = NSA_HEAD_DIM
    return jnp.concatenate([q_all[:, r * dh:(r + 1) * dh] for r in range(NSA_REP)], axis=0)


def _tile_list(hit_col, tile0, pad_tile, mats_ref):
    n = SEL_LANES
    lane = lax.broadcasted_iota(jnp.int32, (1, n), 1)
    blk = lax.broadcasted_iota(jnp.int32, (n, n), 0)
    slot = lax.broadcasted_iota(jnp.int32, (n, n), 1).astype(F32)
    hit = jnp.broadcast_to(hit_col, (n, n))
    both = jnp.maximum(hit, pltpu.roll(hit, n - 1, axis=0))
    act = jnp.where((both > 0.0) & (blk % 2 == 0) & (blk // 2 < tile0), 1.0, 0.0)
    rank = _dot(mats_ref[0], act.astype(BF16))
    cnt = jnp.sum(act, axis=0, keepdims=True)
    place = jnp.where((act > 0.0) & (rank == slot), 1.0, 0.0)
    tiles = _dot(mats_ref[1, 0:8], place.astype(BF16))[0:1]
    out = jnp.where(lane.astype(F32) < cnt, tiles, float(pad_tile))
    return jnp.where(lane == n - 1, cnt, out).astype(jnp.int32)


def _nsa_cmp_kernel(slopes_ref, q_ref, kc_ref, vct_ref, ovt_ref, mats_ref, ocmp_ref, pen_ref, tiles_ref, *,
                    seq, sub_blocks):
    step = pl.program_id(2)
    nq_step = sub_blocks * Q_BLOCK
    chunk = min(SEL_LANES, kc_ref.shape[2])
    n_chunks = kc_ref.shape[2] // chunk
    need = ((step + 1) * nq_step - CMP_BLOCK) // CMP_STRIDE + 1
    n_need = (need + chunk - 1) // chunk
    for v in range(1, n_chunks + 1):
        cond = (n_need == v) if v < n_chunks else (n_need >= v)
        if v == 1:
            cond = n_need <= 1

        @pl.when(cond)
        def _(v=v):
            hit = None
            for sub in range(sub_blocks):
                rows = slice(sub * Q_BLOCK, (sub + 1) * Q_BLOCK)
                h = _nsa_cmp_block(slopes_ref, q_ref[rows, :], kc_ref, vct_ref, ovt_ref,
                                   ocmp_ref.at[0, 0, sub], pen_ref.at[0, 0, rows],
                                   step * sub_blocks + sub, seq, v * chunk)
                hit = h if hit is None else jnp.maximum(hit, h)
            tiles_ref[0, 0, 0] = _tile_list(hit, step * sub_blocks, seq // KEY_TILE, mats_ref)


def _nsa_cmp_block(slopes_ref, q_all, kc_ref, vct_ref, ovt_ref, ocmp_ref, pen_ref, qb, seq, n_use):
    g = pl.program_id(1)
    nq = Q_BLOCK
    n_cpad = n_use
    n_cmp = seq // CMP_STRIDE - CMP_BLOCK // CMP_STRIDE + 1
    t0 = qb * nq

    s_t = _dot_nt(kc_ref[0, 0, :n_use, :], _stack_heads(q_all))
    n_s = lax.broadcasted_iota(jnp.int32, (n_cpad, 1), 0)
    q_l = lax.broadcasted_iota(jnp.int32, (1, nq), 1)
    dist = (t0 - (CMP_BLOCK - 1)) + q_l - n_s * CMP_STRIDE
    valid = (dist >= 0) & (n_s < n_cmp)
    dist_f = dist.astype(F32)
    probs = []
    p_sum = jnp.zeros((n_cpad, nq), F32)
    for r in range(NSA_REP):
        slope = slopes_ref[g * NSA_REP + r]
        s = jnp.where(valid, s_t[:, r * nq:(r + 1) * nq] - slope * dist_f, NEG)
        m = jnp.max(s, axis=0, keepdims=True)
        e = jnp.exp(s - m)
        inv = jnp.where(m > 0.5 * NEG, 1.0 / jnp.sum(e, axis=0, keepdims=True), 0.0)
        p = e * inv
        probs.append(p.astype(BF16))
        p_sum = p_sum + p
    ocmp_ref[...] = _dot(vct_ref[0, 0, :, :n_use], jnp.concatenate(probs, axis=1))

    p_hi = p_sum.astype(BF16)
    p_lo = (p_sum - p_hi.astype(F32)).astype(BF16)
    ovt = ovt_ref[:, :n_use]
    imp = _dot(ovt, p_hi) + _dot(ovt, p_lo)
    j_i = lax.broadcasted_iota(jnp.int32, (SEL_LANES, 1), 0)
    t_l = t0 + q_l
    cur = t_l // SEL_BLOCK
    forced = (j_i == 0) | (j_i == cur) | (j_i == cur - 1)
    causal_blk = j_i * SEL_BLOCK <= t_l
    val = jnp.where(causal_blk, jnp.where(forced, imp + FORCE_BONUS, imp), NEG)
    j_f = j_i.astype(F32)
    sel_t = jnp.zeros((SEL_LANES, nq), F32)
    for _ in range(min(SEL_TOPK, seq // SEL_BLOCK)):
        mx = jnp.max(val, axis=0, keepdims=True)
        first = jnp.min(jnp.where(val == mx, j_f, float(SEL_LANES)), axis=0, keepdims=True)
        pick = j_f == first
        sel_t = jnp.where(pick, 1.0, sel_t)
        val = jnp.where(pick, PICKED, val)
    sel = sel_t.T
    pen_ref[...] = ((sel - 1.0) * MASK_BIG).astype(pen_ref.dtype)
    return jnp.max(sel_t, axis=1, keepdims=True)


def _nsa_cmp(seg_a, kvc, kvct, ovt, slopes, batch, seq):
    nqb = seq // Q_BLOCK
    gq = NSA_REP * NSA_HEAD_DIM
    g_ = NSA_KV_GROUPS
    n_cpad = kvc.shape[2]
    sub = _nsa_sub_blocks(seq)
    nstep = nqb // sub
    idx = np.arange(SEL_LANES)
    mats = jnp.asarray(np.stack([idx[:, None] > idx[None, :],
                                 np.broadcast_to(idx[None, :] // 2, (SEL_LANES, SEL_LANES))]), BF16)
    return pl.pallas_call(
        functools.partial(_nsa_cmp_kernel, seq=seq, sub_blocks=sub),
        grid=(batch, g_, nstep),
        in_specs=[
            pl.BlockSpec(memory_space=pltpu.SMEM),
            pl.BlockSpec((sub * Q_BLOCK, gq), lambda b, g, q: (b * nstep + q, A_Q // gq + g)),
            pl.BlockSpec((1, 1, n_cpad, NSA_HEAD_DIM), lambda b, g, q: (b, g, 0, 0)),
            pl.BlockSpec((1, 1, NSA_HEAD_DIM, n_cpad), lambda b, g, q: (b, g_ + g, 0, 0)),
            pl.BlockSpec((SEL_LANES, n_cpad), lambda b, g, q: (0, 0)),
            pl.BlockSpec((2, SEL_LANES, SEL_LANES), lambda b, g, q: (0, 0, 0)),
        ],
        out_specs=[
            pl.BlockSpec((1, 1, sub, NSA_HEAD_DIM, gq), lambda b, g, q: (b, g, q, 0, 0)),
            pl.BlockSpec((1, 1, sub * Q_BLOCK, SEL_LANES), lambda b, g, q: (b, g, q, 0)),
            pl.BlockSpec((1, 1, 1, 1, SEL_LANES), lambda b, g, q: (b, g, q, 0, 0)),
        ],
        out_shape=[
            jax.ShapeDtypeStruct((batch, g_, nqb, NSA_HEAD_DIM, gq), F32),
            jax.ShapeDtypeStruct((batch, g_, seq, SEL_LANES), BF16),
            jax.ShapeDtypeStruct((batch, g_, nstep, 1, SEL_LANES), jnp.int32),
        ],
        compiler_params=pltpu.CompilerParams(
            dimension_semantics=("parallel", "parallel", "parallel"), vmem_limit_bytes=VMEM_LIMIT),
        name="nsa_cmp",
    )(slopes, seg_a, kvc, kvct, ovt, mats)


def _nsa_attn_kernel(slopes_ref, tiles_ref, q_ref, gate_ref, pen_ref, ocmp_ref, ks_ref, vs_ref,
                     kw_ref, vw_ref, onehot_ref, wext_ref, dbias_ref, ubias_ref, wbias_ref, o_ref,
                     ksel_ref, vselt_ref, kwin_ref, vwint_ref,
                     qa_ref, qw_ref, m_ref, acc_ref, owin_ref, sa_ref, sb_ref,
                     *, seq, sub_blocks):
    g = pl.program_id(1)
    step = pl.program_id(2)
    dh = NSA_HEAD_DIM
    nq = sub_blocks * Q_BLOCK
    kt = KEY_TILE
    per = SEL_TILES_PER_STEP
    pad_tile = seq // kt
    t0 = step * nq
    tile0 = step * sub_blocks

    @pl.when(step == 0)
    def _():
        ksel_ref[0:seq, :dh] = ks_ref[...]
        ksel_ref[0:seq, dh:] = onehot_ref[...]
        ksel_ref[seq:, :dh] = jnp.zeros((kt, dh), BF16)
        ksel_ref[seq:, dh:] = jnp.ones((kt, SEL_LANES), BF16)
        lane2 = lax.broadcasted_iota(jnp.int32, (WINDOW, dh + LANE), 1)
        kwin_ref[0:WINDOW, :] = jnp.where(lane2 == dh + WCOL_PAD, 1.0, 0.0).astype(BF16)
        kwin_ref[WINDOW:, :dh] = kw_ref[...]
        kwin_ref[WINDOW:, dh:] = wext_ref[...]
        zero_tile = jnp.zeros((dh + SUM_ROWS, kt), BF16)
        ones_rows = jnp.ones((SUM_ROWS, kt), BF16)
        vselt_ref[pad_tile] = zero_tile
        for i in range(WIN_TILES):
            vwint_ref[i] = zero_tile

        def transpose_tile(t, carry):
            r0 = pl.multiple_of(t * kt, kt)
            vselt_ref[t, :dh] = vs_ref[pl.ds(r0, kt), :].astype(F32).T.astype(BF16)
            vselt_ref[t, dh:] = ones_rows
            vwint_ref[t + WIN_TILES, :dh] = vw_ref[pl.ds(r0, kt), :].astype(F32).T.astype(BF16)
            vwint_ref[t + WIN_TILES, dh:] = ones_rows
            return carry

        lax.fori_loop(0, seq // kt, transpose_tile, 0)

    slopes = [slopes_ref[g * NSA_REP + r] for r in range(NSA_REP)]
    row_q = lax.broadcasted_iota(jnp.int32, (nq, 1), 0)
    lane = lax.broadcasted_iota(jnp.int32, (1, LANE), 1)
    q_all = q_ref[...]

    pen = pen_ref[0, 0].astype(F32)
    blk_rel = ((lane - (t0 + row_q) // SEL_BLOCK) * SEL_BLOCK).astype(F32)
    for r in range(NSA_REP):
        qa_ref[r * nq:(r + 1) * nq, :dh] = q_all[:, r * dh:(r + 1) * dh]
        qa_ref[r * nq:(r + 1) * nq, dh:] = (pen + slopes[r] * blk_rel).astype(BF16)

    def listed(idx):
        return tiles_ref[0, 0, 0, 0, idx]

    cnt = listed(SEL_LANES - 1)

    def group_scores(it):
        keys = jnp.concatenate(
            [ksel_ref[pl.ds(pl.multiple_of(listed(it * per + i) * kt, kt), kt), :]
             for i in range(per)], axis=0)
        return _dot_nt(keys, qa_ref[...]) + ubias_ref[0]

    s = _dot_nt(ksel_ref[pl.ds(pl.multiple_of(t0, kt), nq), :], qa_ref[...]) + dbias_ref[0]
    m0 = jnp.max(s, axis=0, keepdims=True)
    p = jnp.exp(s - m0)
    m_ref[...] = m0
    vals = jnp.concatenate([vselt_ref[tile0 + a] for a in range(sub_blocks)], axis=1)
    acc_ref[...] = _dot(vals, p.astype(BF16))
    sa_ref[...] = group_scores(0)

    tp = t0 + WINDOW + row_q
    t_hi = (tp // SEL_BLOCK).astype(F32)
    t_lo = (tp % SEL_BLOCK).astype(F32)
    for r in range(NSA_REP):
        sl = slopes[r]
        ext = jnp.where(lane == WCOL_PAD, -MASK_BIG, 0.0)
        ext = jnp.where(lane == WCOL_HI, sl * SEL_BLOCK, ext)
        ext = jnp.where(lane == WCOL_LO, sl, ext)
        ext = jnp.where(lane == WCOL_QHI, -sl * SEL_BLOCK * t_hi, ext)
        ext = jnp.where(lane == WCOL_QLO, -sl * t_lo, ext)
        qw_ref[r * nq:(r + 1) * nq, :dh] = q_all[:, r * dh:(r + 1) * dh]
        qw_ref[r * nq:(r + 1) * nq, dh:] = ext.astype(BF16)
    wlen = WINDOW + nq
    s = _dot_nt(kwin_ref[pl.ds(pl.multiple_of(t0, kt), wlen), :], qw_ref[...]) + wbias_ref[...]
    e = jnp.exp(s - jnp.max(s, axis=0, keepdims=True))
    vwin = jnp.concatenate([vwint_ref[tile0 + i] for i in range(WIN_TILES + sub_blocks)], axis=1)
    ow = _dot(vwin, e.astype(BF16))
    owin_ref[...] = ow[:dh] / ow[dh:dh + 1]

    def absorb(s, grp):
        vals = jnp.concatenate([vselt_ref[listed(grp * per + i)] for i in range(per)], axis=1)
        m_old = m_ref[...]
        m_new = jnp.maximum(m_old, jnp.max(s, axis=0, keepdims=True))
        alpha = jnp.exp(m_old - m_new)
        p = jnp.exp(s - m_new)
        acc_ref[...] = alpha * acc_ref[...] + _dot(vals, p.astype(BF16))
        m_ref[...] = m_new

    def sel_body(it, carry):
        s = sa_ref[...]
        sb_ref[...] = group_scores(2 * it + 1)
        absorb(s, 2 * it)
        s = sb_ref[...]
        sa_ref[...] = group_scores(2 * it + 2)
        absorb(s, 2 * it + 1)
        return carry

    lax.fori_loop(0, (cnt + 2 * per - 1) // (2 * per), sel_body, 0)
    o_sel = acc_ref[:dh, :] / acc_ref[dh:dh + 1, :]
    o_win = owin_ref[...]

    gate_t = _sigmoid(gate_ref[...]).T
    for a in range(sub_blocks):
        o_cmp = ocmp_ref[0, 0, a]
        qs = slice(a * Q_BLOCK, (a + 1) * Q_BLOCK)
        for r in range(NSA_REP):
            cols = slice(r * nq + a * Q_BLOCK, r * nq + (a + 1) * Q_BLOCK)
            c0 = N_BRANCH * r
            out_t = (gate_t[c0:c0 + 1, qs] * o_cmp[:, r * Q_BLOCK:(r + 1) * Q_BLOCK]
                     + gate_t[c0 + 1:c0 + 2, qs] * o_sel[:, cols]
                     + gate_t[c0 + 2:c0 + 3, qs] * o_win[:, cols])
            o_ref[qs, r * dh:(r + 1) * dh] = out_t.T.astype(o_ref.dtype)


def _nsa_sub_blocks(seq):
    return 2 if (seq // Q_BLOCK) % 2 == 0 else 1


def _nsa_attn(seg_a, seg_b, pen, ocmp, tiles, onehot, wext, dbias, ubias, wbias, slopes, batch, seq):
    n = batch * seq
    sub = _nsa_sub_blocks(seq)
    nq = sub * Q_BLOCK
    nqb = seq // nq
    gq = NSA_REP * NSA_HEAD_DIM
    g_ = NSA_KV_GROUPS
    dh, kt = NSA_HEAD_DIM, KEY_TILE
    dk = dh + SEL_LANES
    rq = NSA_REP * nq
    n_t = seq // kt

    def kv_spec(kind):
        return pl.BlockSpec((seq, dh), lambda b, g, q, k=kind: (b, A_KV // dh + k * g_ + g))

    def const_spec(arr):
        return pl.BlockSpec(arr.shape, lambda b, g, q, nd=arr.ndim: (0,) * nd)

    return pl.pallas_call(
        functools.partial(_nsa_attn_kernel, seq=seq, sub_blocks=sub),
        grid=(batch, g_, nqb),
        in_specs=[
            pl.BlockSpec(memory_space=pltpu.SMEM),
            pl.BlockSpec((1, 1, 1, 1, SEL_LANES), lambda b, g, q: (b, g, q, 0, 0),
                         memory_space=pltpu.SMEM),
            pl.BlockSpec((nq, gq), lambda b, g, q: (b * nqb + q, A_Q // gq + g)),
            pl.BlockSpec((nq, LANE), lambda b, g, q: (b * nqb + q, B_GATE // LANE + g)),
            pl.BlockSpec((1, 1, nq, SEL_LANES), lambda b, g, q: (b, g, q, 0)),
            pl.BlockSpec((1, 1, sub, NSA_HEAD_DIM, gq), lambda b, g, q: (b, g, q, 0, 0)),
            kv_spec(2), kv_spec(3), kv_spec(4), kv_spec(5),
            const_spec(onehot), const_spec(wext),
            pl.BlockSpec((1,) + dbias.shape[1:], lambda b, g, q: (g, 0, 0)),
            pl.BlockSpec((1,) + ubias.shape[1:], lambda b, g, q: (g, 0, 0)),
            const_spec(wbias),
        ],
        out_specs=pl.BlockSpec((nq, gq), lambda b, g, q: (b * nqb + q, g)),
        out_shape=jax.ShapeDtypeStruct((n, NSA_WIDTH), BF16),
        scratch_shapes=[
            pltpu.VMEM((seq + kt, dk), BF16),
            pltpu.VMEM((n_t + 1, dh + SUM_ROWS, kt), BF16),
            pltpu.VMEM((seq + WINDOW, dh + LANE), BF16),
            pltpu.VMEM((n_t + WIN_TILES, dh + SUM_ROWS, kt), BF16),
            pltpu.VMEM((rq, dk), BF16),
            pltpu.VMEM((rq, dk), BF16),
            pltpu.VMEM((1, rq), F32),
            pltpu.VMEM((dh + SUM_ROWS, rq), F32),
            pltpu.VMEM((dh, rq), F32),
            pltpu.VMEM((SEL_TILES_PER_STEP * KEY_TILE, rq), F32),
            pltpu.VMEM((SEL_TILES_PER_STEP * KEY_TILE, rq), F32),
        ],
        compiler_params=pltpu.CompilerParams(
            dimension_semantics=("parallel", "parallel", "arbitrary"), vmem_limit_bytes=VMEM_LIMIT),
        name="nsa_attn",
    )(slopes, tiles, seg_a, seg_b, pen, ocmp, seg_a, seg_a, seg_a, seg_a, onehot, wext,
      dbias, ubias, wbias)


def _nsa_tables(slopes, seq):
    g_, kt = NSA_KV_GROUPS, KEY_TILE
    pos = np.arange(seq)
    onehot = jnp.asarray(pos[:, None] // SEL_BLOCK == np.arange(SEL_LANES)[None, :], BF16)
    ext = np.zeros((seq, LANE), np.float32)
    ext[:, WCOL_HI] = (pos + WINDOW) // SEL_BLOCK
    ext[:, WCOL_LO] = (pos + WINDOW) % SEL_BLOCK
    ext[:, WCOL_QHI] = 1.0
    ext[:, WCOL_QLO] = 1.0
    wext = jnp.asarray(ext, BF16)

    nq = _nsa_sub_blocks(seq) * Q_BLOCK

    def alibi_in_block(rows):
        u = jnp.asarray((np.arange(rows) % SEL_BLOCK).astype(np.float32))[None, :, None, None]
        t = jnp.broadcast_to(slopes.reshape(g_, 1, NSA_REP, 1) * u, (g_, rows, NSA_REP, nq))
        return t.reshape(g_, rows, NSA_REP * nq)

    ubias = alibi_in_block(SEL_TILES_PER_STEP * kt)
    kq = np.arange(nq)[:, None] <= np.arange(nq)[None, :]
    causal = np.tile(np.where(kq, 0.0, NEG).astype(np.float32), (1, NSA_REP))
    dbias = alibi_in_block(nq) + jnp.asarray(causal)[None]
    ki = np.arange(WINDOW + nq)[:, None]
    qi = np.arange(nq)[None, :]
    band = np.where((ki > qi) & (ki <= qi + WINDOW), 0.0, NEG).astype(np.float32)
    wbias = jnp.asarray(np.tile(band, (1, NSA_REP)))
    return onehot, wext, dbias, ubias, wbias


def _log_sigmoid(x):
    return jnp.minimum(x, 0.0) - jnp.log(1.0 + jnp.exp(-jnp.abs(x)))


def _split3(x):
    hi = x.astype(BF16)
    r1 = x - hi.astype(F32)
    mid = r1.astype(BF16)
    lo = (r1 - mid.astype(F32)).astype(BF16)
    return hi, mid, lo


def _mlstm_kernel(bias_ref, q_ref, k_ref, v_ref, o_ref, ifc_ref, ng_ref,
                  tri_ref, y_ref, c_ref, n_ref, m_ref):
    ch = pl.program_id(1)

    @pl.when(ch == 0)
    def _():
        c_ref[...] = jnp.zeros(c_ref.shape, F32)
        n_ref[...] = jnp.zeros(n_ref.shape, F32)
        m_ref[...] = jnp.zeros(m_ref.shape, F32)

    for sq in range(q_ref.shape[0]):
        _mlstm_chunk(bias_ref, q_ref.at[sq], k_ref.at[sq], v_ref.at[sq], o_ref.at[sq], ifc_ref.at[sq],
                     ng_ref, tri_ref, y_ref.at[sq], c_ref.at[sq], n_ref.at[sq], m_ref.at[sq])


def _mlstm_chunk(bias_ref, q_ref, k_ref, v_ref, o_ref, ifc_ref, ng_ref, tri_ref, y_ref,
                 c_ref, n_ref, m_ref):
    nh, dh = MLSTM_HEADS, MLSTM_HEAD_DIM
    L = q_ref.shape[0]
    tri = tri_ref[...]
    lane8 = lax.broadcasted_iota(jnp.int32, (1, LANE), 1)
    bias_c = jnp.zeros((1, LANE), F32)
    for h in range(nh):
        bias_c = jnp.where(lane8 == h, bias_ref[h], bias_c)
        bias_c = jnp.where(lane8 == nh + h, bias_ref[nh + h], bias_c)
    pre_c = ifc_ref[...] + bias_c
    cum_c = sum(_dot(tri, part) for part in _split3(_log_sigmoid(pre_c)))
    pre_r = pre_c.T[:2 * nh]
    cum_r = sum(_dot_nt(part, tri) for part in _split3(_log_sigmoid(pre_r)))

    rr = lax.broadcasted_iota(jnp.int32, (L, 1), 0)
    cc = lax.broadcasted_iota(jnp.int32, (1, L), 1)
    causal = cc <= rr

    for h in range(nh):
        cols = slice(h * dh, (h + 1) * dh)
        qb = q_ref[:, cols]
        kb = k_ref[:, cols]
        vh = v_ref[:, cols]
        qh = qb.astype(F32)
        kh = kb.astype(F32)
        b_c = cum_c[:, nh + h:nh + h + 1]
        li_c = pre_c[:, h:h + 1]
        b_r = cum_r[nh + h:nh + h + 1, :]
        li_r = pre_r[h:h + 1, :]
        m_prev = m_ref[h:h + 1, 0:1]

        dmat = jnp.where(causal, b_c - b_r + li_r, NEG)
        a = b_c + m_prev
        m_j = jnp.maximum(a, jnp.max(dmat, axis=1, keepdims=True))
        w_intra = jnp.exp(dmat - m_j)
        w_inter = jnp.exp(a - m_j)
        sc = _dot_nt(qb, kb) * w_intra
        c_old = c_ref[h]
        n_old = n_ref[h:h + 1, :]
        num = w_inter * _dot(qb, c_old.astype(BF16)) + _dot(sc.astype(BF16), vh)
        den = (w_inter * jnp.sum(qh * n_old, axis=1, keepdims=True)
               + jnp.sum(sc, axis=1, keepdims=True))
        hid = num / jnp.maximum(jnp.abs(den), jnp.exp(-m_j))

        g_tot = b_r[:, L - 1:L]
        lw_c = g_tot - b_c + li_c
        lw_r = g_tot - b_r + li_r
        m_new = jnp.maximum(g_tot + m_prev, jnp.max(lw_r, axis=1, keepdims=True))
        decay = jnp.exp(g_tot + m_prev - m_new)
        kw = jnp.exp(lw_c - m_new) * kh
        c_ref[h] = decay * c_old + _dot(kw.T.astype(BF16), vh)
        n_ref[h:h + 1, :] = decay * n_old + jnp.sum(kw, axis=0, keepdims=True)
        m_ref[h:h + 1, :] = jnp.broadcast_to(m_new, (1, LANE))

        hn = hid * lax.rsqrt(jnp.mean(hid * hid, axis=-1, keepdims=True) + EPS) * ng_ref[:, cols]
        y_ref[:, cols] = (_sigmoid(o_ref[:, cols]) * hn).astype(y_ref.dtype)


def _mlstm(seg_a, seg_b, bias, norm_g, tri, batch, seq, chunk):
    nc = seq // chunk
    w = MLSTM_WIDTH
    nh, dh = MLSTM_HEADS, MLSTM_HEAD_DIM
    seqs = 1
    a3 = seg_a.reshape(batch, seq, SEG_A)
    b3 = seg_b.reshape(batch, seq, SEG_B)

    def col_spec(off):
        return pl.BlockSpec((seqs, chunk, w), lambda b, c, o=off // w: (b, c, o))

    y = pl.pallas_call(
        _mlstm_kernel,
        grid=(batch // seqs, nc),
        in_specs=[
            pl.BlockSpec(memory_space=pltpu.SMEM),
            col_spec(A_QK), col_spec(A_QK + w), col_spec(A_V), col_spec(B_O),
            pl.BlockSpec((seqs, chunk, LANE), lambda b, c: (b, c, B_IF // LANE)),
            pl.BlockSpec((1, w), lambda b, c: (0, 0)),
            pl.BlockSpec((chunk, chunk), lambda b, c: (0, 0)),
        ],
        out_specs=pl.BlockSpec((seqs, chunk, w), lambda b, c: (b, c, 0)),
        out_shape=jax.ShapeDtypeStruct((batch, seq, w), BF16),
        scratch_shapes=[
            pltpu.VMEM((seqs, nh, dh, dh), F32),
            pltpu.VMEM((seqs, 8, dh), F32),
            pltpu.VMEM((seqs, 8, LANE), F32),
        ],
        compiler_params=pltpu.CompilerParams(
            dimension_semantics=("parallel", "arbitrary"), vmem_limit_bytes=VMEM_LIMIT),
        name="mlstm",
    )(bias, a3, a3, a3, b3, b3, norm_g, tri)
    return y.reshape(batch * seq, w)


def _outproj_kernel(x_ref, ya_ref, ym_ref, wa_ref, wm_ref, o_ref):
    o_ref[...] = x_ref[...] + _dot(ya_ref[...], wa_ref[...]) + _dot(ym_ref[...], wm_ref[...])


def _outproj(x2, ya, ym, wa, wm, tm):
    n, d = x2.shape
    return pl.pallas_call(
        _outproj_kernel,
        grid=(n // tm,),
        in_specs=[
            pl.BlockSpec((tm, d), lambda i: (i, 0)),
            pl.BlockSpec((tm, ya.shape[1]), lambda i: (i, 0)),
            pl.BlockSpec((tm, ym.shape[1]), lambda i: (i, 0)),
            pl.BlockSpec(wa.shape, lambda i: (0, 0)),
            pl.BlockSpec(wm.shape, lambda i: (0, 0)),
        ],
        out_specs=pl.BlockSpec((tm, d), lambda i: (i, 0)),
        out_shape=jax.ShapeDtypeStruct((n, d), F32),
        compiler_params=pltpu.CompilerParams(
            dimension_semantics=("parallel",), vmem_limit_bytes=VMEM_LIMIT),
        name="outproj",
    )(x2, ya, ym, wa, wm)


def _mlp_kernel(x_ref, g_ref, w1_ref, w2_ref, gf_ref, o_ref, h_ref):
    f = pl.program_id(1)

    @pl.when(f == 0)
    def _():
        x = x_ref[...]
        r = lax.rsqrt(jnp.mean(x * x, axis=-1, keepdims=True) + EPS)
        h_ref[...] = (x * r * g_ref[...]).astype(BF16)
        o_ref[...] = x

    u = jnp.maximum(_dot(h_ref[...], w1_ref[...]), 0.0)
    o_ref[...] += _dot((u * u).astype(BF16), w2_ref[...])

    @pl.when(f == pl.num_programs(1) - 1)
    def _():
        x2 = o_ref[...]
        r = lax.rsqrt(jnp.mean(x2 * x2, axis=-1, keepdims=True) + EPS)
        o_ref[...] = x2 * r * gf_ref[...]


def _mlp(x1, g, w1, w2, gf, tm, tf):
    n, d = x1.shape
    dff = w1.shape[1]
    return pl.pallas_call(
        _mlp_kernel,
        grid=(n // tm, dff // tf),
        in_specs=[
            pl.BlockSpec((tm, d), lambda i, f: (i, 0)),
            pl.BlockSpec((1, d), lambda i, f: (0, 0)),
            pl.BlockSpec((d, tf), lambda i, f: (0, f)),
            pl.BlockSpec((tf, d), lambda i, f: (f, 0)),
            pl.BlockSpec((1, d), lambda i, f: (0, 0)),
        ],
        out_specs=pl.BlockSpec((tm, d), lambda i, f: (i, 0)),
        out_shape=jax.ShapeDtypeStruct((n, d), F32),
        scratch_shapes=[pltpu.VMEM((tm, d), BF16)],
        compiler_params=pltpu.CompilerParams(
            dimension_semantics=("parallel", "arbitrary"), vmem_limit_bytes=VMEM_LIMIT),
        name="mlp",
    )(x1, g, w1, w2, gf)


def _row_tile(n, want):
    t = want
    while n % t:
        t //= 2
    return t


def _layer(x2, batch, seq, norm_mix_g, w_in, w_cmp_k1, w_cmp_k2, pos_cmp_k, w_cmp_v1, w_cmp_v2,
           pos_cmp_v, conv_w, conv_b, b_igate, b_fgate, mlstm_norm_g, w_out, norm_mlp_g,
           w_mlp_in, w_mlp_out):
    n, d = x2.shape
    assert seq % Q_BLOCK == 0 and seq >= WINDOW + Q_BLOCK and seq // SEL_BLOCK <= SEL_LANES
    g_ = NSA_KV_GROUPS
    nh = MLSTM_HEADS

    c_gate = NSA_WIDTH + 6 * NSA_KV_WIDTH
    c_qk = c_gate + NSA_HEADS * N_BRANCH
    c_v = c_qk + 2 * MLSTM_WIDTH
    c_o = c_v + MLSTM_WIDTH
    c_i = c_o + MLSTM_WIDTH
    c_f = c_i + nh
    w16 = w_in.astype(BF16)
    gate_cols = []
    per_g = NSA_REP * N_BRANCH
    for g in range(g_):
        gate_cols += [w16[:, c_gate + g * per_g:c_gate + (g + 1) * per_g],
                      jnp.zeros((d, LANE - per_g), BF16)]
    w_ab = jnp.concatenate(
        [w16[:, c_v:c_o], w16[:, c_qk:c_v], w16[:, :c_gate],
         w16[:, c_o:c_i]] + gate_cols
        + [w16[:, c_i:c_f + nh], jnp.zeros((d, SEG_B - B_IF - 2 * nh), BF16)],
        axis=1)
    scale = jnp.concatenate([jnp.ones((1, A_QK + MLSTM_WIDTH), F32),
                             jnp.full((1, MLSTM_WIDTH), MLSTM_HEAD_DIM ** -0.5, F32),
                             jnp.full((1, NSA_WIDTH), NSA_HEAD_DIM ** -0.5, F32),
                             jnp.ones((1, SEG_A - A_KV + SEG_B), F32)], axis=1)
    conv_pad = ((0, 0), (A_QK, SEG_A - A_Q + SEG_B))
    cw_ab = jnp.pad(conv_w, conv_pad)
    cb_ab = jnp.pad(conv_b.reshape(1, -1), conv_pad)
    g_mix = norm_mix_g.reshape(1, d)

    tm = _row_tile(seq, ROW_TILE)
    seg_a, seg_b = _inproj(x2, g_mix, w_ab, scale, cw_ab, cb_ab, tm, COL_TILE, (A_QK, A_Q), seq)

    n_sub = seq // CMP_STRIDE
    dh = NSA_HEAD_DIM
    w1s = jnp.stack([w_cmp_k1, w_cmp_v1]).reshape(2, 2, CMP_STRIDE, dh, dh)
    w1r = jnp.concatenate([w1s[:, 0], w1s[:, 1]], axis=-1).astype(BF16)
    w2s = jnp.stack([w_cmp_k2, w_cmp_v2]).astype(BF16)
    poss = jnp.stack([pos_cmp_k, pos_cmp_v]).reshape(2, 2, CMP_STRIDE, dh).transpose(0, 2, 1, 3)
    posr = jnp.pad(poss, ((0, 0), (0, 0), (0, 6), (0, 0))).astype(BF16)
    kvc, kvct = _compress(seg_a, w1r, w2s, posr, batch, seq)

    cmp_start = np.arange(n_sub) * CMP_STRIDE
    sel_start = np.arange(SEL_LANES) * SEL_BLOCK
    ovt = ((cmp_start[None, :] < sel_start[:, None] + SEL_BLOCK)
           & (cmp_start[None, :] + CMP_BLOCK - 1 >= sel_start[:, None])
           & (np.arange(n_sub)[None, :] < n_sub - CMP_BLOCK // CMP_STRIDE + 1))
    ovt = jnp.asarray(ovt, BF16)
    slopes = jnp.exp2(-8.0 * jnp.arange(1, NSA_HEADS + 1, dtype=F32) / NSA_HEADS)
    ocmp, pen, tiles = _nsa_cmp(seg_a, kvc, kvct, ovt, slopes, batch, seq)
    onehot, wext, dbias, ubias, wbias = _nsa_tables(slopes, seq)
    y_a = _nsa_attn(seg_a, seg_b, pen, ocmp, tiles, onehot, wext, dbias, ubias, wbias,
                    slopes, batch, seq)

    chunk = 256 if seq % 256 == 0 else 128
    bias = jnp.concatenate([b_igate, b_fgate]).astype(F32)
    tri = jnp.asarray(np.tril(np.ones((chunk, chunk), np.float32)), BF16)
    y_m = _mlstm(seg_a, seg_b, bias, mlstm_norm_g.reshape(1, -1), tri, batch, seq, chunk)

    w_o = w_out.astype(BF16)
    x1 = _outproj(x2, y_a, y_m, w_o[:NSA_WIDTH], w_o[NSA_WIDTH:], _row_tile(n, 512))
    return x1, (norm_mlp_g.reshape(1, d), w_mlp_in.astype(BF16), w_mlp_out.astype(BF16))


def kernel(x, norm_mix_g, w_in, w_cmp_k1, w_cmp_k2, pos_cmp_k, w_cmp_v1, w_cmp_v2, pos_cmp_v, conv_w, conv_b, b_igate, b_fgate, mlstm_norm_g, w_out, norm_mlp_g, w_mlp_in, w_mlp_out, norm_f_g):
    batch, seq, d = x.shape
    depth = w_in.shape[0]
    assert depth == 1, "the final RMSNorm is fused into the last layer's channel mixer"
    x2 = x.reshape(batch * seq, d)
    tm = _row_tile(batch * seq, ROW_TILE)
    for l in range(depth):
        x1, (g_mlp, w1, w2) = _layer(
            x2, batch, seq, norm_mix_g[l], w_in[l], w_cmp_k1[l], w_cmp_k2[l], pos_cmp_k[l],
            w_cmp_v1[l], w_cmp_v2[l], pos_cmp_v[l], conv_w[l], conv_b[l], b_igate[l], b_fgate[l],
            mlstm_norm_g[l], w_out[l], norm_mlp_g[l], w_mlp_in[l], w_mlp_out[l])
        x2 = _mlp(x1, g_mlp, w1, w2, norm_f_g.reshape(1, d), tm, 512)
    return x2.reshape(batch, seq, d)
```

```python
import functools

import numpy as np
import jax
import jax.numpy as jnp
from jax import lax
from jax.experimental import pallas as pl
from jax.experimental.pallas import tpu as pltpu

F32 = jnp.float32
BF16 = jnp.bfloat16

EPS = 1e-6
NEG = -1e30
FORCE_BONUS = 1e4
PICKED = -3e38
MASK_BIG = 1e30

D_MODEL = 2048
NSA_HEAD_DIM = 128
NSA_WIDTH = D_MODEL // 2
NSA_HEADS = NSA_WIDTH // NSA_HEAD_DIM
NSA_REP = 4
NSA_KV_GROUPS = NSA_HEADS // NSA_REP
NSA_KV_WIDTH = NSA_KV_GROUPS * NSA_HEAD_DIM
CMP_BLOCK = 32
CMP_STRIDE = 16
SEL_BLOCK = 64
SEL_TOPK = 16
WINDOW = 512
Q_BLOCK = 128
N_BRANCH = 3
MLSTM_HEAD_DIM = 256
MLSTM_WIDTH = D_MODEL - NSA_WIDTH
MLSTM_HEADS = MLSTM_WIDTH // MLSTM_HEAD_DIM
CONV_WIDTH = 4

LANE = 128
SEL_LANES = 128
KEY_TILE = 128
SEL_TILES_PER_STEP = 2
WIN_TILES = WINDOW // KEY_TILE
SUM_ROWS = 16
VMEM_LIMIT = 56 * 1024 * 1024
ROW_TILE = 1024

A_V, A_QK = 0, MLSTM_WIDTH
A_Q = A_QK + 2 * MLSTM_WIDTH
A_KV = A_Q + NSA_WIDTH
SEG_A = A_KV + 6 * NSA_KV_WIDTH
B_O = 0
B_GATE = MLSTM_WIDTH
B_IF = B_GATE + NSA_KV_GROUPS * LANE
SEG_B = B_IF + 2 * LANE
COL_TILE = 512

WCOL_PAD, WCOL_HI, WCOL_LO, WCOL_QHI, WCOL_QLO = 0, 1, 2, 3, 4


def _dot(a, b):
    return jnp.dot(a, b, preferred_element_type=F32)


def _dot_nt(a, b):
    return lax.dot_general(a, b, (((1,), (1,)), ((), ())), preferred_element_type=F32)


def _sigmoid(x):
    return 1.0 / (1.0 + jnp.exp(-x))


def _shifted(x, tail, s):
    xs = pltpu.roll(x, s, axis=0)
    ts = pltpu.roll(tail, s, axis=0)
    row8 = lax.broadcasted_iota(jnp.int32, (8, 1), 0)
    head = jnp.where(row8 < s, ts, xs[:8])
    return jnp.concatenate([head, xs[8:]], axis=0)


def _conv_silu(x, tail, w, b):
    y = b + _shifted(x, tail, CONV_WIDTH - 1) * w[0:1]
    for i in range(1, CONV_WIDTH - 1):
        y = y + _shifted(x, tail, CONV_WIDTH - 1 - i) * w[i:i + 1]
    y = y + x * w[CONV_WIDTH - 1:CONV_WIDTH]
    return y * _sigmoid(y)


def _inproj_kernel(x_ref, g_ref, w_ref, cs_ref, cw_ref, cb_ref, oa_ref, ob_ref, h_ref, halo_ref, *,
                   na, conv_lo, conv_hi, tiles_per_seq):
    i = pl.program_id(0)
    j = pl.program_id(1)

    @pl.when(j == 0)
    def _():
        x = x_ref[...]
        r = lax.rsqrt(jnp.mean(x * x, axis=-1, keepdims=True) + EPS)
        h_ref[...] = (x * r * g_ref[...]).astype(BF16)

    is_conv = (j >= conv_lo) & (j < conv_hi)

    @pl.when((j < na) & jnp.logical_not(is_conv))
    def _():
        oa_ref[...] = (_dot(h_ref[...], w_ref[...]) * cs_ref[...]).astype(oa_ref.dtype)

    @pl.when(j >= na)
    def _():
        ob_ref[...] = (_dot(h_ref[...], w_ref[...]) * cs_ref[...]).astype(ob_ref.dtype)

    @pl.when(is_conv)
    def _():
        slot = j - conv_lo

        @pl.when(i % tiles_per_seq == 0)
        def _():
            halo_ref[slot] = jnp.zeros(halo_ref.shape[1:], F32)

        acc = _dot(h_ref[...], w_ref[...])
        tail = halo_ref[slot]
        halo_ref[slot] = acc[acc.shape[0] - 8:]
        y = _conv_silu(acc, tail, cw_ref[...], cb_ref[...])
        oa_ref[...] = (y * cs_ref[...]).astype(oa_ref.dtype)


def _inproj(x2, g, w, cs, cw, cb, tm, tn, conv_cols, seq):
    n, d = x2.shape
    na, nb = SEG_A // tn, SEG_B // tn
    conv_lo, conv_hi = conv_cols[0] // tn, conv_cols[1] // tn
    assert conv_cols[0] % tn == 0 and conv_cols[1] % tn == 0 and seq % tm == 0 and conv_hi <= na
    assert SEG_A % tn == 0 and SEG_B % tn == 0 and w.shape[1] == SEG_A + SEG_B
    return pl.pallas_call(
        functools.partial(_inproj_kernel, na=na, conv_lo=conv_lo, conv_hi=conv_hi,
                          tiles_per_seq=seq // tm),
        grid=(n // tm, na + nb),
        in_specs=[
            pl.BlockSpec((tm, d), lambda i, j: (i, 0)),
            pl.BlockSpec((1, d), lambda i, j: (0, 0)),
            pl.BlockSpec((d, tn), lambda i, j: (0, j)),
            pl.BlockSpec((1, tn), lambda i, j: (0, j)),
            pl.BlockSpec((CONV_WIDTH, tn), lambda i, j: (0, j)),
            pl.BlockSpec((1, tn), lambda i, j: (0, j)),
        ],
        out_specs=[pl.BlockSpec((tm, tn), lambda i, j: (i, jnp.minimum(j, na - 1))),
                   pl.BlockSpec((tm, tn), lambda i, j: (i, jnp.maximum(j - na, 0)))],
        out_shape=[jax.ShapeDtypeStruct((n, SEG_A), BF16), jax.ShapeDtypeStruct((n, SEG_B), F32)],
        scratch_shapes=[pltpu.VMEM((tm, d), BF16),
                        pltpu.VMEM((conv_hi - conv_lo, 8, tn), F32)],
        compiler_params=pltpu.CompilerParams(
            dimension_semantics=("arbitrary", "arbitrary"), vmem_limit_bytes=VMEM_LIMIT),
        name="inproj",
    )(x2, g, w, cs, cw, cb)


def _compress_kernel(x_ref, w1_ref, w2_ref, pos_ref, o_ref, ot_ref, xf_ref):
    n_sub = o_ref.shape[2]
    dh = NSA_HEAD_DIM
    xf_ref[...] = x_ref[...].astype(F32)
    acc = jnp.zeros((n_sub, 2 * dh), F32)
    posw = jnp.zeros((1, dh), F32)
    for p in range(CMP_STRIDE):
        wp = w1_ref[0, p]
        acc = acc + _dot(xf_ref[pl.ds(p, n_sub, stride=CMP_STRIDE), :].astype(BF16), wp)
        pw = _dot(pos_ref[0, p], wp)
        posw = posw + pw[0:1, :dh] + pw[1:2, dh:]
    bot = pltpu.roll(acc[:, dh:], n_sub - 1, axis=0)
    pre = acc[:, :dh] + bot + posw
    hid = pre * _sigmoid(pre)
    out = _dot(hid.astype(BF16), w2_ref[0])
    o_ref[0, 0] = out.astype(o_ref.dtype)
    ot_ref[0, 0] = out.T.astype(ot_ref.dtype)


def _compress(seg_a, w1r, w2s, posr, batch, seq):
    g = NSA_KV_GROUPS
    c = 2 * g
    dh = NSA_HEAD_DIM
    n_sub = seq // CMP_STRIDE
    return pl.pallas_call(
        _compress_kernel,
        grid=(batch, c),
        in_specs=[
            pl.BlockSpec((seq, dh), lambda i, j: (i, A_KV // dh + j)),
            pl.BlockSpec((1, CMP_STRIDE, dh, 2 * dh), lambda i, j: (j // g, 0, 0, 0)),
            pl.BlockSpec((1, dh, dh), lambda i, j: (j // g, 0, 0)),
            pl.BlockSpec((1, CMP_STRIDE, 8, dh), lambda i, j: (j // g, 0, 0, 0)),
        ],
        out_specs=[pl.BlockSpec((1, 1, n_sub, dh), lambda i, j: (i, j, 0, 0)),
                   pl.BlockSpec((1, 1, dh, n_sub), lambda i, j: (i, j, 0, 0))],
        out_shape=[jax.ShapeDtypeStruct((batch, c, n_sub, dh), BF16),
                   jax.ShapeDtypeStruct((batch, c, dh, n_sub), BF16)],
        scratch_shapes=[pltpu.VMEM((seq, dh), F32)],
        compiler_params=pltpu.CompilerParams(
            dimension_semantics=("parallel", "parallel"), vmem_limit_bytes=VMEM_LIMIT),
        name="compress",
    )(seg_a, w1r, w2s, posr)


def _stack_heads(q_all):
    dh = NSA_HEAD_DIM
    return jnp.concatenate([q_all[:, r * dh:(r + 1) * dh] for r in range(NSA_REP)], axis=0)


def _tile_list(hit_col, tile0, pad_tile, mats_ref):
    n = SEL_LANES
    lane = lax.broadcasted_iota(jnp.int32, (1, n), 1)
    blk = lax.broadcasted_iota(jnp.int32, (n, n), 0)
    slot = lax.broadcasted_iota(jnp.int32, (n, n), 1).astype(F32)
    hit = jnp.broadcast_to(hit_col, (n, n))
    both = jnp.maximum(hit, pltpu.roll(hit, n - 1, axis=0))
    act = jnp.where((both > 0.0) & (blk % 2 == 0) & (blk // 2 < tile0), 1.0, 0.0)
    rank = _dot(mats_ref[0], act.astype(BF16))
    cnt = jnp.sum(act, axis=0, keepdims=True)
    place = jnp.where((act > 0.0) & (rank == slot), 1.0, 0.0)
    tiles = _dot(mats_ref[1, 0:8], place.astype(BF16))[0:1]
    out = jnp.where(lane.astype(F32) < cnt, tiles, float(pad_tile))
    return jnp.where(lane == n - 1, cnt, out).astype(jnp.int32)


def _nsa_cmp_kernel(slopes_ref, q_ref, kc_ref, vct_ref, ovt_ref, mats_ref, ocmp_ref, pen_ref, tiles_ref, *,
                    seq, sub_blocks):
    step = pl.program_id(2)
    nq_step = sub_blocks * Q_BLOCK
    chunk = min(SEL_LANES, kc_ref.shape[2])
    n_chunks = kc_ref.shape[2] // chunk
    need = ((step + 1) * nq_step - CMP_BLOCK) // CMP_STRIDE + 1
    n_need = (need + chunk - 1) // chunk
    for v in range(1, n_chunks + 1):
        cond = (n_need == v) if v < n_chunks else (n_need >= v)
        if v == 1:
            cond = n_need <= 1

        @pl.when(cond)
        def _(v=v):
            hit = None
            for sub in range(sub_blocks):
                rows = slice(sub * Q_BLOCK, (sub + 1) * Q_BLOCK)
                h = _nsa_cmp_block(slopes_ref, q_ref[rows, :], kc_ref, vct_ref, ovt_ref,
                                   ocmp_ref.at[0, 0, sub], pen_ref.at[0, 0, rows],
                                   step * sub_blocks + sub, seq, v * chunk,
                                   past_first_block=(sub >= 1 or v >= 2))
                hit = h if hit is None else jnp.maximum(hit, h)
            tiles_ref[0, 0, 0] = _tile_list(hit, step * sub_blocks, seq // KEY_TILE, mats_ref)


def _nsa_cmp_block(slopes_ref, q_all, kc_ref, vct_ref, ovt_ref, ocmp_ref, pen_ref, qb, seq, n_use,
                   past_first_block):
    g = pl.program_id(1)
    nq = Q_BLOCK
    n_cpad = n_use
    n_cmp = seq // CMP_STRIDE - CMP_BLOCK // CMP_STRIDE + 1
    t0 = qb * nq

    s_t = _dot_nt(kc_ref[0, 0, :n_use, :], _stack_heads(q_all))
    n_s = lax.broadcasted_iota(jnp.int32, (n_cpad, 1), 0)
    q_l = lax.broadcasted_iota(jnp.int32, (1, nq), 1)
    dist = (t0 - (CMP_BLOCK - 1)) + q_l - n_s * CMP_STRIDE
    valid = (dist >= 0) & (n_s < n_cmp)
    dist_f = dist.astype(F32)
    probs = []
    p_sum = jnp.zeros((n_cpad, nq), F32)
    for r in range(NSA_REP):
        slope = slopes_ref[g * NSA_REP + r]
        s = jnp.where(valid, s_t[:, r * nq:(r + 1) * nq] - slope * dist_f, NEG)
        m = jnp.max(s, axis=0, keepdims=True)
        e = jnp.exp(s - m)
        inv = jnp.where(m > 0.5 * NEG, 1.0 / jnp.sum(e, axis=0, keepdims=True), 0.0)
        p = e * inv
        probs.append(p.astype(BF16))
        p_sum = p_sum + p
    ocmp_ref[...] = _dot(vct_ref[0, 0, :, :n_use], jnp.concatenate(probs, axis=1))

    p_hi = p_sum.astype(BF16)
    p_lo = (p_sum - p_hi.astype(F32)).astype(BF16)
    ovt = ovt_ref[:, :n_use]
    imp = _dot(ovt, p_hi) + _dot(ovt, p_lo)
    j_i = lax.broadcasted_iota(jnp.int32, (SEL_LANES, 1), 0)
    t_l = t0 + q_l
    cur = t_l // SEL_BLOCK
    forced = (j_i == 0) | (j_i == cur) | (j_i == cur - 1)
    causal_blk = j_i * SEL_BLOCK <= t_l
    val = jnp.where(causal_blk, jnp.where(forced, imp + FORCE_BONUS, imp), NEG)
    j_f = j_i.astype(F32)
    sel_t = jnp.zeros((SEL_LANES, nq), F32)
    rounds = min(SEL_TOPK, seq // SEL_BLOCK)
    if past_first_block and rounds >= 3:
        sel_t = jnp.where(forced, 1.0, sel_t)
        val = jnp.where(forced, PICKED, val)
        rounds -= 3
    for _ in range(rounds):
        mx = jnp.max(val, axis=0, keepdims=True)
        first = jnp.min(jnp.where(val == mx, j_f, float(SEL_LANES)), axis=0, keepdims=True)
        pick = j_f == first
        sel_t = jnp.where(pick, 1.0, sel_t)
        val = jnp.where(pick, PICKED, val)
    sel = sel_t.T
    pen_ref[...] = ((sel - 1.0) * MASK_BIG).astype(pen_ref.dtype)
    return jnp.max(sel_t, axis=1, keepdims=True)


def _nsa_cmp(seg_a, kvc, kvct, ovt, slopes, batch, seq):
    nqb = seq // Q_BLOCK
    gq = NSA_REP * NSA_HEAD_DIM
    g_ = NSA_KV_GROUPS
    n_cpad = kvc.shape[2]
    sub = _nsa_sub_blocks(seq)
    nstep = nqb // sub
    idx = np.arange(SEL_LANES)
    mats = jnp.asarray(np.stack([idx[:, None] > idx[None, :],
                                 np.broadcast_to(idx[None, :] // 2, (SEL_LANES, SEL_LANES))]), BF16)
    return pl.pallas_call(
        functools.partial(_nsa_cmp_kernel, seq=seq, sub_blocks=sub),
        grid=(batch, g_, nstep),
        in_specs=[
            pl.BlockSpec(memory_space=pltpu.SMEM),
            pl.BlockSpec((sub * Q_BLOCK, gq), lambda b, g, q: (b * nstep + q, A_Q // gq + g)),
            pl.BlockSpec((1, 1, n_cpad, NSA_HEAD_DIM), lambda b, g, q: (b, g, 0, 0)),
            pl.BlockSpec((1, 1, NSA_HEAD_DIM, n_cpad), lambda b, g, q: (b, g_ + g, 0, 0)),
            pl.BlockSpec((SEL_LANES, n_cpad), lambda b, g, q: (0, 0)),
            pl.BlockSpec((2, SEL_LANES, SEL_LANES), lambda b, g, q: (0, 0, 0)),
        ],
        out_specs=[
            pl.BlockSpec((1, 1, sub, NSA_HEAD_DIM, gq), lambda b, g, q: (b, g, q, 0, 0)),
            pl.BlockSpec((1, 1, sub * Q_BLOCK, SEL_LANES), lambda b, g, q: (b, g, q, 0)),
            pl.BlockSpec((1, 1, 1, 1, SEL_LANES), lambda b, g, q: (b, g, q, 0, 0)),
        ],
        out_shape=[
            jax.ShapeDtypeStruct((batch, g_, nqb, NSA_HEAD_DIM, gq), F32),
            jax.ShapeDtypeStruct((batch, g_, seq, SEL_LANES), BF16),
            jax.ShapeDtypeStruct((batch, g_, nstep, 1, SEL_LANES), jnp.int32),
        ],
        compiler_params=pltpu.CompilerParams(
            dimension_semantics=("parallel", "parallel", "parallel"), vmem_limit_bytes=VMEM_LIMIT),
        name="nsa_cmp",
    )(slopes, seg_a, kvc, kvct, ovt, mats)


def _nsa_attn_kernel(slopes_ref, tiles_ref, q_ref, gate_ref, pen_ref, ocmp_ref, ks_ref, vs_ref,
                     kw_ref, vw_ref, onehot_ref, wext_ref, dbias_ref, ubias_ref, wbias_ref, o_ref,
                     ksel_ref, vselt_ref, kwin_ref, vwint_ref,
                     qa_ref, qw_ref, m_ref, acc_ref, owin_ref, sa_ref, sb_ref,
                     *, seq, sub_blocks):
    g = pl.program_id(1)
    step = pl.program_id(2)
    dh = NSA_HEAD_DIM
    nq = sub_blocks * Q_BLOCK
    kt = KEY_TILE
    per = SEL_TILES_PER_STEP
    pad_tile = seq // kt
    t0 = step * nq
    tile0 = step * sub_blocks

    @pl.when(step == 0)
    def _():
        ksel_ref[0:seq, :dh] = ks_ref[...]
        ksel_ref[0:seq, dh:] = onehot_ref[...]
        ksel_ref[seq:, :dh] = jnp.zeros((kt, dh), BF16)
        ksel_ref[seq:, dh:] = jnp.ones((kt, SEL_LANES), BF16)
        lane2 = lax.broadcasted_iota(jnp.int32, (WINDOW, dh + LANE), 1)
        kwin_ref[0:WINDOW, :] = jnp.where(lane2 == dh + WCOL_PAD, 1.0, 0.0).astype(BF16)
        kwin_ref[WINDOW:, :dh] = kw_ref[...]
        kwin_ref[WINDOW:, dh:] = wext_ref[...]
        zero_tile = jnp.zeros((dh + SUM_ROWS, kt), BF16)
        ones_rows = jnp.ones((SUM_ROWS, kt), BF16)
        vselt_ref[pad_tile] = zero_tile
        for i in range(WIN_TILES):
            vwint_ref[i] = zero_tile

        def transpose_tile(t, carry):
            r0 = pl.multiple_of(t * kt, kt)
            vselt_ref[t, :dh] = vs_ref[pl.ds(r0, kt), :].astype(F32).T.astype(BF16)
            vselt_ref[t, dh:] = ones_rows
            vwint_ref[t + WIN_TILES, :dh] = vw_ref[pl.ds(r0, kt), :].astype(F32).T.astype(BF16)
            vwint_ref[t + WIN_TILES, dh:] = ones_rows
            return carry

        lax.fori_loop(0, seq // kt, transpose_tile, 0)

    slopes = [slopes_ref[g * NSA_REP + r] for r in range(NSA_REP)]
    row_q = lax.broadcasted_iota(jnp.int32, (nq, 1), 0)
    lane = lax.broadcasted_iota(jnp.int32, (1, LANE), 1)
    q_all = q_ref[...]

    pen = pen_ref[0, 0].astype(F32)
    blk_rel = ((lane - (t0 + row_q) // SEL_BLOCK) * SEL_BLOCK).astype(F32)
    for r in range(NSA_REP):
        qa_ref[r * nq:(r + 1) * nq, :dh] = q_all[:, r * dh:(r + 1) * dh]
        qa_ref[r * nq:(r + 1) * nq, dh:] = (pen + slopes[r] * blk_rel).astype(BF16)

    def listed(idx):
        return tiles_ref[0, 0, 0, 0, idx]

    cnt = listed(SEL_LANES - 1)

    def group_scores(it):
        keys = jnp.concatenate(
            [ksel_ref[pl.ds(pl.multiple_of(listed(it * per + i) * kt, kt), kt), :]
             for i in range(per)], axis=0)
        return _dot_nt(keys, qa_ref[...]) + ubias_ref[0]

    s = _dot_nt(ksel_ref[pl.ds(pl.multiple_of(t0, kt), nq), :], qa_ref[...]) + dbias_ref[0]
    m0 = jnp.max(s, axis=0, keepdims=True)
    p = jnp.exp(s - m0)
    m_ref[...] = m0
    vals = jnp.concatenate([vselt_ref[tile0 + a] for a in range(sub_blocks)], axis=1)
    acc_ref[...] = _dot(vals, p.astype(BF16))
    sa_ref[...] = group_scores(0)

    tp = t0 + WINDOW + row_q
    t_hi = (tp // SEL_BLOCK).astype(F32)
    t_lo = (tp % SEL_BLOCK).astype(F32)
    for r in range(NSA_REP):
        sl = slopes[r]
        ext = jnp.where(lane == WCOL_PAD, -MASK_BIG, 0.0)
        ext = jnp.where(lane == WCOL_HI, sl * SEL_BLOCK, ext)
        ext = jnp.where(lane == WCOL_LO, sl, ext)
        ext = jnp.where(lane == WCOL_QHI, -sl * SEL_BLOCK * t_hi, ext)
        ext = jnp.where(lane == WCOL_QLO, -sl * t_lo, ext)
        qw_ref[r * nq:(r + 1) * nq, :dh] = q_all[:, r * dh:(r + 1) * dh]
        qw_ref[r * nq:(r + 1) * nq, dh:] = ext.astype(BF16)
    wlen = WINDOW + nq
    s = _dot_nt(kwin_ref[pl.ds(pl.multiple_of(t0, kt), wlen), :], qw_ref[...]) + wbias_ref[...]
    e = jnp.exp(s - jnp.max(s, axis=0, keepdims=True))
    vwin = jnp.concatenate([vwint_ref[tile0 + i] for i in range(WIN_TILES + sub_blocks)], axis=1)
    ow = _dot(vwin, e.astype(BF16))
    owin_ref[...] = ow[:dh] / ow[dh:dh + 1]

    def absorb(s, grp):
        vals = jnp.concatenate([vselt_ref[listed(grp * per + i)] for i in range(per)], axis=1)
        m_old = m_ref[...]
        m_new = jnp.maximum(m_old, jnp.max(s, axis=0, keepdims=True))
        alpha = jnp.exp(m_old - m_new)
        p = jnp.exp(s - m_new)
        acc_ref[...] = alpha * acc_ref[...] + _dot(vals, p.astype(BF16))
        m_ref[...] = m_new

    def sel_body(it, carry):
        s = sa_ref[...]
        sb_ref[...] = group_scores(2 * it + 1)
        absorb(s, 2 * it)
        s = sb_ref[...]
        sa_ref[...] = group_scores(2 * it + 2)
        absorb(s, 2 * it + 1)
        return carry

    lax.fori_loop(0, (cnt + 2 * per - 1) // (2 * per), sel_body, 0)
    o_sel = acc_ref[:dh, :] / acc_ref[dh:dh + 1, :]
    o_win = owin_ref[...]

    gate_t = _sigmoid(gate_ref[...]).T
    for a in range(sub_blocks):
        o_cmp = ocmp_ref[0, 0, a]
        qs = slice(a * Q_BLOCK, (a + 1) * Q_BLOCK)
        for r in range(NSA_REP):
            cols = slice(r * nq + a * Q_BLOCK, r * nq + (a + 1) * Q_BLOCK)
            c0 = N_BRANCH * r
            out_t = (gate_t[c0:c0 + 1, qs] * o_cmp[:, r * Q_BLOCK:(r + 1) * Q_BLOCK]
                     + gate_t[c0 + 1:c0 + 2, qs] * o_sel[:, cols]
                     + gate_t[c0 + 2:c0 + 3, qs] * o_win[:, cols])
            o_ref[qs, r * dh:(r + 1) * dh] = out_t.T.astype(o_ref.dtype)


def _nsa_sub_blocks(seq):
    return 2 if (seq // Q_BLOCK) % 2 == 0 else 1


def _nsa_attn(seg_a, seg_b, pen, ocmp, tiles, onehot, wext, dbias, ubias, wbias, slopes, batch, seq):
    n = batch * seq
    sub = _nsa_sub_blocks(seq)
    nq = sub * Q_BLOCK
    nqb = seq // nq
    gq = NSA_REP * NSA_HEAD_DIM
    g_ = NSA_KV_GROUPS
    dh, kt = NSA_HEAD_DIM, KEY_TILE
    dk = dh + SEL_LANES
    rq = NSA_REP * nq
    n_t = seq // kt

    def kv_spec(kind):
        return pl.BlockSpec((seq, dh), lambda b, g, q, k=kind: (b, A_KV // dh + k * g_ + g))

    def const_spec(arr):
        return pl.BlockSpec(arr.shape, lambda b, g, q, nd=arr.ndim: (0,) * nd)

    return pl.pallas_call(
        functools.partial(_nsa_attn_kernel, seq=seq, sub_blocks=sub),
        grid=(batch, g_, nqb),
        in_specs=[
            pl.BlockSpec(memory_space=pltpu.SMEM),
            pl.BlockSpec((1, 1, 1, 1, SEL_LANES), lambda b, g, q: (b, g, q, 0, 0),
                         memory_space=pltpu.SMEM),
            pl.BlockSpec((nq, gq), lambda b, g, q: (b * nqb + q, A_Q // gq + g)),
            pl.BlockSpec((nq, LANE), lambda b, g, q: (b * nqb + q, B_GATE // LANE + g)),
            pl.BlockSpec((1, 1, nq, SEL_LANES), lambda b, g, q: (b, g, q, 0)),
            pl.BlockSpec((1, 1, sub, NSA_HEAD_DIM, gq), lambda b, g, q: (b, g, q, 0, 0)),
            kv_spec(2), kv_spec(3), kv_spec(4), kv_spec(5),
            const_spec(onehot), const_spec(wext),
            pl.BlockSpec((1,) + dbias.shape[1:], lambda b, g, q: (g, 0, 0)),
            pl.BlockSpec((1,) + ubias.shape[1:], lambda b, g, q: (g, 0, 0)),
            const_spec(wbias),
        ],
        out_specs=pl.BlockSpec((nq, gq), lambda b, g, q: (b * nqb + q, g)),
        out_shape=jax.ShapeDtypeStruct((n, NSA_WIDTH), BF16),
        scratch_shapes=[
            pltpu.VMEM((seq + kt, dk), BF16),
            pltpu.VMEM((n_t + 1, dh + SUM_ROWS, kt), BF16),
            pltpu.VMEM((seq + WINDOW, dh + LANE), BF16),
            pltpu.VMEM((n_t + WIN_TILES, dh + SUM_ROWS, kt), BF16),
            pltpu.VMEM((rq, dk), BF16),
            pltpu.VMEM((rq, dk), BF16),
            pltpu.VMEM((1, rq), F32),
            pltpu.VMEM((dh + SUM_ROWS, rq), F32),
            pltpu.VMEM((dh, rq), F32),
            pltpu.VMEM((SEL_TILES_PER_STEP * KEY_TILE, rq), F32),
            pltpu.VMEM((SEL_TILES_PER_STEP * KEY_TILE, rq), F32),
        ],
        compiler_params=pltpu.CompilerParams(
            dimension_semantics=("parallel", "parallel", "arbitrary"), vmem_limit_bytes=VMEM_LIMIT),
        name="nsa_attn",
    )(slopes, tiles, seg_a, seg_b, pen, ocmp, seg_a, seg_a, seg_a, seg_a, onehot, wext,
      dbias, ubias, wbias)


def _nsa_tables(slopes, seq):
    g_, kt = NSA_KV_GROUPS, KEY_TILE
    pos = np.arange(seq)
    onehot = jnp.asarray(pos[:, None] // SEL_BLOCK == np.arange(SEL_LANES)[None, :], BF16)
    ext = np.zeros((seq, LANE), np.float32)
    ext[:, WCOL_HI] = (pos + WINDOW) // SEL_BLOCK
    ext[:, WCOL_LO] = (pos + WINDOW) % SEL_BLOCK
    ext[:, WCOL_QHI] = 1.0
    ext[:, WCOL_QLO] = 1.0
    wext = jnp.asarray(ext, BF16)

    nq = _nsa_sub_blocks(seq) * Q_BLOCK

    def alibi_in_block(rows):
        u = jnp.asarray((np.arange(rows) % SEL_BLOCK).astype(np.float32))[None, :, None, None]
        t = jnp.broadcast_to(slopes.reshape(g_, 1, NSA_REP, 1) * u, (g_, rows, NSA_REP, nq))
        return t.reshape(g_, rows, NSA_REP * nq)

    ubias = alibi_in_block(SEL_TILES_PER_STEP * kt)
    kq = np.arange(nq)[:, None] <= np.arange(nq)[None, :]
    causal = np.tile(np.where(kq, 0.0, NEG).astype(np.float32), (1, NSA_REP))
    dbias = alibi_in_block(nq) + jnp.asarray(causal)[None]
    ki = np.arange(WINDOW + nq)[:, None]
    qi = np.arange(nq)[None, :]
    band = np.where((ki > qi) & (ki <= qi + WINDOW), 0.0, NEG).astype(np.float32)
    wbias = jnp.asarray(np.tile(band, (1, NSA_REP)))
    return onehot, wext, dbias, ubias, wbias


def _log_sigmoid(x):
    return jnp.minimum(x, 0.0) - jnp.log(1.0 + jnp.exp(-jnp.abs(x)))


def _split3(x):
    hi = x.astype(BF16)
    r1 = x - hi.astype(F32)
    mid = r1.astype(BF16)
    lo = (r1 - mid.astype(F32)).astype(BF16)
    return hi, mid, lo


def _mlstm_kernel(bias_ref, q_ref, k_ref, v_ref, o_ref, ifc_ref, ng_ref,
                  tri_ref, y_ref, c_ref, n_ref, m_ref):
    ch = pl.program_id(1)

    @pl.when(ch == 0)
    def _():
        c_ref[...] = jnp.zeros(c_ref.shape, F32)
        n_ref[...] = jnp.zeros(n_ref.shape, F32)
        m_ref[...] = jnp.zeros(m_ref.shape, F32)

    for sq in range(q_ref.shape[0]):
        _mlstm_chunk(bias_ref, q_ref.at[sq], k_ref.at[sq], v_ref.at[sq], o_ref.at[sq], ifc_ref.at[sq],
                     ng_ref, tri_ref, y_ref.at[sq], c_ref.at[sq], n_ref.at[sq], m_ref.at[sq])


def _mlstm_chunk(bias_ref, q_ref, k_ref, v_ref, o_ref, ifc_ref, ng_ref, tri_ref, y_ref,
                 c_ref, n_ref, m_ref):
    nh, dh = MLSTM_HEADS, MLSTM_HEAD_DIM
    L = q_ref.shape[0]
    tri = tri_ref[...]
    lane8 = lax.broadcasted_iota(jnp.int32, (1, LANE), 1)
    bias_c = jnp.zeros((1, LANE), F32)
    for h in range(nh):
        bias_c = jnp.where(lane8 == h, bias_ref[h], bias_c)
        bias_c = jnp.where(lane8 == nh + h, bias_ref[nh + h], bias_c)
    pre_c = ifc_ref[...] + bias_c
    cum_c = sum(_dot(tri, part) for part in _split3(_log_sigmoid(pre_c)))
    pre_r = pre_c.T[:2 * nh]
    cum_r = sum(_dot_nt(part, tri) for part in _split3(_log_sigmoid(pre_r)))

    rr = lax.broadcasted_iota(jnp.int32, (L, 1), 0)
    cc = lax.broadcasted_iota(jnp.int32, (1, L), 1)
    causal = cc <= rr

    for h in range(nh):
        cols = slice(h * dh, (h + 1) * dh)
        qb = q_ref[:, cols]
        kb = k_ref[:, cols]
        vh = v_ref[:, cols]
        qh = qb.astype(F32)
        kh = kb.astype(F32)
        b_c = cum_c[:, nh + h:nh + h + 1]
        li_c = pre_c[:, h:h + 1]
        b_r = cum_r[nh + h:nh + h + 1, :]
        li_r = pre_r[h:h + 1, :]
        m_prev = m_ref[h:h + 1, 0:1]

        dmat = jnp.where(causal, b_c - b_r + li_r, NEG)
        a = b_c + m_prev
        m_j = jnp.maximum(a, jnp.max(dmat, axis=1, keepdims=True))
        w_intra = jnp.exp(dmat - m_j)
        w_inter = jnp.exp(a - m_j)
        sc = _dot_nt(qb, kb) * w_intra
        c_old = c_ref[h]
        n_old = n_ref[h:h + 1, :]
        num = w_inter * _dot(qb, c_old.astype(BF16)) + _dot(sc.astype(BF16), vh)
        den = (w_inter * jnp.sum(qh * n_old, axis=1, keepdims=True)
               + jnp.sum(sc, axis=1, keepdims=True))
        hid = num / jnp.maximum(jnp.abs(den), jnp.exp(-m_j))

        g_tot = b_r[:, L - 1:L]
        lw_c = g_tot - b_c + li_c
        lw_r = g_tot - b_r + li_r
        m_new = jnp.maximum(g_tot + m_prev, jnp.max(lw_r, axis=1, keepdims=True))
        decay = jnp.exp(g_tot + m_prev - m_new)
        kw = jnp.exp(lw_c - m_new) * kh
        c_ref[h] = decay * c_old + _dot(kw.T.astype(BF16), vh)
        n_ref[h:h + 1, :] = decay * n_old + jnp.sum(kw, axis=0, keepdims=True)
        m_ref[h:h + 1, :] = jnp.broadcast_to(m_new, (1, LANE))

        hn = hid * lax.rsqrt(jnp.mean(hid * hid, axis=-1, keepdims=True) + EPS) * ng_ref[:, cols]
        y_ref[:, cols] = (_sigmoid(o_ref[:, cols]) * hn).astype(y_ref.dtype)


def _mlstm(seg_a, seg_b, bias, norm_g, tri, batch, seq, chunk):
    nc = seq // chunk
    w = MLSTM_WIDTH
    nh, dh = MLSTM_HEADS, MLSTM_HEAD_DIM
    seqs = 1
    a3 = seg_a.reshape(batch, seq, SEG_A)
    b3 = seg_b.reshape(batch, seq, SEG_B)

    def col_spec(off):
        return pl.BlockSpec((seqs, chunk, w), lambda b, c, o=off // w: (b, c, o))

    y = pl.pallas_call(
        _mlstm_kernel,
        grid=(batch // seqs, nc),
        in_specs=[
            pl.BlockSpec(memory_space=pltpu.SMEM),
            col_spec(A_QK), col_spec(A_QK + w), col_spec(A_V), col_spec(B_O),
            pl.BlockSpec((seqs, chunk, LANE), lambda b, c: (b, c, B_IF // LANE)),
            pl.BlockSpec((1, w), lambda b, c: (0, 0)),
            pl.BlockSpec((chunk, chunk), lambda b, c: (0, 0)),
        ],
        out_specs=pl.BlockSpec((seqs, chunk, w), lambda b, c: (b, c, 0)),
        out_shape=jax.ShapeDtypeStruct((batch, seq, w), BF16),
        scratch_shapes=[
            pltpu.VMEM((seqs, nh, dh, dh), F32),
            pltpu.VMEM((seqs, 8, dh), F32),
            pltpu.VMEM((seqs, 8, LANE), F32),
        ],
        compiler_params=pltpu.CompilerParams(
            dimension_semantics=("parallel", "arbitrary"), vmem_limit_bytes=VMEM_LIMIT),
        name="mlstm",
    )(bias, a3, a3, a3, b3, b3, norm_g, tri)
    return y.reshape(batch * seq, w)


def _outproj_kernel(x_ref, ya_ref, ym_ref, wa_ref, wm_ref, o_ref):
    o_ref[...] = x_ref[...] + _dot(ya_ref[...], wa_ref[...]) + _dot(ym_ref[...], wm_ref[...])


def _outproj(x2, ya, ym, wa, wm, tm):
    n, d = x2.shape
    return pl.pallas_call(
        _outproj_kernel,
        grid=(n // tm,),
        in_specs=[
            pl.BlockSpec((tm, d), lambda i: (i, 0)),
            pl.BlockSpec((tm, ya.shape[1]), lambda i: (i, 0)),
            pl.BlockSpec((tm, ym.shape[1]), lambda i: (i, 0)),
            pl.BlockSpec(wa.shape, lambda i: (0, 0)),
            pl.BlockSpec(wm.shape, lambda i: (0, 0)),
        ],
        out_specs=pl.BlockSpec((tm, d), lambda i: (i, 0)),
        out_shape=jax.ShapeDtypeStruct((n, d), F32),
        compiler_params=pltpu.CompilerParams(
            dimension_semantics=("parallel",), vmem_limit_bytes=VMEM_LIMIT),
        name="outproj",
    )(x2, ya, ym, wa, wm)


def _mlp_kernel(x_ref, g_ref, w1_ref, w2_ref, gf_ref, o_ref, h_ref):
    f = pl.program_id(1)

    @pl.when(f == 0)
    def _():
        x = x_ref[...]
        r = lax.rsqrt(jnp.mean(x * x, axis=-1, keepdims=True) + EPS)
        h_ref[...] = (x * r * g_ref[...]).astype(BF16)
        o_ref[...] = x

    u = jnp.maximum(_dot(h_ref[...], w1_ref[...]), 0.0)
    o_ref[...] += _dot((u * u).astype(BF16), w2_ref[...])

    @pl.when(f == pl.num_programs(1) - 1)
    def _():
        x2 = o_ref[...]
        r = lax.rsqrt(jnp.mean(x2 * x2, axis=-1, keepdims=True) + EPS)
        o_ref[...] = x2 * r * gf_ref[...]


def _mlp(x1, g, w1, w2, gf, tm, tf):
    n, d = x1.shape
    dff = w1.shape[1]
    return pl.pallas_call(
        _mlp_kernel,
        grid=(n // tm, dff // tf),
        in_specs=[
            pl.BlockSpec((tm, d), lambda i, f: (i, 0)),
            pl.BlockSpec((1, d), lambda i, f: (0, 0)),
            pl.BlockSpec((d, tf), lambda i, f: (0, f)),
            pl.BlockSpec((tf, d), lambda i, f: (f, 0)),
            pl.BlockSpec((1, d), lambda i, f: (0, 0)),
        ],
        out_specs=pl.BlockSpec((tm, d), lambda i, f: (i, 0)),
        out_shape=jax.ShapeDtypeStruct((n, d), F32),
        scratch_shapes=[pltpu.VMEM((tm, d), BF16)],
        compiler_params=pltpu.CompilerParams(
            dimension_semantics=("parallel", "arbitrary"), vmem_limit_bytes=VMEM_LIMIT),
        name="mlp",
    )(x1, g, w1, w2, gf)


def _row_tile(n, want):
    t = want
    while n % t:
        t //= 2
    return t


def _layer(x2, batch, seq, norm_mix_g, w_in, w_cmp_k1, w_cmp_k2, pos_cmp_k, w_cmp_v1, w_cmp_v2,
           pos_cmp_v, conv_w, conv_b, b_igate, b_fgate, mlstm_norm_g, w_out, norm_mlp_g,
           w_mlp_in, w_mlp_out):
    n, d = x2.shape
    assert seq % Q_BLOCK == 0 and seq >= WINDOW + Q_BLOCK and seq // SEL_BLOCK <= SEL_LANES
    g_ = NSA_KV_GROUPS
    nh = MLSTM_HEADS

    c_gate = NSA_WIDTH + 6 * NSA_KV_WIDTH
    c_qk = c_gate + NSA_HEADS * N_BRANCH
    c_v = c_qk + 2 * MLSTM_WIDTH
    c_o = c_v + MLSTM_WIDTH
    c_i = c_o + MLSTM_WIDTH
    c_f = c_i + nh
    w16 = w_in.astype(BF16)
    gate_cols = []
    per_g = NSA_REP * N_BRANCH
    for g in range(g_):
        gate_cols += [w16[:, c_gate + g * per_g:c_gate + (g + 1) * per_g],
                      jnp.zeros((d, LANE - per_g), BF16)]
    w_ab = jnp.concatenate(
        [w16[:, c_v:c_o], w16[:, c_qk:c_v], w16[:, :c_gate],
         w16[:, c_o:c_i]] + gate_cols
        + [w16[:, c_i:c_f + nh], jnp.zeros((d, SEG_B - B_IF - 2 * nh), BF16)],
        axis=1)
    scale = jnp.concatenate([jnp.ones((1, A_QK + MLSTM_WIDTH), F32),
                             jnp.full((1, MLSTM_WIDTH), MLSTM_HEAD_DIM ** -0.5, F32),
                             jnp.full((1, NSA_WIDTH), NSA_HEAD_DIM ** -0.5, F32),
                             jnp.ones((1, SEG_A - A_KV + SEG_B), F32)], axis=1)
    conv_pad = ((0, 0), (A_QK, SEG_A - A_Q + SEG_B))
    cw_ab = jnp.pad(conv_w, conv_pad)
    cb_ab = jnp.pad(conv_b.reshape(1, -1), conv_pad)
    g_mix = norm_mix_g.reshape(1, d)

    tm = _row_tile(seq, ROW_TILE)
    seg_a, seg_b = _inproj(x2, g_mix, w_ab, scale, cw_ab, cb_ab, tm, COL_TILE, (A_QK, A_Q), seq)

    n_sub = seq // CMP_STRIDE
    dh = NSA_HEAD_DIM
    w1s = jnp.stack([w_cmp_k1, w_cmp_v1]).reshape(2, 2, CMP_STRIDE, dh, dh)
    w1r = jnp.concatenate([w1s[:, 0], w1s[:, 1]], axis=-1).astype(BF16)
    w2s = jnp.stack([w_cmp_k2, w_cmp_v2]).astype(BF16)
    poss = jnp.stack([pos_cmp_k, pos_cmp_v]).reshape(2, 2, CMP_STRIDE, dh).transpose(0, 2, 1, 3)
    posr = jnp.pad(poss, ((0, 0), (0, 0), (0, 6), (0, 0))).astype(BF16)
    kvc, kvct = _compress(seg_a, w1r, w2s, posr, batch, seq)

    cmp_start = np.arange(n_sub) * CMP_STRIDE
    sel_start = np.arange(SEL_LANES) * SEL_BLOCK
    ovt = ((cmp_start[None, :] < sel_start[:, None] + SEL_BLOCK)
           & (cmp_start[None, :] + CMP_BLOCK - 1 >= sel_start[:, None])
           & (np.arange(n_sub)[None, :] < n_sub - CMP_BLOCK // CMP_STRIDE + 1))
    ovt = jnp.asarray(ovt, BF16)
    slopes = jnp.exp2(-8.0 * jnp.arange(1, NSA_HEADS + 1, dtype=F32) / NSA_HEADS)
    ocmp, pen, tiles = _nsa_cmp(seg_a, kvc, kvct, ovt, slopes, batch, seq)
    onehot, wext, dbias, ubias, wbias = _nsa_tables(slopes, seq)
    y_a = _nsa_attn(seg_a, seg_b, pen, ocmp, tiles, onehot, wext, dbias, ubias, wbias,
                    slopes, batch, seq)

    chunk = 256 if seq % 256 == 0 else 128
    bias = jnp.concatenate([b_igate, b_fgate]).astype(F32)
    tri = jnp.asarray(np.tril(np.ones((chunk, chunk), np.float32)), BF16)
    y_m = _mlstm(seg_a, seg_b, bias, mlstm_norm_g.reshape(1, -1), tri, batch, seq, chunk)

    w_o = w_out.astype(BF16)
    x1 = _outproj(x2, y_a, y_m, w_o[:NSA_WIDTH], w_o[NSA_WIDTH:], _row_tile(n, 512))
    return x1, (norm_mlp_g.reshape(1, d), w_mlp_in.astype(BF16), w_mlp_out.astype(BF16))


def kernel(x, norm_mix_g, w_in, w_cmp_k1, w_cmp_k2, pos_cmp_k, w_cmp_v1, w_cmp_v2, pos_cmp_v, conv_w, conv_b, b_igate, b_fgate, mlstm_norm_g, w_out, norm_mlp_g, w_mlp_in, w_mlp_out, norm_f_g):
    batch, seq, d = x.shape
    depth = w_in.shape[0]
    assert depth == 1, "the final RMSNorm is fused into the last layer's channel mixer"
    x2 = x.reshape(batch * seq, d)
    tm = _row_tile(batch * seq, ROW_TILE)
    for l in range(depth):
        x1, (g_mlp, w1, w2) = _layer(
            x2, batch, seq, norm_mix_g[l], w_in[l], w_cmp_k1[l], w_cmp_k2[l], pos_cmp_k[l],
            w_cmp_v1[l], w_cmp_v2[l], pos_cmp_v[l], conv_w[l], conv_b[l], b_igate[l], b_fgate[l],
            mlstm_norm_g[l], w_out[l], norm_mlp_g[l], w_mlp_in[l], w_mlp_out[l])
        x2 = _mlp(x1, g_mlp, w1, w2, norm_f_g.reshape(1, d), tm, 512)
    return x2.reshape(batch, seq, d)
```

```python
import functools

import numpy as np
import jax
import jax.numpy as jnp
from jax import lax
from jax.experimental import pallas as pl
from jax.experimental.pallas import tpu as pltpu

F32 = jnp.float32
BF16 = jnp.bfloat16

EPS = 1e-6
NEG = -1e30
FORCE_BONUS = 1e4
PICKED = -3e38
MASK_BIG = 1e30

D_MODEL = 2048
NSA_HEAD_DIM = 128
NSA_WIDTH = D_MODEL // 2
NSA_HEADS = NSA_WIDTH // NSA_HEAD_DIM
NSA_REP = 4
NSA_KV_GROUPS = NSA_HEADS // NSA_REP
NSA_KV_WIDTH = NSA_KV_GROUPS * NSA_HEAD_DIM
CMP_BLOCK = 32
CMP_STRIDE = 16
SEL_BLOCK = 64
SEL_TOPK = 16
WINDOW = 512
Q_BLOCK = 128
N_BRANCH = 3
MLSTM_HEAD_DIM = 256
MLSTM_WIDTH = D_MODEL - NSA_WIDTH
MLSTM_HEADS = MLSTM_WIDTH // MLSTM_HEAD_DIM
CONV_WIDTH = 4

LANE = 128
SEL_LANES = 128
KEY_TILE = 128
SEL_TILES_PER_STEP = 2
WIN_TILES = WINDOW // KEY_TILE
SUM_ROWS = 16
VMEM_LIMIT = 56 * 1024 * 1024
ROW_TILE = 1024

A_V, A_QK = 0, MLSTM_WIDTH
A_Q = A_QK + 2 * MLSTM_WIDTH
A_KV = A_Q + NSA_WIDTH
SEG_A = A_KV + 6 * NSA_KV_WIDTH
B_O = 0
B_GATE = MLSTM_WIDTH
B_IF = B_GATE + NSA_KV_GROUPS * LANE
SEG_B = B_IF + 2 * LANE
COL_TILE = 512

WCOL_PAD, WCOL_HI, WCOL_LO, WCOL_QHI, WCOL_QLO = 0, 1, 2, 3, 4


def _dot(a, b):
    return jnp.dot(a, b, preferred_element_type=F32)


def _dot_nt(a, b):
    return lax.dot_general(a, b, (((1,), (1,)), ((), ())), preferred_element_type=F32)


def _sigmoid(x):
    return 1.0 / (1.0 + jnp.exp(-x))


def _shifted(x, tail, s):
    xs = pltpu.roll(x, s, axis=0)
    ts = pltpu.roll(tail, s, axis=0)
    row8 = lax.broadcasted_iota(jnp.int32, (8, 1), 0)
    head = jnp.where(row8 < s, ts, xs[:8])
    return jnp.concatenate([head, xs[8:]], axis=0)


def _conv_silu(x, tail, w, b):
    y = b + _shifted(x, tail, CONV_WIDTH - 1) * w[0:1]
    for i in range(1, CONV_WIDTH - 1):
        y = y + _shifted(x, tail, CONV_WIDTH - 1 - i) * w[i:i + 1]
    y = y + x * w[CONV_WIDTH - 1:CONV_WIDTH]
    return y * _sigmoid(y)


def _inproj_kernel(x_ref, g_ref, w_ref, cs_ref, cw_ref, cb_ref, oa_ref, ob_ref, h_ref, halo_ref, *,
                   nb, n_plain, tiles_per_seq):
    i = pl.program_id(0)
    j = pl.program_id(1)

    @pl.when(j == 0)
    def _():
        x = x_ref[...]
        r = lax.rsqrt(jnp.mean(x * x, axis=-1, keepdims=True) + EPS)
        h_ref[...] = (x * r * g_ref[...]).astype(BF16)

    is_conv = j >= nb + n_plain

    @pl.when((j >= nb) & jnp.logical_not(is_conv))
    def _():
        oa_ref[...] = (_dot(h_ref[...], w_ref[...]) * cs_ref[...]).astype(oa_ref.dtype)

    @pl.when(j < nb)
    def _():
        ob_ref[...] = (_dot(h_ref[...], w_ref[...]) * cs_ref[...]).astype(ob_ref.dtype)

    @pl.when(is_conv)
    def _():
        slot = j - (nb + n_plain)

        @pl.when(i % tiles_per_seq == 0)
        def _():
            halo_ref[slot] = jnp.zeros(halo_ref.shape[1:], F32)

        acc = _dot(h_ref[...], w_ref[...])
        tail = halo_ref[slot]
        halo_ref[slot] = acc[acc.shape[0] - 8:]
        y = _conv_silu(acc, tail, cw_ref[...], cb_ref[...])
        oa_ref[...] = (y * cs_ref[...]).astype(oa_ref.dtype)


def _inproj(x2, g, w, cs, cw, cb, tm, tn, conv_cols, seq):
    n, d = x2.shape
    na, nb = SEG_A // tn, SEG_B // tn
    conv_lo, conv_hi = conv_cols[0] // tn, conv_cols[1] // tn
    assert conv_cols[0] % tn == 0 and conv_cols[1] % tn == 0 and seq % tm == 0 and conv_hi <= na
    assert SEG_A % tn == 0 and SEG_B % tn == 0 and w.shape[1] == SEG_A + SEG_B
    n_conv = conv_hi - conv_lo
    n_plain = na - n_conv

    def a_tile(j):
        p = jnp.maximum(j - nb, 0)
        plain = jnp.where(p < conv_lo, p, p + n_conv)
        return jnp.where(p < n_plain, plain, conv_lo + p - n_plain)

    def w_tile(i, j):
        return (0, jnp.where(j < nb, na + j, a_tile(j)))

    return pl.pallas_call(
        functools.partial(_inproj_kernel, nb=nb, n_plain=n_plain, tiles_per_seq=seq // tm),
        grid=(n // tm, na + nb),
        in_specs=[
            pl.BlockSpec((tm, d), lambda i, j: (i, 0)),
            pl.BlockSpec((1, d), lambda i, j: (0, 0)),
            pl.BlockSpec((d, tn), w_tile),
            pl.BlockSpec((1, tn), w_tile),
            pl.BlockSpec((CONV_WIDTH, tn), w_tile),
            pl.BlockSpec((1, tn), w_tile),
        ],
        out_specs=[pl.BlockSpec((tm, tn), lambda i, j: (i, a_tile(j))),
                   pl.BlockSpec((tm, tn), lambda i, j: (i, jnp.minimum(j, nb - 1)))],
        out_shape=[jax.ShapeDtypeStruct((n, SEG_A), BF16), jax.ShapeDtypeStruct((n, SEG_B), F32)],
        scratch_shapes=[pltpu.VMEM((tm, d), BF16),
                        pltpu.VMEM((conv_hi - conv_lo, 8, tn), F32)],
        compiler_params=pltpu.CompilerParams(
            dimension_semantics=("arbitrary", "arbitrary"), vmem_limit_bytes=VMEM_LIMIT),
        name="inproj",
    )(x2, g, w, cs, cw, cb)


def _compress_kernel(x_ref, w1_ref, w2_ref, pos_ref, o_ref, ot_ref, xf_ref):
    n_sub = o_ref.shape[2]
    dh = NSA_HEAD_DIM
    xf_ref[...] = x_ref[...].astype(F32)
    acc = jnp.zeros((n_sub, 2 * dh), F32)
    posw = jnp.zeros((1, dh), F32)
    for p in range(CMP_STRIDE):
        wp = w1_ref[0, p]
        acc = acc + _dot(xf_ref[pl.ds(p, n_sub, stride=CMP_STRIDE), :].astype(BF16), wp)
        pw = _dot(pos_ref[0, p], wp)
        posw = posw + pw[0:1, :dh] + pw[1:2, dh:]
    bot = pltpu.roll(acc[:, dh:], n_sub - 1, axis=0)
    pre = acc[:, :dh] + bot + posw
    hid = pre * _sigmoid(pre)
    out = _dot(hid.astype(BF16), w2_ref[0])
    o_ref[0, 0] = out.astype(o_ref.dtype)
    ot_ref[0, 0] = out.T.astype(ot_ref.dtype)


def _compress(seg_a, w1r, w2s, posr, batch, seq):
    g = NSA_KV_GROUPS
    c = 2 * g
    dh = NSA_HEAD_DIM
    n_sub = seq // CMP_STRIDE
    return pl.pallas_call(
        _compress_kernel,
        grid=(batch, c),
        in_specs=[
            pl.BlockSpec((seq, dh), lambda i, j: (i, A_KV // dh + j)),
            pl.BlockSpec((1, CMP_STRIDE, dh, 2 * dh), lambda i, j: (j // g, 0, 0, 0)),
            pl.BlockSpec((1, dh, dh), lambda i, j: (j // g, 0, 0)),
            pl.BlockSpec((1, CMP_STRIDE, 8, dh), lambda i, j: (j // g, 0, 0, 0)),
        ],
        out_specs=[pl.BlockSpec((1, 1, n_sub, dh), lambda i, j: (i, j, 0, 0)),
                   pl.BlockSpec((1, 1, dh, n_sub), lambda i, j: (i, j, 0, 0))],
        out_shape=[jax.ShapeDtypeStruct((batch, c, n_sub, dh), BF16),
                   jax.ShapeDtypeStruct((batch, c, dh, n_sub), BF16)],
        scratch_shapes=[pltpu.VMEM((seq, dh), F32)],
        compiler_params=pltpu.CompilerParams(
            dimension_semantics=("parallel", "parallel"), vmem_limit_bytes=VMEM_LIMIT),
        name="compress",
    )(seg_a, w1r, w2s, posr)


def _stack_heads(q_all):
    dh = NSA_HEAD_DIM
    return jnp.concatenate([q_all[:, r * dh:(r + 1) * dh] for r in range(NSA_REP)], axis=0)


def _tile_list(hit_col, tile0, pad_tile, mats_ref):
    n = SEL_LANES
    lane = lax.broadcasted_iota(jnp.int32, (1, n), 1)
    blk = lax.broadcasted_iota(jnp.int32, (n, n), 0)
    slot = lax.broadcasted_iota(jnp.int32, (n, n), 1).astype(F32)
    hit = jnp.broadcast_to(hit_col, (n, n))
    both = jnp.maximum(hit, pltpu.roll(hit, n - 1, axis=0))
    act = jnp.where((both > 0.0) & (blk % 2 == 0) & (blk // 2 < tile0), 1.0, 0.0)
    rank = _dot(mats_ref[0], act.astype(BF16))
    cnt = jnp.sum(act, axis=0, keepdims=True)
    place = jnp.where((act > 0.0) & (rank == slot), 1.0, 0.0)
    tiles = _dot(mats_ref[1, 0:8], place.astype(BF16))[0:1]
    out = jnp.where(lane.astype(F32) < cnt, tiles, float(pad_tile))
    return jnp.where(lane == n - 1, cnt, out).astype(jnp.int32)


def _nsa_cmp_kernel(slopes_ref, q_ref, kc_ref, vct_ref, ovt_ref, mats_ref, ocmp_ref, pen_ref, tiles_ref, *,
                    seq, sub_blocks):
    step = pl.program_id(2)
    nq_step = sub_blocks * Q_BLOCK
    chunk = min(SEL_LANES, kc_ref.shape[2])
    n_chunks = kc_ref.shape[2] // chunk
    need = ((step + 1) * nq_step - CMP_BLOCK) // CMP_STRIDE + 1
    n_need = (need + chunk - 1) // chunk
    for v in range(1, n_chunks + 1):
        cond = (n_need == v) if v < n_chunks else (n_need >= v)
        if v == 1:
            cond = n_need <= 1

        @pl.when(cond)
        def _(v=v):
            hit = None
            for sub in range(sub_blocks):
                rows = slice(sub * Q_BLOCK, (sub + 1) * Q_BLOCK)
                h = _nsa_cmp_block(slopes_ref, q_ref[rows, :], kc_ref, vct_ref, ovt_ref,
                                   ocmp_ref.at[0, 0, sub], pen_ref.at[0, 0, rows],
                                   step * sub_blocks + sub, seq, v * chunk,
                                   past_first_block=(sub >= 1 or v >= 2))
                hit = h if hit is None else jnp.maximum(hit, h)
            tiles_ref[0, 0, 0] = _tile_list(hit, step * sub_blocks, seq // KEY_TILE, mats_ref)


def _nsa_cmp_block(slopes_ref, q_all, kc_ref, vct_ref, ovt_ref, ocmp_ref, pen_ref, qb, seq, n_use,
                   past_first_block):
    g = pl.program_id(1)
    nq = Q_BLOCK
    n_cpad = n_use
    n_cmp = seq // CMP_STRIDE - CMP_BLOCK // CMP_STRIDE + 1
    t0 = qb * nq

    s_t = _dot_nt(kc_ref[0, 0, :n_use, :], _stack_heads(q_all))
    n_s = lax.broadcasted_iota(jnp.int32, (n_cpad, 1), 0)
    q_l = lax.broadcasted_iota(jnp.int32, (1, nq), 1)
    dist = (t0 - (CMP_BLOCK - 1)) + q_l - n_s * CMP_STRIDE
    valid = (dist >= 0) & (n_s < n_cmp)
    dist_f = dist.astype(F32)
    probs = []
    p_sum = jnp.zeros((n_cpad, nq), F32)
    for r in range(NSA_REP):
        slope = slopes_ref[g * NSA_REP + r]
        s = jnp.where(valid, s_t[:, r * nq:(r + 1) * nq] - slope * dist_f, NEG)
        m = jnp.max(s, axis=0, keepdims=True)
        e = jnp.exp(s - m)
        inv = jnp.where(m > 0.5 * NEG, 1.0 / jnp.sum(e, axis=0, keepdims=True), 0.0)
        p = e * inv
        probs.append(p.astype(BF16))
        p_sum = p_sum + p
    ocmp_ref[...] = _dot(vct_ref[0, 0, :, :n_use], jnp.concatenate(probs, axis=1))

    p_hi = p_sum.astype(BF16)
    p_lo = (p_sum - p_hi.astype(F32)).astype(BF16)
    ovt = ovt_ref[:, :n_use]
    imp = _dot(ovt, p_hi) + _dot(ovt, p_lo)
    j_i = lax.broadcasted_iota(jnp.int32, (SEL_LANES, 1), 0)
    t_l = t0 + q_l
    cur = t_l // SEL_BLOCK
    forced = (j_i == 0) | (j_i == cur) | (j_i == cur - 1)
    causal_blk = j_i * SEL_BLOCK <= t_l
    val = jnp.where(causal_blk, jnp.where(forced, imp + FORCE_BONUS, imp), NEG)
    j_f = j_i.astype(F32)
    sel_t = jnp.zeros((SEL_LANES, nq), F32)
    rounds = min(SEL_TOPK, seq // SEL_BLOCK)
    if past_first_block and rounds >= 3:
        sel_t = jnp.where(forced, 1.0, sel_t)
        val = jnp.where(forced, PICKED, val)
        rounds -= 3
    for _ in range(rounds):
        mx = jnp.max(val, axis=0, keepdims=True)
        first = jnp.min(jnp.where(val == mx, j_f, float(SEL_LANES)), axis=0, keepdims=True)
        pick = j_f == first
        sel_t = jnp.where(pick, 1.0, sel_t)
        val = jnp.where(pick, PICKED, val)
    sel = sel_t.T
    pen_ref[...] = ((sel - 1.0) * MASK_BIG).astype(pen_ref.dtype)
    return jnp.max(sel_t, axis=1, keepdims=True)


def _nsa_cmp(seg_a, kvc, kvct, ovt, slopes, batch, seq):
    nqb = seq // Q_BLOCK
    gq = NSA_REP * NSA_HEAD_DIM
    g_ = NSA_KV_GROUPS
    n_cpad = kvc.shape[2]
    sub = _nsa_sub_blocks(seq)
    nstep = nqb // sub
    idx = np.arange(SEL_LANES)
    mats = jnp.asarray(np.stack([idx[:, None] > idx[None, :],
                                 np.broadcast_to(idx[None, :] // 2, (SEL_LANES, SEL_LANES))]), BF16)
    return pl.pallas_call(
        functools.partial(_nsa_cmp_kernel, seq=seq, sub_blocks=sub),
        grid=(batch, g_, nstep),
        in_specs=[
            pl.BlockSpec(memory_space=pltpu.SMEM),
            pl.BlockSpec((sub * Q_BLOCK, gq), lambda b, g, q: (b * nstep + q, A_Q // gq + g)),
            pl.BlockSpec((1, 1, n_cpad, NSA_HEAD_DIM), lambda b, g, q: (b, g, 0, 0)),
            pl.BlockSpec((1, 1, NSA_HEAD_DIM, n_cpad), lambda b, g, q: (b, g_ + g, 0, 0)),
            pl.BlockSpec((SEL_LANES, n_cpad), lambda b, g, q: (0, 0)),
            pl.BlockSpec((2, SEL_LANES, SEL_LANES), lambda b, g, q: (0, 0, 0)),
        ],
        out_specs=[
            pl.BlockSpec((1, 1, sub, NSA_HEAD_DIM, gq), lambda b, g, q: (b, g, q, 0, 0)),
            pl.BlockSpec((1, 1, sub * Q_BLOCK, SEL_LANES), lambda b, g, q: (b, g, q, 0)),
            pl.BlockSpec((1, 1, 1, 1, SEL_LANES), lambda b, g, q: (b, g, q, 0, 0)),
        ],
        out_shape=[
            jax.ShapeDtypeStruct((batch, g_, nqb, NSA_HEAD_DIM, gq), F32),
            jax.ShapeDtypeStruct((batch, g_, seq, SEL_LANES), BF16),
            jax.ShapeDtypeStruct((batch, g_, nstep, 1, SEL_LANES), jnp.int32),
        ],
        compiler_params=pltpu.CompilerParams(
            dimension_semantics=("parallel", "parallel", "parallel"), vmem_limit_bytes=VMEM_LIMIT),
        name="nsa_cmp",
    )(slopes, seg_a, kvc, kvct, ovt, mats)


def _nsa_attn_kernel(slopes_ref, tiles_ref, q_ref, gate_ref, pen_ref, ocmp_ref, ks_ref, vs_ref,
                     kw_ref, vw_ref, onehot_ref, wext_ref, dbias_ref, ubias_ref, wbias_ref, o_ref,
                     ksel_ref, vselt_ref, kwin_ref, vwint_ref,
                     qa_ref, qw_ref, m_ref, acc_ref, owin_ref, sa_ref, sb_ref,
                     *, seq, sub_blocks):
    g = pl.program_id(1)
    step = pl.program_id(2)
    dh = NSA_HEAD_DIM
    nq = sub_blocks * Q_BLOCK
    kt = KEY_TILE
    per = SEL_TILES_PER_STEP
    pad_tile = seq // kt
    t0 = step * nq
    tile0 = step * sub_blocks

    @pl.when(step == 0)
    def _():
        ksel_ref[0:seq, :dh] = ks_ref[...]
        ksel_ref[0:seq, dh:] = onehot_ref[...]
        ksel_ref[seq:, :dh] = jnp.zeros((kt, dh), BF16)
        ksel_ref[seq:, dh:] = jnp.ones((kt, SEL_LANES), BF16)
        lane2 = lax.broadcasted_iota(jnp.int32, (WINDOW, dh + LANE), 1)
        kwin_ref[0:WINDOW, :] = jnp.where(lane2 == dh + WCOL_PAD, 1.0, 0.0).astype(BF16)
        kwin_ref[WINDOW:, :dh] = kw_ref[...]
        kwin_ref[WINDOW:, dh:] = wext_ref[...]
        zero_tile = jnp.zeros((dh + SUM_ROWS, kt), BF16)
        ones_rows = jnp.ones((SUM_ROWS, kt), BF16)
        vselt_ref[pad_tile] = zero_tile
        for i in range(WIN_TILES):
            vwint_ref[i] = zero_tile

        def transpose_tile(t, carry):
            r0 = pl.multiple_of(t * kt, kt)
            vselt_ref[t, :dh] = vs_ref[pl.ds(r0, kt), :].astype(F32).T.astype(BF16)
            vselt_ref[t, dh:] = ones_rows
            vwint_ref[t + WIN_TILES, :dh] = vw_ref[pl.ds(r0, kt), :].astype(F32).T.astype(BF16)
            vwint_ref[t + WIN_TILES, dh:] = ones_rows
            return carry

        lax.fori_loop(0, seq // kt, transpose_tile, 0)

    slopes = [slopes_ref[g * NSA_REP + r] for r in range(NSA_REP)]
    row_q = lax.broadcasted_iota(jnp.int32, (nq, 1), 0)
    lane = lax.broadcasted_iota(jnp.int32, (1, LANE), 1)
    q_all = q_ref[...]

    pen = pen_ref[0, 0].astype(F32)
    blk_rel = ((lane - (t0 + row_q) // SEL_BLOCK) * SEL_BLOCK).astype(F32)
    for r in range(NSA_REP):
        qa_ref[r * nq:(r + 1) * nq, :dh] = q_all[:, r * dh:(r + 1) * dh]
        qa_ref[r * nq:(r + 1) * nq, dh:] = (pen + slopes[r] * blk_rel).astype(BF16)

    def listed(idx):
        return tiles_ref[0, 0, 0, 0, idx]

    cnt = listed(SEL_LANES - 1)

    def group_scores(it):
        keys = jnp.concatenate(
            [ksel_ref[pl.ds(pl.multiple_of(listed(it * per + i) * kt, kt), kt), :]
             for i in range(per)], axis=0)
        return _dot_nt(keys, qa_ref[...]) + ubias_ref[0]

    s = _dot_nt(ksel_ref[pl.ds(pl.multiple_of(t0, kt), nq), :], qa_ref[...]) + dbias_ref[0]
    m0 = jnp.max(s, axis=0, keepdims=True)
    p = jnp.exp(s - m0)
    m_ref[...] = m0
    vals = jnp.concatenate([vselt_ref[tile0 + a] for a in range(sub_blocks)], axis=1)
    acc_ref[...] = _dot(vals, p.astype(BF16))
    sa_ref[...] = group_scores(0)

    tp = t0 + WINDOW + row_q
    t_hi = (tp // SEL_BLOCK).astype(F32)
    t_lo = (tp % SEL_BLOCK).astype(F32)
    for r in range(NSA_REP):
        sl = slopes[r]
        ext = jnp.where(lane == WCOL_PAD, -MASK_BIG, 0.0)
        ext = jnp.where(lane == WCOL_HI, sl * SEL_BLOCK, ext)
        ext = jnp.where(lane == WCOL_LO, sl, ext)
        ext = jnp.where(lane == WCOL_QHI, -sl * SEL_BLOCK * t_hi, ext)
        ext = jnp.where(lane == WCOL_QLO, -sl * t_lo, ext)
        qw_ref[r * nq:(r + 1) * nq, :dh] = q_all[:, r * dh:(r + 1) * dh]
        qw_ref[r * nq:(r + 1) * nq, dh:] = ext.astype(BF16)
    wlen = WINDOW + nq
    s = _dot_nt(kwin_ref[pl.ds(pl.multiple_of(t0, kt), wlen), :], qw_ref[...]) + wbias_ref[...]
    e = jnp.exp(s - jnp.max(s, axis=0, keepdims=True))
    vwin = jnp.concatenate([vwint_ref[tile0 + i] for i in range(WIN_TILES + sub_blocks)], axis=1)
    ow = _dot(vwin, e.astype(BF16))
    owin_ref[...] = ow[:dh] / ow[dh:dh + 1]

    def absorb(s, grp):
        vals = jnp.concatenate([vselt_ref[listed(grp * per + i)] for i in range(per)], axis=1)
        m_old = m_ref[...]
        m_new = jnp.maximum(m_old, jnp.max(s, axis=0, keepdims=True))
        alpha = jnp.exp(m_old - m_new)
        p = jnp.exp(s - m_new)
        acc_ref[...] = alpha * acc_ref[...] + _dot(vals, p.astype(BF16))
        m_ref[...] = m_new

    def sel_body(it, carry):
        s = sa_ref[...]
        sb_ref[...] = group_scores(2 * it + 1)
        absorb(s, 2 * it)
        s = sb_ref[...]
        sa_ref[...] = group_scores(2 * it + 2)
        absorb(s, 2 * it + 1)
        return carry

    lax.fori_loop(0, (cnt + 2 * per - 1) // (2 * per), sel_body, 0)
    o_sel = acc_ref[:dh, :] / acc_ref[dh:dh + 1, :]
    o_win = owin_ref[...]

    gate_t = _sigmoid(gate_ref[...]).T
    for a in range(sub_blocks):
        o_cmp = ocmp_ref[0, 0, a]
        qs = slice(a * Q_BLOCK, (a + 1) * Q_BLOCK)
        for r in range(NSA_REP):
            cols = slice(r * nq + a * Q_BLOCK, r * nq + (a + 1) * Q_BLOCK)
            c0 = N_BRANCH * r
            out_t = (gate_t[c0:c0 + 1, qs] * o_cmp[:, r * Q_BLOCK:(r + 1) * Q_BLOCK]
                     + gate_t[c0 + 1:c0 + 2, qs] * o_sel[:, cols]
                     + gate_t[c0 + 2:c0 + 3, qs] * o_win[:, cols])
            o_ref[qs, r * dh:(r + 1) * dh] = out_t.T.astype(o_ref.dtype)


def _nsa_sub_blocks(seq):
    return 2 if (seq // Q_BLOCK) % 2 == 0 else 1


def _nsa_attn(seg_a, seg_b, pen, ocmp, tiles, onehot, wext, dbias, ubias, wbias, slopes, batch, seq):
    n = batch * seq
    sub = _nsa_sub_blocks(seq)
    nq = sub * Q_BLOCK
    nqb = seq // nq
    gq = NSA_REP * NSA_HEAD_DIM
    g_ = NSA_KV_GROUPS
    dh, kt = NSA_HEAD_DIM, KEY_TILE
    dk = dh + SEL_LANES
    rq = NSA_REP * nq
    n_t = seq // kt

    def kv_spec(kind):
        return pl.BlockSpec((seq, dh), lambda b, g, q, k=kind: (b, A_KV // dh + k * g_ + g))

    def const_spec(arr):
        return pl.BlockSpec(arr.shape, lambda b, g, q, nd=arr.ndim: (0,) * nd)

    return pl.pallas_call(
        functools.partial(_nsa_attn_kernel, seq=seq, sub_blocks=sub),
        grid=(batch, g_, nqb),
        in_specs=[
            pl.BlockSpec(memory_space=pltpu.SMEM),
            pl.BlockSpec((1, 1, 1, 1, SEL_LANES), lambda b, g, q: (b, g, q, 0, 0),
                         memory_space=pltpu.SMEM),
            pl.BlockSpec((nq, gq), lambda b, g, q: (b * nqb + q, A_Q // gq + g)),
            pl.BlockSpec((nq, LANE), lambda b, g, q: (b * nqb + q, B_GATE // LANE + g)),
            pl.BlockSpec((1, 1, nq, SEL_LANES), lambda b, g, q: (b, g, q, 0)),
            pl.BlockSpec((1, 1, sub, NSA_HEAD_DIM, gq), lambda b, g, q: (b, g, q, 0, 0)),
            kv_spec(2), kv_spec(3), kv_spec(4), kv_spec(5),
            const_spec(onehot), const_spec(wext),
            pl.BlockSpec((1,) + dbias.shape[1:], lambda b, g, q: (g, 0, 0)),
            pl.BlockSpec((1,) + ubias.shape[1:], lambda b, g, q: (g, 0, 0)),
            const_spec(wbias),
        ],
        out_specs=pl.BlockSpec((nq, gq), lambda b, g, q: (b * nqb + q, g)),
        out_shape=jax.ShapeDtypeStruct((n, NSA_WIDTH), BF16),
        scratch_shapes=[
            pltpu.VMEM((seq + kt, dk), BF16),
            pltpu.VMEM((n_t + 1, dh + SUM_ROWS, kt), BF16),
            pltpu.VMEM((seq + WINDOW, dh + LANE), BF16),
            pltpu.VMEM((n_t + WIN_TILES, dh + SUM_ROWS, kt), BF16),
            pltpu.VMEM((rq, dk), BF16),
            pltpu.VMEM((rq, dk), BF16),
            pltpu.VMEM((1, rq), F32),
            pltpu.VMEM((dh + SUM_ROWS, rq), F32),
            pltpu.VMEM((dh, rq), F32),
            pltpu.VMEM((SEL_TILES_PER_STEP * KEY_TILE, rq), F32),
            pltpu.VMEM((SEL_TILES_PER_STEP * KEY_TILE, rq), F32),
        ],
        compiler_params=pltpu.CompilerParams(
            dimension_semantics=("parallel", "parallel", "arbitrary"), vmem_limit_bytes=VMEM_LIMIT),
        name="nsa_attn",
    )(slopes, tiles, seg_a, seg_b, pen, ocmp, seg_a, seg_a, seg_a, seg_a, onehot, wext,
      dbias, ubias, wbias)


def _nsa_tables(slopes, seq):
    g_, kt = NSA_KV_GROUPS, KEY_TILE
    pos = np.arange(seq)
    onehot = jnp.asarray(pos[:, None] // SEL_BLOCK == np.arange(SEL_LANES)[None, :], BF16)
    ext = np.zeros((seq, LANE), np.float32)
    ext[:, WCOL_HI] = (pos + WINDOW) // SEL_BLOCK
    ext[:, WCOL_LO] = (pos + WINDOW) % SEL_BLOCK
    ext[:, WCOL_QHI] = 1.0
    ext[:, WCOL_QLO] = 1.0
    wext = jnp.asarray(ext, BF16)

    nq = _nsa_sub_blocks(seq) * Q_BLOCK

    def alibi_in_block(rows):
        u = jnp.asarray((np.arange(rows) % SEL_BLOCK).astype(np.float32))[None, :, None, None]
        t = jnp.broadcast_to(slopes.reshape(g_, 1, NSA_REP, 1) * u, (g_, rows, NSA_REP, nq))
        return t.reshape(g_, rows, NSA_REP * nq)

    ubias = alibi_in_block(SEL_TILES_PER_STEP * kt)
    kq = np.arange(nq)[:, None] <= np.arange(nq)[None, :]
    causal = np.tile(np.where(kq, 0.0, NEG).astype(np.float32), (1, NSA_REP))
    dbias = alibi_in_block(nq) + jnp.asarray(causal)[None]
    ki = np.arange(WINDOW + nq)[:, None]
    qi = np.arange(nq)[None, :]
    band = np.where((ki > qi) & (ki <= qi + WINDOW), 0.0, NEG).astype(np.float32)
    wbias = jnp.asarray(np.tile(band, (1, NSA_REP)))
    return onehot, wext, dbias, ubias, wbias


def _log_sigmoid(x):
    return jnp.minimum(x, 0.0) - jnp.log(1.0 + jnp.exp(-jnp.abs(x)))


def _split3(x):
    hi = x.astype(BF16)
    r1 = x - hi.astype(F32)
    mid = r1.astype(BF16)
    lo = (r1 - mid.astype(F32)).astype(BF16)
    return hi, mid, lo


def _mlstm_kernel(bias_ref, q_ref, k_ref, v_ref, o_ref, ifc_ref, ng_ref,
                  tri_ref, y_ref, c_ref, n_ref, m_ref):
    ch = pl.program_id(1)

    @pl.when(ch == 0)
    def _():
        c_ref[...] = jnp.zeros(c_ref.shape, F32)
        n_ref[...] = jnp.zeros(n_ref.shape, F32)
        m_ref[...] = jnp.zeros(m_ref.shape, F32)

    for sq in range(q_ref.shape[0]):
        _mlstm_chunk(bias_ref, q_ref.at[sq], k_ref.at[sq], v_ref.at[sq], o_ref.at[sq], ifc_ref.at[sq],
                     ng_ref, tri_ref, y_ref.at[sq], c_ref.at[sq], n_ref.at[sq], m_ref.at[sq])


def _mlstm_chunk(bias_ref, q_ref, k_ref, v_ref, o_ref, ifc_ref, ng_ref, tri_ref, y_ref,
                 c_ref, n_ref, m_ref):
    nh, dh = MLSTM_HEADS, MLSTM_HEAD_DIM
    L = q_ref.shape[0]
    tri = tri_ref[...]
    lane8 = lax.broadcasted_iota(jnp.int32, (1, LANE), 1)
    bias_c = jnp.zeros((1, LANE), F32)
    for h in range(nh):
        bias_c = jnp.where(lane8 == h, bias_ref[h], bias_c)
        bias_c = jnp.where(lane8 == nh + h, bias_ref[nh + h], bias_c)
    pre_c = ifc_ref[...] + bias_c
    cum_c = sum(_dot(tri, part) for part in _split3(_log_sigmoid(pre_c)))
    pre_r = pre_c.T[:2 * nh]
    cum_r = sum(_dot_nt(part, tri) for part in _split3(_log_sigmoid(pre_r)))

    rr = lax.broadcasted_iota(jnp.int32, (L, 1), 0)
    cc = lax.broadcasted_iota(jnp.int32, (1, L), 1)
    causal = cc <= rr

    for h in range(nh):
        cols = slice(h * dh, (h + 1) * dh)
        qb = q_ref[:, cols]
        kb = k_ref[:, cols]
        vh = v_ref[:, cols]
        qh = qb.astype(F32)
        kh = kb.astype(F32)
        b_c = cum_c[:, nh + h:nh + h + 1]
        li_c = pre_c[:, h:h + 1]
        b_r = cum_r[nh + h:nh + h + 1, :]
        li_r = pre_r[h:h + 1, :]
        m_prev = m_ref[h:h + 1, 0:1]

        dmat = jnp.where(causal, b_c - b_r + li_r, NEG)
        a = b_c + m_prev
        m_j = jnp.maximum(a, jnp.max(dmat, axis=1, keepdims=True))
        w_intra = jnp.exp(dmat - m_j)
        w_inter = jnp.exp(a - m_j)
        sc = _dot_nt(qb, kb) * w_intra
        c_old = c_ref[h]
        n_old = n_ref[h:h + 1, :]
        num = w_inter * _dot(qb, c_old.astype(BF16)) + _dot(sc.astype(BF16), vh)
        den = (w_inter * jnp.sum(qh * n_old, axis=1, keepdims=True)
               + jnp.sum(sc, axis=1, keepdims=True))
        hid = num / jnp.maximum(jnp.abs(den), jnp.exp(-m_j))

        g_tot = b_r[:, L - 1:L]
        lw_c = g_tot - b_c + li_c
        lw_r = g_tot - b_r + li_r
        m_new = jnp.maximum(g_tot + m_prev, jnp.max(lw_r, axis=1, keepdims=True))
        decay = jnp.exp(g_tot + m_prev - m_new)
        kw = jnp.exp(lw_c - m_new) * kh
        c_ref[h] = decay * c_old + _dot(kw.T.astype(BF16), vh)
        n_ref[h:h + 1, :] = decay * n_old + jnp.sum(kw, axis=0, keepdims=True)
        m_ref[h:h + 1, :] = jnp.broadcast_to(m_new, (1, LANE))

        hn = hid * lax.rsqrt(jnp.mean(hid * hid, axis=-1, keepdims=True) + EPS) * ng_ref[:, cols]
        y_ref[:, cols] = (_sigmoid(o_ref[:, cols]) * hn).astype(y_ref.dtype)


def _mlstm(seg_a, seg_b, bias, norm_g, tri, batch, seq, chunk):
    nc = seq // chunk
    w = MLSTM_WIDTH
    nh, dh = MLSTM_HEADS, MLSTM_HEAD_DIM
    seqs = 1
    a3 = seg_a.reshape(batch, seq, SEG_A)
    b3 = seg_b.reshape(batch, seq, SEG_B)

    def col_spec(off):
        return pl.BlockSpec((seqs, chunk, w), lambda b, c, o=off // w: (b, c, o))

    y = pl.pallas_call(
        _mlstm_kernel,
        grid=(batch // seqs, nc),
        in_specs=[
            pl.BlockSpec(memory_space=pltpu.SMEM),
            col_spec(A_QK), col_spec(A_QK + w), col_spec(A_V), col_spec(B_O),
            pl.BlockSpec((seqs, chunk, LANE), lambda b, c: (b, c, B_IF // LANE)),
            pl.BlockSpec((1, w), lambda b, c: (0, 0)),
            pl.BlockSpec((chunk, chunk), lambda b, c: (0, 0)),
        ],
        out_specs=pl.BlockSpec((seqs, chunk, w), lambda b, c: (b, c, 0)),
        out_shape=jax.ShapeDtypeStruct((batch, seq, w), BF16),
        scratch_shapes=[
            pltpu.VMEM((seqs, nh, dh, dh), F32),
            pltpu.VMEM((seqs, 8, dh), F32),
            pltpu.VMEM((seqs, 8, LANE), F32),
        ],
        compiler_params=pltpu.CompilerParams(
            dimension_semantics=("parallel", "arbitrary"), vmem_limit_bytes=VMEM_LIMIT),
        name="mlstm",
    )(bias, a3, a3, a3, b3, b3, norm_g, tri)
    return y.reshape(batch * seq, w)


def _outproj_kernel(x_ref, ya_ref, ym_ref, wa_ref, wm_ref, o_ref):
    o_ref[...] = x_ref[...] + _dot(ya_ref[...], wa_ref[...]) + _dot(ym_ref[...], wm_ref[...])


def _outproj(x2, ya, ym, wa, wm, tm):
    n, d = x2.shape
    return pl.pallas_call(
        _outproj_kernel,
        grid=(n // tm,),
        in_specs=[
            pl.BlockSpec((tm, d), lambda i: (i, 0)),
            pl.BlockSpec((tm, ya.shape[1]), lambda i: (i, 0)),
            pl.BlockSpec((tm, ym.shape[1]), lambda i: (i, 0)),
            pl.BlockSpec(wa.shape, lambda i: (0, 0)),
            pl.BlockSpec(wm.shape, lambda i: (0, 0)),
        ],
        out_specs=pl.BlockSpec((tm, d), lambda i: (i, 0)),
        out_shape=jax.ShapeDtypeStruct((n, d), F32),
        compiler_params=pltpu.CompilerParams(
            dimension_semantics=("parallel",), vmem_limit_bytes=VMEM_LIMIT),
        name="outproj",
    )(x2, ya, ym, wa, wm)


def _mlp_kernel(x_ref, g_ref, w1_ref, w2_ref, gf_ref, o_ref, h_ref):
    f = pl.program_id(1)

    @pl.when(f == 0)
    def _():
        x = x_ref[...]
        r = lax.rsqrt(jnp.mean(x * x, axis=-1, keepdims=True) + EPS)
        h_ref[...] = (x * r * g_ref[...]).astype(BF16)
        o_ref[...] = x

    u = jnp.maximum(_dot(h_ref[...], w1_ref[...]), 0.0)
    o_ref[...] += _dot((u * u).astype(BF16), w2_ref[...])

    @pl.when(f == pl.num_programs(1) - 1)
    def _():
        x2 = o_ref[...]
        r = lax.rsqrt(jnp.mean(x2 * x2, axis=-1, keepdims=True) + EPS)
        o_ref[...] = x2 * r * gf_ref[...]


def _mlp(x1, g, w1, w2, gf, tm, tf):
    n, d = x1.shape
    dff = w1.shape[1]
    return pl.pallas_call(
        _mlp_kernel,
        grid=(n // tm, dff // tf),
        in_specs=[
            pl.BlockSpec((tm, d), lambda i, f: (i, 0)),
            pl.BlockSpec((1, d), lambda i, f: (0, 0)),
            pl.BlockSpec((d, tf), lambda i, f: (0, f)),
            pl.BlockSpec((tf, d), lambda i, f: (f, 0)),
            pl.BlockSpec((1, d), lambda i, f: (0, 0)),
        ],
        out_specs=pl.BlockSpec((tm, d), lambda i, f: (i, 0)),
        out_shape=jax.ShapeDtypeStruct((n, d), F32),
        scratch_shapes=[pltpu.VMEM((tm, d), BF16)],
        compiler_params=pltpu.CompilerParams(
            dimension_semantics=("parallel", "arbitrary"), vmem_limit_bytes=VMEM_LIMIT),
        name="mlp",
    )(x1, g, w1, w2, gf)


def _row_tile(n, want):
    t = want
    while n % t:
        t //= 2
    return t


def _layer(x2, batch, seq, norm_mix_g, w_in, w_cmp_k1, w_cmp_k2, pos_cmp_k, w_cmp_v1, w_cmp_v2,
           pos_cmp_v, conv_w, conv_b, b_igate, b_fgate, mlstm_norm_g, w_out, norm_mlp_g,
           w_mlp_in, w_mlp_out):
    n, d = x2.shape
    assert seq % Q_BLOCK == 0 and seq >= WINDOW + Q_BLOCK and seq // SEL_BLOCK <= SEL_LANES
    g_ = NSA_KV_GROUPS
    nh = MLSTM_HEADS

    c_gate = NSA_WIDTH + 6 * NSA_KV_WIDTH
    c_qk = c_gate + NSA_HEADS * N_BRANCH
    c_v = c_qk + 2 * MLSTM_WIDTH
    c_o = c_v + MLSTM_WIDTH
    c_i = c_o + MLSTM_WIDTH
    c_f = c_i + nh
    w16 = w_in.astype(BF16)
    gate_cols = []
    per_g = NSA_REP * N_BRANCH
    for g in range(g_):
        gate_cols += [w16[:, c_gate + g * per_g:c_gate + (g + 1) * per_g],
                      jnp.zeros((d, LANE - per_g), BF16)]
    w_ab = jnp.concatenate(
        [w16[:, c_v:c_o], w16[:, c_qk:c_v], w16[:, :c_gate],
         w16[:, c_o:c_i]] + gate_cols
        + [w16[:, c_i:c_f + nh], jnp.zeros((d, SEG_B - B_IF - 2 * nh), BF16)],
        axis=1)
    scale = jnp.concatenate([jnp.ones((1, A_QK + MLSTM_WIDTH), F32),
                             jnp.full((1, MLSTM_WIDTH), MLSTM_HEAD_DIM ** -0.5, F32),
                             jnp.full((1, NSA_WIDTH), NSA_HEAD_DIM ** -0.5, F32),
                             jnp.ones((1, SEG_A - A_KV + SEG_B), F32)], axis=1)
    conv_pad = ((0, 0), (A_QK, SEG_A - A_Q + SEG_B))
    cw_ab = jnp.pad(conv_w, conv_pad)
    cb_ab = jnp.pad(conv_b.reshape(1, -1), conv_pad)
    g_mix = norm_mix_g.reshape(1, d)

    tm = _row_tile(seq, ROW_TILE)
    seg_a, seg_b = _inproj(x2, g_mix, w_ab, scale, cw_ab, cb_ab, tm, COL_TILE, (A_QK, A_Q), seq)

    n_sub = seq // CMP_STRIDE
    dh = NSA_HEAD_DIM
    w1s = jnp.stack([w_cmp_k1, w_cmp_v1]).reshape(2, 2, CMP_STRIDE, dh, dh)
    w1r = jnp.concatenate([w1s[:, 0], w1s[:, 1]], axis=-1).astype(BF16)
    w2s = jnp.stack([w_cmp_k2, w_cmp_v2]).astype(BF16)
    poss = jnp.stack([pos_cmp_k, pos_cmp_v]).reshape(2, 2, CMP_STRIDE, dh).transpose(0, 2, 1, 3)
    posr = jnp.pad(poss, ((0, 0), (0, 0), (0, 6), (0, 0))).astype(BF16)
    kvc, kvct = _compress(seg_a, w1r, w2s, posr, batch, seq)

    cmp_start = np.arange(n_sub) * CMP_STRIDE
    sel_start = np.arange(SEL_LANES) * SEL_BLOCK
    ovt = ((cmp_start[None, :] < sel_start[:, None] + SEL_BLOCK)
           & (cmp_start[None, :] + CMP_BLOCK - 1 >= sel_start[:, None])
           & (np.arange(n_sub)[None, :] < n_sub - CMP_BLOCK // CMP_STRIDE + 1))
    ovt = jnp.asarray(ovt, BF16)
    slopes = jnp.exp2(-8.0 * jnp.arange(1, NSA_HEADS + 1, dtype=F32) / NSA_HEADS)
    ocmp, pen, tiles = _nsa_cmp(seg_a, kvc, kvct, ovt, slopes, batch, seq)
    onehot, wext, dbias, ubias, wbias = _nsa_tables(slopes, seq)
    y_a = _nsa_attn(seg_a, seg_b, pen, ocmp, tiles, onehot, wext, dbias, ubias, wbias,
                    slopes, batch, seq)

    chunk = 256 if seq % 256 == 0 else 128
    bias = jnp.concatenate([b_igate, b_fgate]).astype(F32)
    tri = jnp.asarray(np.tril(np.ones((chunk, chunk), np.float32)), BF16)
    y_m = _mlstm(seg_a, seg_b, bias, mlstm_norm_g.reshape(1, -1), tri, batch, seq, chunk)

    w_o = w_out.astype(BF16)
    x1 = _outproj(x2, y_a, y_m, w_o[:NSA_WIDTH], w_o[NSA_WIDTH:], _row_tile(n, 512))
    return x1, (norm_mlp_g.reshape(1, d), w_mlp_in.astype(BF16), w_mlp_out.astype(BF16))


def kernel(x, norm_mix_g, w_in, w_cmp_k1, w_cmp_k2, pos_cmp_k, w_cmp_v1, w_cmp_v2, pos_cmp_v, conv_w, conv_b, b_igate, b_fgate, mlstm_norm_g, w_out, norm_mlp_g, w_mlp_in, w_mlp_out, norm_f_g):
    batch, seq, d = x.shape
    depth = w_in.shape[0]
    assert depth == 1, "the final RMSNorm is fused into the last layer's channel mixer"
    x2 = x.reshape(batch * seq, d)
    tm = _row_tile(batch * seq, ROW_TILE)
    for l in range(depth):
        x1, (g_mlp, w1, w2) = _layer(
            x2, batch, seq, norm_mix_g[l], w_in[l], w_cmp_k1[l], w_cmp_k2[l], pos_cmp_k[l],
            w_cmp_v1[l], w_cmp_v2[l], pos_cmp_v[l], conv_w[l], conv_b[l], b_igate[l], b_fgate[l],
            mlstm_norm_g[l], w_out[l], norm_mlp_g[l], w_mlp_in[l], w_mlp_out[l])
        x2 = _mlp(x1, g_mlp, w1, w2, norm_f_g.reshape(1, d), tm, 512)
    return x2.reshape(batch, seq, d)
```

```python
import functools

import numpy as np
import jax
import jax.numpy as jnp
from jax import lax
from jax.experimental import pallas as pl
from jax.experimental.pallas import tpu as pltpu

F32 = jnp.float32
BF16 = jnp.bfloat16

EPS = 1e-6
NEG = -1e30
FORCE_BONUS = 1e4
PICKED = -3e38
MASK_BIG = 1e30

D_MODEL = 2048
NSA_HEAD_DIM = 128
NSA_WIDTH = D_MODEL // 2
NSA_HEADS = NSA_WIDTH // NSA_HEAD_DIM
NSA_REP = 4
NSA_KV_GROUPS = NSA_HEADS // NSA_REP
NSA_KV_WIDTH = NSA_KV_GROUPS * NSA_HEAD_DIM
CMP_BLOCK = 32
CMP_STRIDE = 16
SEL_BLOCK = 64
SEL_TOPK = 16
WINDOW = 512
Q_BLOCK = 128
N_BRANCH = 3
MLSTM_HEAD_DIM = 256
MLSTM_WIDTH = D_MODEL - NSA_WIDTH
MLSTM_HEADS = MLSTM_WIDTH // MLSTM_HEAD_DIM
CONV_WIDTH = 4

LANE = 128
SEL_LANES = 128
KEY_TILE = 128
SEL_TILES_PER_STEP = 2
WIN_TILES = WINDOW // KEY_TILE
SUM_ROWS = 16
VMEM_LIMIT = 56 * 1024 * 1024
ROW_TILE = 1024

A_V, A_QK = 0, MLSTM_WIDTH
A_Q = A_QK + 2 * MLSTM_WIDTH
A_KV = A_Q + NSA_WIDTH
SEG_A = A_KV + 6 * NSA_KV_WIDTH
B_O = 0
B_GATE = MLSTM_WIDTH
B_IF = B_GATE + NSA_KV_GROUPS * LANE
SEG_B = B_IF + 2 * LANE
COL_TILE = 512

WCOL_PAD, WCOL_HI, WCOL_LO, WCOL_QHI, WCOL_QLO = 0, 1, 2, 3, 4


def _dot(a, b):
    return jnp.dot(a, b, preferred_element_type=F32)


def _dot_nt(a, b):
    return lax.dot_general(a, b, (((1,), (1,)), ((), ())), preferred_element_type=F32)


def _sigmoid(x):
    return 1.0 / (1.0 + jnp.exp(-x))


def _shifted(x, tail, s):
    xs = pltpu.roll(x, s, axis=0)
    ts = pltpu.roll(tail, s, axis=0)
    row8 = lax.broadcasted_iota(jnp.int32, (8, 1), 0)
    head = jnp.where(row8 < s, ts, xs[:8])
    return jnp.concatenate([head, xs[8:]], axis=0)


def _conv_silu(x, tail, w, b):
    y = b + _shifted(x, tail, CONV_WIDTH - 1) * w[0:1]
    for i in range(1, CONV_WIDTH - 1):
        y = y + _shifted(x, tail, CONV_WIDTH - 1 - i) * w[i:i + 1]
    y = y + x * w[CONV_WIDTH - 1:CONV_WIDTH]
    return y * _sigmoid(y)


def _inproj_kernel(x_ref, g_ref, w_ref, cs_ref, cw_ref, cb_ref, oa_ref, ob_ref, h_ref, halo_ref, *,
                   nb, n_plain, tiles_per_seq):
    i = pl.program_id(0)
    j = pl.program_id(1)

    @pl.when(j == 0)
    def _():
        x = x_ref[...]
        r = lax.rsqrt(jnp.mean(x * x, axis=-1, keepdims=True) + EPS)
        h_ref[...] = (x * r * g_ref[...]).astype(BF16)

    is_conv = j >= nb + n_plain

    @pl.when((j >= nb) & jnp.logical_not(is_conv))
    def _():
        oa_ref[...] = (_dot(h_ref[...], w_ref[...]) * cs_ref[...]).astype(oa_ref.dtype)

    @pl.when(j < nb)
    def _():
        ob_ref[...] = (_dot(h_ref[...], w_ref[...]) * cs_ref[...]).astype(ob_ref.dtype)

    @pl.when(is_conv)
    def _():
        slot = j - (nb + n_plain)

        @pl.when(i % tiles_per_seq == 0)
        def _():
            halo_ref[slot] = jnp.zeros(halo_ref.shape[1:], F32)

        acc = _dot(h_ref[...], w_ref[...])
        tail = halo_ref[slot]
        halo_ref[slot] = acc[acc.shape[0] - 8:]
        y = _conv_silu(acc, tail, cw_ref[...], cb_ref[...])
        oa_ref[...] = (y * cs_ref[...]).astype(oa_ref.dtype)


def _inproj(x2, g, w, cs, cw, cb, tm, tn, conv_cols, seq):
    n, d = x2.shape
    na, nb = SEG_A // tn, SEG_B // tn
    conv_lo, conv_hi = conv_cols[0] // tn, conv_cols[1] // tn
    assert conv_cols[0] % tn == 0 and conv_cols[1] % tn == 0 and seq % tm == 0 and conv_hi <= na
    assert SEG_A % tn == 0 and SEG_B % tn == 0 and w.shape[1] == SEG_A + SEG_B
    n_conv = conv_hi - conv_lo
    n_plain = na - n_conv

    def a_tile(j):
        p = jnp.maximum(j - nb, 0)
        plain = jnp.where(p < conv_lo, p, p + n_conv)
        return jnp.where(p < n_plain, plain, conv_lo + p - n_plain)

    def w_tile(i, j):
        return (0, jnp.where(j < nb, na + j, a_tile(j)))

    return pl.pallas_call(
        functools.partial(_inproj_kernel, nb=nb, n_plain=n_plain, tiles_per_seq=seq // tm),
        grid=(n // tm, na + nb),
        in_specs=[
            pl.BlockSpec((tm, d), lambda i, j: (jnp.minimum(i + jnp.minimum(j, 1), n // tm - 1), 0)),
            pl.BlockSpec((1, d), lambda i, j: (0, 0)),
            pl.BlockSpec((d, tn), w_tile),
            pl.BlockSpec((1, tn), w_tile),
            pl.BlockSpec((CONV_WIDTH, tn), w_tile),
            pl.BlockSpec((1, tn), w_tile),
        ],
        out_specs=[pl.BlockSpec((tm, tn), lambda i, j: (i, a_tile(j))),
                   pl.BlockSpec((tm, tn), lambda i, j: (i, jnp.minimum(j, nb - 1)))],
        out_shape=[jax.ShapeDtypeStruct((n, SEG_A), BF16), jax.ShapeDtypeStruct((n, SEG_B), F32)],
        scratch_shapes=[pltpu.VMEM((tm, d), BF16),
                        pltpu.VMEM((conv_hi - conv_lo, 8, tn), F32)],
        compiler_params=pltpu.CompilerParams(
            dimension_semantics=("arbitrary", "arbitrary"), vmem_limit_bytes=VMEM_LIMIT),
        name="inproj",
    )(x2, g, w, cs, cw, cb)


def _compress_kernel(x_ref, w1_ref, w2_ref, pos_ref, o_ref, ot_ref, xf_ref):
    n_sub = o_ref.shape[2]
    dh = NSA_HEAD_DIM
    xf_ref[...] = x_ref[...].astype(F32)
    acc = jnp.zeros((n_sub, 2 * dh), F32)
    posw = jnp.zeros((1, dh), F32)
    for p in range(CMP_STRIDE):
        wp = w1_ref[0, p]
        acc = acc + _dot(xf_ref[pl.ds(p, n_sub, stride=CMP_STRIDE), :].astype(BF16), wp)
        pw = _dot(pos_ref[0, p], wp)
        posw = posw + pw[0:1, :dh] + pw[1:2, dh:]
    bot = pltpu.roll(acc[:, dh:], n_sub - 1, axis=0)
    pre = acc[:, :dh] + bot + posw
    hid = pre * _sigmoid(pre)
    out = _dot(hid.astype(BF16), w2_ref[0])
    o_ref[0, 0] = out.astype(o_ref.dtype)
    ot_ref[0, 0] = out.T.astype(ot_ref.dtype)


def _compress(seg_a, w1r, w2s, posr, batch, seq):
    g = NSA_KV_GROUPS
    c = 2 * g
    dh = NSA_HEAD_DIM
    n_sub = seq // CMP_STRIDE
    return pl.pallas_call(
        _compress_kernel,
        grid=(batch, c),
        in_specs=[
            pl.BlockSpec((seq, dh), lambda i, j: (i, A_KV // dh + j)),
            pl.BlockSpec((1, CMP_STRIDE, dh, 2 * dh), lambda i, j: (j // g, 0, 0, 0)),
            pl.BlockSpec((1, dh, dh), lambda i, j: (j // g, 0, 0)),
            pl.BlockSpec((1, CMP_STRIDE, 8, dh), lambda i, j: (j // g, 0, 0, 0)),
        ],
        out_specs=[pl.BlockSpec((1, 1, n_sub, dh), lambda i, j: (i, j, 0, 0)),
                   pl.BlockSpec((1, 1, dh, n_sub), lambda i, j: (i, j, 0, 0))],
        out_shape=[jax.ShapeDtypeStruct((batch, c, n_sub, dh), BF16),
                   jax.ShapeDtypeStruct((batch, c, dh, n_sub), BF16)],
        scratch_shapes=[pltpu.VMEM((seq, dh), F32)],
        compiler_params=pltpu.CompilerParams(
            dimension_semantics=("parallel", "parallel"), vmem_limit_bytes=VMEM_LIMIT),
        name="compress",
    )(seg_a, w1r, w2s, posr)


def _stack_heads(q_all):
    dh = NSA_HEAD_DIM
    return jnp.concatenate([q_all[:, r * dh:(r + 1) * dh] for r in range(NSA_REP)], axis=0)


def _tile_list(hit_col, tile0, pad_tile, mats_ref):
    n = SEL_LANES
    lane = lax.broadcasted_iota(jnp.int32, (1, n), 1)
    blk = lax.broadcasted_iota(jnp.int32, (n, n), 0)
    slot = lax.broadcasted_iota(jnp.int32, (n, n), 1).astype(F32)
    hit = jnp.broadcast_to(hit_col, (n, n))
    both = jnp.maximum(hit, pltpu.roll(hit, n - 1, axis=0))
    act = jnp.where((both > 0.0) & (blk % 2 == 0) & (blk // 2 < tile0), 1.0, 0.0)
    rank = _dot(mats_ref[0], act.astype(BF16))
    cnt = jnp.sum(act, axis=0, keepdims=True)
    place = jnp.where((act > 0.0) & (rank == slot), 1.0, 0.0)
    tiles = _dot(mats_ref[1, 0:8], place.astype(BF16))[0:1]
    out = jnp.where(lane.astype(F32) < cnt, tiles, float(pad_tile))
    return jnp.where(lane == n - 1, cnt, out).astype(jnp.int32)


def _nsa_cmp_kernel(slopes_ref, q_ref, kc_ref, vct_ref, ovt_ref, mats_ref, ocmp_ref, pen_ref, tiles_ref, *,
                    seq, sub_blocks):
    step = pl.program_id(2)
    nq_step = sub_blocks * Q_BLOCK
    chunk = min(SEL_LANES, kc_ref.shape[2])
    n_chunks = kc_ref.shape[2] // chunk
    need = ((step + 1) * nq_step - CMP_BLOCK) // CMP_STRIDE + 1
    n_need = (need + chunk - 1) // chunk
    for v in range(1, n_chunks + 1):
        cond = (n_need == v) if v < n_chunks else (n_need >= v)
        if v == 1:
            cond = n_need <= 1

        @pl.when(cond)
        def _(v=v):
            hit = None
            for sub in range(sub_blocks):
                rows = slice(sub * Q_BLOCK, (sub + 1) * Q_BLOCK)
                h = _nsa_cmp_block(slopes_ref, q_ref[rows, :], kc_ref, vct_ref, ovt_ref,
                                   ocmp_ref.at[0, 0, sub], pen_ref.at[0, 0, rows],
                                   step * sub_blocks + sub, seq, v * chunk,
                                   past_first_block=(sub >= 1 or v >= 2))
                hit = h if hit is None else jnp.maximum(hit, h)
            tiles_ref[0, 0, 0] = _tile_list(hit, step * sub_blocks, seq // KEY_TILE, mats_ref)


def _nsa_cmp_block(slopes_ref, q_all, kc_ref, vct_ref, ovt_ref, ocmp_ref, pen_ref, qb, seq, n_use,
                   past_first_block):
    g = pl.program_id(1)
    nq = Q_BLOCK
    n_cpad = n_use
    n_cmp = seq // CMP_STRIDE - CMP_BLOCK // CMP_STRIDE + 1
    t0 = qb * nq

    s_t = _dot_nt(kc_ref[0, 0, :n_use, :], _stack_heads(q_all))
    n_s = lax.broadcasted_iota(jnp.int32, (n_cpad, 1), 0)
    q_l = lax.broadcasted_iota(jnp.int32, (1, nq), 1)
    dist = (t0 - (CMP_BLOCK - 1)) + q_l - n_s * CMP_STRIDE
    valid = (dist >= 0) & (n_s < n_cmp)
    dist_f = dist.astype(F32)
    probs = []
    p_sum = jnp.zeros((n_cpad, nq), F32)
    for r in range(NSA_REP):
        slope = slopes_ref[g * NSA_REP + r]
        s = jnp.where(valid, s_t[:, r * nq:(r + 1) * nq] - slope * dist_f, NEG)
        m = jnp.max(s, axis=0, keepdims=True)
        e = jnp.exp(s - m)
        inv = jnp.where(m > 0.5 * NEG, 1.0 / jnp.sum(e, axis=0, keepdims=True), 0.0)
        p = e * inv
        probs.append(p.astype(BF16))
        p_sum = p_sum + p
    ocmp_ref[...] = _dot(vct_ref[0, 0, :, :n_use], jnp.concatenate(probs, axis=1))

    p_hi = p_sum.astype(BF16)
    p_lo = (p_sum - p_hi.astype(F32)).astype(BF16)
    ovt = ovt_ref[:, :n_use]
    imp = _dot(ovt, p_hi) + _dot(ovt, p_lo)
    j_i = lax.broadcasted_iota(jnp.int32, (SEL_LANES, 1), 0)
    t_l = t0 + q_l
    cur = t_l // SEL_BLOCK
    forced = (j_i == 0) | (j_i == cur) | (j_i == cur - 1)
    causal_blk = j_i * SEL_BLOCK <= t_l
    val = jnp.where(causal_blk, jnp.where(forced, imp + FORCE_BONUS, imp), NEG)
    j_f = j_i.astype(F32)
    sel_t = jnp.zeros((SEL_LANES, nq), F32)
    rounds = min(SEL_TOPK, seq // SEL_BLOCK)
    if past_first_block and rounds >= 3:
        sel_t = jnp.where(forced, 1.0, sel_t)
        val = jnp.where(forced, PICKED, val)
        rounds -= 3
    for _ in range(rounds):
        mx = jnp.max(val, axis=0, keepdims=True)
        first = jnp.min(jnp.where(val == mx, j_f, float(SEL_LANES)), axis=0, keepdims=True)
        pick = j_f == first
        sel_t = jnp.where(pick, 1.0, sel_t)
        val = jnp.where(pick, PICKED, val)
    sel = sel_t.T
    pen_ref[...] = ((sel - 1.0) * MASK_BIG).astype(pen_ref.dtype)
    return jnp.max(sel_t, axis=1, keepdims=True)


def _nsa_cmp(seg_a, kvc, kvct, ovt, slopes, batch, seq):
    nqb = seq // Q_BLOCK
    gq = NSA_REP * NSA_HEAD_DIM
    g_ = NSA_KV_GROUPS
    n_cpad = kvc.shape[2]
    sub = _nsa_sub_blocks(seq)
    nstep = nqb // sub
    idx = np.arange(SEL_LANES)
    mats = jnp.asarray(np.stack([idx[:, None] > idx[None, :],
                                 np.broadcast_to(idx[None, :] // 2, (SEL_LANES, SEL_LANES))]), BF16)
    return pl.pallas_call(
        functools.partial(_nsa_cmp_kernel, seq=seq, sub_blocks=sub),
        grid=(batch, g_, nstep),
        in_specs=[
            pl.BlockSpec(memory_space=pltpu.SMEM),
            pl.BlockSpec((sub * Q_BLOCK, gq), lambda b, g, q: (b * nstep + q, A_Q // gq + g)),
            pl.BlockSpec((1, 1, n_cpad, NSA_HEAD_DIM), lambda b, g, q: (b, g, 0, 0)),
            pl.BlockSpec((1, 1, NSA_HEAD_DIM, n_cpad), lambda b, g, q: (b, g_ + g, 0, 0)),
            pl.BlockSpec((SEL_LANES, n_cpad), lambda b, g, q: (0, 0)),
            pl.BlockSpec((2, SEL_LANES, SEL_LANES), lambda b, g, q: (0, 0, 0)),
        ],
        out_specs=[
            pl.BlockSpec((1, 1, sub, NSA_HEAD_DIM, gq), lambda b, g, q: (b, g, q, 0, 0)),
            pl.BlockSpec((1, 1, sub * Q_BLOCK, SEL_LANES), lambda b, g, q: (b, g, q, 0)),
            pl.BlockSpec((1, 1, 1, 1, SEL_LANES), lambda b, g, q: (b, g, q, 0, 0)),
        ],
        out_shape=[
            jax.ShapeDtypeStruct((batch, g_, nqb, NSA_HEAD_DIM, gq), F32),
            jax.ShapeDtypeStruct((batch, g_, seq, SEL_LANES), BF16),
            jax.ShapeDtypeStruct((batch, g_, nstep, 1, SEL_LANES), jnp.int32),
        ],
        compiler_params=pltpu.CompilerParams(
            dimension_semantics=("parallel", "parallel", "parallel"), vmem_limit_bytes=VMEM_LIMIT),
        name="nsa_cmp",
    )(slopes, seg_a, kvc, kvct, ovt, mats)


def _nsa_attn_kernel(slopes_ref, tiles_ref, q_ref, gate_ref, pen_ref, ocmp_ref, ks_ref, vs_ref,
                     kw_ref, vw_ref, onehot_ref, wext_ref, dbias_ref, ubias_ref, wbias_ref, o_ref,
                     ksel_ref, vselt_ref, kwin_ref, vwint_ref,
                     qa_ref, qw_ref, m_ref, acc_ref, owin_ref, sa_ref, sb_ref,
                     *, seq, sub_blocks):
    g = pl.program_id(1)
    step = pl.program_id(2)
    dh = NSA_HEAD_DIM
    nq = sub_blocks * Q_BLOCK
    kt = KEY_TILE
    per = SEL_TILES_PER_STEP
    pad_tile = seq // kt
    t0 = step * nq
    tile0 = step * sub_blocks

    @pl.when(step == 0)
    def _():
        ksel_ref[0:seq, :dh] = ks_ref[...]
        ksel_ref[0:seq, dh:] = onehot_ref[...]
        ksel_ref[seq:, :dh] = jnp.zeros((kt, dh), BF16)
        ksel_ref[seq:, dh:] = jnp.ones((kt, SEL_LANES), BF16)
        lane2 = lax.broadcasted_iota(jnp.int32, (WINDOW, dh + LANE), 1)
        kwin_ref[0:WINDOW, :] = jnp.where(lane2 == dh + WCOL_PAD, 1.0, 0.0).astype(BF16)
        kwin_ref[WINDOW:, :dh] = kw_ref[...]
        kwin_ref[WINDOW:, dh:] = wext_ref[...]
        zero_tile = jnp.zeros((dh + SUM_ROWS, kt), BF16)
        ones_rows = jnp.ones((SUM_ROWS, kt), BF16)
        vselt_ref[pad_tile] = zero_tile
        for i in range(WIN_TILES):
            vwint_ref[i] = zero_tile

        def transpose_tile(t, carry):
            r0 = pl.multiple_of(t * kt, kt)
            vselt_ref[t, :dh] = vs_ref[pl.ds(r0, kt), :].astype(F32).T.astype(BF16)
            vselt_ref[t, dh:] = ones_rows
            vwint_ref[t + WIN_TILES, :dh] = vw_ref[pl.ds(r0, kt), :].astype(F32).T.astype(BF16)
            vwint_ref[t + WIN_TILES, dh:] = ones_rows
            return carry

        lax.fori_loop(0, seq // kt, transpose_tile, 0)

    slopes = [slopes_ref[g * NSA_REP + r] for r in range(NSA_REP)]
    row_q = lax.broadcasted_iota(jnp.int32, (nq, 1), 0)
    lane = lax.broadcasted_iota(jnp.int32, (1, LANE), 1)
    q_all = q_ref[...]

    pen = pen_ref[0, 0].astype(F32)
    blk_rel = ((lane - (t0 + row_q) // SEL_BLOCK) * SEL_BLOCK).astype(F32)
    for r in range(NSA_REP):
        qa_ref[r * nq:(r + 1) * nq, :dh] = q_all[:, r * dh:(r + 1) * dh]
        qa_ref[r * nq:(r + 1) * nq, dh:] = (pen + slopes[r] * blk_rel).astype(BF16)

    def listed(idx):
        return tiles_ref[0, 0, 0, 0, idx]

    cnt = listed(SEL_LANES - 1)

    def group_scores(it):
        keys = jnp.concatenate(
            [ksel_ref[pl.ds(pl.multiple_of(listed(it * per + i) * kt, kt), kt), :]
             for i in range(per)], axis=0)
        return _dot_nt(keys, qa_ref[...]) + ubias_ref[0]

    s = _dot_nt(ksel_ref[pl.ds(pl.multiple_of(t0, kt), nq), :], qa_ref[...]) + dbias_ref[0]
    m0 = jnp.max(s, axis=0, keepdims=True)
    p = jnp.exp(s - m0)
    m_ref[...] = m0
    vals = jnp.concatenate([vselt_ref[tile0 + a] for a in range(sub_blocks)], axis=1)
    acc_ref[...] = _dot(vals, p.astype(BF16))
    sa_ref[...] = group_scores(0)

    tp = t0 + WINDOW + row_q
    t_hi = (tp // SEL_BLOCK).astype(F32)
    t_lo = (tp % SEL_BLOCK).astype(F32)
    for r in range(NSA_REP):
        sl = slopes[r]
        ext = jnp.where(lane == WCOL_PAD, -MASK_BIG, 0.0)
        ext = jnp.where(lane == WCOL_HI, sl * SEL_BLOCK, ext)
        ext = jnp.where(lane == WCOL_LO, sl, ext)
        ext = jnp.where(lane == WCOL_QHI, -sl * SEL_BLOCK * t_hi, ext)
        ext = jnp.where(lane == WCOL_QLO, -sl * t_lo, ext)
        qw_ref[r * nq:(r + 1) * nq, :dh] = q_all[:, r * dh:(r + 1) * dh]
        qw_ref[r * nq:(r + 1) * nq, dh:] = ext.astype(BF16)
    wlen = WINDOW + nq
    s = _dot_nt(kwin_ref[pl.ds(pl.multiple_of(t0, kt), wlen), :], qw_ref[...]) + wbias_ref[...]
    e = jnp.exp(s - jnp.max(s, axis=0, keepdims=True))
    vwin = jnp.concatenate([vwint_ref[tile0 + i] for i in range(WIN_TILES + sub_blocks)], axis=1)
    ow = _dot(vwin, e.astype(BF16))
    owin_ref[...] = ow[:dh] / ow[dh:dh + 1]

    def absorb(s, grp):
        vals = jnp.concatenate([vselt_ref[listed(grp * per + i)] for i in range(per)], axis=1)
        m_old = m_ref[...]
        m_new = jnp.maximum(m_old, jnp.max(s, axis=0, keepdims=True))
        alpha = jnp.exp(m_old - m_new)
        p = jnp.exp(s - m_new)
        acc_ref[...] = alpha * acc_ref[...] + _dot(vals, p.astype(BF16))
        m_ref[...] = m_new

    def sel_body(it, carry):
        s = sa_ref[...]
        sb_ref[...] = group_scores(2 * it + 1)
        absorb(s, 2 * it)
        s = sb_ref[...]
        sa_ref[...] = group_scores(2 * it + 2)
        absorb(s, 2 * it + 1)
        return carry

    lax.fori_loop(0, (cnt + 2 * per - 1) // (2 * per), sel_body, 0)
    o_sel = acc_ref[:dh, :] / acc_ref[dh:dh + 1, :]
    o_win = owin_ref[...]

    gate_t = _sigmoid(gate_ref[...]).T
    for a in range(sub_blocks):
        o_cmp = ocmp_ref[0, 0, a]
        qs = slice(a * Q_BLOCK, (a + 1) * Q_BLOCK)
        for r in range(NSA_REP):
            cols = slice(r * nq + a * Q_BLOCK, r * nq + (a + 1) * Q_BLOCK)
            c0 = N_BRANCH * r
            out_t = (gate_t[c0:c0 + 1, qs] * o_cmp[:, r * Q_BLOCK:(r + 1) * Q_BLOCK]
                     + gate_t[c0 + 1:c0 + 2, qs] * o_sel[:, cols]
                     + gate_t[c0 + 2:c0 + 3, qs] * o_win[:, cols])
            o_ref[qs, r * dh:(r + 1) * dh] = out_t.T.astype(o_ref.dtype)


def _nsa_sub_blocks(seq):
    return 2 if (seq // Q_BLOCK) % 2 == 0 else 1


def _nsa_attn(seg_a, seg_b, pen, ocmp, tiles, onehot, wext, dbias, ubias, wbias, slopes, batch, seq):
    n = batch * seq
    sub = _nsa_sub_blocks(seq)
    nq = sub * Q_BLOCK
    nqb = seq // nq
    gq = NSA_REP * NSA_HEAD_DIM
    g_ = NSA_KV_GROUPS
    dh, kt = NSA_HEAD_DIM, KEY_TILE
    dk = dh + SEL_LANES
    rq = NSA_REP * nq
    n_t = seq // kt

    def kv_spec(kind):
        return pl.BlockSpec((seq, dh), lambda b, g, q, k=kind: (b, A_KV // dh + k * g_ + g))

    def const_spec(arr):
        return pl.BlockSpec(arr.shape, lambda b, g, q, nd=arr.ndim: (0,) * nd)

    return pl.pallas_call(
        functools.partial(_nsa_attn_kernel, seq=seq, sub_blocks=sub),
        grid=(batch, g_, nqb),
        in_specs=[
            pl.BlockSpec(memory_space=pltpu.SMEM),
            pl.BlockSpec((1, 1, 1, 1, SEL_LANES), lambda b, g, q: (b, g, q, 0, 0),
                         memory_space=pltpu.SMEM),
            pl.BlockSpec((nq, gq), lambda b, g, q: (b * nqb + q, A_Q // gq + g)),
            pl.BlockSpec((nq, LANE), lambda b, g, q: (b * nqb + q, B_GATE // LANE + g)),
            pl.BlockSpec((1, 1, nq, SEL_LANES), lambda b, g, q: (b, g, q, 0)),
            pl.BlockSpec((1, 1, sub, NSA_HEAD_DIM, gq), lambda b, g, q: (b, g, q, 0, 0)),
            kv_spec(2), kv_spec(3), kv_spec(4), kv_spec(5),
            const_spec(onehot), const_spec(wext),
            pl.BlockSpec((1,) + dbias.shape[1:], lambda b, g, q: (g, 0, 0)),
            pl.BlockSpec((1,) + ubias.shape[1:], lambda b, g, q: (g, 0, 0)),
            const_spec(wbias),
        ],
        out_specs=pl.BlockSpec((nq, gq), lambda b, g, q: (b * nqb + q, g)),
        out_shape=jax.ShapeDtypeStruct((n, NSA_WIDTH), BF16),
        scratch_shapes=[
            pltpu.VMEM((seq + kt, dk), BF16),
            pltpu.VMEM((n_t + 1, dh + SUM_ROWS, kt), BF16),
            pltpu.VMEM((seq + WINDOW, dh + LANE), BF16),
            pltpu.VMEM((n_t + WIN_TILES, dh + SUM_ROWS, kt), BF16),
            pltpu.VMEM((rq, dk), BF16),
            pltpu.VMEM((rq, dk), BF16),
            pltpu.VMEM((1, rq), F32),
            pltpu.VMEM((dh + SUM_ROWS, rq), F32),
            pltpu.VMEM((dh, rq), F32),
            pltpu.VMEM((SEL_TILES_PER_STEP * KEY_TILE, rq), F32),
            pltpu.VMEM((SEL_TILES_PER_STEP * KEY_TILE, rq), F32),
        ],
        compiler_params=pltpu.CompilerParams(
            dimension_semantics=("parallel", "parallel", "arbitrary"), vmem_limit_bytes=VMEM_LIMIT),
        name="nsa_attn",
    )(slopes, tiles, seg_a, seg_b, pen, ocmp, seg_a, seg_a, seg_a, seg_a, onehot, wext,
      dbias, ubias, wbias)


def _nsa_tables(slopes, seq):
    g_, kt = NSA_KV_GROUPS, KEY_TILE
    pos = np.arange(seq)
    onehot = jnp.asarray(pos[:, None] // SEL_BLOCK == np.arange(SEL_LANES)[None, :], BF16)
    ext = np.zeros((seq, LANE), np.float32)
    ext[:, WCOL_HI] = (pos + WINDOW) // SEL_BLOCK
    ext[:, WCOL_LO] = (pos + WINDOW) % SEL_BLOCK
    ext[:, WCOL_QHI] = 1.0
    ext[:, WCOL_QLO] = 1.0
    wext = jnp.asarray(ext, BF16)

    nq = _nsa_sub_blocks(seq) * Q_BLOCK

    def alibi_in_block(rows):
        u = jnp.asarray((np.arange(rows) % SEL_BLOCK).astype(np.float32))[None, :, None, None]
        t = jnp.broadcast_to(slopes.reshape(g_, 1, NSA_REP, 1) * u, (g_, rows, NSA_REP, nq))
        return t.reshape(g_, rows, NSA_REP * nq)

    ubias = alibi_in_block(SEL_TILES_PER_STEP * kt)
    kq = np.arange(nq)[:, None] <= np.arange(nq)[None, :]
    causal = np.tile(np.where(kq, 0.0, NEG).astype(np.float32), (1, NSA_REP))
    dbias = alibi_in_block(nq) + jnp.asarray(causal)[None]
    ki = np.arange(WINDOW + nq)[:, None]
    qi = np.arange(nq)[None, :]
    band = np.where((ki > qi) & (ki <= qi + WINDOW), 0.0, NEG).astype(np.float32)
    wbias = jnp.asarray(np.tile(band, (1, NSA_REP)))
    return onehot, wext, dbias, ubias, wbias


def _log_sigmoid(x):
    return jnp.minimum(x, 0.0) - jnp.log(1.0 + jnp.exp(-jnp.abs(x)))


def _split3(x):
    hi = x.astype(BF16)
    r1 = x - hi.astype(F32)
    mid = r1.astype(BF16)
    lo = (r1 - mid.astype(F32)).astype(BF16)
    return hi, mid, lo


def _mlstm_kernel(bias_ref, q_ref, k_ref, v_ref, o_ref, ifc_ref, ng_ref,
                  tri_ref, y_ref, c_ref, n_ref, m_ref):
    ch = pl.program_id(1)

    @pl.when(ch == 0)
    def _():
        c_ref[...] = jnp.zeros(c_ref.shape, F32)
        n_ref[...] = jnp.zeros(n_ref.shape, F32)
        m_ref[...] = jnp.zeros(m_ref.shape, F32)

    for sq in range(q_ref.shape[0]):
        _mlstm_chunk(bias_ref, q_ref.at[sq], k_ref.at[sq], v_ref.at[sq], o_ref.at[sq], ifc_ref.at[sq],
                     ng_ref, tri_ref, y_ref.at[sq], c_ref.at[sq], n_ref.at[sq], m_ref.at[sq])


def _mlstm_chunk(bias_ref, q_ref, k_ref, v_ref, o_ref, ifc_ref, ng_ref, tri_ref, y_ref,
                 c_ref, n_ref, m_ref):
    nh, dh = MLSTM_HEADS, MLSTM_HEAD_DIM
    L = q_ref.shape[0]
    tri = tri_ref[...]
    lane8 = lax.broadcasted_iota(jnp.int32, (1, LANE), 1)
    bias_c = jnp.zeros((1, LANE), F32)
    for h in range(nh):
        bias_c = jnp.where(lane8 == h, bias_ref[h], bias_c)
        bias_c = jnp.where(lane8 == nh + h, bias_ref[nh + h], bias_c)
    pre_c = ifc_ref[...] + bias_c
    cum_c = sum(_dot(tri, part) for part in _split3(_log_sigmoid(pre_c)))
    pre_r = pre_c.T[:2 * nh]
    cum_r = sum(_dot_nt(part, tri) for part in _split3(_log_sigmoid(pre_r)))

    rr = lax.broadcasted_iota(jnp.int32, (L, 1), 0)
    cc = lax.broadcasted_iota(jnp.int32, (1, L), 1)
    causal = cc <= rr

    for h in range(nh):
        cols = slice(h * dh, (h + 1) * dh)
        qb = q_ref[:, cols]
        kb = k_ref[:, cols]
        vh = v_ref[:, cols]
        qh = qb.astype(F32)
        kh = kb.astype(F32)
        b_c = cum_c[:, nh + h:nh + h + 1]
        li_c = pre_c[:, h:h + 1]
        b_r = cum_r[nh + h:nh + h + 1, :]
        li_r = pre_r[h:h + 1, :]
        m_prev = m_ref[h:h + 1, 0:1]

        dmat = jnp.where(causal, b_c - b_r + li_r, NEG)
        a = b_c + m_prev
        m_j = jnp.maximum(a, jnp.max(dmat, axis=1, keepdims=True))
        w_intra = jnp.exp(dmat - m_j)
        w_inter = jnp.exp(a - m_j)
        sc = _dot_nt(qb, kb) * w_intra
        c_old = c_ref[h]
        n_old = n_ref[h:h + 1, :]
        num = w_inter * _dot(qb, c_old.astype(BF16)) + _dot(sc.astype(BF16), vh)
        den = (w_inter * jnp.sum(qh * n_old, axis=1, keepdims=True)
               + jnp.sum(sc, axis=1, keepdims=True))
        hid = num / jnp.maximum(jnp.abs(den), jnp.exp(-m_j))

        g_tot = b_r[:, L - 1:L]
        lw_c = g_tot - b_c + li_c
        lw_r = g_tot - b_r + li_r
        m_new = jnp.maximum(g_tot + m_prev, jnp.max(lw_r, axis=1, keepdims=True))
        decay = jnp.exp(g_tot + m_prev - m_new)
        kw = jnp.exp(lw_c - m_new) * kh
        c_ref[h] = decay * c_old + _dot(kw.T.astype(BF16), vh)
        n_ref[h:h + 1, :] = decay * n_old + jnp.sum(kw, axis=0, keepdims=True)
        m_ref[h:h + 1, :] = jnp.broadcast_to(m_new, (1, LANE))

        hn = hid * lax.rsqrt(jnp.mean(hid * hid, axis=-1, keepdims=True) + EPS) * ng_ref[:, cols]
        y_ref[:, cols] = (_sigmoid(o_ref[:, cols]) * hn).astype(y_ref.dtype)


def _mlstm(seg_a, seg_b, bias, norm_g, tri, batch, seq, chunk):
    nc = seq // chunk
    w = MLSTM_WIDTH
    nh, dh = MLSTM_HEADS, MLSTM_HEAD_DIM
    seqs = 1
    a3 = seg_a.reshape(batch, seq, SEG_A)
    b3 = seg_b.reshape(batch, seq, SEG_B)

    def col_spec(off):
        return pl.BlockSpec((seqs, chunk, w), lambda b, c, o=off // w: (b, c, o))

    y = pl.pallas_call(
        _mlstm_kernel,
        grid=(batch // seqs, nc),
        in_specs=[
            pl.BlockSpec(memory_space=pltpu.SMEM),
            col_spec(A_QK), col_spec(A_QK + w), col_spec(A_V), col_spec(B_O),
            pl.BlockSpec((seqs, chunk, LANE), lambda b, c: (b, c, B_IF // LANE)),
            pl.BlockSpec((1, w), lambda b, c: (0, 0)),
            pl.BlockSpec((chunk, chunk), lambda b, c: (0, 0)),
        ],
        out_specs=pl.BlockSpec((seqs, chunk, w), lambda b, c: (b, c, 0)),
        out_shape=jax.ShapeDtypeStruct((batch, seq, w), BF16),
        scratch_shapes=[
            pltpu.VMEM((seqs, nh, dh, dh), F32),
            pltpu.VMEM((seqs, 8, dh), F32),
            pltpu.VMEM((seqs, 8, LANE), F32),
        ],
        compiler_params=pltpu.CompilerParams(
            dimension_semantics=("parallel", "arbitrary"), vmem_limit_bytes=VMEM_LIMIT),
        name="mlstm",
    )(bias, a3, a3, a3, b3, b3, norm_g, tri)
    return y.reshape(batch * seq, w)


def _outproj_kernel(x_ref, ya_ref, ym_ref, wa_ref, wm_ref, o_ref):
    o_ref[...] = x_ref[...] + _dot(ya_ref[...], wa_ref[...]) + _dot(ym_ref[...], wm_ref[...])


def _outproj(x2, ya, ym, wa, wm, tm):
    n, d = x2.shape
    return pl.pallas_call(
        _outproj_kernel,
        grid=(n // tm,),
        in_specs=[
            pl.BlockSpec((tm, d), lambda i: (i, 0)),
            pl.BlockSpec((tm, ya.shape[1]), lambda i: (i, 0)),
            pl.BlockSpec((tm, ym.shape[1]), lambda i: (i, 0)),
            pl.BlockSpec(wa.shape, lambda i: (0, 0)),
            pl.BlockSpec(wm.shape, lambda i: (0, 0)),
        ],
        out_specs=pl.BlockSpec((tm, d), lambda i: (i, 0)),
        out_shape=jax.ShapeDtypeStruct((n, d), F32),
        compiler_params=pltpu.CompilerParams(
            dimension_semantics=("parallel",), vmem_limit_bytes=VMEM_LIMIT),
        name="outproj",
    )(x2, ya, ym, wa, wm)


def _mlp_kernel(x_ref, g_ref, w1_ref, w2_ref, gf_ref, o_ref, h_ref):
    f = pl.program_id(1)

    @pl.when(f == 0)
    def _():
        x = x_ref[...]
        r = lax.rsqrt(jnp.mean(x * x, axis=-1, keepdims=True) + EPS)
        h_ref[...] = (x * r * g_ref[...]).astype(BF16)
        o_ref[...] = x

    u = jnp.maximum(_dot(h_ref[...], w1_ref[...]), 0.0)
    o_ref[...] += _dot((u * u).astype(BF16), w2_ref[...])

    @pl.when(f == pl.num_programs(1) - 1)
    def _():
        x2 = o_ref[...]
        r = lax.rsqrt(jnp.mean(x2 * x2, axis=-1, keepdims=True) + EPS)
        o_ref[...] = x2 * r * gf_ref[...]


def _mlp(x1, g, w1, w2, gf, tm, tf):
    n, d = x1.shape
    dff = w1.shape[1]
    return pl.pallas_call(
        _mlp_kernel,
        grid=(n // tm, dff // tf),
        in_specs=[
            pl.BlockSpec((tm, d), lambda i, f: (jnp.minimum(i + jnp.minimum(f, 1), n // tm - 1), 0)),
            pl.BlockSpec((1, d), lambda i, f: (0, 0)),
            pl.BlockSpec((d, tf), lambda i, f: (0, f)),
            pl.BlockSpec((tf, d), lambda i, f: (f, 0)),
            pl.BlockSpec((1, d), lambda i, f: (0, 0)),
        ],
        out_specs=pl.BlockSpec((tm, d), lambda i, f: (i, 0)),
        out_shape=jax.ShapeDtypeStruct((n, d), F32),
        scratch_shapes=[pltpu.VMEM((tm, d), BF16)],
        compiler_params=pltpu.CompilerParams(
            dimension_semantics=("parallel", "arbitrary"), vmem_limit_bytes=VMEM_LIMIT),
        name="mlp",
    )(x1, g, w1, w2, gf)


def _row_tile(n, want):
    t = want
    while n % t:
        t //= 2
    return t


def _layer(x2, batch, seq, norm_mix_g, w_in, w_cmp_k1, w_cmp_k2, pos_cmp_k, w_cmp_v1, w_cmp_v2,
           pos_cmp_v, conv_w, conv_b, b_igate, b_fgate, mlstm_norm_g, w_out, norm_mlp_g,
           w_mlp_in, w_mlp_out):
    n, d = x2.shape
    assert seq % Q_BLOCK == 0 and seq >= WINDOW + Q_BLOCK and seq // SEL_BLOCK <= SEL_LANES
    g_ = NSA_KV_GROUPS
    nh = MLSTM_HEADS

    c_gate = NSA_WIDTH + 6 * NSA_KV_WIDTH
    c_qk = c_gate + NSA_HEADS * N_BRANCH
    c_v = c_qk + 2 * MLSTM_WIDTH
    c_o = c_v + MLSTM_WIDTH
    c_i = c_o + MLSTM_WIDTH
    c_f = c_i + nh
    w16 = w_in.astype(BF16)
    gate_cols = []
    per_g = NSA_REP * N_BRANCH
    for g in range(g_):
        gate_cols += [w16[:, c_gate + g * per_g:c_gate + (g + 1) * per_g],
                      jnp.zeros((d, LANE - per_g), BF16)]
    w_ab = jnp.concatenate(
        [w16[:, c_v:c_o], w16[:, c_qk:c_v], w16[:, :c_gate],
         w16[:, c_o:c_i]] + gate_cols
        + [w16[:, c_i:c_f + nh], jnp.zeros((d, SEG_B - B_IF - 2 * nh), BF16)],
        axis=1)
    scale = jnp.concatenate([jnp.ones((1, A_QK + MLSTM_WIDTH), F32),
                             jnp.full((1, MLSTM_WIDTH), MLSTM_HEAD_DIM ** -0.5, F32),
                             jnp.full((1, NSA_WIDTH), NSA_HEAD_DIM ** -0.5, F32),
                             jnp.ones((1, SEG_A - A_KV + SEG_B), F32)], axis=1)
    conv_pad = ((0, 0), (A_QK, SEG_A - A_Q + SEG_B))
    cw_ab = jnp.pad(conv_w, conv_pad)
    cb_ab = jnp.pad(conv_b.reshape(1, -1), conv_pad)
    g_mix = norm_mix_g.reshape(1, d)

    tm = _row_tile(seq, ROW_TILE)
    seg_a, seg_b = _inproj(x2, g_mix, w_ab, scale, cw_ab, cb_ab, tm, COL_TILE, (A_QK, A_Q), seq)

    n_sub = seq // CMP_STRIDE
    dh = NSA_HEAD_DIM
    w1s = jnp.stack([w_cmp_k1, w_cmp_v1]).reshape(2, 2, CMP_STRIDE, dh, dh)
    w1r = jnp.concatenate([w1s[:, 0], w1s[:, 1]], axis=-1).astype(BF16)
    w2s = jnp.stack([w_cmp_k2, w_cmp_v2]).astype(BF16)
    poss = jnp.stack([pos_cmp_k, pos_cmp_v]).reshape(2, 2, CMP_STRIDE, dh).transpose(0, 2, 1, 3)
    posr = jnp.pad(poss, ((0, 0), (0, 0), (0, 6), (0, 0))).astype(BF16)
    kvc, kvct = _compress(seg_a, w1r, w2s, posr, batch, seq)

    cmp_start = np.arange(n_sub) * CMP_STRIDE
    sel_start = np.arange(SEL_LANES) * SEL_BLOCK
    ovt = ((cmp_start[None, :] < sel_start[:, None] + SEL_BLOCK)
           & (cmp_start[None, :] + CMP_BLOCK - 1 >= sel_start[:, None])
           & (np.arange(n_sub)[None, :] < n_sub - CMP_BLOCK // CMP_STRIDE + 1))
    ovt = jnp.asarray(ovt, BF16)
    slopes = jnp.exp2(-8.0 * jnp.arange(1, NSA_HEADS + 1, dtype=F32) / NSA_HEADS)
    ocmp, pen, tiles = _nsa_cmp(seg_a, kvc, kvct, ovt, slopes, batch, seq)
    onehot, wext, dbias, ubias, wbias = _nsa_tables(slopes, seq)
    y_a = _nsa_attn(seg_a, seg_b, pen, ocmp, tiles, onehot, wext, dbias, ubias, wbias,
                    slopes, batch, seq)

    chunk = 256 if seq % 256 == 0 else 128
    bias = jnp.concatenate([b_igate, b_fgate]).astype(F32)
    tri = jnp.asarray(np.tril(np.ones((chunk, chunk), np.float32)), BF16)
    y_m = _mlstm(seg_a, seg_b, bias, mlstm_norm_g.reshape(1, -1), tri, batch, seq, chunk)

    w_o = w_out.astype(BF16)
    x1 = _outproj(x2, y_a, y_m, w_o[:NSA_WIDTH], w_o[NSA_WIDTH:], _row_tile(n, 512))
    return x1, (norm_mlp_g.reshape(1, d), w_mlp_in.astype(BF16), w_mlp_out.astype(BF16))


def kernel(x, norm_mix_g, w_in, w_cmp_k1, w_cmp_k2, pos_cmp_k, w_cmp_v1, w_cmp_v2, pos_cmp_v, conv_w, conv_b, b_igate, b_fgate, mlstm_norm_g, w_out, norm_mlp_g, w_mlp_in, w_mlp_out, norm_f_g):
    batch, seq, d = x.shape
    depth = w_in.shape[0]
    assert depth == 1, "the final RMSNorm is fused into the last layer's channel mixer"
    x2 = x.reshape(batch * seq, d)
    tm = _row_tile(batch * seq, ROW_TILE)
    for l in range(depth):
        x1, (g_mlp, w1, w2) = _layer(
            x2, batch, seq, norm_mix_g[l], w_in[l], w_cmp_k1[l], w_cmp_k2[l], pos_cmp_k[l],
            w_cmp_v1[l], w_cmp_v2[l], pos_cmp_v[l], conv_w[l], conv_b[l], b_igate[l], b_fgate[l],
            mlstm_norm_g[l], w_out[l], norm_mlp_g[l], w_mlp_in[l], w_mlp_out[l])
        x2 = _mlp(x1, g_mlp, w1, w2, norm_f_g.reshape(1, d), tm, 512)
    return x2.reshape(batch, seq, d)
```

```python
import functools

import numpy as np
import jax
import jax.numpy as jnp
from jax import lax
from jax.experimental import pallas as pl
from jax.experimental.pallas import tpu as pltpu

F32 = jnp.float32
BF16 = jnp.bfloat16

EPS = 1e-6
NEG = -1e30
FORCE_BONUS = 1e4
PICKED = -3e38
MASK_BIG = 1e30

D_MODEL = 2048
NSA_HEAD_DIM = 128
NSA_WIDTH = D_MODEL // 2
NSA_HEADS = NSA_WIDTH // NSA_HEAD_DIM
NSA_REP = 4
NSA_KV_GROUPS = NSA_HEADS // NSA_REP
NSA_KV_WIDTH = NSA_KV_GROUPS * NSA_HEAD_DIM
CMP_BLOCK = 32
CMP_STRIDE = 16
SEL_BLOCK = 64
SEL_TOPK = 16
WINDOW = 512
Q_BLOCK = 128
N_BRANCH = 3
MLSTM_HEAD_DIM = 256
MLSTM_WIDTH = D_MODEL - NSA_WIDTH
MLSTM_HEADS = MLSTM_WIDTH // MLSTM_HEAD_DIM
CONV_WIDTH = 4

LANE = 128
SEL_LANES = 128
KEY_TILE = 128
SEL_TILES_PER_STEP = 2
WIN_TILES = WINDOW // KEY_TILE
SUM_ROWS = 16
VMEM_LIMIT = 56 * 1024 * 1024
ROW_TILE = 1024

A_V, A_QK = 0, MLSTM_WIDTH
A_Q = A_QK + 2 * MLSTM_WIDTH
A_KV = A_Q + NSA_WIDTH
SEG_A = A_KV + 6 * NSA_KV_WIDTH
B_O = 0
B_GATE = MLSTM_WIDTH
B_IF = B_GATE + NSA_KV_GROUPS * LANE
SEG_B = B_IF + 2 * LANE
COL_TILE = 512

WCOL_PAD, WCOL_HI, WCOL_LO, WCOL_QHI, WCOL_QLO = 0, 1, 2, 3, 4


def _dot(a, b):
    return jnp.dot(a, b, preferred_element_type=F32)


def _dot_nt(a, b):
    return lax.dot_general(a, b, (((1,), (1,)), ((), ())), preferred_element_type=F32)


def _sigmoid(x):
    return 1.0 / (1.0 + jnp.exp(-x))


def _shifted(x, tail, s):
    xs = pltpu.roll(x, s, axis=0)
    ts = pltpu.roll(tail, s, axis=0)
    row8 = lax.broadcasted_iota(jnp.int32, (8, 1), 0)
    head = jnp.where(row8 < s, ts, xs[:8])
    return jnp.concatenate([head, xs[8:]], axis=0)


def _conv_silu(x, tail, w, b):
    y = b + _shifted(x, tail, CONV_WIDTH - 1) * w[0:1]
    for i in range(1, CONV_WIDTH - 1):
        y = y + _shifted(x, tail, CONV_WIDTH - 1 - i) * w[i:i + 1]
    y = y + x * w[CONV_WIDTH - 1:CONV_WIDTH]
    return y * _sigmoid(y)


def _inproj_kernel(x_ref, g_ref, w_ref, cs_ref, cw_ref, cb_ref, oa_ref, ob_ref, h_ref, halo_ref, *,
                   nb, n_plain, tiles_per_seq):
    i = pl.program_id(0)
    j = pl.program_id(1)

    @pl.when(j == 0)
    def _():
        x = x_ref[...]
        r = lax.rsqrt(jnp.mean(x * x, axis=-1, keepdims=True) + EPS)
        h_ref[...] = (x * r * g_ref[...]).astype(BF16)

    is_conv = j >= nb + n_plain

    @pl.when((j >= nb) & jnp.logical_not(is_conv))
    def _():
        oa_ref[...] = (_dot(h_ref[...], w_ref[...]) * cs_ref[...]).astype(oa_ref.dtype)

    @pl.when(j < nb)
    def _():
        ob_ref[...] = (_dot(h_ref[...], w_ref[...]) * cs_ref[...]).astype(ob_ref.dtype)

    @pl.when(is_conv)
    def _():
        slot = j - (nb + n_plain)

        @pl.when(i % tiles_per_seq == 0)
        def _():
            halo_ref[slot] = jnp.zeros(halo_ref.shape[1:], F32)

        acc = _dot(h_ref[...], w_ref[...])
        tail = halo_ref[slot]
        halo_ref[slot] = acc[acc.shape[0] - 8:]
        y = _conv_silu(acc, tail, cw_ref[...], cb_ref[...])
        oa_ref[...] = (y * cs_ref[...]).astype(oa_ref.dtype)


def _inproj(x2, g, w, cs, cw, cb, tm, tn, conv_cols, seq):
    n, d = x2.shape
    na, nb = SEG_A // tn, SEG_B // tn
    conv_lo, conv_hi = conv_cols[0] // tn, conv_cols[1] // tn
    assert conv_cols[0] % tn == 0 and conv_cols[1] % tn == 0 and seq % tm == 0 and conv_hi <= na
    assert SEG_A % tn == 0 and SEG_B % tn == 0 and w.shape[1] == SEG_A + SEG_B
    n_conv = conv_hi - conv_lo
    n_plain = na - n_conv

    def a_tile(j):
        p = jnp.maximum(j - nb, 0)
        plain = jnp.where(p < conv_lo, p, p + n_conv)
        return jnp.where(p < n_plain, plain, conv_lo + p - n_plain)

    def w_tile(i, j):
        return (0, jnp.where(j < nb, na + j, a_tile(j)))

    return pl.pallas_call(
        functools.partial(_inproj_kernel, nb=nb, n_plain=n_plain, tiles_per_seq=seq // tm),
        grid=(n // tm, na + nb),
        in_specs=[
            pl.BlockSpec((tm, d), lambda i, j: (i, 0)),
            pl.BlockSpec((1, d), lambda i, j: (0, 0)),
            pl.BlockSpec((d, tn), w_tile),
            pl.BlockSpec((1, tn), w_tile),
            pl.BlockSpec((CONV_WIDTH, tn), w_tile),
            pl.BlockSpec((1, tn), w_tile),
        ],
        out_specs=[pl.BlockSpec((tm, tn), lambda i, j: (i, a_tile(j))),
                   pl.BlockSpec((tm, tn), lambda i, j: (i, jnp.minimum(j, nb - 1)))],
        out_shape=[jax.ShapeDtypeStruct((n, SEG_A), BF16), jax.ShapeDtypeStruct((n, SEG_B), F32)],
        scratch_shapes=[pltpu.VMEM((tm, d), BF16),
                        pltpu.VMEM((conv_hi - conv_lo, 8, tn), F32)],
        compiler_params=pltpu.CompilerParams(
            dimension_semantics=("arbitrary", "arbitrary"), vmem_limit_bytes=VMEM_LIMIT),
        name="inproj",
    )(x2, g, w, cs, cw, cb)


def _compress_kernel(x_ref, w1_ref, w2_ref, pos_ref, o_ref, ot_ref, xf_ref):
    n_sub = o_ref.shape[2]
    dh = NSA_HEAD_DIM
    xf_ref[...] = x_ref[...].astype(F32)
    acc = jnp.zeros((n_sub, 2 * dh), F32)
    posw = jnp.zeros((1, dh), F32)
    for p in range(CMP_STRIDE):
        wp = w1_ref[0, p]
        acc = acc + _dot(xf_ref[pl.ds(p, n_sub, stride=CMP_STRIDE), :].astype(BF16), wp)
        pw = _dot(pos_ref[0, p], wp)
        posw = posw + pw[0:1, :dh] + pw[1:2, dh:]
    bot = pltpu.roll(acc[:, dh:], n_sub - 1, axis=0)
    pre = acc[:, :dh] + bot + posw
    hid = pre * _sigmoid(pre)
    out = _dot(hid.astype(BF16), w2_ref[0])
    o_ref[0, 0] = out.astype(o_ref.dtype)
    ot_ref[0, 0] = out.T.astype(ot_ref.dtype)


def _compress(seg_a, w1r, w2s, posr, batch, seq):
    g = NSA_KV_GROUPS
    c = 2 * g
    dh = NSA_HEAD_DIM
    n_sub = seq // CMP_STRIDE
    return pl.pallas_call(
        _compress_kernel,
        grid=(batch, c),
        in_specs=[
            pl.BlockSpec((seq, dh), lambda i, j: (i, A_KV // dh + j)),
            pl.BlockSpec((1, CMP_STRIDE, dh, 2 * dh), lambda i, j: (j // g, 0, 0, 0)),
            pl.BlockSpec((1, dh, dh), lambda i, j: (j // g, 0, 0)),
            pl.BlockSpec((1, CMP_STRIDE, 8, dh), lambda i, j: (j // g, 0, 0, 0)),
        ],
        out_specs=[pl.BlockSpec((1, 1, n_sub, dh), lambda i, j: (i, j, 0, 0)),
                   pl.BlockSpec((1, 1, dh, n_sub), lambda i, j: (i, j, 0, 0))],
        out_shape=[jax.ShapeDtypeStruct((batch, c, n_sub, dh), BF16),
                   jax.ShapeDtypeStruct((batch, c, dh, n_sub), BF16)],
        scratch_shapes=[pltpu.VMEM((seq, dh), F32)],
        compiler_params=pltpu.CompilerParams(
            dimension_semantics=("parallel", "parallel"), vmem_limit_bytes=VMEM_LIMIT),
        name="compress",
    )(seg_a, w1r, w2s, posr)


def _stack_heads(q_all):
    dh = NSA_HEAD_DIM
    return jnp.concatenate([q_all[:, r * dh:(r + 1) * dh] for r in range(NSA_REP)], axis=0)


def _tile_list(hit_col, tile0, pad_tile, mats_ref):
    n = SEL_LANES
    lane = lax.broadcasted_iota(jnp.int32, (1, n), 1)
    blk = lax.broadcasted_iota(jnp.int32, (n, n), 0)
    slot = lax.broadcasted_iota(jnp.int32, (n, n), 1).astype(F32)
    hit = jnp.broadcast_to(hit_col, (n, n))
    both = jnp.maximum(hit, pltpu.roll(hit, n - 1, axis=0))
    act = jnp.where((both > 0.0) & (blk % 2 == 0) & (blk // 2 < tile0), 1.0, 0.0)
    rank = _dot(mats_ref[0], act.astype(BF16))
    cnt = jnp.sum(act, axis=0, keepdims=True)
    place = jnp.where((act > 0.0) & (rank == slot), 1.0, 0.0)
    tiles = _dot(mats_ref[1, 0:8], place.astype(BF16))[0:1]
    out = jnp.where(lane.astype(F32) < cnt, tiles, float(pad_tile))
    return jnp.where(lane == n - 1, cnt, out).astype(jnp.int32)


def _nsa_cmp_kernel(slopes_ref, q_ref, kc_ref, vct_ref, ovt_ref, mats_ref, ocmp_ref, pen_ref, tiles_ref, *,
                    seq, sub_blocks):
    step = pl.program_id(2)
    nq_step = sub_blocks * Q_BLOCK
    chunk = min(SEL_LANES, kc_ref.shape[2])
    n_chunks = kc_ref.shape[2] // chunk
    need = ((step + 1) * nq_step - CMP_BLOCK) // CMP_STRIDE + 1
    n_need = (need + chunk - 1) // chunk
    for v in range(1, n_chunks + 1):
        cond = (n_need == v) if v < n_chunks else (n_need >= v)
        if v == 1:
            cond = n_need <= 1

        @pl.when(cond)
        def _(v=v):
            hit = None
            for sub in range(sub_blocks):
                rows = slice(sub * Q_BLOCK, (sub + 1) * Q_BLOCK)
                h = _nsa_cmp_block(slopes_ref, q_ref[rows, :], kc_ref, vct_ref, ovt_ref,
                                   ocmp_ref.at[0, 0, sub], pen_ref.at[0, 0, rows],
                                   step * sub_blocks + sub, seq, v * chunk,
                                   past_first_block=(sub >= 1 or v >= 2))
                hit = h if hit is None else jnp.maximum(hit, h)
            tiles_ref[0, 0, 0] = _tile_list(hit, step * sub_blocks, seq // KEY_TILE, mats_ref)


def _nsa_cmp_block(slopes_ref, q_all, kc_ref, vct_ref, ovt_ref, ocmp_ref, pen_ref, qb, seq, n_use,
                   past_first_block):
    g = pl.program_id(1)
    nq = Q_BLOCK
    n_cpad = n_use
    n_cmp = seq // CMP_STRIDE - CMP_BLOCK // CMP_STRIDE + 1
    t0 = qb * nq

    s_t = _dot_nt(kc_ref[0, 0, :n_use, :], _stack_heads(q_all))
    n_s = lax.broadcasted_iota(jnp.int32, (n_cpad, 1), 0)
    q_l = lax.broadcasted_iota(jnp.int32, (1, nq), 1)
    dist = (t0 - (CMP_BLOCK - 1)) + q_l - n_s * CMP_STRIDE
    valid = (dist >= 0) & (n_s < n_cmp)
    dist_f = dist.astype(F32)
    probs = []
    p_sum = jnp.zeros((n_cpad, nq), F32)
    for r in range(NSA_REP):
        slope = slopes_ref[g * NSA_REP + r]
        s = jnp.where(valid, s_t[:, r * nq:(r + 1) * nq] - slope * dist_f, NEG)
        m = jnp.max(s, axis=0, keepdims=True)
        e = jnp.exp(s - m)
        inv = jnp.where(m > 0.5 * NEG, 1.0 / jnp.sum(e, axis=0, keepdims=True), 0.0)
        p = e * inv
        probs.append(p.astype(BF16))
        p_sum = p_sum + p
    ocmp_ref[...] = _dot(vct_ref[0, 0, :, :n_use], jnp.concatenate(probs, axis=1))

    p_hi = p_sum.astype(BF16)
    p_lo = (p_sum - p_hi.astype(F32)).astype(BF16)
    ovt = ovt_ref[:, :n_use]
    imp = _dot(ovt, p_hi) + _dot(ovt, p_lo)
    j_i = lax.broadcasted_iota(jnp.int32, (SEL_LANES, 1), 0)
    t_l = t0 + q_l
    cur = t_l // SEL_BLOCK
    forced = (j_i == 0) | (j_i == cur) | (j_i == cur - 1)
    causal_blk = j_i * SEL_BLOCK <= t_l
    val = jnp.where(causal_blk, jnp.where(forced, imp + FORCE_BONUS, imp), NEG)
    j_f = j_i.astype(F32)
    sel_t = jnp.zeros((SEL_LANES, nq), F32)
    rounds = min(SEL_TOPK, seq // SEL_BLOCK)
    if past_first_block and rounds >= 3:
        sel_t = jnp.where(forced, 1.0, sel_t)
        val = jnp.where(forced, PICKED, val)
        rounds -= 3
    for _ in range(rounds):
        mx = jnp.max(val, axis=0, keepdims=True)
        first = jnp.min(jnp.where(val == mx, j_f, float(SEL_LANES)), axis=0, keepdims=True)
        pick = j_f == first
        sel_t = jnp.where(pick, 1.0, sel_t)
        val = jnp.where(pick, PICKED, val)
    sel = sel_t.T
    pen_ref[...] = ((sel - 1.0) * MASK_BIG).astype(pen_ref.dtype)
    return jnp.max(sel_t, axis=1, keepdims=True)


def _nsa_cmp(seg_a, kvc, kvct, ovt, slopes, batch, seq):
    nqb = seq // Q_BLOCK
    gq = NSA_REP * NSA_HEAD_DIM
    g_ = NSA_KV_GROUPS
    n_cpad = kvc.shape[2]
    sub = _nsa_sub_blocks(seq)
    nstep = nqb // sub
    idx = np.arange(SEL_LANES)
    mats = jnp.asarray(np.stack([idx[:, None] > idx[None, :],
                                 np.broadcast_to(idx[None, :] // 2, (SEL_LANES, SEL_LANES))]), BF16)
    return pl.pallas_call(
        functools.partial(_nsa_cmp_kernel, seq=seq, sub_blocks=sub),
        grid=(batch, g_, nstep),
        in_specs=[
            pl.BlockSpec(memory_space=pltpu.SMEM),
            pl.BlockSpec((sub * Q_BLOCK, gq), lambda b, g, q: (b * nstep + q, A_Q // gq + g)),
            pl.BlockSpec((1, 1, n_cpad, NSA_HEAD_DIM), lambda b, g, q: (b, g, 0, 0)),
            pl.BlockSpec((1, 1, NSA_HEAD_DIM, n_cpad), lambda b, g, q: (b, g_ + g, 0, 0)),
            pl.BlockSpec((SEL_LANES, n_cpad), lambda b, g, q: (0, 0)),
            pl.BlockSpec((2, SEL_LANES, SEL_LANES), lambda b, g, q: (0, 0, 0)),
        ],
        out_specs=[
            pl.BlockSpec((1, 1, sub, NSA_HEAD_DIM, gq), lambda b, g, q: (b, g, q, 0, 0)),
            pl.BlockSpec((1, 1, sub * Q_BLOCK, SEL_LANES), lambda b, g, q: (b, g, q, 0)),
            pl.BlockSpec((1, 1, 1, 1, SEL_LANES), lambda b, g, q: (b, g, q, 0, 0)),
        ],
        out_shape=[
            jax.ShapeDtypeStruct((batch, g_, nqb, NSA_HEAD_DIM, gq), F32),
            jax.ShapeDtypeStruct((batch, g_, seq, SEL_LANES), BF16),
            jax.ShapeDtypeStruct((batch, g_, nstep, 1, SEL_LANES), jnp.int32),
        ],
        compiler_params=pltpu.CompilerParams(
            dimension_semantics=("parallel", "parallel", "parallel"), vmem_limit_bytes=VMEM_LIMIT),
        name="nsa_cmp",
    )(slopes, seg_a, kvc, kvct, ovt, mats)


def _nsa_attn_kernel(slopes_ref, tiles_ref, q_ref, gate_ref, pen_ref, ocmp_ref, ks_ref, vs_ref,
                     kw_ref, vw_ref, onehot_ref, wext_ref, dbias_ref, ubias_ref, wbias_ref, o_ref,
                     ksel_ref, vselt_ref, kwin_ref, vwint_ref,
                     qa_ref, qw_ref, m_ref, acc_ref, owin_ref, sa_ref, sb_ref,
                     *, seq, sub_blocks):
    g = pl.program_id(1)
    step = pl.program_id(2)
    dh = NSA_HEAD_DIM
    nq = sub_blocks * Q_BLOCK
    kt = KEY_TILE
    per = SEL_TILES_PER_STEP
    pad_tile = seq // kt
    t0 = step * nq
    tile0 = step * sub_blocks

    @pl.when(step == 0)
    def _():
        ksel_ref[0:seq, :dh] = ks_ref[...]
        ksel_ref[0:seq, dh:] = onehot_ref[...]
        ksel_ref[seq:, :dh] = jnp.zeros((kt, dh), BF16)
        ksel_ref[seq:, dh:] = jnp.ones((kt, SEL_LANES), BF16)
        lane2 = lax.broadcasted_iota(jnp.int32, (WINDOW, dh + LANE), 1)
        kwin_ref[0:WINDOW, :] = jnp.where(lane2 == dh + WCOL_PAD, 1.0, 0.0).astype(BF16)
        kwin_ref[WINDOW:, :dh] = kw_ref[...]
        kwin_ref[WINDOW:, dh:] = wext_ref[...]
        zero_tile = jnp.zeros((dh + SUM_ROWS, kt), BF16)
        ones_rows = jnp.ones((SUM_ROWS, kt), BF16)
        vselt_ref[pad_tile] = zero_tile
        for i in range(WIN_TILES):
            vwint_ref[i] = zero_tile

        def transpose_tile(t, carry):
            r0 = pl.multiple_of(t * kt, kt)
            vselt_ref[t, :dh] = vs_ref[pl.ds(r0, kt), :].astype(F32).T.astype(BF16)
            vselt_ref[t, dh:] = ones_rows
            vwint_ref[t + WIN_TILES, :dh] = vw_ref[pl.ds(r0, kt), :].astype(F32).T.astype(BF16)
            vwint_ref[t + WIN_TILES, dh:] = ones_rows
            return carry

        lax.fori_loop(0, seq // kt, transpose_tile, 0)

    slopes = [slopes_ref[g * NSA_REP + r] for r in range(NSA_REP)]
    row_q = lax.broadcasted_iota(jnp.int32, (nq, 1), 0)
    lane = lax.broadcasted_iota(jnp.int32, (1, LANE), 1)
    q_all = q_ref[...]

    pen = pen_ref[0, 0].astype(F32)
    blk_rel = ((lane - (t0 + row_q) // SEL_BLOCK) * SEL_BLOCK).astype(F32)
    for r in range(NSA_REP):
        qa_ref[r * nq:(r + 1) * nq, :dh] = q_all[:, r * dh:(r + 1) * dh]
        qa_ref[r * nq:(r + 1) * nq, dh:] = (pen + slopes[r] * blk_rel).astype(BF16)

    def listed(idx):
        return tiles_ref[0, 0, 0, 0, idx]

    cnt = listed(SEL_LANES - 1)

    def group_scores(it):
        keys = jnp.concatenate(
            [ksel_ref[pl.ds(pl.multiple_of(listed(it * per + i) * kt, kt), kt), :]
             for i in range(per)], axis=0)
        return _dot_nt(keys, qa_ref[...]) + ubias_ref[0]

    s = _dot_nt(ksel_ref[pl.ds(pl.multiple_of(t0, kt), nq), :], qa_ref[...]) + dbias_ref[0]
    m0 = jnp.max(s, axis=0, keepdims=True)
    p = jnp.exp(s - m0)
    m_ref[...] = m0
    vals = jnp.concatenate([vselt_ref[tile0 + a] for a in range(sub_blocks)], axis=1)
    acc_ref[...] = _dot(vals, p.astype(BF16))
    sa_ref[...] = group_scores(0)

    tp = t0 + WINDOW + row_q
    t_hi = (tp // SEL_BLOCK).astype(F32)
    t_lo = (tp % SEL_BLOCK).astype(F32)
    for r in range(NSA_REP):
        sl = slopes[r]
        ext = jnp.where(lane == WCOL_PAD, -MASK_BIG, 0.0)
        ext = jnp.where(lane == WCOL_HI, sl * SEL_BLOCK, ext)
        ext = jnp.where(lane == WCOL_LO, sl, ext)
        ext = jnp.where(lane == WCOL_QHI, -sl * SEL_BLOCK * t_hi, ext)
        ext = jnp.where(lane == WCOL_QLO, -sl * t_lo, ext)
        qw_ref[r * nq:(r + 1) * nq, :dh] = q_all[:, r * dh:(r + 1) * dh]
        qw_ref[r * nq:(r + 1) * nq, dh:] = ext.astype(BF16)
    wlen = WINDOW + nq
    s = _dot_nt(kwin_ref[pl.ds(pl.multiple_of(t0, kt), wlen), :], qw_ref[...]) + wbias_ref[...]
    e = jnp.exp(s - jnp.max(s, axis=0, keepdims=True))
    vwin = jnp.concatenate([vwint_ref[tile0 + i] for i in range(WIN_TILES + sub_blocks)], axis=1)
    ow = _dot(vwin, e.astype(BF16))
    owin_ref[...] = ow[:dh] / ow[dh:dh + 1]

    def absorb(s, grp):
        vals = jnp.concatenate([vselt_ref[listed(grp * per + i)] for i in range(per)], axis=1)
        m_old = m_ref[...]
        m_new = jnp.maximum(m_old, jnp.max(s, axis=0, keepdims=True))
        alpha = jnp.exp(m_old - m_new)
        p = jnp.exp(s - m_new)
        acc_ref[...] = alpha * acc_ref[...] + _dot(vals, p.astype(BF16))
        m_ref[...] = m_new

    def sel_body(it, carry):
        s = sa_ref[...]
        sb_ref[...] = group_scores(2 * it + 1)
        absorb(s, 2 * it)
        s = sb_ref[...]
        sa_ref[...] = group_scores(2 * it + 2)
        absorb(s, 2 * it + 1)
        return carry

    lax.fori_loop(0, (cnt + 2 * per - 1) // (2 * per), sel_body, 0)
    o_sel = acc_ref[:dh, :] / acc_ref[dh:dh + 1, :]
    o_win = owin_ref[...]

    gate_t = _sigmoid(gate_ref[...]).T
    for a in range(sub_blocks):
        o_cmp = ocmp_ref[0, 0, a]
        qs = slice(a * Q_BLOCK, (a + 1) * Q_BLOCK)
        for r in range(NSA_REP):
            cols = slice(r * nq + a * Q_BLOCK, r * nq + (a + 1) * Q_BLOCK)
            c0 = N_BRANCH * r
            out_t = (gate_t[c0:c0 + 1, qs] * o_cmp[:, r * Q_BLOCK:(r + 1) * Q_BLOCK]
                     + gate_t[c0 + 1:c0 + 2, qs] * o_sel[:, cols]
                     + gate_t[c0 + 2:c0 + 3, qs] * o_win[:, cols])
            o_ref[qs, r * dh:(r + 1) * dh] = out_t.T.astype(o_ref.dtype)


def _nsa_sub_blocks(seq):
    return 2 if (seq // Q_BLOCK) % 2 == 0 else 1


def _nsa_attn(seg_a, seg_b, pen, ocmp, tiles, onehot, wext, dbias, ubias, wbias, slopes, batch, seq):
    n = batch * seq
    sub = _nsa_sub_blocks(seq)
    nq = sub * Q_BLOCK
    nqb = seq // nq
    gq = NSA_REP * NSA_HEAD_DIM
    g_ = NSA_KV_GROUPS
    dh, kt = NSA_HEAD_DIM, KEY_TILE
    dk = dh + SEL_LANES
    rq = NSA_REP * nq
    n_t = seq // kt

    def kv_spec(kind):
        return pl.BlockSpec((seq, dh), lambda b, g, q, k=kind: (b, A_KV // dh + k * g_ + g))

    def const_spec(arr):
        return pl.BlockSpec(arr.shape, lambda b, g, q, nd=arr.ndim: (0,) * nd)

    return pl.pallas_call(
        functools.partial(_nsa_attn_kernel, seq=seq, sub_blocks=sub),
        grid=(batch, g_, nqb),
        in_specs=[
            pl.BlockSpec(memory_space=pltpu.SMEM),
            pl.BlockSpec((1, 1, 1, 1, SEL_LANES), lambda b, g, q: (b, g, q, 0, 0),
                         memory_space=pltpu.SMEM),
            pl.BlockSpec((nq, gq), lambda b, g, q: (b * nqb + q, A_Q // gq + g)),
            pl.BlockSpec((nq, LANE), lambda b, g, q: (b * nqb + q, B_GATE // LANE + g)),
            pl.BlockSpec((1, 1, nq, SEL_LANES), lambda b, g, q: (b, g, q, 0)),
            pl.BlockSpec((1, 1, sub, NSA_HEAD_DIM, gq), lambda b, g, q: (b, g, q, 0, 0)),
            kv_spec(2), kv_spec(3), kv_spec(4), kv_spec(5),
            const_spec(onehot), const_spec(wext),
            pl.BlockSpec((1,) + dbias.shape[1:], lambda b, g, q: (g, 0, 0)),
            pl.BlockSpec((1,) + ubias.shape[1:], lambda b, g, q: (g, 0, 0)),
            const_spec(wbias),
        ],
        out_specs=pl.BlockSpec((nq, gq), lambda b, g, q: (b * nqb + q, g)),
        out_shape=jax.ShapeDtypeStruct((n, NSA_WIDTH), BF16),
        scratch_shapes=[
            pltpu.VMEM((seq + kt, dk), BF16),
            pltpu.VMEM((n_t + 1, dh + SUM_ROWS, kt), BF16),
            pltpu.VMEM((seq + WINDOW, dh + LANE), BF16),
            pltpu.VMEM((n_t + WIN_TILES, dh + SUM_ROWS, kt), BF16),
            pltpu.VMEM((rq, dk), BF16),
            pltpu.VMEM((rq, dk), BF16),
            pltpu.VMEM((1, rq), F32),
            pltpu.VMEM((dh + SUM_ROWS, rq), F32),
            pltpu.VMEM((dh, rq), F32),
            pltpu.VMEM((SEL_TILES_PER_STEP * KEY_TILE, rq), F32),
            pltpu.VMEM((SEL_TILES_PER_STEP * KEY_TILE, rq), F32),
        ],
        compiler_params=pltpu.CompilerParams(
            dimension_semantics=("parallel", "parallel", "arbitrary"), vmem_limit_bytes=VMEM_LIMIT),
        name="nsa_attn",
    )(slopes, tiles, seg_a, seg_b, pen, ocmp, seg_a, seg_a, seg_a, seg_a, onehot, wext,
      dbias, ubias, wbias)


def _nsa_tables(slopes, seq):
    g_, kt = NSA_KV_GROUPS, KEY_TILE
    pos = np.arange(seq)
    onehot = jnp.asarray(pos[:, None] // SEL_BLOCK == np.arange(SEL_LANES)[None, :], BF16)
    ext = np.zeros((seq, LANE), np.float32)
    ext[:, WCOL_HI] = (pos + WINDOW) // SEL_BLOCK
    ext[:, WCOL_LO] = (pos + WINDOW) % SEL_BLOCK
    ext[:, WCOL_QHI] = 1.0
    ext[:, WCOL_QLO] = 1.0
    wext = jnp.asarray(ext, BF16)

    nq = _nsa_sub_blocks(seq) * Q_BLOCK

    def alibi_in_block(rows):
        u = jnp.asarray((np.arange(rows) % SEL_BLOCK).astype(np.float32))[None, :, None, None]
        t = jnp.broadcast_to(slopes.reshape(g_, 1, NSA_REP, 1) * u, (g_, rows, NSA_REP, nq))
        return t.reshape(g_, rows, NSA_REP * nq)

    ubias = alibi_in_block(SEL_TILES_PER_STEP * kt)
    kq = np.arange(nq)[:, None] <= np.arange(nq)[None, :]
    causal = np.tile(np.where(kq, 0.0, NEG).astype(np.float32), (1, NSA_REP))
    dbias = alibi_in_block(nq) + jnp.asarray(causal)[None]
    ki = np.arange(WINDOW + nq)[:, None]
    qi = np.arange(nq)[None, :]
    band = np.where((ki > qi) & (ki <= qi + WINDOW), 0.0, NEG).astype(np.float32)
    wbias = jnp.asarray(np.tile(band, (1, NSA_REP)))
    return onehot, wext, dbias, ubias, wbias


def _log_sigmoid(x):
    return jnp.minimum(x, 0.0) - jnp.log(1.0 + jnp.exp(-jnp.abs(x)))


def _split3(x):
    hi = x.astype(BF16)
    r1 = x - hi.astype(F32)
    mid = r1.astype(BF16)
    lo = (r1 - mid.astype(F32)).astype(BF16)
    return hi, mid, lo


def _mlstm_kernel(bias_ref, q_ref, k_ref, v_ref, o_ref, ifc_ref, ng_ref,
                  tri_ref, y_ref, c_ref, n_ref, m_ref):
    ch = pl.program_id(1)

    @pl.when(ch == 0)
    def _():
        c_ref[...] = jnp.zeros(c_ref.shape, F32)
        n_ref[...] = jnp.zeros(n_ref.shape, F32)
        m_ref[...] = jnp.zeros(m_ref.shape, F32)

    for sq in range(q_ref.shape[0]):
        _mlstm_chunk(bias_ref, q_ref.at[sq], k_ref.at[sq], v_ref.at[sq], o_ref.at[sq], ifc_ref.at[sq],
                     ng_ref, tri_ref, y_ref.at[sq], c_ref.at[sq], n_ref.at[sq], m_ref.at[sq])


def _mlstm_chunk(bias_ref, q_ref, k_ref, v_ref, o_ref, ifc_ref, ng_ref, tri_ref, y_ref,
                 c_ref, n_ref, m_ref):
    nh, dh = MLSTM_HEADS, MLSTM_HEAD_DIM
    L = q_ref.shape[0]
    tri = tri_ref[...]
    lane8 = lax.broadcasted_iota(jnp.int32, (1, LANE), 1)
    bias_c = jnp.zeros((1, LANE), F32)
    for h in range(nh):
        bias_c = jnp.where(lane8 == h, bias_ref[h], bias_c)
        bias_c = jnp.where(lane8 == nh + h, bias_ref[nh + h], bias_c)
    pre_c = ifc_ref[...] + bias_c
    cum_c = sum(_dot(tri, part) for part in _split3(_log_sigmoid(pre_c)))
    pre_r = pre_c.T[:2 * nh]
    cum_r = sum(_dot_nt(part, tri) for part in _split3(_log_sigmoid(pre_r)))

    rr = lax.broadcasted_iota(jnp.int32, (L, 1), 0)
    cc = lax.broadcasted_iota(jnp.int32, (1, L), 1)
    causal = cc <= rr

    for h in range(nh):
        cols = slice(h * dh, (h + 1) * dh)
        qb = q_ref[:, cols]
        kb = k_ref[:, cols]
        vh = v_ref[:, cols]
        qh = qb.astype(F32)
        kh = kb.astype(F32)
        b_c = cum_c[:, nh + h:nh + h + 1]
        li_c = pre_c[:, h:h + 1]
        b_r = cum_r[nh + h:nh + h + 1, :]
        li_r = pre_r[h:h + 1, :]
        m_prev = m_ref[h:h + 1, 0:1]

        dmat = jnp.where(causal, b_c - b_r + li_r, NEG)
        a = b_c + m_prev
        m_j = jnp.maximum(a, jnp.max(dmat, axis=1, keepdims=True))
        w_intra = jnp.exp(dmat - m_j)
        w_inter = jnp.exp(a - m_j)
        sc = _dot_nt(qb, kb) * w_intra
        c_old = c_ref[h]
        n_old = n_ref[h:h + 1, :]
        num = w_inter * _dot(qb, c_old.astype(BF16)) + _dot(sc.astype(BF16), vh)
        den = (w_inter * jnp.sum(qh * n_old, axis=1, keepdims=True)
               + jnp.sum(sc, axis=1, keepdims=True))
        hid = num / jnp.maximum(jnp.abs(den), jnp.exp(-m_j))

        g_tot = b_r[:, L - 1:L]
        lw_c = g_tot - b_c + li_c
        lw_r = g_tot - b_r + li_r
        m_new = jnp.maximum(g_tot + m_prev, jnp.max(lw_r, axis=1, keepdims=True))
        decay = jnp.exp(g_tot + m_prev - m_new)
        kw = jnp.exp(lw_c - m_new) * kh
        c_ref[h] = decay * c_old + _dot(kw.T.astype(BF16), vh)
        n_ref[h:h + 1, :] = decay * n_old + jnp.sum(kw, axis=0, keepdims=True)
        m_ref[h:h + 1, :] = jnp.broadcast_to(m_new, (1, LANE))

        hn = hid * lax.rsqrt(jnp.mean(hid * hid, axis=-1, keepdims=True) + EPS) * ng_ref[:, cols]
        y_ref[:, cols] = (_sigmoid(o_ref[:, cols]) * hn).astype(y_ref.dtype)


def _mlstm(seg_a, seg_b, bias, norm_g, tri, batch, seq, chunk):
    nc = seq // chunk
    w = MLSTM_WIDTH
    nh, dh = MLSTM_HEADS, MLSTM_HEAD_DIM
    seqs = 1
    a3 = seg_a.reshape(batch, seq, SEG_A)
    b3 = seg_b.reshape(batch, seq, SEG_B)

    def col_spec(off):
        return pl.BlockSpec((seqs, chunk, w), lambda b, c, o=off // w: (b, c, o))

    y = pl.pallas_call(
        _mlstm_kernel,
        grid=(batch // seqs, nc),
        in_specs=[
            pl.BlockSpec(memory_space=pltpu.SMEM),
            col_spec(A_QK), col_spec(A_QK + w), col_spec(A_V), col_spec(B_O),
            pl.BlockSpec((seqs, chunk, LANE), lambda b, c: (b, c, B_IF // LANE)),
            pl.BlockSpec((1, w), lambda b, c: (0, 0)),
            pl.BlockSpec((chunk, chunk), lambda b, c: (0, 0)),
        ],
        out_specs=pl.BlockSpec((seqs, chunk, w), lambda b, c: (b, c, 0)),
        out_shape=jax.ShapeDtypeStruct((batch, seq, w), BF16),
        scratch_shapes=[
            pltpu.VMEM((seqs, nh, dh, dh), F32),
            pltpu.VMEM((seqs, 8, dh), F32),
            pltpu.VMEM((seqs, 8, LANE), F32),
        ],
        compiler_params=pltpu.CompilerParams(
            dimension_semantics=("parallel", "arbitrary"), vmem_limit_bytes=VMEM_LIMIT),
        name="mlstm",
    )(bias, a3, a3, a3, b3, b3, norm_g, tri)
    return y.reshape(batch * seq, w)


def _outproj_kernel(x_ref, ya_ref, ym_ref, wa_ref, wm_ref, o_ref):
    o_ref[...] = x_ref[...] + _dot(ya_ref[...], wa_ref[...]) + _dot(ym_ref[...], wm_ref[...])


def _outproj(x2, ya, ym, wa, wm, tm):
    n, d = x2.shape
    return pl.pallas_call(
        _outproj_kernel,
        grid=(n // tm,),
        in_specs=[
            pl.BlockSpec((tm, d), lambda i: (i, 0)),
            pl.BlockSpec((tm, ya.shape[1]), lambda i: (i, 0)),
            pl.BlockSpec((tm, ym.shape[1]), lambda i: (i, 0)),
            pl.BlockSpec(wa.shape, lambda i: (0, 0)),
            pl.BlockSpec(wm.shape, lambda i: (0, 0)),
        ],
        out_specs=pl.BlockSpec((tm, d), lambda i: (i, 0)),
        out_shape=jax.ShapeDtypeStruct((n, d), F32),
        compiler_params=pltpu.CompilerParams(
            dimension_semantics=("parallel",), vmem_limit_bytes=VMEM_LIMIT),
        name="outproj",
    )(x2, ya, ym, wa, wm)


def _mlp_kernel(x_ref, g_ref, w1_ref, w2_ref, gf_ref, o_ref, h_ref):
    f = pl.program_id(1)

    @pl.when(f == 0)
    def _():
        x = x_ref[...]
        r = lax.rsqrt(jnp.mean(x * x, axis=-1, keepdims=True) + EPS)
        h_ref[...] = (x * r * g_ref[...]).astype(BF16)
        o_ref[...] = x

    half = w1_ref.shape[1] // 2
    h = h_ref[...]
    ua = jnp.maximum(_dot(h, w1_ref[:, :half]), 0.0)
    ub = jnp.maximum(_dot(h, w1_ref[:, half:]), 0.0)
    o_ref[...] += (_dot((ua * ua).astype(BF16), w2_ref[:half, :])
                   + _dot((ub * ub).astype(BF16), w2_ref[half:, :]))

    @pl.when(f == pl.num_programs(1) - 1)
    def _():
        x2 = o_ref[...]
        r = lax.rsqrt(jnp.mean(x2 * x2, axis=-1, keepdims=True) + EPS)
        o_ref[...] = x2 * r * gf_ref[...]


def _mlp(x1, g, w1, w2, gf, tm, tf):
    n, d = x1.shape
    dff = w1.shape[1]
    return pl.pallas_call(
        _mlp_kernel,
        grid=(n // tm, dff // tf),
        in_specs=[
            pl.BlockSpec((tm, d), lambda i, f: (i, 0)),
            pl.BlockSpec((1, d), lambda i, f: (0, 0)),
            pl.BlockSpec((d, tf), lambda i, f: (0, f)),
            pl.BlockSpec((tf, d), lambda i, f: (f, 0)),
            pl.BlockSpec((1, d), lambda i, f: (0, 0)),
        ],
        out_specs=pl.BlockSpec((tm, d), lambda i, f: (i, 0)),
        out_shape=jax.ShapeDtypeStruct((n, d), F32),
        scratch_shapes=[pltpu.VMEM((tm, d), BF16)],
        compiler_params=pltpu.CompilerParams(
            dimension_semantics=("parallel", "arbitrary"), vmem_limit_bytes=VMEM_LIMIT),
        name="mlp",
    )(x1, g, w1, w2, gf)


def _row_tile(n, want):
    t = want
    while n % t:
        t //= 2
    return t


def _layer(x2, batch, seq, norm_mix_g, w_in, w_cmp_k1, w_cmp_k2, pos_cmp_k, w_cmp_v1, w_cmp_v2,
           pos_cmp_v, conv_w, conv_b, b_igate, b_fgate, mlstm_norm_g, w_out, norm_mlp_g,
           w_mlp_in, w_mlp_out):
    n, d = x2.shape
    assert seq % Q_BLOCK == 0 and seq >= WINDOW + Q_BLOCK and seq // SEL_BLOCK <= SEL_LANES
    g_ = NSA_KV_GROUPS
    nh = MLSTM_HEADS

    c_gate = NSA_WIDTH + 6 * NSA_KV_WIDTH
    c_qk = c_gate + NSA_HEADS * N_BRANCH
    c_v = c_qk + 2 * MLSTM_WIDTH
    c_o = c_v + MLSTM_WIDTH
    c_i = c_o + MLSTM_WIDTH
    c_f = c_i + nh
    w16 = w_in.astype(BF16)
    gate_cols = []
    per_g = NSA_REP * N_BRANCH
    for g in range(g_):
        gate_cols += [w16[:, c_gate + g * per_g:c_gate + (g + 1) * per_g],
                      jnp.zeros((d, LANE - per_g), BF16)]
    w_ab = jnp.concatenate(
        [w16[:, c_v:c_o], w16[:, c_qk:c_v], w16[:, :c_gate],
         w16[:, c_o:c_i]] + gate_cols
        + [w16[:, c_i:c_f + nh], jnp.zeros((d, SEG_B - B_IF - 2 * nh), BF16)],
        axis=1)
    scale = jnp.concatenate([jnp.ones((1, A_QK + MLSTM_WIDTH), F32),
                             jnp.full((1, MLSTM_WIDTH), MLSTM_HEAD_DIM ** -0.5, F32),
                             jnp.full((1, NSA_WIDTH), NSA_HEAD_DIM ** -0.5, F32),
                             jnp.ones((1, SEG_A - A_KV + SEG_B), F32)], axis=1)
    conv_pad = ((0, 0), (A_QK, SEG_A - A_Q + SEG_B))
    cw_ab = jnp.pad(conv_w, conv_pad)
    cb_ab = jnp.pad(conv_b.reshape(1, -1), conv_pad)
    g_mix = norm_mix_g.reshape(1, d)

    tm = _row_tile(seq, ROW_TILE)
    seg_a, seg_b = _inproj(x2, g_mix, w_ab, scale, cw_ab, cb_ab, tm, COL_TILE, (A_QK, A_Q), seq)

    n_sub = seq // CMP_STRIDE
    dh = NSA_HEAD_DIM
    w1s = jnp.stack([w_cmp_k1, w_cmp_v1]).reshape(2, 2, CMP_STRIDE, dh, dh)
    w1r = jnp.concatenate([w1s[:, 0], w1s[:, 1]], axis=-1).astype(BF16)
    w2s = jnp.stack([w_cmp_k2, w_cmp_v2]).astype(BF16)
    poss = jnp.stack([pos_cmp_k, pos_cmp_v]).reshape(2, 2, CMP_STRIDE, dh).transpose(0, 2, 1, 3)
    posr = jnp.pad(poss, ((0, 0), (0, 0), (0, 6), (0, 0))).astype(BF16)
    kvc, kvct = _compress(seg_a, w1r, w2s, posr, batch, seq)

    cmp_start = np.arange(n_sub) * CMP_STRIDE
    sel_start = np.arange(SEL_LANES) * SEL_BLOCK
    ovt = ((cmp_start[None, :] < sel_start[:, None] + SEL_BLOCK)
           & (cmp_start[None, :] + CMP_BLOCK - 1 >= sel_start[:, None])
           & (np.arange(n_sub)[None, :] < n_sub - CMP_BLOCK // CMP_STRIDE + 1))
    ovt = jnp.asarray(ovt, BF16)
    slopes = jnp.exp2(-8.0 * jnp.arange(1, NSA_HEADS + 1, dtype=F32) / NSA_HEADS)
    ocmp, pen, tiles = _nsa_cmp(seg_a, kvc, kvct, ovt, slopes, batch, seq)
    onehot, wext, dbias, ubias, wbias = _nsa_tables(slopes, seq)
    y_a = _nsa_attn(seg_a, seg_b, pen, ocmp, tiles, onehot, wext, dbias, ubias, wbias,
                    slopes, batch, seq)

    chunk = 256 if seq % 256 == 0 else 128
    bias = jnp.concatenate([b_igate, b_fgate]).astype(F32)
    tri = jnp.asarray(np.tril(np.ones((chunk, chunk), np.float32)), BF16)
    y_m = _mlstm(seg_a, seg_b, bias, mlstm_norm_g.reshape(1, -1), tri, batch, seq, chunk)

    w_o = w_out.astype(BF16)
    x1 = _outproj(x2, y_a, y_m, w_o[:NSA_WIDTH], w_o[NSA_WIDTH:], _row_tile(n, 512))
    return x1, (norm_mlp_g.reshape(1, d), w_mlp_in.astype(BF16), w_mlp_out.astype(BF16))


def kernel(x, norm_mix_g, w_in, w_cmp_k1, w_cmp_k2, pos_cmp_k, w_cmp_v1, w_cmp_v2, pos_cmp_v, conv_w, conv_b, b_igate, b_fgate, mlstm_norm_g, w_out, norm_mlp_g, w_mlp_in, w_mlp_out, norm_f_g):
    batch, seq, d = x.shape
    depth = w_in.shape[0]
    assert depth == 1, "the final RMSNorm is fused into the last layer's channel mixer"
    x2 = x.reshape(batch * seq, d)
    tm = _row_tile(batch * seq, ROW_TILE)
    for l in range(depth):
        x1, (g_mlp, w1, w2) = _layer(
            x2, batch, seq, norm_mix_g[l], w_in[l], w_cmp_k1[l], w_cmp_k2[l], pos_cmp_k[l],
            w_cmp_v1[l], w_cmp_v2[l], pos_cmp_v[l], conv_w[l], conv_b[l], b_igate[l], b_fgate[l],
            mlstm_norm_g[l], w_out[l], norm_mlp_g[l], w_mlp_in[l], w_mlp_out[l])
        x2 = _mlp(x1, g_mlp, w1, w2, norm_f_g.reshape(1, d), tm, 512)
    return x2.reshape(batch, seq, d)
```

```python
import functools

import numpy as np
import jax
import jax.numpy as jnp
from jax import lax
from jax.experimental import pallas as pl
from jax.experimental.pallas import tpu as pltpu

F32 = jnp.float32
BF16 = jnp.bfloat16

EPS = 1e-6
NEG = -1e30
FORCE_BONUS = 1e4
PICKED = -3e38
MASK_BIG = 1e30

D_MODEL = 2048
NSA_HEAD_DIM = 128
NSA_WIDTH = D_MODEL // 2
NSA_HEADS = NSA_WIDTH // NSA_HEAD_DIM
NSA_REP = 4
NSA_KV_GROUPS = NSA_HEADS // NSA_REP
NSA_KV_WIDTH = NSA_KV_GROUPS * NSA_HEAD_DIM
CMP_BLOCK = 32
CMP_STRIDE = 16
SEL_BLOCK = 64
SEL_TOPK = 16
WINDOW = 512
Q_BLOCK = 128
N_BRANCH = 3
MLSTM_HEAD_DIM = 256
MLSTM_WIDTH = D_MODEL - NSA_WIDTH
MLSTM_HEADS = MLSTM_WIDTH // MLSTM_HEAD_DIM
CONV_WIDTH = 4

LANE = 128
SEL_LANES = 128
KEY_TILE = 128
SEL_TILES_PER_STEP = 2
WIN_TILES = WINDOW // KEY_TILE
SUM_ROWS = 16
VMEM_LIMIT = 56 * 1024 * 1024
ROW_TILE = 1024

A_V, A_QK = 0, MLSTM_WIDTH
A_Q = A_QK + 2 * MLSTM_WIDTH
A_KV = A_Q + NSA_WIDTH
SEG_A = A_KV + 6 * NSA_KV_WIDTH
B_O = 0
SEG_B = MLSTM_WIDTH
C_GATE_STRIDE = 16
C_IF = NSA_KV_GROUPS * C_GATE_STRIDE
SEG_C = LANE
COL_TILE = 512

WCOL_PAD, WCOL_HI, WCOL_LO, WCOL_QHI, WCOL_QLO = 0, 1, 2, 3, 4


def _dot(a, b):
    return jnp.dot(a, b, preferred_element_type=F32)


def _dot_nt(a, b):
    return lax.dot_general(a, b, (((1,), (1,)), ((), ())), preferred_element_type=F32)


def _sigmoid(x):
    return 1.0 / (1.0 + jnp.exp(-x))


def _shifted(x, tail, s):
    xs = pltpu.roll(x, s, axis=0)
    ts = pltpu.roll(tail, s, axis=0)
    row8 = lax.broadcasted_iota(jnp.int32, (8, 1), 0)
    head = jnp.where(row8 < s, ts, xs[:8])
    return jnp.concatenate([head, xs[8:]], axis=0)


def _conv_silu(x, tail, w, b):
    y = b + _shifted(x, tail, CONV_WIDTH - 1) * w[0:1]
    for i in range(1, CONV_WIDTH - 1):
        y = y + _shifted(x, tail, CONV_WIDTH - 1 - i) * w[i:i + 1]
    y = y + x * w[CONV_WIDTH - 1:CONV_WIDTH]
    return y * _sigmoid(y)


def _inproj_kernel(x_ref, g_ref, w_ref, wc_ref, cs_ref, cw_ref, cb_ref, oa_ref, ob_ref, oc_ref,
                   h_ref, halo_ref, *, nb, n_plain, tiles_per_seq):
    i = pl.program_id(0)
    j = pl.program_id(1)

    @pl.when(j == 0)
    def _():
        x = x_ref[...]
        r = lax.rsqrt(jnp.mean(x * x, axis=-1, keepdims=True) + EPS)
        h_ref[...] = (x * r * g_ref[...]).astype(BF16)

    is_conv = j > nb + n_plain

    @pl.when((j > nb) & jnp.logical_not(is_conv))
    def _():
        oa_ref[...] = (_dot(h_ref[...], w_ref[...]) * cs_ref[...]).astype(oa_ref.dtype)

    @pl.when(j < nb)
    def _():
        ob_ref[...] = (_dot(h_ref[...], w_ref[...]) * cs_ref[...]).astype(ob_ref.dtype)

    @pl.when(j == nb)
    def _():
        oc_ref[...] = _dot(h_ref[...], wc_ref[...])

    @pl.when(is_conv)
    def _():
        slot = j - (nb + 1 + n_plain)

        @pl.when(i % tiles_per_seq == 0)
        def _():
            halo_ref[slot] = jnp.zeros(halo_ref.shape[1:], F32)

        acc = _dot(h_ref[...], w_ref[...])
        tail = halo_ref[slot]
        halo_ref[slot] = acc[acc.shape[0] - 8:]
        y = _conv_silu(acc, tail, cw_ref[...], cb_ref[...])
        oa_ref[...] = (y * cs_ref[...]).astype(oa_ref.dtype)


def _inproj(x2, g, w, w_c, cs, cw, cb, tm, tn, conv_cols, seq):
    n, d = x2.shape
    na, nb = SEG_A // tn, SEG_B // tn
    conv_lo, conv_hi = conv_cols[0] // tn, conv_cols[1] // tn
    assert conv_cols[0] % tn == 0 and conv_cols[1] % tn == 0 and seq % tm == 0 and conv_hi <= na
    assert SEG_A % tn == 0 and SEG_B % tn == 0 and w.shape[1] == SEG_A + SEG_B
    n_conv = conv_hi - conv_lo
    n_plain = na - n_conv

    def a_tile(j):
        p = jnp.maximum(j - nb - 1, 0)
        plain = jnp.where(p < conv_lo, p, p + n_conv)
        return jnp.where(p < n_plain, plain, conv_lo + p - n_plain)

    def w_tile(i, j):
        return (0, jnp.where(j <= nb, na + jnp.minimum(j, nb - 1), a_tile(j)))

    return pl.pallas_call(
        functools.partial(_inproj_kernel, nb=nb, n_plain=n_plain, tiles_per_seq=seq // tm),
        grid=(n // tm, na + nb + 1),
        in_specs=[
            pl.BlockSpec((tm, d), lambda i, j: (i, 0)),
            pl.BlockSpec((1, d), lambda i, j: (0, 0)),
            pl.BlockSpec((d, tn), w_tile),
            pl.BlockSpec((d, SEG_C), lambda i, j: (0, 0)),
            pl.BlockSpec((1, tn), w_tile),
            pl.BlockSpec((CONV_WIDTH, tn), w_tile),
            pl.BlockSpec((1, tn), w_tile),
        ],
        out_specs=[pl.BlockSpec((tm, tn), lambda i, j: (i, a_tile(j))),
                   pl.BlockSpec((tm, tn), lambda i, j: (i, jnp.minimum(j, nb - 1))),
                   pl.BlockSpec((tm, SEG_C), lambda i, j: (i, 0))],
        out_shape=[jax.ShapeDtypeStruct((n, SEG_A), BF16), jax.ShapeDtypeStruct((n, SEG_B), F32),
                   jax.ShapeDtypeStruct((n, SEG_C), F32)],
        scratch_shapes=[pltpu.VMEM((tm, d), BF16),
                        pltpu.VMEM((conv_hi - conv_lo, 8, tn), F32)],
        compiler_params=pltpu.CompilerParams(
            dimension_semantics=("arbitrary", "arbitrary"), vmem_limit_bytes=VMEM_LIMIT),
        name="inproj",
    )(x2, g, w, w_c, cs, cw, cb)


def _compress_kernel(x_ref, w1_ref, w2_ref, pos_ref, o_ref, ot_ref, xf_ref):
    n_sub = o_ref.shape[2]
    dh = NSA_HEAD_DIM
    xf_ref[...] = x_ref[...].astype(F32)
    acc = jnp.zeros((n_sub, 2 * dh), F32)
    posw = jnp.zeros((1, dh), F32)
    for p in range(CMP_STRIDE):
        wp = w1_ref[0, p]
        acc = acc + _dot(xf_ref[pl.ds(p, n_sub, stride=CMP_STRIDE), :].astype(BF16), wp)
        pw = _dot(pos_ref[0, p], wp)
        posw = posw + pw[0:1, :dh] + pw[1:2, dh:]
    bot = pltpu.roll(acc[:, dh:], n_sub - 1, axis=0)
    pre = acc[:, :dh] + bot + posw
    hid = pre * _sigmoid(pre)
    out = _dot(hid.astype(BF16), w2_ref[0])
    o_ref[0, 0] = out.astype(o_ref.dtype)
    ot_ref[0, 0] = out.T.astype(ot_ref.dtype)


def _compress(seg_a, w1r, w2s, posr, batch, seq):
    g = NSA_KV_GROUPS
    c = 2 * g
    dh = NSA_HEAD_DIM
    n_sub = seq // CMP_STRIDE
    return pl.pallas_call(
        _compress_kernel,
        grid=(batch, c),
        in_specs=[
            pl.BlockSpec((seq, dh), lambda i, j: (i, A_KV // dh + j)),
            pl.BlockSpec((1, CMP_STRIDE, dh, 2 * dh), lambda i, j: (j // g, 0, 0, 0)),
            pl.BlockSpec((1, dh, dh), lambda i, j: (j // g, 0, 0)),
            pl.BlockSpec((1, CMP_STRIDE, 8, dh), lambda i, j: (j // g, 0, 0, 0)),
        ],
        out_specs=[pl.BlockSpec((1, 1, n_sub, dh), lambda i, j: (i, j, 0, 0)),
                   pl.BlockSpec((1, 1, dh, n_sub), lambda i, j: (i, j, 0, 0))],
        out_shape=[jax.ShapeDtypeStruct((batch, c, n_sub, dh), BF16),
                   jax.ShapeDtypeStruct((batch, c, dh, n_sub), BF16)],
        scratch_shapes=[pltpu.VMEM((seq, dh), F32)],
        compiler_params=pltpu.CompilerParams(
            dimension_semantics=("parallel", "parallel"), vmem_limit_bytes=VMEM_LIMIT),
        name="compress",
    )(seg_a, w1r, w2s, posr)


def _stack_heads(q_all):
    dh = NSA_HEAD_DIM
    return jnp.concatenate([q_all[:, r * dh:(r + 1) * dh] for r in range(NSA_REP)], axis=0)


def _tile_list(hit_col, tile0, pad_tile, mats_ref):
    n = SEL_LANES
    lane = lax.broadcasted_iota(jnp.int32, (1, n), 1)
    blk = lax.broadcasted_iota(jnp.int32, (n, n), 0)
    slot = lax.broadcasted_iota(jnp.int32, (n, n), 1).astype(F32)
    hit = jnp.broadcast_to(hit_col, (n, n))
    both = jnp.maximum(hit, pltpu.roll(hit, n - 1, axis=0))
    act = jnp.where((both > 0.0) & (blk % 2 == 0) & (blk // 2 < tile0), 1.0, 0.0)
    rank = _dot(mats_ref[0], act.astype(BF16))
    cnt = jnp.sum(act, axis=0, keepdims=True)
    place = jnp.where((act > 0.0) & (rank == slot), 1.0, 0.0)
    tiles = _dot(mats_ref[1, 0:8], place.astype(BF16))[0:1]
    out = jnp.where(lane.astype(F32) < cnt, tiles, float(pad_tile))
    return jnp.where(lane == n - 1, cnt, out).astype(jnp.int32)


def _nsa_cmp_kernel(slopes_ref, q_ref, kc_ref, vct_ref, ovt_ref, mats_ref, ocmp_ref, pen_ref, tiles_ref, *,
                    seq, sub_blocks):
    step = pl.program_id(2)
    nq_step = sub_blocks * Q_BLOCK
    chunk = min(SEL_LANES, kc_ref.shape[2])
    n_chunks = kc_ref.shape[2] // chunk
    need = ((step + 1) * nq_step - CMP_BLOCK) // CMP_STRIDE + 1
    n_need = (need + chunk - 1) // chunk
    for v in range(1, n_chunks + 1):
        cond = (n_need == v) if v < n_chunks else (n_need >= v)
        if v == 1:
            cond = n_need <= 1

        @pl.when(cond)
        def _(v=v):
            hit = None
            for sub in range(sub_blocks):
                rows = slice(sub * Q_BLOCK, (sub + 1) * Q_BLOCK)
                h = _nsa_cmp_block(slopes_ref, q_ref[rows, :], kc_ref, vct_ref, ovt_ref,
                                   ocmp_ref.at[0, 0, sub], pen_ref.at[0, 0, rows],
                                   step * sub_blocks + sub, seq, v * chunk,
                                   past_first_block=(sub >= 1 or v >= 2))
                hit = h if hit is None else jnp.maximum(hit, h)
            tiles_ref[0, 0, 0] = _tile_list(hit, step * sub_blocks, seq // KEY_TILE, mats_ref)


def _nsa_cmp_block(slopes_ref, q_all, kc_ref, vct_ref, ovt_ref, ocmp_ref, pen_ref, qb, seq, n_use,
                   past_first_block):
    g = pl.program_id(1)
    nq = Q_BLOCK
    n_cpad = n_use
    n_cmp = seq // CMP_STRIDE - CMP_BLOCK // CMP_STRIDE + 1
    t0 = qb * nq

    s_t = _dot_nt(kc_ref[0, 0, :n_use, :], _stack_heads(q_all))
    n_s = lax.broadcasted_iota(jnp.int32, (n_cpad, 1), 0)
    q_l = lax.broadcasted_iota(jnp.int32, (1, nq), 1)
    dist = (t0 - (CMP_BLOCK - 1)) + q_l - n_s * CMP_STRIDE
    valid = (dist >= 0) & (n_s < n_cmp)
    dist_f = dist.astype(F32)
    probs = []
    p_sum = jnp.zeros((n_cpad, nq), F32)
    for r in range(NSA_REP):
        slope = slopes_ref[g * NSA_REP + r]
        s = jnp.where(valid, s_t[:, r * nq:(r + 1) * nq] - slope * dist_f, NEG)
        m = jnp.max(s, axis=0, keepdims=True)
        e = jnp.exp(s - m)
        inv = jnp.where(m > 0.5 * NEG, 1.0 / jnp.sum(e, axis=0, keepdims=True), 0.0)
        p = e * inv
        probs.append(p.astype(BF16))
        p_sum = p_sum + p
    ocmp_ref[...] = _dot(vct_ref[0, 0, :, :n_use], jnp.concatenate(probs, axis=1))

    p_hi = p_sum.astype(BF16)
    p_lo = (p_sum - p_hi.astype(F32)).astype(BF16)
    ovt = ovt_ref[:, :n_use]
    imp = _dot(ovt, p_hi) + _dot(ovt, p_lo)
    j_i = lax.broadcasted_iota(jnp.int32, (SEL_LANES, 1), 0)
    t_l = t0 + q_l
    cur = t_l // SEL_BLOCK
    forced = (j_i == 0) | (j_i == cur) | (j_i == cur - 1)
    causal_blk = j_i * SEL_BLOCK <= t_l
    val = jnp.where(causal_blk, jnp.where(forced, imp + FORCE_BONUS, imp), NEG)
    j_f = j_i.astype(F32)
    sel_t = jnp.zeros((SEL_LANES, nq), F32)
    rounds = min(SEL_TOPK, seq // SEL_BLOCK)
    if past_first_block and rounds >= 3:
        sel_t = jnp.where(forced, 1.0, sel_t)
        val = jnp.where(forced, PICKED, val)
        rounds -= 3
    for _ in range(rounds):
        mx = jnp.max(val, axis=0, keepdims=True)
        first = jnp.min(jnp.where(val == mx, j_f, float(SEL_LANES)), axis=0, keepdims=True)
        pick = j_f == first
        sel_t = jnp.where(pick, 1.0, sel_t)
        val = jnp.where(pick, PICKED, val)
    sel = sel_t.T
    pen_ref[...] = ((sel - 1.0) * MASK_BIG).astype(pen_ref.dtype)
    return jnp.max(sel_t, axis=1, keepdims=True)


def _nsa_cmp(seg_a, kvc, kvct, ovt, slopes, batch, seq):
    nqb = seq // Q_BLOCK
    gq = NSA_REP * NSA_HEAD_DIM
    g_ = NSA_KV_GROUPS
    n_cpad = kvc.shape[2]
    sub = _nsa_sub_blocks(seq)
    nstep = nqb // sub
    idx = np.arange(SEL_LANES)
    mats = jnp.asarray(np.stack([idx[:, None] > idx[None, :],
                                 np.broadcast_to(idx[None, :] // 2, (SEL_LANES, SEL_LANES))]), BF16)
    return pl.pallas_call(
        functools.partial(_nsa_cmp_kernel, seq=seq, sub_blocks=sub),
        grid=(batch, g_, nstep),
        in_specs=[
            pl.BlockSpec(memory_space=pltpu.SMEM),
            pl.BlockSpec((sub * Q_BLOCK, gq), lambda b, g, q: (b * nstep + q, A_Q // gq + g)),
            pl.BlockSpec((1, 1, n_cpad, NSA_HEAD_DIM), lambda b, g, q: (b, g, 0, 0)),
            pl.BlockSpec((1, 1, NSA_HEAD_DIM, n_cpad), lambda b, g, q: (b, g_ + g, 0, 0)),
            pl.BlockSpec((SEL_LANES, n_cpad), lambda b, g, q: (0, 0)),
            pl.BlockSpec((2, SEL_LANES, SEL_LANES), lambda b, g, q: (0, 0, 0)),
        ],
        out_specs=[
            pl.BlockSpec((1, 1, sub, NSA_HEAD_DIM, gq), lambda b, g, q: (b, g, q, 0, 0)),
            pl.BlockSpec((1, 1, sub * Q_BLOCK, SEL_LANES), lambda b, g, q: (b, g, q, 0)),
            pl.BlockSpec((1, 1, 1, 1, SEL_LANES), lambda b, g, q: (b, g, q, 0, 0)),
        ],
        out_shape=[
            jax.ShapeDtypeStruct((batch, g_, nqb, NSA_HEAD_DIM, gq), F32),
            jax.ShapeDtypeStruct((batch, g_, seq, SEL_LANES), BF16),
            jax.ShapeDtypeStruct((batch, g_, nstep, 1, SEL_LANES), jnp.int32),
        ],
        compiler_params=pltpu.CompilerParams(
            dimension_semantics=("parallel", "parallel", "parallel"), vmem_limit_bytes=VMEM_LIMIT),
        name="nsa_cmp",
    )(slopes, seg_a, kvc, kvct, ovt, mats)


def _nsa_attn_kernel(slopes_ref, tiles_ref, q_ref, gate_ref, pen_ref, ocmp_ref, ks_ref, vs_ref,
                     kw_ref, vw_ref, onehot_ref, wext_ref, dbias_ref, ubias_ref, wbias_ref, o_ref,
                     ksel_ref, vselt_ref, kwin_ref, vwint_ref,
                     qa_ref, qw_ref, m_ref, acc_ref, owin_ref, gt_ref, sa_ref, sb_ref,
                     *, seq, sub_blocks):
    g = pl.program_id(1)
    step = pl.program_id(2)
    dh = NSA_HEAD_DIM
    nq = sub_blocks * Q_BLOCK
    kt = KEY_TILE
    per = SEL_TILES_PER_STEP
    pad_tile = seq // kt
    t0 = step * nq
    tile0 = step * sub_blocks

    @pl.when(step == 0)
    def _():
        ksel_ref[0:seq, :dh] = ks_ref[...]
        ksel_ref[0:seq, dh:] = onehot_ref[...]
        ksel_ref[seq:, :dh] = jnp.zeros((kt, dh), BF16)
        ksel_ref[seq:, dh:] = jnp.ones((kt, SEL_LANES), BF16)
        lane2 = lax.broadcasted_iota(jnp.int32, (WINDOW, dh + LANE), 1)
        kwin_ref[0:WINDOW, :] = jnp.where(lane2 == dh + WCOL_PAD, 1.0, 0.0).astype(BF16)
        kwin_ref[WINDOW:, :dh] = kw_ref[...]
        kwin_ref[WINDOW:, dh:] = wext_ref[...]
        zero_tile = jnp.zeros((dh + SUM_ROWS, kt), BF16)
        ones_rows = jnp.ones((SUM_ROWS, kt), BF16)
        vselt_ref[pad_tile] = zero_tile
        for i in range(WIN_TILES):
            vwint_ref[i] = zero_tile

        def transpose_tile(t, carry):
            r0 = pl.multiple_of(t * kt, kt)
            vselt_ref[t, :dh] = vs_ref[pl.ds(r0, kt), :].astype(F32).T.astype(BF16)
            vselt_ref[t, dh:] = ones_rows
            vwint_ref[t + WIN_TILES, :dh] = vw_ref[pl.ds(r0, kt), :].astype(F32).T.astype(BF16)
            vwint_ref[t + WIN_TILES, dh:] = ones_rows
            return carry

        lax.fori_loop(0, seq // kt, transpose_tile, 0)

    slopes = [slopes_ref[g * NSA_REP + r] for r in range(NSA_REP)]
    row_q = lax.broadcasted_iota(jnp.int32, (nq, 1), 0)
    lane = lax.broadcasted_iota(jnp.int32, (1, LANE), 1)
    q_all = q_ref[...]

    pen = pen_ref[0, 0].astype(F32)
    blk_rel = ((lane - (t0 + row_q) // SEL_BLOCK) * SEL_BLOCK).astype(F32)
    for r in range(NSA_REP):
        qa_ref[r * nq:(r + 1) * nq, :dh] = q_all[:, r * dh:(r + 1) * dh]
        qa_ref[r * nq:(r + 1) * nq, dh:] = (pen + slopes[r] * blk_rel).astype(BF16)

    def listed(idx):
        return tiles_ref[0, 0, 0, 0, idx]

    cnt = listed(SEL_LANES - 1)

    def group_scores(it):
        keys = jnp.concatenate(
            [ksel_ref[pl.ds(pl.multiple_of(listed(it * per + i) * kt, kt), kt), :]
             for i in range(per)], axis=0)
        return _dot_nt(keys, qa_ref[...]) + ubias_ref[0]

    s = _dot_nt(ksel_ref[pl.ds(pl.multiple_of(t0, kt), nq), :], qa_ref[...]) + dbias_ref[0]
    m0 = jnp.max(s, axis=0, keepdims=True)
    p = jnp.exp(s - m0)
    m_ref[...] = m0
    vals = jnp.concatenate([vselt_ref[tile0 + a] for a in range(sub_blocks)], axis=1)
    acc_ref[...] = _dot(vals, p.astype(BF16))
    sa_ref[...] = group_scores(0)

    tp = t0 + WINDOW + row_q
    t_hi = (tp // SEL_BLOCK).astype(F32)
    t_lo = (tp % SEL_BLOCK).astype(F32)
    for r in range(NSA_REP):
        sl = slopes[r]
        ext = jnp.where(lane == WCOL_PAD, -MASK_BIG, 0.0)
        ext = jnp.where(lane == WCOL_HI, sl * SEL_BLOCK, ext)
        ext = jnp.where(lane == WCOL_LO, sl, ext)
        ext = jnp.where(lane == WCOL_QHI, -sl * SEL_BLOCK * t_hi, ext)
        ext = jnp.where(lane == WCOL_QLO, -sl * t_lo, ext)
        qw_ref[r * nq:(r + 1) * nq, :dh] = q_all[:, r * dh:(r + 1) * dh]
        qw_ref[r * nq:(r + 1) * nq, dh:] = ext.astype(BF16)
    wlen = WINDOW + nq
    s = _dot_nt(kwin_ref[pl.ds(pl.multiple_of(t0, kt), wlen), :], qw_ref[...]) + wbias_ref[...]
    e = jnp.exp(s - jnp.max(s, axis=0, keepdims=True))
    vwin = jnp.concatenate([vwint_ref[tile0 + i] for i in range(WIN_TILES + sub_blocks)], axis=1)
    ow = _dot(vwin, e.astype(BF16))
    owin_ref[...] = ow[:dh] / ow[dh:dh + 1]

    def absorb(s, grp):
        vals = jnp.concatenate([vselt_ref[listed(grp * per + i)] for i in range(per)], axis=1)
        m_old = m_ref[...]
        m_new = jnp.maximum(m_old, jnp.max(s, axis=0, keepdims=True))
        alpha = jnp.exp(m_old - m_new)
        p = jnp.exp(s - m_new)
        acc_ref[...] = alpha * acc_ref[...] + _dot(vals, p.astype(BF16))
        m_ref[...] = m_new

    def sel_body(it, carry):
        s = sa_ref[...]
        sb_ref[...] = group_scores(2 * it + 1)
        absorb(s, 2 * it)
        s = sb_ref[...]
        sa_ref[...] = group_scores(2 * it + 2)
        absorb(s, 2 * it + 1)
        return carry

    lax.fori_loop(0, (cnt + 2 * per - 1) // (2 * per), sel_body, 0)
    o_sel = acc_ref[:dh, :] / acc_ref[dh:dh + 1, :]
    o_win = owin_ref[...]

    gt_ref[...] = _sigmoid(gate_ref[...]).T
    gate_t = gt_ref[pl.ds(pl.multiple_of(g * C_GATE_STRIDE, C_GATE_STRIDE), C_GATE_STRIDE), :]
    for a in range(sub_blocks):
        o_cmp = ocmp_ref[0, 0, a]
        qs = slice(a * Q_BLOCK, (a + 1) * Q_BLOCK)
        for r in range(NSA_REP):
            cols = slice(r * nq + a * Q_BLOCK, r * nq + (a + 1) * Q_BLOCK)
            c0 = N_BRANCH * r
            out_t = (gate_t[c0:c0 + 1, qs] * o_cmp[:, r * Q_BLOCK:(r + 1) * Q_BLOCK]
                     + gate_t[c0 + 1:c0 + 2, qs] * o_sel[:, cols]
                     + gate_t[c0 + 2:c0 + 3, qs] * o_win[:, cols])
            o_ref[qs, r * dh:(r + 1) * dh] = out_t.T.astype(o_ref.dtype)


def _nsa_sub_blocks(seq):
    return 2 if (seq // Q_BLOCK) % 2 == 0 else 1


def _nsa_attn(seg_a, seg_c, pen, ocmp, tiles, onehot, wext, dbias, ubias, wbias, slopes, batch, seq):
    n = batch * seq
    sub = _nsa_sub_blocks(seq)
    nq = sub * Q_BLOCK
    nqb = seq // nq
    gq = NSA_REP * NSA_HEAD_DIM
    g_ = NSA_KV_GROUPS
    dh, kt = NSA_HEAD_DIM, KEY_TILE
    dk = dh + SEL_LANES
    rq = NSA_REP * nq
    n_t = seq // kt

    def kv_spec(kind):
        return pl.BlockSpec((seq, dh), lambda b, g, q, k=kind: (b, A_KV // dh + k * g_ + g))

    def const_spec(arr):
        return pl.BlockSpec(arr.shape, lambda b, g, q, nd=arr.ndim: (0,) * nd)

    return pl.pallas_call(
        functools.partial(_nsa_attn_kernel, seq=seq, sub_blocks=sub),
        grid=(batch, g_, nqb),
        in_specs=[
            pl.BlockSpec(memory_space=pltpu.SMEM),
            pl.BlockSpec((1, 1, 1, 1, SEL_LANES), lambda b, g, q: (b, g, q, 0, 0),
                         memory_space=pltpu.SMEM),
            pl.BlockSpec((nq, gq), lambda b, g, q: (b * nqb + q, A_Q // gq + g)),
            pl.BlockSpec((nq, SEG_C), lambda b, g, q: (b * nqb + q, 0)),
            pl.BlockSpec((1, 1, nq, SEL_LANES), lambda b, g, q: (b, g, q, 0)),
            pl.BlockSpec((1, 1, sub, NSA_HEAD_DIM, gq), lambda b, g, q: (b, g, q, 0, 0)),
            kv_spec(2), kv_spec(3), kv_spec(4), kv_spec(5),
            const_spec(onehot), const_spec(wext),
            pl.BlockSpec((1,) + dbias.shape[1:], lambda b, g, q: (g, 0, 0)),
            pl.BlockSpec((1,) + ubias.shape[1:], lambda b, g, q: (g, 0, 0)),
            const_spec(wbias),
        ],
        out_specs=pl.BlockSpec((nq, gq), lambda b, g, q: (b * nqb + q, g)),
        out_shape=jax.ShapeDtypeStruct((n, NSA_WIDTH), BF16),
        scratch_shapes=[
            pltpu.VMEM((seq + kt, dk), BF16),
            pltpu.VMEM((n_t + 1, dh + SUM_ROWS, kt), BF16),
            pltpu.VMEM((seq + WINDOW, dh + LANE), BF16),
            pltpu.VMEM((n_t + WIN_TILES, dh + SUM_ROWS, kt), BF16),
            pltpu.VMEM((rq, dk), BF16),
            pltpu.VMEM((rq, dk), BF16),
            pltpu.VMEM((1, rq), F32),
            pltpu.VMEM((dh + SUM_ROWS, rq), F32),
            pltpu.VMEM((dh, rq), F32),
            pltpu.VMEM((SEG_C, nq), F32),
            pltpu.VMEM((SEL_TILES_PER_STEP * KEY_TILE, rq), F32),
            pltpu.VMEM((SEL_TILES_PER_STEP * KEY_TILE, rq), F32),
        ],
        compiler_params=pltpu.CompilerParams(
            dimension_semantics=("parallel", "parallel", "arbitrary"), vmem_limit_bytes=VMEM_LIMIT),
        name="nsa_attn",
    )(slopes, tiles, seg_a, seg_c, pen, ocmp, seg_a, seg_a, seg_a, seg_a, onehot, wext,
      dbias, ubias, wbias)


def _nsa_tables(slopes, seq):
    g_, kt = NSA_KV_GROUPS, KEY_TILE
    pos = np.arange(seq)
    onehot = jnp.asarray(pos[:, None] // SEL_BLOCK == np.arange(SEL_LANES)[None, :], BF16)
    ext = np.zeros((seq, LANE), np.float32)
    ext[:, WCOL_HI] = (pos + WINDOW) // SEL_BLOCK
    ext[:, WCOL_LO] = (pos + WINDOW) % SEL_BLOCK
    ext[:, WCOL_QHI] = 1.0
    ext[:, WCOL_QLO] = 1.0
    wext = jnp.asarray(ext, BF16)

    nq = _nsa_sub_blocks(seq) * Q_BLOCK

    def alibi_in_block(rows):
        u = jnp.asarray((np.arange(rows) % SEL_BLOCK).astype(np.float32))[None, :, None, None]
        t = jnp.broadcast_to(slopes.reshape(g_, 1, NSA_REP, 1) * u, (g_, rows, NSA_REP, nq))
        return t.reshape(g_, rows, NSA_REP * nq)

    ubias = alibi_in_block(SEL_TILES_PER_STEP * kt)
    kq = np.arange(nq)[:, None] <= np.arange(nq)[None, :]
    causal = np.tile(np.where(kq, 0.0, NEG).astype(np.float32), (1, NSA_REP))
    dbias = alibi_in_block(nq) + jnp.asarray(causal)[None]
    ki = np.arange(WINDOW + nq)[:, None]
    qi = np.arange(nq)[None, :]
    band = np.where((ki > qi) & (ki <= qi + WINDOW), 0.0, NEG).astype(np.float32)
    wbias = jnp.asarray(np.tile(band, (1, NSA_REP)))
    return onehot, wext, dbias, ubias, wbias


def _log_sigmoid(x):
    return jnp.minimum(x, 0.0) - jnp.log(1.0 + jnp.exp(-jnp.abs(x)))


def _split3(x):
    hi = x.astype(BF16)
    r1 = x - hi.astype(F32)
    mid = r1.astype(BF16)
    lo = (r1 - mid.astype(F32)).astype(BF16)
    return hi, mid, lo


def _mlstm_kernel(bias_ref, q_ref, k_ref, v_ref, o_ref, ifc_ref, ng_ref,
                  tri_ref, y_ref, c_ref, n_ref, m_ref):
    ch = pl.program_id(1)

    @pl.when(ch == 0)
    def _():
        c_ref[...] = jnp.zeros(c_ref.shape, F32)
        n_ref[...] = jnp.zeros(n_ref.shape, F32)
        m_ref[...] = jnp.zeros(m_ref.shape, F32)

    for sq in range(q_ref.shape[0]):
        _mlstm_chunk(bias_ref, q_ref.at[sq], k_ref.at[sq], v_ref.at[sq], o_ref.at[sq], ifc_ref.at[sq],
                     ng_ref, tri_ref, y_ref.at[sq], c_ref.at[sq], n_ref.at[sq], m_ref.at[sq])


def _mlstm_chunk(bias_ref, q_ref, k_ref, v_ref, o_ref, ifc_ref, ng_ref, tri_ref, y_ref,
                 c_ref, n_ref, m_ref):
    nh, dh = MLSTM_HEADS, MLSTM_HEAD_DIM
    L = q_ref.shape[0]
    tri = tri_ref[...]
    lane8 = lax.broadcasted_iota(jnp.int32, (1, LANE), 1)
    bias_c = jnp.zeros((1, LANE), F32)
    for h in range(nh):
        bias_c = jnp.where(lane8 == C_IF + h, bias_ref[h], bias_c)
        bias_c = jnp.where(lane8 == C_IF + nh + h, bias_ref[nh + h], bias_c)
    pre_c = ifc_ref[...] + bias_c
    cum_c = sum(_dot(tri, part) for part in _split3(_log_sigmoid(pre_c)))
    pre_r = pre_c.T[C_IF:C_IF + 2 * nh]
    cum_r = sum(_dot_nt(part, tri) for part in _split3(_log_sigmoid(pre_r)))

    rr = lax.broadcasted_iota(jnp.int32, (L, 1), 0)
    cc = lax.broadcasted_iota(jnp.int32, (1, L), 1)
    causal = cc <= rr

    for h in range(nh):
        cols = slice(h * dh, (h + 1) * dh)
        qb = q_ref[:, cols]
        kb = k_ref[:, cols]
        vh = v_ref[:, cols]
        qh = qb.astype(F32)
        kh = kb.astype(F32)
        b_c = cum_c[:, C_IF + nh + h:C_IF + nh + h + 1]
        li_c = pre_c[:, C_IF + h:C_IF + h + 1]
        b_r = cum_r[nh + h:nh + h + 1, :]
        li_r = pre_r[h:h + 1, :]
        m_prev = m_ref[h:h + 1, 0:1]

        dmat = jnp.where(causal, b_c - b_r + li_r, NEG)
        a = b_c + m_prev
        m_j = jnp.maximum(a, jnp.max(dmat, axis=1, keepdims=True))
        w_intra = jnp.exp(dmat - m_j)
        w_inter = jnp.exp(a - m_j)
        sc = _dot_nt(qb, kb) * w_intra
        c_old = c_ref[h]
        n_old = n_ref[h:h + 1, :]
        num = w_inter * _dot(qb, c_old.astype(BF16)) + _dot(sc.astype(BF16), vh)
        den = (w_inter * jnp.sum(qh * n_old, axis=1, keepdims=True)
               + jnp.sum(sc, axis=1, keepdims=True))
        hid = num / jnp.maximum(jnp.abs(den), jnp.exp(-m_j))

        g_tot = b_r[:, L - 1:L]
        lw_c = g_tot - b_c + li_c
        lw_r = g_tot - b_r + li_r
        m_new = jnp.maximum(g_tot + m_prev, jnp.max(lw_r, axis=1, keepdims=True))
        decay = jnp.exp(g_tot + m_prev - m_new)
        kw = jnp.exp(lw_c - m_new) * kh
        c_ref[h] = decay * c_old + _dot(kw.T.astype(BF16), vh)
        n_ref[h:h + 1, :] = decay * n_old + jnp.sum(kw, axis=0, keepdims=True)
        m_ref[h:h + 1, :] = jnp.broadcast_to(m_new, (1, LANE))

        hn = hid * lax.rsqrt(jnp.mean(hid * hid, axis=-1, keepdims=True) + EPS) * ng_ref[:, cols]
        y_ref[:, cols] = (_sigmoid(o_ref[:, cols]) * hn).astype(y_ref.dtype)


def _mlstm(seg_a, seg_b, seg_c, bias, norm_g, tri, batch, seq, chunk):
    nc = seq // chunk
    w = MLSTM_WIDTH
    nh, dh = MLSTM_HEADS, MLSTM_HEAD_DIM
    seqs = 1
    a3 = seg_a.reshape(batch, seq, SEG_A)
    b3 = seg_b.reshape(batch, seq, SEG_B)
    c3 = seg_c.reshape(batch, seq, SEG_C)

    def col_spec(off):
        return pl.BlockSpec((seqs, chunk, w), lambda b, c, o=off // w: (b, c, o))

    y = pl.pallas_call(
        _mlstm_kernel,
        grid=(batch // seqs, nc),
        in_specs=[
            pl.BlockSpec(memory_space=pltpu.SMEM),
            col_spec(A_QK), col_spec(A_QK + w), col_spec(A_V), col_spec(B_O),
            pl.BlockSpec((seqs, chunk, SEG_C), lambda b, c: (b, c, 0)),
            pl.BlockSpec((1, w), lambda b, c: (0, 0)),
            pl.BlockSpec((chunk, chunk), lambda b, c: (0, 0)),
        ],
        out_specs=pl.BlockSpec((seqs, chunk, w), lambda b, c: (b, c, 0)),
        out_shape=jax.ShapeDtypeStruct((batch, seq, w), BF16),
        scratch_shapes=[
            pltpu.VMEM((seqs, nh, dh, dh), F32),
            pltpu.VMEM((seqs, 8, dh), F32),
            pltpu.VMEM((seqs, 8, LANE), F32),
        ],
        compiler_params=pltpu.CompilerParams(
            dimension_semantics=("parallel", "arbitrary"), vmem_limit_bytes=VMEM_LIMIT),
        name="mlstm",
    )(bias, a3, a3, a3, b3, c3, norm_g, tri)
    return y.reshape(batch * seq, w)


def _outproj_kernel(x_ref, ya_ref, ym_ref, wa_ref, wm_ref, o_ref):
    o_ref[...] = x_ref[...] + _dot(ya_ref[...], wa_ref[...]) + _dot(ym_ref[...], wm_ref[...])


def _outproj(x2, ya, ym, wa, wm, tm):
    n, d = x2.shape
    return pl.pallas_call(
        _outproj_kernel,
        grid=(n // tm,),
        in_specs=[
            pl.BlockSpec((tm, d), lambda i: (i, 0)),
            pl.BlockSpec((tm, ya.shape[1]), lambda i: (i, 0)),
            pl.BlockSpec((tm, ym.shape[1]), lambda i: (i, 0)),
            pl.BlockSpec(wa.shape, lambda i: (0, 0)),
            pl.BlockSpec(wm.shape, lambda i: (0, 0)),
        ],
        out_specs=pl.BlockSpec((tm, d), lambda i: (i, 0)),
        out_shape=jax.ShapeDtypeStruct((n, d), F32),
        compiler_params=pltpu.CompilerParams(
            dimension_semantics=("parallel",), vmem_limit_bytes=VMEM_LIMIT),
        name="outproj",
    )(x2, ya, ym, wa, wm)


def _mlp_kernel(x_ref, g_ref, w1_ref, w2_ref, gf_ref, o_ref, h_ref):
    f = pl.program_id(1)

    @pl.when(f == 0)
    def _():
        x = x_ref[...]
        r = lax.rsqrt(jnp.mean(x * x, axis=-1, keepdims=True) + EPS)
        h_ref[...] = (x * r * g_ref[...]).astype(BF16)
        o_ref[...] = x

    u = jnp.maximum(_dot(h_ref[...], w1_ref[...]), 0.0)
    o_ref[...] += _dot((u * u).astype(BF16), w2_ref[...])

    @pl.when(f == pl.num_programs(1) - 1)
    def _():
        x2 = o_ref[...]
        r = lax.rsqrt(jnp.mean(x2 * x2, axis=-1, keepdims=True) + EPS)
        o_ref[...] = x2 * r * gf_ref[...]


def _mlp(x1, g, w1, w2, gf, tm, tf):
    n, d = x1.shape
    dff = w1.shape[1]
    return pl.pallas_call(
        _mlp_kernel,
        grid=(n // tm, dff // tf),
        in_specs=[
            pl.BlockSpec((tm, d), lambda i, f: (i, 0)),
            pl.BlockSpec((1, d), lambda i, f: (0, 0)),
            pl.BlockSpec((d, tf), lambda i, f: (0, f)),
            pl.BlockSpec((tf, d), lambda i, f: (f, 0)),
            pl.BlockSpec((1, d), lambda i, f: (0, 0)),
        ],
        out_specs=pl.BlockSpec((tm, d), lambda i, f: (i, 0)),
        out_shape=jax.ShapeDtypeStruct((n, d), F32),
        scratch_shapes=[pltpu.VMEM((tm, d), BF16)],
        compiler_params=pltpu.CompilerParams(
            dimension_semantics=("parallel", "arbitrary"), vmem_limit_bytes=VMEM_LIMIT),
        name="mlp",
    )(x1, g, w1, w2, gf)


def _row_tile(n, want):
    t = want
    while n % t:
        t //= 2
    return t


def _layer(x2, batch, seq, norm_mix_g, w_in, w_cmp_k1, w_cmp_k2, pos_cmp_k, w_cmp_v1, w_cmp_v2,
           pos_cmp_v, conv_w, conv_b, b_igate, b_fgate, mlstm_norm_g, w_out, norm_mlp_g,
           w_mlp_in, w_mlp_out):
    n, d = x2.shape
    assert seq % Q_BLOCK == 0 and seq >= WINDOW + Q_BLOCK and seq // SEL_BLOCK <= SEL_LANES
    nh = MLSTM_HEADS

    c_gate = NSA_WIDTH + 6 * NSA_KV_WIDTH
    c_qk = c_gate + NSA_HEADS * N_BRANCH
    c_v = c_qk + 2 * MLSTM_WIDTH
    c_o = c_v + MLSTM_WIDTH
    c_i = c_o + MLSTM_WIDTH
    c_f = c_i + nh
    w16 = w_in.astype(BF16)
    w_ab = jnp.concatenate(
        [w16[:, c_v:c_o], w16[:, c_qk:c_v], w16[:, :c_gate],
         w16[:, c_o:c_i]], axis=1)
    per_g = NSA_REP * N_BRANCH
    gate_cols = []
    for g in range(NSA_KV_GROUPS):
        gate_cols += [w16[:, c_gate + g * per_g:c_gate + (g + 1) * per_g],
                      jnp.zeros((d, C_GATE_STRIDE - per_g), BF16)]
    w_c = jnp.concatenate(gate_cols + [w16[:, c_i:c_f + nh],
                                       jnp.zeros((d, SEG_C - C_IF - 2 * nh), BF16)], axis=1)
    scale = jnp.concatenate([jnp.ones((1, A_QK + MLSTM_WIDTH), F32),
                             jnp.full((1, MLSTM_WIDTH), MLSTM_HEAD_DIM ** -0.5, F32),
                             jnp.full((1, NSA_WIDTH), NSA_HEAD_DIM ** -0.5, F32),
                             jnp.ones((1, SEG_A - A_KV + SEG_B), F32)], axis=1)
    conv_pad = ((0, 0), (A_QK, SEG_A - A_Q + SEG_B))
    cw_ab = jnp.pad(conv_w, conv_pad)
    cb_ab = jnp.pad(conv_b.reshape(1, -1), conv_pad)
    g_mix = norm_mix_g.reshape(1, d)

    tm = _row_tile(seq, ROW_TILE)
    seg_a, seg_b, seg_c = _inproj(x2, g_mix, w_ab, w_c, scale, cw_ab, cb_ab, tm, COL_TILE,
                                  (A_QK, A_Q), seq)

    n_sub = seq // CMP_STRIDE
    dh = NSA_HEAD_DIM
    w1s = jnp.stack([w_cmp_k1, w_cmp_v1]).reshape(2, 2, CMP_STRIDE, dh, dh)
    w1r = jnp.concatenate([w1s[:, 0], w1s[:, 1]], axis=-1).astype(BF16)
    w2s = jnp.stack([w_cmp_k2, w_cmp_v2]).astype(BF16)
    poss = jnp.stack([pos_cmp_k, pos_cmp_v]).reshape(2, 2, CMP_STRIDE, dh).transpose(0, 2, 1, 3)
    posr = jnp.pad(poss, ((0, 0), (0, 0), (0, 6), (0, 0))).astype(BF16)
    kvc, kvct = _compress(seg_a, w1r, w2s, posr, batch, seq)

    cmp_start = np.arange(n_sub) * CMP_STRIDE
    sel_start = np.arange(SEL_LANES) * SEL_BLOCK
    ovt = ((cmp_start[None, :] < sel_start[:, None] + SEL_BLOCK)
           & (cmp_start[None, :] + CMP_BLOCK - 1 >= sel_start[:, None])
           & (np.arange(n_sub)[None, :] < n_sub - CMP_BLOCK // CMP_STRIDE + 1))
    ovt = jnp.asarray(ovt, BF16)
    slopes = jnp.exp2(-8.0 * jnp.arange(1, NSA_HEADS + 1, dtype=F32) / NSA_HEADS)
    ocmp, pen, tiles = _nsa_cmp(seg_a, kvc, kvct, ovt, slopes, batch, seq)
    onehot, wext, dbias, ubias, wbias = _nsa_tables(slopes, seq)
    y_a = _nsa_attn(seg_a, seg_c, pen, ocmp, tiles, onehot, wext, dbias, ubias, wbias,
                    slopes, batch, seq)

    chunk = 256 if seq % 256 == 0 else 128
    bias = jnp.concatenate([b_igate, b_fgate]).astype(F32)
    tri = jnp.asarray(np.tril(np.ones((chunk, chunk), np.float32)), BF16)
    y_m = _mlstm(seg_a, seg_b, seg_c, bias, mlstm_norm_g.reshape(1, -1), tri, batch, seq, chunk)

    w_o = w_out.astype(BF16)
    x1 = _outproj(x2, y_a, y_m, w_o[:NSA_WIDTH], w_o[NSA_WIDTH:], _row_tile(n, 512))
    return x1, (norm_mlp_g.reshape(1, d), w_mlp_in.astype(BF16), w_mlp_out.astype(BF16))


def kernel(x, norm_mix_g, w_in, w_cmp_k1, w_cmp_k2, pos_cmp_k, w_cmp_v1, w_cmp_v2, pos_cmp_v, conv_w, conv_b, b_igate, b_fgate, mlstm_norm_g, w_out, norm_mlp_g, w_mlp_in, w_mlp_out, norm_f_g):
    batch, seq, d = x.shape
    depth = w_in.shape[0]
    assert depth == 1, "the final RMSNorm is fused into the last layer's channel mixer"
    x2 = x.reshape(batch * seq, d)
    tm = _row_tile(batch * seq, ROW_TILE)
    for l in range(depth):
        x1, (g_mlp, w1, w2) = _layer(
            x2, batch, seq, norm_mix_g[l], w_in[l], w_cmp_k1[l], w_cmp_k2[l], pos_cmp_k[l],
            w_cmp_v1[l], w_cmp_v2[l], pos_cmp_v[l], conv_w[l], conv_b[l], b_igate[l], b_fgate[l],
            mlstm_norm_g[l], w_out[l], norm_mlp_g[l], w_mlp_in[l], w_mlp_out[l])
        x2 = _mlp(x1, g_mlp, w1, w2, norm_f_g.reshape(1, d), tm, 512)
    return x2.reshape(batch, seq, d)
```

```python
import functools

import numpy as np
import jax
import jax.numpy as jnp
from jax import lax
from jax.experimental import pallas as pl
from jax.experimental.pallas import tpu as pltpu

F32 = jnp.float32
BF16 = jnp.bfloat16

EPS = 1e-6
NEG = -1e30
FORCE_BONUS = 1e4
PICKED = -3e38
MASK_BIG = 1e30

D_MODEL = 2048
NSA_HEAD_DIM = 128
NSA_WIDTH = D_MODEL // 2
NSA_HEADS = NSA_WIDTH // NSA_HEAD_DIM
NSA_REP = 4
NSA_KV_GROUPS = NSA_HEADS // NSA_REP
NSA_KV_WIDTH = NSA_KV_GROUPS * NSA_HEAD_DIM
CMP_BLOCK = 32
CMP_STRIDE = 16
SEL_BLOCK = 64
SEL_TOPK = 16
WINDOW = 512
Q_BLOCK = 128
N_BRANCH = 3
MLSTM_HEAD_DIM = 256
MLSTM_WIDTH = D_MODEL - NSA_WIDTH
MLSTM_HEADS = MLSTM_WIDTH // MLSTM_HEAD_DIM
CONV_WIDTH = 4

LANE = 128
SEL_LANES = 128
KEY_TILE = 128
SEL_TILES_PER_STEP = 2
WIN_TILES = WINDOW // KEY_TILE
SUM_ROWS = 16
VMEM_LIMIT = 56 * 1024 * 1024
ROW_TILE = 1024

A_V, A_QK = 0, MLSTM_WIDTH
A_Q = A_QK + 2 * MLSTM_WIDTH
A_KV = A_Q + NSA_WIDTH
SEG_A = A_KV + 6 * NSA_KV_WIDTH
B_O = 0
SEG_B = MLSTM_WIDTH
C_GATE_STRIDE = 16
C_IF = NSA_KV_GROUPS * C_GATE_STRIDE
SEG_C = LANE
COL_TILE = 512

WCOL_PAD, WCOL_HI, WCOL_LO, WCOL_QHI, WCOL_QLO = 0, 1, 2, 3, 4


def _dot(a, b):
    return jnp.dot(a, b, preferred_element_type=F32)


def _dot_nt(a, b):
    return lax.dot_general(a, b, (((1,), (1,)), ((), ())), preferred_element_type=F32)


def _sigmoid(x):
    return 1.0 / (1.0 + jnp.exp(-x))


def _shifted(x, tail, s):
    xs = pltpu.roll(x, s, axis=0)
    ts = pltpu.roll(tail, s, axis=0)
    row8 = lax.broadcasted_iota(jnp.int32, (8, 1), 0)
    head = jnp.where(row8 < s, ts, xs[:8])
    return jnp.concatenate([head, xs[8:]], axis=0)


def _conv_silu(x, tail, w, b):
    y = b + _shifted(x, tail, CONV_WIDTH - 1) * w[0:1]
    for i in range(1, CONV_WIDTH - 1):
        y = y + _shifted(x, tail, CONV_WIDTH - 1 - i) * w[i:i + 1]
    y = y + x * w[CONV_WIDTH - 1:CONV_WIDTH]
    return y * _sigmoid(y)


def _inproj_kernel(x_ref, g_ref, w_ref, wc_ref, cs_ref, cw_ref, cb_ref, oa_ref, ob_ref, oc_ref,
                   h_ref, halo_ref, *, nb, n_plain, tiles_per_seq):
    i = pl.program_id(0)
    j = pl.program_id(1)

    @pl.when(j == 0)
    def _():
        x = x_ref[...]
        r = lax.rsqrt(jnp.mean(x * x, axis=-1, keepdims=True) + EPS)
        h_ref[...] = (x * r * g_ref[...]).astype(BF16)

    is_conv = j > nb + n_plain

    @pl.when((j > nb) & jnp.logical_not(is_conv))
    def _():
        oa_ref[...] = (_dot(h_ref[...], w_ref[...]) * cs_ref[...]).astype(oa_ref.dtype)

    @pl.when(j < nb)
    def _():
        ob_ref[...] = (_dot(h_ref[...], w_ref[...]) * cs_ref[...]).astype(ob_ref.dtype)

    @pl.when(j == nb)
    def _():
        oc_ref[...] = _dot(h_ref[...], wc_ref[...])

    @pl.when(is_conv)
    def _():
        slot = j - (nb + 1 + n_plain)

        @pl.when(i % tiles_per_seq == 0)
        def _():
            halo_ref[slot] = jnp.zeros(halo_ref.shape[1:], F32)

        acc = _dot(h_ref[...], w_ref[...])
        tail = halo_ref[slot]
        halo_ref[slot] = acc[acc.shape[0] - 8:]
        y = _conv_silu(acc, tail, cw_ref[...], cb_ref[...])
        oa_ref[...] = (y * cs_ref[...]).astype(oa_ref.dtype)


def _inproj(x2, g, w, w_c, cs, cw, cb, tm, tn, conv_cols, seq):
    n, d = x2.shape
    na, nb = SEG_A // tn, SEG_B // tn
    conv_lo, conv_hi = conv_cols[0] // tn, conv_cols[1] // tn
    assert conv_cols[0] % tn == 0 and conv_cols[1] % tn == 0 and seq % tm == 0 and conv_hi <= na
    assert SEG_A % tn == 0 and SEG_B % tn == 0 and w.shape[1] == SEG_A + SEG_B
    n_conv = conv_hi - conv_lo
    n_plain = na - n_conv

    def a_tile(j):
        p = jnp.maximum(j - nb - 1, 0)
        plain = jnp.where(p < conv_lo, p, p + n_conv)
        return jnp.where(p < n_plain, plain, conv_lo + p - n_plain)

    def w_tile(i, j):
        return (0, jnp.where(j <= nb, na + jnp.minimum(j, nb - 1), a_tile(j)))

    return pl.pallas_call(
        functools.partial(_inproj_kernel, nb=nb, n_plain=n_plain, tiles_per_seq=seq // tm),
        grid=(n // tm, na + nb + 1),
        in_specs=[
            pl.BlockSpec((tm, d), lambda i, j: (i, 0)),
            pl.BlockSpec((1, d), lambda i, j: (0, 0)),
            pl.BlockSpec((d, tn), w_tile),
            pl.BlockSpec((d, SEG_C), lambda i, j: (0, 0)),
            pl.BlockSpec((1, tn), w_tile),
            pl.BlockSpec((CONV_WIDTH, tn), w_tile),
            pl.BlockSpec((1, tn), w_tile),
        ],
        out_specs=[pl.BlockSpec((tm, tn), lambda i, j: (i, a_tile(j))),
                   pl.BlockSpec((tm, tn), lambda i, j: (i, jnp.minimum(j, nb - 1))),
                   pl.BlockSpec((tm, SEG_C), lambda i, j: (i, 0))],
        out_shape=[jax.ShapeDtypeStruct((n, SEG_A), BF16), jax.ShapeDtypeStruct((n, SEG_B), F32),
                   jax.ShapeDtypeStruct((n, SEG_C), F32)],
        scratch_shapes=[pltpu.VMEM((tm, d), BF16),
                        pltpu.VMEM((conv_hi - conv_lo, 8, tn), F32)],
        compiler_params=pltpu.CompilerParams(
            dimension_semantics=("arbitrary", "arbitrary"), vmem_limit_bytes=VMEM_LIMIT),
        name="inproj",
    )(x2, g, w, w_c, cs, cw, cb)


def _compress_kernel(x_ref, w1_ref, w2_ref, pos_ref, o_ref, ot_ref, xf_ref):
    n_sub = o_ref.shape[2]
    dh = NSA_HEAD_DIM
    xf_ref[...] = x_ref[...].astype(F32)
    acc = jnp.zeros((n_sub, 2 * dh), F32)
    posw = jnp.zeros((1, dh), F32)
    for p in range(CMP_STRIDE):
        wp = w1_ref[0, p]
        acc = acc + _dot(xf_ref[pl.ds(p, n_sub, stride=CMP_STRIDE), :].astype(BF16), wp)
        pw = _dot(pos_ref[0, p], wp)
        posw = posw + pw[0:1, :dh] + pw[1:2, dh:]
    bot = pltpu.roll(acc[:, dh:], n_sub - 1, axis=0)
    pre = acc[:, :dh] + bot + posw
    hid = pre * _sigmoid(pre)
    out = _dot(hid.astype(BF16), w2_ref[0])
    o_ref[0, 0] = out.astype(o_ref.dtype)
    ot_ref[0, 0] = out.T.astype(ot_ref.dtype)


def _compress(seg_a, w1r, w2s, posr, batch, seq):
    g = NSA_KV_GROUPS
    c = 2 * g
    dh = NSA_HEAD_DIM
    n_sub = seq // CMP_STRIDE
    return pl.pallas_call(
        _compress_kernel,
        grid=(batch, c),
        in_specs=[
            pl.BlockSpec((seq, dh), lambda i, j: (i, A_KV // dh + j)),
            pl.BlockSpec((1, CMP_STRIDE, dh, 2 * dh), lambda i, j: (j // g, 0, 0, 0)),
            pl.BlockSpec((1, dh, dh), lambda i, j: (j // g, 0, 0)),
            pl.BlockSpec((1, CMP_STRIDE, 8, dh), lambda i, j: (j // g, 0, 0, 0)),
        ],
        out_specs=[pl.BlockSpec((1, 1, n_sub, dh), lambda i, j: (i, j, 0, 0)),
                   pl.BlockSpec((1, 1, dh, n_sub), lambda i, j: (i, j, 0, 0))],
        out_shape=[jax.ShapeDtypeStruct((batch, c, n_sub, dh), BF16),
                   jax.ShapeDtypeStruct((batch, c, dh, n_sub), BF16)],
        scratch_shapes=[pltpu.VMEM((seq, dh), F32)],
        compiler_params=pltpu.CompilerParams(
            dimension_semantics=("parallel", "parallel"), vmem_limit_bytes=VMEM_LIMIT),
        name="compress",
    )(seg_a, w1r, w2s, posr)


def _stack_heads(q_all):
    dh = NSA_HEAD_DIM
    return jnp.concatenate([q_all[:, r * dh:(r + 1) * dh] for r in range(NSA_REP)], axis=0)


def _tile_list(hit_col, tile0, pad_tile, mats_ref):
    n = SEL_LANES
    lane = lax.broadcasted_iota(jnp.int32, (1, n), 1)
    blk = lax.broadcasted_iota(jnp.int32, (n, n), 0)
    slot = lax.broadcasted_iota(jnp.int32, (n, n), 1).astype(F32)
    hit = jnp.broadcast_to(hit_col, (n, n))
    both = jnp.maximum(hit, pltpu.roll(hit, n - 1, axis=0))
    act = jnp.where((both > 0.0) & (blk % 2 == 0) & (blk // 2 < tile0), 1.0, 0.0)
    rank = _dot(mats_ref[0], act.astype(BF16))
    cnt = jnp.sum(act, axis=0, keepdims=True)
    place = jnp.where((act > 0.0) & (rank == slot), 1.0, 0.0)
    tiles = _dot(mats_ref[1, 0:8], place.astype(BF16))[0:1]
    out = jnp.where(lane.astype(F32) < cnt, tiles, float(pad_tile))
    return jnp.where(lane == n - 1, cnt, out).astype(jnp.int32)


def _nsa_cmp_kernel(slopes_ref, q_ref, kc_ref, vct_ref, ovt_ref, mats_ref, ocmp_ref, pen_ref, tiles_ref, *,
                    seq, sub_blocks):
    step = pl.program_id(2)
    nq_step = sub_blocks * Q_BLOCK
    chunk = min(SEL_LANES, kc_ref.shape[2])
    n_chunks = kc_ref.shape[2] // chunk
    need = ((step + 1) * nq_step - CMP_BLOCK) // CMP_STRIDE + 1
    n_need = (need + chunk - 1) // chunk
    for v in range(1, n_chunks + 1):
        cond = (n_need == v) if v < n_chunks else (n_need >= v)
        if v == 1:
            cond = n_need <= 1

        @pl.when(cond)
        def _(v=v):
            hit = None
            for sub in range(sub_blocks):
                rows = slice(sub * Q_BLOCK, (sub + 1) * Q_BLOCK)
                h = _nsa_cmp_block(slopes_ref, q_ref[rows, :], kc_ref, vct_ref, ovt_ref,
                                   ocmp_ref.at[0, 0, sub], pen_ref.at[0, 0, rows],
                                   step * sub_blocks + sub, seq, v * chunk,
                                   past_first_block=(sub >= 1 or v >= 2))
                hit = h if hit is None else jnp.maximum(hit, h)
            tiles_ref[0, 0, 0] = _tile_list(hit, step * sub_blocks, seq // KEY_TILE, mats_ref)


def _nsa_cmp_block(slopes_ref, q_all, kc_ref, vct_ref, ovt_ref, ocmp_ref, pen_ref, qb, seq, n_use,
                   past_first_block):
    g = pl.program_id(1)
    nq = Q_BLOCK
    n_cpad = n_use
    n_cmp = seq // CMP_STRIDE - CMP_BLOCK // CMP_STRIDE + 1
    t0 = qb * nq

    s_t = _dot_nt(kc_ref[0, 0, :n_use, :], _stack_heads(q_all))
    n_s = lax.broadcasted_iota(jnp.int32, (n_cpad, 1), 0)
    q_l = lax.broadcasted_iota(jnp.int32, (1, nq), 1)
    dist = (t0 - (CMP_BLOCK - 1)) + q_l - n_s * CMP_STRIDE
    valid = (dist >= 0) & (n_s < n_cmp)
    dist_f = dist.astype(F32)
    probs = []
    p_sum = jnp.zeros((n_cpad, nq), F32)
    for r in range(NSA_REP):
        slope = slopes_ref[g * NSA_REP + r]
        s = jnp.where(valid, s_t[:, r * nq:(r + 1) * nq] - slope * dist_f, NEG)
        m = jnp.max(s, axis=0, keepdims=True)
        e = jnp.exp(s - m)
        inv = jnp.where(m > 0.5 * NEG, 1.0 / jnp.sum(e, axis=0, keepdims=True), 0.0)
        p = e * inv
        probs.append(p.astype(BF16))
        p_sum = p_sum + p
    ocmp_ref[...] = _dot(vct_ref[0, 0, :, :n_use], jnp.concatenate(probs, axis=1))

    p_hi = p_sum.astype(BF16)
    p_lo = (p_sum - p_hi.astype(F32)).astype(BF16)
    ovt = ovt_ref[:, :n_use]
    imp = _dot(ovt, p_hi) + _dot(ovt, p_lo)
    j_i = lax.broadcasted_iota(jnp.int32, (SEL_LANES, 1), 0)
    t_l = t0 + q_l
    cur = t_l // SEL_BLOCK
    forced = (j_i == 0) | (j_i == cur) | (j_i == cur - 1)
    causal_blk = j_i * SEL_BLOCK <= t_l
    val = jnp.where(causal_blk, jnp.where(forced, imp + FORCE_BONUS, imp), NEG)
    j_f = j_i.astype(F32)
    sel_t = jnp.zeros((SEL_LANES, nq), F32)
    rounds = min(SEL_TOPK, seq // SEL_BLOCK)
    if past_first_block and rounds >= 3:
        sel_t = jnp.where(forced, 1.0, sel_t)
        val = jnp.where(forced, PICKED, val)
        rounds -= 3
    for _ in range(rounds):
        mx = jnp.max(val, axis=0, keepdims=True)
        first = jnp.min(jnp.where(val == mx, j_f, float(SEL_LANES)), axis=0, keepdims=True)
        pick = j_f == first
        sel_t = jnp.where(pick, 1.0, sel_t)
        val = jnp.where(pick, PICKED, val)
    sel = sel_t.T
    pen_ref[...] = ((sel - 1.0) * MASK_BIG).astype(pen_ref.dtype)
    return jnp.max(sel_t, axis=1, keepdims=True)


def _nsa_cmp(seg_a, kvc, kvct, ovt, slopes, batch, seq):
    nqb = seq // Q_BLOCK
    gq = NSA_REP * NSA_HEAD_DIM
    g_ = NSA_KV_GROUPS
    n_cpad = kvc.shape[2]
    sub = _nsa_sub_blocks(seq)
    nstep = nqb // sub
    idx = np.arange(SEL_LANES)
    mats = jnp.asarray(np.stack([idx[:, None] > idx[None, :],
                                 np.broadcast_to(idx[None, :] // 2, (SEL_LANES, SEL_LANES))]), BF16)
    return pl.pallas_call(
        functools.partial(_nsa_cmp_kernel, seq=seq, sub_blocks=sub),
        grid=(batch, g_, nstep),
        in_specs=[
            pl.BlockSpec(memory_space=pltpu.SMEM),
            pl.BlockSpec((sub * Q_BLOCK, gq), lambda b, g, q: (b * nstep + q, A_Q // gq + g)),
            pl.BlockSpec((1, 1, n_cpad, NSA_HEAD_DIM), lambda b, g, q: (b, g, 0, 0)),
            pl.BlockSpec((1, 1, NSA_HEAD_DIM, n_cpad), lambda b, g, q: (b, g_ + g, 0, 0)),
            pl.BlockSpec((SEL_LANES, n_cpad), lambda b, g, q: (0, 0)),
            pl.BlockSpec((2, SEL_LANES, SEL_LANES), lambda b, g, q: (0, 0, 0)),
        ],
        out_specs=[
            pl.BlockSpec((1, 1, sub, NSA_HEAD_DIM, gq), lambda b, g, q: (b, g, q, 0, 0)),
            pl.BlockSpec((1, 1, sub * Q_BLOCK, SEL_LANES), lambda b, g, q: (b, g, q, 0)),
            pl.BlockSpec((1, 1, 1, 1, SEL_LANES), lambda b, g, q: (b, g, q, 0, 0)),
        ],
        out_shape=[
            jax.ShapeDtypeStruct((batch, g_, nqb, NSA_HEAD_DIM, gq), F32),
            jax.ShapeDtypeStruct((batch, g_, seq, SEL_LANES), BF16),
            jax.ShapeDtypeStruct((batch, g_, nstep, 1, SEL_LANES), jnp.int32),
        ],
        compiler_params=pltpu.CompilerParams(
            dimension_semantics=("parallel", "parallel", "parallel"), vmem_limit_bytes=VMEM_LIMIT),
        name="nsa_cmp",
    )(slopes, seg_a, kvc, kvct, ovt, mats)


def _nsa_attn_kernel(slopes_ref, tiles_ref, q_ref, gate_ref, pen_ref, ocmp_ref, ks_ref, vs_ref,
                     kw_ref, vw_ref, onehot_ref, wext_ref, dbias_ref, ubias_ref, wbias_ref, o_ref,
                     ksel_ref, vselt_ref, kwin_ref, vwint_ref,
                     qa_ref, qw_ref, m_ref, acc_ref, owin_ref, gt_ref, sa_ref, sb_ref,
                     *, seq, sub_blocks):
    g = pl.program_id(1)
    step = pl.program_id(2)
    dh = NSA_HEAD_DIM
    nq = sub_blocks * Q_BLOCK
    kt = KEY_TILE
    per = SEL_TILES_PER_STEP
    pad_tile = seq // kt
    t0 = step * nq
    tile0 = step * sub_blocks

    @pl.when(step == 0)
    def _():
        ksel_ref[0:seq, :dh] = ks_ref[...]
        ksel_ref[0:seq, dh:] = onehot_ref[...]
        ksel_ref[seq:, :dh] = jnp.zeros((kt, dh), BF16)
        ksel_ref[seq:, dh:] = jnp.ones((kt, SEL_LANES), BF16)
        lane2 = lax.broadcasted_iota(jnp.int32, (WINDOW, dh + LANE), 1)
        kwin_ref[0:WINDOW, :] = jnp.where(lane2 == dh + WCOL_PAD, 1.0, 0.0).astype(BF16)
        kwin_ref[WINDOW:, :dh] = kw_ref[...]
        kwin_ref[WINDOW:, dh:] = wext_ref[...]
        zero_tile = jnp.zeros((dh + SUM_ROWS, kt), BF16)
        ones_rows = jnp.ones((SUM_ROWS, kt), BF16)
        vselt_ref[pad_tile] = zero_tile
        for i in range(WIN_TILES):
            vwint_ref[i] = zero_tile

        def transpose_tile(t, carry):
            r0 = pl.multiple_of(t * kt, kt)
            vselt_ref[t, :dh] = vs_ref[pl.ds(r0, kt), :].astype(F32).T.astype(BF16)
            vselt_ref[t, dh:] = ones_rows
            vwint_ref[t + WIN_TILES, :dh] = vw_ref[pl.ds(r0, kt), :].astype(F32).T.astype(BF16)
            vwint_ref[t + WIN_TILES, dh:] = ones_rows
            return carry

        lax.fori_loop(0, seq // kt, transpose_tile, 0)

    slopes = [slopes_ref[g * NSA_REP + r] for r in range(NSA_REP)]
    row_q = lax.broadcasted_iota(jnp.int32, (nq, 1), 0)
    lane = lax.broadcasted_iota(jnp.int32, (1, LANE), 1)
    q_all = q_ref[...]

    pen = pen_ref[0, 0].astype(F32)
    blk_rel = ((lane - (t0 + row_q) // SEL_BLOCK) * SEL_BLOCK).astype(F32)
    for r in range(NSA_REP):
        qa_ref[r * nq:(r + 1) * nq, :dh] = q_all[:, r * dh:(r + 1) * dh]
        qa_ref[r * nq:(r + 1) * nq, dh:] = (pen + slopes[r] * blk_rel).astype(BF16)

    def listed(idx):
        return tiles_ref[0, 0, 0, 0, idx]

    cnt = listed(SEL_LANES - 1)

    def group_scores(it):
        keys = jnp.concatenate(
            [ksel_ref[pl.ds(pl.multiple_of(listed(it * per + i) * kt, kt), kt), :]
             for i in range(per)], axis=0)
        return _dot_nt(keys, qa_ref[...]) + ubias_ref[0]

    s = _dot_nt(ksel_ref[pl.ds(pl.multiple_of(t0, kt), nq), :], qa_ref[...]) + dbias_ref[0]
    m0 = jnp.max(s, axis=0, keepdims=True)
    p = jnp.exp(s - m0)
    m_ref[...] = m0
    vals = jnp.concatenate([vselt_ref[tile0 + a] for a in range(sub_blocks)], axis=1)
    acc_ref[...] = _dot(vals, p.astype(BF16))
    sa_ref[...] = group_scores(0)

    tp = t0 + WINDOW + row_q
    t_hi = (tp // SEL_BLOCK).astype(F32)
    t_lo = (tp % SEL_BLOCK).astype(F32)
    for r in range(NSA_REP):
        sl = slopes[r]
        ext = jnp.where(lane == WCOL_PAD, -MASK_BIG, 0.0)
        ext = jnp.where(lane == WCOL_HI, sl * SEL_BLOCK, ext)
        ext = jnp.where(lane == WCOL_LO, sl, ext)
        ext = jnp.where(lane == WCOL_QHI, -sl * SEL_BLOCK * t_hi, ext)
        ext = jnp.where(lane == WCOL_QLO, -sl * t_lo, ext)
        qw_ref[r * nq:(r + 1) * nq, :dh] = q_all[:, r * dh:(r + 1) * dh]
        qw_ref[r * nq:(r + 1) * nq, dh:] = ext.astype(BF16)
    wlen = WINDOW + nq
    s = _dot_nt(kwin_ref[pl.ds(pl.multiple_of(t0, kt), wlen), :], qw_ref[...]) + wbias_ref[...]
    e = jnp.exp(s - jnp.max(s, axis=0, keepdims=True))
    vwin = jnp.concatenate([vwint_ref[tile0 + i] for i in range(WIN_TILES + sub_blocks)], axis=1)
    ow = _dot(vwin, e.astype(BF16))
    owin_ref[...] = ow[:dh] / ow[dh:dh + 1]

    def absorb(s, grp):
        vals = jnp.concatenate([vselt_ref[listed(grp * per + i)] for i in range(per)], axis=1)
        m_old = m_ref[...]
        m_new = jnp.maximum(m_old, jnp.max(s, axis=0, keepdims=True))
        alpha = jnp.exp(m_old - m_new)
        p = jnp.exp(s - m_new)
        acc_ref[...] = alpha * acc_ref[...] + _dot(vals, p.astype(BF16))
        m_ref[...] = m_new

    def sel_body(it, carry):
        s = sa_ref[...]
        sb_ref[...] = group_scores(2 * it + 1)
        absorb(s, 2 * it)
        s = sb_ref[...]
        sa_ref[...] = group_scores(2 * it + 2)
        absorb(s, 2 * it + 1)
        return carry

    lax.fori_loop(0, (cnt + 2 * per - 1) // (2 * per), sel_body, 0)
    o_sel = acc_ref[:dh, :] / acc_ref[dh:dh + 1, :]
    o_win = owin_ref[...]

    gt_ref[...] = _sigmoid(gate_ref[...]).T
    gate_t = gt_ref[pl.ds(pl.multiple_of(g * C_GATE_STRIDE, C_GATE_STRIDE), C_GATE_STRIDE), :]
    for a in range(sub_blocks):
        o_cmp = ocmp_ref[0, 0, a]
        qs = slice(a * Q_BLOCK, (a + 1) * Q_BLOCK)
        for r in range(NSA_REP):
            cols = slice(r * nq + a * Q_BLOCK, r * nq + (a + 1) * Q_BLOCK)
            c0 = N_BRANCH * r
            out_t = (gate_t[c0:c0 + 1, qs] * o_cmp[:, r * Q_BLOCK:(r + 1) * Q_BLOCK]
                     + gate_t[c0 + 1:c0 + 2, qs] * o_sel[:, cols]
                     + gate_t[c0 + 2:c0 + 3, qs] * o_win[:, cols])
            o_ref[qs, r * dh:(r + 1) * dh] = out_t.T.astype(o_ref.dtype)


def _nsa_sub_blocks(seq):
    return 2 if (seq // Q_BLOCK) % 2 == 0 else 1


def _nsa_attn(seg_a, seg_c, pen, ocmp, tiles, onehot, wext, dbias, ubias, wbias, slopes, batch, seq):
    n = batch * seq
    sub = _nsa_sub_blocks(seq)
    nq = sub * Q_BLOCK
    nqb = seq // nq
    gq = NSA_REP * NSA_HEAD_DIM
    g_ = NSA_KV_GROUPS
    dh, kt = NSA_HEAD_DIM, KEY_TILE
    dk = dh + SEL_LANES
    rq = NSA_REP * nq
    n_t = seq // kt

    def kv_spec(kind):
        return pl.BlockSpec((seq, dh), lambda b, g, q, k=kind: (b, A_KV // dh + k * g_ + g))

    def const_spec(arr):
        return pl.BlockSpec(arr.shape, lambda b, g, q, nd=arr.ndim: (0,) * nd)

    return pl.pallas_call(
        functools.partial(_nsa_attn_kernel, seq=seq, sub_blocks=sub),
        grid=(batch, g_, nqb),
        in_specs=[
            pl.BlockSpec(memory_space=pltpu.SMEM),
            pl.BlockSpec((1, 1, 1, 1, SEL_LANES), lambda b, g, q: (b, g, q, 0, 0),
                         memory_space=pltpu.SMEM),
            pl.BlockSpec((nq, gq), lambda b, g, q: (b * nqb + q, A_Q // gq + g)),
            pl.BlockSpec((nq, SEG_C), lambda b, g, q: (b * nqb + q, 0)),
            pl.BlockSpec((1, 1, nq, SEL_LANES), lambda b, g, q: (b, g, q, 0)),
            pl.BlockSpec((1, 1, sub, NSA_HEAD_DIM, gq), lambda b, g, q: (b, g, q, 0, 0)),
            kv_spec(2), kv_spec(3), kv_spec(4), kv_spec(5),
            const_spec(onehot), const_spec(wext),
            pl.BlockSpec((1,) + dbias.shape[1:], lambda b, g, q: (g, 0, 0)),
            pl.BlockSpec((1,) + ubias.shape[1:], lambda b, g, q: (g, 0, 0)),
            const_spec(wbias),
        ],
        out_specs=pl.BlockSpec((nq, gq), lambda b, g, q: (b * nqb + q, g)),
        out_shape=jax.ShapeDtypeStruct((n, NSA_WIDTH), BF16),
        scratch_shapes=[
            pltpu.VMEM((seq + kt, dk), BF16),
            pltpu.VMEM((n_t + 1, dh + SUM_ROWS, kt), BF16),
            pltpu.VMEM((seq + WINDOW, dh + LANE), BF16),
            pltpu.VMEM((n_t + WIN_TILES, dh + SUM_ROWS, kt), BF16),
            pltpu.VMEM((rq, dk), BF16),
            pltpu.VMEM((rq, dk), BF16),
            pltpu.VMEM((1, rq), F32),
            pltpu.VMEM((dh + SUM_ROWS, rq), F32),
            pltpu.VMEM((dh, rq), F32),
            pltpu.VMEM((SEG_C, nq), F32),
            pltpu.VMEM((SEL_TILES_PER_STEP * KEY_TILE, rq), F32),
            pltpu.VMEM((SEL_TILES_PER_STEP * KEY_TILE, rq), F32),
        ],
        compiler_params=pltpu.CompilerParams(
            dimension_semantics=("parallel", "parallel", "arbitrary"), vmem_limit_bytes=VMEM_LIMIT),
        name="nsa_attn",
    )(slopes, tiles, seg_a, seg_c, pen, ocmp, seg_a, seg_a, seg_a, seg_a, onehot, wext,
      dbias, ubias, wbias)


def _nsa_tables(slopes, seq):
    g_, kt = NSA_KV_GROUPS, KEY_TILE
    pos = np.arange(seq)
    onehot = jnp.asarray(pos[:, None] // SEL_BLOCK == np.arange(SEL_LANES)[None, :], BF16)
    ext = np.zeros((seq, LANE), np.float32)
    ext[:, WCOL_HI] = (pos + WINDOW) // SEL_BLOCK
    ext[:, WCOL_LO] = (pos + WINDOW) % SEL_BLOCK
    ext[:, WCOL_QHI] = 1.0
    ext[:, WCOL_QLO] = 1.0
    wext = jnp.asarray(ext, BF16)

    nq = _nsa_sub_blocks(seq) * Q_BLOCK

    def alibi_in_block(rows):
        u = jnp.asarray((np.arange(rows) % SEL_BLOCK).astype(np.float32))[None, :, None, None]
        t = jnp.broadcast_to(slopes.reshape(g_, 1, NSA_REP, 1) * u, (g_, rows, NSA_REP, nq))
        return t.reshape(g_, rows, NSA_REP * nq)

    ubias = alibi_in_block(SEL_TILES_PER_STEP * kt)
    kq = np.arange(nq)[:, None] <= np.arange(nq)[None, :]
    causal = np.tile(np.where(kq, 0.0, NEG).astype(np.float32), (1, NSA_REP))
    dbias = alibi_in_block(nq) + jnp.asarray(causal)[None]
    ki = np.arange(WINDOW + nq)[:, None]
    qi = np.arange(nq)[None, :]
    band = np.where((ki > qi) & (ki <= qi + WINDOW), 0.0, NEG).astype(np.float32)
    wbias = jnp.asarray(np.tile(band, (1, NSA_REP)))
    return onehot, wext, dbias, ubias, wbias


def _log_sigmoid(x):
    return jnp.minimum(x, 0.0) - jnp.log(1.0 + jnp.exp(-jnp.abs(x)))


def _split3(x):
    hi = x.astype(BF16)
    r1 = x - hi.astype(F32)
    mid = r1.astype(BF16)
    lo = (r1 - mid.astype(F32)).astype(BF16)
    return hi, mid, lo


def _mlstm_kernel(bias_ref, q_ref, k_ref, v_ref, o_ref, ifc_ref, ng_ref,
                  tri_ref, y_ref, c_ref, n_ref, m_ref):
    ch = pl.program_id(1)

    @pl.when(ch == 0)
    def _():
        c_ref[...] = jnp.zeros(c_ref.shape, F32)
        n_ref[...] = jnp.zeros(n_ref.shape, F32)
        m_ref[...] = jnp.zeros(m_ref.shape, F32)

    for sq in range(q_ref.shape[0]):
        _mlstm_chunk(bias_ref, q_ref.at[sq], k_ref.at[sq], v_ref.at[sq], o_ref.at[sq], ifc_ref.at[sq],
                     ng_ref, tri_ref, y_ref.at[sq], c_ref.at[sq], n_ref.at[sq], m_ref.at[sq])


def _mlstm_chunk(bias_ref, q_ref, k_ref, v_ref, o_ref, ifc_ref, ng_ref, tri_ref, y_ref,
                 c_ref, n_ref, m_ref):
    nh, dh = MLSTM_HEADS, MLSTM_HEAD_DIM
    L = q_ref.shape[0]
    tri = tri_ref[...]
    lane8 = lax.broadcasted_iota(jnp.int32, (1, LANE), 1)
    bias_c = jnp.zeros((1, LANE), F32)
    for h in range(nh):
        bias_c = jnp.where(lane8 == C_IF + h, bias_ref[h], bias_c)
        bias_c = jnp.where(lane8 == C_IF + nh + h, bias_ref[nh + h], bias_c)
    pre_c = ifc_ref[...] + bias_c
    cum_c = sum(_dot(tri, part) for part in _split3(_log_sigmoid(pre_c)))
    pre_r = pre_c.T[C_IF:C_IF + 2 * nh]
    cum_r = sum(_dot_nt(part, tri) for part in _split3(_log_sigmoid(pre_r)))

    rr = lax.broadcasted_iota(jnp.int32, (L, 1), 0)
    cc = lax.broadcasted_iota(jnp.int32, (1, L), 1)
    causal = cc <= rr

    for h in range(nh):
        cols = slice(h * dh, (h + 1) * dh)
        qb = q_ref[:, cols]
        kb = k_ref[:, cols]
        vh = v_ref[:, cols]
        qh = qb.astype(F32)
        kh = kb.astype(F32)
        b_c = cum_c[:, C_IF + nh + h:C_IF + nh + h + 1]
        li_c = pre_c[:, C_IF + h:C_IF + h + 1]
        b_r = cum_r[nh + h:nh + h + 1, :]
        li_r = pre_r[h:h + 1, :]
        m_prev = m_ref[h:h + 1, 0:1]

        dmat = jnp.where(causal, b_c - b_r + li_r, NEG)
        a = b_c + m_prev
        m_j = jnp.maximum(a, jnp.max(dmat, axis=1, keepdims=True))
        w_intra = jnp.exp(dmat - m_j)
        w_inter = jnp.exp(a - m_j)
        sc = _dot_nt(qb, kb) * w_intra
        c_old = c_ref[h]
        n_old = n_ref[h:h + 1, :]
        num = w_inter * _dot(qb, c_old.astype(BF16)) + _dot(sc.astype(BF16), vh)
        den = (w_inter * jnp.sum(qh * n_old, axis=1, keepdims=True)
               + jnp.sum(sc, axis=1, keepdims=True))
        hid = num / jnp.maximum(jnp.abs(den), jnp.exp(-m_j))

        g_tot = b_r[:, L - 1:L]
        lw_c = g_tot - b_c + li_c
        lw_r = g_tot - b_r + li_r
        m_new = jnp.maximum(g_tot + m_prev, jnp.max(lw_r, axis=1, keepdims=True))
        decay = jnp.exp(g_tot + m_prev - m_new)
        kw = jnp.exp(lw_c - m_new) * kh
        c_ref[h] = decay * c_old + _dot(kw.T.astype(BF16), vh)
        n_ref[h:h + 1, :] = decay * n_old + jnp.sum(kw, axis=0, keepdims=True)
        m_ref[h:h + 1, :] = jnp.broadcast_to(m_new, (1, LANE))

        hn = hid * lax.rsqrt(jnp.mean(hid * hid, axis=-1, keepdims=True) + EPS) * ng_ref[:, cols]
        y_ref[:, cols] = (_sigmoid(o_ref[:, cols]) * hn).astype(y_ref.dtype)


def _mlstm(seg_a, seg_b, seg_c, bias, norm_g, tri, batch, seq, chunk):
    nc = seq // chunk
    w = MLSTM_WIDTH
    nh, dh = MLSTM_HEADS, MLSTM_HEAD_DIM
    seqs = 1
    a3 = seg_a.reshape(batch, seq, SEG_A)
    b3 = seg_b.reshape(batch, seq, SEG_B)
    c3 = seg_c.reshape(batch, seq, SEG_C)

    def col_spec(off):
        return pl.BlockSpec((seqs, chunk, w), lambda b, c, o=off // w: (b, c, o))

    y = pl.pallas_call(
        _mlstm_kernel,
        grid=(batch // seqs, nc),
        in_specs=[
            pl.BlockSpec(memory_space=pltpu.SMEM),
            col_spec(A_QK), col_spec(A_QK + w), col_spec(A_V), col_spec(B_O),
            pl.BlockSpec((seqs, chunk, SEG_C), lambda b, c: (b, c, 0)),
            pl.BlockSpec((1, w), lambda b, c: (0, 0)),
            pl.BlockSpec((chunk, chunk), lambda b, c: (0, 0)),
        ],
        out_specs=pl.BlockSpec((seqs, chunk, w), lambda b, c: (b, c, 0)),
        out_shape=jax.ShapeDtypeStruct((batch, seq, w), BF16),
        scratch_shapes=[
            pltpu.VMEM((seqs, nh, dh, dh), F32),
            pltpu.VMEM((seqs, 8, dh), F32),
            pltpu.VMEM((seqs, 8, LANE), F32),
        ],
        compiler_params=pltpu.CompilerParams(
            dimension_semantics=("parallel", "arbitrary"), vmem_limit_bytes=VMEM_LIMIT),
        name="mlstm",
    )(bias, a3, a3, a3, b3, c3, norm_g, tri)
    return y.reshape(batch * seq, w)


def _outproj_kernel(x_ref, ya_ref, ym_ref, wa_ref, wm_ref, o_ref):
    o_ref[...] = x_ref[...] + _dot(ya_ref[...], wa_ref[...]) + _dot(ym_ref[...], wm_ref[...])


def _outproj(x2, ya, ym, w, tm):
    n, d = x2.shape
    assert ya.shape[1] == ym.shape[1] == d // 2
    return pl.pallas_call(
        _outproj_kernel,
        grid=(n // tm,),
        in_specs=[
            pl.BlockSpec((tm, d), lambda i: (i, 0)),
            pl.BlockSpec((tm, d // 2), lambda i: (i, 0)),
            pl.BlockSpec((tm, d // 2), lambda i: (i, 0)),
            pl.BlockSpec((d // 2, d), lambda i: (0, 0)),
            pl.BlockSpec((d // 2, d), lambda i: (1, 0)),
        ],
        out_specs=pl.BlockSpec((tm, d), lambda i: (i, 0)),
        out_shape=jax.ShapeDtypeStruct((n, d), F32),
        compiler_params=pltpu.CompilerParams(
            dimension_semantics=("parallel",), vmem_limit_bytes=VMEM_LIMIT),
        name="outproj",
    )(x2, ya, ym, w, w)


def _mlp_kernel(x_ref, g_ref, w1_ref, w2_ref, gf_ref, o_ref, h_ref):
    f = pl.program_id(1)

    @pl.when(f == 0)
    def _():
        x = x_ref[...]
        r = lax.rsqrt(jnp.mean(x * x, axis=-1, keepdims=True) + EPS)
        h_ref[...] = (x * r * g_ref[...]).astype(BF16)
        o_ref[...] = x

    u = jnp.maximum(_dot(h_ref[...], w1_ref[...]), 0.0)
    o_ref[...] += _dot((u * u).astype(BF16), w2_ref[...])

    @pl.when(f == pl.num_programs(1) - 1)
    def _():
        x2 = o_ref[...]
        r = lax.rsqrt(jnp.mean(x2 * x2, axis=-1, keepdims=True) + EPS)
        o_ref[...] = x2 * r * gf_ref[...]


def _mlp(x1, g, w1, w2, gf, tm, tf):
    n, d = x1.shape
    dff = w1.shape[1]
    return pl.pallas_call(
        _mlp_kernel,
        grid=(n // tm, dff // tf),
        in_specs=[
            pl.BlockSpec((tm, d), lambda i, f: (i, 0)),
            pl.BlockSpec((1, d), lambda i, f: (0, 0)),
            pl.BlockSpec((d, tf), lambda i, f: (0, f)),
            pl.BlockSpec((tf, d), lambda i, f: (f, 0)),
            pl.BlockSpec((1, d), lambda i, f: (0, 0)),
        ],
        out_specs=pl.BlockSpec((tm, d), lambda i, f: (i, 0)),
        out_shape=jax.ShapeDtypeStruct((n, d), F32),
        scratch_shapes=[pltpu.VMEM((tm, d), BF16)],
        compiler_params=pltpu.CompilerParams(
            dimension_semantics=("parallel", "arbitrary"), vmem_limit_bytes=VMEM_LIMIT),
        name="mlp",
    )(x1, g, w1, w2, gf)


def _row_tile(n, want):
    t = want
    while n % t:
        t //= 2
    return t


def _layer(x2, batch, seq, norm_mix_g, w_in, w_cmp_k1, w_cmp_k2, pos_cmp_k, w_cmp_v1, w_cmp_v2,
           pos_cmp_v, conv_w, conv_b, b_igate, b_fgate, mlstm_norm_g, w_out, norm_mlp_g,
           w_mlp_in, w_mlp_out):
    n, d = x2.shape
    assert seq % Q_BLOCK == 0 and seq >= WINDOW + Q_BLOCK and seq // SEL_BLOCK <= SEL_LANES
    nh = MLSTM_HEADS

    c_gate = NSA_WIDTH + 6 * NSA_KV_WIDTH
    c_qk = c_gate + NSA_HEADS * N_BRANCH
    c_v = c_qk + 2 * MLSTM_WIDTH
    c_o = c_v + MLSTM_WIDTH
    c_i = c_o + MLSTM_WIDTH
    c_f = c_i + nh
    w16 = w_in.astype(BF16)
    w_ab = jnp.concatenate(
        [w16[:, c_v:c_o], w16[:, c_qk:c_v], w16[:, :c_gate],
         w16[:, c_o:c_i]], axis=1)
    per_g = NSA_REP * N_BRANCH
    gate_cols = []
    for g in range(NSA_KV_GROUPS):
        gate_cols += [w16[:, c_gate + g * per_g:c_gate + (g + 1) * per_g],
                      jnp.zeros((d, C_GATE_STRIDE - per_g), BF16)]
    w_c = jnp.concatenate(gate_cols + [w16[:, c_i:c_f + nh],
                                       jnp.zeros((d, SEG_C - C_IF - 2 * nh), BF16)], axis=1)
    scale = jnp.concatenate([jnp.ones((1, A_QK + MLSTM_WIDTH), F32),
                             jnp.full((1, MLSTM_WIDTH), MLSTM_HEAD_DIM ** -0.5, F32),
                             jnp.full((1, NSA_WIDTH), NSA_HEAD_DIM ** -0.5, F32),
                             jnp.ones((1, SEG_A - A_KV + SEG_B), F32)], axis=1)
    conv_pad = ((0, 0), (A_QK, SEG_A - A_Q + SEG_B))
    cw_ab = jnp.pad(conv_w, conv_pad)
    cb_ab = jnp.pad(conv_b.reshape(1, -1), conv_pad)
    g_mix = norm_mix_g.reshape(1, d)

    tm = _row_tile(seq, ROW_TILE)
    seg_a, seg_b, seg_c = _inproj(x2, g_mix, w_ab, w_c, scale, cw_ab, cb_ab, tm, COL_TILE,
                                  (A_QK, A_Q), seq)

    n_sub = seq // CMP_STRIDE
    dh = NSA_HEAD_DIM
    w1s = jnp.stack([w_cmp_k1, w_cmp_v1]).reshape(2, 2, CMP_STRIDE, dh, dh)
    w1r = jnp.concatenate([w1s[:, 0], w1s[:, 1]], axis=-1).astype(BF16)
    w2s = jnp.stack([w_cmp_k2, w_cmp_v2]).astype(BF16)
    poss = jnp.stack([pos_cmp_k, pos_cmp_v]).reshape(2, 2, CMP_STRIDE, dh).transpose(0, 2, 1, 3)
    posr = jnp.pad(poss, ((0, 0), (0, 0), (0, 6), (0, 0))).astype(BF16)
    kvc, kvct = _compress(seg_a, w1r, w2s, posr, batch, seq)

    cmp_start = np.arange(n_sub) * CMP_STRIDE
    sel_start = np.arange(SEL_LANES) * SEL_BLOCK
    ovt = ((cmp_start[None, :] < sel_start[:, None] + SEL_BLOCK)
           & (cmp_start[None, :] + CMP_BLOCK - 1 >= sel_start[:, None])
           & (np.arange(n_sub)[None, :] < n_sub - CMP_BLOCK // CMP_STRIDE + 1))
    ovt = jnp.asarray(ovt, BF16)
    slopes = jnp.exp2(-8.0 * jnp.arange(1, NSA_HEADS + 1, dtype=F32) / NSA_HEADS)
    ocmp, pen, tiles = _nsa_cmp(seg_a, kvc, kvct, ovt, slopes, batch, seq)
    onehot, wext, dbias, ubias, wbias = _nsa_tables(slopes, seq)
    y_a = _nsa_attn(seg_a, seg_c, pen, ocmp, tiles, onehot, wext, dbias, ubias, wbias,
                    slopes, batch, seq)

    chunk = 256 if seq % 256 == 0 else 128
    bias = jnp.concatenate([b_igate, b_fgate]).astype(F32)
    tri = jnp.asarray(np.tril(np.ones((chunk, chunk), np.float32)), BF16)
    y_m = _mlstm(seg_a, seg_b, seg_c, bias, mlstm_norm_g.reshape(1, -1), tri, batch, seq, chunk)

    x1 = _outproj(x2, y_a, y_m, w_out.astype(BF16), _row_tile(n, 512))
    return x1, (norm_mlp_g.reshape(1, d), w_mlp_in.astype(BF16), w_mlp_out.astype(BF16))


def kernel(x, norm_mix_g, w_in, w_cmp_k1, w_cmp_k2, pos_cmp_k, w_cmp_v1, w_cmp_v2, pos_cmp_v, conv_w, conv_b, b_igate, b_fgate, mlstm_norm_g, w_out, norm_mlp_g, w_mlp_in, w_mlp_out, norm_f_g):
    batch, seq, d = x.shape
    depth = w_in.shape[0]
    assert depth == 1, "the final RMSNorm is fused into the last layer's channel mixer"
    x2 = x.reshape(batch * seq, d)
    tm = _row_tile(batch * seq, ROW_TILE)
    for l in range(depth):
        x1, (g_mlp, w1, w2) = _layer(
            x2, batch, seq, norm_mix_g[l], w_in[l], w_cmp_k1[l], w_cmp_k2[l], pos_cmp_k[l],
            w_cmp_v1[l], w_cmp_v2[l], pos_cmp_v[l], conv_w[l], conv_b[l], b_igate[l], b_fgate[l],
            mlstm_norm_g[l], w_out[l], norm_mlp_g[l], w_mlp_in[l], w_mlp_out[l])
        x2 = _mlp(x1, g_mlp, w1, w2, norm_f_g.reshape(1, d), tm, 512)
    return x2.reshape(batch, seq, d)
```

```python
import functools

import numpy as np
import jax
import jax.numpy as jnp
from jax import lax
from jax.experimental import pallas as pl
from jax.experimental.pallas import tpu as pltpu

F32 = jnp.float32
BF16 = jnp.bfloat16

EPS = 1e-6
NEG = -1e30
FORCE_BONUS = 1e4
PICKED = -3e38
MASK_BIG = 1e30

D_MODEL = 2048
NSA_HEAD_DIM = 128
NSA_WIDTH = D_MODEL // 2
NSA_HEADS = NSA_WIDTH // NSA_HEAD_DIM
NSA_REP = 4
NSA_KV_GROUPS = NSA_HEADS // NSA_REP
NSA_KV_WIDTH = NSA_KV_GROUPS * NSA_HEAD_DIM
CMP_BLOCK = 32
CMP_STRIDE = 16
SEL_BLOCK = 64
SEL_TOPK = 16
WINDOW = 512
Q_BLOCK = 128
N_BRANCH = 3
MLSTM_HEAD_DIM = 256
MLSTM_WIDTH = D_MODEL - NSA_WIDTH
MLSTM_HEADS = MLSTM_WIDTH // MLSTM_HEAD_DIM
CONV_WIDTH = 4

LANE = 128
SEL_LANES = 128
KEY_TILE = 128
SEL_TILES_PER_STEP = 2
WIN_TILES = WINDOW // KEY_TILE
SUM_ROWS = 16
VMEM_LIMIT = 56 * 1024 * 1024
ROW_TILE = 1024

A_V, A_QK = 0, MLSTM_WIDTH
A_Q = A_QK + 2 * MLSTM_WIDTH
A_KV = A_Q + NSA_WIDTH
SEG_A = A_KV + 6 * NSA_KV_WIDTH
B_O = 0
SEG_B = MLSTM_WIDTH
C_GATE_STRIDE = 16
C_IF = NSA_KV_GROUPS * C_GATE_STRIDE
SEG_C = LANE
COL_TILE = 512

WCOL_PAD, WCOL_HI, WCOL_LO, WCOL_QHI, WCOL_QLO = 0, 1, 2, 3, 4


def _dot(a, b):
    return jnp.dot(a, b, preferred_element_type=F32)


def _dot_nt(a, b):
    return lax.dot_general(a, b, (((1,), (1,)), ((), ())), preferred_element_type=F32)


def _sigmoid(x):
    return 1.0 / (1.0 + jnp.exp(-x))


def _shifted(x, tail, s):
    xs = pltpu.roll(x, s, axis=0)
    ts = pltpu.roll(tail, s, axis=0)
    row8 = lax.broadcasted_iota(jnp.int32, (8, 1), 0)
    head = jnp.where(row8 < s, ts, xs[:8])
    return jnp.concatenate([head, xs[8:]], axis=0)


def _conv_silu(x, tail, w, b):
    y = b + _shifted(x, tail, CONV_WIDTH - 1) * w[0:1]
    for i in range(1, CONV_WIDTH - 1):
        y = y + _shifted(x, tail, CONV_WIDTH - 1 - i) * w[i:i + 1]
    y = y + x * w[CONV_WIDTH - 1:CONV_WIDTH]
    return y * _sigmoid(y)


def _inproj_kernel(x_ref, g_ref, w_ref, wc_ref, cs_ref, cw_ref, cb_ref, oa_ref, ob_ref, oc_ref,
                   h_ref, halo_ref, *, nb, n_plain, tiles_per_seq):
    i = pl.program_id(0)
    j = pl.program_id(1)

    @pl.when(j == 0)
    def _():
        x = x_ref[...]
        r = lax.rsqrt(jnp.mean(x * x, axis=-1, keepdims=True) + EPS)
        h_ref[...] = (x * r * g_ref[...]).astype(BF16)

    is_conv = j > nb + n_plain

    @pl.when((j > nb) & jnp.logical_not(is_conv))
    def _():
        oa_ref[...] = (_dot(h_ref[...], w_ref[...]) * cs_ref[...]).astype(oa_ref.dtype)

    @pl.when(j < nb)
    def _():
        ob_ref[...] = (_dot(h_ref[...], w_ref[...]) * cs_ref[...]).astype(ob_ref.dtype)

    @pl.when(j == nb)
    def _():
        oc_ref[...] = _dot(h_ref[...], wc_ref[...])

    @pl.when(is_conv)
    def _():
        slot = j - (nb + 1 + n_plain)

        @pl.when(i % tiles_per_seq == 0)
        def _():
            halo_ref[slot] = jnp.zeros(halo_ref.shape[1:], F32)

        acc = _dot(h_ref[...], w_ref[...])
        tail = halo_ref[slot]
        halo_ref[slot] = acc[acc.shape[0] - 8:]
        y = _conv_silu(acc, tail, cw_ref[...], cb_ref[...])
        oa_ref[...] = (y * cs_ref[...]).astype(oa_ref.dtype)


def _inproj(x2, g, w, w_c, cs, cw, cb, tm, tn, conv_cols, seq):
    n, d = x2.shape
    na, nb = SEG_A // tn, SEG_B // tn
    conv_lo, conv_hi = conv_cols[0] // tn, conv_cols[1] // tn
    assert conv_cols[0] % tn == 0 and conv_cols[1] % tn == 0 and seq % tm == 0 and conv_hi <= na
    assert SEG_A % tn == 0 and SEG_B % tn == 0 and w.shape[1] == SEG_A + SEG_B
    n_conv = conv_hi - conv_lo
    n_plain = na - n_conv

    def a_tile(j):
        p = jnp.maximum(j - nb - 1, 0)
        plain = jnp.where(p < conv_lo, p, p + n_conv)
        return jnp.where(p < n_plain, plain, conv_lo + p - n_plain)

    def w_tile(i, j):
        return (0, jnp.where(j <= nb, na + jnp.minimum(j, nb - 1), a_tile(j)))

    return pl.pallas_call(
        functools.partial(_inproj_kernel, nb=nb, n_plain=n_plain, tiles_per_seq=seq // tm),
        grid=(n // tm, na + nb + 1),
        in_specs=[
            pl.BlockSpec((tm, d), lambda i, j: (i, 0)),
            pl.BlockSpec((1, d), lambda i, j: (0, 0)),
            pl.BlockSpec((d, tn), w_tile),
            pl.BlockSpec((d, SEG_C), lambda i, j: (0, 0)),
            pl.BlockSpec((1, tn), w_tile),
            pl.BlockSpec((CONV_WIDTH, tn), w_tile),
            pl.BlockSpec((1, tn), w_tile),
        ],
        out_specs=[pl.BlockSpec((tm, tn), lambda i, j: (i, a_tile(j))),
                   pl.BlockSpec((tm, tn), lambda i, j: (i, jnp.minimum(j, nb - 1))),
                   pl.BlockSpec((tm, SEG_C), lambda i, j: (i, 0))],
        out_shape=[jax.ShapeDtypeStruct((n, SEG_A), BF16), jax.ShapeDtypeStruct((n, SEG_B), F32),
                   jax.ShapeDtypeStruct((n, SEG_C), F32)],
        scratch_shapes=[pltpu.VMEM((tm, d), BF16),
                        pltpu.VMEM((conv_hi - conv_lo, 8, tn), F32)],
        compiler_params=pltpu.CompilerParams(
            dimension_semantics=("arbitrary", "arbitrary"), vmem_limit_bytes=VMEM_LIMIT),
        name="inproj",
    )(x2, g, w, w_c, cs, cw, cb)


def _compress_kernel(x_ref, w1_ref, w2_ref, pos_ref, o_ref, ot_ref, xf_ref):
    n_sub = o_ref.shape[2]
    dh = NSA_HEAD_DIM
    xf_ref[...] = x_ref[...].astype(F32)
    acc = jnp.zeros((n_sub, 2 * dh), F32)
    posw = jnp.zeros((1, dh), F32)
    for p in range(CMP_STRIDE):
        wp = w1_ref[0, p]
        acc = acc + _dot(xf_ref[pl.ds(p, n_sub, stride=CMP_STRIDE), :].astype(BF16), wp)
        pw = _dot(pos_ref[0, p], wp)
        posw = posw + pw[0:1, :dh] + pw[1:2, dh:]
    bot = pltpu.roll(acc[:, dh:], n_sub - 1, axis=0)
    pre = acc[:, :dh] + bot + posw
    hid = pre * _sigmoid(pre)
    out = _dot(hid.astype(BF16), w2_ref[0])
    o_ref[0, 0] = out.astype(o_ref.dtype)
    ot_ref[0, 0] = out.T.astype(ot_ref.dtype)


def _compress(seg_a, w1r, w2s, posr, batch, seq):
    g = NSA_KV_GROUPS
    c = 2 * g
    dh = NSA_HEAD_DIM
    n_sub = seq // CMP_STRIDE
    return pl.pallas_call(
        _compress_kernel,
        grid=(batch, c),
        in_specs=[
            pl.BlockSpec((seq, dh), lambda i, j: (i, A_KV // dh + j)),
            pl.BlockSpec((1, CMP_STRIDE, dh, 2 * dh), lambda i, j: (j // g, 0, 0, 0)),
            pl.BlockSpec((1, dh, dh), lambda i, j: (j // g, 0, 0)),
            pl.BlockSpec((1, CMP_STRIDE, 8, dh), lambda i, j: (j // g, 0, 0, 0)),
        ],
        out_specs=[pl.BlockSpec((1, 1, n_sub, dh), lambda i, j: (i, j, 0, 0)),
                   pl.BlockSpec((1, 1, dh, n_sub), lambda i, j: (i, j, 0, 0))],
        out_shape=[jax.ShapeDtypeStruct((batch, c, n_sub, dh), BF16),
                   jax.ShapeDtypeStruct((batch, c, dh, n_sub), BF16)],
        scratch_shapes=[pltpu.VMEM((seq, dh), F32)],
        compiler_params=pltpu.CompilerParams(
            dimension_semantics=("parallel", "parallel"), vmem_limit_bytes=VMEM_LIMIT),
        name="compress",
    )(seg_a, w1r, w2s, posr)


def _stack_heads(q_all):
    dh = NSA_HEAD_DIM
    return jnp.concatenate([q_all[:, r * dh:(r + 1) * dh] for r in range(NSA_REP)], axis=0)


def _tile_list(hit_col, tile0, pad_tile, mats_ref):
    n = SEL_LANES
    lane = lax.broadcasted_iota(jnp.int32, (1, n), 1)
    blk = lax.broadcasted_iota(jnp.int32, (n, n), 0)
    slot = lax.broadcasted_iota(jnp.int32, (n, n), 1).astype(F32)
    hit = jnp.broadcast_to(hit_col, (n, n))
    both = jnp.maximum(hit, pltpu.roll(hit, n - 1, axis=0))
    act = jnp.where((both > 0.0) & (blk % 2 == 0) & (blk // 2 < tile0), 1.0, 0.0)
    rank = _dot(mats_ref[0], act.astype(BF16))
    cnt = jnp.sum(act, axis=0, keepdims=True)
    place = jnp.where((act > 0.0) & (rank == slot), 1.0, 0.0)
    tiles = _dot(mats_ref[1, 0:8], place.astype(BF16))[0:1]
    out = jnp.where(lane.astype(F32) < cnt, tiles, float(pad_tile))
    return jnp.where(lane == n - 1, cnt, out).astype(jnp.int32)


def _nsa_cmp_kernel(slopes_ref, q_ref, kc_ref, vct_ref, ovt_ref, mats_ref, ocmp_ref, pen_ref, tiles_ref, *,
                    seq, sub_blocks):
    step = pl.program_id(2)
    nq_step = sub_blocks * Q_BLOCK
    chunk = min(SEL_LANES, kc_ref.shape[2])
    n_chunks = kc_ref.shape[2] // chunk
    need = ((step + 1) * nq_step - CMP_BLOCK) // CMP_STRIDE + 1
    n_need = (need + chunk - 1) // chunk
    for v in range(1, n_chunks + 1):
        cond = (n_need == v) if v < n_chunks else (n_need >= v)
        if v == 1:
            cond = n_need <= 1

        @pl.when(cond)
        def _(v=v):
            hit = None
            for sub in range(sub_blocks):
                rows = slice(sub * Q_BLOCK, (sub + 1) * Q_BLOCK)
                h = _nsa_cmp_block(slopes_ref, q_ref[rows, :], kc_ref, vct_ref, ovt_ref,
                                   ocmp_ref.at[0, 0, sub], pen_ref.at[0, 0, rows],
                                   step * sub_blocks + sub, seq, v * chunk,
                                   past_first_block=(sub >= 1 or v >= 2))
                hit = h if hit is None else jnp.maximum(hit, h)
            tiles_ref[0, 0, 0] = _tile_list(hit, step * sub_blocks, seq // KEY_TILE, mats_ref)


def _nsa_cmp_block(slopes_ref, q_all, kc_ref, vct_ref, ovt_ref, ocmp_ref, pen_ref, qb, seq, n_use,
                   past_first_block):
    g = pl.program_id(1)
    nq = Q_BLOCK
    n_cpad = n_use
    n_cmp = seq // CMP_STRIDE - CMP_BLOCK // CMP_STRIDE + 1
    t0 = qb * nq

    s_t = _dot_nt(kc_ref[0, 0, :n_use, :], _stack_heads(q_all))
    n_s = lax.broadcasted_iota(jnp.int32, (n_cpad, 1), 0)
    q_l = lax.broadcasted_iota(jnp.int32, (1, nq), 1)
    dist = (t0 - (CMP_BLOCK - 1)) + q_l - n_s * CMP_STRIDE
    valid = (dist >= 0) & (n_s < n_cmp)
    dist_f = dist.astype(F32)
    probs = []
    p_sum = jnp.zeros((n_cpad, nq), F32)
    for r in range(NSA_REP):
        slope = slopes_ref[g * NSA_REP + r]
        s = jnp.where(valid, s_t[:, r * nq:(r + 1) * nq] - slope * dist_f, NEG)
        m = jnp.max(s, axis=0, keepdims=True)
        e = jnp.exp(s - m)
        inv = jnp.where(m > 0.5 * NEG, 1.0 / jnp.sum(e, axis=0, keepdims=True), 0.0)
        p = e * inv
        probs.append(p.astype(BF16))
        p_sum = p_sum + p
    ocmp_ref[...] = _dot(vct_ref[0, 0, :, :n_use], jnp.concatenate(probs, axis=1))

    p_hi = p_sum.astype(BF16)
    p_lo = (p_sum - p_hi.astype(F32)).astype(BF16)
    ovt = ovt_ref[:, :n_use]
    imp = _dot(ovt, p_hi) + _dot(ovt, p_lo)
    j_i = lax.broadcasted_iota(jnp.int32, (SEL_LANES, 1), 0)
    t_l = t0 + q_l
    cur = t_l // SEL_BLOCK
    forced = (j_i == 0) | (j_i == cur) | (j_i == cur - 1)
    causal_blk = j_i * SEL_BLOCK <= t_l
    val = jnp.where(causal_blk, jnp.where(forced, imp + FORCE_BONUS, imp), NEG)
    j_f = j_i.astype(F32)
    sel_t = jnp.zeros((SEL_LANES, nq), F32)
    rounds = min(SEL_TOPK, seq // SEL_BLOCK)
    if past_first_block and rounds >= 3:
        sel_t = jnp.where(forced, 1.0, sel_t)
        val = jnp.where(forced, PICKED, val)
        rounds -= 3
    for _ in range(rounds):
        mx = jnp.max(val, axis=0, keepdims=True)
        first = jnp.min(jnp.where(val == mx, j_f, float(SEL_LANES)), axis=0, keepdims=True)
        pick = j_f == first
        sel_t = jnp.where(pick, 1.0, sel_t)
        val = jnp.where(pick, PICKED, val)
    sel = sel_t.T
    pen_ref[...] = ((sel - 1.0) * MASK_BIG).astype(pen_ref.dtype)
    return jnp.max(sel_t, axis=1, keepdims=True)


def _nsa_cmp(seg_a, kvc, kvct, ovt, slopes, batch, seq):
    nqb = seq // Q_BLOCK
    gq = NSA_REP * NSA_HEAD_DIM
    g_ = NSA_KV_GROUPS
    n_cpad = kvc.shape[2]
    sub = _nsa_sub_blocks(seq)
    nstep = nqb // sub
    idx = np.arange(SEL_LANES)
    mats = jnp.asarray(np.stack([idx[:, None] > idx[None, :],
                                 np.broadcast_to(idx[None, :] // 2, (SEL_LANES, SEL_LANES))]), BF16)
    return pl.pallas_call(
        functools.partial(_nsa_cmp_kernel, seq=seq, sub_blocks=sub),
        grid=(batch, g_, nstep),
        in_specs=[
            pl.BlockSpec(memory_space=pltpu.SMEM),
            pl.BlockSpec((sub * Q_BLOCK, gq), lambda b, g, q: (b * nstep + q, A_Q // gq + g)),
            pl.BlockSpec((1, 1, n_cpad, NSA_HEAD_DIM), lambda b, g, q: (b, g, 0, 0)),
            pl.BlockSpec((1, 1, NSA_HEAD_DIM, n_cpad), lambda b, g, q: (b, g_ + g, 0, 0)),
            pl.BlockSpec((SEL_LANES, n_cpad), lambda b, g, q: (0, 0)),
            pl.BlockSpec((2, SEL_LANES, SEL_LANES), lambda b, g, q: (0, 0, 0)),
        ],
        out_specs=[
            pl.BlockSpec((1, 1, sub, NSA_HEAD_DIM, gq), lambda b, g, q: (b, g, q, 0, 0)),
            pl.BlockSpec((1, 1, sub * Q_BLOCK, SEL_LANES), lambda b, g, q: (b, g, q, 0)),
            pl.BlockSpec((1, 1, 1, 1, SEL_LANES), lambda b, g, q: (b, g, q, 0, 0)),
        ],
        out_shape=[
            jax.ShapeDtypeStruct((batch, g_, nqb, NSA_HEAD_DIM, gq), F32),
            jax.ShapeDtypeStruct((batch, g_, seq, SEL_LANES), BF16),
            jax.ShapeDtypeStruct((batch, g_, nstep, 1, SEL_LANES), jnp.int32),
        ],
        compiler_params=pltpu.CompilerParams(
            dimension_semantics=("parallel", "parallel", "parallel"), vmem_limit_bytes=VMEM_LIMIT),
        name="nsa_cmp",
    )(slopes, seg_a, kvc, kvct, ovt, mats)


def _nsa_attn_kernel(slopes_ref, tiles_ref, q_ref, gate_ref, pen_ref, ocmp_ref, ks_ref, vs_ref,
                     kw_ref, vw_ref, onehot_ref, wext_ref, dbias_ref, ubias_ref, wbias_ref, o_ref,
                     ksel_ref, vselt_ref, kwin_ref, vwint_ref,
                     qa_ref, qw_ref, m_ref, acc_ref, owin_ref, gt_ref, sa_ref, sb_ref,
                     *, seq, sub_blocks):
    g = pl.program_id(1)
    step = pl.program_id(2)
    dh = NSA_HEAD_DIM
    nq = sub_blocks * Q_BLOCK
    kt = KEY_TILE
    per = SEL_TILES_PER_STEP
    pad_tile = seq // kt
    t0 = step * nq
    tile0 = step * sub_blocks

    @pl.when(step == 0)
    def _():
        ksel_ref[0:seq, :dh] = ks_ref[...]
        ksel_ref[0:seq, dh:] = onehot_ref[...]
        ksel_ref[seq:, :dh] = jnp.zeros((kt, dh), BF16)
        ksel_ref[seq:, dh:] = jnp.ones((kt, SEL_LANES), BF16)
        lane2 = lax.broadcasted_iota(jnp.int32, (WINDOW, dh + LANE), 1)
        kwin_ref[0:WINDOW, :] = jnp.where(lane2 == dh + WCOL_PAD, 1.0, 0.0).astype(BF16)
        kwin_ref[WINDOW:, :dh] = kw_ref[...]
        kwin_ref[WINDOW:, dh:] = wext_ref[...]
        zero_tile = jnp.zeros((dh + SUM_ROWS, kt), BF16)
        ones_rows = jnp.ones((SUM_ROWS, kt), BF16)
        vselt_ref[pad_tile] = zero_tile
        for i in range(WIN_TILES):
            vwint_ref[i] = zero_tile

        def transpose_tile(t, carry):
            r0 = pl.multiple_of(t * kt, kt)
            vselt_ref[t, :dh] = vs_ref[pl.ds(r0, kt), :].astype(F32).T.astype(BF16)
            vselt_ref[t, dh:] = ones_rows
            vwint_ref[t + WIN_TILES, :dh] = vw_ref[pl.ds(r0, kt), :].astype(F32).T.astype(BF16)
            vwint_ref[t + WIN_TILES, dh:] = ones_rows
            return carry

        lax.fori_loop(0, seq // kt, transpose_tile, 0)

    slopes = [slopes_ref[g * NSA_REP + r] for r in range(NSA_REP)]
    row_q = lax.broadcasted_iota(jnp.int32, (nq, 1), 0)
    lane = lax.broadcasted_iota(jnp.int32, (1, LANE), 1)
    q_all = q_ref[...]

    pen = pen_ref[0, 0].astype(F32)
    blk_rel = ((lane - (t0 + row_q) // SEL_BLOCK) * SEL_BLOCK).astype(F32)
    for r in range(NSA_REP):
        qa_ref[r * nq:(r + 1) * nq, :dh] = q_all[:, r * dh:(r + 1) * dh]
        qa_ref[r * nq:(r + 1) * nq, dh:] = (pen + slopes[r] * blk_rel).astype(BF16)

    def listed(idx):
        return tiles_ref[0, 0, 0, 0, idx]

    cnt = listed(SEL_LANES - 1)

    def group_scores(it):
        keys = jnp.concatenate(
            [ksel_ref[pl.ds(pl.multiple_of(listed(it * per + i) * kt, kt), kt), :]
             for i in range(per)], axis=0)
        return _dot_nt(keys, qa_ref[...]) + ubias_ref[0]

    s = _dot_nt(ksel_ref[pl.ds(pl.multiple_of(t0, kt), nq), :], qa_ref[...]) + dbias_ref[0]
    m0 = jnp.max(s, axis=0, keepdims=True)
    p = jnp.exp(s - m0)
    m_ref[...] = m0
    vals = jnp.concatenate([vselt_ref[tile0 + a] for a in range(sub_blocks)], axis=1)
    acc_ref[...] = _dot(vals, p.astype(BF16))
    sa_ref[...] = group_scores(0)

    tp = t0 + WINDOW + row_q
    t_hi = (tp // SEL_BLOCK).astype(F32)
    t_lo = (tp % SEL_BLOCK).astype(F32)
    for r in range(NSA_REP):
        sl = slopes[r]
        ext = jnp.where(lane == WCOL_PAD, -MASK_BIG, 0.0)
        ext = jnp.where(lane == WCOL_HI, sl * SEL_BLOCK, ext)
        ext = jnp.where(lane == WCOL_LO, sl, ext)
        ext = jnp.where(lane == WCOL_QHI, -sl * SEL_BLOCK * t_hi, ext)
        ext = jnp.where(lane == WCOL_QLO, -sl * t_lo, ext)
        qw_ref[r * nq:(r + 1) * nq, :dh] = q_all[:, r * dh:(r + 1) * dh]
        qw_ref[r * nq:(r + 1) * nq, dh:] = ext.astype(BF16)
    wlen = WINDOW + nq
    s = _dot_nt(kwin_ref[pl.ds(pl.multiple_of(t0, kt), wlen), :], qw_ref[...]) + wbias_ref[...]
    e = jnp.exp(s - jnp.max(s, axis=0, keepdims=True))
    vwin = jnp.concatenate([vwint_ref[tile0 + i] for i in range(WIN_TILES + sub_blocks)], axis=1)
    ow = _dot(vwin, e.astype(BF16))
    owin_ref[...] = ow[:dh] / ow[dh:dh + 1]

    def absorb(s, grp):
        vals = jnp.concatenate([vselt_ref[listed(grp * per + i)] for i in range(per)], axis=1)
        m_old = m_ref[...]
        m_new = jnp.maximum(m_old, jnp.max(s, axis=0, keepdims=True))
        alpha = jnp.exp(m_old - m_new)
        p = jnp.exp(s - m_new)
        acc_ref[...] = alpha * acc_ref[...] + _dot(vals, p.astype(BF16))
        m_ref[...] = m_new

    def sel_body(it, carry):
        s = sa_ref[...]
        sb_ref[...] = group_scores(2 * it + 1)
        absorb(s, 2 * it)
        s = sb_ref[...]
        sa_ref[...] = group_scores(2 * it + 2)
        absorb(s, 2 * it + 1)
        return carry

    lax.fori_loop(0, (cnt + 2 * per - 1) // (2 * per), sel_body, 0)
    o_sel = acc_ref[:dh, :] / acc_ref[dh:dh + 1, :]
    o_win = owin_ref[...]

    gt_ref[...] = _sigmoid(gate_ref[...]).T
    gate_t = gt_ref[pl.ds(pl.multiple_of(g * C_GATE_STRIDE, C_GATE_STRIDE), C_GATE_STRIDE), :]
    for a in range(sub_blocks):
        o_cmp = ocmp_ref[0, 0, a]
        qs = slice(a * Q_BLOCK, (a + 1) * Q_BLOCK)
        for r in range(NSA_REP):
            cols = slice(r * nq + a * Q_BLOCK, r * nq + (a + 1) * Q_BLOCK)
            c0 = N_BRANCH * r
            out_t = (gate_t[c0:c0 + 1, qs] * o_cmp[:, r * Q_BLOCK:(r + 1) * Q_BLOCK]
                     + gate_t[c0 + 1:c0 + 2, qs] * o_sel[:, cols]
                     + gate_t[c0 + 2:c0 + 3, qs] * o_win[:, cols])
            o_ref[qs, r * dh:(r + 1) * dh] = out_t.T.astype(o_ref.dtype)


def _nsa_sub_blocks(seq):
    return 2 if (seq // Q_BLOCK) % 2 == 0 else 1


def _nsa_attn(seg_a, seg_c, pen, ocmp, tiles, onehot, wext, dbias, ubias, wbias, slopes, batch, seq):
    n = batch * seq
    sub = _nsa_sub_blocks(seq)
    nq = sub * Q_BLOCK
    nqb = seq // nq
    gq = NSA_REP * NSA_HEAD_DIM
    g_ = NSA_KV_GROUPS
    dh, kt = NSA_HEAD_DIM, KEY_TILE
    dk = dh + SEL_LANES
    rq = NSA_REP * nq
    n_t = seq // kt

    def kv_spec(kind):
        return pl.BlockSpec((seq, dh), lambda b, g, q, k=kind: (b, A_KV // dh + k * g_ + g))

    def const_spec(arr):
        return pl.BlockSpec(arr.shape, lambda b, g, q, nd=arr.ndim: (0,) * nd)

    return pl.pallas_call(
        functools.partial(_nsa_attn_kernel, seq=seq, sub_blocks=sub),
        grid=(batch, g_, nqb),
        in_specs=[
            pl.BlockSpec(memory_space=pltpu.SMEM),
            pl.BlockSpec((1, 1, 1, 1, SEL_LANES), lambda b, g, q: (b, g, q, 0, 0),
                         memory_space=pltpu.SMEM),
            pl.BlockSpec((nq, gq), lambda b, g, q: (b * nqb + q, A_Q // gq + g)),
            pl.BlockSpec((nq, SEG_C), lambda b, g, q: (b * nqb + q, 0)),
            pl.BlockSpec((1, 1, nq, SEL_LANES), lambda b, g, q: (b, g, q, 0)),
            pl.BlockSpec((1, 1, sub, NSA_HEAD_DIM, gq), lambda b, g, q: (b, g, q, 0, 0)),
            kv_spec(2), kv_spec(3), kv_spec(4), kv_spec(5),
            const_spec(onehot), const_spec(wext),
            pl.BlockSpec((1,) + dbias.shape[1:], lambda b, g, q: (g, 0, 0)),
            pl.BlockSpec((1,) + ubias.shape[1:], lambda b, g, q: (g, 0, 0)),
            const_spec(wbias),
        ],
        out_specs=pl.BlockSpec((nq, gq), lambda b, g, q: (b * nqb + q, g)),
        out_shape=jax.ShapeDtypeStruct((n, NSA_WIDTH), BF16),
        scratch_shapes=[
            pltpu.VMEM((seq + kt, dk), BF16),
            pltpu.VMEM((n_t + 1, dh + SUM_ROWS, kt), BF16),
            pltpu.VMEM((seq + WINDOW, dh + LANE), BF16),
            pltpu.VMEM((n_t + WIN_TILES, dh + SUM_ROWS, kt), BF16),
            pltpu.VMEM((rq, dk), BF16),
            pltpu.VMEM((rq, dk), BF16),
            pltpu.VMEM((1, rq), F32),
            pltpu.VMEM((dh + SUM_ROWS, rq), F32),
            pltpu.VMEM((dh, rq), F32),
            pltpu.VMEM((SEG_C, nq), F32),
            pltpu.VMEM((SEL_TILES_PER_STEP * KEY_TILE, rq), F32),
            pltpu.VMEM((SEL_TILES_PER_STEP * KEY_TILE, rq), F32),
        ],
        compiler_params=pltpu.CompilerParams(
            dimension_semantics=("parallel", "parallel", "arbitrary"), vmem_limit_bytes=VMEM_LIMIT),
        name="nsa_attn",
    )(slopes, tiles, seg_a, seg_c, pen, ocmp, seg_a, seg_a, seg_a, seg_a, onehot, wext,
      dbias, ubias, wbias)


def _nsa_tables(slopes, seq):
    g_, kt = NSA_KV_GROUPS, KEY_TILE
    pos = np.arange(seq)
    onehot = jnp.asarray(pos[:, None] // SEL_BLOCK == np.arange(SEL_LANES)[None, :], BF16)
    ext = np.zeros((seq, LANE), np.float32)
    ext[:, WCOL_HI] = (pos + WINDOW) // SEL_BLOCK
    ext[:, WCOL_LO] = (pos + WINDOW) % SEL_BLOCK
    ext[:, WCOL_QHI] = 1.0
    ext[:, WCOL_QLO] = 1.0
    wext = jnp.asarray(ext, BF16)

    nq = _nsa_sub_blocks(seq) * Q_BLOCK

    def alibi_in_block(rows):
        u = jnp.asarray((np.arange(rows) % SEL_BLOCK).astype(np.float32))[None, :, None, None]
        t = jnp.broadcast_to(slopes.reshape(g_, 1, NSA_REP, 1) * u, (g_, rows, NSA_REP, nq))
        return t.reshape(g_, rows, NSA_REP * nq)

    ubias = alibi_in_block(SEL_TILES_PER_STEP * kt)
    kq = np.arange(nq)[:, None] <= np.arange(nq)[None, :]
    causal = np.tile(np.where(kq, 0.0, NEG).astype(np.float32), (1, NSA_REP))
    dbias = alibi_in_block(nq) + jnp.asarray(causal)[None]
    ki = np.arange(WINDOW + nq)[:, None]
    qi = np.arange(nq)[None, :]
    band = np.where((ki > qi) & (ki <= qi + WINDOW), 0.0, NEG).astype(np.float32)
    wbias = jnp.asarray(np.tile(band, (1, NSA_REP)))
    return onehot, wext, dbias, ubias, wbias


def _log_sigmoid(x):
    return jnp.minimum(x, 0.0) - jnp.log(1.0 + jnp.exp(-jnp.abs(x)))


def _split3(x):
    hi = x.astype(BF16)
    r1 = x - hi.astype(F32)
    mid = r1.astype(BF16)
    lo = (r1 - mid.astype(F32)).astype(BF16)
    return hi, mid, lo


def _mlstm_kernel(bias_ref, q_ref, k_ref, v_ref, o_ref, ifc_ref, ng_ref,
                  tri_ref, y_ref, c_ref, n_ref, m_ref):
    ch = pl.program_id(1)

    @pl.when(ch == 0)
    def _():
        c_ref[...] = jnp.zeros(c_ref.shape, F32)
        n_ref[...] = jnp.zeros(n_ref.shape, F32)
        m_ref[...] = jnp.zeros(m_ref.shape, F32)

    for sq in range(q_ref.shape[0]):
        _mlstm_chunk(bias_ref, q_ref.at[sq], k_ref.at[sq], v_ref.at[sq], o_ref.at[sq], ifc_ref.at[sq],
                     ng_ref, tri_ref, y_ref.at[sq], c_ref.at[sq], n_ref.at[sq], m_ref.at[sq])


def _mlstm_chunk(bias_ref, q_ref, k_ref, v_ref, o_ref, ifc_ref, ng_ref, tri_ref, y_ref,
                 c_ref, n_ref, m_ref):
    nh, dh = MLSTM_HEADS, MLSTM_HEAD_DIM
    L = q_ref.shape[0]
    tri = tri_ref[...]
    lane8 = lax.broadcasted_iota(jnp.int32, (1, LANE), 1)
    bias_c = jnp.zeros((1, LANE), F32)
    for h in range(nh):
        bias_c = jnp.where(lane8 == C_IF + h, bias_ref[h], bias_c)
        bias_c = jnp.where(lane8 == C_IF + nh + h, bias_ref[nh + h], bias_c)
    pre_c = ifc_ref[...] + bias_c
    cum_c = sum(_dot(tri, part) for part in _split3(_log_sigmoid(pre_c)))
    pre_r = pre_c.T[C_IF:C_IF + 2 * nh]
    cum_r = sum(_dot_nt(part, tri) for part in _split3(_log_sigmoid(pre_r)))

    rr = lax.broadcasted_iota(jnp.int32, (L, 1), 0)
    cc = lax.broadcasted_iota(jnp.int32, (1, L), 1)
    causal = cc <= rr

    for h in range(nh):
        cols = slice(h * dh, (h + 1) * dh)
        qb = q_ref[:, cols]
        kb = k_ref[:, cols]
        vh = v_ref[:, cols]
        qh = qb.astype(F32)
        kh = kb.astype(F32)
        b_c = cum_c[:, C_IF + nh + h:C_IF + nh + h + 1]
        li_c = pre_c[:, C_IF + h:C_IF + h + 1]
        b_r = cum_r[nh + h:nh + h + 1, :]
        li_r = pre_r[h:h + 1, :]
        m_prev = m_ref[h:h + 1, 0:1]

        dmat = jnp.where(causal, b_c - b_r + li_r, NEG)
        a = b_c + m_prev
        m_j = jnp.maximum(a, jnp.max(dmat, axis=1, keepdims=True))
        w_intra = jnp.exp(dmat - m_j)
        w_inter = jnp.exp(a - m_j)
        sc = _dot_nt(qb, kb) * w_intra
        c_old = c_ref[h]
        n_old = n_ref[h:h + 1, :]
        num = w_inter * _dot(qb, c_old.astype(BF16)) + _dot(sc.astype(BF16), vh)
        den = (w_inter * jnp.sum(qh * n_old, axis=1, keepdims=True)
               + jnp.sum(sc, axis=1, keepdims=True))
        hid = num / jnp.maximum(jnp.abs(den), jnp.exp(-m_j))

        g_tot = b_r[:, L - 1:L]
        lw_c = g_tot - b_c + li_c
        lw_r = g_tot - b_r + li_r
        m_new = jnp.maximum(g_tot + m_prev, jnp.max(lw_r, axis=1, keepdims=True))
        decay = jnp.exp(g_tot + m_prev - m_new)
        kw = jnp.exp(lw_c - m_new) * kh
        c_ref[h] = decay * c_old + _dot(kw.T.astype(BF16), vh)
        n_ref[h:h + 1, :] = decay * n_old + jnp.sum(kw, axis=0, keepdims=True)
        m_ref[h:h + 1, :] = jnp.broadcast_to(m_new, (1, LANE))

        hn = hid * lax.rsqrt(jnp.mean(hid * hid, axis=-1, keepdims=True) + EPS) * ng_ref[:, cols]
        y_ref[:, cols] = (_sigmoid(o_ref[:, cols]) * hn).astype(y_ref.dtype)


def _mlstm(seg_a, seg_b, seg_c, bias, norm_g, tri, batch, seq, chunk):
    nc = seq // chunk
    w = MLSTM_WIDTH
    nh, dh = MLSTM_HEADS, MLSTM_HEAD_DIM
    seqs = 1
    a3 = seg_a.reshape(batch, seq, SEG_A)
    b3 = seg_b.reshape(batch, seq, SEG_B)
    c3 = seg_c.reshape(batch, seq, SEG_C)

    def col_spec(off):
        return pl.BlockSpec((seqs, chunk, w), lambda b, c, o=off // w: (b, c, o))

    y = pl.pallas_call(
        _mlstm_kernel,
        grid=(batch // seqs, nc),
        in_specs=[
            pl.BlockSpec(memory_space=pltpu.SMEM),
            col_spec(A_QK), col_spec(A_QK + w), col_spec(A_V), col_spec(B_O),
            pl.BlockSpec((seqs, chunk, SEG_C), lambda b, c: (b, c, 0)),
            pl.BlockSpec((1, w), lambda b, c: (0, 0)),
            pl.BlockSpec((chunk, chunk), lambda b, c: (0, 0)),
        ],
        out_specs=pl.BlockSpec((seqs, chunk, w), lambda b, c: (b, c, 0)),
        out_shape=jax.ShapeDtypeStruct((batch, seq, w), BF16),
        scratch_shapes=[
            pltpu.VMEM((seqs, nh, dh, dh), F32),
            pltpu.VMEM((seqs, 8, dh), F32),
            pltpu.VMEM((seqs, 8, LANE), F32),
        ],
        compiler_params=pltpu.CompilerParams(
            dimension_semantics=("parallel", "arbitrary"), vmem_limit_bytes=VMEM_LIMIT),
        name="mlstm",
    )(bias, a3, a3, a3, b3, c3, norm_g, tri)
    return y.reshape(batch * seq, w)


def _outproj_kernel(x_ref, ya_ref, ym_ref, wa_ref, wm_ref, o_ref):
    o_ref[...] = x_ref[...] + _dot(ya_ref[...], wa_ref[...]) + _dot(ym_ref[...], wm_ref[...])


def _outproj(x2, ya, ym, w, tm):
    n, d = x2.shape
    assert ya.shape[1] == ym.shape[1] == d // 2
    return pl.pallas_call(
        _outproj_kernel,
        grid=(n // tm,),
        in_specs=[
            pl.BlockSpec((tm, d), lambda i: (i, 0)),
            pl.BlockSpec((tm, d // 2), lambda i: (i, 0)),
            pl.BlockSpec((tm, d // 2), lambda i: (i, 0)),
            pl.BlockSpec((d // 2, d), lambda i: (0, 0)),
            pl.BlockSpec((d // 2, d), lambda i: (1, 0)),
        ],
        out_specs=pl.BlockSpec((tm, d), lambda i: (i, 0)),
        out_shape=jax.ShapeDtypeStruct((n, d), F32),
        compiler_params=pltpu.CompilerParams(
            dimension_semantics=("parallel",), vmem_limit_bytes=VMEM_LIMIT),
        name="outproj",
    )(x2, ya, ym, w, w)


def _mlp_kernel(x_ref, g_ref, w1_ref, w2_ref, gf_ref, o_ref, h_ref):
    f = pl.program_id(1)

    @pl.when(f == 0)
    def _():
        x = x_ref[...]
        r = lax.rsqrt(jnp.mean(x * x, axis=-1, keepdims=True) + EPS)
        h_ref[...] = (x * r * g_ref[...]).astype(BF16)
        o_ref[...] = x

    u = jnp.maximum(_dot(h_ref[...], w1_ref[...]), 0.0)
    o_ref[...] += _dot((u * u).astype(BF16), w2_ref[...])

    @pl.when(f == pl.num_programs(1) - 1)
    def _():
        x2 = o_ref[...]
        r = lax.rsqrt(jnp.mean(x2 * x2, axis=-1, keepdims=True) + EPS)
        o_ref[...] = x2 * r * gf_ref[...]


def _mlp(x1, g, w1, w2, gf, tm, tf):
    n, d = x1.shape
    dff = w1.shape[1]
    return pl.pallas_call(
        _mlp_kernel,
        grid=(n // tm, dff // tf),
        in_specs=[
            pl.BlockSpec((tm, d), lambda i, f: (i, 0)),
            pl.BlockSpec((1, d), lambda i, f: (0, 0)),
            pl.BlockSpec((d, tf), lambda i, f: (0, f)),
            pl.BlockSpec((tf, d), lambda i, f: (f, 0)),
            pl.BlockSpec((1, d), lambda i, f: (0, 0)),
        ],
        out_specs=pl.BlockSpec((tm, d), lambda i, f: (i, 0)),
        out_shape=jax.ShapeDtypeStruct((n, d), F32),
        scratch_shapes=[pltpu.VMEM((tm, d), BF16)],
        compiler_params=pltpu.CompilerParams(
            dimension_semantics=("parallel", "arbitrary"), vmem_limit_bytes=VMEM_LIMIT),
        name="mlp",
    )(x1, g, w1, w2, gf)


def _row_tile(n, want):
    t = want
    while n % t:
        t //= 2
    return t


def _layer(x2, batch, seq, norm_mix_g, w_in, w_cmp_k1, w_cmp_k2, pos_cmp_k, w_cmp_v1, w_cmp_v2,
           pos_cmp_v, conv_w, conv_b, b_igate, b_fgate, mlstm_norm_g, w_out, norm_mlp_g,
           w_mlp_in, w_mlp_out):
    n, d = x2.shape
    assert seq % Q_BLOCK == 0 and seq >= WINDOW + Q_BLOCK and seq // SEL_BLOCK <= SEL_LANES
    nh = MLSTM_HEADS

    c_gate = NSA_WIDTH + 6 * NSA_KV_WIDTH
    c_qk = c_gate + NSA_HEADS * N_BRANCH
    c_v = c_qk + 2 * MLSTM_WIDTH
    c_o = c_v + MLSTM_WIDTH
    c_i = c_o + MLSTM_WIDTH
    c_f = c_i + nh
    def cols16(lo, hi):
        return w_in[:, lo:hi].astype(BF16)

    w_ab = jnp.concatenate(
        [cols16(c_v, c_o), cols16(c_qk, c_v), cols16(0, c_gate),
         cols16(c_o, c_i)], axis=1)
    per_g = NSA_REP * N_BRANCH
    gate_cols = []
    for g in range(NSA_KV_GROUPS):
        gate_cols += [cols16(c_gate + g * per_g, c_gate + (g + 1) * per_g),
                      jnp.zeros((d, C_GATE_STRIDE - per_g), BF16)]
    w_c = jnp.concatenate(gate_cols + [cols16(c_i, c_f + nh),
                                       jnp.zeros((d, SEG_C - C_IF - 2 * nh), BF16)], axis=1)
    scale = jnp.concatenate([jnp.ones((1, A_QK + MLSTM_WIDTH), F32),
                             jnp.full((1, MLSTM_WIDTH), MLSTM_HEAD_DIM ** -0.5, F32),
                             jnp.full((1, NSA_WIDTH), NSA_HEAD_DIM ** -0.5, F32),
                             jnp.ones((1, SEG_A - A_KV + SEG_B), F32)], axis=1)
    conv_pad = ((0, 0), (A_QK, SEG_A - A_Q + SEG_B))
    cw_ab = jnp.pad(conv_w, conv_pad)
    cb_ab = jnp.pad(conv_b.reshape(1, -1), conv_pad)
    g_mix = norm_mix_g.reshape(1, d)

    tm = _row_tile(seq, ROW_TILE)
    seg_a, seg_b, seg_c = _inproj(x2, g_mix, w_ab, w_c, scale, cw_ab, cb_ab, tm, COL_TILE,
                                  (A_QK, A_Q), seq)

    n_sub = seq // CMP_STRIDE
    dh = NSA_HEAD_DIM
    w1s = jnp.stack([w_cmp_k1, w_cmp_v1]).reshape(2, 2, CMP_STRIDE, dh, dh)
    w1r = jnp.concatenate([w1s[:, 0], w1s[:, 1]], axis=-1).astype(BF16)
    w2s = jnp.stack([w_cmp_k2, w_cmp_v2]).astype(BF16)
    poss = jnp.stack([pos_cmp_k, pos_cmp_v]).reshape(2, 2, CMP_STRIDE, dh).transpose(0, 2, 1, 3)
    posr = jnp.pad(poss, ((0, 0), (0, 0), (0, 6), (0, 0))).astype(BF16)
    kvc, kvct = _compress(seg_a, w1r, w2s, posr, batch, seq)

    cmp_start = np.arange(n_sub) * CMP_STRIDE
    sel_start = np.arange(SEL_LANES) * SEL_BLOCK
    ovt = ((cmp_start[None, :] < sel_start[:, None] + SEL_BLOCK)
           & (cmp_start[None, :] + CMP_BLOCK - 1 >= sel_start[:, None])
           & (np.arange(n_sub)[None, :] < n_sub - CMP_BLOCK // CMP_STRIDE + 1))
    ovt = jnp.asarray(ovt, BF16)
    slopes = jnp.exp2(-8.0 * jnp.arange(1, NSA_HEADS + 1, dtype=F32) / NSA_HEADS)
    ocmp, pen, tiles = _nsa_cmp(seg_a, kvc, kvct, ovt, slopes, batch, seq)
    onehot, wext, dbias, ubias, wbias = _nsa_tables(slopes, seq)
    y_a = _nsa_attn(seg_a, seg_c, pen, ocmp, tiles, onehot, wext, dbias, ubias, wbias,
                    slopes, batch, seq)

    chunk = 256 if seq % 256 == 0 else 128
    bias = jnp.concatenate([b_igate, b_fgate]).astype(F32)
    tri = jnp.asarray(np.tril(np.ones((chunk, chunk), np.float32)), BF16)
    y_m = _mlstm(seg_a, seg_b, seg_c, bias, mlstm_norm_g.reshape(1, -1), tri, batch, seq, chunk)

    x1 = _outproj(x2, y_a, y_m, w_out.astype(BF16), _row_tile(n, 512))
    return x1, (norm_mlp_g.reshape(1, d), w_mlp_in.astype(BF16), w_mlp_out.astype(BF16))


def kernel(x, norm_mix_g, w_in, w_cmp_k1, w_cmp_k2, pos_cmp_k, w_cmp_v1, w_cmp_v2, pos_cmp_v, conv_w, conv_b, b_igate, b_fgate, mlstm_norm_g, w_out, norm_mlp_g, w_mlp_in, w_mlp_out, norm_f_g):
    batch, seq, d = x.shape
    depth = w_in.shape[0]
    assert depth == 1, "the final RMSNorm is fused into the last layer's channel mixer"
    x2 = x.reshape(batch * seq, d)
    tm = _row_tile(batch * seq, ROW_TILE)
    for l in range(depth):
        x1, (g_mlp, w1, w2) = _layer(
            x2, batch, seq, norm_mix_g[l], w_in[l], w_cmp_k1[l], w_cmp_k2[l], pos_cmp_k[l],
            w_cmp_v1[l], w_cmp_v2[l], pos_cmp_v[l], conv_w[l], conv_b[l], b_igate[l], b_fgate[l],
            mlstm_norm_g[l], w_out[l], norm_mlp_g[l], w_mlp_in[l], w_mlp_out[l])
        x2 = _mlp(x1, g_mlp, w1, w2, norm_f_g.reshape(1, d), tm, 512)
    return x2.reshape(batch, seq, d)
```

```python
import functools

import numpy as np
import jax
import jax.numpy as jnp
from jax import lax
from jax.experimental import pallas as pl
from jax.experimental.pallas import tpu as pltpu

F32 = jnp.float32
BF16 = jnp.bfloat16

EPS = 1e-6
NEG = -1e30
FORCE_BONUS = 1e4
PICKED = -3e38
MASK_BIG = 1e30

D_MODEL = 2048
NSA_HEAD_DIM = 128
NSA_WIDTH = D_MODEL // 2
NSA_HEADS = NSA_WIDTH // NSA_HEAD_DIM
NSA_REP = 4
NSA_KV_GROUPS = NSA_HEADS // NSA_REP
NSA_KV_WIDTH = NSA_KV_GROUPS * NSA_HEAD_DIM
CMP_BLOCK = 32
CMP_STRIDE = 16
SEL_BLOCK = 64
SEL_TOPK = 16
WINDOW = 512
Q_BLOCK = 128
N_BRANCH = 3
MLSTM_HEAD_DIM = 256
MLSTM_WIDTH = D_MODEL - NSA_WIDTH
MLSTM_HEADS = MLSTM_WIDTH // MLSTM_HEAD_DIM
CONV_WIDTH = 4

LANE = 128
SEL_LANES = 128
KEY_TILE = 128
SEL_TILES_PER_STEP = 2
WIN_TILES = WINDOW // KEY_TILE
SUM_ROWS = 16
VMEM_LIMIT = 56 * 1024 * 1024
ROW_TILE = 1024

A_V, A_QK = 0, MLSTM_WIDTH
A_Q = A_QK + 2 * MLSTM_WIDTH
A_KV = A_Q + NSA_WIDTH
SEG_A = A_KV + 6 * NSA_KV_WIDTH
B_O = 0
SEG_B = MLSTM_WIDTH
C_GATE_STRIDE = 16
C_IF = NSA_KV_GROUPS * C_GATE_STRIDE
SEG_C = LANE
COL_TILE = 512

WCOL_PAD, WCOL_HI, WCOL_LO, WCOL_QHI, WCOL_QLO = 0, 1, 2, 3, 4


def _dot(a, b):
    return jnp.dot(a, b, preferred_element_type=F32)


def _dot_nt(a, b):
    return lax.dot_general(a, b, (((1,), (1,)), ((), ())), preferred_element_type=F32)


def _sigmoid(x):
    return 1.0 / (1.0 + jnp.exp(-x))


def _shifted(x, tail, s):
    xs = pltpu.roll(x, s, axis=0)
    ts = pltpu.roll(tail, s, axis=0)
    row8 = lax.broadcasted_iota(jnp.int32, (8, 1), 0)
    head = jnp.where(row8 < s, ts, xs[:8])
    return jnp.concatenate([head, xs[8:]], axis=0)


def _conv_silu(x, tail, w, b):
    y = b + _shifted(x, tail, CONV_WIDTH - 1) * w[0:1]
    for i in range(1, CONV_WIDTH - 1):
        y = y + _shifted(x, tail, CONV_WIDTH - 1 - i) * w[i:i + 1]
    y = y + x * w[CONV_WIDTH - 1:CONV_WIDTH]
    return y * _sigmoid(y)


def _inproj_kernel(x_ref, g_ref, w_ref, wc_ref, cs_ref, cw_ref, cb_ref, oa_ref, ob_ref, oc_ref,
                   h_ref, halo_ref, *, nb, n_plain, tiles_per_seq):
    i = pl.program_id(0)
    j = pl.program_id(1)

    @pl.when(j == 0)
    def _():
        x = x_ref[...]
        r = lax.rsqrt(jnp.mean(x * x, axis=-1, keepdims=True) + EPS)
        h_ref[...] = (x * r * g_ref[...]).astype(BF16)

    is_conv = j > nb + n_plain

    @pl.when((j > nb) & jnp.logical_not(is_conv))
    def _():
        oa_ref[...] = (_dot(h_ref[...], w_ref[...]) * cs_ref[...]).astype(oa_ref.dtype)

    @pl.when(j < nb)
    def _():
        ob_ref[...] = (_dot(h_ref[...], w_ref[...]) * cs_ref[...]).astype(ob_ref.dtype)

    @pl.when(j == nb)
    def _():
        oc_ref[...] = _dot(h_ref[...], wc_ref[...])

    @pl.when(is_conv)
    def _():
        slot = j - (nb + 1 + n_plain)

        @pl.when(i % tiles_per_seq == 0)
        def _():
            halo_ref[slot] = jnp.zeros(halo_ref.shape[1:], F32)

        acc = _dot(h_ref[...], w_ref[...])
        tail = halo_ref[slot]
        halo_ref[slot] = acc[acc.shape[0] - 8:]
        y = _conv_silu(acc, tail, cw_ref[...], cb_ref[...])
        oa_ref[...] = (y * cs_ref[...]).astype(oa_ref.dtype)


def _inproj(x2, g, w, w_c, cs, cw, cb, tm, tn, conv_cols, seq):
    n, d = x2.shape
    na, nb = SEG_A // tn, SEG_B // tn
    conv_lo, conv_hi = conv_cols[0] // tn, conv_cols[1] // tn
    assert conv_cols[0] % tn == 0 and conv_cols[1] % tn == 0 and seq % tm == 0 and conv_hi <= na
    assert SEG_A % tn == 0 and SEG_B % tn == 0 and w.shape[1] == SEG_A + SEG_B
    n_conv = conv_hi - conv_lo
    n_plain = na - n_conv

    def a_tile(j):
        p = jnp.maximum(j - nb - 1, 0)
        plain = jnp.where(p < conv_lo, p, p + n_conv)
        return jnp.where(p < n_plain, plain, conv_lo + p - n_plain)

    def w_tile(i, j):
        return (0, jnp.where(j <= nb, na + jnp.minimum(j, nb - 1), a_tile(j)))

    return pl.pallas_call(
        functools.partial(_inproj_kernel, nb=nb, n_plain=n_plain, tiles_per_seq=seq // tm),
        grid=(n // tm, na + nb + 1),
        in_specs=[
            pl.BlockSpec((tm, d), lambda i, j: (i, 0)),
            pl.BlockSpec((1, d), lambda i, j: (0, 0)),
            pl.BlockSpec((d, tn), w_tile),
            pl.BlockSpec((d, SEG_C), lambda i, j: (0, 0)),
            pl.BlockSpec((1, tn), w_tile),
            pl.BlockSpec((CONV_WIDTH, tn), w_tile),
            pl.BlockSpec((1, tn), w_tile),
        ],
        out_specs=[pl.BlockSpec((tm, tn), lambda i, j: (i, a_tile(j))),
                   pl.BlockSpec((tm, tn), lambda i, j: (i, jnp.minimum(j, nb - 1))),
                   pl.BlockSpec((tm, SEG_C), lambda i, j: (i, 0))],
        out_shape=[jax.ShapeDtypeStruct((n, SEG_A), BF16), jax.ShapeDtypeStruct((n, SEG_B), F32),
                   jax.ShapeDtypeStruct((n, SEG_C), F32)],
        scratch_shapes=[pltpu.VMEM((tm, d), BF16),
                        pltpu.VMEM((conv_hi - conv_lo, 8, tn), F32)],
        compiler_params=pltpu.CompilerParams(
            dimension_semantics=("arbitrary", "arbitrary"), vmem_limit_bytes=VMEM_LIMIT),
        name="inproj",
    )(x2, g, w, w_c, cs, cw, cb)


def _compress_kernel(x_ref, w1_ref, w2_ref, pos_ref, o_ref, ot_ref, xf_ref):
    n_sub = o_ref.shape[2]
    dh = NSA_HEAD_DIM
    xf_ref[...] = x_ref[...].astype(F32)
    acc = jnp.zeros((n_sub, 2 * dh), F32)
    posw = jnp.zeros((1, dh), F32)
    for p in range(CMP_STRIDE):
        wp = w1_ref[0, p]
        acc = acc + _dot(xf_ref[pl.ds(p, n_sub, stride=CMP_STRIDE), :].astype(BF16), wp)
        pw = _dot(pos_ref[0, p], wp)
        posw = posw + pw[0:1, :dh] + pw[1:2, dh:]
    bot = pltpu.roll(acc[:, dh:], n_sub - 1, axis=0)
    pre = acc[:, :dh] + bot + posw
    hid = pre * _sigmoid(pre)
    out = _dot(hid.astype(BF16), w2_ref[0])
    o_ref[0, 0] = out.astype(o_ref.dtype)
    ot_ref[0, 0] = out.T.astype(ot_ref.dtype)


def _compress(seg_a, w1r, w2s, posr, batch, seq):
    g = NSA_KV_GROUPS
    c = 2 * g
    dh = NSA_HEAD_DIM
    n_sub = seq // CMP_STRIDE
    return pl.pallas_call(
        _compress_kernel,
        grid=(batch, c),
        in_specs=[
            pl.BlockSpec((seq, dh), lambda i, j: (i, A_KV // dh + j)),
            pl.BlockSpec((1, CMP_STRIDE, dh, 2 * dh), lambda i, j: (j // g, 0, 0, 0)),
            pl.BlockSpec((1, dh, dh), lambda i, j: (j // g, 0, 0)),
            pl.BlockSpec((1, CMP_STRIDE, 8, dh), lambda i, j: (j // g, 0, 0, 0)),
        ],
        out_specs=[pl.BlockSpec((1, 1, n_sub, dh), lambda i, j: (i, j, 0, 0)),
                   pl.BlockSpec((1, 1, dh, n_sub), lambda i, j: (i, j, 0, 0))],
        out_shape=[jax.ShapeDtypeStruct((batch, c, n_sub, dh), BF16),
                   jax.ShapeDtypeStruct((batch, c, dh, n_sub), BF16)],
        scratch_shapes=[pltpu.VMEM((seq, dh), F32)],
        compiler_params=pltpu.CompilerParams(
            dimension_semantics=("parallel", "parallel"), vmem_limit_bytes=VMEM_LIMIT),
        name="compress",
    )(seg_a, w1r, w2s, posr)


def _stack_heads(q_all):
    dh = NSA_HEAD_DIM
    return jnp.concatenate([q_all[:, r * dh:(r + 1) * dh] for r in range(NSA_REP)], axis=0)


def _tile_list(hit_col, tile0, pad_tile, mats_ref):
    n = SEL_LANES
    lane = lax.broadcasted_iota(jnp.int32, (1, n), 1)
    blk = lax.broadcasted_iota(jnp.int32, (n, n), 0)
    slot = lax.broadcasted_iota(jnp.int32, (n, n), 1).astype(F32)
    hit = jnp.broadcast_to(hit_col, (n, n))
    both = jnp.maximum(hit, pltpu.roll(hit, n - 1, axis=0))
    act = jnp.where((both > 0.0) & (blk % 2 == 0) & (blk // 2 < tile0), 1.0, 0.0)
    rank = _dot(mats_ref[0], act.astype(BF16))
    cnt = jnp.sum(act, axis=0, keepdims=True)
    place = jnp.where((act > 0.0) & (rank == slot), 1.0, 0.0)
    tiles = _dot(mats_ref[1, 0:8], place.astype(BF16))[0:1]
    out = jnp.where(lane.astype(F32) < cnt, tiles, float(pad_tile))
    return jnp.where(lane == n - 1, cnt, out).astype(jnp.int32)


def _nsa_cmp_kernel(slopes_ref, q_ref, kc_ref, vct_ref, ovt_ref, mats_ref, ocmp_ref, pen_ref, tiles_ref, *,
                    seq, sub_blocks, attn_blocks):
    step = pl.program_id(2)
    nq_step = sub_blocks * Q_BLOCK
    chunk = min(SEL_LANES, kc_ref.shape[2])
    n_chunks = kc_ref.shape[2] // chunk
    need = ((step + 1) * nq_step - CMP_BLOCK) // CMP_STRIDE + 1
    n_need = (need + chunk - 1) // chunk
    for v in range(1, n_chunks + 1):
        cond = (n_need == v) if v < n_chunks else (n_need >= v)
        if v == 1:
            cond = n_need <= 1

        @pl.when(cond)
        def _(v=v):
            hits = []
            for sub in range(sub_blocks):
                rows = slice(sub * Q_BLOCK, (sub + 1) * Q_BLOCK)
                hits.append(_nsa_cmp_block(slopes_ref, q_ref[rows, :], kc_ref, vct_ref, ovt_ref,
                                           ocmp_ref.at[0, 0, sub], pen_ref.at[0, 0, rows],
                                           step * sub_blocks + sub, seq, v * chunk,
                                           past_first_block=(sub >= 1 or v >= 2)))
            for a in range(sub_blocks // attn_blocks):
                hit = functools.reduce(jnp.maximum, hits[a * attn_blocks:(a + 1) * attn_blocks])
                tile0 = step * sub_blocks + a * attn_blocks
                tiles_ref[0, 0, a] = _tile_list(hit, tile0, seq // KEY_TILE, mats_ref)


def _nsa_cmp_block(slopes_ref, q_all, kc_ref, vct_ref, ovt_ref, ocmp_ref, pen_ref, qb, seq, n_use,
                   past_first_block):
    g = pl.program_id(1)
    nq = Q_BLOCK
    n_cpad = n_use
    n_cmp = seq // CMP_STRIDE - CMP_BLOCK // CMP_STRIDE + 1
    t0 = qb * nq

    s_t = _dot_nt(kc_ref[0, 0, :n_use, :], _stack_heads(q_all))
    n_s = lax.broadcasted_iota(jnp.int32, (n_cpad, 1), 0)
    q_l = lax.broadcasted_iota(jnp.int32, (1, nq), 1)
    dist = (t0 - (CMP_BLOCK - 1)) + q_l - n_s * CMP_STRIDE
    valid = (dist >= 0) & (n_s < n_cmp)
    dist_f = dist.astype(F32)
    probs = []
    p_sum = jnp.zeros((n_cpad, nq), F32)
    for r in range(NSA_REP):
        slope = slopes_ref[g * NSA_REP + r]
        s = jnp.where(valid, s_t[:, r * nq:(r + 1) * nq] - slope * dist_f, NEG)
        m = jnp.max(s, axis=0, keepdims=True)
        e = jnp.exp(s - m)
        inv = jnp.where(m > 0.5 * NEG, 1.0 / jnp.sum(e, axis=0, keepdims=True), 0.0)
        p = e * inv
        probs.append(p.astype(BF16))
        p_sum = p_sum + p
    ocmp_ref[...] = _dot(vct_ref[0, 0, :, :n_use], jnp.concatenate(probs, axis=1))

    p_hi = p_sum.astype(BF16)
    p_lo = (p_sum - p_hi.astype(F32)).astype(BF16)
    ovt = ovt_ref[:, :n_use]
    imp = _dot(ovt, p_hi) + _dot(ovt, p_lo)
    j_i = lax.broadcasted_iota(jnp.int32, (SEL_LANES, 1), 0)
    t_l = t0 + q_l
    cur = t_l // SEL_BLOCK
    forced = (j_i == 0) | (j_i == cur) | (j_i == cur - 1)
    causal_blk = j_i * SEL_BLOCK <= t_l
    val = jnp.where(causal_blk, jnp.where(forced, imp + FORCE_BONUS, imp), NEG)
    j_f = j_i.astype(F32)
    sel_t = jnp.zeros((SEL_LANES, nq), F32)
    rounds = min(SEL_TOPK, seq // SEL_BLOCK)
    if past_first_block and rounds >= 3:
        sel_t = jnp.where(forced, 1.0, sel_t)
        val = jnp.where(forced, PICKED, val)
        rounds -= 3
    for _ in range(rounds):
        mx = jnp.max(val, axis=0, keepdims=True)
        first = jnp.min(jnp.where(val == mx, j_f, float(SEL_LANES)), axis=0, keepdims=True)
        pick = j_f == first
        sel_t = jnp.where(pick, 1.0, sel_t)
        val = jnp.where(pick, PICKED, val)
    sel = sel_t.T
    pen_ref[...] = ((sel - 1.0) * MASK_BIG).astype(pen_ref.dtype)
    return jnp.max(sel_t, axis=1, keepdims=True)


def _nsa_cmp(seg_a, kvc, kvct, ovt, slopes, batch, seq):
    nqb = seq // Q_BLOCK
    gq = NSA_REP * NSA_HEAD_DIM
    g_ = NSA_KV_GROUPS
    n_cpad = kvc.shape[2]
    attn = _nsa_sub_blocks(seq)
    sub = 2 * attn if nqb % (2 * attn) == 0 else attn
    nstep = nqb // sub
    idx = np.arange(SEL_LANES)
    mats = jnp.asarray(np.stack([idx[:, None] > idx[None, :],
                                 np.broadcast_to(idx[None, :] // 2, (SEL_LANES, SEL_LANES))]), BF16)
    return pl.pallas_call(
        functools.partial(_nsa_cmp_kernel, seq=seq, sub_blocks=sub, attn_blocks=attn),
        grid=(batch, g_, nstep),
        in_specs=[
            pl.BlockSpec(memory_space=pltpu.SMEM),
            pl.BlockSpec((sub * Q_BLOCK, gq), lambda b, g, q: (b * nstep + q, A_Q // gq + g)),
            pl.BlockSpec((1, 1, n_cpad, NSA_HEAD_DIM), lambda b, g, q: (b, g, 0, 0)),
            pl.BlockSpec((1, 1, NSA_HEAD_DIM, n_cpad), lambda b, g, q: (b, g_ + g, 0, 0)),
            pl.BlockSpec((SEL_LANES, n_cpad), lambda b, g, q: (0, 0)),
            pl.BlockSpec((2, SEL_LANES, SEL_LANES), lambda b, g, q: (0, 0, 0)),
        ],
        out_specs=[
            pl.BlockSpec((1, 1, sub, NSA_HEAD_DIM, gq), lambda b, g, q: (b, g, q, 0, 0)),
            pl.BlockSpec((1, 1, sub * Q_BLOCK, SEL_LANES), lambda b, g, q: (b, g, q, 0)),
            pl.BlockSpec((1, 1, sub // attn, 1, SEL_LANES), lambda b, g, q: (b, g, q, 0, 0)),
        ],
        out_shape=[
            jax.ShapeDtypeStruct((batch, g_, nqb, NSA_HEAD_DIM, gq), F32),
            jax.ShapeDtypeStruct((batch, g_, seq, SEL_LANES), BF16),
            jax.ShapeDtypeStruct((batch, g_, nqb // attn, 1, SEL_LANES), jnp.int32),
        ],
        compiler_params=pltpu.CompilerParams(
            dimension_semantics=("parallel", "parallel", "parallel"), vmem_limit_bytes=VMEM_LIMIT),
        name="nsa_cmp",
    )(slopes, seg_a, kvc, kvct, ovt, mats)


def _nsa_attn_kernel(slopes_ref, tiles_ref, q_ref, gate_ref, pen_ref, ocmp_ref, ks_ref, vs_ref,
                     kw_ref, vw_ref, onehot_ref, wext_ref, dbias_ref, ubias_ref, wbias_ref, o_ref,
                     ksel_ref, vselt_ref, kwin_ref, vwint_ref,
                     qa_ref, qw_ref, m_ref, acc_ref, owin_ref, gt_ref, sa_ref, sb_ref,
                     *, seq, sub_blocks):
    g = pl.program_id(1)
    step = pl.program_id(2)
    dh = NSA_HEAD_DIM
    nq = sub_blocks * Q_BLOCK
    kt = KEY_TILE
    per = SEL_TILES_PER_STEP
    pad_tile = seq // kt
    t0 = step * nq
    tile0 = step * sub_blocks

    @pl.when(step == 0)
    def _():
        ksel_ref[0:seq, :dh] = ks_ref[...]
        ksel_ref[0:seq, dh:] = onehot_ref[...]
        ksel_ref[seq:, :dh] = jnp.zeros((kt, dh), BF16)
        ksel_ref[seq:, dh:] = jnp.ones((kt, SEL_LANES), BF16)
        lane2 = lax.broadcasted_iota(jnp.int32, (WINDOW, dh + LANE), 1)
        kwin_ref[0:WINDOW, :] = jnp.where(lane2 == dh + WCOL_PAD, 1.0, 0.0).astype(BF16)
        kwin_ref[WINDOW:, :dh] = kw_ref[...]
        kwin_ref[WINDOW:, dh:] = wext_ref[...]
        zero_tile = jnp.zeros((dh + SUM_ROWS, kt), BF16)
        ones_rows = jnp.ones((SUM_ROWS, kt), BF16)
        vselt_ref[pad_tile] = zero_tile
        for i in range(WIN_TILES):
            vwint_ref[i] = zero_tile

        def transpose_tile(t, carry):
            r0 = pl.multiple_of(t * kt, kt)
            vselt_ref[t, :dh] = vs_ref[pl.ds(r0, kt), :].astype(F32).T.astype(BF16)
            vselt_ref[t, dh:] = ones_rows
            vwint_ref[t + WIN_TILES, :dh] = vw_ref[pl.ds(r0, kt), :].astype(F32).T.astype(BF16)
            vwint_ref[t + WIN_TILES, dh:] = ones_rows
            return carry

        lax.fori_loop(0, seq // kt, transpose_tile, 0)

    slopes = [slopes_ref[g * NSA_REP + r] for r in range(NSA_REP)]
    row_q = lax.broadcasted_iota(jnp.int32, (nq, 1), 0)
    lane = lax.broadcasted_iota(jnp.int32, (1, LANE), 1)
    q_all = q_ref[...]

    pen = pen_ref[0, 0].astype(F32)
    blk_rel = ((lane - (t0 + row_q) // SEL_BLOCK) * SEL_BLOCK).astype(F32)
    for r in range(NSA_REP):
        qa_ref[r * nq:(r + 1) * nq, :dh] = q_all[:, r * dh:(r + 1) * dh]
        qa_ref[r * nq:(r + 1) * nq, dh:] = (pen + slopes[r] * blk_rel).astype(BF16)

    def listed(idx):
        return tiles_ref[0, 0, 0, 0, idx]

    cnt = listed(SEL_LANES - 1)

    def group_scores(it):
        keys = jnp.concatenate(
            [ksel_ref[pl.ds(pl.multiple_of(listed(it * per + i) * kt, kt), kt), :]
             for i in range(per)], axis=0)
        return _dot_nt(keys, qa_ref[...]) + ubias_ref[0]

    s = _dot_nt(ksel_ref[pl.ds(pl.multiple_of(t0, kt), nq), :], qa_ref[...]) + dbias_ref[0]
    m0 = jnp.max(s, axis=0, keepdims=True)
    p = jnp.exp(s - m0)
    m_ref[...] = m0
    vals = jnp.concatenate([vselt_ref[tile0 + a] for a in range(sub_blocks)], axis=1)
    acc_ref[...] = _dot(vals, p.astype(BF16))
    sa_ref[...] = group_scores(0)

    tp = t0 + WINDOW + row_q
    t_hi = (tp // SEL_BLOCK).astype(F32)
    t_lo = (tp % SEL_BLOCK).astype(F32)
    for r in range(NSA_REP):
        sl = slopes[r]
        ext = jnp.where(lane == WCOL_PAD, -MASK_BIG, 0.0)
        ext = jnp.where(lane == WCOL_HI, sl * SEL_BLOCK, ext)
        ext = jnp.where(lane == WCOL_LO, sl, ext)
        ext = jnp.where(lane == WCOL_QHI, -sl * SEL_BLOCK * t_hi, ext)
        ext = jnp.where(lane == WCOL_QLO, -sl * t_lo, ext)
        qw_ref[r * nq:(r + 1) * nq, :dh] = q_all[:, r * dh:(r + 1) * dh]
        qw_ref[r * nq:(r + 1) * nq, dh:] = ext.astype(BF16)
    wlen = WINDOW + nq
    s = _dot_nt(kwin_ref[pl.ds(pl.multiple_of(t0, kt), wlen), :], qw_ref[...]) + wbias_ref[...]
    e = jnp.exp(s - jnp.max(s, axis=0, keepdims=True))
    vwin = jnp.concatenate([vwint_ref[tile0 + i] for i in range(WIN_TILES + sub_blocks)], axis=1)
    ow = _dot(vwin, e.astype(BF16))
    owin_ref[...] = ow[:dh] / ow[dh:dh + 1]

    def absorb(s, grp):
        vals = jnp.concatenate([vselt_ref[listed(grp * per + i)] for i in range(per)], axis=1)
        m_old = m_ref[...]
        m_new = jnp.maximum(m_old, jnp.max(s, axis=0, keepdims=True))
        alpha = jnp.exp(m_old - m_new)
        p = jnp.exp(s - m_new)
        acc_ref[...] = alpha * acc_ref[...] + _dot(vals, p.astype(BF16))
        m_ref[...] = m_new

    def sel_body(it, carry):
        s = sa_ref[...]
        sb_ref[...] = group_scores(2 * it + 1)
        absorb(s, 2 * it)
        s = sb_ref[...]
        sa_ref[...] = group_scores(2 * it + 2)
        absorb(s, 2 * it + 1)
        return carry

    lax.fori_loop(0, (cnt + 2 * per - 1) // (2 * per), sel_body, 0)
    o_sel = acc_ref[:dh, :] / acc_ref[dh:dh + 1, :]
    o_win = owin_ref[...]

    gt_ref[...] = _sigmoid(gate_ref[...]).T
    gate_t = gt_ref[pl.ds(pl.multiple_of(g * C_GATE_STRIDE, C_GATE_STRIDE), C_GATE_STRIDE), :]
    for a in range(sub_blocks):
        o_cmp = ocmp_ref[0, 0, a]
        qs = slice(a * Q_BLOCK, (a + 1) * Q_BLOCK)
        for r in range(NSA_REP):
            cols = slice(r * nq + a * Q_BLOCK, r * nq + (a + 1) * Q_BLOCK)
            c0 = N_BRANCH * r
            out_t = (gate_t[c0:c0 + 1, qs] * o_cmp[:, r * Q_BLOCK:(r + 1) * Q_BLOCK]
                     + gate_t[c0 + 1:c0 + 2, qs] * o_sel[:, cols]
                     + gate_t[c0 + 2:c0 + 3, qs] * o_win[:, cols])
            o_ref[qs, r * dh:(r + 1) * dh] = out_t.T.astype(o_ref.dtype)


def _nsa_sub_blocks(seq):
    return 2 if (seq // Q_BLOCK) % 2 == 0 else 1


def _nsa_attn(seg_a, seg_c, pen, ocmp, tiles, onehot, wext, dbias, ubias, wbias, slopes, batch, seq):
    n = batch * seq
    sub = _nsa_sub_blocks(seq)
    nq = sub * Q_BLOCK
    nqb = seq // nq
    gq = NSA_REP * NSA_HEAD_DIM
    g_ = NSA_KV_GROUPS
    dh, kt = NSA_HEAD_DIM, KEY_TILE
    dk = dh + SEL_LANES
    rq = NSA_REP * nq
    n_t = seq // kt

    def kv_spec(kind):
        return pl.BlockSpec((seq, dh), lambda b, g, q, k=kind: (b, A_KV // dh + k * g_ + g))

    def const_spec(arr):
        return pl.BlockSpec(arr.shape, lambda b, g, q, nd=arr.ndim: (0,) * nd)

    return pl.pallas_call(
        functools.partial(_nsa_attn_kernel, seq=seq, sub_blocks=sub),
        grid=(batch, g_, nqb),
        in_specs=[
            pl.BlockSpec(memory_space=pltpu.SMEM),
            pl.BlockSpec((1, 1, 1, 1, SEL_LANES), lambda b, g, q: (b, g, q, 0, 0),
                         memory_space=pltpu.SMEM),
            pl.BlockSpec((nq, gq), lambda b, g, q: (b * nqb + q, A_Q // gq + g)),
            pl.BlockSpec((nq, SEG_C), lambda b, g, q: (b * nqb + q, 0)),
            pl.BlockSpec((1, 1, nq, SEL_LANES), lambda b, g, q: (b, g, q, 0)),
            pl.BlockSpec((1, 1, sub, NSA_HEAD_DIM, gq), lambda b, g, q: (b, g, q, 0, 0)),
            kv_spec(2), kv_spec(3), kv_spec(4), kv_spec(5),
            const_spec(onehot), const_spec(wext),
            pl.BlockSpec((1,) + dbias.shape[1:], lambda b, g, q: (g, 0, 0)),
            pl.BlockSpec((1,) + ubias.shape[1:], lambda b, g, q: (g, 0, 0)),
            const_spec(wbias),
        ],
        out_specs=pl.BlockSpec((nq, gq), lambda b, g, q: (b * nqb + q, g)),
        out_shape=jax.ShapeDtypeStruct((n, NSA_WIDTH), BF16),
        scratch_shapes=[
            pltpu.VMEM((seq + kt, dk), BF16),
            pltpu.VMEM((n_t + 1, dh + SUM_ROWS, kt), BF16),
            pltpu.VMEM((seq + WINDOW, dh + LANE), BF16),
            pltpu.VMEM((n_t + WIN_TILES, dh + SUM_ROWS, kt), BF16),
            pltpu.VMEM((rq, dk), BF16),
            pltpu.VMEM((rq, dk), BF16),
            pltpu.VMEM((1, rq), F32),
            pltpu.VMEM((dh + SUM_ROWS, rq), F32),
            pltpu.VMEM((dh, rq), F32),
            pltpu.VMEM((SEG_C, nq), F32),
            pltpu.VMEM((SEL_TILES_PER_STEP * KEY_TILE, rq), F32),
            pltpu.VMEM((SEL_TILES_PER_STEP * KEY_TILE, rq), F32),
        ],
        compiler_params=pltpu.CompilerParams(
            dimension_semantics=("parallel", "parallel", "arbitrary"), vmem_limit_bytes=VMEM_LIMIT),
        name="nsa_attn",
    )(slopes, tiles, seg_a, seg_c, pen, ocmp, seg_a, seg_a, seg_a, seg_a, onehot, wext,
      dbias, ubias, wbias)


def _nsa_tables(slopes, seq):
    g_, kt = NSA_KV_GROUPS, KEY_TILE
    pos = np.arange(seq)
    onehot = jnp.asarray(pos[:, None] // SEL_BLOCK == np.arange(SEL_LANES)[None, :], BF16)
    ext = np.zeros((seq, LANE), np.float32)
    ext[:, WCOL_HI] = (pos + WINDOW) // SEL_BLOCK
    ext[:, WCOL_LO] = (pos + WINDOW) % SEL_BLOCK
    ext[:, WCOL_QHI] = 1.0
    ext[:, WCOL_QLO] = 1.0
    wext = jnp.asarray(ext, BF16)

    nq = _nsa_sub_blocks(seq) * Q_BLOCK

    def alibi_in_block(rows):
        u = jnp.asarray((np.arange(rows) % SEL_BLOCK).astype(np.float32))[None, :, None, None]
        t = jnp.broadcast_to(slopes.reshape(g_, 1, NSA_REP, 1) * u, (g_, rows, NSA_REP, nq))
        return t.reshape(g_, rows, NSA_REP * nq)

    ubias = alibi_in_block(SEL_TILES_PER_STEP * kt)
    kq = np.arange(nq)[:, None] <= np.arange(nq)[None, :]
    causal = np.tile(np.where(kq, 0.0, NEG).astype(np.float32), (1, NSA_REP))
    dbias = alibi_in_block(nq) + jnp.asarray(causal)[None]
    ki = np.arange(WINDOW + nq)[:, None]
    qi = np.arange(nq)[None, :]
    band = np.where((ki > qi) & (ki <= qi + WINDOW), 0.0, NEG).astype(np.float32)
    wbias = jnp.asarray(np.tile(band, (1, NSA_REP)))
    return onehot, wext, dbias, ubias, wbias


def _log_sigmoid(x):
    return jnp.minimum(x, 0.0) - jnp.log(1.0 + jnp.exp(-jnp.abs(x)))


def _split3(x):
    hi = x.astype(BF16)
    r1 = x - hi.astype(F32)
    mid = r1.astype(BF16)
    lo = (r1 - mid.astype(F32)).astype(BF16)
    return hi, mid, lo


def _mlstm_kernel(bias_ref, q_ref, k_ref, v_ref, o_ref, ifc_ref, ng_ref,
                  tri_ref, y_ref, c_ref, n_ref, m_ref):
    ch = pl.program_id(1)

    @pl.when(ch == 0)
    def _():
        c_ref[...] = jnp.zeros(c_ref.shape, F32)
        n_ref[...] = jnp.zeros(n_ref.shape, F32)
        m_ref[...] = jnp.zeros(m_ref.shape, F32)

    for sq in range(q_ref.shape[0]):
        _mlstm_chunk(bias_ref, q_ref.at[sq], k_ref.at[sq], v_ref.at[sq], o_ref.at[sq], ifc_ref.at[sq],
                     ng_ref, tri_ref, y_ref.at[sq], c_ref.at[sq], n_ref.at[sq], m_ref.at[sq])


def _mlstm_chunk(bias_ref, q_ref, k_ref, v_ref, o_ref, ifc_ref, ng_ref, tri_ref, y_ref,
                 c_ref, n_ref, m_ref):
    nh, dh = MLSTM_HEADS, MLSTM_HEAD_DIM
    L = q_ref.shape[0]
    tri = tri_ref[...]
    lane8 = lax.broadcasted_iota(jnp.int32, (1, LANE), 1)
    bias_c = jnp.zeros((1, LANE), F32)
    for h in range(nh):
        bias_c = jnp.where(lane8 == C_IF + h, bias_ref[h], bias_c)
        bias_c = jnp.where(lane8 == C_IF + nh + h, bias_ref[nh + h], bias_c)
    pre_c = ifc_ref[...] + bias_c
    cum_c = sum(_dot(tri, part) for part in _split3(_log_sigmoid(pre_c)))
    pre_r = pre_c.T[C_IF:C_IF + 2 * nh]
    cum_r = sum(_dot_nt(part, tri) for part in _split3(_log_sigmoid(pre_r)))

    rr = lax.broadcasted_iota(jnp.int32, (L, 1), 0)
    cc = lax.broadcasted_iota(jnp.int32, (1, L), 1)
    causal = cc <= rr

    for h in range(nh):
        cols = slice(h * dh, (h + 1) * dh)
        qb = q_ref[:, cols]
        kb = k_ref[:, cols]
        vh = v_ref[:, cols]
        qh = qb.astype(F32)
        kh = kb.astype(F32)
        b_c = cum_c[:, C_IF + nh + h:C_IF + nh + h + 1]
        li_c = pre_c[:, C_IF + h:C_IF + h + 1]
        b_r = cum_r[nh + h:nh + h + 1, :]
        li_r = pre_r[h:h + 1, :]
        m_prev = m_ref[h:h + 1, 0:1]

        dmat = jnp.where(causal, b_c - b_r + li_r, NEG)
        a = b_c + m_prev
        m_j = jnp.maximum(a, jnp.max(dmat, axis=1, keepdims=True))
        w_intra = jnp.exp(dmat - m_j)
        w_inter = jnp.exp(a - m_j)
        sc = _dot_nt(qb, kb) * w_intra
        c_old = c_ref[h]
        n_old = n_ref[h:h + 1, :]
        num = w_inter * _dot(qb, c_old.astype(BF16)) + _dot(sc.astype(BF16), vh)
        den = (w_inter * jnp.sum(qh * n_old, axis=1, keepdims=True)
               + jnp.sum(sc, axis=1, keepdims=True))
        hid = num / jnp.maximum(jnp.abs(den), jnp.exp(-m_j))

        g_tot = b_r[:, L - 1:L]
        lw_c = g_tot - b_c + li_c
        lw_r = g_tot - b_r + li_r
        m_new = jnp.maximum(g_tot + m_prev, jnp.max(lw_r, axis=1, keepdims=True))
        decay = jnp.exp(g_tot + m_prev - m_new)
        kw = jnp.exp(lw_c - m_new) * kh
        c_ref[h] = decay * c_old + _dot(kw.T.astype(BF16), vh)
        n_ref[h:h + 1, :] = decay * n_old + jnp.sum(kw, axis=0, keepdims=True)
        m_ref[h:h + 1, :] = jnp.broadcast_to(m_new, (1, LANE))

        hn = hid * lax.rsqrt(jnp.mean(hid * hid, axis=-1, keepdims=True) + EPS) * ng_ref[:, cols]
        y_ref[:, cols] = (_sigmoid(o_ref[:, cols]) * hn).astype(y_ref.dtype)


def _mlstm(seg_a, seg_b, seg_c, bias, norm_g, tri, batch, seq, chunk):
    nc = seq // chunk
    w = MLSTM_WIDTH
    nh, dh = MLSTM_HEADS, MLSTM_HEAD_DIM
    seqs = 1
    a3 = seg_a.reshape(batch, seq, SEG_A)
    b3 = seg_b.reshape(batch, seq, SEG_B)
    c3 = seg_c.reshape(batch, seq, SEG_C)

    def col_spec(off):
        return pl.BlockSpec((seqs, chunk, w), lambda b, c, o=off // w: (b, c, o))

    y = pl.pallas_call(
        _mlstm_kernel,
        grid=(batch // seqs, nc),
        in_specs=[
            pl.BlockSpec(memory_space=pltpu.SMEM),
            col_spec(A_QK), col_spec(A_QK + w), col_spec(A_V), col_spec(B_O),
            pl.BlockSpec((seqs, chunk, SEG_C), lambda b, c: (b, c, 0)),
            pl.BlockSpec((1, w), lambda b, c: (0, 0)),
            pl.BlockSpec((chunk, chunk), lambda b, c: (0, 0)),
        ],
        out_specs=pl.BlockSpec((seqs, chunk, w), lambda b, c: (b, c, 0)),
        out_shape=jax.ShapeDtypeStruct((batch, seq, w), BF16),
        scratch_shapes=[
            pltpu.VMEM((seqs, nh, dh, dh), F32),
            pltpu.VMEM((seqs, 8, dh), F32),
            pltpu.VMEM((seqs, 8, LANE), F32),
        ],
        compiler_params=pltpu.CompilerParams(
            dimension_semantics=("parallel", "arbitrary"), vmem_limit_bytes=VMEM_LIMIT),
        name="mlstm",
    )(bias, a3, a3, a3, b3, c3, norm_g, tri)
    return y.reshape(batch * seq, w)


def _outproj_kernel(x_ref, ya_ref, ym_ref, wa_ref, wm_ref, o_ref):
    o_ref[...] = x_ref[...] + _dot(ya_ref[...], wa_ref[...]) + _dot(ym_ref[...], wm_ref[...])


def _outproj(x2, ya, ym, w, tm):
    n, d = x2.shape
    assert ya.shape[1] == ym.shape[1] == d // 2
    return pl.pallas_call(
        _outproj_kernel,
        grid=(n // tm,),
        in_specs=[
            pl.BlockSpec((tm, d), lambda i: (i, 0)),
            pl.BlockSpec((tm, d // 2), lambda i: (i, 0)),
            pl.BlockSpec((tm, d // 2), lambda i: (i, 0)),
            pl.BlockSpec((d // 2, d), lambda i: (0, 0)),
            pl.BlockSpec((d // 2, d), lambda i: (1, 0)),
        ],
        out_specs=pl.BlockSpec((tm, d), lambda i: (i, 0)),
        out_shape=jax.ShapeDtypeStruct((n, d), F32),
        compiler_params=pltpu.CompilerParams(
            dimension_semantics=("parallel",), vmem_limit_bytes=VMEM_LIMIT),
        name="outproj",
    )(x2, ya, ym, w, w)


def _mlp_kernel(x_ref, g_ref, w1_ref, w2_ref, gf_ref, o_ref, h_ref):
    f = pl.program_id(1)

    @pl.when(f == 0)
    def _():
        x = x_ref[...]
        r = lax.rsqrt(jnp.mean(x * x, axis=-1, keepdims=True) + EPS)
        h_ref[...] = (x * r * g_ref[...]).astype(BF16)
        o_ref[...] = x

    u = jnp.maximum(_dot(h_ref[...], w1_ref[...]), 0.0)
    o_ref[...] += _dot((u * u).astype(BF16), w2_ref[...])

    @pl.when(f == pl.num_programs(1) - 1)
    def _():
        x2 = o_ref[...]
        r = lax.rsqrt(jnp.mean(x2 * x2, axis=-1, keepdims=True) + EPS)
        o_ref[...] = x2 * r * gf_ref[...]


def _mlp(x1, g, w1, w2, gf, tm, tf):
    n, d = x1.shape
    dff = w1.shape[1]
    return pl.pallas_call(
        _mlp_kernel,
        grid=(n // tm, dff // tf),
        in_specs=[
            pl.BlockSpec((tm, d), lambda i, f: (i, 0)),
            pl.BlockSpec((1, d), lambda i, f: (0, 0)),
            pl.BlockSpec((d, tf), lambda i, f: (0, f)),
            pl.BlockSpec((tf, d), lambda i, f: (f, 0)),
            pl.BlockSpec((1, d), lambda i, f: (0, 0)),
        ],
        out_specs=pl.BlockSpec((tm, d), lambda i, f: (i, 0)),
        out_shape=jax.ShapeDtypeStruct((n, d), F32),
        scratch_shapes=[pltpu.VMEM((tm, d), BF16)],
        compiler_params=pltpu.CompilerParams(
            dimension_semantics=("parallel", "arbitrary"), vmem_limit_bytes=VMEM_LIMIT),
        name="mlp",
    )(x1, g, w1, w2, gf)


def _row_tile(n, want):
    t = want
    while n % t:
        t //= 2
    return t


def _layer(x2, batch, seq, norm_mix_g, w_in, w_cmp_k1, w_cmp_k2, pos_cmp_k, w_cmp_v1, w_cmp_v2,
           pos_cmp_v, conv_w, conv_b, b_igate, b_fgate, mlstm_norm_g, w_out, norm_mlp_g,
           w_mlp_in, w_mlp_out):
    n, d = x2.shape
    assert seq % Q_BLOCK == 0 and seq >= WINDOW + Q_BLOCK and seq // SEL_BLOCK <= SEL_LANES
    nh = MLSTM_HEADS

    c_gate = NSA_WIDTH + 6 * NSA_KV_WIDTH
    c_qk = c_gate + NSA_HEADS * N_BRANCH
    c_v = c_qk + 2 * MLSTM_WIDTH
    c_o = c_v + MLSTM_WIDTH
    c_i = c_o + MLSTM_WIDTH
    c_f = c_i + nh
    def cols16(lo, hi):
        return w_in[:, lo:hi].astype(BF16)

    w_ab = jnp.concatenate(
        [cols16(c_v, c_o), cols16(c_qk, c_v), cols16(0, c_gate),
         cols16(c_o, c_i)], axis=1)
    per_g = NSA_REP * N_BRANCH
    gate_cols = []
    for g in range(NSA_KV_GROUPS):
        gate_cols += [cols16(c_gate + g * per_g, c_gate + (g + 1) * per_g),
                      jnp.zeros((d, C_GATE_STRIDE - per_g), BF16)]
    w_c = jnp.concatenate(gate_cols + [cols16(c_i, c_f + nh),
                                       jnp.zeros((d, SEG_C - C_IF - 2 * nh), BF16)], axis=1)
    scale = jnp.concatenate([jnp.ones((1, A_QK + MLSTM_WIDTH), F32),
                             jnp.full((1, MLSTM_WIDTH), MLSTM_HEAD_DIM ** -0.5, F32),
                             jnp.full((1, NSA_WIDTH), NSA_HEAD_DIM ** -0.5, F32),
                             jnp.ones((1, SEG_A - A_KV + SEG_B), F32)], axis=1)
    conv_pad = ((0, 0), (A_QK, SEG_A - A_Q + SEG_B))
    cw_ab = jnp.pad(conv_w, conv_pad)
    cb_ab = jnp.pad(conv_b.reshape(1, -1), conv_pad)
    g_mix = norm_mix_g.reshape(1, d)

    tm = _row_tile(seq, ROW_TILE)
    seg_a, seg_b, seg_c = _inproj(x2, g_mix, w_ab, w_c, scale, cw_ab, cb_ab, tm, COL_TILE,
                                  (A_QK, A_Q), seq)

    n_sub = seq // CMP_STRIDE
    dh = NSA_HEAD_DIM
    w1s = jnp.stack([w_cmp_k1, w_cmp_v1]).reshape(2, 2, CMP_STRIDE, dh, dh)
    w1r = jnp.concatenate([w1s[:, 0], w1s[:, 1]], axis=-1).astype(BF16)
    w2s = jnp.stack([w_cmp_k2, w_cmp_v2]).astype(BF16)
    poss = jnp.stack([pos_cmp_k, pos_cmp_v]).reshape(2, 2, CMP_STRIDE, dh).transpose(0, 2, 1, 3)
    posr = jnp.pad(poss, ((0, 0), (0, 0), (0, 6), (0, 0))).astype(BF16)
    kvc, kvct = _compress(seg_a, w1r, w2s, posr, batch, seq)

    cmp_start = np.arange(n_sub) * CMP_STRIDE
    sel_start = np.arange(SEL_LANES) * SEL_BLOCK
    ovt = ((cmp_start[None, :] < sel_start[:, None] + SEL_BLOCK)
           & (cmp_start[None, :] + CMP_BLOCK - 1 >= sel_start[:, None])
           & (np.arange(n_sub)[None, :] < n_sub - CMP_BLOCK // CMP_STRIDE + 1))
    ovt = jnp.asarray(ovt, BF16)
    slopes = jnp.exp2(-8.0 * jnp.arange(1, NSA_HEADS + 1, dtype=F32) / NSA_HEADS)
    ocmp, pen, tiles = _nsa_cmp(seg_a, kvc, kvct, ovt, slopes, batch, seq)
    onehot, wext, dbias, ubias, wbias = _nsa_tables(slopes, seq)
    y_a = _nsa_attn(seg_a, seg_c, pen, ocmp, tiles, onehot, wext, dbias, ubias, wbias,
                    slopes, batch, seq)

    chunk = 256 if seq % 256 == 0 else 128
    bias = jnp.concatenate([b_igate, b_fgate]).astype(F32)
    tri = jnp.asarray(np.tril(np.ones((chunk, chunk), np.float32)), BF16)
    y_m = _mlstm(seg_a, seg_b, seg_c, bias, mlstm_norm_g.reshape(1, -1), tri, batch, seq, chunk)

    x1 = _outproj(x2, y_a, y_m, w_out.astype(BF16), _row_tile(n, 512))
    return x1, (norm_mlp_g.reshape(1, d), w_mlp_in.astype(BF16), w_mlp_out.astype(BF16))


def kernel(x, norm_mix_g, w_in, w_cmp_k1, w_cmp_k2, pos_cmp_k, w_cmp_v1, w_cmp_v2, pos_cmp_v, conv_w, conv_b, b_igate, b_fgate, mlstm_norm_g, w_out, norm_mlp_g, w_mlp_in, w_mlp_out, norm_f_g):
    batch, seq, d = x.shape
    depth = w_in.shape[0]
    assert depth == 1, "the final RMSNorm is fused into the last layer's channel mixer"
    x2 = x.reshape(batch * seq, d)
    tm = _row_tile(batch * seq, ROW_TILE)
    for l in range(depth):
        x1, (g_mlp, w1, w2) = _layer(
            x2, batch, seq, norm_mix_g[l], w_in[l], w_cmp_k1[l], w_cmp_k2[l], pos_cmp_k[l],
            w_cmp_v1[l], w_cmp_v2[l], pos_cmp_v[l], conv_w[l], conv_b[l], b_igate[l], b_fgate[l],
            mlstm_norm_g[l], w_out[l], norm_mlp_g[l], w_mlp_in[l], w_mlp_out[l])
        x2 = _mlp(x1, g_mlp, w1, w2, norm_f_g.reshape(1, d), tm, 512)
    return x2.reshape(batch, seq, d)
```

```python
import functools

import numpy as np
import jax
import jax.numpy as jnp
from jax import lax
from jax.experimental import pallas as pl
from jax.experimental.pallas import tpu as pltpu

F32 = jnp.float32
BF16 = jnp.bfloat16

EPS = 1e-6
NEG = -1e30
FORCE_BONUS = 1e4
PICKED = -3e38
MASK_BIG = 1e30

D_MODEL = 2048
NSA_HEAD_DIM = 128
NSA_WIDTH = D_MODEL // 2
NSA_HEADS = NSA_WIDTH // NSA_HEAD_DIM
NSA_REP = 4
NSA_KV_GROUPS = NSA_HEADS // NSA_REP
NSA_KV_WIDTH = NSA_KV_GROUPS * NSA_HEAD_DIM
CMP_BLOCK = 32
CMP_STRIDE = 16
SEL_BLOCK = 64
SEL_TOPK = 16
WINDOW = 512
Q_BLOCK = 128
N_BRANCH = 3
MLSTM_HEAD_DIM = 256
MLSTM_WIDTH = D_MODEL - NSA_WIDTH
MLSTM_HEADS = MLSTM_WIDTH // MLSTM_HEAD_DIM
CONV_WIDTH = 4

LANE = 128
SEL_LANES = 128
KEY_TILE = 128
SEL_TILES_PER_STEP = 2
WIN_TILES = WINDOW // KEY_TILE
SUM_ROWS = 16
VMEM_LIMIT = 56 * 1024 * 1024
ROW_TILE = 1024

A_V, A_QK = 0, MLSTM_WIDTH
A_Q = A_QK + 2 * MLSTM_WIDTH
A_KV = A_Q + NSA_WIDTH
SEG_A = A_KV + 6 * NSA_KV_WIDTH
B_O = 0
SEG_B = MLSTM_WIDTH
C_GATE_STRIDE = 16
C_IF = NSA_KV_GROUPS * C_GATE_STRIDE
SEG_C = LANE
COL_TILE = 512

WCOL_PAD, WCOL_HI, WCOL_LO, WCOL_QHI, WCOL_QLO = 0, 1, 2, 3, 4


def _dot(a, b):
    return jnp.dot(a, b, preferred_element_type=F32)


def _dot_nt(a, b):
    return lax.dot_general(a, b, (((1,), (1,)), ((), ())), preferred_element_type=F32)


def _sigmoid(x):
    return 1.0 / (1.0 + jnp.exp(-x))


def _shifted(x, tail, s):
    xs = pltpu.roll(x, s, axis=0)
    ts = pltpu.roll(tail, s, axis=0)
    row8 = lax.broadcasted_iota(jnp.int32, (8, 1), 0)
    head = jnp.where(row8 < s, ts, xs[:8])
    return jnp.concatenate([head, xs[8:]], axis=0)


def _conv_silu(x, tail, w, b):
    y = b + _shifted(x, tail, CONV_WIDTH - 1) * w[0:1]
    for i in range(1, CONV_WIDTH - 1):
        y = y + _shifted(x, tail, CONV_WIDTH - 1 - i) * w[i:i + 1]
    y = y + x * w[CONV_WIDTH - 1:CONV_WIDTH]
    return y * _sigmoid(y)


def _inproj_kernel(x_ref, g_ref, w_ref, wc_ref, cs_ref, cw_ref, cb_ref, oa_ref, ob_ref, oc_ref,
                   h_ref, halo_ref, *, nb, n_plain, tiles_per_seq):
    i = pl.program_id(0)
    j = pl.program_id(1)

    @pl.when(j == 0)
    def _():
        x = x_ref[...]
        r = lax.rsqrt(jnp.mean(x * x, axis=-1, keepdims=True) + EPS)
        h_ref[...] = (x * r * g_ref[...]).astype(BF16)

    is_conv = j > nb + n_plain

    @pl.when((j > nb) & jnp.logical_not(is_conv))
    def _():
        oa_ref[...] = (_dot(h_ref[...], w_ref[...]) * cs_ref[...]).astype(oa_ref.dtype)

    @pl.when(j < nb)
    def _():
        ob_ref[...] = (_dot(h_ref[...], w_ref[...]) * cs_ref[...]).astype(ob_ref.dtype)

    @pl.when(j == nb)
    def _():
        oc_ref[...] = _dot(h_ref[...], wc_ref[...])

    @pl.when(is_conv)
    def _():
        slot = j - (nb + 1 + n_plain)

        @pl.when(i % tiles_per_seq == 0)
        def _():
            halo_ref[slot] = jnp.zeros(halo_ref.shape[1:], F32)

        acc = _dot(h_ref[...], w_ref[...])
        tail = halo_ref[slot]
        halo_ref[slot] = acc[acc.shape[0] - 8:]
        y = _conv_silu(acc, tail, cw_ref[...], cb_ref[...])
        oa_ref[...] = (y * cs_ref[...]).astype(oa_ref.dtype)


def _inproj(x2, g, w, w_c, cs, cw, cb, tm, tn, conv_cols, seq):
    n, d = x2.shape
    na, nb = SEG_A // tn, SEG_B // tn
    conv_lo, conv_hi = conv_cols[0] // tn, conv_cols[1] // tn
    assert conv_cols[0] % tn == 0 and conv_cols[1] % tn == 0 and seq % tm == 0 and conv_hi <= na
    assert SEG_A % tn == 0 and SEG_B % tn == 0 and w.shape[1] == SEG_A + SEG_B
    n_conv = conv_hi - conv_lo
    n_plain = na - n_conv

    def a_tile(j):
        p = jnp.maximum(j - nb - 1, 0)
        plain = jnp.where(p < conv_lo, p, p + n_conv)
        return jnp.where(p < n_plain, plain, conv_lo + p - n_plain)

    def w_tile(i, j):
        return (0, jnp.where(j <= nb, na + jnp.minimum(j, nb - 1), a_tile(j)))

    return pl.pallas_call(
        functools.partial(_inproj_kernel, nb=nb, n_plain=n_plain, tiles_per_seq=seq // tm),
        grid=(n // tm, na + nb + 1),
        in_specs=[
            pl.BlockSpec((tm, d), lambda i, j: (i, 0)),
            pl.BlockSpec((1, d), lambda i, j: (0, 0)),
            pl.BlockSpec((d, tn), w_tile),
            pl.BlockSpec((d, SEG_C), lambda i, j: (0, 0)),
            pl.BlockSpec((1, tn), w_tile),
            pl.BlockSpec((CONV_WIDTH, tn), w_tile),
            pl.BlockSpec((1, tn), w_tile),
        ],
        out_specs=[pl.BlockSpec((tm, tn), lambda i, j: (i, a_tile(j))),
                   pl.BlockSpec((tm, tn), lambda i, j: (i, jnp.minimum(j, nb - 1))),
                   pl.BlockSpec((tm, SEG_C), lambda i, j: (i, 0))],
        out_shape=[jax.ShapeDtypeStruct((n, SEG_A), BF16), jax.ShapeDtypeStruct((n, SEG_B), F32),
                   jax.ShapeDtypeStruct((n, SEG_C), F32)],
        scratch_shapes=[pltpu.VMEM((tm, d), BF16),
                        pltpu.VMEM((conv_hi - conv_lo, 8, tn), F32)],
        compiler_params=pltpu.CompilerParams(
            dimension_semantics=("arbitrary", "arbitrary"), vmem_limit_bytes=VMEM_LIMIT),
        name="inproj",
    )(x2, g, w, w_c, cs, cw, cb)


def _compress_kernel(x_ref, w1_ref, w2_ref, pos_ref, o_ref, ot_ref, xf_ref):
    n_sub = o_ref.shape[2]
    dh = NSA_HEAD_DIM
    xf_ref[...] = x_ref[...].astype(F32)
    acc = jnp.zeros((n_sub, 2 * dh), F32)
    posw = jnp.zeros((1, dh), F32)
    for p in range(CMP_STRIDE):
        wp = w1_ref[0, p]
        acc = acc + _dot(xf_ref[pl.ds(p, n_sub, stride=CMP_STRIDE), :].astype(BF16), wp)
        pw = _dot(pos_ref[0, p], wp)
        posw = posw + pw[0:1, :dh] + pw[1:2, dh:]
    bot = pltpu.roll(acc[:, dh:], n_sub - 1, axis=0)
    pre = acc[:, :dh] + bot + posw
    hid = pre * _sigmoid(pre)
    out = _dot(hid.astype(BF16), w2_ref[0])
    o_ref[0, 0] = out.astype(o_ref.dtype)
    ot_ref[0, 0] = out.T.astype(ot_ref.dtype)


def _compress(seg_a, w1r, w2s, posr, batch, seq):
    g = NSA_KV_GROUPS
    c = 2 * g
    dh = NSA_HEAD_DIM
    n_sub = seq // CMP_STRIDE
    return pl.pallas_call(
        _compress_kernel,
        grid=(batch, c),
        in_specs=[
            pl.BlockSpec((seq, dh), lambda i, j: (i, A_KV // dh + j)),
            pl.BlockSpec((1, CMP_STRIDE, dh, 2 * dh), lambda i, j: (j // g, 0, 0, 0)),
            pl.BlockSpec((1, dh, dh), lambda i, j: (j // g, 0, 0)),
            pl.BlockSpec((1, CMP_STRIDE, 8, dh), lambda i, j: (j // g, 0, 0, 0)),
        ],
        out_specs=[pl.BlockSpec((1, 1, n_sub, dh), lambda i, j: (i, j, 0, 0)),
                   pl.BlockSpec((1, 1, dh, n_sub), lambda i, j: (i, j, 0, 0))],
        out_shape=[jax.ShapeDtypeStruct((batch, c, n_sub, dh), BF16),
                   jax.ShapeDtypeStruct((batch, c, dh, n_sub), BF16)],
        scratch_shapes=[pltpu.VMEM((seq, dh), F32)],
        compiler_params=pltpu.CompilerParams(
            dimension_semantics=("parallel", "parallel"), vmem_limit_bytes=VMEM_LIMIT),
        name="compress",
    )(seg_a, w1r, w2s, posr)


def _stack_heads(q_all):
    dh = NSA_HEAD_DIM
    return jnp.concatenate([q_all[:, r * dh:(r + 1) * dh] for r in range(NSA_REP)], axis=0)


def _tile_list(hit_col, tile0, pad_tile, mats_ref):
    n = SEL_LANES
    lane = lax.broadcasted_iota(jnp.int32, (1, n), 1)
    blk = lax.broadcasted_iota(jnp.int32, (n, n), 0)
    slot = lax.broadcasted_iota(jnp.int32, (n, n), 1).astype(F32)
    hit = jnp.broadcast_to(hit_col, (n, n))
    both = jnp.maximum(hit, pltpu.roll(hit, n - 1, axis=0))
    act = jnp.where((both > 0.0) & (blk % 2 == 0) & (blk // 2 < tile0), 1.0, 0.0)
    rank = _dot(mats_ref[0], act.astype(BF16))
    cnt = jnp.sum(act, axis=0, keepdims=True)
    place = jnp.where((act > 0.0) & (rank == slot), 1.0, 0.0)
    tiles = _dot(mats_ref[1, 0:8], place.astype(BF16))[0:1]
    out = jnp.where(lane.astype(F32) < cnt, tiles, float(pad_tile))
    return jnp.where(lane == n - 1, cnt, out).astype(jnp.int32)


def _nsa_cmp_kernel(slopes_ref, q_ref, kc_ref, vct_ref, ovt_ref, mats_ref, ocmp_ref, pen_ref, tiles_ref, *,
                    seq, sub_blocks, attn_blocks):
    step = pl.program_id(2)
    nq_step = sub_blocks * Q_BLOCK
    chunk = min(SEL_LANES, kc_ref.shape[2])
    n_chunks = kc_ref.shape[2] // chunk
    need = ((step + 1) * nq_step - CMP_BLOCK) // CMP_STRIDE + 1
    n_need = (need + chunk - 1) // chunk
    for v in range(1, n_chunks + 1):
        cond = (n_need == v) if v < n_chunks else (n_need >= v)
        if v == 1:
            cond = n_need <= 1

        @pl.when(cond)
        def _(v=v):
            hits = []
            for sub in range(sub_blocks):
                rows = slice(sub * Q_BLOCK, (sub + 1) * Q_BLOCK)
                hits.append(_nsa_cmp_block(slopes_ref, q_ref[rows, :], kc_ref, vct_ref, ovt_ref,
                                           ocmp_ref.at[0, 0, sub], pen_ref.at[0, 0, rows],
                                           step * sub_blocks + sub, seq, v * chunk,
                                           past_first_block=(sub >= 1 or v >= 2)))
            for a in range(sub_blocks // attn_blocks):
                hit = functools.reduce(jnp.maximum, hits[a * attn_blocks:(a + 1) * attn_blocks])
                tile0 = step * sub_blocks + a * attn_blocks
                tiles_ref[0, 0, a] = _tile_list(hit, tile0, seq // KEY_TILE, mats_ref)


def _nsa_cmp_block(slopes_ref, q_all, kc_ref, vct_ref, ovt_ref, ocmp_ref, pen_ref, qb, seq, n_use,
                   past_first_block):
    g = pl.program_id(1)
    nq = Q_BLOCK
    n_cpad = n_use
    n_cmp = seq // CMP_STRIDE - CMP_BLOCK // CMP_STRIDE + 1
    t0 = qb * nq

    s_t = _dot_nt(kc_ref[0, 0, :n_use, :], _stack_heads(q_all))
    n_s = lax.broadcasted_iota(jnp.int32, (n_cpad, 1), 0)
    q_l = lax.broadcasted_iota(jnp.int32, (1, nq), 1)
    dist = (t0 - (CMP_BLOCK - 1)) + q_l - n_s * CMP_STRIDE
    valid = (dist >= 0) & (n_s < n_cmp)
    dist_f = dist.astype(F32)
    probs = []
    p_sum = jnp.zeros((n_cpad, nq), F32)
    for r in range(NSA_REP):
        slope = slopes_ref[g * NSA_REP + r]
        s = jnp.where(valid, s_t[:, r * nq:(r + 1) * nq] - slope * dist_f, NEG)
        m = jnp.max(s, axis=0, keepdims=True)
        e = jnp.exp(s - m)
        inv = jnp.where(m > 0.5 * NEG, 1.0 / jnp.sum(e, axis=0, keepdims=True), 0.0)
        p = e * inv
        probs.append(p.astype(BF16))
        p_sum = p_sum + p
    ocmp_ref[...] = _dot(vct_ref[0, 0, :, :n_use], jnp.concatenate(probs, axis=1))

    p_hi = p_sum.astype(BF16)
    p_lo = (p_sum - p_hi.astype(F32)).astype(BF16)
    ovt = ovt_ref[:, :n_use]
    imp = _dot(ovt, p_hi) + _dot(ovt, p_lo)
    j_i = lax.broadcasted_iota(jnp.int32, (SEL_LANES, 1), 0)
    t_l = t0 + q_l
    cur = t_l // SEL_BLOCK
    forced = (j_i == 0) | (j_i == cur) | (j_i == cur - 1)
    causal_blk = j_i * SEL_BLOCK <= t_l
    val = jnp.where(causal_blk, jnp.where(forced, imp + FORCE_BONUS, imp), NEG)
    j_f = j_i.astype(F32)
    sel_t = jnp.zeros((SEL_LANES, nq), F32)
    rounds = min(SEL_TOPK, seq // SEL_BLOCK)
    if past_first_block and rounds >= 3:
        sel_t = jnp.where(forced, 1.0, sel_t)
        val = jnp.where(forced, PICKED, val)
        rounds -= 3
    for _ in range(rounds):
        mx = jnp.max(val, axis=0, keepdims=True)
        first = jnp.min(jnp.where(val == mx, j_f, float(SEL_LANES)), axis=0, keepdims=True)
        pick = j_f == first
        sel_t = jnp.where(pick, 1.0, sel_t)
        val = jnp.where(pick, PICKED, val)
    sel = sel_t.T
    pen_ref[...] = ((sel - 1.0) * MASK_BIG).astype(pen_ref.dtype)
    return jnp.max(sel_t, axis=1, keepdims=True)


def _nsa_cmp(seg_a, kvc, kvct, ovt, slopes, batch, seq):
    nqb = seq // Q_BLOCK
    gq = NSA_REP * NSA_HEAD_DIM
    g_ = NSA_KV_GROUPS
    n_cpad = kvc.shape[2]
    attn = _nsa_sub_blocks(seq)
    sub = 4 * attn if nqb % (4 * attn) == 0 else attn
    nstep = nqb // sub
    idx = np.arange(SEL_LANES)
    mats = jnp.asarray(np.stack([idx[:, None] > idx[None, :],
                                 np.broadcast_to(idx[None, :] // 2, (SEL_LANES, SEL_LANES))]), BF16)
    return pl.pallas_call(
        functools.partial(_nsa_cmp_kernel, seq=seq, sub_blocks=sub, attn_blocks=attn),
        grid=(batch, g_, nstep),
        in_specs=[
            pl.BlockSpec(memory_space=pltpu.SMEM),
            pl.BlockSpec((sub * Q_BLOCK, gq), lambda b, g, q: (b * nstep + q, A_Q // gq + g)),
            pl.BlockSpec((1, 1, n_cpad, NSA_HEAD_DIM), lambda b, g, q: (b, g, 0, 0)),
            pl.BlockSpec((1, 1, NSA_HEAD_DIM, n_cpad), lambda b, g, q: (b, g_ + g, 0, 0)),
            pl.BlockSpec((SEL_LANES, n_cpad), lambda b, g, q: (0, 0)),
            pl.BlockSpec((2, SEL_LANES, SEL_LANES), lambda b, g, q: (0, 0, 0)),
        ],
        out_specs=[
            pl.BlockSpec((1, 1, sub, NSA_HEAD_DIM, gq), lambda b, g, q: (b, g, q, 0, 0)),
            pl.BlockSpec((1, 1, sub * Q_BLOCK, SEL_LANES), lambda b, g, q: (b, g, q, 0)),
            pl.BlockSpec((1, 1, sub // attn, 1, SEL_LANES), lambda b, g, q: (b, g, q, 0, 0)),
        ],
        out_shape=[
            jax.ShapeDtypeStruct((batch, g_, nqb, NSA_HEAD_DIM, gq), F32),
            jax.ShapeDtypeStruct((batch, g_, seq, SEL_LANES), BF16),
            jax.ShapeDtypeStruct((batch, g_, nqb // attn, 1, SEL_LANES), jnp.int32),
        ],
        compiler_params=pltpu.CompilerParams(
            dimension_semantics=("parallel", "parallel", "parallel"), vmem_limit_bytes=VMEM_LIMIT),
        name="nsa_cmp",
    )(slopes, seg_a, kvc, kvct, ovt, mats)


def _nsa_attn_kernel(slopes_ref, tiles_ref, q_ref, gate_ref, pen_ref, ocmp_ref, ks_ref, vs_ref,
                     kw_ref, vw_ref, onehot_ref, wext_ref, dbias_ref, ubias_ref, wbias_ref, o_ref,
                     ksel_ref, vselt_ref, kwin_ref, vwint_ref,
                     qa_ref, qw_ref, m_ref, acc_ref, owin_ref, gt_ref, sa_ref, sb_ref,
                     *, seq, sub_blocks):
    g = pl.program_id(1)
    step = pl.program_id(2)
    dh = NSA_HEAD_DIM
    nq = sub_blocks * Q_BLOCK
    kt = KEY_TILE
    per = SEL_TILES_PER_STEP
    pad_tile = seq // kt
    t0 = step * nq
    tile0 = step * sub_blocks

    @pl.when(step == 0)
    def _():
        ksel_ref[0:seq, :dh] = ks_ref[...]
        ksel_ref[0:seq, dh:] = onehot_ref[...]
        ksel_ref[seq:, :dh] = jnp.zeros((kt, dh), BF16)
        ksel_ref[seq:, dh:] = jnp.ones((kt, SEL_LANES), BF16)
        lane2 = lax.broadcasted_iota(jnp.int32, (WINDOW, dh + LANE), 1)
        kwin_ref[0:WINDOW, :] = jnp.where(lane2 == dh + WCOL_PAD, 1.0, 0.0).astype(BF16)
        kwin_ref[WINDOW:, :dh] = kw_ref[...]
        kwin_ref[WINDOW:, dh:] = wext_ref[...]
        zero_tile = jnp.zeros((dh + SUM_ROWS, kt), BF16)
        ones_rows = jnp.ones((SUM_ROWS, kt), BF16)
        vselt_ref[pad_tile] = zero_tile
        for i in range(WIN_TILES):
            vwint_ref[i] = zero_tile

        def transpose_tile(t, carry):
            r0 = pl.multiple_of(t * kt, kt)
            vselt_ref[t, :dh] = vs_ref[pl.ds(r0, kt), :].astype(F32).T.astype(BF16)
            vselt_ref[t, dh:] = ones_rows
            vwint_ref[t + WIN_TILES, :dh] = vw_ref[pl.ds(r0, kt), :].astype(F32).T.astype(BF16)
            vwint_ref[t + WIN_TILES, dh:] = ones_rows
            return carry

        lax.fori_loop(0, seq // kt, transpose_tile, 0)

    slopes = [slopes_ref[g * NSA_REP + r] for r in range(NSA_REP)]
    row_q = lax.broadcasted_iota(jnp.int32, (nq, 1), 0)
    lane = lax.broadcasted_iota(jnp.int32, (1, LANE), 1)
    q_all = q_ref[...]

    pen = pen_ref[0, 0].astype(F32)
    blk_rel = ((lane - (t0 + row_q) // SEL_BLOCK) * SEL_BLOCK).astype(F32)
    for r in range(NSA_REP):
        qa_ref[r * nq:(r + 1) * nq, :dh] = q_all[:, r * dh:(r + 1) * dh]
        qa_ref[r * nq:(r + 1) * nq, dh:] = (pen + slopes[r] * blk_rel).astype(BF16)

    def listed(idx):
        return tiles_ref[0, 0, 0, 0, idx]

    cnt = listed(SEL_LANES - 1)

    def group_scores(it):
        keys = jnp.concatenate(
            [ksel_ref[pl.ds(pl.multiple_of(listed(it * per + i) * kt, kt), kt), :]
             for i in range(per)], axis=0)
        return _dot_nt(keys, qa_ref[...]) + ubias_ref[0]

    s = _dot_nt(ksel_ref[pl.ds(pl.multiple_of(t0, kt), nq), :], qa_ref[...]) + dbias_ref[0]
    m0 = jnp.max(s, axis=0, keepdims=True)
    p = jnp.exp(s - m0)
    m_ref[...] = m0
    vals = jnp.concatenate([vselt_ref[tile0 + a] for a in range(sub_blocks)], axis=1)
    acc_ref[...] = _dot(vals, p.astype(BF16))
    sa_ref[...] = group_scores(0)

    tp = t0 + WINDOW + row_q
    t_hi = (tp // SEL_BLOCK).astype(F32)
    t_lo = (tp % SEL_BLOCK).astype(F32)
    for r in range(NSA_REP):
        sl = slopes[r]
        ext = jnp.where(lane == WCOL_PAD, -MASK_BIG, 0.0)
        ext = jnp.where(lane == WCOL_HI, sl * SEL_BLOCK, ext)
        ext = jnp.where(lane == WCOL_LO, sl, ext)
        ext = jnp.where(lane == WCOL_QHI, -sl * SEL_BLOCK * t_hi, ext)
        ext = jnp.where(lane == WCOL_QLO, -sl * t_lo, ext)
        qw_ref[r * nq:(r + 1) * nq, :dh] = q_all[:, r * dh:(r + 1) * dh]
        qw_ref[r * nq:(r + 1) * nq, dh:] = ext.astype(BF16)
    wlen = WINDOW + nq
    s = _dot_nt(kwin_ref[pl.ds(pl.multiple_of(t0, kt), wlen), :], qw_ref[...]) + wbias_ref[...]
    e = jnp.exp(s - jnp.max(s, axis=0, keepdims=True))
    vwin = jnp.concatenate([vwint_ref[tile0 + i] for i in range(WIN_TILES + sub_blocks)], axis=1)
    ow = _dot(vwin, e.astype(BF16))
    owin_ref[...] = ow[:dh] / ow[dh:dh + 1]

    def absorb(s, grp):
        vals = jnp.concatenate([vselt_ref[listed(grp * per + i)] for i in range(per)], axis=1)
        m_old = m_ref[...]
        m_new = jnp.maximum(m_old, jnp.max(s, axis=0, keepdims=True))
        alpha = jnp.exp(m_old - m_new)
        p = jnp.exp(s - m_new)
        acc_ref[...] = alpha * acc_ref[...] + _dot(vals, p.astype(BF16))
        m_ref[...] = m_new

    def sel_body(it, carry):
        s = sa_ref[...]
        sb_ref[...] = group_scores(2 * it + 1)
        absorb(s, 2 * it)
        s = sb_ref[...]
        sa_ref[...] = group_scores(2 * it + 2)
        absorb(s, 2 * it + 1)
        return carry

    lax.fori_loop(0, (cnt + 2 * per - 1) // (2 * per), sel_body, 0)
    o_sel = acc_ref[:dh, :] / acc_ref[dh:dh + 1, :]
    o_win = owin_ref[...]

    gt_ref[...] = _sigmoid(gate_ref[...]).T
    gate_t = gt_ref[pl.ds(pl.multiple_of(g * C_GATE_STRIDE, C_GATE_STRIDE), C_GATE_STRIDE), :]
    for a in range(sub_blocks):
        o_cmp = ocmp_ref[0, 0, a]
        qs = slice(a * Q_BLOCK, (a + 1) * Q_BLOCK)
        for r in range(NSA_REP):
            cols = slice(r * nq + a * Q_BLOCK, r * nq + (a + 1) * Q_BLOCK)
            c0 = N_BRANCH * r
            out_t = (gate_t[c0:c0 + 1, qs] * o_cmp[:, r * Q_BLOCK:(r + 1) * Q_BLOCK]
                     + gate_t[c0 + 1:c0 + 2, qs] * o_sel[:, cols]
                     + gate_t[c0 + 2:c0 + 3, qs] * o_win[:, cols])
            o_ref[qs, r * dh:(r + 1) * dh] = out_t.T.astype(o_ref.dtype)


def _nsa_sub_blocks(seq):
    return 2 if (seq // Q_BLOCK) % 2 == 0 else 1


def _nsa_attn(seg_a, seg_c, pen, ocmp, tiles, onehot, wext, dbias, ubias, wbias, slopes, batch, seq):
    n = batch * seq
    sub = _nsa_sub_blocks(seq)
    nq = sub * Q_BLOCK
    nqb = seq // nq
    gq = NSA_REP * NSA_HEAD_DIM
    g_ = NSA_KV_GROUPS
    dh, kt = NSA_HEAD_DIM, KEY_TILE
    dk = dh + SEL_LANES
    rq = NSA_REP * nq
    n_t = seq // kt

    def kv_spec(kind):
        return pl.BlockSpec((seq, dh), lambda b, g, q, k=kind: (b, A_KV // dh + k * g_ + g))

    def const_spec(arr):
        return pl.BlockSpec(arr.shape, lambda b, g, q, nd=arr.ndim: (0,) * nd)

    return pl.pallas_call(
        functools.partial(_nsa_attn_kernel, seq=seq, sub_blocks=sub),
        grid=(batch, g_, nqb),
        in_specs=[
            pl.BlockSpec(memory_space=pltpu.SMEM),
            pl.BlockSpec((1, 1, 1, 1, SEL_LANES), lambda b, g, q: (b, g, q, 0, 0),
                         memory_space=pltpu.SMEM),
            pl.BlockSpec((nq, gq), lambda b, g, q: (b * nqb + q, A_Q // gq + g)),
            pl.BlockSpec((nq, SEG_C), lambda b, g, q: (b * nqb + q, 0)),
            pl.BlockSpec((1, 1, nq, SEL_LANES), lambda b, g, q: (b, g, q, 0)),
            pl.BlockSpec((1, 1, sub, NSA_HEAD_DIM, gq), lambda b, g, q: (b, g, q, 0, 0)),
            kv_spec(2), kv_spec(3), kv_spec(4), kv_spec(5),
            const_spec(onehot), const_spec(wext),
            pl.BlockSpec((1,) + dbias.shape[1:], lambda b, g, q: (g, 0, 0)),
            pl.BlockSpec((1,) + ubias.shape[1:], lambda b, g, q: (g, 0, 0)),
            const_spec(wbias),
        ],
        out_specs=pl.BlockSpec((nq, gq), lambda b, g, q: (b * nqb + q, g)),
        out_shape=jax.ShapeDtypeStruct((n, NSA_WIDTH), BF16),
        scratch_shapes=[
            pltpu.VMEM((seq + kt, dk), BF16),
            pltpu.VMEM((n_t + 1, dh + SUM_ROWS, kt), BF16),
            pltpu.VMEM((seq + WINDOW, dh + LANE), BF16),
            pltpu.VMEM((n_t + WIN_TILES, dh + SUM_ROWS, kt), BF16),
            pltpu.VMEM((rq, dk), BF16),
            pltpu.VMEM((rq, dk), BF16),
            pltpu.VMEM((1, rq), F32),
            pltpu.VMEM((dh + SUM_ROWS, rq), F32),
            pltpu.VMEM((dh, rq), F32),
            pltpu.VMEM((SEG_C, nq), F32),
            pltpu.VMEM((SEL_TILES_PER_STEP * KEY_TILE, rq), F32),
            pltpu.VMEM((SEL_TILES_PER_STEP * KEY_TILE, rq), F32),
        ],
        compiler_params=pltpu.CompilerParams(
            dimension_semantics=("parallel", "parallel", "arbitrary"), vmem_limit_bytes=VMEM_LIMIT),
        name="nsa_attn",
    )(slopes, tiles, seg_a, seg_c, pen, ocmp, seg_a, seg_a, seg_a, seg_a, onehot, wext,
      dbias, ubias, wbias)


def _nsa_tables(slopes, seq):
    g_, kt = NSA_KV_GROUPS, KEY_TILE
    pos = np.arange(seq)
    onehot = jnp.asarray(pos[:, None] // SEL_BLOCK == np.arange(SEL_LANES)[None, :], BF16)
    ext = np.zeros((seq, LANE), np.float32)
    ext[:, WCOL_HI] = (pos + WINDOW) // SEL_BLOCK
    ext[:, WCOL_LO] = (pos + WINDOW) % SEL_BLOCK
    ext[:, WCOL_QHI] = 1.0
    ext[:, WCOL_QLO] = 1.0
    wext = jnp.asarray(ext, BF16)

    nq = _nsa_sub_blocks(seq) * Q_BLOCK

    def alibi_in_block(rows):
        u = jnp.asarray((np.arange(rows) % SEL_BLOCK).astype(np.float32))[None, :, None, None]
        t = jnp.broadcast_to(slopes.reshape(g_, 1, NSA_REP, 1) * u, (g_, rows, NSA_REP, nq))
        return t.reshape(g_, rows, NSA_REP * nq)

    ubias = alibi_in_block(SEL_TILES_PER_STEP * kt)
    kq = np.arange(nq)[:, None] <= np.arange(nq)[None, :]
    causal = np.tile(np.where(kq, 0.0, NEG).astype(np.float32), (1, NSA_REP))
    dbias = alibi_in_block(nq) + jnp.asarray(causal)[None]
    ki = np.arange(WINDOW + nq)[:, None]
    qi = np.arange(nq)[None, :]
    band = np.where((ki > qi) & (ki <= qi + WINDOW), 0.0, NEG).astype(np.float32)
    wbias = jnp.asarray(np.tile(band, (1, NSA_REP)))
    return onehot, wext, dbias, ubias, wbias


def _log_sigmoid(x):
    return jnp.minimum(x, 0.0) - jnp.log(1.0 + jnp.exp(-jnp.abs(x)))


def _split3(x):
    hi = x.astype(BF16)
    r1 = x - hi.astype(F32)
    mid = r1.astype(BF16)
    lo = (r1 - mid.astype(F32)).astype(BF16)
    return hi, mid, lo


def _mlstm_kernel(bias_ref, q_ref, k_ref, v_ref, o_ref, ifc_ref, ng_ref,
                  tri_ref, y_ref, c_ref, n_ref, m_ref):
    ch = pl.program_id(1)

    @pl.when(ch == 0)
    def _():
        c_ref[...] = jnp.zeros(c_ref.shape, F32)
        n_ref[...] = jnp.zeros(n_ref.shape, F32)
        m_ref[...] = jnp.zeros(m_ref.shape, F32)

    for sq in range(q_ref.shape[0]):
        _mlstm_chunk(bias_ref, q_ref.at[sq], k_ref.at[sq], v_ref.at[sq], o_ref.at[sq], ifc_ref.at[sq],
                     ng_ref, tri_ref, y_ref.at[sq], c_ref.at[sq], n_ref.at[sq], m_ref.at[sq])


def _mlstm_chunk(bias_ref, q_ref, k_ref, v_ref, o_ref, ifc_ref, ng_ref, tri_ref, y_ref,
                 c_ref, n_ref, m_ref):
    nh, dh = MLSTM_HEADS, MLSTM_HEAD_DIM
    L = q_ref.shape[0]
    tri = tri_ref[...]
    lane8 = lax.broadcasted_iota(jnp.int32, (1, LANE), 1)
    bias_c = jnp.zeros((1, LANE), F32)
    for h in range(nh):
        bias_c = jnp.where(lane8 == C_IF + h, bias_ref[h], bias_c)
        bias_c = jnp.where(lane8 == C_IF + nh + h, bias_ref[nh + h], bias_c)
    pre_c = ifc_ref[...] + bias_c
    cum_c = sum(_dot(tri, part) for part in _split3(_log_sigmoid(pre_c)))
    pre_r = pre_c.T[C_IF:C_IF + 2 * nh]
    cum_r = sum(_dot_nt(part, tri) for part in _split3(_log_sigmoid(pre_r)))

    rr = lax.broadcasted_iota(jnp.int32, (L, 1), 0)
    cc = lax.broadcasted_iota(jnp.int32, (1, L), 1)
    causal = cc <= rr

    for h in range(nh):
        cols = slice(h * dh, (h + 1) * dh)
        qb = q_ref[:, cols]
        kb = k_ref[:, cols]
        vh = v_ref[:, cols]
        qh = qb.astype(F32)
        kh = kb.astype(F32)
        b_c = cum_c[:, C_IF + nh + h:C_IF + nh + h + 1]
        li_c = pre_c[:, C_IF + h:C_IF + h + 1]
        b_r = cum_r[nh + h:nh + h + 1, :]
        li_r = pre_r[h:h + 1, :]
        m_prev = m_ref[h:h + 1, 0:1]

        dmat = jnp.where(causal, b_c - b_r + li_r, NEG)
        a = b_c + m_prev
        m_j = jnp.maximum(a, jnp.max(dmat, axis=1, keepdims=True))
        w_intra = jnp.exp(dmat - m_j)
        w_inter = jnp.exp(a - m_j)
        sc = _dot_nt(qb, kb) * w_intra
        c_old = c_ref[h]
        n_old = n_ref[h:h + 1, :]
        num = w_inter * _dot(qb, c_old.astype(BF16)) + _dot(sc.astype(BF16), vh)
        den = (w_inter * jnp.sum(qh * n_old, axis=1, keepdims=True)
               + jnp.sum(sc, axis=1, keepdims=True))
        hid = num / jnp.maximum(jnp.abs(den), jnp.exp(-m_j))

        g_tot = b_r[:, L - 1:L]
        lw_c = g_tot - b_c + li_c
        lw_r = g_tot - b_r + li_r
        m_new = jnp.maximum(g_tot + m_prev, jnp.max(lw_r, axis=1, keepdims=True))
        decay = jnp.exp(g_tot + m_prev - m_new)
        kw = jnp.exp(lw_c - m_new) * kh
        c_ref[h] = decay * c_old + _dot(kw.T.astype(BF16), vh)
        n_ref[h:h + 1, :] = decay * n_old + jnp.sum(kw, axis=0, keepdims=True)
        m_ref[h:h + 1, :] = jnp.broadcast_to(m_new, (1, LANE))

        hn = hid * lax.rsqrt(jnp.mean(hid * hid, axis=-1, keepdims=True) + EPS) * ng_ref[:, cols]
        y_ref[:, cols] = (_sigmoid(o_ref[:, cols]) * hn).astype(y_ref.dtype)


def _mlstm(seg_a, seg_b, seg_c, bias, norm_g, tri, batch, seq, chunk):
    nc = seq // chunk
    w = MLSTM_WIDTH
    nh, dh = MLSTM_HEADS, MLSTM_HEAD_DIM
    seqs = 1
    a3 = seg_a.reshape(batch, seq, SEG_A)
    b3 = seg_b.reshape(batch, seq, SEG_B)
    c3 = seg_c.reshape(batch, seq, SEG_C)

    def col_spec(off):
        return pl.BlockSpec((seqs, chunk, w), lambda b, c, o=off // w: (b, c, o))

    y = pl.pallas_call(
        _mlstm_kernel,
        grid=(batch // seqs, nc),
        in_specs=[
            pl.BlockSpec(memory_space=pltpu.SMEM),
            col_spec(A_QK), col_spec(A_QK + w), col_spec(A_V), col_spec(B_O),
            pl.BlockSpec((seqs, chunk, SEG_C), lambda b, c: (b, c, 0)),
            pl.BlockSpec((1, w), lambda b, c: (0, 0)),
            pl.BlockSpec((chunk, chunk), lambda b, c: (0, 0)),
        ],
        out_specs=pl.BlockSpec((seqs, chunk, w), lambda b, c: (b, c, 0)),
        out_shape=jax.ShapeDtypeStruct((batch, seq, w), BF16),
        scratch_shapes=[
            pltpu.VMEM((seqs, nh, dh, dh), F32),
            pltpu.VMEM((seqs, 8, dh), F32),
            pltpu.VMEM((seqs, 8, LANE), F32),
        ],
        compiler_params=pltpu.CompilerParams(
            dimension_semantics=("parallel", "arbitrary"), vmem_limit_bytes=VMEM_LIMIT),
        name="mlstm",
    )(bias, a3, a3, a3, b3, c3, norm_g, tri)
    return y.reshape(batch * seq, w)


def _outproj_kernel(x_ref, ya_ref, ym_ref, wa_ref, wm_ref, o_ref):
    o_ref[...] = x_ref[...] + _dot(ya_ref[...], wa_ref[...]) + _dot(ym_ref[...], wm_ref[...])


def _outproj(x2, ya, ym, w, tm):
    n, d = x2.shape
    assert ya.shape[1] == ym.shape[1] == d // 2
    return pl.pallas_call(
        _outproj_kernel,
        grid=(n // tm,),
        in_specs=[
            pl.BlockSpec((tm, d), lambda i: (i, 0)),
            pl.BlockSpec((tm, d // 2), lambda i: (i, 0)),
            pl.BlockSpec((tm, d // 2), lambda i: (i, 0)),
            pl.BlockSpec((d // 2, d), lambda i: (0, 0)),
            pl.BlockSpec((d // 2, d), lambda i: (1, 0)),
        ],
        out_specs=pl.BlockSpec((tm, d), lambda i: (i, 0)),
        out_shape=jax.ShapeDtypeStruct((n, d), F32),
        compiler_params=pltpu.CompilerParams(
            dimension_semantics=("parallel",), vmem_limit_bytes=VMEM_LIMIT),
        name="outproj",
    )(x2, ya, ym, w, w)


def _mlp_kernel(x_ref, g_ref, w1_ref, w2_ref, gf_ref, o_ref, h_ref):
    f = pl.program_id(1)

    @pl.when(f == 0)
    def _():
        x = x_ref[...]
        r = lax.rsqrt(jnp.mean(x * x, axis=-1, keepdims=True) + EPS)
        h_ref[...] = (x * r * g_ref[...]).astype(BF16)
        o_ref[...] = x

    u = jnp.maximum(_dot(h_ref[...], w1_ref[...]), 0.0)
    o_ref[...] += _dot((u * u).astype(BF16), w2_ref[...])

    @pl.when(f == pl.num_programs(1) - 1)
    def _():
        x2 = o_ref[...]
        r = lax.rsqrt(jnp.mean(x2 * x2, axis=-1, keepdims=True) + EPS)
        o_ref[...] = x2 * r * gf_ref[...]


def _mlp(x1, g, w1, w2, gf, tm, tf):
    n, d = x1.shape
    dff = w1.shape[1]
    return pl.pallas_call(
        _mlp_kernel,
        grid=(n // tm, dff // tf),
        in_specs=[
            pl.BlockSpec((tm, d), lambda i, f: (i, 0)),
            pl.BlockSpec((1, d), lambda i, f: (0, 0)),
            pl.BlockSpec((d, tf), lambda i, f: (0, f)),
            pl.BlockSpec((tf, d), lambda i, f: (f, 0)),
            pl.BlockSpec((1, d), lambda i, f: (0, 0)),
        ],
        out_specs=pl.BlockSpec((tm, d), lambda i, f: (i, 0)),
        out_shape=jax.ShapeDtypeStruct((n, d), F32),
        scratch_shapes=[pltpu.VMEM((tm, d), BF16)],
        compiler_params=pltpu.CompilerParams(
            dimension_semantics=("parallel", "arbitrary"), vmem_limit_bytes=VMEM_LIMIT),
        name="mlp",
    )(x1, g, w1, w2, gf)


def _row_tile(n, want):
    t = want
    while n % t:
        t //= 2
    return t


def _layer(x2, batch, seq, norm_mix_g, w_in, w_cmp_k1, w_cmp_k2, pos_cmp_k, w_cmp_v1, w_cmp_v2,
           pos_cmp_v, conv_w, conv_b, b_igate, b_fgate, mlstm_norm_g, w_out, norm_mlp_g,
           w_mlp_in, w_mlp_out):
    n, d = x2.shape
    assert seq % Q_BLOCK == 0 and seq >= WINDOW + Q_BLOCK and seq // SEL_BLOCK <= SEL_LANES
    nh = MLSTM_HEADS

    c_gate = NSA_WIDTH + 6 * NSA_KV_WIDTH
    c_qk = c_gate + NSA_HEADS * N_BRANCH
    c_v = c_qk + 2 * MLSTM_WIDTH
    c_o = c_v + MLSTM_WIDTH
    c_i = c_o + MLSTM_WIDTH
    c_f = c_i + nh
    def cols16(lo, hi):
        return w_in[:, lo:hi].astype(BF16)

    w_ab = jnp.concatenate(
        [cols16(c_v, c_o), cols16(c_qk, c_v), cols16(0, c_gate),
         cols16(c_o, c_i)], axis=1)
    per_g = NSA_REP * N_BRANCH
    gate_cols = []
    for g in range(NSA_KV_GROUPS):
        gate_cols += [cols16(c_gate + g * per_g, c_gate + (g + 1) * per_g),
                      jnp.zeros((d, C_GATE_STRIDE - per_g), BF16)]
    w_c = jnp.concatenate(gate_cols + [cols16(c_i, c_f + nh),
                                       jnp.zeros((d, SEG_C - C_IF - 2 * nh), BF16)], axis=1)
    scale = jnp.concatenate([jnp.ones((1, A_QK + MLSTM_WIDTH), F32),
                             jnp.full((1, MLSTM_WIDTH), MLSTM_HEAD_DIM ** -0.5, F32),
                             jnp.full((1, NSA_WIDTH), NSA_HEAD_DIM ** -0.5, F32),
                             jnp.ones((1, SEG_A - A_KV + SEG_B), F32)], axis=1)
    conv_pad = ((0, 0), (A_QK, SEG_A - A_Q + SEG_B))
    cw_ab = jnp.pad(conv_w, conv_pad)
    cb_ab = jnp.pad(conv_b.reshape(1, -1), conv_pad)
    g_mix = norm_mix_g.reshape(1, d)

    tm = _row_tile(seq, ROW_TILE)
    seg_a, seg_b, seg_c = _inproj(x2, g_mix, w_ab, w_c, scale, cw_ab, cb_ab, tm, COL_TILE,
                                  (A_QK, A_Q), seq)

    n_sub = seq // CMP_STRIDE
    dh = NSA_HEAD_DIM
    w1s = jnp.stack([w_cmp_k1, w_cmp_v1]).reshape(2, 2, CMP_STRIDE, dh, dh)
    w1r = jnp.concatenate([w1s[:, 0], w1s[:, 1]], axis=-1).astype(BF16)
    w2s = jnp.stack([w_cmp_k2, w_cmp_v2]).astype(BF16)
    poss = jnp.stack([pos_cmp_k, pos_cmp_v]).reshape(2, 2, CMP_STRIDE, dh).transpose(0, 2, 1, 3)
    posr = jnp.pad(poss, ((0, 0), (0, 0), (0, 6), (0, 0))).astype(BF16)
    kvc, kvct = _compress(seg_a, w1r, w2s, posr, batch, seq)

    cmp_start = np.arange(n_sub) * CMP_STRIDE
    sel_start = np.arange(SEL_LANES) * SEL_BLOCK
    ovt = ((cmp_start[None, :] < sel_start[:, None] + SEL_BLOCK)
           & (cmp_start[None, :] + CMP_BLOCK - 1 >= sel_start[:, None])
           & (np.arange(n_sub)[None, :] < n_sub - CMP_BLOCK // CMP_STRIDE + 1))
    ovt = jnp.asarray(ovt, BF16)
    slopes = jnp.exp2(-8.0 * jnp.arange(1, NSA_HEADS + 1, dtype=F32) / NSA_HEADS)
    ocmp, pen, tiles = _nsa_cmp(seg_a, kvc, kvct, ovt, slopes, batch, seq)
    onehot, wext, dbias, ubias, wbias = _nsa_tables(slopes, seq)
    y_a = _nsa_attn(seg_a, seg_c, pen, ocmp, tiles, onehot, wext, dbias, ubias, wbias,
                    slopes, batch, seq)

    chunk = 256 if seq % 256 == 0 else 128
    bias = jnp.concatenate([b_igate, b_fgate]).astype(F32)
    tri = jnp.asarray(np.tril(np.ones((chunk, chunk), np.float32)), BF16)
    y_m = _mlstm(seg_a, seg_b, seg_c, bias, mlstm_norm_g.reshape(1, -1), tri, batch, seq, chunk)

    x1 = _outproj(x2, y_a, y_m, w_out.astype(BF16), _row_tile(n, 512))
    return x1, (norm_mlp_g.reshape(1, d), w_mlp_in.astype(BF16), w_mlp_out.astype(BF16))


def kernel(x, norm_mix_g, w_in, w_cmp_k1, w_cmp_k2, pos_cmp_k, w_cmp_v1, w_cmp_v2, pos_cmp_v, conv_w, conv_b, b_igate, b_fgate, mlstm_norm_g, w_out, norm_mlp_g, w_mlp_in, w_mlp_out, norm_f_g):
    batch, seq, d = x.shape
    depth = w_in.shape[0]
    assert depth == 1, "the final RMSNorm is fused into the last layer's channel mixer"
    x2 = x.reshape(batch * seq, d)
    tm = _row_tile(batch * seq, ROW_TILE)
    for l in range(depth):
        x1, (g_mlp, w1, w2) = _layer(
            x2, batch, seq, norm_mix_g[l], w_in[l], w_cmp_k1[l], w_cmp_k2[l], pos_cmp_k[l],
            w_cmp_v1[l], w_cmp_v2[l], pos_cmp_v[l], conv_w[l], conv_b[l], b_igate[l], b_fgate[l],
            mlstm_norm_g[l], w_out[l], norm_mlp_g[l], w_mlp_in[l], w_mlp_out[l])
        x2 = _mlp(x1, g_mlp, w1, w2, norm_f_g.reshape(1, d), tm, 512)
    return x2.reshape(batch, seq, d)
```

```python
import functools

import numpy as np
import jax
import jax.numpy as jnp
from jax import lax
from jax.experimental import pallas as pl
from jax.experimental.pallas import tpu as pltpu

F32 = jnp.float32
BF16 = jnp.bfloat16

EPS = 1e-6
NEG = -1e30
FORCE_BONUS = 1e4
PICKED = -3e38
MASK_BIG = 1e30

D_MODEL = 2048
NSA_HEAD_DIM = 128
NSA_WIDTH = D_MODEL // 2
NSA_HEADS = NSA_WIDTH // NSA_HEAD_DIM
NSA_REP = 4
NSA_KV_GROUPS = NSA_HEADS // NSA_REP
NSA_KV_WIDTH = NSA_KV_GROUPS * NSA_HEAD_DIM
CMP_BLOCK = 32
CMP_STRIDE = 16
SEL_BLOCK = 64
SEL_TOPK = 16
WINDOW = 512
Q_BLOCK = 128
N_BRANCH = 3
MLSTM_HEAD_DIM = 256
MLSTM_WIDTH = D_MODEL - NSA_WIDTH
MLSTM_HEADS = MLSTM_WIDTH // MLSTM_HEAD_DIM
CONV_WIDTH = 4

LANE = 128
SEL_LANES = 128
KEY_TILE = 128
SEL_TILES_PER_STEP = 2
WIN_TILES = WINDOW // KEY_TILE
SUM_ROWS = 16
VMEM_LIMIT = 56 * 1024 * 1024
ROW_TILE = 1024

A_V, A_QK = 0, MLSTM_WIDTH
A_Q = A_QK + 2 * MLSTM_WIDTH
A_KV = A_Q + NSA_WIDTH
SEG_A = A_KV + 6 * NSA_KV_WIDTH
B_O = 0
SEG_B = MLSTM_WIDTH
C_GATE_STRIDE = 16
C_IF = NSA_KV_GROUPS * C_GATE_STRIDE
SEG_C = LANE
COL_TILE = 512

WCOL_PAD, WCOL_HI, WCOL_LO, WCOL_QHI, WCOL_QLO = 0, 1, 2, 3, 4


def _dot(a, b):
    return jnp.dot(a, b, preferred_element_type=F32)


def _dot_nt(a, b):
    return lax.dot_general(a, b, (((1,), (1,)), ((), ())), preferred_element_type=F32)


def _sigmoid(x):
    return 1.0 / (1.0 + jnp.exp(-x))


def _shifted(x, tail, s):
    xs = pltpu.roll(x, s, axis=0)
    ts = pltpu.roll(tail, s, axis=0)
    row8 = lax.broadcasted_iota(jnp.int32, (8, 1), 0)
    head = jnp.where(row8 < s, ts, xs[:8])
    return jnp.concatenate([head, xs[8:]], axis=0)


def _conv_silu(x, tail, w, b):
    y = b + _shifted(x, tail, CONV_WIDTH - 1) * w[0:1]
    for i in range(1, CONV_WIDTH - 1):
        y = y + _shifted(x, tail, CONV_WIDTH - 1 - i) * w[i:i + 1]
    y = y + x * w[CONV_WIDTH - 1:CONV_WIDTH]
    return y * _sigmoid(y)


def _inproj_kernel(x_ref, g_ref, w_ref, wc_ref, cs_ref, cw_ref, cb_ref, oa_ref, ob_ref, oc_ref,
                   h_ref, halo_ref, *, nb, n_plain, tiles_per_seq):
    i = pl.program_id(0)
    j = pl.program_id(1)

    @pl.when(j == 0)
    def _():
        x = x_ref[...]
        r = lax.rsqrt(jnp.mean(x * x, axis=-1, keepdims=True) + EPS)
        h_ref[...] = (x * r * g_ref[...]).astype(BF16)

    is_conv = j > nb + n_plain

    @pl.when((j > nb) & jnp.logical_not(is_conv))
    def _():
        oa_ref[...] = (_dot(h_ref[...], w_ref[...]) * cs_ref[...]).astype(oa_ref.dtype)

    @pl.when(j < nb)
    def _():
        ob_ref[...] = (_dot(h_ref[...], w_ref[...]) * cs_ref[...]).astype(ob_ref.dtype)

    @pl.when(j == nb)
    def _():
        oc_ref[...] = _dot(h_ref[...], wc_ref[...])

    @pl.when(is_conv)
    def _():
        slot = j - (nb + 1 + n_plain)

        @pl.when(i % tiles_per_seq == 0)
        def _():
            halo_ref[slot] = jnp.zeros(halo_ref.shape[1:], F32)

        acc = _dot(h_ref[...], w_ref[...])
        tail = halo_ref[slot]
        halo_ref[slot] = acc[acc.shape[0] - 8:]
        y = _conv_silu(acc, tail, cw_ref[...], cb_ref[...])
        oa_ref[...] = (y * cs_ref[...]).astype(oa_ref.dtype)


def _inproj(x2, g, w, w_c, cs, cw, cb, tm, tn, conv_cols, seq):
    n, d = x2.shape
    na, nb = SEG_A // tn, SEG_B // tn
    conv_lo, conv_hi = conv_cols[0] // tn, conv_cols[1] // tn
    assert conv_cols[0] % tn == 0 and conv_cols[1] % tn == 0 and seq % tm == 0 and conv_hi <= na
    assert SEG_A % tn == 0 and SEG_B % tn == 0 and w.shape[1] == SEG_A + SEG_B
    n_conv = conv_hi - conv_lo
    n_plain = na - n_conv

    def a_tile(j):
        p = jnp.maximum(j - nb - 1, 0)
        plain = jnp.where(p < conv_lo, p, p + n_conv)
        return jnp.where(p < n_plain, plain, conv_lo + p - n_plain)

    def w_tile(i, j):
        return (0, jnp.where(j <= nb, na + jnp.minimum(j, nb - 1), a_tile(j)))

    return pl.pallas_call(
        functools.partial(_inproj_kernel, nb=nb, n_plain=n_plain, tiles_per_seq=seq // tm),
        grid=(n // tm, na + nb + 1),
        in_specs=[
            pl.BlockSpec((tm, d), lambda i, j: (i, 0)),
            pl.BlockSpec((1, d), lambda i, j: (0, 0)),
            pl.BlockSpec((d, tn), w_tile),
            pl.BlockSpec((d, SEG_C), lambda i, j: (0, 0)),
            pl.BlockSpec((1, tn), w_tile),
            pl.BlockSpec((CONV_WIDTH, tn), w_tile),
            pl.BlockSpec((1, tn), w_tile),
        ],
        out_specs=[pl.BlockSpec((tm, tn), lambda i, j: (i, a_tile(j))),
                   pl.BlockSpec((tm, tn), lambda i, j: (i, jnp.minimum(j, nb - 1))),
                   pl.BlockSpec((tm, SEG_C), lambda i, j: (i, 0))],
        out_shape=[jax.ShapeDtypeStruct((n, SEG_A), BF16), jax.ShapeDtypeStruct((n, SEG_B), F32),
                   jax.ShapeDtypeStruct((n, SEG_C), F32)],
        scratch_shapes=[pltpu.VMEM((tm, d), BF16),
                        pltpu.VMEM((conv_hi - conv_lo, 8, tn), F32)],
        compiler_params=pltpu.CompilerParams(
            dimension_semantics=("arbitrary", "arbitrary"), vmem_limit_bytes=VMEM_LIMIT),
        name="inproj",
    )(x2, g, w, w_c, cs, cw, cb)


def _compress_kernel(x_ref, w1_ref, w2_ref, pos_ref, o_ref, ot_ref, xf_ref):
    n_sub = o_ref.shape[2]
    dh = NSA_HEAD_DIM
    xf_ref[...] = x_ref[...].astype(F32)
    acc = jnp.zeros((n_sub, 2 * dh), F32)
    posw = jnp.zeros((1, dh), F32)
    for p in range(CMP_STRIDE):
        wp = w1_ref[0, p]
        acc = acc + _dot(xf_ref[pl.ds(p, n_sub, stride=CMP_STRIDE), :].astype(BF16), wp)
        pw = _dot(pos_ref[0, p], wp)
        posw = posw + pw[0:1, :dh] + pw[1:2, dh:]
    bot = pltpu.roll(acc[:, dh:], n_sub - 1, axis=0)
    pre = acc[:, :dh] + bot + posw
    hid = pre * _sigmoid(pre)
    out = _dot(hid.astype(BF16), w2_ref[0])
    o_ref[0, 0] = out.astype(o_ref.dtype)
    ot_ref[0, 0] = out.T.astype(ot_ref.dtype)


def _compress(seg_a, w1r, w2s, posr, batch, seq):
    g = NSA_KV_GROUPS
    c = 2 * g
    dh = NSA_HEAD_DIM
    n_sub = seq // CMP_STRIDE
    return pl.pallas_call(
        _compress_kernel,
        grid=(batch, c),
        in_specs=[
            pl.BlockSpec((seq, dh), lambda i, j: (i, A_KV // dh + j)),
            pl.BlockSpec((1, CMP_STRIDE, dh, 2 * dh), lambda i, j: (j // g, 0, 0, 0)),
            pl.BlockSpec((1, dh, dh), lambda i, j: (j // g, 0, 0)),
            pl.BlockSpec((1, CMP_STRIDE, 8, dh), lambda i, j: (j // g, 0, 0, 0)),
        ],
        out_specs=[pl.BlockSpec((1, 1, n_sub, dh), lambda i, j: (i, j, 0, 0)),
                   pl.BlockSpec((1, 1, dh, n_sub), lambda i, j: (i, j, 0, 0))],
        out_shape=[jax.ShapeDtypeStruct((batch, c, n_sub, dh), BF16),
                   jax.ShapeDtypeStruct((batch, c, dh, n_sub), BF16)],
        scratch_shapes=[pltpu.VMEM((seq, dh), F32)],
        compiler_params=pltpu.CompilerParams(
            dimension_semantics=("parallel", "parallel"), vmem_limit_bytes=VMEM_LIMIT),
        name="compress",
    )(seg_a, w1r, w2s, posr)


def _stack_heads(q_all):
    dh = NSA_HEAD_DIM
    return jnp.concatenate([q_all[:, r * dh:(r + 1) * dh] for r in range(NSA_REP)], axis=0)


def _tile_list(hit_col, tile0, pad_tile, mats_ref):
    n = SEL_LANES
    lane = lax.broadcasted_iota(jnp.int32, (1, n), 1)
    blk = lax.broadcasted_iota(jnp.int32, (n, n), 0)
    slot = lax.broadcasted_iota(jnp.int32, (n, n), 1).astype(F32)
    hit = jnp.broadcast_to(hit_col, (n, n))
    both = jnp.maximum(hit, pltpu.roll(hit, n - 1, axis=0))
    act = jnp.where((both > 0.0) & (blk % 2 == 0) & (blk // 2 < tile0), 1.0, 0.0)
    rank = _dot(mats_ref[0], act.astype(BF16))
    cnt = jnp.sum(act, axis=0, keepdims=True)
    place = jnp.where((act > 0.0) & (rank == slot), 1.0, 0.0)
    tiles = _dot(mats_ref[1, 0:8], place.astype(BF16))[0:1]
    out = jnp.where(lane.astype(F32) < cnt, tiles, float(pad_tile))
    return jnp.where(lane == n - 1, cnt, out).astype(jnp.int32)


def _nsa_cmp_kernel(slopes_ref, q_ref, kc_ref, vct_ref, ovt_ref, mats_ref, ocmp_ref, pen_ref, tiles_ref, *,
                    seq, sub_blocks, attn_blocks):
    step = pl.program_id(2)
    nq_step = sub_blocks * Q_BLOCK
    chunk = min(SEL_LANES, kc_ref.shape[2])
    n_chunks = kc_ref.shape[2] // chunk
    need = ((step + 1) * nq_step - CMP_BLOCK) // CMP_STRIDE + 1
    n_need = (need + chunk - 1) // chunk
    for v in range(1, n_chunks + 1):
        cond = (n_need == v) if v < n_chunks else (n_need >= v)
        if v == 1:
            cond = n_need <= 1

        @pl.when(cond)
        def _(v=v):
            hits = []
            for sub in range(sub_blocks):
                rows = slice(sub * Q_BLOCK, (sub + 1) * Q_BLOCK)
                hits.append(_nsa_cmp_block(slopes_ref, q_ref[rows, :], kc_ref, vct_ref, ovt_ref,
                                           ocmp_ref.at[0, 0, sub], pen_ref.at[0, 0, rows],
                                           step * sub_blocks + sub, seq, v * chunk,
                                           past_first_block=(sub >= 1 or v >= 2)))
            for a in range(sub_blocks // attn_blocks):
                hit = functools.reduce(jnp.maximum, hits[a * attn_blocks:(a + 1) * attn_blocks])
                tile0 = step * sub_blocks + a * attn_blocks
                tiles_ref[0, 0, a] = _tile_list(hit, tile0, seq // KEY_TILE, mats_ref)


def _nsa_cmp_block(slopes_ref, q_all, kc_ref, vct_ref, ovt_ref, ocmp_ref, pen_ref, qb, seq, n_use,
                   past_first_block):
    g = pl.program_id(1)
    nq = Q_BLOCK
    n_cpad = n_use
    n_cmp = seq // CMP_STRIDE - CMP_BLOCK // CMP_STRIDE + 1
    t0 = qb * nq

    s_t = _dot_nt(kc_ref[0, 0, :n_use, :], _stack_heads(q_all))
    n_s = lax.broadcasted_iota(jnp.int32, (n_cpad, 1), 0)
    q_l = lax.broadcasted_iota(jnp.int32, (1, nq), 1)
    dist = (t0 - (CMP_BLOCK - 1)) + q_l - n_s * CMP_STRIDE
    valid = (dist >= 0) & (n_s < n_cmp)
    dist_f = dist.astype(F32)
    probs = []
    p_sum = jnp.zeros((n_cpad, nq), F32)
    for r in range(NSA_REP):
        slope = slopes_ref[g * NSA_REP + r]
        s = jnp.where(valid, s_t[:, r * nq:(r + 1) * nq] - slope * dist_f, NEG)
        m = jnp.max(s, axis=0, keepdims=True)
        e = jnp.exp(s - m)
        inv = jnp.where(m > 0.5 * NEG, 1.0 / jnp.sum(e, axis=0, keepdims=True), 0.0)
        p = e * inv
        probs.append(p.astype(BF16))
        p_sum = p_sum + p
    ocmp_ref[...] = _dot(vct_ref[0, 0, :, :n_use], jnp.concatenate(probs, axis=1))

    p_hi = p_sum.astype(BF16)
    p_lo = (p_sum - p_hi.astype(F32)).astype(BF16)
    ovt = ovt_ref[:, :n_use]
    imp = _dot(ovt, p_hi) + _dot(ovt, p_lo)
    j_i = lax.broadcasted_iota(jnp.int32, (SEL_LANES, 1), 0)
    t_l = t0 + q_l
    cur = t_l // SEL_BLOCK
    forced = (j_i == 0) | (j_i == cur) | (j_i == cur - 1)
    causal_blk = j_i * SEL_BLOCK <= t_l
    val = jnp.where(causal_blk, jnp.where(forced, imp + FORCE_BONUS, imp), NEG)
    j_f = j_i.astype(F32)
    sel_t = jnp.zeros((SEL_LANES, nq), F32)
    rounds = min(SEL_TOPK, seq // SEL_BLOCK)
    if past_first_block and rounds >= 3:
        sel_t = jnp.where(forced, 1.0, sel_t)
        val = jnp.where(forced, PICKED, val)
        rounds -= 3
    for _ in range(rounds):
        mx = jnp.max(val, axis=0, keepdims=True)
        first = jnp.min(jnp.where(val == mx, j_f, float(SEL_LANES)), axis=0, keepdims=True)
        pick = j_f == first
        sel_t = jnp.where(pick, 1.0, sel_t)
        val = jnp.where(pick, PICKED, val)
    sel = sel_t.T
    pen_ref[...] = ((sel - 1.0) * MASK_BIG).astype(pen_ref.dtype)
    return jnp.max(sel_t, axis=1, keepdims=True)


def _nsa_cmp(seg_a, kvc, kvct, ovt, slopes, batch, seq):
    nqb = seq // Q_BLOCK
    gq = NSA_REP * NSA_HEAD_DIM
    g_ = NSA_KV_GROUPS
    n_cpad = kvc.shape[2]
    attn = _nsa_sub_blocks(seq)
    sub = next(m * attn for m in (8, 4, 2, 1) if nqb % (m * attn) == 0)
    nstep = nqb // sub
    idx = np.arange(SEL_LANES)
    mats = jnp.asarray(np.stack([idx[:, None] > idx[None, :],
                                 np.broadcast_to(idx[None, :] // 2, (SEL_LANES, SEL_LANES))]), BF16)
    return pl.pallas_call(
        functools.partial(_nsa_cmp_kernel, seq=seq, sub_blocks=sub, attn_blocks=attn),
        grid=(batch, g_, nstep),
        in_specs=[
            pl.BlockSpec(memory_space=pltpu.SMEM),
            pl.BlockSpec((sub * Q_BLOCK, gq), lambda b, g, q: (b * nstep + q, A_Q // gq + g)),
            pl.BlockSpec((1, 1, n_cpad, NSA_HEAD_DIM), lambda b, g, q: (b, g, 0, 0)),
            pl.BlockSpec((1, 1, NSA_HEAD_DIM, n_cpad), lambda b, g, q: (b, g_ + g, 0, 0)),
            pl.BlockSpec((SEL_LANES, n_cpad), lambda b, g, q: (0, 0)),
            pl.BlockSpec((2, SEL_LANES, SEL_LANES), lambda b, g, q: (0, 0, 0)),
        ],
        out_specs=[
            pl.BlockSpec((1, 1, sub, NSA_HEAD_DIM, gq), lambda b, g, q: (b, g, q, 0, 0)),
            pl.BlockSpec((1, 1, sub * Q_BLOCK, SEL_LANES), lambda b, g, q: (b, g, q, 0)),
            pl.BlockSpec((1, 1, sub // attn, 1, SEL_LANES), lambda b, g, q: (b, g, q, 0, 0)),
        ],
        out_shape=[
            jax.ShapeDtypeStruct((batch, g_, nqb, NSA_HEAD_DIM, gq), F32),
            jax.ShapeDtypeStruct((batch, g_, seq, SEL_LANES), BF16),
            jax.ShapeDtypeStruct((batch, g_, nqb // attn, 1, SEL_LANES), jnp.int32),
        ],
        compiler_params=pltpu.CompilerParams(
            dimension_semantics=("parallel", "parallel", "parallel"), vmem_limit_bytes=VMEM_LIMIT),
        name="nsa_cmp",
    )(slopes, seg_a, kvc, kvct, ovt, mats)


def _nsa_attn_kernel(slopes_ref, tiles_ref, q_ref, gate_ref, pen_ref, ocmp_ref, ks_ref, vs_ref,
                     kw_ref, vw_ref, onehot_ref, wext_ref, dbias_ref, ubias_ref, wbias_ref, o_ref,
                     ksel_ref, vselt_ref, kwin_ref, vwint_ref,
                     qa_ref, qw_ref, m_ref, acc_ref, owin_ref, gt_ref, sa_ref, sb_ref,
                     *, seq, sub_blocks):
    g = pl.program_id(1)
    step = pl.program_id(2)
    dh = NSA_HEAD_DIM
    nq = sub_blocks * Q_BLOCK
    kt = KEY_TILE
    per = SEL_TILES_PER_STEP
    pad_tile = seq // kt
    t0 = step * nq
    tile0 = step * sub_blocks

    @pl.when(step == 0)
    def _():
        ksel_ref[0:seq, :dh] = ks_ref[...]
        ksel_ref[0:seq, dh:] = onehot_ref[...]
        ksel_ref[seq:, :dh] = jnp.zeros((kt, dh), BF16)
        ksel_ref[seq:, dh:] = jnp.ones((kt, SEL_LANES), BF16)
        lane2 = lax.broadcasted_iota(jnp.int32, (WINDOW, dh + LANE), 1)
        kwin_ref[0:WINDOW, :] = jnp.where(lane2 == dh + WCOL_PAD, 1.0, 0.0).astype(BF16)
        kwin_ref[WINDOW:, :dh] = kw_ref[...]
        kwin_ref[WINDOW:, dh:] = wext_ref[...]
        zero_tile = jnp.zeros((dh + SUM_ROWS, kt), BF16)
        ones_rows = jnp.ones((SUM_ROWS, kt), BF16)
        vselt_ref[pad_tile] = zero_tile
        for i in range(WIN_TILES):
            vwint_ref[i] = zero_tile

        def transpose_tile(t, carry):
            r0 = pl.multiple_of(t * kt, kt)
            vselt_ref[t, :dh] = vs_ref[pl.ds(r0, kt), :].astype(F32).T.astype(BF16)
            vselt_ref[t, dh:] = ones_rows
            vwint_ref[t + WIN_TILES, :dh] = vw_ref[pl.ds(r0, kt), :].astype(F32).T.astype(BF16)
            vwint_ref[t + WIN_TILES, dh:] = ones_rows
            return carry

        lax.fori_loop(0, seq // kt, transpose_tile, 0)

    slopes = [slopes_ref[g * NSA_REP + r] for r in range(NSA_REP)]
    row_q = lax.broadcasted_iota(jnp.int32, (nq, 1), 0)
    lane = lax.broadcasted_iota(jnp.int32, (1, LANE), 1)
    q_all = q_ref[...]

    pen = pen_ref[0, 0].astype(F32)
    blk_rel = ((lane - (t0 + row_q) // SEL_BLOCK) * SEL_BLOCK).astype(F32)
    for r in range(NSA_REP):
        qa_ref[r * nq:(r + 1) * nq, :dh] = q_all[:, r * dh:(r + 1) * dh]
        qa_ref[r * nq:(r + 1) * nq, dh:] = (pen + slopes[r] * blk_rel).astype(BF16)

    def listed(idx):
        return tiles_ref[0, 0, 0, 0, idx]

    cnt = listed(SEL_LANES - 1)

    def group_scores(it):
        keys = jnp.concatenate(
            [ksel_ref[pl.ds(pl.multiple_of(listed(it * per + i) * kt, kt), kt), :]
             for i in range(per)], axis=0)
        return _dot_nt(keys, qa_ref[...]) + ubias_ref[0]

    s = _dot_nt(ksel_ref[pl.ds(pl.multiple_of(t0, kt), nq), :], qa_ref[...]) + dbias_ref[0]
    m0 = jnp.max(s, axis=0, keepdims=True)
    p = jnp.exp(s - m0)
    m_ref[...] = m0
    vals = jnp.concatenate([vselt_ref[tile0 + a] for a in range(sub_blocks)], axis=1)
    acc_ref[...] = _dot(vals, p.astype(BF16))
    sa_ref[...] = group_scores(0)

    tp = t0 + WINDOW + row_q
    t_hi = (tp // SEL_BLOCK).astype(F32)
    t_lo = (tp % SEL_BLOCK).astype(F32)
    for r in range(NSA_REP):
        sl = slopes[r]
        ext = jnp.where(lane == WCOL_PAD, -MASK_BIG, 0.0)
        ext = jnp.where(lane == WCOL_HI, sl * SEL_BLOCK, ext)
        ext = jnp.where(lane == WCOL_LO, sl, ext)
        ext = jnp.where(lane == WCOL_QHI, -sl * SEL_BLOCK * t_hi, ext)
        ext = jnp.where(lane == WCOL_QLO, -sl * t_lo, ext)
        qw_ref[r * nq:(r + 1) * nq, :dh] = q_all[:, r * dh:(r + 1) * dh]
        qw_ref[r * nq:(r + 1) * nq, dh:] = ext.astype(BF16)
    wlen = WINDOW + nq
    s = _dot_nt(kwin_ref[pl.ds(pl.multiple_of(t0, kt), wlen), :], qw_ref[...]) + wbias_ref[...]
    e = jnp.exp(s - jnp.max(s, axis=0, keepdims=True))
    vwin = jnp.concatenate([vwint_ref[tile0 + i] for i in range(WIN_TILES + sub_blocks)], axis=1)
    ow = _dot(vwin, e.astype(BF16))
    owin_ref[...] = ow[:dh] / ow[dh:dh + 1]

    def absorb(s, grp):
        vals = jnp.concatenate([vselt_ref[listed(grp * per + i)] for i in range(per)], axis=1)
        m_old = m_ref[...]
        m_new = jnp.maximum(m_old, jnp.max(s, axis=0, keepdims=True))
        alpha = jnp.exp(m_old - m_new)
        p = jnp.exp(s - m_new)
        acc_ref[...] = alpha * acc_ref[...] + _dot(vals, p.astype(BF16))
        m_ref[...] = m_new

    def sel_body(it, carry):
        s = sa_ref[...]
        sb_ref[...] = group_scores(2 * it + 1)
        absorb(s, 2 * it)
        s = sb_ref[...]
        sa_ref[...] = group_scores(2 * it + 2)
        absorb(s, 2 * it + 1)
        return carry

    lax.fori_loop(0, (cnt + 2 * per - 1) // (2 * per), sel_body, 0)
    o_sel = acc_ref[:dh, :] / acc_ref[dh:dh + 1, :]
    o_win = owin_ref[...]

    gt_ref[...] = _sigmoid(gate_ref[...]).T
    gate_t = gt_ref[pl.ds(pl.multiple_of(g * C_GATE_STRIDE, C_GATE_STRIDE), C_GATE_STRIDE), :]
    for a in range(sub_blocks):
        o_cmp = ocmp_ref[0, 0, a]
        qs = slice(a * Q_BLOCK, (a + 1) * Q_BLOCK)
        for r in range(NSA_REP):
            cols = slice(r * nq + a * Q_BLOCK, r * nq + (a + 1) * Q_BLOCK)
            c0 = N_BRANCH * r
            out_t = (gate_t[c0:c0 + 1, qs] * o_cmp[:, r * Q_BLOCK:(r + 1) * Q_BLOCK]
                     + gate_t[c0 + 1:c0 + 2, qs] * o_sel[:, cols]
                     + gate_t[c0 + 2:c0 + 3, qs] * o_win[:, cols])
            o_ref[qs, r * dh:(r + 1) * dh] = out_t.T.astype(o_ref.dtype)


def _nsa_sub_blocks(seq):
    return 2 if (seq // Q_BLOCK) % 2 == 0 else 1


def _nsa_attn(seg_a, seg_c, pen, ocmp, tiles, onehot, wext, dbias, ubias, wbias, slopes, batch, seq):
    n = batch * seq
    sub = _nsa_sub_blocks(seq)
    nq = sub * Q_BLOCK
    nqb = seq // nq
    gq = NSA_REP * NSA_HEAD_DIM
    g_ = NSA_KV_GROUPS
    dh, kt = NSA_HEAD_DIM, KEY_TILE
    dk = dh + SEL_LANES
    rq = NSA_REP * nq
    n_t = seq // kt

    def kv_spec(kind):
        return pl.BlockSpec((seq, dh), lambda b, g, q, k=kind: (b, A_KV // dh + k * g_ + g))

    def const_spec(arr):
        return pl.BlockSpec(arr.shape, lambda b, g, q, nd=arr.ndim: (0,) * nd)

    return pl.pallas_call(
        functools.partial(_nsa_attn_kernel, seq=seq, sub_blocks=sub),
        grid=(batch, g_, nqb),
        in_specs=[
            pl.BlockSpec(memory_space=pltpu.SMEM),
            pl.BlockSpec((1, 1, 1, 1, SEL_LANES), lambda b, g, q: (b, g, q, 0, 0),
                         memory_space=pltpu.SMEM),
            pl.BlockSpec((nq, gq), lambda b, g, q: (b * nqb + q, A_Q // gq + g)),
            pl.BlockSpec((nq, SEG_C), lambda b, g, q: (b * nqb + q, 0)),
            pl.BlockSpec((1, 1, nq, SEL_LANES), lambda b, g, q: (b, g, q, 0)),
            pl.BlockSpec((1, 1, sub, NSA_HEAD_DIM, gq), lambda b, g, q: (b, g, q, 0, 0)),
            kv_spec(2), kv_spec(3), kv_spec(4), kv_spec(5),
            const_spec(onehot), const_spec(wext),
            pl.BlockSpec((1,) + dbias.shape[1:], lambda b, g, q: (g, 0, 0)),
            pl.BlockSpec((1,) + ubias.shape[1:], lambda b, g, q: (g, 0, 0)),
            const_spec(wbias),
        ],
        out_specs=pl.BlockSpec((nq, gq), lambda b, g, q: (b * nqb + q, g)),
        out_shape=jax.ShapeDtypeStruct((n, NSA_WIDTH), BF16),
        scratch_shapes=[
            pltpu.VMEM((seq + kt, dk), BF16),
            pltpu.VMEM((n_t + 1, dh + SUM_ROWS, kt), BF16),
            pltpu.VMEM((seq + WINDOW, dh + LANE), BF16),
            pltpu.VMEM((n_t + WIN_TILES, dh + SUM_ROWS, kt), BF16),
            pltpu.VMEM((rq, dk), BF16),
            pltpu.VMEM((rq, dk), BF16),
            pltpu.VMEM((1, rq), F32),
            pltpu.VMEM((dh + SUM_ROWS, rq), F32),
            pltpu.VMEM((dh, rq), F32),
            pltpu.VMEM((SEG_C, nq), F32),
            pltpu.VMEM((SEL_TILES_PER_STEP * KEY_TILE, rq), F32),
            pltpu.VMEM((SEL_TILES_PER_STEP * KEY_TILE, rq), F32),
        ],
        compiler_params=pltpu.CompilerParams(
            dimension_semantics=("parallel", "parallel", "arbitrary"), vmem_limit_bytes=VMEM_LIMIT),
        name="nsa_attn",
    )(slopes, tiles, seg_a, seg_c, pen, ocmp, seg_a, seg_a, seg_a, seg_a, onehot, wext,
      dbias, ubias, wbias)


def _nsa_tables(slopes, seq):
    g_, kt = NSA_KV_GROUPS, KEY_TILE
    pos = np.arange(seq)
    onehot = jnp.asarray(pos[:, None] // SEL_BLOCK == np.arange(SEL_LANES)[None, :], BF16)
    ext = np.zeros((seq, LANE), np.float32)
    ext[:, WCOL_HI] = (pos + WINDOW) // SEL_BLOCK
    ext[:, WCOL_LO] = (pos + WINDOW) % SEL_BLOCK
    ext[:, WCOL_QHI] = 1.0
    ext[:, WCOL_QLO] = 1.0
    wext = jnp.asarray(ext, BF16)

    nq = _nsa_sub_blocks(seq) * Q_BLOCK

    def alibi_in_block(rows):
        u = jnp.asarray((np.arange(rows) % SEL_BLOCK).astype(np.float32))[None, :, None, None]
        t = jnp.broadcast_to(slopes.reshape(g_, 1, NSA_REP, 1) * u, (g_, rows, NSA_REP, nq))
        return t.reshape(g_, rows, NSA_REP * nq)

    ubias = alibi_in_block(SEL_TILES_PER_STEP * kt)
    kq = np.arange(nq)[:, None] <= np.arange(nq)[None, :]
    causal = np.tile(np.where(kq, 0.0, NEG).astype(np.float32), (1, NSA_REP))
    dbias = alibi_in_block(nq) + jnp.asarray(causal)[None]
    ki = np.arange(WINDOW + nq)[:, None]
    qi = np.arange(nq)[None, :]
    band = np.where((ki > qi) & (ki <= qi + WINDOW), 0.0, NEG).astype(np.float32)
    wbias = jnp.asarray(np.tile(band, (1, NSA_REP)))
    return onehot, wext, dbias, ubias, wbias


def _log_sigmoid(x):
    return jnp.minimum(x, 0.0) - jnp.log(1.0 + jnp.exp(-jnp.abs(x)))


def _split3(x):
    hi = x.astype(BF16)
    r1 = x - hi.astype(F32)
    mid = r1.astype(BF16)
    lo = (r1 - mid.astype(F32)).astype(BF16)
    return hi, mid, lo


def _mlstm_kernel(bias_ref, q_ref, k_ref, v_ref, o_ref, ifc_ref, ng_ref,
                  tri_ref, y_ref, c_ref, n_ref, m_ref):
    ch = pl.program_id(1)

    @pl.when(ch == 0)
    def _():
        c_ref[...] = jnp.zeros(c_ref.shape, F32)
        n_ref[...] = jnp.zeros(n_ref.shape, F32)
        m_ref[...] = jnp.zeros(m_ref.shape, F32)

    for sq in range(q_ref.shape[0]):
        _mlstm_chunk(bias_ref, q_ref.at[sq], k_ref.at[sq], v_ref.at[sq], o_ref.at[sq], ifc_ref.at[sq],
                     ng_ref, tri_ref, y_ref.at[sq], c_ref.at[sq], n_ref.at[sq], m_ref.at[sq])


def _mlstm_chunk(bias_ref, q_ref, k_ref, v_ref, o_ref, ifc_ref, ng_ref, tri_ref, y_ref,
                 c_ref, n_ref, m_ref):
    nh, dh = MLSTM_HEADS, MLSTM_HEAD_DIM
    L = q_ref.shape[0]
    tri = tri_ref[...]
    lane8 = lax.broadcasted_iota(jnp.int32, (1, LANE), 1)
    bias_c = jnp.zeros((1, LANE), F32)
    for h in range(nh):
        bias_c = jnp.where(lane8 == C_IF + h, bias_ref[h], bias_c)
        bias_c = jnp.where(lane8 == C_IF + nh + h, bias_ref[nh + h], bias_c)
    pre_c = ifc_ref[...] + bias_c
    cum_c = sum(_dot(tri, part) for part in _split3(_log_sigmoid(pre_c)))
    pre_r = pre_c.T[C_IF:C_IF + 2 * nh]
    cum_r = sum(_dot_nt(part, tri) for part in _split3(_log_sigmoid(pre_r)))

    rr = lax.broadcasted_iota(jnp.int32, (L, 1), 0)
    cc = lax.broadcasted_iota(jnp.int32, (1, L), 1)
    causal = cc <= rr

    for h in range(nh):
        cols = slice(h * dh, (h + 1) * dh)
        qb = q_ref[:, cols]
        kb = k_ref[:, cols]
        vh = v_ref[:, cols]
        qh = qb.astype(F32)
        kh = kb.astype(F32)
        b_c = cum_c[:, C_IF + nh + h:C_IF + nh + h + 1]
        li_c = pre_c[:, C_IF + h:C_IF + h + 1]
        b_r = cum_r[nh + h:nh + h + 1, :]
        li_r = pre_r[h:h + 1, :]
        m_prev = m_ref[h:h + 1, 0:1]

        dmat = jnp.where(causal, b_c - b_r + li_r, NEG)
        a = b_c + m_prev
        m_j = jnp.maximum(a, jnp.max(dmat, axis=1, keepdims=True))
        w_intra = jnp.exp(dmat - m_j)
        w_inter = jnp.exp(a - m_j)
        sc = _dot_nt(qb, kb) * w_intra
        c_old = c_ref[h]
        n_old = n_ref[h:h + 1, :]
        num = w_inter * _dot(qb, c_old.astype(BF16)) + _dot(sc.astype(BF16), vh)
        den = (w_inter * jnp.sum(qh * n_old, axis=1, keepdims=True)
               + jnp.sum(sc, axis=1, keepdims=True))
        hid = num / jnp.maximum(jnp.abs(den), jnp.exp(-m_j))

        g_tot = b_r[:, L - 1:L]
        lw_c = g_tot - b_c + li_c
        lw_r = g_tot - b_r + li_r
        m_new = jnp.maximum(g_tot + m_prev, jnp.max(lw_r, axis=1, keepdims=True))
        decay = jnp.exp(g_tot + m_prev - m_new)
        kw = jnp.exp(lw_c - m_new) * kh
        c_ref[h] = decay * c_old + _dot(kw.T.astype(BF16), vh)
        n_ref[h:h + 1, :] = decay * n_old + jnp.sum(kw, axis=0, keepdims=True)
        m_ref[h:h + 1, :] = jnp.broadcast_to(m_new, (1, LANE))

        hn = hid * lax.rsqrt(jnp.mean(hid * hid, axis=-1, keepdims=True) + EPS) * ng_ref[:, cols]
        y_ref[:, cols] = (_sigmoid(o_ref[:, cols]) * hn).astype(y_ref.dtype)


def _mlstm(seg_a, seg_b, seg_c, bias, norm_g, tri, batch, seq, chunk):
    nc = seq // chunk
    w = MLSTM_WIDTH
    nh, dh = MLSTM_HEADS, MLSTM_HEAD_DIM
    seqs = 1
    a3 = seg_a.reshape(batch, seq, SEG_A)
    b3 = seg_b.reshape(batch, seq, SEG_B)
    c3 = seg_c.reshape(batch, seq, SEG_C)

    def col_spec(off):
        return pl.BlockSpec((seqs, chunk, w), lambda b, c, o=off // w: (b, c, o))

    y = pl.pallas_call(
        _mlstm_kernel,
        grid=(batch // seqs, nc),
        in_specs=[
            pl.BlockSpec(memory_space=pltpu.SMEM),
            col_spec(A_QK), col_spec(A_QK + w), col_spec(A_V), col_spec(B_O),
            pl.BlockSpec((seqs, chunk, SEG_C), lambda b, c: (b, c, 0)),
            pl.BlockSpec((1, w), lambda b, c: (0, 0)),
            pl.BlockSpec((chunk, chunk), lambda b, c: (0, 0)),
        ],
        out_specs=pl.BlockSpec((seqs, chunk, w), lambda b, c: (b, c, 0)),
        out_shape=jax.ShapeDtypeStruct((batch, seq, w), BF16),
        scratch_shapes=[
            pltpu.VMEM((seqs, nh, dh, dh), F32),
            pltpu.VMEM((seqs, 8, dh), F32),
            pltpu.VMEM((seqs, 8, LANE), F32),
        ],
        compiler_params=pltpu.CompilerParams(
            dimension_semantics=("parallel", "arbitrary"), vmem_limit_bytes=VMEM_LIMIT),
        name="mlstm",
    )(bias, a3, a3, a3, b3, c3, norm_g, tri)
    return y.reshape(batch * seq, w)


def _outproj_kernel(x_ref, ya_ref, ym_ref, wa_ref, wm_ref, o_ref):
    o_ref[...] = x_ref[...] + _dot(ya_ref[...], wa_ref[...]) + _dot(ym_ref[...], wm_ref[...])


def _outproj(x2, ya, ym, w, tm):
    n, d = x2.shape
    assert ya.shape[1] == ym.shape[1] == d // 2
    return pl.pallas_call(
        _outproj_kernel,
        grid=(n // tm,),
        in_specs=[
            pl.BlockSpec((tm, d), lambda i: (i, 0)),
            pl.BlockSpec((tm, d // 2), lambda i: (i, 0)),
            pl.BlockSpec((tm, d // 2), lambda i: (i, 0)),
            pl.BlockSpec((d // 2, d), lambda i: (0, 0)),
            pl.BlockSpec((d // 2, d), lambda i: (1, 0)),
        ],
        out_specs=pl.BlockSpec((tm, d), lambda i: (i, 0)),
        out_shape=jax.ShapeDtypeStruct((n, d), F32),
        compiler_params=pltpu.CompilerParams(
            dimension_semantics=("parallel",), vmem_limit_bytes=VMEM_LIMIT),
        name="outproj",
    )(x2, ya, ym, w, w)


def _mlp_kernel(x_ref, g_ref, w1_ref, w2_ref, gf_ref, o_ref, h_ref):
    f = pl.program_id(1)

    @pl.when(f == 0)
    def _():
        x = x_ref[...]
        r = lax.rsqrt(jnp.mean(x * x, axis=-1, keepdims=True) + EPS)
        h_ref[...] = (x * r * g_ref[...]).astype(BF16)
        o_ref[...] = x

    u = jnp.maximum(_dot(h_ref[...], w1_ref[...]), 0.0)
    o_ref[...] += _dot((u * u).astype(BF16), w2_ref[...])

    @pl.when(f == pl.num_programs(1) - 1)
    def _():
        x2 = o_ref[...]
        r = lax.rsqrt(jnp.mean(x2 * x2, axis=-1, keepdims=True) + EPS)
        o_ref[...] = x2 * r * gf_ref[...]


def _mlp(x1, g, w1, w2, gf, tm, tf):
    n, d = x1.shape
    dff = w1.shape[1]
    return pl.pallas_call(
        _mlp_kernel,
        grid=(n // tm, dff // tf),
        in_specs=[
            pl.BlockSpec((tm, d), lambda i, f: (i, 0)),
            pl.BlockSpec((1, d), lambda i, f: (0, 0)),
            pl.BlockSpec((d, tf), lambda i, f: (0, f)),
            pl.BlockSpec((tf, d), lambda i, f: (f, 0)),
            pl.BlockSpec((1, d), lambda i, f: (0, 0)),
        ],
        out_specs=pl.BlockSpec((tm, d), lambda i, f: (i, 0)),
        out_shape=jax.ShapeDtypeStruct((n, d), F32),
        scratch_shapes=[pltpu.VMEM((tm, d), BF16)],
        compiler_params=pltpu.CompilerParams(
            dimension_semantics=("parallel", "arbitrary"), vmem_limit_bytes=VMEM_LIMIT),
        name="mlp",
    )(x1, g, w1, w2, gf)


def _row_tile(n, want):
    t = want
    while n % t:
        t //= 2
    return t


def _layer(x2, batch, seq, norm_mix_g, w_in, w_cmp_k1, w_cmp_k2, pos_cmp_k, w_cmp_v1, w_cmp_v2,
           pos_cmp_v, conv_w, conv_b, b_igate, b_fgate, mlstm_norm_g, w_out, norm_mlp_g,
           w_mlp_in, w_mlp_out):
    n, d = x2.shape
    assert seq % Q_BLOCK == 0 and seq >= WINDOW + Q_BLOCK and seq // SEL_BLOCK <= SEL_LANES
    nh = MLSTM_HEADS

    c_gate = NSA_WIDTH + 6 * NSA_KV_WIDTH
    c_qk = c_gate + NSA_HEADS * N_BRANCH
    c_v = c_qk + 2 * MLSTM_WIDTH
    c_o = c_v + MLSTM_WIDTH
    c_i = c_o + MLSTM_WIDTH
    c_f = c_i + nh
    def cols16(lo, hi):
        return w_in[:, lo:hi].astype(BF16)

    w_ab = jnp.concatenate(
        [cols16(c_v, c_o), cols16(c_qk, c_v), cols16(0, c_gate),
         cols16(c_o, c_i)], axis=1)
    per_g = NSA_REP * N_BRANCH
    gate_cols = []
    for g in range(NSA_KV_GROUPS):
        gate_cols += [cols16(c_gate + g * per_g, c_gate + (g + 1) * per_g),
                      jnp.zeros((d, C_GATE_STRIDE - per_g), BF16)]
    w_c = jnp.concatenate(gate_cols + [cols16(c_i, c_f + nh),
                                       jnp.zeros((d, SEG_C - C_IF - 2 * nh), BF16)], axis=1)
    scale = jnp.concatenate([jnp.ones((1, A_QK + MLSTM_WIDTH), F32),
                             jnp.full((1, MLSTM_WIDTH), MLSTM_HEAD_DIM ** -0.5, F32),
                             jnp.full((1, NSA_WIDTH), NSA_HEAD_DIM ** -0.5, F32),
                             jnp.ones((1, SEG_A - A_KV + SEG_B), F32)], axis=1)
    conv_pad = ((0, 0), (A_QK, SEG_A - A_Q + SEG_B))
    cw_ab = jnp.pad(conv_w, conv_pad)
    cb_ab = jnp.pad(conv_b.reshape(1, -1), conv_pad)
    g_mix = norm_mix_g.reshape(1, d)

    tm = _row_tile(seq, ROW_TILE)
    seg_a, seg_b, seg_c = _inproj(x2, g_mix, w_ab, w_c, scale, cw_ab, cb_ab, tm, COL_TILE,
                                  (A_QK, A_Q), seq)

    n_sub = seq // CMP_STRIDE
    dh = NSA_HEAD_DIM
    w1s = jnp.stack([w_cmp_k1, w_cmp_v1]).reshape(2, 2, CMP_STRIDE, dh, dh)
    w1r = jnp.concatenate([w1s[:, 0], w1s[:, 1]], axis=-1).astype(BF16)
    w2s = jnp.stack([w_cmp_k2, w_cmp_v2]).astype(BF16)
    poss = jnp.stack([pos_cmp_k, pos_cmp_v]).reshape(2, 2, CMP_STRIDE, dh).transpose(0, 2, 1, 3)
    posr = jnp.pad(poss, ((0, 0), (0, 0), (0, 6), (0, 0))).astype(BF16)
    kvc, kvct = _compress(seg_a, w1r, w2s, posr, batch, seq)

    cmp_start = np.arange(n_sub) * CMP_STRIDE
    sel_start = np.arange(SEL_LANES) * SEL_BLOCK
    ovt = ((cmp_start[None, :] < sel_start[:, None] + SEL_BLOCK)
           & (cmp_start[None, :] + CMP_BLOCK - 1 >= sel_start[:, None])
           & (np.arange(n_sub)[None, :] < n_sub - CMP_BLOCK // CMP_STRIDE + 1))
    ovt = jnp.asarray(ovt, BF16)
    slopes = jnp.exp2(-8.0 * jnp.arange(1, NSA_HEADS + 1, dtype=F32) / NSA_HEADS)
    ocmp, pen, tiles = _nsa_cmp(seg_a, kvc, kvct, ovt, slopes, batch, seq)
    onehot, wext, dbias, ubias, wbias = _nsa_tables(slopes, seq)
    y_a = _nsa_attn(seg_a, seg_c, pen, ocmp, tiles, onehot, wext, dbias, ubias, wbias,
                    slopes, batch, seq)

    chunk = 256 if seq % 256 == 0 else 128
    bias = jnp.concatenate([b_igate, b_fgate]).astype(F32)
    tri = jnp.asarray(np.tril(np.ones((chunk, chunk), np.float32)), BF16)
    y_m = _mlstm(seg_a, seg_b, seg_c, bias, mlstm_norm_g.reshape(1, -1), tri, batch, seq, chunk)

    x1 = _outproj(x2, y_a, y_m, w_out.astype(BF16), _row_tile(n, 512))
    return x1, (norm_mlp_g.reshape(1, d), w_mlp_in.astype(BF16), w_mlp_out.astype(BF16))


def kernel(x, norm_mix_g, w_in, w_cmp_k1, w_cmp_k2, pos_cmp_k, w_cmp_v1, w_cmp_v2, pos_cmp_v, conv_w, conv_b, b_igate, b_fgate, mlstm_norm_g, w_out, norm_mlp_g, w_mlp_in, w_mlp_out, norm_f_g):
    batch, seq, d = x.shape
    depth = w_in.shape[0]
    assert depth == 1, "the final RMSNorm is fused into the last layer's channel mixer"
    x2 = x.reshape(batch * seq, d)
    tm = _row_tile(batch * seq, ROW_TILE)
    for l in range(depth):
        x1, (g_mlp, w1, w2) = _layer(
            x2, batch, seq, norm_mix_g[l], w_in[l], w_cmp_k1[l], w_cmp_k2[l], pos_cmp_k[l],
            w_cmp_v1[l], w_cmp_v2[l], pos_cmp_v[l], conv_w[l], conv_b[l], b_igate[l], b_fgate[l],
            mlstm_norm_g[l], w_out[l], norm_mlp_g[l], w_mlp_in[l], w_mlp_out[l])
        x2 = _mlp(x1, g_mlp, w1, w2, norm_f_g.reshape(1, d), tm, 512)
    return x2.reshape(batch, seq, d)
```
